```python
import math
import jax, jax.numpy as jnp
from jax import lax
import numpy as np

D_MODEL = 2048
BATCH = 8
SEQ = 4096
DEPTH = 2

HEAD_DIM = 128
ATTN_GROUPS = ((128, 1), (512, 4), (2048, 16))
N_ATTN_GROUPS = len(ATTN_GROUPS)
HEADS_PER_GROUP = 4
ATTN_QKV_WIDTH = 3 * N_ATTN_GROUPS * HEADS_PER_GROUP * HEAD_DIM
ATTN_OUT_WIDTH = HEADS_PER_GROUP * HEAD_DIM
SG_CHUNK = 128
SG_GROUPS = 8
SG_GROUP_DIM = 128
SG_WIDTH = SG_GROUPS * SG_GROUP_DIM
N_BRANCHES = 2
GATE_WIDTH = N_BRANCHES * D_MODEL
IN_WIDTH = ATTN_QKV_WIDTH + 2 * SG_WIDTH + GATE_WIDTH
D_FF = -(-8 * D_MODEL // (3 * 256)) * 256
PLE_DIM = 256
ROPE_THETA = 10000.0
NORM_EPS = 1e-6
NEG_INF = -1e30

kernel_name = "hybrid_dilated_attn_gmlp_gated_encoder"


def rms_norm(x, g):
    xf = x.astype(jnp.float32)
    y = xf * lax.rsqrt(jnp.mean(xf * xf, axis=-1, keepdims=True) + NORM_EPS)
    return (y * g.astype(jnp.float32)).astype(x.dtype)


def layer_norm(x, g, b):
    xf = x.astype(jnp.float32)
    mu = jnp.mean(xf, axis=-1, keepdims=True)
    xc = xf - mu
    y = xc * lax.rsqrt(jnp.mean(xc * xc, axis=-1, keepdims=True) + NORM_EPS)
    return (y * g.astype(jnp.float32) + b.astype(jnp.float32)).astype(x.dtype)


def rope_tables(seq):
    pos = jnp.arange(seq, dtype=jnp.float32)
    inv_freq = ROPE_THETA ** (-jnp.arange(0, HEAD_DIM, 2, dtype=jnp.float32) / HEAD_DIM)
    ang = pos[:, None] * inv_freq[None, :]
    return jnp.cos(ang), jnp.sin(ang)


def apply_rope(t, cos, sin):
    half = HEAD_DIM // 2
    tf = t.astype(jnp.float32)
    t1, t2 = tf[..., :half], tf[..., half:]
    c = cos[None, :, None, None, :]
    s = sin[None, :, None, None, :]
    return jnp.concatenate([t1 * c - t2 * s, t2 * c + t1 * s], axis=-1).astype(t.dtype)


def dilated_window_attention(q, k, v, window, dilation):
    B, S, H, hd = q.shape
    d = dilation
    radius = window // (2 * d)
    blk = radius
    L = S // d
    nb = -(-L // blk)
    Lp = nb * blk
    N = B * d

    def strided(t):
        return t.reshape(B, L, d, H, hd).transpose(0, 2, 1, 3, 4).reshape(N, L, H, hd)

    qs, ks, vs = strided(q), strided(k), strided(v)
    qb = jnp.pad(qs, ((0, 0), (0, Lp - L), (0, 0), (0, 0))).reshape(N, nb, blk, H, hd)

    def band(t):
        tp = jnp.pad(t, ((0, 0), (blk, Lp - L + blk), (0, 0), (0, 0))).reshape(N, nb + 2, blk, H, hd)
        return jnp.concatenate([tp[:, :-2], tp[:, 1:-1], tp[:, 2:]], axis=2)

    kb, vb = band(ks), band(vs)
    qi = jnp.arange(nb)[:, None] * blk + jnp.arange(blk)[None, :]
    ki = jnp.arange(nb)[:, None] * blk - blk + jnp.arange(3 * blk)[None, :]
    dist = ki[:, None, :] - qi[:, :, None]
    valid = (jnp.abs(dist) <= radius) & (ki[:, None, :] >= 0) & (ki[:, None, :] < L)

    s = jnp.einsum("nbqhd,nbkhd->nbhqk", qb, kb, preferred_element_type=jnp.float32)
    s = s * (hd ** -0.5)
    s = jnp.where(valid[None, :, None, :, :], s, NEG_INF)
    m = jnp.max(s, axis=-1, keepdims=True)
    e = jnp.exp(s - m)
    den = jnp.sum(e, axis=-1, keepdims=True)
    o = jnp.einsum("nbhqk,nbkhd->nbqhd", e / den, vb.astype(jnp.float32))
    lse = (m + jnp.log(den))[..., 0].transpose(0, 1, 3, 2)

    o = o.reshape(N, Lp, H, hd)[:, :L].reshape(B, d, L, H, hd).transpose(0, 2, 1, 3, 4).reshape(B, S, H, hd)
    lse = lse.reshape(N, Lp, H)[:, :L].reshape(B, d, L, H).transpose(0, 2, 1, 3).reshape(B, S, H)
    return o, lse


def spatial_gating(u, v, sg_w, sg_b, ln_g, ln_b):
    B, S, _ = v.shape
    u = jax.nn.gelu(u)
    v = layer_norm(jax.nn.gelu(v), ln_g, ln_b)
    vc = v.reshape(B, S // SG_CHUNK, SG_CHUNK, SG_GROUPS, SG_GROUP_DIM)
    mixed = jnp.einsum("gij,bcjgd->bcigd", sg_w, vc) + sg_b.T[None, None, :, :, None]
    return u * mixed.reshape(B, S, SG_WIDTH)


def _fwd_setup_inputs(seed: int = 0) -> dict:
    key = jax.random.key(seed)
    ks = jax.random.split(key, 20)
    f32 = jnp.float32

    def nrm(k, shape, fan_in):
        return jax.random.normal(k, shape, f32) * (fan_in ** -0.5)

    def gain(k, shape):
        return 1.0 + 0.02 * jax.random.normal(k, shape, f32)

    return {
        "x": jax.random.normal(ks[0], (BATCH, SEQ, D_MODEL), f32),
        "p": jax.random.normal(ks[1], (DEPTH, BATCH, SEQ, PLE_DIM), f32),
        "w_in": nrm(ks[2], (DEPTH, D_MODEL, IN_WIDTH), D_MODEL),
        "w_br_attn": nrm(ks[3], (DEPTH, ATTN_OUT_WIDTH, D_MODEL), ATTN_OUT_WIDTH),
        "w_br_sg": nrm(ks[4], (DEPTH, SG_WIDTH, D_MODEL), SG_WIDTH),
        "w_out": nrm(ks[5], (DEPTH, D_MODEL, D_MODEL), D_MODEL),
        "sg_w": nrm(ks[6], (DEPTH, SG_GROUPS, SG_CHUNK, SG_CHUNK), SG_CHUNK),
        "sg_b": 0.02 * jax.random.normal(ks[7], (DEPTH, SG_GROUPS, SG_CHUNK), f32),
        "sg_ln_g": gain(ks[8], (DEPTH, SG_WIDTH)),
        "sg_ln_b": 0.02 * jax.random.normal(ks[9], (DEPTH, SG_WIDTH), f32),
        "norm_mix": gain(ks[10], (DEPTH, D_MODEL)),
        "norm_ffn": gain(ks[11], (DEPTH, D_MODEL)),
        "norm_ple": gain(ks[12], (DEPTH, D_MODEL)),
        "norm_final": gain(ks[13], (D_MODEL,)),
        "w_ff_gate": nrm(ks[14], (DEPTH, D_MODEL, D_FF), D_MODEL),
        "w_ff_up": nrm(ks[15], (DEPTH, D_MODEL, D_FF), D_MODEL),
        "w_ff_down": nrm(ks[16], (DEPTH, D_FF, D_MODEL), D_FF),
        "w_ple_gate": nrm(ks[17], (DEPTH, D_MODEL, D_MODEL), D_MODEL),
        "w_ple": nrm(ks[18], (DEPTH, PLE_DIM, D_MODEL), PLE_DIM),
    }


def _fwd_reference(x, p, w_in, w_br_attn, w_br_sg, w_out, sg_w, sg_b, sg_ln_g, sg_ln_b,
              norm_mix, norm_ffn, norm_ple, norm_final, w_ff_gate, w_ff_up, w_ff_down,
              w_ple_gate, w_ple):
    B, S, D = x.shape
    cos, sin = rope_tables(S)
    o_sg0 = ATTN_QKV_WIDTH
    o_g0 = ATTN_QKV_WIDTH + 2 * SG_WIDTH
    for i in range(DEPTH):
        h = rms_norm(x, norm_mix[i])
        z = h @ w_in[i]
        qkv = z[..., :ATTN_QKV_WIDTH].reshape(B, S, 3, N_ATTN_GROUPS, HEADS_PER_GROUP, HEAD_DIM)
        q = apply_rope(qkv[:, :, 0], cos, sin)
        k = apply_rope(qkv[:, :, 1], cos, sin)
        v = qkv[:, :, 2]
        outs, lses = [], []
        for g, (win, dil) in enumerate(ATTN_GROUPS):
            o_g, l_g = dilated_window_attention(q[:, :, g], k[:, :, g], v[:, :, g], win, dil)
            outs.append(o_g)
            lses.append(l_g)
        w_grp = jax.nn.softmax(jnp.stack(lses, axis=0), axis=0)
        attn = jnp.einsum("gbsh,gbshd->bshd", w_grp, jnp.stack(outs, axis=0))
        y_attn = attn.reshape(B, S, ATTN_OUT_WIDTH).astype(x.dtype) @ w_br_attn[i]

        sg = spatial_gating(z[..., o_sg0:o_sg0 + SG_WIDTH], z[..., o_sg0 + SG_WIDTH:o_g0],
                            sg_w[i], sg_b[i], sg_ln_g[i], sg_ln_b[i])
        y_sg = sg @ w_br_sg[i]

        gates = jax.nn.sigmoid(z[..., o_g0:].astype(jnp.float32)).reshape(B, S, N_BRANCHES, D).astype(x.dtype)
        merged = gates[:, :, 0] * y_attn + gates[:, :, 1] * y_sg
        x = x + merged @ w_out[i]

        h2 = rms_norm(x, norm_ffn[i])
        x = x + (jax.nn.silu(h2 @ w_ff_gate[i]) * (h2 @ w_ff_up[i])) @ w_ff_down[i]

        gate_p = jax.nn.sigmoid((rms_norm(x, norm_ple[i]) @ w_ple_gate[i]).astype(jnp.float32)).astype(x.dtype)
        x = x + gate_p * (p[i].astype(x.dtype) @ w_ple[i])
    return rms_norm(x, norm_final)


import jax as _jax
import jax.numpy as _jnp

TWIN_FORMAT = 'train_step'
FWD_PARAMS = ['x', 'p', 'w_in', 'w_br_attn', 'w_br_sg', 'w_out', 'sg_w', 'sg_b', 'sg_ln_g', 'sg_ln_b', 'norm_mix', 'norm_ffn', 'norm_ple', 'norm_final', 'w_ff_gate', 'w_ff_up', 'w_ff_down', 'w_ple_gate', 'w_ple']
TWIN_WEIGHTS = ['w_in', 'w_br_attn', 'w_br_sg', 'w_out', 'sg_w', 'sg_b', 'sg_ln_g', 'sg_ln_b', 'norm_mix', 'norm_ffn', 'norm_ple', 'norm_final', 'w_ff_gate', 'w_ff_up', 'w_ff_down', 'w_ple_gate', 'w_ple']
TWIN_DIFF_INPUT = 'x'
TWIN_INPUTS = ['x', 'p', 'w_in', 'w_br_attn', 'w_br_sg', 'w_out', 'sg_w', 'sg_b', 'sg_ln_g', 'sg_ln_b', 'norm_mix', 'norm_ffn', 'norm_ple', 'norm_final', 'w_ff_gate', 'w_ff_up', 'w_ff_down', 'w_ple_gate', 'w_ple', 'loss_target', 'm_w_in', 'm_w_br_attn', 'm_w_br_sg', 'm_w_out', 'm_sg_w', 'm_sg_b', 'm_sg_ln_g', 'm_sg_ln_b', 'm_norm_mix', 'm_norm_ffn', 'm_norm_ple', 'm_norm_final', 'm_w_ff_gate', 'm_w_ff_up', 'm_w_ff_down', 'm_w_ple_gate', 'm_w_ple', 'v_w_in', 'v_w_br_attn', 'v_w_br_sg', 'v_w_out', 'v_sg_w', 'v_sg_b', 'v_sg_ln_g', 'v_sg_ln_b', 'v_norm_mix', 'v_norm_ffn', 'v_norm_ple', 'v_norm_final', 'v_w_ff_gate', 'v_w_ff_up', 'v_w_ff_down', 'v_w_ple_gate', 'v_w_ple']
TWIN_OUTPUTS = ['loss', 'grad_x', 'grad_w_in', 'grad_w_br_attn', 'grad_w_br_sg', 'grad_w_out', 'grad_sg_w', 'grad_sg_b', 'grad_sg_ln_g', 'grad_sg_ln_b', 'grad_norm_mix', 'grad_norm_ffn', 'grad_norm_ple', 'grad_norm_final', 'grad_w_ff_gate', 'grad_w_ff_up', 'grad_w_ff_down', 'grad_w_ple_gate', 'grad_w_ple', 'delta_w_in', 'delta_w_br_attn', 'delta_w_br_sg', 'delta_w_out', 'delta_sg_w', 'delta_sg_b', 'delta_sg_ln_g', 'delta_sg_ln_b', 'delta_norm_mix', 'delta_norm_ffn', 'delta_norm_ple', 'delta_norm_final', 'delta_w_ff_gate', 'delta_w_ff_up', 'delta_w_ff_down', 'delta_w_ple_gate', 'delta_w_ple', 'new_m_w_in', 'new_m_w_br_attn', 'new_m_w_br_sg', 'new_m_w_out', 'new_m_sg_w', 'new_m_sg_b', 'new_m_sg_ln_g', 'new_m_sg_ln_b', 'new_m_norm_mix', 'new_m_norm_ffn', 'new_m_norm_ple', 'new_m_norm_final', 'new_m_w_ff_gate', 'new_m_w_ff_up', 'new_m_w_ff_down', 'new_m_w_ple_gate', 'new_m_w_ple', 'new_v_w_in', 'new_v_w_br_attn', 'new_v_w_br_sg', 'new_v_w_out', 'new_v_sg_w', 'new_v_sg_b', 'new_v_sg_ln_g', 'new_v_sg_ln_b', 'new_v_norm_mix', 'new_v_norm_ffn', 'new_v_norm_ple', 'new_v_norm_final', 'new_v_w_ff_gate', 'new_v_w_ff_up', 'new_v_w_ff_down', 'new_v_w_ple_gate', 'new_v_w_ple']
TWIN_LEAF_KINDS = {'loss': 'loss', 'grad_x': 'grad_x', 'grad_w_in': 'grad_w', 'grad_w_br_attn': 'grad_w', 'grad_w_br_sg': 'grad_w', 'grad_w_out': 'grad_w', 'grad_sg_w': 'grad_w', 'grad_sg_b': 'grad_w', 'grad_sg_ln_g': 'grad_w', 'grad_sg_ln_b': 'grad_w', 'grad_norm_mix': 'grad_w', 'grad_norm_ffn': 'grad_w', 'grad_norm_ple': 'grad_w', 'grad_norm_final': 'grad_w', 'grad_w_ff_gate': 'grad_w', 'grad_w_ff_up': 'grad_w', 'grad_w_ff_down': 'grad_w', 'grad_w_ple_gate': 'grad_w', 'grad_w_ple': 'grad_w', 'delta_w_in': 'delta_w', 'delta_w_br_attn': 'delta_w', 'delta_w_br_sg': 'delta_w', 'delta_w_out': 'delta_w', 'delta_sg_w': 'delta_w', 'delta_sg_b': 'delta_w', 'delta_sg_ln_g': 'delta_w', 'delta_sg_ln_b': 'delta_w', 'delta_norm_mix': 'delta_w', 'delta_norm_ffn': 'delta_w', 'delta_norm_ple': 'delta_w', 'delta_norm_final': 'delta_w', 'delta_w_ff_gate': 'delta_w', 'delta_w_ff_up': 'delta_w', 'delta_w_ff_down': 'delta_w', 'delta_w_ple_gate': 'delta_w', 'delta_w_ple': 'delta_w', 'new_m_w_in': 'new_m', 'new_m_w_br_attn': 'new_m', 'new_m_w_br_sg': 'new_m', 'new_m_w_out': 'new_m', 'new_m_sg_w': 'new_m', 'new_m_sg_b': 'new_m', 'new_m_sg_ln_g': 'new_m', 'new_m_sg_ln_b': 'new_m', 'new_m_norm_mix': 'new_m', 'new_m_norm_ffn': 'new_m', 'new_m_norm_ple': 'new_m', 'new_m_norm_final': 'new_m', 'new_m_w_ff_gate': 'new_m', 'new_m_w_ff_up': 'new_m', 'new_m_w_ff_down': 'new_m', 'new_m_w_ple_gate': 'new_m', 'new_m_w_ple': 'new_m', 'new_v_w_in': 'new_v', 'new_v_w_br_attn': 'new_v', 'new_v_w_br_sg': 'new_v', 'new_v_w_out': 'new_v', 'new_v_sg_w': 'new_v', 'new_v_sg_b': 'new_v', 'new_v_sg_ln_g': 'new_v', 'new_v_sg_ln_b': 'new_v', 'new_v_norm_mix': 'new_v', 'new_v_norm_ffn': 'new_v', 'new_v_norm_ple': 'new_v', 'new_v_norm_final': 'new_v', 'new_v_w_ff_gate': 'new_v', 'new_v_w_ff_up': 'new_v', 'new_v_w_ff_down': 'new_v', 'new_v_w_ple_gate': 'new_v', 'new_v_w_ple': 'new_v'}


def _forward(args):
    return _fwd_reference(*[args[k] for k in FWD_PARAMS])


def _output_shape():
    out = _jax.eval_shape(lambda: _forward(_fwd_setup_inputs(0)))
    return out.shape, out.dtype

N_MICROBATCH = 1
ADAM_LR = 0.001
ADAM_B1 = 0.9
ADAM_B2 = 0.999
ADAM_EPS = 1e-08
ADAM_WD = 0.01
ADAM_STEP = 10
PER_EXAMPLE_BATCH_AXIS = {'x': 0, 'p': 1, 'loss_target': 0}
SHARED_INPUTS = []
_WEIGHT_DTYPES = {'w_in': _jnp.float32, 'w_br_attn': _jnp.float32, 'w_br_sg': _jnp.float32, 'w_out': _jnp.float32, 'sg_w': _jnp.float32, 'sg_b': _jnp.float32, 'sg_ln_g': _jnp.float32, 'sg_ln_b': _jnp.float32, 'norm_mix': _jnp.float32, 'norm_ffn': _jnp.float32, 'norm_ple': _jnp.float32, 'norm_final': _jnp.float32, 'w_ff_gate': _jnp.float32, 'w_ff_up': _jnp.float32, 'w_ff_down': _jnp.float32, 'w_ple_gate': _jnp.float32, 'w_ple': _jnp.float32}
MOMENT_SCALE = {'w_in': 2.030769e-02, 'w_br_attn': 5.253440e-03, 'w_br_sg': 2.853428e-02, 'w_out': 2.897262e-02, 'sg_w': 4.054432e-02, 'sg_b': 3.991700e-02, 'sg_ln_g': 4.164536e-02, 'sg_ln_b': 4.013875e-02, 'norm_mix': 4.661389e-02, 'norm_ffn': 5.520718e-02, 'norm_ple': 1.351341e-02, 'norm_final': 1.598191e+01, 'w_ff_gate': 2.417167e-02, 'w_ff_up': 2.339249e-02, 'w_ff_down': 3.880151e-02, 'w_ple_gate': 1.362112e-02, 'w_ple': 3.478193e-02}


def _to_microbatches(a, axis):
    t = _jnp.moveaxis(a, axis, 0)
    t = t.reshape((N_MICROBATCH, t.shape[0] // N_MICROBATCH) + t.shape[1:])
    return _jnp.moveaxis(t, 1, axis + 1)


def setup_inputs(seed: int = 0) -> dict:
    inp = _fwd_setup_inputs(seed)
    key = _jax.random.fold_in(_jax.random.key(seed), 7919)
    shape, _ = _output_shape()
    out = dict(inp)
    out["loss_target"] = _jax.random.normal(_jax.random.fold_in(key, 0), shape, _jnp.float32)
    for i, name in enumerate(TWIN_WEIGHTS):
        w = inp[name].astype(_jnp.float32)
        if MOMENT_SCALE is None:
            s = _jnp.sqrt(_jnp.mean(_jnp.square(w)) + 1e-30)
        else:
            s = MOMENT_SCALE[name]
        km, kv = _jax.random.split(_jax.random.fold_in(key, i + 1))
        out[name] = w
        out["m_" + name] = s * _jax.random.normal(km, w.shape, _jnp.float32)
        out["v_" + name] = (s * s) * _jax.random.uniform(kv, w.shape, _jnp.float32, 0.5, 1.5)
    if N_MICROBATCH > 1:
        for name, axis in PER_EXAMPLE_BATCH_AXIS.items():
            out[name] = _to_microbatches(out[name], axis)
    return {'x': out['x'], 'p': out['p'], 'w_in': out['w_in'], 'w_br_attn': out['w_br_attn'], 'w_br_sg': out['w_br_sg'], 'w_out': out['w_out'], 'sg_w': out['sg_w'], 'sg_b': out['sg_b'], 'sg_ln_g': out['sg_ln_g'], 'sg_ln_b': out['sg_ln_b'], 'norm_mix': out['norm_mix'], 'norm_ffn': out['norm_ffn'], 'norm_ple': out['norm_ple'], 'norm_final': out['norm_final'], 'w_ff_gate': out['w_ff_gate'], 'w_ff_up': out['w_ff_up'], 'w_ff_down': out['w_ff_down'], 'w_ple_gate': out['w_ple_gate'], 'w_ple': out['w_ple'], 'loss_target': out['loss_target'], 'm_w_in': out['m_w_in'], 'm_w_br_attn': out['m_w_br_attn'], 'm_w_br_sg': out['m_w_br_sg'], 'm_w_out': out['m_w_out'], 'm_sg_w': out['m_sg_w'], 'm_sg_b': out['m_sg_b'], 'm_sg_ln_g': out['m_sg_ln_g'], 'm_sg_ln_b': out['m_sg_ln_b'], 'm_norm_mix': out['m_norm_mix'], 'm_norm_ffn': out['m_norm_ffn'], 'm_norm_ple': out['m_norm_ple'], 'm_norm_final': out['m_norm_final'], 'm_w_ff_gate': out['m_w_ff_gate'], 'm_w_ff_up': out['m_w_ff_up'], 'm_w_ff_down': out['m_w_ff_down'], 'm_w_ple_gate': out['m_w_ple_gate'], 'm_w_ple': out['m_w_ple'], 'v_w_in': out['v_w_in'], 'v_w_br_attn': out['v_w_br_attn'], 'v_w_br_sg': out['v_w_br_sg'], 'v_w_out': out['v_w_out'], 'v_sg_w': out['v_sg_w'], 'v_sg_b': out['v_sg_b'], 'v_sg_ln_g': out['v_sg_ln_g'], 'v_sg_ln_b': out['v_sg_ln_b'], 'v_norm_mix': out['v_norm_mix'], 'v_norm_ffn': out['v_norm_ffn'], 'v_norm_ple': out['v_norm_ple'], 'v_norm_final': out['v_norm_final'], 'v_w_ff_gate': out['v_w_ff_gate'], 'v_w_ff_up': out['v_w_ff_up'], 'v_w_ff_down': out['v_w_ff_down'], 'v_w_ple_gate': out['v_w_ple_gate'], 'v_w_ple': out['v_w_ple']}


def _loss(weights, diff, rest, loss_target):
    with _jax.named_scope("forward"):
        args = {**rest, TWIN_DIFF_INPUT: diff, **{k: w.astype(_WEIGHT_DTYPES[k]) for k, w in weights.items()}}
        y = _forward(args)
    with _jax.named_scope("loss_head"):
        err = _jnp.square(y.astype(_jnp.float32) - loss_target)
        return 0.5 * _jnp.sum(_jnp.mean(err, axis=-1)) if err.ndim else 0.5 * err


def _adamw(w, g, m, v):
    m = ADAM_B1 * m + (1.0 - ADAM_B1) * g
    v = ADAM_B2 * v + (1.0 - ADAM_B2) * _jnp.square(g)
    m_hat = m / (1.0 - ADAM_B1 ** ADAM_STEP)
    v_hat = v / (1.0 - ADAM_B2 ** ADAM_STEP)
    delta = -ADAM_LR * (m_hat / (_jnp.sqrt(v_hat) + ADAM_EPS) + ADAM_WD * w)
    return delta, m, v


def reference(x, p, w_in, w_br_attn, w_br_sg, w_out, sg_w, sg_b, sg_ln_g, sg_ln_b, norm_mix, norm_ffn, norm_ple, norm_final, w_ff_gate, w_ff_up, w_ff_down, w_ple_gate, w_ple, loss_target, m_w_in, m_w_br_attn, m_w_br_sg, m_w_out, m_sg_w, m_sg_b, m_sg_ln_g, m_sg_ln_b, m_norm_mix, m_norm_ffn, m_norm_ple, m_norm_final, m_w_ff_gate, m_w_ff_up, m_w_ff_down, m_w_ple_gate, m_w_ple, v_w_in, v_w_br_attn, v_w_br_sg, v_w_out, v_sg_w, v_sg_b, v_sg_ln_g, v_sg_ln_b, v_norm_mix, v_norm_ffn, v_norm_ple, v_norm_final, v_w_ff_gate, v_w_ff_up, v_w_ff_down, v_w_ple_gate, v_w_ple):
    given = dict(x=x, p=p, w_in=w_in, w_br_attn=w_br_attn, w_br_sg=w_br_sg, w_out=w_out, sg_w=sg_w, sg_b=sg_b, sg_ln_g=sg_ln_g, sg_ln_b=sg_ln_b, norm_mix=norm_mix, norm_ffn=norm_ffn, norm_ple=norm_ple, norm_final=norm_final, w_ff_gate=w_ff_gate, w_ff_up=w_ff_up, w_ff_down=w_ff_down, w_ple_gate=w_ple_gate, w_ple=w_ple, loss_target=loss_target, m_w_in=m_w_in, m_w_br_attn=m_w_br_attn, m_w_br_sg=m_w_br_sg, m_w_out=m_w_out, m_sg_w=m_sg_w, m_sg_b=m_sg_b, m_sg_ln_g=m_sg_ln_g, m_sg_ln_b=m_sg_ln_b, m_norm_mix=m_norm_mix, m_norm_ffn=m_norm_ffn, m_norm_ple=m_norm_ple, m_norm_final=m_norm_final, m_w_ff_gate=m_w_ff_gate, m_w_ff_up=m_w_ff_up, m_w_ff_down=m_w_ff_down, m_w_ple_gate=m_w_ple_gate, m_w_ple=m_w_ple, v_w_in=v_w_in, v_w_br_attn=v_w_br_attn, v_w_br_sg=v_w_br_sg, v_w_out=v_w_out, v_sg_w=v_sg_w, v_sg_b=v_sg_b, v_sg_ln_g=v_sg_ln_g, v_sg_ln_b=v_sg_ln_b, v_norm_mix=v_norm_mix, v_norm_ffn=v_norm_ffn, v_norm_ple=v_norm_ple, v_norm_final=v_norm_final, v_w_ff_gate=v_w_ff_gate, v_w_ff_up=v_w_ff_up, v_w_ff_down=v_w_ff_down, v_w_ple_gate=v_w_ple_gate, v_w_ple=v_w_ple)
    weights = {n: given[n] for n in TWIN_WEIGHTS}
    shared = {n: given[n] for n in SHARED_INPUTS}
    per_example = {n: given[n] for n in ['x', 'p']}
    grad_fn = _jax.value_and_grad(_loss, argnums=(0, 1))

    def one_microbatch(ex, loss_target):
        ex = dict(ex)
        diff = ex.pop(TWIN_DIFF_INPUT)
        return grad_fn(weights, diff, {**shared, **ex}, loss_target)

    if N_MICROBATCH == 1:
        loss, (grad_w, grad_x) = one_microbatch(per_example, given["loss_target"])
    else:
        def body(carry, xs):
            loss_sum, grad_sum = carry
            l_k, (gw_k, gx_k) = one_microbatch(xs[0], xs[1])
            with _jax.named_scope("update"):
                return (loss_sum + l_k, _jax.tree.map(_jnp.add, grad_sum, gw_k)), gx_k

        init = (_jnp.zeros((), _jnp.float32), _jax.tree.map(_jnp.zeros_like, weights))
        (loss, grad_w), grad_x = _jax.lax.scan(body, init, (per_example, given["loss_target"]))
    with _jax.named_scope("update"):
        delta_w, new_m, new_v = {}, {}, {}
        for n in TWIN_WEIGHTS:
            delta_w[n], new_m[n], new_v[n] = _adamw(weights[n], grad_w[n], given["m_" + n], given["v_" + n])
    return (loss, grad_x, *[grad_w[n] for n in TWIN_WEIGHTS], *[delta_w[n] for n in TWIN_WEIGHTS],
            *[new_m[n] for n in TWIN_WEIGHTS], *[new_v[n] for n in TWIN_WEIGHTS])
```

```python
import functools
import math

import jax
import jax.numpy as jnp
from jax import lax
from jax.experimental import pallas as pl
from jax.experimental.pallas import tpu as pltpu

F32 = jnp.float32
BF16 = jnp.bfloat16
MESH = pl.DeviceIdType.MESH

HEAD_DIM = 128
ATTN_GROUPS = ((128, 1), (512, 4), (2048, 16))
N_GROUPS = 3
HEADS = 4
QKV_W = 3 * N_GROUPS * HEADS * HEAD_DIM
ATTN_W = HEADS * HEAD_DIM
SG_CHUNK = 128
SG_GROUPS = 8
SG_W = 1024
RADIUS = 64
ROPE_THETA = 10000.0
NORM_EPS = 1e-6
NEG_INF = -1e30
ADAM_LR, ADAM_B1, ADAM_B2, ADAM_EPS, ADAM_WD, ADAM_STEP = 0.001, 0.9, 0.999, 1e-08, 0.01, 10

VMEM_CAP_V7X = 56 * 1024 * 1024
LANES = 128
EW_TILE_ELEMS = 256 * 1024

BIG = ("w_in", "w_br_attn", "w_br_sg", "w_out", "w_ff_gate", "w_ff_up", "w_ff_down", "w_ple_gate", "w_ple")
ROW_SHARDED = ("w_out", "w_ff_down", "w_ple_gate")
SMALL = ("sg_w", "sg_b", "sg_ln_g", "sg_ln_b", "norm_mix", "norm_ffn", "norm_ple", "norm_final")
WEIGHTS = ("w_in", "w_br_attn", "w_br_sg", "w_out", "sg_w", "sg_b", "sg_ln_g", "sg_ln_b", "norm_mix", "norm_ffn",
           "norm_ple", "norm_final", "w_ff_gate", "w_ff_up", "w_ff_down", "w_ple_gate", "w_ple")


def _pick(n, prefs):
    for t in prefs:
        if n % t == 0:
            return t
    return n


def _nbytes(shape, dtype):
    return math.prod(shape) * jnp.dtype(dtype).itemsize


def _vmem_limit(block_bytes, temp_bytes=0):
    est = 2 * block_bytes + temp_bytes
    return int(min(VMEM_CAP_V7X, max(16 * 1024 * 1024, est + est // 4)))


def _sigmoid(x):
    return 1.0 / (1.0 + jnp.exp(-x))


_GELU_C = math.sqrt(2.0 / math.pi)


def _gelu(x):
    return 0.5 * x * (1.0 + jnp.tanh(_GELU_C * (x + 0.044715 * (x * x * x))))


def _gelu_grad(x):
    t = jnp.tanh(_GELU_C * (x + 0.044715 * (x * x * x)))
    return 0.5 * (1.0 + t) + 0.5 * x * (1.0 - t * t) * (_GELU_C * (1.0 + 3.0 * 0.044715 * (x * x)))


def _lead(arr, l, blk, idx):
    if arr.ndim == 2:
        return pl.BlockSpec(blk, idx)
    return pl.BlockSpec((None,) + blk, lambda *g: (l,) + idx(*g))


def _mm(name, prods, M, N, outs, epilogue, tiles=(), rows=(), tm=1024, tn=1024, nk=1):
    assert M % tm == 0 and N % tn == 0, (name, M, N, tm, tn)
    in_specs, args, block_bytes = [], [], 0
    for p in prods:
        K = p["K"]
        assert K % nk == 0, (name, K, nk)
        tk = K // nk
        p["tk"] = tk
        a_off, bk_off, bn_off = p.get("a_off", 0), p.get("bk_off", 0), p.get("bn_off", 0)
        assert bn_off % tn == 0 and bk_off % tk == 0
        if p["mode"] == "nn":
            assert a_off % tk == 0
            a_spec = _lead(p["a"], p.get("al"), (tm, tk), lambda i, j, k, o=a_off // tk: (i, o + k))
            b_spec = _lead(p["b"], p.get("bl"), (tk, tn), lambda i, j, k, ok=bk_off // tk, on=bn_off // tn: (ok + k, on + j))
        elif p["mode"] == "nt":
            assert a_off % tk == 0
            a_spec = _lead(p["a"], p.get("al"), (tm, tk), lambda i, j, k, o=a_off // tk: (i, o + k))
            b_spec = _lead(p["b"], p.get("bl"), (tn, tk), lambda i, j, k, ok=bk_off // tk, on=bn_off // tn: (on + j, ok + k))
        else:
            assert a_off % tm == 0
            a_spec = _lead(p["a"], p.get("al"), (tk, tm), lambda i, j, k, o=a_off // tm: (k, o + i))
            b_spec = _lead(p["b"], p.get("bl"), (tk, tn), lambda i, j, k, on=bn_off // tn: (k, on + j))
        in_specs += [a_spec, b_spec]
        args += [p["a"], p["b"]]
        block_bytes += (tm + tn) * tk * 2
    for t in tiles:
        off = t.get("off", 0)
        assert off % tn == 0
        in_specs.append(_lead(t["x"], t.get("l"), (tm, tn), lambda i, j, k, o=off // tn: (i, o + j)))
        args.append(t["x"])
        block_bytes += tm * tn * t["x"].dtype.itemsize
    for r in rows:
        in_specs.append(pl.BlockSpec((1, tn), lambda i, j, k: (0, j)))
        args.append(r)
    out_shapes, out_specs, aliases = [], [], {}
    for o_i, o in enumerate(outs):
        off = o.get("col_off", 0)
        assert off % tn == 0
        out_shapes.append(jax.ShapeDtypeStruct(o["shape"], o["dtype"]))
        idx = lambda i, j, k, oo=off // tn: (i, oo + j)
        if len(o["shape"]) == 2:
            out_specs.append(pl.BlockSpec((tm, tn), idx))
        else:
            out_specs.append(pl.BlockSpec((None, tm, tn), lambda i, j, k, l=o["l"], f=idx: (l,) + f(i, j, k)))
        if o.get("alias") is not None:
            aliases[len(args)] = o_i
            in_specs.append(pl.BlockSpec(memory_space=pl.ANY))
            args.append(o["alias"])
        block_bytes += tm * tn * jnp.dtype(o["dtype"]).itemsize
    n_p, n_t, n_r, n_o = len(prods), len(tiles), len(rows), len(outs)
    n_alias = len(aliases)
    modes = [p["mode"] for p in prods]

    def body(*refs):
        ab = refs[: 2 * n_p]
        t_refs = refs[2 * n_p: 2 * n_p + n_t]
        r_refs = refs[2 * n_p + n_t: 2 * n_p + n_t + n_r]
        o_refs = refs[2 * n_p + n_t + n_r + n_alias: 2 * n_p + n_t + n_r + n_alias + n_o]
        acc_refs = refs[2 * n_p + n_t + n_r + n_alias + n_o:]
        parts = []
        for q in range(n_p):
            a, b = ab[2 * q][...], ab[2 * q + 1][...]
            dims = {"nn": (((1,), (0,)), ((), ())), "nt": (((1,), (1,)), ((), ())), "tn": (((0,), (0,)), ((), ()))}[modes[q]]
            parts.append(lax.dot_general(a, b, dims, preferred_element_type=F32))

        def finish(accs):
            res = epilogue(accs, [t[...] for t in t_refs], [r[...] for r in r_refs])
            for o_ref, val in zip(o_refs, res, strict=True):
                o_ref[...] = val.astype(o_ref.dtype)

        if nk == 1:
            finish(parts)
        else:
            k = pl.program_id(2)

            @pl.when(k == 0)
            def _():
                for acc, part in zip(acc_refs, parts):
                    acc[...] = part

            @pl.when(k > 0)
            def _():
                for acc, part in zip(acc_refs, parts):
                    acc[...] += part

            @pl.when(k == nk - 1)
            def _():
                finish([acc[...] for acc in acc_refs])

    scratch = [pltpu.VMEM((tm, tn), F32) for _ in prods] if nk > 1 else []
    temp = (n_p + 2) * tm * tn * 4
    res = pl.pallas_call(
        body, name=name, grid=(M // tm, N // tn, nk), in_specs=in_specs, out_specs=out_specs, out_shape=out_shapes,
        scratch_shapes=scratch, input_output_aliases=aliases,
        compiler_params=pltpu.CompilerParams(dimension_semantics=("parallel", "parallel", "arbitrary"),
                                             vmem_limit_bytes=_vmem_limit(block_bytes, temp)),
    )(*args)
    return res


def _first(accs, tiles, rows):
    return [accs[0]]


def _ew(name, fn, ins, outs, R, C, tr=None, tc=None):
    tc = tc or _pick(C, (2048, 1536, 1408, 1024, 896, 512, 384, 256, 128))
    tr = tr or _pick(R, [t for t in (512, 256, 128, 64, 32, 16) if t * tc <= EW_TILE_ELEMS] + [8])
    in_specs, args, bb = [], [], 0
    for x in ins:
        if isinstance(x, tuple):
            arr, l = x
            in_specs.append(pl.BlockSpec((None, tr, tc), lambda i, j, l=l: (l, i, j)))
        else:
            arr = x
            in_specs.append(pl.BlockSpec((tr, tc), lambda i, j: (i, j)))
        args.append(arr)
        bb += tr * tc * arr.dtype.itemsize
    out_shapes = [jax.ShapeDtypeStruct((R, C), d) for d in outs]
    out_specs = [pl.BlockSpec((tr, tc), lambda i, j: (i, j)) for _ in outs]
    bb += sum(tr * tc * jnp.dtype(d).itemsize for d in outs)
    n_in = len(ins)

    def body(*refs):
        res = fn(*[r[...] for r in refs[:n_in]])
        for o_ref, val in zip(refs[n_in:], res, strict=True):
            o_ref[...] = val.astype(o_ref.dtype)

    return pl.pallas_call(
        body, name=name, grid=(R // tr, C // tc), in_specs=in_specs, out_specs=out_specs, out_shape=out_shapes,
        compiler_params=pltpu.CompilerParams(dimension_semantics=("parallel", "parallel"),
                                             vmem_limit_bytes=_vmem_limit(bb, 6 * tr * tc * 4)),
    )(*args)


def _rmsnorm_fwd(name, x, g):
    S, D = x.shape
    tr = _pick(S, (256, 128, 64, 8))

    def body(x_ref, g_ref, h_ref):
        xv = x_ref[...]
        r = lax.rsqrt(jnp.mean(xv * xv, axis=-1, keepdims=True) + NORM_EPS)
        h_ref[...] = (xv * r * g_ref[...]).astype(BF16)

    return pl.pallas_call(
        body, name=name, grid=(S // tr,),
        in_specs=[pl.BlockSpec((tr, D), lambda i: (i, 0)), pl.BlockSpec((1, D), lambda i: (0, 0))],
        out_specs=pl.BlockSpec((tr, D), lambda i: (i, 0)), out_shape=jax.ShapeDtypeStruct((S, D), BF16),
        compiler_params=pltpu.CompilerParams(dimension_semantics=("parallel",),
                                             vmem_limit_bytes=_vmem_limit(tr * D * 6, 3 * tr * D * 4)),
    )(x, g)


def _rmsnorm_bwd(name, x, g, dh, dres):
    S, D = x.shape
    tr = _pick(S, (256, 128, 64, 8))

    def body(x_ref, g_ref, dh_ref, dres_ref, dx_ref, dxb_ref, dg_ref):
        xv = x_ref[...]
        dy = dh_ref[...].astype(F32)
        r = lax.rsqrt(jnp.mean(xv * xv, axis=-1, keepdims=True) + NORM_EPS)
        a = dy * g_ref[...]
        dx = dres_ref[...] + r * a - xv * (r * r * r) * jnp.mean(a * xv, axis=-1, keepdims=True)
        dx_ref[...] = dx
        dxb_ref[...] = dx.astype(BF16)
        part = jnp.sum(dy * xv * r, axis=0, keepdims=True)

        @pl.when(pl.program_id(0) == 0)
        def _():
            dg_ref[...] = part

        @pl.when(pl.program_id(0) > 0)
        def _():
            dg_ref[...] += part

    row = pl.BlockSpec((tr, D), lambda i: (i, 0))
    vec = pl.BlockSpec((1, D), lambda i: (0, 0))
    return pl.pallas_call(
        body, name=name, grid=(S // tr,), in_specs=[row, vec, row, row], out_specs=[row, row, vec],
        out_shape=[jax.ShapeDtypeStruct((S, D), F32), jax.ShapeDtypeStruct((S, D), BF16), jax.ShapeDtypeStruct((1, D), F32)],
        compiler_params=pltpu.CompilerParams(dimension_semantics=("arbitrary",),
                                             vmem_limit_bytes=_vmem_limit(tr * D * 18, 5 * tr * D * 4)),
    )(x, g, dh, dres)


def _loss_head(x, g, target):
    S, D = x.shape
    tr = _pick(S, (256, 128, 64, 8))

    def body(x_ref, g_ref, t_ref, loss_ref, dx_ref, dxb_ref, dg_ref):
        xv = x_ref[...]
        r = lax.rsqrt(jnp.mean(xv * xv, axis=-1, keepdims=True) + NORM_EPS)
        xn = xv * r
        diff = xn * g_ref[...] - t_ref[...]
        dy = diff * (1.0 / D)
        a = dy * g_ref[...]
        dx = r * a - xv * (r * r * r) * jnp.mean(a * xv, axis=-1, keepdims=True)
        dx_ref[...] = dx
        dxb_ref[...] = dx.astype(BF16)
        part = jnp.sum(dy * xn, axis=0, keepdims=True)
        cell = (lax.broadcasted_iota(jnp.int32, (8, LANES), 0) == 0) & (lax.broadcasted_iota(jnp.int32, (8, LANES), 1) == 0)
        lpart = jnp.where(cell, 0.5 * jnp.sum(jnp.mean(diff * diff, axis=-1, keepdims=True)), 0.0)

        @pl.when(pl.program_id(0) == 0)
        def _():
            dg_ref[...] = part
            loss_ref[...] = lpart

        @pl.when(pl.program_id(0) > 0)
        def _():
            dg_ref[...] += part
            loss_ref[...] += lpart

    row = pl.BlockSpec((tr, D), lambda i: (i, 0))
    vec = pl.BlockSpec((1, D), lambda i: (0, 0))
    return pl.pallas_call(
        body, name="loss_head", grid=(S // tr,), in_specs=[row, vec, row],
        out_specs=[pl.BlockSpec((8, LANES), lambda i: (0, 0)), row, row, vec],
        out_shape=[jax.ShapeDtypeStruct((8, LANES), F32), jax.ShapeDtypeStruct((S, D), F32),
                   jax.ShapeDtypeStruct((S, D), BF16), jax.ShapeDtypeStruct((1, D), F32)],
        compiler_params=pltpu.CompilerParams(dimension_semantics=("arbitrary",),
                                             vmem_limit_bytes=_vmem_limit(tr * D * 14, 6 * tr * D * 4)),
    )(x, g, target)


def _rope_tables(S):
    pos = jnp.arange(S, dtype=F32)
    inv_freq = ROPE_THETA ** (-jnp.arange(0, HEAD_DIM, 2, dtype=F32) / HEAD_DIM)
    ang = pos[:, None] * inv_freq[None, :]
    cos, sin = jnp.cos(ang), jnp.sin(ang)
    return jnp.concatenate([cos, cos], axis=-1), jnp.concatenate([-sin, sin], axis=-1)


def _rope_fwd(name, z, cosf, sinf):
    S = z.shape[0]
    tr = _pick(S, (256, 128, 64, 8))
    n_rot = 2 * N_GROUPS * HEADS

    def body(z_ref, c_ref, s_ref, o_ref):
        c, s = c_ref[...], s_ref[...]
        for j in range(QKV_W // HEAD_DIM):
            t = z_ref[:, j * HEAD_DIM:(j + 1) * HEAD_DIM]
            if j < n_rot:
                t = t * c + pltpu.roll(t, HEAD_DIM // 2, axis=1) * s
            o_ref[:, j * HEAD_DIM:(j + 1) * HEAD_DIM] = t.astype(BF16)

    tab = pl.BlockSpec((tr, HEAD_DIM), lambda i: (i, 0))
    return pl.pallas_call(
        body, name=name, grid=(S // tr,), in_specs=[pl.BlockSpec((tr, QKV_W), lambda i: (i, 0)), tab, tab],
        out_specs=pl.BlockSpec((tr, QKV_W), lambda i: (i, 0)), out_shape=jax.ShapeDtypeStruct((S, QKV_W), BF16),
        compiler_params=pltpu.CompilerParams(dimension_semantics=("parallel",),
                                             vmem_limit_bytes=_vmem_limit(tr * QKV_W * 6, tr * QKV_W * 4)),
    )(z, cosf, sinf)


def _rope_bwd(name, dq, dk, dv, cosf, sinf, dz):
    S = dq.shape[0]
    tr = _pick(S, (256, 128, 64, 8))
    W3 = QKV_W // 3
    nh = W3 // HEAD_DIM

    def body(dq_ref, dk_ref, dv_ref, c_ref, s_ref, dz_in, o_ref):
        c, s = c_ref[...], s_ref[...]
        for part, ref in enumerate((dq_ref, dk_ref)):
            for j in range(nh):
                t = ref[:, j * HEAD_DIM:(j + 1) * HEAD_DIM].astype(F32)
                t = t * c - pltpu.roll(t, HEAD_DIM // 2, axis=1) * s
                o_ref[:, part * W3 + j * HEAD_DIM: part * W3 + (j + 1) * HEAD_DIM] = t.astype(BF16)
        o_ref[:, 2 * W3:] = dv_ref[...]

    third = pl.BlockSpec((tr, W3), lambda i: (i, 0))
    tab = pl.BlockSpec((tr, HEAD_DIM), lambda i: (i, 0))
    return pl.pallas_call(
        body, name=name, grid=(S // tr,),
        in_specs=[third, third, third, tab, tab, pl.BlockSpec(memory_space=pl.ANY)],
        out_specs=pl.BlockSpec((tr, QKV_W), lambda i: (i, 0)), out_shape=jax.ShapeDtypeStruct(dz.shape, dz.dtype),
        input_output_aliases={5: 0},
        compiler_params=pltpu.CompilerParams(dimension_semantics=("parallel",),
                                             vmem_limit_bytes=_vmem_limit(tr * QKV_W * 4, tr * QKV_W * 4)),
    )(dq, dk, dv, cosf, sinf, dz)


ATTN_TQ = 256


def _window(i0, d, S):
    W = min(S, ATTN_TQ + 2 * RADIUS * d)
    start = jnp.clip(i0 - RADIUS * d, 0, S - W)
    return W, pl.multiple_of(start, RADIUS)


def _band_mask(shape, q_axis, off, d):
    kq = lax.broadcasted_iota(jnp.int32, shape, 1 - q_axis) - lax.broadcasted_iota(jnp.int32, shape, q_axis) + off
    return (jnp.abs(kq) <= RADIUS * d) & ((kq & (d - 1)) == 0)


_NT = (((1,), (1,)), ((), ()))


def _attn_fwd(name, qkv):
    S = qkv.shape[0]
    T = ATTN_TQ
    scale = HEAD_DIM ** -0.5
    nq = N_GROUPS * HEADS

    def body(*refs):
        q_refs, k_refs, v_refs = refs[0:3], refs[3:6], refs[6:9]
        o_ref, lc_ref, lr_ref = refs[9:12]
        i0 = pl.program_id(1) * T
        m = jnp.full((T, 1), NEG_INF, F32)
        l = jnp.zeros((T, 1), F32)
        acc = jnp.zeros((T, HEAD_DIM), F32)
        for g, (_, d) in enumerate(ATTN_GROUPS):
            W, start = _window(i0, d, S)
            kw = k_refs[g][pl.ds(start, W), :]
            vw = v_refs[g][pl.ds(start, W), :]
            s = lax.dot_general(q_refs[g][...], kw, _NT, preferred_element_type=F32) * scale
            s = jnp.where(_band_mask((T, W), 0, start - i0, d), s, NEG_INF)
            m_new = jnp.maximum(m, jnp.max(s, axis=1, keepdims=True))
            alpha = jnp.exp(m - m_new)
            p = jnp.exp(s - m_new)
            l = l * alpha + jnp.sum(p, axis=1, keepdims=True)
            acc = acc * alpha + jnp.dot(p.astype(BF16), vw, preferred_element_type=F32)
            m = m_new
        o_ref[...] = (acc / l).astype(BF16)
        lse = m + jnp.log(l)
        lc_ref[...] = lse
        lr_ref[...] = jnp.broadcast_to(lse, (T, LANES)).T[0:1, :]

    in_specs = [pl.BlockSpec((T, HEAD_DIM), lambda h, i, g=g: (i, g * HEADS + h)) for g in range(N_GROUPS)]
    in_specs += [pl.BlockSpec((S, HEAD_DIM), lambda h, i, g=g: (0, nq + g * HEADS + h)) for g in range(N_GROUPS)]
    in_specs += [pl.BlockSpec((S, HEAD_DIM), lambda h, i, g=g: (0, 2 * nq + g * HEADS + h)) for g in range(N_GROUPS)]
    wmax = min(S, T + 2 * RADIUS * ATTN_GROUPS[-1][1])
    return pl.pallas_call(
        body, name=name, grid=(HEADS, S // T), in_specs=in_specs,
        out_specs=[pl.BlockSpec((T, HEAD_DIM), lambda h, i: (i, h)), pl.BlockSpec((None, T, 1), lambda h, i: (h, i, 0)),
                   pl.BlockSpec((None, 1, T), lambda h, i: (h, 0, i))],
        out_shape=[jax.ShapeDtypeStruct((S, ATTN_W), BF16), jax.ShapeDtypeStruct((HEADS, S, 1), F32),
                   jax.ShapeDtypeStruct((HEADS, 1, S), F32)],
        compiler_params=pltpu.CompilerParams(dimension_semantics=("parallel", "arbitrary"),
                                             vmem_limit_bytes=_vmem_limit(6 * S * HEAD_DIM * 2 + 8 * T * HEAD_DIM * 4, 5 * T * wmax * 4)),
    )(*([qkv] * 9))


def _attn_bwd(name, qkv, attn, dattn, lse_c, lse_r):
    S = qkv.shape[0]
    T = ATTN_TQ
    scale = HEAD_DIM ** -0.5
    nq = N_GROUPS * HEADS
    W3 = QKV_W // 3
    n_i = S // T

    def body(q_ref, k_ref, v_ref, o_ref, do_ref, lc_ref, lr_ref, dq_ref, dk_ref, dv_ref, dk_acc, dv_acc):
        g_id, i = pl.program_id(1), pl.program_id(2)
        i0 = i * T

        @pl.when(i == 0)
        def _():
            dk_acc[...] = jnp.zeros_like(dk_acc)
            dv_acc[...] = jnp.zeros_like(dv_acc)

        q, do = q_ref[...], do_ref[...]
        dof = do.astype(F32)
        delta_c = jnp.sum(dof * o_ref[...].astype(F32), axis=1, keepdims=True)
        delta_r = jnp.broadcast_to(delta_c, (T, LANES)).T[0:1, :]
        lse_col, lse_row = lc_ref[...], lr_ref[...]

        def group(d):
            W, start = _window(i0, d, S)
            kw = k_ref[pl.ds(start, W), :]
            vw = v_ref[pl.ds(start, W), :]
            s = lax.dot_general(q, kw, _NT, preferred_element_type=F32) * scale
            p = jnp.where(_band_mask((T, W), 0, start - i0, d), jnp.exp(s - lse_col), 0.0)
            dp = lax.dot_general(do, vw, _NT, preferred_element_type=F32)
            ds = p * (dp - delta_c)
            dq_ref[...] = (jnp.dot(ds.astype(BF16), kw, preferred_element_type=F32) * scale).astype(BF16)
            st = lax.dot_general(kw, q, _NT, preferred_element_type=F32) * scale
            pt = jnp.where(_band_mask((W, T), 1, start - i0, d), jnp.exp(st - lse_row), 0.0)
            dpt = lax.dot_general(vw, do, _NT, preferred_element_type=F32)
            dst = pt * (dpt - delta_r)
            dk_acc[pl.ds(start, W), :] += jnp.dot(dst.astype(BF16), q, preferred_element_type=F32) * scale
            dv_acc[pl.ds(start, W), :] += jnp.dot(pt.astype(BF16), do, preferred_element_type=F32)

        for g, (_, d) in enumerate(ATTN_GROUPS):
            pl.when(g_id == g)(functools.partial(group, d))

        @pl.when(i == n_i - 1)
        def _():
            dk_ref[...] = dk_acc[...].astype(BF16)
            dv_ref[...] = dv_acc[...].astype(BF16)

    tile = lambda off: pl.BlockSpec((T, HEAD_DIM), lambda h, g, i: (i, off + g * HEADS + h))
    full = lambda off: pl.BlockSpec((S, HEAD_DIM), lambda h, g, i: (0, off + g * HEADS + h))
    headt = pl.BlockSpec((T, HEAD_DIM), lambda h, g, i: (i, h))
    wmax = min(S, T + 2 * RADIUS * ATTN_GROUPS[-1][1])
    return pl.pallas_call(
        body, name=name, grid=(HEADS, N_GROUPS, n_i),
        in_specs=[tile(0), full(nq), full(2 * nq), headt, headt,
                  pl.BlockSpec((None, T, 1), lambda h, g, i: (h, i, 0)), pl.BlockSpec((None, 1, T), lambda h, g, i: (h, 0, i))],
        out_specs=[tile(0), full(0), full(0)],
        out_shape=[jax.ShapeDtypeStruct((S, W3), BF16)] * 3,
        scratch_shapes=[pltpu.VMEM((S, HEAD_DIM), F32), pltpu.VMEM((S, HEAD_DIM), F32)],
        compiler_params=pltpu.CompilerParams(dimension_semantics=("parallel", "arbitrary", "arbitrary"),
                                             vmem_limit_bytes=_vmem_limit(4 * S * HEAD_DIM * 2 + 8 * T * HEAD_DIM * 4,
                                                                          2 * S * HEAD_DIM * 4 + 8 * T * wmax * 4)),
    )(qkv, qkv, qkv, attn, dattn, lse_c, lse_r)


def _sg_parts(u, v, lng, lnb):
    gu = _gelu(u)
    gv = _gelu(v)
    mu = jnp.mean(gv, axis=-1, keepdims=True)
    xc = gv - mu
    rstd = lax.rsqrt(jnp.mean(xc * xc, axis=-1, keepdims=True) + NORM_EPS)
    xhat = xc * rstd
    vn = xhat * lng + lnb
    return gu, xhat, rstd, vn


def _sg_fwd(name, z, sg_w, sg_bc, lng, lnb, o_sg0):
    S = z.shape[0]
    T = SG_CHUNK
    cb = 512
    assert o_sg0 % cb == 0
    b0 = o_sg0 // cb

    def body(u0, u1, v0, v1, w_ref, b_ref, g_ref, be_ref, o_ref):
        u = jnp.concatenate([u0[...], u1[...]], axis=1)
        v = jnp.concatenate([v0[...], v1[...]], axis=1)
        gu, _, _, vn = _sg_parts(u, v, g_ref[...], be_ref[...])
        vnb = vn.astype(BF16)
        for g in range(SG_GROUPS):
            sl = slice(g * SG_CHUNK, (g + 1) * SG_CHUNK)
            mixed = jnp.dot(w_ref[g], vnb[:, sl], preferred_element_type=F32) + b_ref[g]
            o_ref[:, sl] = (gu[:, sl] * mixed).astype(BF16)

    zs = lambda k: pl.BlockSpec((T, cb), lambda i, k=k: (i, b0 + k))
    const3 = lambda shp: pl.BlockSpec(shp, lambda i: (0, 0, 0))
    vec = pl.BlockSpec((1, SG_W), lambda i: (0, 0))
    return pl.pallas_call(
        body, name=name, grid=(S // T,),
        in_specs=[zs(0), zs(1), zs(2), zs(3), const3((SG_GROUPS, SG_CHUNK, SG_CHUNK)), const3((SG_GROUPS, SG_CHUNK, 1)), vec, vec],
        out_specs=pl.BlockSpec((T, SG_W), lambda i: (i, 0)), out_shape=jax.ShapeDtypeStruct((S, SG_W), BF16),
        compiler_params=pltpu.CompilerParams(dimension_semantics=("parallel",), vmem_limit_bytes=_vmem_limit(4 * 1024 * 1024, 8 * T * SG_W * 4)),
    )(z, z, z, z, sg_w, sg_bc, lng, lnb)


def _sg_bwd(name, z, dsg, sg_w, sg_wt, sg_bc, lng, lnb, o_sg0, dz):
    S = z.shape[0]
    T = SG_CHUNK
    cb = 512
    b0 = o_sg0 // cb

    def body(u0, u1, v0, v1, d_ref, w_ref, wt_ref, b_ref, g_ref, be_ref, dz_in, dz_ref, dw_ref, db_ref, dg_ref, dbe_ref, stage):
        i, jj = pl.program_id(0), pl.program_id(1)

        @pl.when(jj == 0)
        def _():
            u = jnp.concatenate([u0[...], u1[...]], axis=1)
            v = jnp.concatenate([v0[...], v1[...]], axis=1)
            gu, xhat, rstd, vn = _sg_parts(u, v, g_ref[...], be_ref[...])
            vnb = vn.astype(BF16)
            dsg_v = d_ref[...].astype(F32)
            dmix = dsg_v * gu
            dmixb = dmix.astype(BF16)
            dvn_parts, mixed_parts, dw_parts, db_parts = [], [], [], []
            for g in range(SG_GROUPS):
                sl = slice(g * SG_CHUNK, (g + 1) * SG_CHUNK)
                mixed_parts.append(jnp.dot(w_ref[g], vnb[:, sl], preferred_element_type=F32) + b_ref[g])
                dvn_parts.append(jnp.dot(wt_ref[g], dmixb[:, sl], preferred_element_type=F32))
                dw_parts.append(lax.dot_general(dmixb[:, sl], vnb[:, sl], _NT, preferred_element_type=F32))
                db_parts.append(jnp.sum(dmix[:, sl], axis=1, keepdims=True))
            mixed = jnp.concatenate(mixed_parts, axis=1)
            dvn = jnp.concatenate(dvn_parts, axis=1)
            dzu = dsg_v * mixed * _gelu_grad(u)
            dxh = dvn * g_ref[...]
            dgv = rstd * (dxh - jnp.mean(dxh, axis=-1, keepdims=True) - xhat * jnp.mean(dxh * xhat, axis=-1, keepdims=True))
            dzv = dgv * _gelu_grad(v)
            stage[0] = dzu[:, :cb].astype(BF16)
            stage[1] = dzu[:, cb:].astype(BF16)
            stage[2] = dzv[:, :cb].astype(BF16)
            stage[3] = dzv[:, cb:].astype(BF16)
            dgp = jnp.sum(dvn * xhat, axis=0, keepdims=True)
            dbp = jnp.sum(dvn, axis=0, keepdims=True)

            @pl.when(i == 0)
            def _():
                for g in range(SG_GROUPS):
                    dw_ref[g] = dw_parts[g]
                    db_ref[g] = db_parts[g]
                dg_ref[...] = dgp
                dbe_ref[...] = dbp

            @pl.when(i > 0)
            def _():
                for g in range(SG_GROUPS):
                    dw_ref[g] += dw_parts[g]
                    db_ref[g] += db_parts[g]
                dg_ref[...] += dgp
                dbe_ref[...] += dbp

        dz_ref[...] = stage[jj]

    zs = lambda k: pl.BlockSpec((T, cb), lambda i, jj, k=k: (i, b0 + k))
    const3 = lambda shp: pl.BlockSpec(shp, lambda i, jj: (0, 0, 0))
    vec = pl.BlockSpec((1, SG_W), lambda i, jj: (0, 0))
    return pl.pallas_call(
        body, name=name, grid=(S // T, 4),
        in_specs=[zs(0), zs(1), zs(2), zs(3), pl.BlockSpec((T, SG_W), lambda i, jj: (i, 0)),
                  const3((SG_GROUPS, SG_CHUNK, SG_CHUNK)), const3((SG_GROUPS, SG_CHUNK, SG_CHUNK)), const3((SG_GROUPS, SG_CHUNK, 1)),
                  vec, vec, pl.BlockSpec(memory_space=pl.ANY)],
        out_specs=[pl.BlockSpec((T, cb), lambda i, jj: (i, b0 + jj)), const3((SG_GROUPS, SG_CHUNK, SG_CHUNK)),
                   const3((SG_GROUPS, SG_CHUNK, 1)), vec, vec],
        out_shape=[jax.ShapeDtypeStruct(dz.shape, dz.dtype), jax.ShapeDtypeStruct((SG_GROUPS, SG_CHUNK, SG_CHUNK), F32),
                   jax.ShapeDtypeStruct((SG_GROUPS, SG_CHUNK, 1), F32), jax.ShapeDtypeStruct((1, SG_W), F32),
                   jax.ShapeDtypeStruct((1, SG_W), F32)],
        scratch_shapes=[pltpu.VMEM((4, T, cb), BF16)],
        input_output_aliases={10: 0},
        compiler_params=pltpu.CompilerParams(dimension_semantics=("arbitrary", "arbitrary"),
                                             vmem_limit_bytes=_vmem_limit(6 * 1024 * 1024, 16 * T * SG_W * 4)),
    )(z, z, z, z, dsg, sg_w, sg_wt, sg_bc, lng, lnb, dz)


def _gate_bwd(name, z, dmerged, y_attn, y_sg, o_g0, in_w):
    S, D = dmerged.shape
    tr = _pick(S, (512, 256, 128, 8))
    cb = _pick(D, (512, 256, 128))
    assert o_g0 % cb == 0
    nd = D // cb
    b0 = o_g0 // cb

    def body(z_ref, dm_ref, ya_ref, ys_ref, dz_ref, dy_ref):
        jj = pl.program_id(1)
        gate = _sigmoid(z_ref[...])
        dm = dm_ref[...].astype(F32)
        y = jnp.where(jj < nd, ya_ref[...], ys_ref[...]).astype(F32)
        dz_ref[...] = (dm * y * gate * (1.0 - gate)).astype(BF16)
        dy_ref[...] = (dm * gate).astype(BF16)

    half = pl.BlockSpec((tr, cb), lambda i, jj: (i, jj % nd))
    return pl.pallas_call(
        body, name=name, grid=(S // tr, 2 * nd),
        in_specs=[pl.BlockSpec((tr, cb), lambda i, jj: (i, b0 + jj)), half, half, half],
        out_specs=[pl.BlockSpec((tr, cb), lambda i, jj: (i, b0 + jj)), pl.BlockSpec((tr, cb), lambda i, jj: (i, jj))],
        out_shape=[jax.ShapeDtypeStruct((S, in_w), BF16), jax.ShapeDtypeStruct((S, 2 * D), BF16)],
        compiler_params=pltpu.CompilerParams(dimension_semantics=("parallel", "arbitrary"),
                                             vmem_limit_bytes=_vmem_limit(tr * cb * 14, 6 * tr * cb * 4)),
    )(z, dmerged, y_attn, y_sg)


def _row(v):
    return v.reshape(1, -1)


def _local_step(x, p, target, wf, small):
    S, D = x.shape
    L = p.shape[0]
    in_w = wf["w_in"].shape[2]
    ff = wf["w_ff_gate"].shape[2]
    ple = p.shape[2]
    o_sg0, o_g0 = QKV_W, QKV_W + 2 * SG_W
    cosf, sinf = _rope_tables(S)
    pb = p.astype(BF16)
    tmb = _pick(S, (1024, 512, 256))
    tn_in = _pick(in_w, (1536, 1024, 512))
    tn_d = _pick(D, (1024, 512, 256))
    tn_g = _pick(D, (512, 256))
    tn_ff = _pick(ff, (512, 256))
    nk_d = D // _pick(D, (512, 256))
    nk_ff = ff // _pick(ff, (512, 256))
    nk_in = in_w // _pick(in_w, (1536, 1024, 512))
    nk_s = S // _pick(S, (512, 256))

    saved = []
    xs = x
    for i in range(L):
        sv = {"x0": xs}
        h = _rmsnorm_fwd(f"norm_mix_{i}", xs, _row(small["norm_mix"][i]))
        (z,) = _mm(f"in_proj_{i}", [dict(a=h, b=wf["w_in"], bl=i, mode="nn", K=D)], S, in_w,
                   [dict(shape=(S, in_w), dtype=F32)], _first, tm=tmb, tn=tn_in, nk=nk_d)
        qkv = _rope_fwd(f"rope_{i}", z, cosf, sinf)
        attn, lse_c, lse_r = _attn_fwd(f"attn_{i}", qkv)
        sgw = small["sg_w"][i].astype(BF16)
        sgbc = small["sg_b"][i].reshape(SG_GROUPS, SG_CHUNK, 1)
        sg = _sg_fwd(f"sgu_{i}", z, sgw, sgbc, _row(small["sg_ln_g"][i]), _row(small["sg_ln_b"][i]), o_sg0)

        def merge(accs, tiles, rows):
            ya, ys = accs[0].astype(BF16), accs[1].astype(BF16)
            g0, g1 = _sigmoid(tiles[0]), _sigmoid(tiles[1])
            return [ya, ys, g0 * ya.astype(F32) + g1 * ys.astype(F32)]

        y_attn, y_sg, merged = _mm(
            f"branches_{i}",
            [dict(a=attn, b=wf["w_br_attn"], bl=i, mode="nn", K=ATTN_W), dict(a=sg, b=wf["w_br_sg"], bl=i, mode="nn", K=SG_W)],
            S, D, [dict(shape=(S, D), dtype=BF16)] * 3, merge,
            tiles=[dict(x=z, off=o_g0), dict(x=z, off=o_g0 + D)], tm=tmb, tn=tn_g)
        (x1,) = _mm(f"out_proj_{i}", [dict(a=merged, b=wf["w_out"], bl=i, mode="nn", K=D)], S, D,
                    [dict(shape=(S, D), dtype=F32)], lambda a, t, r: [t[0] + a[0]], tiles=[dict(x=xs)], tm=tmb, tn=tn_d, nk=nk_d)
        h2 = _rmsnorm_fwd(f"norm_ffn_{i}", x1, _row(small["norm_ffn"][i]))

        def swiglu(accs, tiles, rows):
            fg = accs[0].astype(BF16).astype(F32)
            fu = accs[1].astype(BF16).astype(F32)
            return [fg, fu, fg * _sigmoid(fg) * fu]

        ffg, ffu, act = _mm(
            f"ff_in_{i}",
            [dict(a=h2, b=wf["w_ff_gate"], bl=i, mode="nn", K=D), dict(a=h2, b=wf["w_ff_up"], bl=i, mode="nn", K=D)],
            S, ff, [dict(shape=(S, ff), dtype=BF16)] * 3, swiglu, tm=tmb, tn=tn_ff)
        (x2,) = _mm(f"ff_out_{i}", [dict(a=act, b=wf["w_ff_down"], bl=i, mode="nn", K=ff)], S, D,
                    [dict(shape=(S, D), dtype=F32)], lambda a, t, r: [t[0] + a[0]], tiles=[dict(x=x1)], tm=tmb, tn=tn_d, nk=nk_ff)
        h3 = _rmsnorm_fwd(f"norm_ple_{i}", x2, _row(small["norm_ple"][i]))

        def ple_mix(accs, tiles, rows):
            gp = _sigmoid(accs[0]).astype(BF16)
            pe = accs[1].astype(BF16)
            return [tiles[0] + gp.astype(F32) * pe.astype(F32), gp, pe]

        x3, gp, pe = _mm(
            f"ple_{i}",
            [dict(a=h3, b=wf["w_ple_gate"], bl=i, mode="nn", K=D), dict(a=pb, al=i, b=wf["w_ple"], bl=i, mode="nn", K=ple)],
            S, D, [dict(shape=(S, D), dtype=F32), dict(shape=(S, D), dtype=BF16), dict(shape=(S, D), dtype=BF16)], ple_mix,
            tiles=[dict(x=x2)], tm=tmb, tn=tn_g)
        sv.update(h=h, z=z, qkv=qkv, attn=attn, lse_c=lse_c, lse_r=lse_r, sg=sg, y_attn=y_attn, y_sg=y_sg, merged=merged,
                  x1=x1, h2=h2, ffg=ffg, ffu=ffu, act=act, x2=x2, h3=h3, gp=gp, pe=pe, sgw=sgw, sgbc=sgbc)
        saved.append(sv)
        xs = x3

    loss_cell, dx, dxb, dg_final = _loss_head(xs, _row(small["norm_final"]), target)

    gw = {n: None for n in BIG}
    gs = {n: [None] * L for n in SMALL if n != "norm_final"}

    def dw(n, i, a, a_off, b, bn_off, K_rows, N_cols, tm, tn):
        (gw[n],) = _mm(f"d_{n}_{i}", [dict(a=a, b=b, mode="tn", K=S, a_off=a_off, bn_off=bn_off)], K_rows, N_cols,
                       [dict(shape=(L, K_rows, N_cols), dtype=BF16, l=i, alias=gw[n])], _first, tm=tm, tn=tn, nk=nk_s)

    for i in reversed(range(L)):
        sv = saved[i]
        dpre, dpe = _ew(f"ple_gate_bwd_{i}",
                        lambda d, g, e: [d * e.astype(F32) * g.astype(F32) * (1.0 - g.astype(F32)), d * g.astype(F32)],
                        [dx, sv["gp"], sv["pe"]], [BF16, BF16], S, D)
        (dh3,) = _mm(f"d_h3_{i}", [dict(a=dpre, b=wf["w_ple_gate"], bl=i, mode="nt", K=D)], S, D,
                     [dict(shape=(S, D), dtype=F32)], _first, tm=tmb, tn=tn_d, nk=nk_d)
        dw("w_ple_gate", i, sv["h3"], 0, dpre, 0, D, D, tn_d, tn_d)
        dw("w_ple", i, pb[i], 0, dpe, 0, ple, D, _pick(ple, (256, 128)), _pick(D, (2048, 1024, 512, 256)))
        dx, dxb, gs["norm_ple"][i] = _rmsnorm_bwd(f"norm_ple_bwd_{i}", sv["x2"], _row(small["norm_ple"][i]), dh3, dx)
        def swiglu_bwd(accs, tiles, rows):
            da = accs[0].astype(BF16).astype(F32)
            fg, fu = tiles[0].astype(F32), tiles[1].astype(F32)
            sg_ = _sigmoid(fg)
            return [da * fu * (sg_ * (1.0 + fg * (1.0 - sg_))), da * (fg * sg_)]

        dffg, dffu = _mm(f"d_act_{i}", [dict(a=dxb, b=wf["w_ff_down"], bl=i, mode="nt", K=D)], S, ff,
                         [dict(shape=(S, ff), dtype=BF16)] * 2, swiglu_bwd, tiles=[dict(x=sv["ffg"]), dict(x=sv["ffu"])],
                         tm=tmb, tn=tn_ff, nk=nk_d)
        dw("w_ff_down", i, sv["act"], 0, dxb, 0, ff, D, tn_ff, _pick(D, (2048, 1024, 512, 256)))
        (dh2,) = _mm(f"d_h2_{i}", [dict(a=dffg, b=wf["w_ff_gate"], bl=i, mode="nt", K=ff),
                                   dict(a=dffu, b=wf["w_ff_up"], bl=i, mode="nt", K=ff)], S, D,
                     [dict(shape=(S, D), dtype=F32)], lambda a, t, r: [a[0] + a[1]], tm=tmb, tn=tn_d, nk=nk_ff)
        dw("w_ff_gate", i, sv["h2"], 0, dffg, 0, D, ff, _pick(D, (2048, 1024, 512, 256)), tn_ff)
        dw("w_ff_up", i, sv["h2"], 0, dffu, 0, D, ff, _pick(D, (2048, 1024, 512, 256)), tn_ff)
        dx, dxb, gs["norm_ffn"][i] = _rmsnorm_bwd(f"norm_ffn_bwd_{i}", sv["x1"], _row(small["norm_ffn"][i]), dh2, dx)
        (dmerged,) = _mm(f"d_merged_{i}", [dict(a=dxb, b=wf["w_out"], bl=i, mode="nt", K=D)], S, D,
                         [dict(shape=(S, D), dtype=BF16)], _first, tm=tmb, tn=tn_d, nk=nk_d)
        dw("w_out", i, sv["merged"], 0, dxb, 0, D, D, tn_d, tn_d)
        dz, dy = _gate_bwd(f"gate_bwd_{i}", sv["z"], dmerged, sv["y_attn"], sv["y_sg"], o_g0, in_w)
        (dattn,) = _mm(f"d_attn_{i}", [dict(a=dy, b=wf["w_br_attn"], bl=i, mode="nt", K=D)], S, ATTN_W,
                       [dict(shape=(S, ATTN_W), dtype=BF16)], _first, tm=tmb, tn=ATTN_W, nk=nk_d)
        (dsg,) = _mm(f"d_sg_{i}", [dict(a=dy, a_off=D, b=wf["w_br_sg"], bl=i, mode="nt", K=D)], S, SG_W,
                     [dict(shape=(S, SG_W), dtype=BF16)], _first, tm=tmb, tn=SG_W, nk=nk_d)
        dw("w_br_attn", i, sv["attn"], 0, dy, 0, ATTN_W, D, ATTN_W, _pick(D, (2048, 1024, 512, 256)))
        dw("w_br_sg", i, sv["sg"], 0, dy, D, SG_W, D, SG_W, _pick(D, (1024, 512, 256)))
        sgwt = jnp.swapaxes(small["sg_w"][i], 1, 2).astype(BF16)
        dz, gs["sg_w"][i], dsgb, dlg, dlb = _sg_bwd(f"sgu_bwd_{i}", sv["z"], dsg, sv["sgw"], sgwt, sv["sgbc"],
                                                    _row(small["sg_ln_g"][i]), _row(small["sg_ln_b"][i]), o_sg0, dz)
        gs["sg_b"][i], gs["sg_ln_g"][i], gs["sg_ln_b"][i] = dsgb.reshape(SG_GROUPS, SG_CHUNK), dlg[0], dlb[0]
        dq, dk, dv = _attn_bwd(f"attn_bwd_{i}", sv["qkv"], sv["attn"], dattn, sv["lse_c"], sv["lse_r"])
        dz = _rope_bwd(f"rope_bwd_{i}", dq, dk, dv, cosf, sinf, dz)
        (dh,) = _mm(f"d_h_{i}", [dict(a=dz, b=wf["w_in"], bl=i, mode="nt", K=in_w)], S, D,
                    [dict(shape=(S, D), dtype=F32)], _first, tm=tmb, tn=tn_d, nk=nk_in)
        dw("w_in", i, sv["h"], 0, dz, 0, D, in_w, tn_d, tn_in)
        dx, dxb, gs["norm_mix"][i] = _rmsnorm_bwd(f"norm_mix_bwd_{i}", sv["x0"], _row(small["norm_mix"][i]), dh, dx)

    gsmall = {n: jnp.stack([jnp.reshape(v, small[n].shape[1:]) for v in gs[n]]) for n in gs}
    gsmall["norm_final"] = dg_final[0]
    return loss_cell, dx, gw, gsmall


def _place():
    x, y, c = lax.axis_index("x"), lax.axis_index("y"), lax.axis_index("c")
    return x, y, c, 2 * x + y


def _chip_of(s):
    return s // 2, s % 2


def _aligned(v, m):
    return v if isinstance(v, int) else pl.multiple_of(v, m)


def _piece(name, shape, s, c):
    K, N = shape
    if name in ROW_SHARDED:
        ks = K // 4
        return s * ks + c * (ks // 2), ks // 2, 0, N
    ns = N // 4
    return c * (K // 2), K // 2, s * ns, ns


def _gather_weights(pieces, shapes):
    names = list(pieces)
    n_w = len(names)

    def body(*refs):
        src = refs[:n_w]
        dst = refs[n_w:2 * n_w]
        send_sems, recv_sems, local_sems = refs[2 * n_w:]
        x, y, c, s = _place()
        sib = (x, y, 1 - c)
        rel = [1, 2, 3]

        def where(w, ps, pc):
            r0, nr, c0, nc = _piece(names[w], shapes[names[w]], ps, pc)
            return dst[w].at[:, pl.ds(_aligned(r0, 16), nr), pl.ds(_aligned(c0, LANES), nc)]

        def copy(w, k, ps, pc, to, from_src=False):
            return pltpu.make_async_remote_copy(
                src_ref=src[w] if from_src else where(w, ps, pc), dst_ref=where(w, ps, pc),
                send_sem=send_sems.at[w, k], recv_sem=recv_sems.at[w, k], device_id=to, device_id_type=MESH)

        mine, first, passed = [], [], []
        for w in range(n_w):
            cp = pltpu.make_async_copy(src[w], where(w, s, c), local_sems.at[w])
            cp.start()
            mine.append(cp)
            first.append(copy(w, 0, s, c, sib, from_src=True))
            for j in rel:
                first.append(copy(w, j, s, c, (*_chip_of(s ^ j), c), from_src=True))
        for cp in first:
            cp.start()
        for w in range(n_w):
            for j in rel:
                copy(w, j, s ^ j, c, sib).wait_recv()
                fw = copy(w, 3 + j, s ^ j, c, sib)
                fw.start()
                passed.append(fw)
        for w in range(n_w):
            copy(w, 0, s, 1 - c, sib).wait_recv()
            for j in rel:
                copy(w, 3 + j, s ^ j, 1 - c, sib).wait_recv()
        for cp in first + passed:
            cp.wait_send()
        for cp in mine:
            cp.wait()

    anyspec = pl.BlockSpec(memory_space=pl.ANY)
    out = pl.pallas_call(
        body, name="gather_weights", in_specs=[anyspec] * n_w, out_specs=[anyspec] * n_w,
        out_shape=[jax.ShapeDtypeStruct((pieces[n].shape[0],) + tuple(shapes[n]), BF16) for n in names],
        scratch_shapes=[pltpu.SemaphoreType.DMA((n_w, 7)), pltpu.SemaphoreType.DMA((n_w, 7)), pltpu.SemaphoreType.DMA((n_w,))],
    )(*[pieces[n] for n in names])
    return dict(zip(names, out))


def _halves_view(name, g):
    L, K, N = g.shape
    if name in ROW_SHARDED:
        return g.reshape(L, 4, 2, K // 8, N)
    return g.reshape(L, 2, K // 2, N)


def _exchange_halves(gw):
    names = list(gw)
    n_w = len(names)
    views = [_halves_view(n, gw[n]) for n in names]
    half_shapes = [v.shape[:-3] + v.shape[-2:] for v in views]

    def body(*refs):
        src = refs[:n_w]
        own = refs[n_w:2 * n_w]
        got = refs[2 * n_w:3 * n_w]
        send_sems, recv_sems, local_sems = refs[3 * n_w:]
        x, y, c, s = _place()
        sib = (x, y, 1 - c)

        def half(w, hc):
            return src[w].at[:, :, hc] if len(half_shapes[w]) == 4 else src[w].at[:, hc]

        local, remote = [], []
        for w in range(n_w):
            local.append(pltpu.make_async_copy(half(w, c), own[w], local_sems.at[w]))
            remote.append(pltpu.make_async_remote_copy(src_ref=half(w, 1 - c), dst_ref=got[w], send_sem=send_sems.at[w],
                                                       recv_sem=recv_sems.at[w], device_id=sib, device_id_type=MESH))
        for cp in local + remote:
            cp.start()
        for cp in remote:
            cp.wait()
        for cp in local:
            cp.wait()

    anyspec = pl.BlockSpec(memory_space=pl.ANY)
    out = pl.pallas_call(
        body, name="exchange_halves", in_specs=[anyspec] * n_w, out_specs=[anyspec] * (2 * n_w),
        out_shape=[jax.ShapeDtypeStruct(hs, BF16) for hs in half_shapes] * 2,
        scratch_shapes=[pltpu.SemaphoreType.DMA((n_w,)), pltpu.SemaphoreType.DMA((n_w,)), pltpu.SemaphoreType.DMA((n_w,))],
    )(*views)
    return dict(zip(names, out[:n_w])), dict(zip(names, out[n_w:]))


def _scatter_chip_sums(psum):
    names = list(psum)
    n_w = len(names)

    def shard_shape(n):
        v = psum[n].shape
        return (v[0], v[2], v[3]) if n in ROW_SHARDED else (v[0], v[1], v[2] // 4)

    def body(*refs):
        src = refs[:n_w]
        dst = refs[n_w:2 * n_w]
        send_sems, recv_sems, local_sems = refs[2 * n_w:]
        x, y, c, s = _place()

        def shard(w, t):
            if names[w] in ROW_SHARDED:
                return src[w].at[:, t]
            ns = src[w].shape[2] // 4
            return src[w].at[:, :, pl.ds(pl.multiple_of(t * ns, LANES), ns)]

        local, remote = [], []
        for w in range(n_w):
            local.append(pltpu.make_async_copy(shard(w, s), dst[w].at[0], local_sems.at[w]))
            for j in (1, 2, 3):
                remote.append(pltpu.make_async_remote_copy(
                    src_ref=shard(w, s ^ j), dst_ref=dst[w].at[j], send_sem=send_sems.at[w, j - 1], recv_sem=recv_sems.at[w, j - 1],
                    device_id=(*_chip_of(s ^ j), c), device_id_type=MESH))
        for cp in local + remote:
            cp.start()
        for cp in remote:
            cp.wait()
        for cp in local:
            cp.wait()

    anyspec = pl.BlockSpec(memory_space=pl.ANY)
    out = pl.pallas_call(
        body, name="scatter_chip_sums", in_specs=[anyspec] * n_w, out_specs=[anyspec] * n_w,
        out_shape=[jax.ShapeDtypeStruct((4,) + shard_shape(n), BF16) for n in names],
        scratch_shapes=[pltpu.SemaphoreType.DMA((n_w, 3)), pltpu.SemaphoreType.DMA((n_w, 3)), pltpu.SemaphoreType.DMA((n_w,))],
    )(*[psum[n] for n in names])
    return dict(zip(names, out))


def _share_halves(ghalf):
    names = list(ghalf)
    n_w = len(names)

    def body(*refs):
        src = refs[:n_w]
        dst = refs[n_w:2 * n_w]
        send_sems, recv_sems, local_sems = refs[2 * n_w:]
        x, y, c, s = _place()
        sib = (x, y, 1 - c)
        local, remote = [], []
        for w in range(n_w):
            local.append(pltpu.make_async_copy(src[w], dst[w].at[:, c], local_sems.at[w]))
            remote.append(pltpu.make_async_remote_copy(src_ref=src[w], dst_ref=dst[w].at[:, c], send_sem=send_sems.at[w],
                                                       recv_sem=recv_sems.at[w], device_id=sib, device_id_type=MESH))
        for cp in local + remote:
            cp.start()
        for cp in remote:
            cp.wait()
        for cp in local:
            cp.wait()

    anyspec = pl.BlockSpec(memory_space=pl.ANY)
    out = pl.pallas_call(
        body, name="share_halves", in_specs=[anyspec] * n_w, out_specs=[anyspec] * n_w,
        out_shape=[jax.ShapeDtypeStruct((ghalf[n].shape[0], 2) + ghalf[n].shape[1:], F32) for n in names],
        scratch_shapes=[pltpu.SemaphoreType.DMA((n_w,)), pltpu.SemaphoreType.DMA((n_w,)), pltpu.SemaphoreType.DMA((n_w,))],
    )(*[ghalf[n] for n in names])
    return dict(zip(names, out))


def _gather_small(v):
    m_per, n = v.shape

    def body(x_ref, out_ref, send_sems, recv_sems, local_sem):
        x, y, c, s = _place()
        me, sibling = (x, y, c), (x, y, 1 - c)
        chips = [(1 - x, y), (x, 1 - y), (1 - x, 1 - y)]

        def rows(px, py, pc):
            return out_ref.at[pl.ds(pl.multiple_of((4 * px + 2 * py + pc) * m_per, 8), m_per), :]

        def copy(k, block, to, src=None):
            return pltpu.make_async_remote_copy(src_ref=rows(*block) if src is None else src, dst_ref=rows(*block),
                                                send_sem=send_sems.at[k], recv_sem=recv_sems.at[k], device_id=to, device_id_type=MESH)

        mine = pltpu.make_async_copy(x_ref, rows(*me), local_sem)
        mine.start()
        first = [copy(0, me, sibling, src=x_ref)]
        first += [copy(1 + j, me, (*chip, c), src=x_ref) for j, chip in enumerate(chips)]
        for cp in first:
            cp.start()
        passed = [copy(4 + j, (*chip, c), sibling) for j, chip in enumerate(chips)]
        for j, chip in enumerate(chips):
            copy(1 + j, (*chip, c), me).wait_recv()
            passed[j].start()
        copy(0, sibling, me).wait_recv()
        for j, chip in enumerate(chips):
            copy(4 + j, (*chip, 1 - c), me).wait_recv()
        for cp in first + passed:
            cp.wait_send()
        mine.wait()

    return pl.pallas_call(
        body, name="gather_small", out_shape=jax.ShapeDtypeStruct((8 * m_per, n), v.dtype),
        in_specs=[pl.BlockSpec(memory_space=pltpu.VMEM)], out_specs=pl.BlockSpec(memory_space=pltpu.VMEM),
        scratch_shapes=[pltpu.SemaphoreType.DMA((7,)), pltpu.SemaphoreType.DMA((7,)), pltpu.SemaphoreType.DMA],
        compiler_params=pltpu.CompilerParams(vmem_limit_bytes=_vmem_limit(9 * m_per * n * 4)),
    )(v)


def _adamw_math(w, g, m, v):
    m = ADAM_B1 * m + (1.0 - ADAM_B1) * g
    v = ADAM_B2 * v + (1.0 - ADAM_B2) * (g * g)
    m_hat = m / (1.0 - ADAM_B1 ** ADAM_STEP)
    v_hat = v / (1.0 - ADAM_B2 ** ADAM_STEP)
    delta = -ADAM_LR * (m_hat / (jnp.sqrt(v_hat) + ADAM_EPS) + ADAM_WD * w)
    return delta, m, v


def _adamw(name, w, g, m, v):
    shape = w.shape
    C = shape[-1]
    R = math.prod(shape[:-1])
    f = lambda a: a.reshape(R, C)
    delta, nm, nv = _ew(name, lambda w_, g_, m_, v_: list(_adamw_math(w_, g_, m_, v_)), [f(w), f(g), f(m), f(v)], [F32] * 3, R, C)
    return delta.reshape(shape), nm.reshape(shape), nv.reshape(shape)


def _pack_small(d):
    return jnp.concatenate([d[n].reshape(-1, LANES) for n in SMALL], axis=0)


def _unpack_small(flat, like):
    out, r = {}, 0
    for n in SMALL:
        k = like[n].size // LANES
        out[n] = flat[r:r + k].reshape(like[n].shape)
        r += k
    return out


def _small_update(gall, w, m, v):
    M = w.shape[0]
    tr = _pick(M, (552, 276, 184, 96, 48, 24, 8))

    def body(*refs):
        g = refs[0][...]
        for d in range(1, 8):
            g = g + refs[d][...]
        delta, nm, nv = _adamw_math(refs[8][...], g, refs[9][...], refs[10][...])
        refs[11][...] = g
        refs[12][...] = delta
        refs[13][...] = nm
        refs[14][...] = nv

    blk = pl.BlockSpec((tr, LANES), lambda i: (i, 0))
    in_specs = [pl.BlockSpec((tr, LANES), lambda i, d=d: (d * (M // tr) + i, 0)) for d in range(8)] + [blk] * 3
    return pl.pallas_call(
        body, name="small_update", grid=(M // tr,), in_specs=in_specs, out_specs=[blk] * 4,
        out_shape=[jax.ShapeDtypeStruct((M, LANES), F32)] * 4,
        compiler_params=pltpu.CompilerParams(dimension_semantics=("parallel",)),
    )(*([gall] * 8), w, m, v)


def _step(x, p, target, w, m, v):
    L = p.shape[0]
    x_i, y_i, c, s = _place()
    shapes = {}
    for n in BIG:
        _, K, N = w[n].shape
        shapes[n] = (4 * K, N) if n in ROW_SHARDED else (K, 4 * N)
    pieces = {}
    for n in BIG:
        half = w[n].shape[1] // 2
        pieces[n] = lax.dynamic_slice_in_dim(w[n], c * half, half, axis=1).astype(BF16)
    wf = _gather_weights(pieces, shapes)
    small = {n: w[n] for n in SMALL}
    loss_cell, dx, gw, gsmall = _local_step(x[0], p[:, 0], target[0], wf, small)
    loss = lax.psum(jnp.sum(loss_cell), ("x", "y", "c"))
    own, got = _exchange_halves(gw)
    chip_sum = {}
    for n in BIG:
        shp = own[n].shape
        C = shp[-1]
        R = math.prod(shp[:-1])
        (ps,) = _ew(f"chip_sum_{n}", lambda a, b: [a.astype(F32) + b.astype(F32)], [own[n].reshape(R, C), got[n].reshape(R, C)], [BF16], R, C)
        chip_sum[n] = ps.reshape(shp)
    parts = _scatter_chip_sums(chip_sum)
    ghalf = {}
    for n in BIG:
        shp = parts[n].shape[1:]
        C = shp[-1]
        R = math.prod(shp[:-1])
        flat = parts[n].reshape(4, R, C)
        (gh,) = _ew(f"shard_sum_{n}", lambda a, b, c_, d: [((a.astype(F32) + b.astype(F32)) + c_.astype(F32)) + d.astype(F32)],
                    [(flat, 0), (flat, 1), (flat, 2), (flat, 3)], [F32], R, C)
        ghalf[n] = gh.reshape(shp)
    gfull = _share_halves(ghalf)
    grad, delta, new_m, new_v = {}, {}, {}, {}
    for n in BIG:
        grad[n] = gfull[n].reshape(w[n].shape)
        delta[n], new_m[n], new_v[n] = _adamw(f"adamw_{n}", w[n], grad[n], m[n], v[n])
    gall = _gather_small(_pack_small(gsmall))
    gsum, dsm, nms, nvs = _small_update(gall, _pack_small(small), _pack_small({n: m[n] for n in SMALL}),
                                        _pack_small({n: v[n] for n in SMALL}))
    for dst, flat in ((grad, gsum), (delta, dsm), (new_m, nms), (new_v, nvs)):
        dst.update(_unpack_small(flat, small))
    return loss, dx[None], grad, delta, new_m, new_v


def kernel(x, p, w_in, w_br_attn, w_br_sg, w_out, sg_w, sg_b, sg_ln_g, sg_ln_b, norm_mix, norm_ffn, norm_ple, norm_final, w_ff_gate, w_ff_up, w_ff_down, w_ple_gate, w_ple, loss_target, m_w_in, m_w_br_attn, m_w_br_sg, m_w_out, m_sg_w, m_sg_b, m_sg_ln_g, m_sg_ln_b, m_norm_mix, m_norm_ffn, m_norm_ple, m_norm_final, m_w_ff_gate, m_w_ff_up, m_w_ff_down, m_w_ple_gate, m_w_ple, v_w_in, v_w_br_attn, v_w_br_sg, v_w_out, v_sg_w, v_sg_b, v_sg_ln_g, v_sg_ln_b, v_norm_mix, v_norm_ffn, v_norm_ple, v_norm_final, v_w_ff_gate, v_w_ff_up, v_w_ff_down, v_w_ple_gate, v_w_ple):
    w = dict(w_in=w_in, w_br_attn=w_br_attn, w_br_sg=w_br_sg, w_out=w_out, sg_w=sg_w, sg_b=sg_b, sg_ln_g=sg_ln_g, sg_ln_b=sg_ln_b,
             norm_mix=norm_mix, norm_ffn=norm_ffn, norm_ple=norm_ple, norm_final=norm_final, w_ff_gate=w_ff_gate, w_ff_up=w_ff_up,
             w_ff_down=w_ff_down, w_ple_gate=w_ple_gate, w_ple=w_ple)
    m = dict(w_in=m_w_in, w_br_attn=m_w_br_attn, w_br_sg=m_w_br_sg, w_out=m_w_out, sg_w=m_sg_w, sg_b=m_sg_b, sg_ln_g=m_sg_ln_g,
             sg_ln_b=m_sg_ln_b, norm_mix=m_norm_mix, norm_ffn=m_norm_ffn, norm_ple=m_norm_ple, norm_final=m_norm_final,
             w_ff_gate=m_w_ff_gate, w_ff_up=m_w_ff_up, w_ff_down=m_w_ff_down, w_ple_gate=m_w_ple_gate, w_ple=m_w_ple)
    v = dict(w_in=v_w_in, w_br_attn=v_w_br_attn, w_br_sg=v_w_br_sg, w_out=v_w_out, sg_w=v_sg_w, sg_b=v_sg_b, sg_ln_g=v_sg_ln_g,
             sg_ln_b=v_sg_ln_b, norm_mix=v_norm_mix, norm_ffn=v_norm_ffn, norm_ple=v_norm_ple, norm_final=v_norm_final,
             w_ff_gate=v_w_ff_gate, w_ff_up=v_w_ff_up, w_ff_down=v_w_ff_down, w_ple_gate=v_w_ple_gate, w_ple=v_w_ple)
    loss, grad_x, grad, delta, new_m, new_v = _step(x, p, loss_target, w, m, v)
    return (loss, grad_x, *[grad[n] for n in WEIGHTS], *[delta[n] for n in WEIGHTS], *[new_m[n] for n in WEIGHTS],
            *[new_v[n] for n in WEIGHTS])
```

```python
import functools
import math

import jax
import jax.numpy as jnp
from jax import lax
from jax.experimental import pallas as pl
from jax.experimental.pallas import tpu as pltpu

F32 = jnp.float32
BF16 = jnp.bfloat16
MESH = pl.DeviceIdType.MESH

HEAD_DIM = 128
ATTN_GROUPS = ((128, 1), (512, 4), (2048, 16))
N_GROUPS = 3
HEADS = 4
QKV_W = 3 * N_GROUPS * HEADS * HEAD_DIM
ATTN_W = HEADS * HEAD_DIM
SG_CHUNK = 128
SG_GROUPS = 8
SG_W = 1024
RADIUS = 64
ROPE_THETA = 10000.0
NORM_EPS = 1e-6
NEG_INF = -1e30
ADAM_LR, ADAM_B1, ADAM_B2, ADAM_EPS, ADAM_WD, ADAM_STEP = 0.001, 0.9, 0.999, 1e-08, 0.01, 10

VMEM_CAP_V7X = 56 * 1024 * 1024
LANES = 128
EW_TILE_ELEMS = 256 * 1024
MM_VMEM_BUDGET = 44 * 1024 * 1024

BIG = ("w_in", "w_br_attn", "w_br_sg", "w_out", "w_ff_gate", "w_ff_up", "w_ff_down", "w_ple_gate", "w_ple")
ROW_SHARDED = ("w_out", "w_ff_down", "w_ple_gate")
SMALL = ("sg_w", "sg_b", "sg_ln_g", "sg_ln_b", "norm_mix", "norm_ffn", "norm_ple", "norm_final")
WEIGHTS = ("w_in", "w_br_attn", "w_br_sg", "w_out", "sg_w", "sg_b", "sg_ln_g", "sg_ln_b", "norm_mix", "norm_ffn",
           "norm_ple", "norm_final", "w_ff_gate", "w_ff_up", "w_ff_down", "w_ple_gate", "w_ple")


def _pick(n, prefs):
    for t in prefs:
        if n % t == 0:
            return t
    return n


def _nbytes(shape, dtype):
    return math.prod(shape) * jnp.dtype(dtype).itemsize


def _vmem_limit(block_bytes, temp_bytes=0):
    est = 2 * block_bytes + temp_bytes
    assert est <= VMEM_CAP_V7X, est
    return VMEM_CAP_V7X


def _sigmoid(x):
    return 1.0 / (1.0 + jnp.exp(-x))


_GELU_C = math.sqrt(2.0 / math.pi)


def _gelu(x):
    return 0.5 * x * (1.0 + jnp.tanh(_GELU_C * (x + 0.044715 * (x * x * x))))


def _gelu_grad(x):
    t = jnp.tanh(_GELU_C * (x + 0.044715 * (x * x * x)))
    return 0.5 * (1.0 + t) + 0.5 * x * (1.0 - t * t) * (_GELU_C * (1.0 + 3.0 * 0.044715 * (x * x)))


def _lead(arr, l, blk, idx):
    if arr.ndim == 2:
        return pl.BlockSpec(blk, idx)
    return pl.BlockSpec((None,) + blk, lambda *g: (l,) + idx(*g))


def _k_steps(prods, tm, tn, fixed_bytes):
    for nk in range(1, 129):
        if any(p["K"] % nk or (p["K"] // nk) % LANES for p in prods):
            continue
        if 2 * sum((tm + tn) * (p["K"] // nk) * 2 for p in prods) + fixed_bytes <= MM_VMEM_BUDGET:
            return nk
    raise ValueError("no contraction split fits VMEM")


def _mm(name, prods, M, N, outs, epilogue, tiles=(), rows=(), tm=1024, tn=1024):
    assert M % tm == 0 and N % tn == 0, (name, M, N, tm, tn)
    fixed = 2 * tm * tn * (sum(t["x"].dtype.itemsize for t in tiles) + sum(jnp.dtype(o["dtype"]).itemsize for o in outs))
    fixed += (len(prods) + 2) * tm * tn * 4
    nk = _k_steps(prods, tm, tn, fixed)
    in_specs, args, block_bytes = [], [], 0
    for p in prods:
        K = p["K"]
        assert K % nk == 0, (name, K, nk)
        tk = K // nk
        p["tk"] = tk
        a_off, bk_off, bn_off = p.get("a_off", 0), p.get("bk_off", 0), p.get("bn_off", 0)
        assert bn_off % tn == 0 and bk_off % tk == 0
        if p["mode"] == "nn":
            assert a_off % tk == 0
            a_spec = _lead(p["a"], p.get("al"), (tm, tk), lambda i, j, k, o=a_off // tk: (i, o + k))
            b_spec = _lead(p["b"], p.get("bl"), (tk, tn), lambda i, j, k, ok=bk_off // tk, on=bn_off // tn: (ok + k, on + j))
        elif p["mode"] == "nt":
            assert a_off % tk == 0
            a_spec = _lead(p["a"], p.get("al"), (tm, tk), lambda i, j, k, o=a_off // tk: (i, o + k))
            b_spec = _lead(p["b"], p.get("bl"), (tn, tk), lambda i, j, k, ok=bk_off // tk, on=bn_off // tn: (on + j, ok + k))
        else:
            assert a_off % tm == 0
            a_spec = _lead(p["a"], p.get("al"), (tk, tm), lambda i, j, k, o=a_off // tm: (k, o + i))
            b_spec = _lead(p["b"], p.get("bl"), (tk, tn), lambda i, j, k, on=bn_off // tn: (k, on + j))
        in_specs += [a_spec, b_spec]
        args += [p["a"], p["b"]]
        block_bytes += (tm + tn) * tk * 2
    for t in tiles:
        off = t.get("off", 0)
        assert off % tn == 0
        in_specs.append(_lead(t["x"], t.get("l"), (tm, tn), lambda i, j, k, o=off // tn: (i, o + j)))
        args.append(t["x"])
        block_bytes += tm * tn * t["x"].dtype.itemsize
    for r in rows:
        in_specs.append(pl.BlockSpec((1, tn), lambda i, j, k: (0, j)))
        args.append(r)
    out_shapes, out_specs, aliases = [], [], {}
    for o_i, o in enumerate(outs):
        off = o.get("col_off", 0)
        assert off % tn == 0
        out_shapes.append(jax.ShapeDtypeStruct(o["shape"], o["dtype"]))
        idx = lambda i, j, k, oo=off // tn: (i, oo + j)
        if len(o["shape"]) == 2:
            out_specs.append(pl.BlockSpec((tm, tn), idx))
        else:
            out_specs.append(pl.BlockSpec((None, tm, tn), lambda i, j, k, l=o["l"], f=idx: (l,) + f(i, j, k)))
        if o.get("alias") is not None:
            aliases[len(args)] = o_i
            in_specs.append(pl.BlockSpec(memory_space=pl.ANY))
            args.append(o["alias"])
        block_bytes += tm * tn * jnp.dtype(o["dtype"]).itemsize
    n_p, n_t, n_r, n_o = len(prods), len(tiles), len(rows), len(outs)
    n_alias = len(aliases)
    modes = [p["mode"] for p in prods]

    def body(*refs):
        ab = refs[: 2 * n_p]
        t_refs = refs[2 * n_p: 2 * n_p + n_t]
        r_refs = refs[2 * n_p + n_t: 2 * n_p + n_t + n_r]
        o_refs = refs[2 * n_p + n_t + n_r + n_alias: 2 * n_p + n_t + n_r + n_alias + n_o]
        acc_refs = refs[2 * n_p + n_t + n_r + n_alias + n_o:]
        dims = {"nn": (((1,), (0,)), ((), ())), "nt": (((1,), (1,)), ((), ())), "tn": (((0,), (0,)), ((), ()))}

        def part(q):
            return lax.dot_general(ab[2 * q][...], ab[2 * q + 1][...], dims[modes[q]], preferred_element_type=F32)

        def finish(accs):
            res = epilogue(accs, [t[...] for t in t_refs], [r[...] for r in r_refs])
            for o_ref, val in zip(o_refs, res, strict=True):
                o_ref[...] = val.astype(o_ref.dtype)

        if nk == 1:
            finish([part(q) for q in range(n_p)])
        else:
            k = pl.program_id(2)

            @pl.when(k == 0)
            def _():
                for q, acc in enumerate(acc_refs):
                    acc[...] = part(q)

            @pl.when(k > 0)
            def _():
                for q, acc in enumerate(acc_refs):
                    acc[...] += part(q)

            @pl.when(k == nk - 1)
            def _():
                finish([acc[...] for acc in acc_refs])

    scratch = [pltpu.VMEM((tm, tn), F32) for _ in prods] if nk > 1 else []
    temp = (n_p + 2) * tm * tn * 4
    res = pl.pallas_call(
        body, name=name, grid=(M // tm, N // tn, nk), in_specs=in_specs, out_specs=out_specs, out_shape=out_shapes,
        scratch_shapes=scratch, input_output_aliases=aliases,
        compiler_params=pltpu.CompilerParams(dimension_semantics=("parallel", "parallel", "arbitrary"),
                                             vmem_limit_bytes=_vmem_limit(block_bytes, temp)),
    )(*args)
    return res


def _first(accs, tiles, rows):
    return [accs[0]]


def _ew(name, fn, ins, outs, R, C, tr=None, tc=None):
    tc = tc or _pick(C, (2048, 1536, 1408, 1024, 896, 512, 384, 256, 128))
    tr = tr or _pick(R, [t for t in (512, 256, 128, 64, 32, 16) if t * tc <= EW_TILE_ELEMS] + [8])
    in_specs, args, bb = [], [], 0
    for x in ins:
        if isinstance(x, tuple):
            arr, l = x
            in_specs.append(pl.BlockSpec((None, tr, tc), lambda i, j, l=l: (l, i, j)))
        else:
            arr = x
            in_specs.append(pl.BlockSpec((tr, tc), lambda i, j: (i, j)))
        args.append(arr)
        bb += tr * tc * arr.dtype.itemsize
    out_shapes = [jax.ShapeDtypeStruct((R, C), d) for d in outs]
    out_specs = [pl.BlockSpec((tr, tc), lambda i, j: (i, j)) for _ in outs]
    bb += sum(tr * tc * jnp.dtype(d).itemsize for d in outs)
    n_in = len(ins)

    def body(*refs):
        res = fn(*[r[...] for r in refs[:n_in]])
        for o_ref, val in zip(refs[n_in:], res, strict=True):
            o_ref[...] = val.astype(o_ref.dtype)

    return pl.pallas_call(
        body, name=name, grid=(R // tr, C // tc), in_specs=in_specs, out_specs=out_specs, out_shape=out_shapes,
        compiler_params=pltpu.CompilerParams(dimension_semantics=("parallel", "parallel"),
                                             vmem_limit_bytes=_vmem_limit(bb, 6 * tr * tc * 4)),
    )(*args)


def _rmsnorm_fwd(name, x, g):
    S, D = x.shape
    tr = _pick(S, (256, 128, 64, 8))

    def body(x_ref, g_ref, h_ref):
        xv = x_ref[...]
        r = lax.rsqrt(jnp.mean(xv * xv, axis=-1, keepdims=True) + NORM_EPS)
        h_ref[...] = (xv * r * g_ref[...]).astype(BF16)

    return pl.pallas_call(
        body, name=name, grid=(S // tr,),
        in_specs=[pl.BlockSpec((tr, D), lambda i: (i, 0)), pl.BlockSpec((1, D), lambda i: (0, 0))],
        out_specs=pl.BlockSpec((tr, D), lambda i: (i, 0)), out_shape=jax.ShapeDtypeStruct((S, D), BF16),
        compiler_params=pltpu.CompilerParams(dimension_semantics=("parallel",),
                                             vmem_limit_bytes=_vmem_limit(tr * D * 6, 3 * tr * D * 4)),
    )(x, g)


def _rmsnorm_bwd(name, x, g, dh, dres):
    S, D = x.shape
    tr = _pick(S, (256, 128, 64, 8))

    def body(x_ref, g_ref, dh_ref, dres_ref, dx_ref, dxb_ref, dg_ref):
        xv = x_ref[...]
        dy = dh_ref[...].astype(F32)
        r = lax.rsqrt(jnp.mean(xv * xv, axis=-1, keepdims=True) + NORM_EPS)
        a = dy * g_ref[...]
        dx = dres_ref[...] + r * a - xv * (r * r * r) * jnp.mean(a * xv, axis=-1, keepdims=True)
        dx_ref[...] = dx
        dxb_ref[...] = dx.astype(BF16)
        part = jnp.sum(dy * xv * r, axis=0, keepdims=True)

        @pl.when(pl.program_id(0) == 0)
        def _():
            dg_ref[...] = part

        @pl.when(pl.program_id(0) > 0)
        def _():
            dg_ref[...] += part

    row = pl.BlockSpec((tr, D), lambda i: (i, 0))
    vec = pl.BlockSpec((1, D), lambda i: (0, 0))
    return pl.pallas_call(
        body, name=name, grid=(S // tr,), in_specs=[row, vec, row, row], out_specs=[row, row, vec],
        out_shape=[jax.ShapeDtypeStruct((S, D), F32), jax.ShapeDtypeStruct((S, D), BF16), jax.ShapeDtypeStruct((1, D), F32)],
        compiler_params=pltpu.CompilerParams(dimension_semantics=("arbitrary",),
                                             vmem_limit_bytes=_vmem_limit(tr * D * 18, 5 * tr * D * 4)),
    )(x, g, dh, dres)


def _loss_head(x, g, target):
    S, D = x.shape
    tr = _pick(S, (256, 128, 64, 8))

    def body(x_ref, g_ref, t_ref, loss_ref, dx_ref, dxb_ref, dg_ref):
        xv = x_ref[...]
        r = lax.rsqrt(jnp.mean(xv * xv, axis=-1, keepdims=True) + NORM_EPS)
        xn = xv * r
        diff = xn * g_ref[...] - t_ref[...]
        dy = diff * (1.0 / D)
        a = dy * g_ref[...]
        dx = r * a - xv * (r * r * r) * jnp.mean(a * xv, axis=-1, keepdims=True)
        dx_ref[...] = dx
        dxb_ref[...] = dx.astype(BF16)
        part = jnp.sum(dy * xn, axis=0, keepdims=True)
        cell = (lax.broadcasted_iota(jnp.int32, (8, LANES), 0) == 0) & (lax.broadcasted_iota(jnp.int32, (8, LANES), 1) == 0)
        lpart = jnp.where(cell, 0.5 * jnp.sum(jnp.mean(diff * diff, axis=-1, keepdims=True)), 0.0)

        @pl.when(pl.program_id(0) == 0)
        def _():
            dg_ref[...] = part
            loss_ref[...] = lpart

        @pl.when(pl.program_id(0) > 0)
        def _():
            dg_ref[...] += part
            loss_ref[...] += lpart

    row = pl.BlockSpec((tr, D), lambda i: (i, 0))
    vec = pl.BlockSpec((1, D), lambda i: (0, 0))
    return pl.pallas_call(
        body, name="loss_head", grid=(S // tr,), in_specs=[row, vec, row],
        out_specs=[pl.BlockSpec((8, LANES), lambda i: (0, 0)), row, row, vec],
        out_shape=[jax.ShapeDtypeStruct((8, LANES), F32), jax.ShapeDtypeStruct((S, D), F32),
                   jax.ShapeDtypeStruct((S, D), BF16), jax.ShapeDtypeStruct((1, D), F32)],
        compiler_params=pltpu.CompilerParams(dimension_semantics=("arbitrary",),
                                             vmem_limit_bytes=_vmem_limit(tr * D * 14, 6 * tr * D * 4)),
    )(x, g, target)


def _rope_tables(S):
    pos = jnp.arange(S, dtype=F32)
    inv_freq = ROPE_THETA ** (-jnp.arange(0, HEAD_DIM, 2, dtype=F32) / HEAD_DIM)
    ang = pos[:, None] * inv_freq[None, :]
    cos, sin = jnp.cos(ang), jnp.sin(ang)
    return jnp.concatenate([cos, cos], axis=-1), jnp.concatenate([-sin, sin], axis=-1)


def _rope_fwd(name, z, cosf, sinf):
    S = z.shape[0]
    tr = _pick(S, (256, 128, 64, 8))
    n_rot = 2 * N_GROUPS * HEADS

    def body(z_ref, c_ref, s_ref, o_ref):
        c, s = c_ref[...], s_ref[...]
        for j in range(QKV_W // HEAD_DIM):
            t = z_ref[:, j * HEAD_DIM:(j + 1) * HEAD_DIM]
            if j < n_rot:
                t = t * c + pltpu.roll(t, HEAD_DIM // 2, axis=1) * s
            o_ref[:, j * HEAD_DIM:(j + 1) * HEAD_DIM] = t.astype(BF16)

    tab = pl.BlockSpec((tr, HEAD_DIM), lambda i: (i, 0))
    return pl.pallas_call(
        body, name=name, grid=(S // tr,), in_specs=[pl.BlockSpec((tr, QKV_W), lambda i: (i, 0)), tab, tab],
        out_specs=pl.BlockSpec((tr, QKV_W), lambda i: (i, 0)), out_shape=jax.ShapeDtypeStruct((S, QKV_W), BF16),
        compiler_params=pltpu.CompilerParams(dimension_semantics=("parallel",),
                                             vmem_limit_bytes=_vmem_limit(tr * QKV_W * 6, tr * QKV_W * 4)),
    )(z, cosf, sinf)


def _rope_bwd(name, dq, dk, dv, cosf, sinf, dz):
    S = dq.shape[0]
    tr = _pick(S, (256, 128, 64, 8))
    W3 = QKV_W // 3
    nh = W3 // HEAD_DIM

    def body(dq_ref, dk_ref, dv_ref, c_ref, s_ref, dz_in, o_ref):
        c, s = c_ref[...], s_ref[...]
        for part, ref in enumerate((dq_ref, dk_ref)):
            for j in range(nh):
                t = ref[:, j * HEAD_DIM:(j + 1) * HEAD_DIM].astype(F32)
                t = t * c - pltpu.roll(t, HEAD_DIM // 2, axis=1) * s
                o_ref[:, part * W3 + j * HEAD_DIM: part * W3 + (j + 1) * HEAD_DIM] = t.astype(BF16)
        o_ref[:, 2 * W3:] = dv_ref[...]

    third = pl.BlockSpec((tr, W3), lambda i: (i, 0))
    tab = pl.BlockSpec((tr, HEAD_DIM), lambda i: (i, 0))
    return pl.pallas_call(
        body, name=name, grid=(S // tr,),
        in_specs=[third, third, third, tab, tab, pl.BlockSpec(memory_space=pl.ANY)],
        out_specs=pl.BlockSpec((tr, QKV_W), lambda i: (i, 0)), out_shape=jax.ShapeDtypeStruct(dz.shape, dz.dtype),
        input_output_aliases={5: 0},
        compiler_params=pltpu.CompilerParams(dimension_semantics=("parallel",),
                                             vmem_limit_bytes=_vmem_limit(tr * QKV_W * 4, tr * QKV_W * 4)),
    )(dq, dk, dv, cosf, sinf, dz)


ATTN_TQ = 256


def _window(i0, d, S):
    W = min(S, ATTN_TQ + 2 * RADIUS * d)
    start = jnp.clip(i0 - RADIUS * d, 0, S - W)
    return W, pl.multiple_of(start, RADIUS)


def _band_mask(shape, q_axis, off, d):
    kq = lax.broadcasted_iota(jnp.int32, shape, 1 - q_axis) - lax.broadcasted_iota(jnp.int32, shape, q_axis) + off
    return (jnp.abs(kq) <= RADIUS * d) & ((kq & (d - 1)) == 0)


_NT = (((1,), (1,)), ((), ()))


def _attn_fwd(name, qkv):
    S = qkv.shape[0]
    T = ATTN_TQ
    scale = HEAD_DIM ** -0.5
    nq = N_GROUPS * HEADS

    def body(*refs):
        q_refs, k_refs, v_refs = refs[0:3], refs[3:6], refs[6:9]
        o_ref, lc_ref, lr_ref = refs[9:12]
        i0 = pl.program_id(1) * T
        m = jnp.full((T, 1), NEG_INF, F32)
        l = jnp.zeros((T, 1), F32)
        acc = jnp.zeros((T, HEAD_DIM), F32)
        for g, (_, d) in enumerate(ATTN_GROUPS):
            W, start = _window(i0, d, S)
            kw = k_refs[g][pl.ds(start, W), :]
            vw = v_refs[g][pl.ds(start, W), :]
            s = lax.dot_general(q_refs[g][...], kw, _NT, preferred_element_type=F32) * scale
            s = jnp.where(_band_mask((T, W), 0, start - i0, d), s, NEG_INF)
            m_new = jnp.maximum(m, jnp.max(s, axis=1, keepdims=True))
            alpha = jnp.exp(m - m_new)
            p = jnp.exp(s - m_new)
            l = l * alpha + jnp.sum(p, axis=1, keepdims=True)
            acc = acc * alpha + jnp.dot(p.astype(BF16), vw, preferred_element_type=F32)
            m = m_new
        o_ref[...] = (acc / l).astype(BF16)
        lse = m + jnp.log(l)
        lc_ref[...] = lse
        lr_ref[...] = jnp.broadcast_to(lse, (T, LANES)).T[0:1, :]

    in_specs = [pl.BlockSpec((T, HEAD_DIM), lambda h, i, g=g: (i, g * HEADS + h)) for g in range(N_GROUPS)]
    in_specs += [pl.BlockSpec((S, HEAD_DIM), lambda h, i, g=g: (0, nq + g * HEADS + h)) for g in range(N_GROUPS)]
    in_specs += [pl.BlockSpec((S, HEAD_DIM), lambda h, i, g=g: (0, 2 * nq + g * HEADS + h)) for g in range(N_GROUPS)]
    wmax = min(S, T + 2 * RADIUS * ATTN_GROUPS[-1][1])
    return pl.pallas_call(
        body, name=name, grid=(HEADS, S // T), in_specs=in_specs,
        out_specs=[pl.BlockSpec((T, HEAD_DIM), lambda h, i: (i, h)), pl.BlockSpec((None, T, 1), lambda h, i: (h, i, 0)),
                   pl.BlockSpec((None, 1, T), lambda h, i: (h, 0, i))],
        out_shape=[jax.ShapeDtypeStruct((S, ATTN_W), BF16), jax.ShapeDtypeStruct((HEADS, S, 1), F32),
                   jax.ShapeDtypeStruct((HEADS, 1, S), F32)],
        compiler_params=pltpu.CompilerParams(dimension_semantics=("parallel", "arbitrary"),
                                             vmem_limit_bytes=_vmem_limit(6 * S * HEAD_DIM * 2 + 8 * T * HEAD_DIM * 4, 5 * T * wmax * 4)),
    )(*([qkv] * 9))


def _attn_bwd(name, qkv, attn, dattn, lse_c, lse_r):
    S = qkv.shape[0]
    T = ATTN_TQ
    scale = HEAD_DIM ** -0.5
    nq = N_GROUPS * HEADS
    W3 = QKV_W // 3
    n_i = S // T

    def body(q_ref, k_ref, v_ref, o_ref, do_ref, lc_ref, lr_ref, dq_ref, dk_ref, dv_ref, dk_acc, dv_acc):
        g_id, i = pl.program_id(1), pl.program_id(2)
        i0 = i * T

        @pl.when(i == 0)
        def _():
            dk_acc[...] = jnp.zeros_like(dk_acc)
            dv_acc[...] = jnp.zeros_like(dv_acc)

        q, do = q_ref[...], do_ref[...]
        dof = do.astype(F32)
        delta_c = jnp.sum(dof * o_ref[...].astype(F32), axis=1, keepdims=True)
        delta_r = jnp.broadcast_to(delta_c, (T, LANES)).T[0:1, :]
        lse_col, lse_row = lc_ref[...], lr_ref[...]

        def group(d):
            W, start = _window(i0, d, S)
            kw = k_ref[pl.ds(start, W), :]
            vw = v_ref[pl.ds(start, W), :]
            s = lax.dot_general(q, kw, _NT, preferred_element_type=F32) * scale
            p = jnp.where(_band_mask((T, W), 0, start - i0, d), jnp.exp(s - lse_col), 0.0)
            dp = lax.dot_general(do, vw, _NT, preferred_element_type=F32)
            ds = p * (dp - delta_c)
            dq_ref[...] = (jnp.dot(ds.astype(BF16), kw, preferred_element_type=F32) * scale).astype(BF16)
            st = lax.dot_general(kw, q, _NT, preferred_element_type=F32) * scale
            pt = jnp.where(_band_mask((W, T), 1, start - i0, d), jnp.exp(st - lse_row), 0.0)
            dpt = lax.dot_general(vw, do, _NT, preferred_element_type=F32)
            dst = pt * (dpt - delta_r)
            dk_acc[pl.ds(start, W), :] += jnp.dot(dst.astype(BF16), q, preferred_element_type=F32) * scale
            dv_acc[pl.ds(start, W), :] += jnp.dot(pt.astype(BF16), do, preferred_element_type=F32)

        for g, (_, d) in enumerate(ATTN_GROUPS):
            pl.when(g_id == g)(functools.partial(group, d))

        @pl.when(i == n_i - 1)
        def _():
            dk_ref[...] = dk_acc[...].astype(BF16)
            dv_ref[...] = dv_acc[...].astype(BF16)

    tile = lambda off: pl.BlockSpec((T, HEAD_DIM), lambda h, g, i: (i, off + g * HEADS + h))
    full = lambda off: pl.BlockSpec((S, HEAD_DIM), lambda h, g, i: (0, off + g * HEADS + h))
    headt = pl.BlockSpec((T, HEAD_DIM), lambda h, g, i: (i, h))
    wmax = min(S, T + 2 * RADIUS * ATTN_GROUPS[-1][1])
    return pl.pallas_call(
        body, name=name, grid=(HEADS, N_GROUPS, n_i),
        in_specs=[tile(0), full(nq), full(2 * nq), headt, headt,
                  pl.BlockSpec((None, T, 1), lambda h, g, i: (h, i, 0)), pl.BlockSpec((None, 1, T), lambda h, g, i: (h, 0, i))],
        out_specs=[tile(0), full(0), full(0)],
        out_shape=[jax.ShapeDtypeStruct((S, W3), BF16)] * 3,
        scratch_shapes=[pltpu.VMEM((S, HEAD_DIM), F32), pltpu.VMEM((S, HEAD_DIM), F32)],
        compiler_params=pltpu.CompilerParams(dimension_semantics=("parallel", "arbitrary", "arbitrary"),
                                             vmem_limit_bytes=_vmem_limit(4 * S * HEAD_DIM * 2 + 8 * T * HEAD_DIM * 4,
                                                                          2 * S * HEAD_DIM * 4 + 8 * T * wmax * 4)),
    )(qkv, qkv, qkv, attn, dattn, lse_c, lse_r)


def _sg_parts(u, v, lng, lnb):
    gu = _gelu(u)
    gv = _gelu(v)
    mu = jnp.mean(gv, axis=-1, keepdims=True)
    xc = gv - mu
    rstd = lax.rsqrt(jnp.mean(xc * xc, axis=-1, keepdims=True) + NORM_EPS)
    xhat = xc * rstd
    vn = xhat * lng + lnb
    return gu, xhat, rstd, vn


def _sg_fwd(name, z, sg_w, sg_bc, lng, lnb, o_sg0):
    S = z.shape[0]
    T = SG_CHUNK
    cb = 512
    assert o_sg0 % cb == 0
    b0 = o_sg0 // cb

    def body(u0, u1, v0, v1, w_ref, b_ref, g_ref, be_ref, o_ref):
        u = jnp.concatenate([u0[...], u1[...]], axis=1)
        v = jnp.concatenate([v0[...], v1[...]], axis=1)
        gu, _, _, vn = _sg_parts(u, v, g_ref[...], be_ref[...])
        vnb = vn.astype(BF16)
        for g in range(SG_GROUPS):
            sl = slice(g * SG_CHUNK, (g + 1) * SG_CHUNK)
            mixed = jnp.dot(w_ref[g], vnb[:, sl], preferred_element_type=F32) + b_ref[g]
            o_ref[:, sl] = (gu[:, sl] * mixed).astype(BF16)

    zs = lambda k: pl.BlockSpec((T, cb), lambda i, k=k: (i, b0 + k))
    const3 = lambda shp: pl.BlockSpec(shp, lambda i: (0, 0, 0))
    vec = pl.BlockSpec((1, SG_W), lambda i: (0, 0))
    return pl.pallas_call(
        body, name=name, grid=(S // T,),
        in_specs=[zs(0), zs(1), zs(2), zs(3), const3((SG_GROUPS, SG_CHUNK, SG_CHUNK)), const3((SG_GROUPS, SG_CHUNK, 1)), vec, vec],
        out_specs=pl.BlockSpec((T, SG_W), lambda i: (i, 0)), out_shape=jax.ShapeDtypeStruct((S, SG_W), BF16),
        compiler_params=pltpu.CompilerParams(dimension_semantics=("parallel",), vmem_limit_bytes=_vmem_limit(4 * 1024 * 1024, 8 * T * SG_W * 4)),
    )(z, z, z, z, sg_w, sg_bc, lng, lnb)


def _sg_bwd(name, z, dsg, sg_w, sg_wt, sg_bc, lng, lnb, o_sg0, dz):
    S = z.shape[0]
    T = SG_CHUNK
    cb = 512
    b0 = o_sg0 // cb

    def body(u0, u1, v0, v1, d_ref, w_ref, wt_ref, b_ref, g_ref, be_ref, dz_in, dz_ref, dw_ref, db_ref, dg_ref, dbe_ref, stage):
        i, jj = pl.program_id(0), pl.program_id(1)

        @pl.when(jj == 0)
        def _():
            u = jnp.concatenate([u0[...], u1[...]], axis=1)
            v = jnp.concatenate([v0[...], v1[...]], axis=1)
            gu, xhat, rstd, vn = _sg_parts(u, v, g_ref[...], be_ref[...])
            vnb = vn.astype(BF16)
            dsg_v = d_ref[...].astype(F32)
            dmix = dsg_v * gu
            dmixb = dmix.astype(BF16)
            dvn_parts, mixed_parts, dw_parts, db_parts = [], [], [], []
            for g in range(SG_GROUPS):
                sl = slice(g * SG_CHUNK, (g + 1) * SG_CHUNK)
                mixed_parts.append(jnp.dot(w_ref[g], vnb[:, sl], preferred_element_type=F32) + b_ref[g])
                dvn_parts.append(jnp.dot(wt_ref[g], dmixb[:, sl], preferred_element_type=F32))
                dw_parts.append(lax.dot_general(dmixb[:, sl], vnb[:, sl], _NT, preferred_element_type=F32))
                db_parts.append(jnp.sum(dmix[:, sl], axis=1, keepdims=True))
            mixed = jnp.concatenate(mixed_parts, axis=1)
            dvn = jnp.concatenate(dvn_parts, axis=1)
            dzu = dsg_v * mixed * _gelu_grad(u)
            dxh = dvn * g_ref[...]
            dgv = rstd * (dxh - jnp.mean(dxh, axis=-1, keepdims=True) - xhat * jnp.mean(dxh * xhat, axis=-1, keepdims=True))
            dzv = dgv * _gelu_grad(v)
            stage[0] = dzu[:, :cb].astype(BF16)
            stage[1] = dzu[:, cb:].astype(BF16)
            stage[2] = dzv[:, :cb].astype(BF16)
            stage[3] = dzv[:, cb:].astype(BF16)
            dgp = jnp.sum(dvn * xhat, axis=0, keepdims=True)
            dbp = jnp.sum(dvn, axis=0, keepdims=True)

            @pl.when(i == 0)
            def _():
                for g in range(SG_GROUPS):
                    dw_ref[g] = dw_parts[g]
                    db_ref[g] = db_parts[g]
                dg_ref[...] = dgp
                dbe_ref[...] = dbp

            @pl.when(i > 0)
            def _():
                for g in range(SG_GROUPS):
                    dw_ref[g] += dw_parts[g]
                    db_ref[g] += db_parts[g]
                dg_ref[...] += dgp
                dbe_ref[...] += dbp

        dz_ref[...] = stage[jj]

    zs = lambda k: pl.BlockSpec((T, cb), lambda i, jj, k=k: (i, b0 + k))
    const3 = lambda shp: pl.BlockSpec(shp, lambda i, jj: (0, 0, 0))
    vec = pl.BlockSpec((1, SG_W), lambda i, jj: (0, 0))
    return pl.pallas_call(
        body, name=name, grid=(S // T, 4),
        in_specs=[zs(0), zs(1), zs(2), zs(3), pl.BlockSpec((T, SG_W), lambda i, jj: (i, 0)),
                  const3((SG_GROUPS, SG_CHUNK, SG_CHUNK)), const3((SG_GROUPS, SG_CHUNK, SG_CHUNK)), const3((SG_GROUPS, SG_CHUNK, 1)),
                  vec, vec, pl.BlockSpec(memory_space=pl.ANY)],
        out_specs=[pl.BlockSpec((T, cb), lambda i, jj: (i, b0 + jj)), const3((SG_GROUPS, SG_CHUNK, SG_CHUNK)),
                   const3((SG_GROUPS, SG_CHUNK, 1)), vec, vec],
        out_shape=[jax.ShapeDtypeStruct(dz.shape, dz.dtype), jax.ShapeDtypeStruct((SG_GROUPS, SG_CHUNK, SG_CHUNK), F32),
                   jax.ShapeDtypeStruct((SG_GROUPS, SG_CHUNK, 1), F32), jax.ShapeDtypeStruct((1, SG_W), F32),
                   jax.ShapeDtypeStruct((1, SG_W), F32)],
        scratch_shapes=[pltpu.VMEM((4, T, cb), BF16)],
        input_output_aliases={10: 0},
        compiler_params=pltpu.CompilerParams(dimension_semantics=("arbitrary", "arbitrary"),
                                             vmem_limit_bytes=_vmem_limit(6 * 1024 * 1024, 16 * T * SG_W * 4)),
    )(z, z, z, z, dsg, sg_w, sg_wt, sg_bc, lng, lnb, dz)


def _gate_bwd(name, z, dmerged, y_attn, y_sg, o_g0, in_w):
    S, D = dmerged.shape
    tr = _pick(S, (512, 256, 128, 8))
    cb = _pick(D, (512, 256, 128))
    assert o_g0 % cb == 0
    nd = D // cb
    b0 = o_g0 // cb

    def body(z_ref, dm_ref, ya_ref, ys_ref, dz_ref, dy_ref):
        jj = pl.program_id(1)
        gate = _sigmoid(z_ref[...])
        dm = dm_ref[...].astype(F32)
        y = jnp.where(jj < nd, ya_ref[...], ys_ref[...]).astype(F32)
        dz_ref[...] = (dm * y * gate * (1.0 - gate)).astype(BF16)
        dy_ref[...] = (dm * gate).astype(BF16)

    half = pl.BlockSpec((tr, cb), lambda i, jj: (i, jj % nd))
    return pl.pallas_call(
        body, name=name, grid=(S // tr, 2 * nd),
        in_specs=[pl.BlockSpec((tr, cb), lambda i, jj: (i, b0 + jj)), half, half, half],
        out_specs=[pl.BlockSpec((tr, cb), lambda i, jj: (i, b0 + jj)), pl.BlockSpec((tr, cb), lambda i, jj: (i, jj))],
        out_shape=[jax.ShapeDtypeStruct((S, in_w), BF16), jax.ShapeDtypeStruct((S, 2 * D), BF16)],
        compiler_params=pltpu.CompilerParams(dimension_semantics=("parallel", "arbitrary"),
                                             vmem_limit_bytes=_vmem_limit(tr * cb * 14, 6 * tr * cb * 4)),
    )(z, dmerged, y_attn, y_sg)


def _row(v):
    return v.reshape(1, -1)


def _local_step(x, p, target, wf, small):
    S, D = x.shape
    L = p.shape[0]
    in_w = wf["w_in"].shape[2]
    ff = wf["w_ff_gate"].shape[2]
    ple = p.shape[2]
    o_sg0, o_g0 = QKV_W, QKV_W + 2 * SG_W
    cosf, sinf = _rope_tables(S)
    pb = p.astype(BF16)
    tmb = _pick(S, (1024, 512, 256))
    tn_in = _pick(in_w, (768, 1024, 512))
    tn_d = _pick(D, (1024, 512, 256))
    tn_g = _pick(D, (512, 256))
    tn_ff = _pick(ff, (512, 256))

    saved = []
    xs = x
    for i in range(L):
        sv = {"x0": xs}
        h = _rmsnorm_fwd(f"norm_mix_{i}", xs, _row(small["norm_mix"][i]))
        (z,) = _mm(f"in_proj_{i}", [dict(a=h, b=wf["w_in"], bl=i, mode="nn", K=D)], S, in_w,
                   [dict(shape=(S, in_w), dtype=F32)], _first, tm=tmb, tn=tn_in)
        qkv = _rope_fwd(f"rope_{i}", z, cosf, sinf)
        attn, lse_c, lse_r = _attn_fwd(f"attn_{i}", qkv)
        sgw = small["sg_w"][i].astype(BF16)
        sgbc = small["sg_b"][i].reshape(SG_GROUPS, SG_CHUNK, 1)
        sg = _sg_fwd(f"sgu_{i}", z, sgw, sgbc, _row(small["sg_ln_g"][i]), _row(small["sg_ln_b"][i]), o_sg0)

        def merge(accs, tiles, rows):
            ya, ys = accs[0].astype(BF16), accs[1].astype(BF16)
            g0, g1 = _sigmoid(tiles[0]), _sigmoid(tiles[1])
            return [ya, ys, g0 * ya.astype(F32) + g1 * ys.astype(F32)]

        y_attn, y_sg, merged = _mm(
            f"branches_{i}",
            [dict(a=attn, b=wf["w_br_attn"], bl=i, mode="nn", K=ATTN_W), dict(a=sg, b=wf["w_br_sg"], bl=i, mode="nn", K=SG_W)],
            S, D, [dict(shape=(S, D), dtype=BF16)] * 3, merge,
            tiles=[dict(x=z, off=o_g0), dict(x=z, off=o_g0 + D)], tm=tmb, tn=tn_g)
        (x1,) = _mm(f"out_proj_{i}", [dict(a=merged, b=wf["w_out"], bl=i, mode="nn", K=D)], S, D,
                    [dict(shape=(S, D), dtype=F32)], lambda a, t, r: [t[0] + a[0]], tiles=[dict(x=xs)], tm=tmb, tn=tn_d)
        h2 = _rmsnorm_fwd(f"norm_ffn_{i}", x1, _row(small["norm_ffn"][i]))

        def swiglu(accs, tiles, rows):
            fg = accs[0].astype(BF16).astype(F32)
            fu = accs[1].astype(BF16).astype(F32)
            return [fg, fu, fg * _sigmoid(fg) * fu]

        ffg, ffu, act = _mm(
            f"ff_in_{i}",
            [dict(a=h2, b=wf["w_ff_gate"], bl=i, mode="nn", K=D), dict(a=h2, b=wf["w_ff_up"], bl=i, mode="nn", K=D)],
            S, ff, [dict(shape=(S, ff), dtype=BF16)] * 3, swiglu, tm=tmb, tn=tn_ff)
        (x2,) = _mm(f"ff_out_{i}", [dict(a=act, b=wf["w_ff_down"], bl=i, mode="nn", K=ff)], S, D,
                    [dict(shape=(S, D), dtype=F32)], lambda a, t, r: [t[0] + a[0]], tiles=[dict(x=x1)], tm=tmb, tn=tn_d)
        h3 = _rmsnorm_fwd(f"norm_ple_{i}", x2, _row(small["norm_ple"][i]))

        def ple_mix(accs, tiles, rows):
            gp = _sigmoid(accs[0]).astype(BF16)
            pe = accs[1].astype(BF16)
            return [tiles[0] + gp.astype(F32) * pe.astype(F32), gp, pe]

        x3, gp, pe = _mm(
            f"ple_{i}",
            [dict(a=h3, b=wf["w_ple_gate"], bl=i, mode="nn", K=D), dict(a=pb, al=i, b=wf["w_ple"], bl=i, mode="nn", K=ple)],
            S, D, [dict(shape=(S, D), dtype=F32), dict(shape=(S, D), dtype=BF16), dict(shape=(S, D), dtype=BF16)], ple_mix,
            tiles=[dict(x=x2)], tm=tmb, tn=tn_g)
        sv.update(h=h, z=z, qkv=qkv, attn=attn, lse_c=lse_c, lse_r=lse_r, sg=sg, y_attn=y_attn, y_sg=y_sg, merged=merged,
                  x1=x1, h2=h2, ffg=ffg, ffu=ffu, act=act, x2=x2, h3=h3, gp=gp, pe=pe, sgw=sgw, sgbc=sgbc)
        saved.append(sv)
        xs = x3

    loss_cell, dx, dxb, dg_final = _loss_head(xs, _row(small["norm_final"]), target)

    gw = {n: None for n in BIG}
    gs = {n: [None] * L for n in SMALL if n != "norm_final"}

    def dw(n, i, a, a_off, b, bn_off, K_rows, N_cols, tm, tn):
        (gw[n],) = _mm(f"d_{n}_{i}", [dict(a=a, b=b, mode="tn", K=S, a_off=a_off, bn_off=bn_off)], K_rows, N_cols,
                       [dict(shape=(L, K_rows, N_cols), dtype=BF16, l=i, alias=gw[n])], _first, tm=tm, tn=tn)

    for i in reversed(range(L)):
        sv = saved[i]
        dpre, dpe = _ew(f"ple_gate_bwd_{i}",
                        lambda d, g, e: [d * e.astype(F32) * g.astype(F32) * (1.0 - g.astype(F32)), d * g.astype(F32)],
                        [dx, sv["gp"], sv["pe"]], [BF16, BF16], S, D)
        (dh3,) = _mm(f"d_h3_{i}", [dict(a=dpre, b=wf["w_ple_gate"], bl=i, mode="nt", K=D)], S, D,
                     [dict(shape=(S, D), dtype=F32)], _first, tm=tmb, tn=tn_d)
        dw("w_ple_gate", i, sv["h3"], 0, dpre, 0, D, D, tn_d, tn_d)
        dw("w_ple", i, pb[i], 0, dpe, 0, ple, D, _pick(ple, (256, 128)), _pick(D, (2048, 1024, 512, 256)))
        dx, dxb, gs["norm_ple"][i] = _rmsnorm_bwd(f"norm_ple_bwd_{i}", sv["x2"], _row(small["norm_ple"][i]), dh3, dx)
        def swiglu_bwd(accs, tiles, rows):
            da = accs[0].astype(BF16).astype(F32)
            fg, fu = tiles[0].astype(F32), tiles[1].astype(F32)
            sg_ = _sigmoid(fg)
            return [da * fu * (sg_ * (1.0 + fg * (1.0 - sg_))), da * (fg * sg_)]

        dffg, dffu = _mm(f"d_act_{i}", [dict(a=dxb, b=wf["w_ff_down"], bl=i, mode="nt", K=D)], S, ff,
                         [dict(shape=(S, ff), dtype=BF16)] * 2, swiglu_bwd, tiles=[dict(x=sv["ffg"]), dict(x=sv["ffu"])],
                         tm=tmb, tn=tn_ff)
        dw("w_ff_down", i, sv["act"], 0, dxb, 0, ff, D, tn_ff, _pick(D, (2048, 1024, 512, 256)))
        (dh2,) = _mm(f"d_h2_{i}", [dict(a=dffg, b=wf["w_ff_gate"], bl=i, mode="nt", K=ff),
                                   dict(a=dffu, b=wf["w_ff_up"], bl=i, mode="nt", K=ff)], S, D,
                     [dict(shape=(S, D), dtype=F32)], lambda a, t, r: [a[0] + a[1]], tm=tmb, tn=tn_d)
        dw("w_ff_gate", i, sv["h2"], 0, dffg, 0, D, ff, _pick(D, (2048, 1024, 512, 256)), tn_ff)
        dw("w_ff_up", i, sv["h2"], 0, dffu, 0, D, ff, _pick(D, (2048, 1024, 512, 256)), tn_ff)
        dx, dxb, gs["norm_ffn"][i] = _rmsnorm_bwd(f"norm_ffn_bwd_{i}", sv["x1"], _row(small["norm_ffn"][i]), dh2, dx)
        (dmerged,) = _mm(f"d_merged_{i}", [dict(a=dxb, b=wf["w_out"], bl=i, mode="nt", K=D)], S, D,
                         [dict(shape=(S, D), dtype=BF16)], _first, tm=tmb, tn=tn_d)
        dw("w_out", i, sv["merged"], 0, dxb, 0, D, D, tn_d, tn_d)
        dz, dy = _gate_bwd(f"gate_bwd_{i}", sv["z"], dmerged, sv["y_attn"], sv["y_sg"], o_g0, in_w)
        (dattn,) = _mm(f"d_attn_{i}", [dict(a=dy, b=wf["w_br_attn"], bl=i, mode="nt", K=D)], S, ATTN_W,
                       [dict(shape=(S, ATTN_W), dtype=BF16)], _first, tm=tmb, tn=ATTN_W)
        (dsg,) = _mm(f"d_sg_{i}", [dict(a=dy, a_off=D, b=wf["w_br_sg"], bl=i, mode="nt", K=D)], S, SG_W,
                     [dict(shape=(S, SG_W), dtype=BF16)], _first, tm=tmb, tn=SG_W)
        dw("w_br_attn", i, sv["attn"], 0, dy, 0, ATTN_W, D, ATTN_W, _pick(D, (2048, 1024, 512, 256)))
        dw("w_br_sg", i, sv["sg"], 0, dy, D, SG_W, D, SG_W, _pick(D, (1024, 512, 256)))
        sgwt = jnp.swapaxes(small["sg_w"][i], 1, 2).astype(BF16)
        dz, gs["sg_w"][i], dsgb, dlg, dlb = _sg_bwd(f"sgu_bwd_{i}", sv["z"], dsg, sv["sgw"], sgwt, sv["sgbc"],
                                                    _row(small["sg_ln_g"][i]), _row(small["sg_ln_b"][i]), o_sg0, dz)
        gs["sg_b"][i], gs["sg_ln_g"][i], gs["sg_ln_b"][i] = dsgb.reshape(SG_GROUPS, SG_CHUNK), dlg[0], dlb[0]
        dq, dk, dv = _attn_bwd(f"attn_bwd_{i}", sv["qkv"], sv["attn"], dattn, sv["lse_c"], sv["lse_r"])
        dz = _rope_bwd(f"rope_bwd_{i}", dq, dk, dv, cosf, sinf, dz)
        (dh,) = _mm(f"d_h_{i}", [dict(a=dz, b=wf["w_in"], bl=i, mode="nt", K=in_w)], S, D,
                    [dict(shape=(S, D), dtype=F32)], _first, tm=tmb, tn=tn_d)
        dw("w_in", i, sv["h"], 0, dz, 0, D, in_w, tn_d, tn_in)
        dx, dxb, gs["norm_mix"][i] = _rmsnorm_bwd(f"norm_mix_bwd_{i}", sv["x0"], _row(small["norm_mix"][i]), dh, dx)

    gsmall = {n: jnp.stack([jnp.reshape(v, small[n].shape[1:]) for v in gs[n]]) for n in gs}
    gsmall["norm_final"] = dg_final[0]
    return loss_cell, dx, gw, gsmall


def _place():
    x, y, c = lax.axis_index("x"), lax.axis_index("y"), lax.axis_index("c")
    return x, y, c, 2 * x + y


def _chip_of(s):
    return s // 2, s % 2


def _aligned(v, m):
    return v if isinstance(v, int) else pl.multiple_of(v, m)


def _piece(name, shape, s, c):
    K, N = shape
    if name in ROW_SHARDED:
        ks = K // 4
        return s * ks + c * (ks // 2), ks // 2, 0, N
    ns = N // 4
    return c * (K // 2), K // 2, s * ns, ns


def _gather_weights(pieces, shapes):
    names = list(pieces)
    n_w = len(names)

    def body(*refs):
        src = refs[:n_w]
        dst = refs[n_w:2 * n_w]
        send_sems, recv_sems, local_sems = refs[2 * n_w:]
        x, y, c, s = _place()
        sib = (x, y, 1 - c)
        rel = [1, 2, 3]

        def where(w, ps, pc):
            r0, nr, c0, nc = _piece(names[w], shapes[names[w]], ps, pc)
            return dst[w].at[:, pl.ds(_aligned(r0, 16), nr), pl.ds(_aligned(c0, LANES), nc)]

        def copy(w, k, ps, pc, to, from_src=False):
            return pltpu.make_async_remote_copy(
                src_ref=src[w] if from_src else where(w, ps, pc), dst_ref=where(w, ps, pc),
                send_sem=send_sems.at[w, k], recv_sem=recv_sems.at[w, k], device_id=to, device_id_type=MESH)

        mine, first, passed = [], [], []
        for w in range(n_w):
            cp = pltpu.make_async_copy(src[w], where(w, s, c), local_sems.at[w])
            cp.start()
            mine.append(cp)
            first.append(copy(w, 0, s, c, sib, from_src=True))
            for j in rel:
                first.append(copy(w, j, s, c, (*_chip_of(s ^ j), c), from_src=True))
        for cp in first:
            cp.start()
        for w in range(n_w):
            for j in rel:
                copy(w, j, s ^ j, c, sib).wait_recv()
                fw = copy(w, 3 + j, s ^ j, c, sib)
                fw.start()
                passed.append(fw)
        for w in range(n_w):
            copy(w, 0, s, 1 - c, sib).wait_recv()
            for j in rel:
                copy(w, 3 + j, s ^ j, 1 - c, sib).wait_recv()
        for cp in first + passed:
            cp.wait_send()
        for cp in mine:
            cp.wait()

    anyspec = pl.BlockSpec(memory_space=pl.ANY)
    out = pl.pallas_call(
        body, name="gather_weights", in_specs=[anyspec] * n_w, out_specs=[anyspec] * n_w,
        out_shape=[jax.ShapeDtypeStruct((pieces[n].shape[0],) + tuple(shapes[n]), BF16) for n in names],
        scratch_shapes=[pltpu.SemaphoreType.DMA((n_w, 7)), pltpu.SemaphoreType.DMA((n_w, 7)), pltpu.SemaphoreType.DMA((n_w,))],
    )(*[pieces[n] for n in names])
    return dict(zip(names, out))


def _halves_view(name, g):
    L, K, N = g.shape
    if name in ROW_SHARDED:
        return g.reshape(L * 4, 2, K // 8, N)
    return g.reshape(L, 2, K // 2, N)


def _exchange_halves(views):
    names = list(views)
    n_w = len(names)

    def body(*refs):
        src = refs[:n_w]
        got = refs[n_w:2 * n_w]
        send_sems, recv_sems = refs[2 * n_w:]
        x, y, c, s = _place()
        remote = [pltpu.make_async_remote_copy(src_ref=src[w].at[:, 1 - c], dst_ref=got[w], send_sem=send_sems.at[w],
                                               recv_sem=recv_sems.at[w], device_id=(x, y, 1 - c), device_id_type=MESH)
                  for w in range(n_w)]
        for cp in remote:
            cp.start()
        for cp in remote:
            cp.wait()

    anyspec = pl.BlockSpec(memory_space=pl.ANY)
    out = pl.pallas_call(
        body, name="exchange_halves", in_specs=[anyspec] * n_w, out_specs=[anyspec] * n_w,
        out_shape=[jax.ShapeDtypeStruct((v.shape[0],) + v.shape[2:], BF16) for v in views.values()],
        scratch_shapes=[pltpu.SemaphoreType.DMA((n_w,)), pltpu.SemaphoreType.DMA((n_w,))],
    )(*views.values())
    return dict(zip(names, out))


def _chip_sum(name, view, got, place):
    A, _, R, C = view.shape
    tc = _pick(C, (2048, 1536, 1408, 1024, 512, 256, 128))
    tr = _pick(R, [t for t in (512, 256, 128, 64, 32, 16) if t * tc <= EW_TILE_ELEMS] + [8])

    def body(p_ref, own_ref, got_ref, o_ref):
        o_ref[...] = (own_ref[...].astype(F32) + got_ref[...].astype(F32)).astype(BF16)

    flat = pl.BlockSpec((None, tr, tc), lambda a, i, j, p: (a, i, j))
    return pl.pallas_call(
        body, name=name, out_shape=jax.ShapeDtypeStruct((A, R, C), BF16),
        grid_spec=pltpu.PrefetchScalarGridSpec(
            num_scalar_prefetch=1, grid=(A, R // tr, C // tc),
            in_specs=[pl.BlockSpec((None, None, tr, tc), lambda a, i, j, p: (a, p[0], i, j)), flat], out_specs=flat),
        compiler_params=pltpu.CompilerParams(dimension_semantics=("parallel", "parallel", "parallel"),
                                             vmem_limit_bytes=_vmem_limit(6 * tr * tc, 3 * tr * tc * 4)),
    )(place, view, got)


def _shard_view(name, ps, L):
    return ps.reshape(L, 4, *ps.shape[1:]) if name in ROW_SHARDED else ps


def _scatter_chip_sums(psum):
    names = list(psum)
    n_w = len(names)

    def shard_shape(n):
        v = psum[n].shape
        return (v[0], v[2], v[3]) if n in ROW_SHARDED else (v[0], v[1], v[2] // 4)

    def body(*refs):
        src = refs[:n_w]
        dst = refs[n_w:2 * n_w]
        send_sems, recv_sems = refs[2 * n_w:]
        x, y, c, s = _place()

        def shard(w, t):
            if names[w] in ROW_SHARDED:
                return src[w].at[:, t]
            ns = src[w].shape[2] // 4
            return src[w].at[:, :, pl.ds(pl.multiple_of(t * ns, LANES), ns)]

        remote = []
        for w in range(n_w):
            for j in (1, 2, 3):
                remote.append(pltpu.make_async_remote_copy(
                    src_ref=shard(w, s ^ j), dst_ref=dst[w].at[j - 1], send_sem=send_sems.at[w, j - 1],
                    recv_sem=recv_sems.at[w, j - 1], device_id=(*_chip_of(s ^ j), c), device_id_type=MESH))
        for cp in remote:
            cp.start()
        for cp in remote:
            cp.wait()

    anyspec = pl.BlockSpec(memory_space=pl.ANY)
    out = pl.pallas_call(
        body, name="scatter_chip_sums", in_specs=[anyspec] * n_w, out_specs=[anyspec] * n_w,
        out_shape=[jax.ShapeDtypeStruct((3,) + shard_shape(n), BF16) for n in names],
        scratch_shapes=[pltpu.SemaphoreType.DMA((n_w, 3)), pltpu.SemaphoreType.DMA((n_w, 3))],
    )(*[psum[n] for n in names])
    return dict(zip(names, out))


def _shard_sum(name, ps, parts, place, row_sharded):
    _, L, R, C = parts.shape
    tc = _pick(C, (2048, 1408, 1024, 896, 512, 384, 256, 128))
    tr = _pick(R, [t for t in (512, 256, 128, 64, 32, 16) if t * tc <= EW_TILE_ELEMS] + [8])

    def body(p_ref, own_ref, a_ref, b_ref, c_ref, o_ref):
        o_ref[...] = ((own_ref[...].astype(F32) + a_ref[...].astype(F32)) + b_ref[...].astype(F32)) + c_ref[...].astype(F32)

    if row_sharded:
        own_spec = pl.BlockSpec((None, None, tr, tc), lambda l, i, j, p: (l, p[1], i, j))
    else:
        own_spec = pl.BlockSpec((None, tr, tc), lambda l, i, j, p: (l, i, p[1] * (C // tc) + j))
    part = lambda k: pl.BlockSpec((None, None, tr, tc), lambda l, i, j, p, k=k: (k, l, i, j))
    return pl.pallas_call(
        body, name=name, out_shape=jax.ShapeDtypeStruct((L, 2, R, C), F32),
        grid_spec=pltpu.PrefetchScalarGridSpec(
            num_scalar_prefetch=1, grid=(L, R // tr, C // tc), in_specs=[own_spec, part(0), part(1), part(2)],
            out_specs=pl.BlockSpec((None, None, tr, tc), lambda l, i, j, p: (l, p[0], i, j))),
        compiler_params=pltpu.CompilerParams(dimension_semantics=("parallel", "parallel", "parallel"),
                                             vmem_limit_bytes=_vmem_limit(12 * tr * tc, 5 * tr * tc * 4)),
    )(place, ps, parts, parts, parts)


def _share_halves(ghalf):
    names = list(ghalf)
    n_w = len(names)

    def body(*refs):
        src = refs[:n_w]
        dst = refs[n_w:2 * n_w]
        send_sems, recv_sems = refs[2 * n_w:]
        x, y, c, s = _place()
        remote = [pltpu.make_async_remote_copy(src_ref=src[w].at[:, c], dst_ref=dst[w].at[:, c], send_sem=send_sems.at[w],
                                               recv_sem=recv_sems.at[w], device_id=(x, y, 1 - c), device_id_type=MESH)
                  for w in range(n_w)]
        for cp in remote:
            cp.start()
        for cp in remote:
            cp.wait()

    anyspec = pl.BlockSpec(memory_space=pl.ANY)
    out = pl.pallas_call(
        body, name="share_halves", in_specs=[anyspec] * n_w, out_specs=[anyspec] * n_w,
        out_shape=[jax.ShapeDtypeStruct(ghalf[n].shape, F32) for n in names],
        input_output_aliases={w: w for w in range(n_w)},
        scratch_shapes=[pltpu.SemaphoreType.DMA((n_w,)), pltpu.SemaphoreType.DMA((n_w,))],
    )(*[ghalf[n] for n in names])
    return dict(zip(names, out))


def _gather_small(v):
    m_per, n = v.shape

    def body(x_ref, out_ref, send_sems, recv_sems, local_sem):
        x, y, c, s = _place()
        me, sibling = (x, y, c), (x, y, 1 - c)
        chips = [(1 - x, y), (x, 1 - y), (1 - x, 1 - y)]

        def rows(px, py, pc):
            return out_ref.at[pl.ds(pl.multiple_of((4 * px + 2 * py + pc) * m_per, 8), m_per), :]

        def copy(k, block, to, src=None):
            return pltpu.make_async_remote_copy(src_ref=rows(*block) if src is None else src, dst_ref=rows(*block),
                                                send_sem=send_sems.at[k], recv_sem=recv_sems.at[k], device_id=to, device_id_type=MESH)

        mine = pltpu.make_async_copy(x_ref, rows(*me), local_sem)
        mine.start()
        first = [copy(0, me, sibling, src=x_ref)]
        first += [copy(1 + j, me, (*chip, c), src=x_ref) for j, chip in enumerate(chips)]
        for cp in first:
            cp.start()
        passed = [copy(4 + j, (*chip, c), sibling) for j, chip in enumerate(chips)]
        for j, chip in enumerate(chips):
            copy(1 + j, (*chip, c), me).wait_recv()
            passed[j].start()
        copy(0, sibling, me).wait_recv()
        for j, chip in enumerate(chips):
            copy(4 + j, (*chip, 1 - c), me).wait_recv()
        for cp in first + passed:
            cp.wait_send()
        mine.wait()

    return pl.pallas_call(
        body, name="gather_small", out_shape=jax.ShapeDtypeStruct((8 * m_per, n), v.dtype),
        in_specs=[pl.BlockSpec(memory_space=pltpu.VMEM)], out_specs=pl.BlockSpec(memory_space=pltpu.VMEM),
        scratch_shapes=[pltpu.SemaphoreType.DMA((7,)), pltpu.SemaphoreType.DMA((7,)), pltpu.SemaphoreType.DMA],
        compiler_params=pltpu.CompilerParams(vmem_limit_bytes=_vmem_limit(9 * m_per * n * 4)),
    )(v)


def _adamw_math(w, g, m, v):
    m = ADAM_B1 * m + (1.0 - ADAM_B1) * g
    v = ADAM_B2 * v + (1.0 - ADAM_B2) * (g * g)
    m_hat = m / (1.0 - ADAM_B1 ** ADAM_STEP)
    v_hat = v / (1.0 - ADAM_B2 ** ADAM_STEP)
    delta = -ADAM_LR * (m_hat / (jnp.sqrt(v_hat) + ADAM_EPS) + ADAM_WD * w)
    return delta, m, v


def _adamw(name, w, g, m, v):
    shape = w.shape
    C = shape[-1]
    R = math.prod(shape[:-1])
    f = lambda a: a.reshape(R, C)
    delta, nm, nv = _ew(name, lambda w_, g_, m_, v_: list(_adamw_math(w_, g_, m_, v_)), [f(w), f(g), f(m), f(v)], [F32] * 3, R, C)
    return delta.reshape(shape), nm.reshape(shape), nv.reshape(shape)


def _pack_small(d):
    return jnp.concatenate([d[n].reshape(-1, LANES) for n in SMALL], axis=0)


def _unpack_small(flat, like):
    out, r = {}, 0
    for n in SMALL:
        k = like[n].size // LANES
        out[n] = flat[r:r + k].reshape(like[n].shape)
        r += k
    return out


def _small_update(gall, w, m, v):
    M = w.shape[0]
    tr = _pick(M, (552, 276, 184, 96, 48, 24, 8))

    def body(*refs):
        g = refs[0][...]
        for d in range(1, 8):
            g = g + refs[d][...]
        delta, nm, nv = _adamw_math(refs[8][...], g, refs[9][...], refs[10][...])
        refs[11][...] = g
        refs[12][...] = delta
        refs[13][...] = nm
        refs[14][...] = nv

    blk = pl.BlockSpec((tr, LANES), lambda i: (i, 0))
    in_specs = [pl.BlockSpec((tr, LANES), lambda i, d=d: (d * (M // tr) + i, 0)) for d in range(8)] + [blk] * 3
    return pl.pallas_call(
        body, name="small_update", grid=(M // tr,), in_specs=in_specs, out_specs=[blk] * 4,
        out_shape=[jax.ShapeDtypeStruct((M, LANES), F32)] * 4,
        compiler_params=pltpu.CompilerParams(dimension_semantics=("parallel",), vmem_limit_bytes=_vmem_limit(15 * tr * LANES * 4)),
    )(*([gall] * 8), w, m, v)


def _step(x, p, target, w, m, v):
    L = p.shape[0]
    x_i, y_i, c, s = _place()
    shapes = {}
    for n in BIG:
        _, K, N = w[n].shape
        shapes[n] = (4 * K, N) if n in ROW_SHARDED else (K, 4 * N)
    pieces = {}
    for n in BIG:
        half = w[n].shape[1] // 2
        pieces[n] = lax.dynamic_slice_in_dim(w[n], c * half, half, axis=1).astype(BF16)
    wf = _gather_weights(pieces, shapes)
    small = {n: w[n] for n in SMALL}
    loss_cell, dx, gw, gsmall = _local_step(x[0], p[:, 0], target[0], wf, small)
    loss = lax.psum(jnp.sum(loss_cell), ("x", "y", "c"))
    place = jnp.stack([c, s]).astype(jnp.int32)
    views = {n: _halves_view(n, gw[n]) for n in BIG}
    got = _exchange_halves(views)
    chip_sum = {n: _shard_view(n, _chip_sum(f"chip_sum_{n}", views[n], got[n], place), L) for n in BIG}
    parts = _scatter_chip_sums(chip_sum)
    ghalf = {n: _shard_sum(f"shard_sum_{n}", chip_sum[n], parts[n], place, n in ROW_SHARDED) for n in BIG}
    gfull = _share_halves(ghalf)
    grad, delta, new_m, new_v = {}, {}, {}, {}
    for n in BIG:
        grad[n] = gfull[n].reshape(w[n].shape)
        delta[n], new_m[n], new_v[n] = _adamw(f"adamw_{n}", w[n], grad[n], m[n], v[n])
    gall = _gather_small(_pack_small(gsmall))
    gsum, dsm, nms, nvs = _small_update(gall, _pack_small(small), _pack_small({n: m[n] for n in SMALL}),
                                        _pack_small({n: v[n] for n in SMALL}))
    for dst, flat in ((grad, gsum), (delta, dsm), (new_m, nms), (new_v, nvs)):
        dst.update(_unpack_small(flat, small))
    return loss, dx[None], grad, delta, new_m, new_v


def kernel(x, p, w_in, w_br_attn, w_br_sg, w_out, sg_w, sg_b, sg_ln_g, sg_ln_b, norm_mix, norm_ffn, norm_ple, norm_final, w_ff_gate, w_ff_up, w_ff_down, w_ple_gate, w_ple, loss_target, m_w_in, m_w_br_attn, m_w_br_sg, m_w_out, m_sg_w, m_sg_b, m_sg_ln_g, m_sg_ln_b, m_norm_mix, m_norm_ffn, m_norm_ple, m_norm_final, m_w_ff_gate, m_w_ff_up, m_w_ff_down, m_w_ple_gate, m_w_ple, v_w_in, v_w_br_attn, v_w_br_sg, v_w_out, v_sg_w, v_sg_b, v_sg_ln_g, v_sg_ln_b, v_norm_mix, v_norm_ffn, v_norm_ple, v_norm_final, v_w_ff_gate, v_w_ff_up, v_w_ff_down, v_w_ple_gate, v_w_ple):
    w = dict(w_in=w_in, w_br_attn=w_br_attn, w_br_sg=w_br_sg, w_out=w_out, sg_w=sg_w, sg_b=sg_b, sg_ln_g=sg_ln_g, sg_ln_b=sg_ln_b,
             norm_mix=norm_mix, norm_ffn=norm_ffn, norm_ple=norm_ple, norm_final=norm_final, w_ff_gate=w_ff_gate, w_ff_up=w_ff_up,
             w_ff_down=w_ff_down, w_ple_gate=w_ple_gate, w_ple=w_ple)
    m = dict(w_in=m_w_in, w_br_attn=m_w_br_attn, w_br_sg=m_w_br_sg, w_out=m_w_out, sg_w=m_sg_w, sg_b=m_sg_b, sg_ln_g=m_sg_ln_g,
             sg_ln_b=m_sg_ln_b, norm_mix=m_norm_mix, norm_ffn=m_norm_ffn, norm_ple=m_norm_ple, norm_final=m_norm_final,
             w_ff_gate=m_w_ff_gate, w_ff_up=m_w_ff_up, w_ff_down=m_w_ff_down, w_ple_gate=m_w_ple_gate, w_ple=m_w_ple)
    v = dict(w_in=v_w_in, w_br_attn=v_w_br_attn, w_br_sg=v_w_br_sg, w_out=v_w_out, sg_w=v_sg_w, sg_b=v_sg_b, sg_ln_g=v_sg_ln_g,
             sg_ln_b=v_sg_ln_b, norm_mix=v_norm_mix, norm_ffn=v_norm_ffn, norm_ple=v_norm_ple, norm_final=v_norm_final,
             w_ff_gate=v_w_ff_gate, w_ff_up=v_w_ff_up, w_ff_down=v_w_ff_down, w_ple_gate=v_w_ple_gate, w_ple=v_w_ple)
    loss, grad_x, grad, delta, new_m, new_v = _step(x, p, loss_target, w, m, v)
    return (loss, grad_x, *[grad[n] for n in WEIGHTS], *[delta[n] for n in WEIGHTS], *[new_m[n] for n in WEIGHTS],
            *[new_v[n] for n in WEIGHTS])
```

```python
import functools
import math

import jax
import jax.numpy as jnp
from jax import lax
from jax.experimental import pallas as pl
from jax.experimental.pallas import tpu as pltpu
from jax.experimental.pallas import tpu_sc as plsc

F32 = jnp.float32
BF16 = jnp.bfloat16
MESH = pl.DeviceIdType.MESH

HEAD_DIM = 128
ATTN_GROUPS = ((128, 1), (512, 4), (2048, 16))
N_GROUPS = 3
HEADS = 4
QKV_W = 3 * N_GROUPS * HEADS * HEAD_DIM
ATTN_W = HEADS * HEAD_DIM
SG_CHUNK = 128
SG_GROUPS = 8
SG_W = 1024
RADIUS = 64
ROPE_THETA = 10000.0
NORM_EPS = 1e-6
NEG_INF = -1e30
ADAM_LR, ADAM_B1, ADAM_B2, ADAM_EPS, ADAM_WD, ADAM_STEP = 0.001, 0.9, 0.999, 1e-08, 0.01, 10

VMEM_CAP_V7X = 56 * 1024 * 1024
LANES = 128
EW_TILE_ELEMS = 256 * 1024
MM_VMEM_BUDGET = 44 * 1024 * 1024

GATHER_COLLECTIVE_ID = 1
SCATTER_COLLECTIVE_ID = 2

BIG = ("w_in", "w_br_attn", "w_br_sg", "w_out", "w_ff_gate", "w_ff_up", "w_ff_down", "w_ple_gate", "w_ple")
ROW_SHARDED = ("w_out", "w_ff_down", "w_ple_gate")
SMALL = ("sg_w", "sg_b", "sg_ln_g", "sg_ln_b", "norm_mix", "norm_ffn", "norm_ple", "norm_final")
WEIGHTS = ("w_in", "w_br_attn", "w_br_sg", "w_out", "sg_w", "sg_b", "sg_ln_g", "sg_ln_b", "norm_mix", "norm_ffn",
           "norm_ple", "norm_final", "w_ff_gate", "w_ff_up", "w_ff_down", "w_ple_gate", "w_ple")


def _pick(n, prefs):
    for t in prefs:
        if n % t == 0:
            return t
    return n


def _nbytes(shape, dtype):
    return math.prod(shape) * jnp.dtype(dtype).itemsize


def _vmem_limit(block_bytes, temp_bytes=0):
    est = 2 * block_bytes + temp_bytes
    assert est <= VMEM_CAP_V7X, est
    return VMEM_CAP_V7X


def _sigmoid(x):
    return 1.0 / (1.0 + jnp.exp(-x))


_GELU_C = math.sqrt(2.0 / math.pi)


def _gelu(x):
    return 0.5 * x * (1.0 + jnp.tanh(_GELU_C * (x + 0.044715 * (x * x * x))))


def _gelu_grad(x):
    t = jnp.tanh(_GELU_C * (x + 0.044715 * (x * x * x)))
    return 0.5 * (1.0 + t) + 0.5 * x * (1.0 - t * t) * (_GELU_C * (1.0 + 3.0 * 0.044715 * (x * x)))


def _lead(arr, l, blk, idx):
    if arr.ndim == 2:
        return pl.BlockSpec(blk, idx)
    return pl.BlockSpec((None,) + blk, lambda *g: (l,) + idx(*g))


def _k_steps(prods, tm, tn, fixed_bytes):
    for nk in range(1, 129):
        if any(p["K"] % nk or (p["K"] // nk) % LANES for p in prods):
            continue
        if 2 * sum((tm + tn) * (p["K"] // nk) * 2 for p in prods) + fixed_bytes <= MM_VMEM_BUDGET:
            return nk
    raise ValueError("no contraction split fits VMEM")


def _mm(name, prods, M, N, outs, epilogue, tiles=(), rows=(), tm=1024, tn=1024):
    assert M % tm == 0 and N % tn == 0, (name, M, N, tm, tn)
    fixed = 2 * tm * tn * (sum(t["x"].dtype.itemsize for t in tiles) + sum(jnp.dtype(o["dtype"]).itemsize for o in outs))
    fixed += (len(prods) + 2) * tm * tn * 4
    nk = _k_steps(prods, tm, tn, fixed)
    in_specs, args, block_bytes = [], [], 0
    for p in prods:
        if isinstance(p["b"], (list, tuple)):
            p["b"], p["bl"] = p["b"][p["bl"]], None
        K = p["K"]
        assert K % nk == 0, (name, K, nk)
        tk = K // nk
        p["tk"] = tk
        a_off, bk_off, bn_off = p.get("a_off", 0), p.get("bk_off", 0), p.get("bn_off", 0)
        assert bn_off % tn == 0 and bk_off % tk == 0
        if p["mode"] == "nn":
            assert a_off % tk == 0
            a_spec = _lead(p["a"], p.get("al"), (tm, tk), lambda i, j, k, o=a_off // tk: (i, o + k))
            b_spec = _lead(p["b"], p.get("bl"), (tk, tn), lambda i, j, k, ok=bk_off // tk, on=bn_off // tn: (ok + k, on + j))
        elif p["mode"] == "nt":
            assert a_off % tk == 0
            a_spec = _lead(p["a"], p.get("al"), (tm, tk), lambda i, j, k, o=a_off // tk: (i, o + k))
            b_spec = _lead(p["b"], p.get("bl"), (tn, tk), lambda i, j, k, ok=bk_off // tk, on=bn_off // tn: (on + j, ok + k))
        else:
            assert a_off % tm == 0
            a_spec = _lead(p["a"], p.get("al"), (tk, tm), lambda i, j, k, o=a_off // tm: (k, o + i))
            b_spec = _lead(p["b"], p.get("bl"), (tk, tn), lambda i, j, k, on=bn_off // tn: (k, on + j))
        in_specs += [a_spec, b_spec]
        args += [p["a"], p["b"]]
        block_bytes += (tm + tn) * tk * 2
    for t in tiles:
        off = t.get("off", 0)
        assert off % tn == 0
        in_specs.append(_lead(t["x"], t.get("l"), (tm, tn), lambda i, j, k, o=off // tn: (i, o + j)))
        args.append(t["x"])
        block_bytes += tm * tn * t["x"].dtype.itemsize
    for r in rows:
        in_specs.append(pl.BlockSpec((1, tn), lambda i, j, k: (0, j)))
        args.append(r)
    out_shapes, out_specs, aliases = [], [], {}
    for o_i, o in enumerate(outs):
        off = o.get("col_off", 0)
        assert off % tn == 0
        out_shapes.append(jax.ShapeDtypeStruct(o["shape"], o["dtype"]))
        idx = lambda i, j, k, oo=off // tn: (i, oo + j)
        if len(o["shape"]) == 2:
            out_specs.append(pl.BlockSpec((tm, tn), idx))
        else:
            out_specs.append(pl.BlockSpec((None, tm, tn), lambda i, j, k, l=o["l"], f=idx: (l,) + f(i, j, k)))
        if o.get("alias") is not None:
            aliases[len(args)] = o_i
            in_specs.append(pl.BlockSpec(memory_space=pl.ANY))
            args.append(o["alias"])
        block_bytes += tm * tn * jnp.dtype(o["dtype"]).itemsize
    n_p, n_t, n_r, n_o = len(prods), len(tiles), len(rows), len(outs)
    n_alias = len(aliases)
    modes = [p["mode"] for p in prods]

    def body(*refs):
        ab = refs[: 2 * n_p]
        t_refs = refs[2 * n_p: 2 * n_p + n_t]
        r_refs = refs[2 * n_p + n_t: 2 * n_p + n_t + n_r]
        o_refs = refs[2 * n_p + n_t + n_r + n_alias: 2 * n_p + n_t + n_r + n_alias + n_o]
        acc_refs = refs[2 * n_p + n_t + n_r + n_alias + n_o:]
        dims = {"nn": (((1,), (0,)), ((), ())), "nt": (((1,), (1,)), ((), ())), "tn": (((0,), (0,)), ((), ()))}

        def part(q):
            return lax.dot_general(ab[2 * q][...], ab[2 * q + 1][...], dims[modes[q]], preferred_element_type=F32)

        def finish(accs):
            res = epilogue(accs, [t[...] for t in t_refs], [r[...] for r in r_refs])
            for o_ref, val in zip(o_refs, res, strict=True):
                o_ref[...] = val.astype(o_ref.dtype)

        if nk == 1:
            finish([part(q) for q in range(n_p)])
        else:
            k = pl.program_id(2)

            @pl.when(k == 0)
            def _():
                for q, acc in enumerate(acc_refs):
                    acc[...] = part(q)

            @pl.when(k > 0)
            def _():
                for q, acc in enumerate(acc_refs):
                    acc[...] += part(q)

            @pl.when(k == nk - 1)
            def _():
                finish([acc[...] for acc in acc_refs])

    scratch = [pltpu.VMEM((tm, tn), F32) for _ in prods] if nk > 1 else []
    temp = (n_p + 2) * tm * tn * 4
    res = pl.pallas_call(
        body, name=name, grid=(M // tm, N // tn, nk), in_specs=in_specs, out_specs=out_specs, out_shape=out_shapes,
        scratch_shapes=scratch, input_output_aliases=aliases,
        compiler_params=pltpu.CompilerParams(dimension_semantics=("parallel", "parallel", "arbitrary"),
                                             vmem_limit_bytes=_vmem_limit(block_bytes, temp)),
    )(*args)
    return res


def _first(accs, tiles, rows):
    return [accs[0]]


def _ew(name, fn, ins, outs, R, C, tr=None, tc=None):
    tc = tc or _pick(C, (2048, 1536, 1408, 1024, 896, 512, 384, 256, 128))
    tr = tr or _pick(R, [t for t in (512, 256, 128, 64, 32, 16) if t * tc <= EW_TILE_ELEMS] + [8])
    in_specs, args, bb = [], [], 0
    for x in ins:
        if isinstance(x, tuple):
            arr, l = x
            in_specs.append(pl.BlockSpec((None, tr, tc), lambda i, j, l=l: (l, i, j)))
        else:
            arr = x
            in_specs.append(pl.BlockSpec((tr, tc), lambda i, j: (i, j)))
        args.append(arr)
        bb += tr * tc * arr.dtype.itemsize
    out_shapes = [jax.ShapeDtypeStruct((R, C), d) for d in outs]
    out_specs = [pl.BlockSpec((tr, tc), lambda i, j: (i, j)) for _ in outs]
    bb += sum(tr * tc * jnp.dtype(d).itemsize for d in outs)
    n_in = len(ins)

    def body(*refs):
        res = fn(*[r[...] for r in refs[:n_in]])
        for o_ref, val in zip(refs[n_in:], res, strict=True):
            o_ref[...] = val.astype(o_ref.dtype)

    return pl.pallas_call(
        body, name=name, grid=(R // tr, C // tc), in_specs=in_specs, out_specs=out_specs, out_shape=out_shapes,
        compiler_params=pltpu.CompilerParams(dimension_semantics=("parallel", "parallel"),
                                             vmem_limit_bytes=_vmem_limit(bb, 6 * tr * tc * 4)),
    )(*args)


def _rmsnorm_fwd(name, x, g):
    S, D = x.shape
    tr = _pick(S, (256, 128, 64, 8))

    def body(x_ref, g_ref, h_ref):
        xv = x_ref[...]
        r = lax.rsqrt(jnp.mean(xv * xv, axis=-1, keepdims=True) + NORM_EPS)
        h_ref[...] = (xv * r * g_ref[...]).astype(BF16)

    return pl.pallas_call(
        body, name=name, grid=(S // tr,),
        in_specs=[pl.BlockSpec((tr, D), lambda i: (i, 0)), pl.BlockSpec((1, D), lambda i: (0, 0))],
        out_specs=pl.BlockSpec((tr, D), lambda i: (i, 0)), out_shape=jax.ShapeDtypeStruct((S, D), BF16),
        compiler_params=pltpu.CompilerParams(dimension_semantics=("parallel",),
                                             vmem_limit_bytes=_vmem_limit(tr * D * 6, 3 * tr * D * 4)),
    )(x, g)


def _rmsnorm_bwd(name, x, g, dh, dres):
    S, D = x.shape
    tr = _pick(S, (256, 128, 64, 8))

    def body(x_ref, g_ref, dh_ref, dres_ref, dx_ref, dxb_ref, dg_ref):
        xv = x_ref[...]
        dy = dh_ref[...].astype(F32)
        r = lax.rsqrt(jnp.mean(xv * xv, axis=-1, keepdims=True) + NORM_EPS)
        a = dy * g_ref[...]
        dx = dres_ref[...] + r * a - xv * (r * r * r) * jnp.mean(a * xv, axis=-1, keepdims=True)
        dx_ref[...] = dx
        dxb_ref[...] = dx.astype(BF16)
        part = jnp.sum(dy * xv * r, axis=0, keepdims=True)

        @pl.when(pl.program_id(0) == 0)
        def _():
            dg_ref[...] = part

        @pl.when(pl.program_id(0) > 0)
        def _():
            dg_ref[...] += part

    row = pl.BlockSpec((tr, D), lambda i: (i, 0))
    vec = pl.BlockSpec((1, D), lambda i: (0, 0))
    return pl.pallas_call(
        body, name=name, grid=(S // tr,), in_specs=[row, vec, row, row], out_specs=[row, row, vec],
        out_shape=[jax.ShapeDtypeStruct((S, D), F32), jax.ShapeDtypeStruct((S, D), BF16), jax.ShapeDtypeStruct((1, D), F32)],
        compiler_params=pltpu.CompilerParams(dimension_semantics=("arbitrary",),
                                             vmem_limit_bytes=_vmem_limit(tr * D * 18, 5 * tr * D * 4)),
    )(x, g, dh, dres)


def _loss_head(x, g, target):
    S, D = x.shape
    tr = _pick(S, (256, 128, 64, 8))

    def body(x_ref, g_ref, t_ref, loss_ref, dx_ref, dxb_ref, dg_ref):
        xv = x_ref[...]
        r = lax.rsqrt(jnp.mean(xv * xv, axis=-1, keepdims=True) + NORM_EPS)
        xn = xv * r
        diff = xn * g_ref[...] - t_ref[...]
        dy = diff * (1.0 / D)
        a = dy * g_ref[...]
        dx = r * a - xv * (r * r * r) * jnp.mean(a * xv, axis=-1, keepdims=True)
        dx_ref[...] = dx
        dxb_ref[...] = dx.astype(BF16)
        part = jnp.sum(dy * xn, axis=0, keepdims=True)
        cell = (lax.broadcasted_iota(jnp.int32, (8, LANES), 0) == 0) & (lax.broadcasted_iota(jnp.int32, (8, LANES), 1) == 0)
        lpart = jnp.where(cell, 0.5 * jnp.sum(jnp.mean(diff * diff, axis=-1, keepdims=True)), 0.0)

        @pl.when(pl.program_id(0) == 0)
        def _():
            dg_ref[...] = part
            loss_ref[...] = lpart

        @pl.when(pl.program_id(0) > 0)
        def _():
            dg_ref[...] += part
            loss_ref[...] += lpart

    row = pl.BlockSpec((tr, D), lambda i: (i, 0))
    vec = pl.BlockSpec((1, D), lambda i: (0, 0))
    return pl.pallas_call(
        body, name="loss_head", grid=(S // tr,), in_specs=[row, vec, row],
        out_specs=[pl.BlockSpec((8, LANES), lambda i: (0, 0)), row, row, vec],
        out_shape=[jax.ShapeDtypeStruct((8, LANES), F32), jax.ShapeDtypeStruct((S, D), F32),
                   jax.ShapeDtypeStruct((S, D), BF16), jax.ShapeDtypeStruct((1, D), F32)],
        compiler_params=pltpu.CompilerParams(dimension_semantics=("arbitrary",),
                                             vmem_limit_bytes=_vmem_limit(tr * D * 14, 6 * tr * D * 4)),
    )(x, g, target)


def _rope_tables(S):
    pos = jnp.arange(S, dtype=F32)
    inv_freq = ROPE_THETA ** (-jnp.arange(0, HEAD_DIM, 2, dtype=F32) / HEAD_DIM)
    ang = pos[:, None] * inv_freq[None, :]
    cos, sin = jnp.cos(ang), jnp.sin(ang)
    return jnp.concatenate([cos, cos], axis=-1), jnp.concatenate([-sin, sin], axis=-1)


def _rope_fwd(name, z, cosf, sinf):
    S = z.shape[0]
    tr = _pick(S, (256, 128, 64, 8))
    n_rot = 2 * N_GROUPS * HEADS

    def body(z_ref, c_ref, s_ref, o_ref):
        c, s = c_ref[...], s_ref[...]
        for j in range(QKV_W // HEAD_DIM):
            t = z_ref[:, j * HEAD_DIM:(j + 1) * HEAD_DIM]
            if j < n_rot:
                t = t * c + pltpu.roll(t, HEAD_DIM // 2, axis=1) * s
            o_ref[:, j * HEAD_DIM:(j + 1) * HEAD_DIM] = t.astype(BF16)

    tab = pl.BlockSpec((tr, HEAD_DIM), lambda i: (i, 0))
    return pl.pallas_call(
        body, name=name, grid=(S // tr,), in_specs=[pl.BlockSpec((tr, QKV_W), lambda i: (i, 0)), tab, tab],
        out_specs=pl.BlockSpec((tr, QKV_W), lambda i: (i, 0)), out_shape=jax.ShapeDtypeStruct((S, QKV_W), BF16),
        compiler_params=pltpu.CompilerParams(dimension_semantics=("parallel",),
                                             vmem_limit_bytes=_vmem_limit(tr * QKV_W * 6, tr * QKV_W * 4)),
    )(z, cosf, sinf)


def _rope_bwd(name, dq, dk, dv, cosf, sinf, dz):
    S = dq.shape[0]
    tr = _pick(S, (256, 128, 64, 8))
    W3 = QKV_W // 3
    nh = W3 // HEAD_DIM

    def body(dq_ref, dk_ref, dv_ref, c_ref, s_ref, dz_in, o_ref):
        c, s = c_ref[...], s_ref[...]
        for part, ref in enumerate((dq_ref, dk_ref)):
            for j in range(nh):
                t = ref[:, j * HEAD_DIM:(j + 1) * HEAD_DIM].astype(F32)
                t = t * c - pltpu.roll(t, HEAD_DIM // 2, axis=1) * s
                o_ref[:, part * W3 + j * HEAD_DIM: part * W3 + (j + 1) * HEAD_DIM] = t.astype(BF16)
        o_ref[:, 2 * W3:] = dv_ref[...]

    third = pl.BlockSpec((tr, W3), lambda i: (i, 0))
    tab = pl.BlockSpec((tr, HEAD_DIM), lambda i: (i, 0))
    return pl.pallas_call(
        body, name=name, grid=(S // tr,),
        in_specs=[third, third, third, tab, tab, pl.BlockSpec(memory_space=pl.ANY)],
        out_specs=pl.BlockSpec((tr, QKV_W), lambda i: (i, 0)), out_shape=jax.ShapeDtypeStruct(dz.shape, dz.dtype),
        input_output_aliases={5: 0},
        compiler_params=pltpu.CompilerParams(dimension_semantics=("parallel",),
                                             vmem_limit_bytes=_vmem_limit(tr * QKV_W * 4, tr * QKV_W * 4)),
    )(dq, dk, dv, cosf, sinf, dz)


ATTN_TQ = 256


def _window(i0, d, S):
    W = min(S, ATTN_TQ + 2 * RADIUS * d)
    start = jnp.clip(i0 - RADIUS * d, 0, S - W)
    return W, pl.multiple_of(start, RADIUS)


def _band_mask(shape, q_axis, off, d):
    kq = lax.broadcasted_iota(jnp.int32, shape, 1 - q_axis) - lax.broadcasted_iota(jnp.int32, shape, q_axis) + off
    return (jnp.abs(kq) <= RADIUS * d) & ((kq & (d - 1)) == 0)


_NT = (((1,), (1,)), ((), ()))


def _attn_fwd(name, qkv):
    S = qkv.shape[0]
    T = ATTN_TQ
    scale = HEAD_DIM ** -0.5
    nq = N_GROUPS * HEADS

    def body(*refs):
        q_refs, k_refs, v_refs = refs[0:3], refs[3:6], refs[6:9]
        o_ref, lc_ref, lr_ref = refs[9:12]
        i0 = pl.program_id(1) * T
        m = jnp.full((T, 1), NEG_INF, F32)
        l = jnp.zeros((T, 1), F32)
        acc = jnp.zeros((T, HEAD_DIM), F32)
        for g, (_, d) in enumerate(ATTN_GROUPS):
            W, start = _window(i0, d, S)
            kw = k_refs[g][pl.ds(start, W), :]
            vw = v_refs[g][pl.ds(start, W), :]
            s = lax.dot_general(q_refs[g][...], kw, _NT, preferred_element_type=F32) * scale
            s = jnp.where(_band_mask((T, W), 0, start - i0, d), s, NEG_INF)
            m_new = jnp.maximum(m, jnp.max(s, axis=1, keepdims=True))
            alpha = jnp.exp(m - m_new)
            p = jnp.exp(s - m_new)
            l = l * alpha + jnp.sum(p, axis=1, keepdims=True)
            acc = acc * alpha + jnp.dot(p.astype(BF16), vw, preferred_element_type=F32)
            m = m_new
        o_ref[...] = (acc / l).astype(BF16)
        lse = m + jnp.log(l)
        lc_ref[...] = lse
        lr_ref[...] = jnp.broadcast_to(lse, (T, LANES)).T[0:1, :]

    in_specs = [pl.BlockSpec((T, HEAD_DIM), lambda h, i, g=g: (i, g * HEADS + h)) for g in range(N_GROUPS)]
    in_specs += [pl.BlockSpec((S, HEAD_DIM), lambda h, i, g=g: (0, nq + g * HEADS + h)) for g in range(N_GROUPS)]
    in_specs += [pl.BlockSpec((S, HEAD_DIM), lambda h, i, g=g: (0, 2 * nq + g * HEADS + h)) for g in range(N_GROUPS)]
    wmax = min(S, T + 2 * RADIUS * ATTN_GROUPS[-1][1])
    return pl.pallas_call(
        body, name=name, grid=(HEADS, S // T), in_specs=in_specs,
        out_specs=[pl.BlockSpec((T, HEAD_DIM), lambda h, i: (i, h)), pl.BlockSpec((None, T, 1), lambda h, i: (h, i, 0)),
                   pl.BlockSpec((None, 1, T), lambda h, i: (h, 0, i))],
        out_shape=[jax.ShapeDtypeStruct((S, ATTN_W), BF16), jax.ShapeDtypeStruct((HEADS, S, 1), F32),
                   jax.ShapeDtypeStruct((HEADS, 1, S), F32)],
        compiler_params=pltpu.CompilerParams(dimension_semantics=("parallel", "arbitrary"),
                                             vmem_limit_bytes=_vmem_limit(6 * S * HEAD_DIM * 2 + 8 * T * HEAD_DIM * 4, 5 * T * wmax * 4)),
    )(*([qkv] * 9))


def _attn_bwd(name, qkv, attn, dattn, lse_c, lse_r):
    S = qkv.shape[0]
    T = ATTN_TQ
    scale = HEAD_DIM ** -0.5
    nq = N_GROUPS * HEADS
    W3 = QKV_W // 3
    n_i = S // T

    def body(q_ref, k_ref, v_ref, o_ref, do_ref, lc_ref, lr_ref, dq_ref, dk_ref, dv_ref, dk_acc, dv_acc):
        g_id, i = pl.program_id(1), pl.program_id(2)
        i0 = i * T

        @pl.when(i == 0)
        def _():
            dk_acc[...] = jnp.zeros_like(dk_acc)
            dv_acc[...] = jnp.zeros_like(dv_acc)

        q, do = q_ref[...], do_ref[...]
        dof = do.astype(F32)
        delta_c = jnp.sum(dof * o_ref[...].astype(F32), axis=1, keepdims=True)
        delta_r = jnp.broadcast_to(delta_c, (T, LANES)).T[0:1, :]
        lse_col, lse_row = lc_ref[...], lr_ref[...]

        def group(d):
            W, start = _window(i0, d, S)
            kw = k_ref[pl.ds(start, W), :]
            vw = v_ref[pl.ds(start, W), :]
            s = lax.dot_general(q, kw, _NT, preferred_element_type=F32) * scale
            p = jnp.where(_band_mask((T, W), 0, start - i0, d), jnp.exp(s - lse_col), 0.0)
            dp = lax.dot_general(do, vw, _NT, preferred_element_type=F32)
            ds = p * (dp - delta_c)
            dq_ref[...] = (jnp.dot(ds.astype(BF16), kw, preferred_element_type=F32) * scale).astype(BF16)
            st = lax.dot_general(kw, q, _NT, preferred_element_type=F32) * scale
            pt = jnp.where(_band_mask((W, T), 1, start - i0, d), jnp.exp(st - lse_row), 0.0)
            dpt = lax.dot_general(vw, do, _NT, preferred_element_type=F32)
            dst = pt * (dpt - delta_r)
            dk_acc[pl.ds(start, W), :] += jnp.dot(dst.astype(BF16), q, preferred_element_type=F32) * scale
            dv_acc[pl.ds(start, W), :] += jnp.dot(pt.astype(BF16), do, preferred_element_type=F32)

        for g, (_, d) in enumerate(ATTN_GROUPS):
            pl.when(g_id == g)(functools.partial(group, d))

        @pl.when(i == n_i - 1)
        def _():
            dk_ref[...] = dk_acc[...].astype(BF16)
            dv_ref[...] = dv_acc[...].astype(BF16)

    tile = lambda off: pl.BlockSpec((T, HEAD_DIM), lambda h, g, i: (i, off + g * HEADS + h))
    full = lambda off: pl.BlockSpec((S, HEAD_DIM), lambda h, g, i: (0, off + g * HEADS + h))
    headt = pl.BlockSpec((T, HEAD_DIM), lambda h, g, i: (i, h))
    wmax = min(S, T + 2 * RADIUS * ATTN_GROUPS[-1][1])
    return pl.pallas_call(
        body, name=name, grid=(HEADS, N_GROUPS, n_i),
        in_specs=[tile(0), full(nq), full(2 * nq), headt, headt,
                  pl.BlockSpec((None, T, 1), lambda h, g, i: (h, i, 0)), pl.BlockSpec((None, 1, T), lambda h, g, i: (h, 0, i))],
        out_specs=[tile(0), full(0), full(0)],
        out_shape=[jax.ShapeDtypeStruct((S, W3), BF16)] * 3,
        scratch_shapes=[pltpu.VMEM((S, HEAD_DIM), F32), pltpu.VMEM((S, HEAD_DIM), F32)],
        compiler_params=pltpu.CompilerParams(dimension_semantics=("parallel", "arbitrary", "arbitrary"),
                                             vmem_limit_bytes=_vmem_limit(4 * S * HEAD_DIM * 2 + 8 * T * HEAD_DIM * 4,
                                                                          2 * S * HEAD_DIM * 4 + 8 * T * wmax * 4)),
    )(qkv, qkv, qkv, attn, dattn, lse_c, lse_r)


def _sg_parts(u, v, lng, lnb):
    gu = _gelu(u)
    gv = _gelu(v)
    mu = jnp.mean(gv, axis=-1, keepdims=True)
    xc = gv - mu
    rstd = lax.rsqrt(jnp.mean(xc * xc, axis=-1, keepdims=True) + NORM_EPS)
    xhat = xc * rstd
    vn = xhat * lng + lnb
    return gu, xhat, rstd, vn


def _sg_fwd(name, z, sg_w, sg_bc, lng, lnb, o_sg0):
    S = z.shape[0]
    T = SG_CHUNK
    cb = 512
    assert o_sg0 % cb == 0
    b0 = o_sg0 // cb

    def body(u0, u1, v0, v1, w_ref, b_ref, g_ref, be_ref, o_ref):
        u = jnp.concatenate([u0[...], u1[...]], axis=1)
        v = jnp.concatenate([v0[...], v1[...]], axis=1)
        gu, _, _, vn = _sg_parts(u, v, g_ref[...], be_ref[...])
        vnb = vn.astype(BF16)
        for g in range(SG_GROUPS):
            sl = slice(g * SG_CHUNK, (g + 1) * SG_CHUNK)
            mixed = jnp.dot(w_ref[g], vnb[:, sl], preferred_element_type=F32) + b_ref[g]
            o_ref[:, sl] = (gu[:, sl] * mixed).astype(BF16)

    zs = lambda k: pl.BlockSpec((T, cb), lambda i, k=k: (i, b0 + k))
    const3 = lambda shp: pl.BlockSpec(shp, lambda i: (0, 0, 0))
    vec = pl.BlockSpec((1, SG_W), lambda i: (0, 0))
    return pl.pallas_call(
        body, name=name, grid=(S // T,),
        in_specs=[zs(0), zs(1), zs(2), zs(3), const3((SG_GROUPS, SG_CHUNK, SG_CHUNK)), const3((SG_GROUPS, SG_CHUNK, 1)), vec, vec],
        out_specs=pl.BlockSpec((T, SG_W), lambda i: (i, 0)), out_shape=jax.ShapeDtypeStruct((S, SG_W), BF16),
        compiler_params=pltpu.CompilerParams(dimension_semantics=("parallel",), vmem_limit_bytes=_vmem_limit(4 * 1024 * 1024, 8 * T * SG_W * 4)),
    )(z, z, z, z, sg_w, sg_bc, lng, lnb)


def _sg_bwd(name, z, dsg, sg_w, sg_wt, sg_bc, lng, lnb, o_sg0, dz):
    S = z.shape[0]
    T = SG_CHUNK
    cb = 512
    b0 = o_sg0 // cb

    def body(u0, u1, v0, v1, d_ref, w_ref, wt_ref, b_ref, g_ref, be_ref, dz_in, dz_ref, dw_ref, db_ref, dg_ref, dbe_ref, stage):
        i, jj = pl.program_id(0), pl.program_id(1)

        @pl.when(jj == 0)
        def _():
            u = jnp.concatenate([u0[...], u1[...]], axis=1)
            v = jnp.concatenate([v0[...], v1[...]], axis=1)
            gu, xhat, rstd, vn = _sg_parts(u, v, g_ref[...], be_ref[...])
            vnb = vn.astype(BF16)
            dsg_v = d_ref[...].astype(F32)
            dmix = dsg_v * gu
            dmixb = dmix.astype(BF16)
            dvn_parts, mixed_parts, dw_parts, db_parts = [], [], [], []
            for g in range(SG_GROUPS):
                sl = slice(g * SG_CHUNK, (g + 1) * SG_CHUNK)
                mixed_parts.append(jnp.dot(w_ref[g], vnb[:, sl], preferred_element_type=F32) + b_ref[g])
                dvn_parts.append(jnp.dot(wt_ref[g], dmixb[:, sl], preferred_element_type=F32))
                dw_parts.append(lax.dot_general(dmixb[:, sl], vnb[:, sl], _NT, preferred_element_type=F32))
                db_parts.append(jnp.sum(dmix[:, sl], axis=1, keepdims=True))
            mixed = jnp.concatenate(mixed_parts, axis=1)
            dvn = jnp.concatenate(dvn_parts, axis=1)
            dzu = dsg_v * mixed * _gelu_grad(u)
            dxh = dvn * g_ref[...]
            dgv = rstd * (dxh - jnp.mean(dxh, axis=-1, keepdims=True) - xhat * jnp.mean(dxh * xhat, axis=-1, keepdims=True))
            dzv = dgv * _gelu_grad(v)
            stage[0] = dzu[:, :cb].astype(BF16)
            stage[1] = dzu[:, cb:].astype(BF16)
            stage[2] = dzv[:, :cb].astype(BF16)
            stage[3] = dzv[:, cb:].astype(BF16)
            dgp = jnp.sum(dvn * xhat, axis=0, keepdims=True)
            dbp = jnp.sum(dvn, axis=0, keepdims=True)

            @pl.when(i == 0)
            def _():
                for g in range(SG_GROUPS):
                    dw_ref[g] = dw_parts[g]
                    db_ref[g] = db_parts[g]
                dg_ref[...] = dgp
                dbe_ref[...] = dbp

            @pl.when(i > 0)
            def _():
                for g in range(SG_GROUPS):
                    dw_ref[g] += dw_parts[g]
                    db_ref[g] += db_parts[g]
                dg_ref[...] += dgp
                dbe_ref[...] += dbp

        dz_ref[...] = stage[jj]

    zs = lambda k: pl.BlockSpec((T, cb), lambda i, jj, k=k: (i, b0 + k))
    const3 = lambda shp: pl.BlockSpec(shp, lambda i, jj: (0, 0, 0))
    vec = pl.BlockSpec((1, SG_W), lambda i, jj: (0, 0))
    return pl.pallas_call(
        body, name=name, grid=(S // T, 4),
        in_specs=[zs(0), zs(1), zs(2), zs(3), pl.BlockSpec((T, SG_W), lambda i, jj: (i, 0)),
                  const3((SG_GROUPS, SG_CHUNK, SG_CHUNK)), const3((SG_GROUPS, SG_CHUNK, SG_CHUNK)), const3((SG_GROUPS, SG_CHUNK, 1)),
                  vec, vec, pl.BlockSpec(memory_space=pl.ANY)],
        out_specs=[pl.BlockSpec((T, cb), lambda i, jj: (i, b0 + jj)), const3((SG_GROUPS, SG_CHUNK, SG_CHUNK)),
                   const3((SG_GROUPS, SG_CHUNK, 1)), vec, vec],
        out_shape=[jax.ShapeDtypeStruct(dz.shape, dz.dtype), jax.ShapeDtypeStruct((SG_GROUPS, SG_CHUNK, SG_CHUNK), F32),
                   jax.ShapeDtypeStruct((SG_GROUPS, SG_CHUNK, 1), F32), jax.ShapeDtypeStruct((1, SG_W), F32),
                   jax.ShapeDtypeStruct((1, SG_W), F32)],
        scratch_shapes=[pltpu.VMEM((4, T, cb), BF16)],
        input_output_aliases={10: 0},
        compiler_params=pltpu.CompilerParams(dimension_semantics=("arbitrary", "arbitrary"),
                                             vmem_limit_bytes=_vmem_limit(6 * 1024 * 1024, 16 * T * SG_W * 4)),
    )(z, z, z, z, dsg, sg_w, sg_wt, sg_bc, lng, lnb, dz)


def _gate_bwd(name, z, dmerged, y_attn, y_sg, o_g0, in_w):
    S, D = dmerged.shape
    tr = _pick(S, (512, 256, 128, 8))
    cb = _pick(D, (512, 256, 128))
    assert o_g0 % cb == 0
    nd = D // cb
    b0 = o_g0 // cb

    def body(z_ref, dm_ref, ya_ref, ys_ref, dz_ref, dy_ref):
        jj = pl.program_id(1)
        gate = _sigmoid(z_ref[...])
        dm = dm_ref[...].astype(F32)
        y = jnp.where(jj < nd, ya_ref[...], ys_ref[...]).astype(F32)
        dz_ref[...] = (dm * y * gate * (1.0 - gate)).astype(BF16)
        dy_ref[...] = (dm * gate).astype(BF16)

    half = pl.BlockSpec((tr, cb), lambda i, jj: (i, jj % nd))
    return pl.pallas_call(
        body, name=name, grid=(S // tr, 2 * nd),
        in_specs=[pl.BlockSpec((tr, cb), lambda i, jj: (i, b0 + jj)), half, half, half],
        out_specs=[pl.BlockSpec((tr, cb), lambda i, jj: (i, b0 + jj)), pl.BlockSpec((tr, cb), lambda i, jj: (i, jj))],
        out_shape=[jax.ShapeDtypeStruct((S, in_w), BF16), jax.ShapeDtypeStruct((S, 2 * D), BF16)],
        compiler_params=pltpu.CompilerParams(dimension_semantics=("parallel", "arbitrary"),
                                             vmem_limit_bytes=_vmem_limit(tr * cb * 14, 6 * tr * cb * 4)),
    )(z, dmerged, y_attn, y_sg)


def _row(v):
    return v.reshape(1, -1)


def _local_step(x, p, target, wf, small, after_layer):
    S, D = x.shape
    L = p.shape[0]
    in_w = wf["w_in"][0].shape[1]
    ff = wf["w_ff_gate"][0].shape[1]
    ple = p.shape[2]
    o_sg0, o_g0 = QKV_W, QKV_W + 2 * SG_W
    cosf, sinf = _rope_tables(S)
    pb = p.astype(BF16)
    tmb = _pick(S, (1024, 512, 256))
    tn_in = _pick(in_w, (768, 1024, 512))
    tn_d = _pick(D, (1024, 512, 256))
    tn_g = _pick(D, (512, 256))
    tn_ff = _pick(ff, (512, 256))

    saved = []
    xs = x
    for i in range(L):
        sv = {"x0": xs}
        h = _rmsnorm_fwd(f"norm_mix_{i}", xs, _row(small["norm_mix"][i]))
        (z,) = _mm(f"in_proj_{i}", [dict(a=h, b=wf["w_in"], bl=i, mode="nn", K=D)], S, in_w,
                   [dict(shape=(S, in_w), dtype=F32)], _first, tm=tmb, tn=tn_in)
        qkv = _rope_fwd(f"rope_{i}", z, cosf, sinf)
        attn, lse_c, lse_r = _attn_fwd(f"attn_{i}", qkv)
        sgw = small["sg_w"][i].astype(BF16)
        sgbc = small["sg_b"][i].reshape(SG_GROUPS, SG_CHUNK, 1)
        sg = _sg_fwd(f"sgu_{i}", z, sgw, sgbc, _row(small["sg_ln_g"][i]), _row(small["sg_ln_b"][i]), o_sg0)

        def merge(accs, tiles, rows):
            ya, ys = accs[0].astype(BF16), accs[1].astype(BF16)
            g0, g1 = _sigmoid(tiles[0]), _sigmoid(tiles[1])
            return [ya, ys, g0 * ya.astype(F32) + g1 * ys.astype(F32)]

        y_attn, y_sg, merged = _mm(
            f"branches_{i}",
            [dict(a=attn, b=wf["w_br_attn"], bl=i, mode="nn", K=ATTN_W), dict(a=sg, b=wf["w_br_sg"], bl=i, mode="nn", K=SG_W)],
            S, D, [dict(shape=(S, D), dtype=BF16)] * 3, merge,
            tiles=[dict(x=z, off=o_g0), dict(x=z, off=o_g0 + D)], tm=tmb, tn=tn_g)
        (x1,) = _mm(f"out_proj_{i}", [dict(a=merged, b=wf["w_out"], bl=i, mode="nn", K=D)], S, D,
                    [dict(shape=(S, D), dtype=F32)], lambda a, t, r: [t[0] + a[0]], tiles=[dict(x=xs)], tm=tmb, tn=tn_d)
        h2 = _rmsnorm_fwd(f"norm_ffn_{i}", x1, _row(small["norm_ffn"][i]))

        def swiglu(accs, tiles, rows):
            fg = accs[0].astype(BF16).astype(F32)
            fu = accs[1].astype(BF16).astype(F32)
            return [fg, fu, fg * _sigmoid(fg) * fu]

        ffg, ffu, act = _mm(
            f"ff_in_{i}",
            [dict(a=h2, b=wf["w_ff_gate"], bl=i, mode="nn", K=D), dict(a=h2, b=wf["w_ff_up"], bl=i, mode="nn", K=D)],
            S, ff, [dict(shape=(S, ff), dtype=BF16)] * 3, swiglu, tm=tmb, tn=tn_ff)
        (x2,) = _mm(f"ff_out_{i}", [dict(a=act, b=wf["w_ff_down"], bl=i, mode="nn", K=ff)], S, D,
                    [dict(shape=(S, D), dtype=F32)], lambda a, t, r: [t[0] + a[0]], tiles=[dict(x=x1)], tm=tmb, tn=tn_d)
        h3 = _rmsnorm_fwd(f"norm_ple_{i}", x2, _row(small["norm_ple"][i]))

        def ple_mix(accs, tiles, rows):
            gp = _sigmoid(accs[0]).astype(BF16)
            pe = accs[1].astype(BF16)
            return [tiles[0] + gp.astype(F32) * pe.astype(F32), gp, pe]

        x3, gp, pe = _mm(
            f"ple_{i}",
            [dict(a=h3, b=wf["w_ple_gate"], bl=i, mode="nn", K=D), dict(a=pb, al=i, b=wf["w_ple"], bl=i, mode="nn", K=ple)],
            S, D, [dict(shape=(S, D), dtype=F32), dict(shape=(S, D), dtype=BF16), dict(shape=(S, D), dtype=BF16)], ple_mix,
            tiles=[dict(x=x2)], tm=tmb, tn=tn_g)
        sv.update(h=h, z=z, qkv=qkv, attn=attn, lse_c=lse_c, lse_r=lse_r, sg=sg, y_attn=y_attn, y_sg=y_sg, merged=merged,
                  x1=x1, h2=h2, ffg=ffg, ffu=ffu, act=act, x2=x2, h3=h3, gp=gp, pe=pe, sgw=sgw, sgbc=sgbc)
        saved.append(sv)
        xs = x3

    loss_cell, dx, dxb, dg_final = _loss_head(xs, _row(small["norm_final"]), target)

    gw = {n: [None] * L for n in BIG}
    gs = {n: [None] * L for n in SMALL if n != "norm_final"}

    def dw(n, i, a, a_off, b, bn_off, K_rows, N_cols, tm, tn):
        (gw[n][i],) = _mm(f"d_{n}_{i}", [dict(a=a, b=b, mode="tn", K=S, a_off=a_off, bn_off=bn_off)], K_rows, N_cols,
                          [dict(shape=(K_rows, N_cols), dtype=BF16)], _first, tm=tm, tn=tn)

    for i in reversed(range(L)):
        sv = saved[i]
        dpre, dpe = _ew(f"ple_gate_bwd_{i}",
                        lambda d, g, e: [d * e.astype(F32) * g.astype(F32) * (1.0 - g.astype(F32)), d * g.astype(F32)],
                        [dx, sv["gp"], sv["pe"]], [BF16, BF16], S, D)
        (dh3,) = _mm(f"d_h3_{i}", [dict(a=dpre, b=wf["w_ple_gate"], bl=i, mode="nt", K=D)], S, D,
                     [dict(shape=(S, D), dtype=F32)], _first, tm=tmb, tn=tn_d)
        dw("w_ple_gate", i, sv["h3"], 0, dpre, 0, D, D, tn_d, tn_d)
        dw("w_ple", i, pb[i], 0, dpe, 0, ple, D, _pick(ple, (256, 128)), _pick(D, (2048, 1024, 512, 256)))
        dx, dxb, gs["norm_ple"][i] = _rmsnorm_bwd(f"norm_ple_bwd_{i}", sv["x2"], _row(small["norm_ple"][i]), dh3, dx)
        def swiglu_bwd(accs, tiles, rows):
            da = accs[0].astype(BF16).astype(F32)
            fg, fu = tiles[0].astype(F32), tiles[1].astype(F32)
            sg_ = _sigmoid(fg)
            return [da * fu * (sg_ * (1.0 + fg * (1.0 - sg_))), da * (fg * sg_)]

        dffg, dffu = _mm(f"d_act_{i}", [dict(a=dxb, b=wf["w_ff_down"], bl=i, mode="nt", K=D)], S, ff,
                         [dict(shape=(S, ff), dtype=BF16)] * 2, swiglu_bwd, tiles=[dict(x=sv["ffg"]), dict(x=sv["ffu"])],
                         tm=tmb, tn=tn_ff)
        dw("w_ff_down", i, sv["act"], 0, dxb, 0, ff, D, tn_ff, _pick(D, (2048, 1024, 512, 256)))
        (dh2,) = _mm(f"d_h2_{i}", [dict(a=dffg, b=wf["w_ff_gate"], bl=i, mode="nt", K=ff),
                                   dict(a=dffu, b=wf["w_ff_up"], bl=i, mode="nt", K=ff)], S, D,
                     [dict(shape=(S, D), dtype=F32)], lambda a, t, r: [a[0] + a[1]], tm=tmb, tn=tn_d)
        dw("w_ff_gate", i, sv["h2"], 0, dffg, 0, D, ff, _pick(D, (2048, 1024, 512, 256)), tn_ff)
        dw("w_ff_up", i, sv["h2"], 0, dffu, 0, D, ff, _pick(D, (2048, 1024, 512, 256)), tn_ff)
        dx, dxb, gs["norm_ffn"][i] = _rmsnorm_bwd(f"norm_ffn_bwd_{i}", sv["x1"], _row(small["norm_ffn"][i]), dh2, dx)
        (dmerged,) = _mm(f"d_merged_{i}", [dict(a=dxb, b=wf["w_out"], bl=i, mode="nt", K=D)], S, D,
                         [dict(shape=(S, D), dtype=BF16)], _first, tm=tmb, tn=tn_d)
        dw("w_out", i, sv["merged"], 0, dxb, 0, D, D, tn_d, tn_d)
        dz, dy = _gate_bwd(f"gate_bwd_{i}", sv["z"], dmerged, sv["y_attn"], sv["y_sg"], o_g0, in_w)
        (dattn,) = _mm(f"d_attn_{i}", [dict(a=dy, b=wf["w_br_attn"], bl=i, mode="nt", K=D)], S, ATTN_W,
                       [dict(shape=(S, ATTN_W), dtype=BF16)], _first, tm=tmb, tn=ATTN_W)
        (dsg,) = _mm(f"d_sg_{i}", [dict(a=dy, a_off=D, b=wf["w_br_sg"], bl=i, mode="nt", K=D)], S, SG_W,
                     [dict(shape=(S, SG_W), dtype=BF16)], _first, tm=tmb, tn=SG_W)
        dw("w_br_attn", i, sv["attn"], 0, dy, 0, ATTN_W, D, ATTN_W, _pick(D, (2048, 1024, 512, 256)))
        dw("w_br_sg", i, sv["sg"], 0, dy, D, SG_W, D, SG_W, _pick(D, (1024, 512, 256)))
        sgwt = jnp.swapaxes(small["sg_w"][i], 1, 2).astype(BF16)
        dz, gs["sg_w"][i], dsgb, dlg, dlb = _sg_bwd(f"sgu_bwd_{i}", sv["z"], dsg, sv["sgw"], sgwt, sv["sgbc"],
                                                    _row(small["sg_ln_g"][i]), _row(small["sg_ln_b"][i]), o_sg0, dz)
        gs["sg_b"][i], gs["sg_ln_g"][i], gs["sg_ln_b"][i] = dsgb.reshape(SG_GROUPS, SG_CHUNK), dlg[0], dlb[0]
        dq, dk, dv = _attn_bwd(f"attn_bwd_{i}", sv["qkv"], sv["attn"], dattn, sv["lse_c"], sv["lse_r"])
        dz = _rope_bwd(f"rope_bwd_{i}", dq, dk, dv, cosf, sinf, dz)
        (dh,) = _mm(f"d_h_{i}", [dict(a=dz, b=wf["w_in"], bl=i, mode="nt", K=in_w)], S, D,
                    [dict(shape=(S, D), dtype=F32)], _first, tm=tmb, tn=tn_d)
        dw("w_in", i, sv["h"], 0, dz, 0, D, in_w, tn_d, tn_in)
        dx, dxb, gs["norm_mix"][i] = _rmsnorm_bwd(f"norm_mix_bwd_{i}", sv["x0"], _row(small["norm_mix"][i]), dh, dx)
        after_layer(i, {n: gw[n][i] for n in BIG})

    gsmall ={n: jnp.stack([jnp.reshape(v, small[n].shape[1:]) for v in gs[n]]) for n in gs}
    gsmall["norm_final"] = dg_final[0]
    return loss_cell, dx, gsmall


def _place():
    x, y, c = lax.axis_index("x"), lax.axis_index("y"), lax.axis_index("c")
    return x, y, c, 2 * x + y


def _chip_of(s):
    return s // 2, s % 2


def _aligned(v, m):
    return v if isinstance(v, int) else pl.multiple_of(v, m)


def _piece(name, shape, s, c):
    K, N = shape
    if name in ROW_SHARDED:
        ks = K // 4
        return s * ks + c * (ks // 2), ks // 2, 0, N
    ns = N // 4
    return c * (K // 2), K // 2, s * ns, ns


def _handshake(peers):
    barrier = pltpu.get_barrier_semaphore()
    for peer in peers:
        pl.semaphore_signal(barrier, inc=1, device_id=peer, device_id_type=MESH)
    pl.semaphore_wait(barrier, len(peers))


def _gather_body(names, shapes, src, dst, send_sems, recv_sems, local_sems):
    n_w = len(names)
    x, y, c, s = _place()
    sib = (x, y, 1 - c)
    rel = [1, 2, 3]

    def where(w, ps, pc):
        r0, nr, c0, nc = _piece(names[w], shapes[names[w]], ps, pc)
        return dst[w].at[pl.ds(_aligned(r0, 16), nr), pl.ds(_aligned(c0, LANES), nc)]

    def copy(w, k, ps, pc, to, from_src=False):
        return pltpu.make_async_remote_copy(
            src_ref=src[w] if from_src else where(w, ps, pc), dst_ref=where(w, ps, pc),
            send_sem=send_sems.at[w, k], recv_sem=recv_sems.at[w, k], device_id=to, device_id_type=MESH)

    mine, first, passed = [], [], []
    for w in range(n_w):
        cp = pltpu.make_async_copy(src[w], where(w, s, c), local_sems.at[w])
        cp.start()
        mine.append(cp)
        first.append(copy(w, 0, s, c, sib, from_src=True))
        for j in rel:
            first.append(copy(w, j, s, c, (*_chip_of(s ^ j), c), from_src=True))
    for cp in first:
        cp.start()
    for w in range(n_w):
        for j in rel:
            copy(w, j, s ^ j, c, sib).wait_recv()
            fw = copy(w, 3 + j, s ^ j, c, sib)
            fw.start()
            passed.append(fw)
    for w in range(n_w):
        copy(w, 0, s, 1 - c, sib).wait_recv()
        for j in rel:
            copy(w, 3 + j, s ^ j, 1 - c, sib).wait_recv()
    for cp in first + passed:
        cp.wait_send()
    for cp in mine:
        cp.wait()


def _gather_sems(n_w):
    return (pltpu.SemaphoreType.DMA((n_w, 7)), pltpu.SemaphoreType.DMA((n_w, 7)), pltpu.SemaphoreType.DMA((n_w,)))


def _gather_peers():
    x, y, c, s = _place()
    return [(x, y, 1 - c)] + [(*_chip_of(s ^ j), c) for j in (1, 2, 3)]


def _gather_weights(name, pieces, shapes):
    names = list(pieces)
    n_w = len(names)

    def body(*refs):
        _gather_body(names, shapes, refs[:n_w], refs[n_w:2 * n_w], *refs[2 * n_w:])

    anyspec = pl.BlockSpec(memory_space=pl.ANY)
    out = pl.pallas_call(
        body, name=name, in_specs=[anyspec] * n_w, out_specs=[anyspec] * n_w,
        out_shape=[jax.ShapeDtypeStruct(tuple(shapes[n]), BF16) for n in names], scratch_shapes=list(_gather_sems(n_w)),
    )(*[pieces[n] for n in names])
    return dict(zip(names, out))


def _gather_weights_async(name, pieces, shapes):
    names = list(pieces)
    n_w = len(names)
    src = [jax.new_ref(pieces[n], memory_space=pltpu.MemorySpace.HBM) for n in names]
    dst = [jax.empty_ref(jax.ShapeDtypeStruct(tuple(shapes[n]), BF16), memory_space=pltpu.MemorySpace.HBM) for n in names]

    @pl.kernel(mesh=plsc.ScalarSubcoreMesh(axis_name="seq", num_cores=1), name=name, scratch_types=_gather_sems(n_w),
               compiler_params=pltpu.CompilerParams(collective_id=GATHER_COLLECTIVE_ID))
    def launch(send_sems, recv_sems, local_sems):
        _handshake(_gather_peers())
        _gather_body(names, shapes, src, dst, send_sems, recv_sems, local_sems)

    launch()
    return {n: d[...] for n, d in zip(names, dst)}


def _halves_view(name, g):
    L, K, N = g.shape
    if name in ROW_SHARDED:
        return g.reshape(L * 4, 2, K // 8, N)
    return g.reshape(L, 2, K // 2, N)


def _exchange_halves(name, views):
    names = list(views)
    n_w = len(names)

    def body(*refs):
        src = refs[:n_w]
        got = refs[n_w:2 * n_w]
        send_sems, recv_sems = refs[2 * n_w:]
        x, y, c, s = _place()
        remote = [pltpu.make_async_remote_copy(src_ref=src[w].at[:, 1 - c], dst_ref=got[w], send_sem=send_sems.at[w],
                                               recv_sem=recv_sems.at[w], device_id=(x, y, 1 - c), device_id_type=MESH)
                  for w in range(n_w)]
        for cp in remote:
            cp.start()
        for cp in remote:
            cp.wait()

    anyspec = pl.BlockSpec(memory_space=pl.ANY)
    out = pl.pallas_call(
        body, name=name, in_specs=[anyspec] * n_w, out_specs=[anyspec] * n_w,
        out_shape=[jax.ShapeDtypeStruct((v.shape[0],) + v.shape[2:], BF16) for v in views.values()],
        scratch_shapes=[pltpu.SemaphoreType.DMA((n_w,)), pltpu.SemaphoreType.DMA((n_w,))],
    )(*views.values())
    return dict(zip(names, out))


def _chip_sum(name, view, got, place):
    A, _, R, C = view.shape
    tc = _pick(C, (2048, 1536, 1408, 1024, 512, 256, 128))
    tr = _pick(R, [t for t in (512, 256, 128, 64, 32, 16) if t * tc <= EW_TILE_ELEMS] + [8])

    def body(p_ref, own_ref, got_ref, o_ref):
        o_ref[...] = (own_ref[...].astype(F32) + got_ref[...].astype(F32)).astype(BF16)

    flat = pl.BlockSpec((None, tr, tc), lambda a, i, j, p: (a, i, j))
    return pl.pallas_call(
        body, name=name, out_shape=jax.ShapeDtypeStruct((A, R, C), BF16),
        grid_spec=pltpu.PrefetchScalarGridSpec(
            num_scalar_prefetch=1, grid=(A, R // tr, C // tc),
            in_specs=[pl.BlockSpec((None, None, tr, tc), lambda a, i, j, p: (a, p[0], i, j)), flat], out_specs=flat),
        compiler_params=pltpu.CompilerParams(dimension_semantics=("parallel", "parallel", "parallel"),
                                             vmem_limit_bytes=_vmem_limit(6 * tr * tc, 3 * tr * tc * 4)),
    )(place, view, got)


def _shard_view(name, ps, L):
    return ps.reshape(L, 4, *ps.shape[1:]) if name in ROW_SHARDED else ps


def _scatter_body(names, src, dst, send_sems, recv_sems):
    x, y, c, s = _place()

    def shard(w, t):
        if names[w] in ROW_SHARDED:
            return src[w].at[:, t]
        ns = src[w].shape[2] // 4
        return src[w].at[:, :, pl.ds(pl.multiple_of(t * ns, LANES), ns)]

    remote = []
    for w in range(len(names)):
        for j in (1, 2, 3):
            remote.append(pltpu.make_async_remote_copy(
                src_ref=shard(w, s ^ j), dst_ref=dst[w].at[j - 1], send_sem=send_sems.at[w, j - 1],
                recv_sem=recv_sems.at[w, j - 1], device_id=(*_chip_of(s ^ j), c), device_id_type=MESH))
    for cp in remote:
        cp.start()
    for cp in remote:
        cp.wait()


def _scatter_out_shape(name, v):
    return (3, v[0], v[2], v[3]) if name in ROW_SHARDED else (3, v[0], v[1], v[2] // 4)


def _scatter_sems(n_w):
    return (pltpu.SemaphoreType.DMA((n_w, 3)), pltpu.SemaphoreType.DMA((n_w, 3)))


def _scatter_chip_sums(name, psum):
    names = list(psum)
    n_w = len(names)

    def body(*refs):
        _scatter_body(names, refs[:n_w], refs[n_w:2 * n_w], *refs[2 * n_w:])

    anyspec = pl.BlockSpec(memory_space=pl.ANY)
    out = pl.pallas_call(
        body, name=name, in_specs=[anyspec] * n_w, out_specs=[anyspec] * n_w,
        out_shape=[jax.ShapeDtypeStruct(_scatter_out_shape(n, psum[n].shape), BF16) for n in names],
        scratch_shapes=list(_scatter_sems(n_w)),
    )(*[psum[n] for n in names])
    return dict(zip(names, out))


def _scatter_chip_sums_async(name, psum):
    names = list(psum)
    n_w = len(names)
    src = [jax.new_ref(psum[n], memory_space=pltpu.MemorySpace.HBM) for n in names]
    dst = [jax.empty_ref(jax.ShapeDtypeStruct(_scatter_out_shape(n, psum[n].shape), BF16), memory_space=pltpu.MemorySpace.HBM)
           for n in names]

    @pl.kernel(mesh=plsc.ScalarSubcoreMesh(axis_name="seq", num_cores=1), name=name, scratch_types=_scatter_sems(n_w),
               compiler_params=pltpu.CompilerParams(collective_id=SCATTER_COLLECTIVE_ID))
    def launch(send_sems, recv_sems):
        _handshake(_gather_peers()[1:])
        _scatter_body(names, src, dst, send_sems, recv_sems)

    launch()
    return {n: d[...] for n, d in zip(names, dst)}


def _shard_sum(name, ps, parts, place, row_sharded, layer, n_layers, into):
    _, _, R, C = parts.shape
    tc = _pick(C, (2048, 1408, 1024, 896, 512, 384, 256, 128))
    tr = _pick(R, [t for t in (512, 256, 128, 64, 32, 16) if t * tc <= EW_TILE_ELEMS] + [8])

    def body(p_ref, own_ref, a_ref, b_ref, c_ref, *rest):
        o_ref = rest[-1]
        o_ref[...] = ((own_ref[...].astype(F32) + a_ref[...].astype(F32)) + b_ref[...].astype(F32)) + c_ref[...].astype(F32)

    if row_sharded:
        own_spec = pl.BlockSpec((None, None, tr, tc), lambda i, j, p: (0, p[1], i, j))
    else:
        own_spec = pl.BlockSpec((None, tr, tc), lambda i, j, p: (0, i, p[1] * (C // tc) + j))
    part = lambda k: pl.BlockSpec((None, None, tr, tc), lambda i, j, p, k=k: (k, 0, i, j))
    in_specs, args, aliases = [own_spec, part(0), part(1), part(2)], [place, ps, parts, parts, parts], {}
    if into is not None:
        in_specs.append(pl.BlockSpec(memory_space=pl.ANY))
        args.append(into)
        aliases = {5: 0}
    return pl.pallas_call(
        body, name=name, out_shape=jax.ShapeDtypeStruct((n_layers, 2, R, C), F32),
        grid_spec=pltpu.PrefetchScalarGridSpec(
            num_scalar_prefetch=1, grid=(R // tr, C // tc), in_specs=in_specs,
            out_specs=pl.BlockSpec((None, None, tr, tc), lambda i, j, p: (layer, p[0], i, j))),
        input_output_aliases=aliases,
        compiler_params=pltpu.CompilerParams(dimension_semantics=("parallel", "parallel"),
                                             vmem_limit_bytes=_vmem_limit(12 * tr * tc, 5 * tr * tc * 4)),
    )(*args)


def _share_halves(ghalf):
    names = list(ghalf)
    n_w = len(names)

    def body(*refs):
        src = refs[:n_w]
        dst = refs[n_w:2 * n_w]
        send_sems, recv_sems = refs[2 * n_w:]
        x, y, c, s = _place()
        remote = [pltpu.make_async_remote_copy(src_ref=src[w].at[:, c], dst_ref=dst[w].at[:, c], send_sem=send_sems.at[w],
                                               recv_sem=recv_sems.at[w], device_id=(x, y, 1 - c), device_id_type=MESH)
                  for w in range(n_w)]
        for cp in remote:
            cp.start()
        for cp in remote:
            cp.wait()

    anyspec = pl.BlockSpec(memory_space=pl.ANY)
    out = pl.pallas_call(
        body, name="share_halves", in_specs=[anyspec] * n_w, out_specs=[anyspec] * n_w,
        out_shape=[jax.ShapeDtypeStruct(ghalf[n].shape, F32) for n in names],
        input_output_aliases={w: w for w in range(n_w)},
        scratch_shapes=[pltpu.SemaphoreType.DMA((n_w,)), pltpu.SemaphoreType.DMA((n_w,))],
    )(*[ghalf[n] for n in names])
    return dict(zip(names, out))


def _gather_small(v):
    m_per, n = v.shape

    def body(x_ref, out_ref, send_sems, recv_sems, local_sem):
        x, y, c, s = _place()
        me, sibling = (x, y, c), (x, y, 1 - c)
        chips = [(1 - x, y), (x, 1 - y), (1 - x, 1 - y)]

        def rows(px, py, pc):
            return out_ref.at[pl.ds(pl.multiple_of((4 * px + 2 * py + pc) * m_per, 8), m_per), :]

        def copy(k, block, to, src=None):
            return pltpu.make_async_remote_copy(src_ref=rows(*block) if src is None else src, dst_ref=rows(*block),
                                                send_sem=send_sems.at[k], recv_sem=recv_sems.at[k], device_id=to, device_id_type=MESH)

        mine = pltpu.make_async_copy(x_ref, rows(*me), local_sem)
        mine.start()
        first = [copy(0, me, sibling, src=x_ref)]
        first += [copy(1 + j, me, (*chip, c), src=x_ref) for j, chip in enumerate(chips)]
        for cp in first:
            cp.start()
        passed = [copy(4 + j, (*chip, c), sibling) for j, chip in enumerate(chips)]
        for j, chip in enumerate(chips):
            copy(1 + j, (*chip, c), me).wait_recv()
            passed[j].start()
        copy(0, sibling, me).wait_recv()
        for j, chip in enumerate(chips):
            copy(4 + j, (*chip, 1 - c), me).wait_recv()
        for cp in first + passed:
            cp.wait_send()
        mine.wait()

    return pl.pallas_call(
        body, name="gather_small", out_shape=jax.ShapeDtypeStruct((8 * m_per, n), v.dtype),
        in_specs=[pl.BlockSpec(memory_space=pltpu.VMEM)], out_specs=pl.BlockSpec(memory_space=pltpu.VMEM),
        scratch_shapes=[pltpu.SemaphoreType.DMA((7,)), pltpu.SemaphoreType.DMA((7,)), pltpu.SemaphoreType.DMA],
        compiler_params=pltpu.CompilerParams(vmem_limit_bytes=_vmem_limit(9 * m_per * n * 4)),
    )(v)


def _adamw_math(w, g, m, v):
    m = ADAM_B1 * m + (1.0 - ADAM_B1) * g
    v = ADAM_B2 * v + (1.0 - ADAM_B2) * (g * g)
    m_hat = m / (1.0 - ADAM_B1 ** ADAM_STEP)
    v_hat = v / (1.0 - ADAM_B2 ** ADAM_STEP)
    delta = -ADAM_LR * (m_hat / (jnp.sqrt(v_hat) + ADAM_EPS) + ADAM_WD * w)
    return delta, m, v


def _adamw(name, w, g, m, v):
    shape = w.shape
    C = shape[-1]
    R = math.prod(shape[:-1])
    f = lambda a: a.reshape(R, C)
    delta, nm, nv = _ew(name, lambda w_, g_, m_, v_: list(_adamw_math(w_, g_, m_, v_)), [f(w), f(g), f(m), f(v)], [F32] * 3, R, C)
    return delta.reshape(shape), nm.reshape(shape), nv.reshape(shape)


def _pack_small(d):
    return jnp.concatenate([d[n].reshape(-1, LANES) for n in SMALL], axis=0)


def _unpack_small(flat, like):
    out, r = {}, 0
    for n in SMALL:
        k = like[n].size // LANES
        out[n] = flat[r:r + k].reshape(like[n].shape)
        r += k
    return out


def _small_update(gall, w, m, v):
    M = w.shape[0]
    tr = _pick(M, (552, 276, 184, 96, 48, 24, 8))

    def body(*refs):
        g = refs[0][...]
        for d in range(1, 8):
            g = g + refs[d][...]
        delta, nm, nv = _adamw_math(refs[8][...], g, refs[9][...], refs[10][...])
        refs[11][...] = g
        refs[12][...] = delta
        refs[13][...] = nm
        refs[14][...] = nv

    blk = pl.BlockSpec((tr, LANES), lambda i: (i, 0))
    in_specs = [pl.BlockSpec((tr, LANES), lambda i, d=d: (d * (M // tr) + i, 0)) for d in range(8)] + [blk] * 3
    return pl.pallas_call(
        body, name="small_update", grid=(M // tr,), in_specs=in_specs, out_specs=[blk] * 4,
        out_shape=[jax.ShapeDtypeStruct((M, LANES), F32)] * 4,
        compiler_params=pltpu.CompilerParams(dimension_semantics=("parallel",), vmem_limit_bytes=_vmem_limit(15 * tr * LANES * 4)),
    )(*([gall] * 8), w, m, v)


def _step(x, p, target, w, m, v):
    L = p.shape[0]
    x_i, y_i, c, s = _place()
    shapes = {}
    for n in BIG:
        _, K, N = w[n].shape
        shapes[n] = (4 * K, N) if n in ROW_SHARDED else (K, 4 * N)
    def pieces_of(i):
        return {n: lax.dynamic_slice_in_dim(w[n][i], c * (w[n].shape[1] // 2), w[n].shape[1] // 2, axis=0).astype(BF16)
                for n in BIG}

    layers = [_gather_weights("gather_weights_0", pieces_of(0), shapes)]
    for i in range(1, L):
        before, mine = lax.optimization_barrier((layers[i - 1], pieces_of(i)))
        layers[i - 1] = before
        layers.append(_gather_weights_async(f"gather_weights_{i}", mine, shapes))
    wf = {n: [layers[i][n] for i in range(L)] for n in BIG}
    small = {n: w[n] for n in SMALL}
    place = jnp.stack([c, s]).astype(jnp.int32)
    reduced = {}

    def after_layer(i, gwi):
        views = {n: _halves_view(n, gwi[n][None]) for n in BIG}
        got = _exchange_halves(f"exchange_halves_{i}", views)
        chip_sum = {n: _shard_view(n, _chip_sum(f"chip_sum_{n}_{i}", views[n], got[n], place), 1) for n in BIG}
        scatter = _scatter_chip_sums_async if i > 0 else _scatter_chip_sums
        reduced[i] = (chip_sum, scatter(f"scatter_chip_sums_{i}", chip_sum))

    loss_cell, dx, gsmall = _local_step(x[0], p[:, 0], target[0], wf, small, after_layer)
    loss = lax.psum(jnp.sum(loss_cell), ("x", "y", "c"))
    ghalf = {n: None for n in BIG}
    for i in reversed(range(L)):
        chip_sum, parts = reduced[i]
        for n in BIG:
            ghalf[n] = _shard_sum(f"shard_sum_{n}_{i}", chip_sum[n], parts[n], place, n in ROW_SHARDED, i, L, ghalf[n])
    gfull = _share_halves(ghalf)
    grad, delta, new_m, new_v = {}, {}, {}, {}
    for n in BIG:
        grad[n] = gfull[n].reshape(w[n].shape)
        delta[n], new_m[n], new_v[n] = _adamw(f"adamw_{n}", w[n], grad[n], m[n], v[n])
    gall = _gather_small(_pack_small(gsmall))
    gsum, dsm, nms, nvs = _small_update(gall, _pack_small(small), _pack_small({n: m[n] for n in SMALL}),
                                        _pack_small({n: v[n] for n in SMALL}))
    for dst, flat in ((grad, gsum), (delta, dsm), (new_m, nms), (new_v, nvs)):
        dst.update(_unpack_small(flat, small))
    return loss, dx[None], grad, delta, new_m, new_v


def kernel(x, p, w_in, w_br_attn, w_br_sg, w_out, sg_w, sg_b, sg_ln_g, sg_ln_b, norm_mix, norm_ffn, norm_ple, norm_final, w_ff_gate, w_ff_up, w_ff_down, w_ple_gate, w_ple, loss_target, m_w_in, m_w_br_attn, m_w_br_sg, m_w_out, m_sg_w, m_sg_b, m_sg_ln_g, m_sg_ln_b, m_norm_mix, m_norm_ffn, m_norm_ple, m_norm_final, m_w_ff_gate, m_w_ff_up, m_w_ff_down, m_w_ple_gate, m_w_ple, v_w_in, v_w_br_attn, v_w_br_sg, v_w_out, v_sg_w, v_sg_b, v_sg_ln_g, v_sg_ln_b, v_norm_mix, v_norm_ffn, v_norm_ple, v_norm_final, v_w_ff_gate, v_w_ff_up, v_w_ff_down, v_w_ple_gate, v_w_ple):
    w = dict(w_in=w_in, w_br_attn=w_br_attn, w_br_sg=w_br_sg, w_out=w_out, sg_w=sg_w, sg_b=sg_b, sg_ln_g=sg_ln_g, sg_ln_b=sg_ln_b,
             norm_mix=norm_mix, norm_ffn=norm_ffn, norm_ple=norm_ple, norm_final=norm_final, w_ff_gate=w_ff_gate, w_ff_up=w_ff_up,
             w_ff_down=w_ff_down, w_ple_gate=w_ple_gate, w_ple=w_ple)
    m = dict(w_in=m_w_in, w_br_attn=m_w_br_attn, w_br_sg=m_w_br_sg, w_out=m_w_out, sg_w=m_sg_w, sg_b=m_sg_b, sg_ln_g=m_sg_ln_g,
             sg_ln_b=m_sg_ln_b, norm_mix=m_norm_mix, norm_ffn=m_norm_ffn, norm_ple=m_norm_ple, norm_final=m_norm_final,
             w_ff_gate=m_w_ff_gate, w_ff_up=m_w_ff_up, w_ff_down=m_w_ff_down, w_ple_gate=m_w_ple_gate, w_ple=m_w_ple)
    v = dict(w_in=v_w_in, w_br_attn=v_w_br_attn, w_br_sg=v_w_br_sg, w_out=v_w_out, sg_w=v_sg_w, sg_b=v_sg_b, sg_ln_g=v_sg_ln_g,
             sg_ln_b=v_sg_ln_b, norm_mix=v_norm_mix, norm_ffn=v_norm_ffn, norm_ple=v_norm_ple, norm_final=v_norm_final,
             w_ff_gate=v_w_ff_gate, w_ff_up=v_w_ff_up, w_ff_down=v_w_ff_down, w_ple_gate=v_w_ple_gate, w_ple=v_w_ple)
    loss, grad_x, grad, delta, new_m, new_v = _step(x, p, loss_target, w, m, v)
    return (loss, grad_x, *[grad[n] for n in WEIGHTS], *[delta[n] for n in WEIGHTS], *[new_m[n] for n in WEIGHTS],
            *[new_v[n] for n in WEIGHTS])
```

```python
import functools
import math

import jax
import jax.numpy as jnp
from jax import lax
from jax.experimental import pallas as pl
from jax.experimental.pallas import tpu as pltpu
from jax.experimental.pallas import tpu_sc as plsc

F32 = jnp.float32
BF16 = jnp.bfloat16
MESH = pl.DeviceIdType.MESH

HEAD_DIM = 128
ATTN_GROUPS = ((128, 1), (512, 4), (2048, 16))
N_GROUPS = 3
HEADS = 4
QKV_W = 3 * N_GROUPS * HEADS * HEAD_DIM
ATTN_W = HEADS * HEAD_DIM
SG_CHUNK = 128
SG_GROUPS = 8
SG_W = 1024
RADIUS = 64
ROPE_THETA = 10000.0
NORM_EPS = 1e-6
NEG_INF = -1e30
ADAM_LR, ADAM_B1, ADAM_B2, ADAM_EPS, ADAM_WD, ADAM_STEP = 0.001, 0.9, 0.999, 1e-08, 0.01, 10

VMEM_CAP_V7X = 56 * 1024 * 1024
LANES = 128
EW_TILE_ELEMS = 256 * 1024
MM_VMEM_BUDGET = 44 * 1024 * 1024

GATHER_COLLECTIVE_ID = 1
SCATTER_COLLECTIVE_ID = 2

BIG = ("w_in", "w_br_attn", "w_br_sg", "w_out", "w_ff_gate", "w_ff_up", "w_ff_down", "w_ple_gate", "w_ple")
ROW_SHARDED = ("w_out", "w_ff_down", "w_ple_gate")
GRAD_GROUPS = {"ffn": ("w_ple_gate", "w_ple", "w_ff_down", "w_ff_gate", "w_ff_up"), "mix": ("w_out", "w_br_attn", "w_br_sg", "w_in")}
SMALL = ("sg_w", "sg_b", "sg_ln_g", "sg_ln_b", "norm_mix", "norm_ffn", "norm_ple", "norm_final")
WEIGHTS = ("w_in", "w_br_attn", "w_br_sg", "w_out", "sg_w", "sg_b", "sg_ln_g", "sg_ln_b", "norm_mix", "norm_ffn",
           "norm_ple", "norm_final", "w_ff_gate", "w_ff_up", "w_ff_down", "w_ple_gate", "w_ple")


def _pick(n, prefs):
    for t in prefs:
        if n % t == 0:
            return t
    return n


def _nbytes(shape, dtype):
    return math.prod(shape) * jnp.dtype(dtype).itemsize


def _vmem_limit(block_bytes, temp_bytes=0):
    est = 2 * block_bytes + temp_bytes
    assert est <= VMEM_CAP_V7X, est
    return VMEM_CAP_V7X


def _sigmoid(x):
    return 1.0 / (1.0 + jnp.exp(-x))


_GELU_C = math.sqrt(2.0 / math.pi)


def _gelu(x):
    return 0.5 * x * (1.0 + jnp.tanh(_GELU_C * (x + 0.044715 * (x * x * x))))


def _gelu_grad(x):
    t = jnp.tanh(_GELU_C * (x + 0.044715 * (x * x * x)))
    return 0.5 * (1.0 + t) + 0.5 * x * (1.0 - t * t) * (_GELU_C * (1.0 + 3.0 * 0.044715 * (x * x)))


def _lead(arr, l, blk, idx):
    if arr.ndim == 2:
        return pl.BlockSpec(blk, idx)
    return pl.BlockSpec((None,) + blk, lambda *g: (l,) + idx(*g))


def _k_steps(prods, tm, tn, fixed_bytes):
    for nk in range(1, 129):
        if any(p["K"] % nk or (p["K"] // nk) % LANES for p in prods):
            continue
        if 2 * sum((tm + tn) * (p["K"] // nk) * 2 for p in prods) + fixed_bytes <= MM_VMEM_BUDGET:
            return nk
    raise ValueError("no contraction split fits VMEM")


def _mm(name, prods, M, N, outs, epilogue, tiles=(), rows=(), tm=1024, tn=1024):
    assert M % tm == 0 and N % tn == 0, (name, M, N, tm, tn)
    fixed = 2 * tm * tn * (sum(t["x"].dtype.itemsize for t in tiles) + sum(jnp.dtype(o["dtype"]).itemsize for o in outs))
    fixed += (len(prods) + 2) * tm * tn * 4
    nk = _k_steps(prods, tm, tn, fixed)
    in_specs, args, block_bytes = [], [], 0
    for p in prods:
        if isinstance(p["b"], (list, tuple)):
            p["b"], p["bl"] = p["b"][p["bl"]], None
        K = p["K"]
        assert K % nk == 0, (name, K, nk)
        tk = K // nk
        p["tk"] = tk
        a_off, bk_off, bn_off = p.get("a_off", 0), p.get("bk_off", 0), p.get("bn_off", 0)
        assert bn_off % tn == 0 and bk_off % tk == 0
        if p["mode"] == "nn":
            assert a_off % tk == 0
            a_spec = _lead(p["a"], p.get("al"), (tm, tk), lambda i, j, k, o=a_off // tk: (i, o + k))
            b_spec = _lead(p["b"], p.get("bl"), (tk, tn), lambda i, j, k, ok=bk_off // tk, on=bn_off // tn: (ok + k, on + j))
        elif p["mode"] == "nt":
            assert a_off % tk == 0
            a_spec = _lead(p["a"], p.get("al"), (tm, tk), lambda i, j, k, o=a_off // tk: (i, o + k))
            b_spec = _lead(p["b"], p.get("bl"), (tn, tk), lambda i, j, k, ok=bk_off // tk, on=bn_off // tn: (on + j, ok + k))
        else:
            assert a_off % tm == 0
            a_spec = _lead(p["a"], p.get("al"), (tk, tm), lambda i, j, k, o=a_off // tm: (k, o + i))
            b_spec = _lead(p["b"], p.get("bl"), (tk, tn), lambda i, j, k, on=bn_off // tn: (k, on + j))
        in_specs += [a_spec, b_spec]
        args += [p["a"], p["b"]]
        block_bytes += (tm + tn) * tk * 2
    for t in tiles:
        off = t.get("off", 0)
        assert off % tn == 0
        in_specs.append(_lead(t["x"], t.get("l"), (tm, tn), lambda i, j, k, o=off // tn: (i, o + j)))
        args.append(t["x"])
        block_bytes += tm * tn * t["x"].dtype.itemsize
    for r in rows:
        in_specs.append(pl.BlockSpec((1, tn), lambda i, j, k: (0, j)))
        args.append(r)
    out_shapes, out_specs, aliases = [], [], {}
    for o_i, o in enumerate(outs):
        off = o.get("col_off", 0)
        assert off % tn == 0
        out_shapes.append(jax.ShapeDtypeStruct(o["shape"], o["dtype"]))
        idx = lambda i, j, k, oo=off // tn: (i, oo + j)
        if len(o["shape"]) == 2:
            out_specs.append(pl.BlockSpec((tm, tn), idx))
        else:
            out_specs.append(pl.BlockSpec((None, tm, tn), lambda i, j, k, l=o["l"], f=idx: (l,) + f(i, j, k)))
        if o.get("alias") is not None:
            aliases[len(args)] = o_i
            in_specs.append(pl.BlockSpec(memory_space=pl.ANY))
            args.append(o["alias"])
        block_bytes += tm * tn * jnp.dtype(o["dtype"]).itemsize
    n_p, n_t, n_r, n_o = len(prods), len(tiles), len(rows), len(outs)
    n_alias = len(aliases)
    modes = [p["mode"] for p in prods]

    def body(*refs):
        ab = refs[: 2 * n_p]
        t_refs = refs[2 * n_p: 2 * n_p + n_t]
        r_refs = refs[2 * n_p + n_t: 2 * n_p + n_t + n_r]
        o_refs = refs[2 * n_p + n_t + n_r + n_alias: 2 * n_p + n_t + n_r + n_alias + n_o]
        acc_refs = refs[2 * n_p + n_t + n_r + n_alias + n_o:]
        dims = {"nn": (((1,), (0,)), ((), ())), "nt": (((1,), (1,)), ((), ())), "tn": (((0,), (0,)), ((), ()))}

        def part(q):
            return lax.dot_general(ab[2 * q][...], ab[2 * q + 1][...], dims[modes[q]], preferred_element_type=F32)

        def finish(accs):
            res = epilogue(accs, [t[...] for t in t_refs], [r[...] for r in r_refs])
            for o_ref, val in zip(o_refs, res, strict=True):
                o_ref[...] = val.astype(o_ref.dtype)

        if nk == 1:
            finish([part(q) for q in range(n_p)])
        else:
            k = pl.program_id(2)

            @pl.when(k == 0)
            def _():
                for q, acc in enumerate(acc_refs):
                    acc[...] = part(q)

            @pl.when(k > 0)
            def _():
                for q, acc in enumerate(acc_refs):
                    acc[...] += part(q)

            @pl.when(k == nk - 1)
            def _():
                finish([acc[...] for acc in acc_refs])

    scratch = [pltpu.VMEM((tm, tn), F32) for _ in prods] if nk > 1 else []
    temp = (n_p + 2) * tm * tn * 4
    res = pl.pallas_call(
        body, name=name, grid=(M // tm, N // tn, nk), in_specs=in_specs, out_specs=out_specs, out_shape=out_shapes,
        scratch_shapes=scratch, input_output_aliases=aliases,
        compiler_params=pltpu.CompilerParams(dimension_semantics=("parallel", "parallel", "arbitrary"),
                                             vmem_limit_bytes=_vmem_limit(block_bytes, temp)),
    )(*args)
    return res


def _first(accs, tiles, rows):
    return [accs[0]]


def _ew(name, fn, ins, outs, R, C, tr=None, tc=None):
    tc = tc or _pick(C, (2048, 1536, 1408, 1024, 896, 512, 384, 256, 128))
    tr = tr or _pick(R, [t for t in (512, 256, 128, 64, 32, 16) if t * tc <= EW_TILE_ELEMS] + [8])
    in_specs, args, bb = [], [], 0
    for x in ins:
        if isinstance(x, tuple):
            arr, l = x
            in_specs.append(pl.BlockSpec((None, tr, tc), lambda i, j, l=l: (l, i, j)))
        else:
            arr = x
            in_specs.append(pl.BlockSpec((tr, tc), lambda i, j: (i, j)))
        args.append(arr)
        bb += tr * tc * arr.dtype.itemsize
    out_shapes = [jax.ShapeDtypeStruct((R, C), d) for d in outs]
    out_specs = [pl.BlockSpec((tr, tc), lambda i, j: (i, j)) for _ in outs]
    bb += sum(tr * tc * jnp.dtype(d).itemsize for d in outs)
    n_in = len(ins)

    def body(*refs):
        res = fn(*[r[...] for r in refs[:n_in]])
        for o_ref, val in zip(refs[n_in:], res, strict=True):
            o_ref[...] = val.astype(o_ref.dtype)

    return pl.pallas_call(
        body, name=name, grid=(R // tr, C // tc), in_specs=in_specs, out_specs=out_specs, out_shape=out_shapes,
        compiler_params=pltpu.CompilerParams(dimension_semantics=("parallel", "parallel"),
                                             vmem_limit_bytes=_vmem_limit(bb, 6 * tr * tc * 4)),
    )(*args)


def _rmsnorm_fwd(name, x, g):
    S, D = x.shape
    tr = _pick(S, (256, 128, 64, 8))

    def body(x_ref, g_ref, h_ref):
        xv = x_ref[...]
        r = lax.rsqrt(jnp.mean(xv * xv, axis=-1, keepdims=True) + NORM_EPS)
        h_ref[...] = (xv * r * g_ref[...]).astype(BF16)

    return pl.pallas_call(
        body, name=name, grid=(S // tr,),
        in_specs=[pl.BlockSpec((tr, D), lambda i: (i, 0)), pl.BlockSpec((1, D), lambda i: (0, 0))],
        out_specs=pl.BlockSpec((tr, D), lambda i: (i, 0)), out_shape=jax.ShapeDtypeStruct((S, D), BF16),
        compiler_params=pltpu.CompilerParams(dimension_semantics=("parallel",),
                                             vmem_limit_bytes=_vmem_limit(tr * D * 6, 3 * tr * D * 4)),
    )(x, g)


def _rmsnorm_bwd(name, x, g, dh, dres):
    S, D = x.shape
    tr = _pick(S, (256, 128, 64, 8))

    def body(x_ref, g_ref, dh_ref, dres_ref, dx_ref, dxb_ref, dg_ref):
        xv = x_ref[...]
        dy = dh_ref[...].astype(F32)
        r = lax.rsqrt(jnp.mean(xv * xv, axis=-1, keepdims=True) + NORM_EPS)
        a = dy * g_ref[...]
        dx = dres_ref[...] + r * a - xv * (r * r * r) * jnp.mean(a * xv, axis=-1, keepdims=True)
        dx_ref[...] = dx
        dxb_ref[...] = dx.astype(BF16)
        part = jnp.sum(dy * xv * r, axis=0, keepdims=True)

        @pl.when(pl.program_id(0) == 0)
        def _():
            dg_ref[...] = part

        @pl.when(pl.program_id(0) > 0)
        def _():
            dg_ref[...] += part

    row = pl.BlockSpec((tr, D), lambda i: (i, 0))
    vec = pl.BlockSpec((1, D), lambda i: (0, 0))
    return pl.pallas_call(
        body, name=name, grid=(S // tr,), in_specs=[row, vec, row, row], out_specs=[row, row, vec],
        out_shape=[jax.ShapeDtypeStruct((S, D), F32), jax.ShapeDtypeStruct((S, D), BF16), jax.ShapeDtypeStruct((1, D), F32)],
        compiler_params=pltpu.CompilerParams(dimension_semantics=("arbitrary",),
                                             vmem_limit_bytes=_vmem_limit(tr * D * 18, 5 * tr * D * 4)),
    )(x, g, dh, dres)


def _loss_head(x, g, target):
    S, D = x.shape
    tr = _pick(S, (256, 128, 64, 8))

    def body(x_ref, g_ref, t_ref, loss_ref, dx_ref, dxb_ref, dg_ref):
        xv = x_ref[...]
        r = lax.rsqrt(jnp.mean(xv * xv, axis=-1, keepdims=True) + NORM_EPS)
        xn = xv * r
        diff = xn * g_ref[...] - t_ref[...]
        dy = diff * (1.0 / D)
        a = dy * g_ref[...]
        dx = r * a - xv * (r * r * r) * jnp.mean(a * xv, axis=-1, keepdims=True)
        dx_ref[...] = dx
        dxb_ref[...] = dx.astype(BF16)
        part = jnp.sum(dy * xn, axis=0, keepdims=True)
        cell = (lax.broadcasted_iota(jnp.int32, (8, LANES), 0) == 0) & (lax.broadcasted_iota(jnp.int32, (8, LANES), 1) == 0)
        lpart = jnp.where(cell, 0.5 * jnp.sum(jnp.mean(diff * diff, axis=-1, keepdims=True)), 0.0)

        @pl.when(pl.program_id(0) == 0)
        def _():
            dg_ref[...] = part
            loss_ref[...] = lpart

        @pl.when(pl.program_id(0) > 0)
        def _():
            dg_ref[...] += part
            loss_ref[...] += lpart

    row = pl.BlockSpec((tr, D), lambda i: (i, 0))
    vec = pl.BlockSpec((1, D), lambda i: (0, 0))
    return pl.pallas_call(
        body, name="loss_head", grid=(S // tr,), in_specs=[row, vec, row],
        out_specs=[pl.BlockSpec((8, LANES), lambda i: (0, 0)), row, row, vec],
        out_shape=[jax.ShapeDtypeStruct((8, LANES), F32), jax.ShapeDtypeStruct((S, D), F32),
                   jax.ShapeDtypeStruct((S, D), BF16), jax.ShapeDtypeStruct((1, D), F32)],
        compiler_params=pltpu.CompilerParams(dimension_semantics=("arbitrary",),
                                             vmem_limit_bytes=_vmem_limit(tr * D * 14, 6 * tr * D * 4)),
    )(x, g, target)


def _rope_tables(S):
    pos = jnp.arange(S, dtype=F32)
    inv_freq = ROPE_THETA ** (-jnp.arange(0, HEAD_DIM, 2, dtype=F32) / HEAD_DIM)
    ang = pos[:, None] * inv_freq[None, :]
    cos, sin = jnp.cos(ang), jnp.sin(ang)
    return jnp.concatenate([cos, cos], axis=-1), jnp.concatenate([-sin, sin], axis=-1)


def _rope_fwd(name, z, cosf, sinf):
    S = z.shape[0]
    tr = _pick(S, (256, 128, 64, 8))
    n_rot = 2 * N_GROUPS * HEADS

    def body(z_ref, c_ref, s_ref, o_ref):
        c, s = c_ref[...], s_ref[...]
        for j in range(QKV_W // HEAD_DIM):
            t = z_ref[:, j * HEAD_DIM:(j + 1) * HEAD_DIM]
            if j < n_rot:
                t = t * c + pltpu.roll(t, HEAD_DIM // 2, axis=1) * s
            o_ref[:, j * HEAD_DIM:(j + 1) * HEAD_DIM] = t.astype(BF16)

    tab = pl.BlockSpec((tr, HEAD_DIM), lambda i: (i, 0))
    return pl.pallas_call(
        body, name=name, grid=(S // tr,), in_specs=[pl.BlockSpec((tr, QKV_W), lambda i: (i, 0)), tab, tab],
        out_specs=pl.BlockSpec((tr, QKV_W), lambda i: (i, 0)), out_shape=jax.ShapeDtypeStruct((S, QKV_W), BF16),
        compiler_params=pltpu.CompilerParams(dimension_semantics=("parallel",),
                                             vmem_limit_bytes=_vmem_limit(tr * QKV_W * 6, tr * QKV_W * 4)),
    )(z, cosf, sinf)


def _rope_bwd(name, dq, dk, dv, cosf, sinf, dz):
    S = dq.shape[0]
    tr = _pick(S, (256, 128, 64, 8))
    W3 = QKV_W // 3
    nh = W3 // HEAD_DIM

    def body(dq_ref, dk_ref, dv_ref, c_ref, s_ref, dz_in, o_ref):
        c, s = c_ref[...], s_ref[...]
        for part, ref in enumerate((dq_ref, dk_ref)):
            for j in range(nh):
                t = ref[:, j * HEAD_DIM:(j + 1) * HEAD_DIM].astype(F32)
                t = t * c - pltpu.roll(t, HEAD_DIM // 2, axis=1) * s
                o_ref[:, part * W3 + j * HEAD_DIM: part * W3 + (j + 1) * HEAD_DIM] = t.astype(BF16)
        o_ref[:, 2 * W3:] = dv_ref[...]

    third = pl.BlockSpec((tr, W3), lambda i: (i, 0))
    tab = pl.BlockSpec((tr, HEAD_DIM), lambda i: (i, 0))
    return pl.pallas_call(
        body, name=name, grid=(S // tr,),
        in_specs=[third, third, third, tab, tab, pl.BlockSpec(memory_space=pl.ANY)],
        out_specs=pl.BlockSpec((tr, QKV_W), lambda i: (i, 0)), out_shape=jax.ShapeDtypeStruct(dz.shape, dz.dtype),
        input_output_aliases={5: 0},
        compiler_params=pltpu.CompilerParams(dimension_semantics=("parallel",),
                                             vmem_limit_bytes=_vmem_limit(tr * QKV_W * 4, tr * QKV_W * 4)),
    )(dq, dk, dv, cosf, sinf, dz)


ATTN_TQ = 256


def _window(i0, d, S):
    W = min(S, ATTN_TQ + 2 * RADIUS * d)
    start = jnp.clip(i0 - RADIUS * d, 0, S - W)
    return W, pl.multiple_of(start, RADIUS)


def _band_mask(shape, q_axis, off, d):
    kq = lax.broadcasted_iota(jnp.int32, shape, 1 - q_axis) - lax.broadcasted_iota(jnp.int32, shape, q_axis) + off
    return (jnp.abs(kq) <= RADIUS * d) & ((kq & (d - 1)) == 0)


_NT = (((1,), (1,)), ((), ()))


def _attn_fwd(name, qkv):
    S = qkv.shape[0]
    T = ATTN_TQ
    scale = HEAD_DIM ** -0.5
    nq = N_GROUPS * HEADS

    def body(*refs):
        q_refs, k_refs, v_refs = refs[0:3], refs[3:6], refs[6:9]
        o_ref, lc_ref, lr_ref = refs[9:12]
        i0 = pl.program_id(1) * T
        m = jnp.full((T, 1), NEG_INF, F32)
        l = jnp.zeros((T, 1), F32)
        acc = jnp.zeros((T, HEAD_DIM), F32)
        for g, (_, d) in enumerate(ATTN_GROUPS):
            W, start = _window(i0, d, S)
            kw = k_refs[g][pl.ds(start, W), :]
            vw = v_refs[g][pl.ds(start, W), :]
            s = lax.dot_general(q_refs[g][...], kw, _NT, preferred_element_type=F32) * scale
            s = jnp.where(_band_mask((T, W), 0, start - i0, d), s, NEG_INF)
            m_new = jnp.maximum(m, jnp.max(s, axis=1, keepdims=True))
            alpha = jnp.exp(m - m_new)
            p = jnp.exp(s - m_new)
            l = l * alpha + jnp.sum(p, axis=1, keepdims=True)
            acc = acc * alpha + jnp.dot(p.astype(BF16), vw, preferred_element_type=F32)
            m = m_new
        o_ref[...] = (acc / l).astype(BF16)
        lse = m + jnp.log(l)
        lc_ref[...] = lse
        lr_ref[...] = jnp.broadcast_to(lse, (T, LANES)).T[0:1, :]

    in_specs = [pl.BlockSpec((T, HEAD_DIM), lambda h, i, g=g: (i, g * HEADS + h)) for g in range(N_GROUPS)]
    in_specs += [pl.BlockSpec((S, HEAD_DIM), lambda h, i, g=g: (0, nq + g * HEADS + h)) for g in range(N_GROUPS)]
    in_specs += [pl.BlockSpec((S, HEAD_DIM), lambda h, i, g=g: (0, 2 * nq + g * HEADS + h)) for g in range(N_GROUPS)]
    wmax = min(S, T + 2 * RADIUS * ATTN_GROUPS[-1][1])
    return pl.pallas_call(
        body, name=name, grid=(HEADS, S // T), in_specs=in_specs,
        out_specs=[pl.BlockSpec((T, HEAD_DIM), lambda h, i: (i, h)), pl.BlockSpec((None, T, 1), lambda h, i: (h, i, 0)),
                   pl.BlockSpec((None, 1, T), lambda h, i: (h, 0, i))],
        out_shape=[jax.ShapeDtypeStruct((S, ATTN_W), BF16), jax.ShapeDtypeStruct((HEADS, S, 1), F32),
                   jax.ShapeDtypeStruct((HEADS, 1, S), F32)],
        compiler_params=pltpu.CompilerParams(dimension_semantics=("parallel", "arbitrary"),
                                             vmem_limit_bytes=_vmem_limit(6 * S * HEAD_DIM * 2 + 8 * T * HEAD_DIM * 4, 5 * T * wmax * 4)),
    )(*([qkv] * 9))


def _attn_bwd(name, qkv, attn, dattn, lse_c, lse_r):
    S = qkv.shape[0]
    T = ATTN_TQ
    scale = HEAD_DIM ** -0.5
    nq = N_GROUPS * HEADS
    W3 = QKV_W // 3
    n_i = S // T

    def body(q_ref, k_ref, v_ref, o_ref, do_ref, lc_ref, lr_ref, dq_ref, dk_ref, dv_ref, dk_acc, dv_acc):
        g_id, i = pl.program_id(1), pl.program_id(2)
        i0 = i * T

        @pl.when(i == 0)
        def _():
            dk_acc[...] = jnp.zeros_like(dk_acc)
            dv_acc[...] = jnp.zeros_like(dv_acc)

        q, do = q_ref[...], do_ref[...]
        dof = do.astype(F32)
        delta_c = jnp.sum(dof * o_ref[...].astype(F32), axis=1, keepdims=True)
        delta_r = jnp.broadcast_to(delta_c, (T, LANES)).T[0:1, :]
        lse_col, lse_row = lc_ref[...], lr_ref[...]

        def group(d):
            W, start = _window(i0, d, S)
            kw = k_ref[pl.ds(start, W), :]
            vw = v_ref[pl.ds(start, W), :]
            s = lax.dot_general(q, kw, _NT, preferred_element_type=F32) * scale
            p = jnp.where(_band_mask((T, W), 0, start - i0, d), jnp.exp(s - lse_col), 0.0)
            dp = lax.dot_general(do, vw, _NT, preferred_element_type=F32)
            ds = p * (dp - delta_c)
            dq_ref[...] = (jnp.dot(ds.astype(BF16), kw, preferred_element_type=F32) * scale).astype(BF16)
            st = lax.dot_general(kw, q, _NT, preferred_element_type=F32) * scale
            pt = jnp.where(_band_mask((W, T), 1, start - i0, d), jnp.exp(st - lse_row), 0.0)
            dpt = lax.dot_general(vw, do, _NT, preferred_element_type=F32)
            dst = pt * (dpt - delta_r)
            dk_acc[pl.ds(start, W), :] += jnp.dot(dst.astype(BF16), q, preferred_element_type=F32) * scale
            dv_acc[pl.ds(start, W), :] += jnp.dot(pt.astype(BF16), do, preferred_element_type=F32)

        for g, (_, d) in enumerate(ATTN_GROUPS):
            pl.when(g_id == g)(functools.partial(group, d))

        @pl.when(i == n_i - 1)
        def _():
            dk_ref[...] = dk_acc[...].astype(BF16)
            dv_ref[...] = dv_acc[...].astype(BF16)

    tile = lambda off: pl.BlockSpec((T, HEAD_DIM), lambda h, g, i: (i, off + g * HEADS + h))
    full = lambda off: pl.BlockSpec((S, HEAD_DIM), lambda h, g, i: (0, off + g * HEADS + h))
    headt = pl.BlockSpec((T, HEAD_DIM), lambda h, g, i: (i, h))
    wmax = min(S, T + 2 * RADIUS * ATTN_GROUPS[-1][1])
    return pl.pallas_call(
        body, name=name, grid=(HEADS, N_GROUPS, n_i),
        in_specs=[tile(0), full(nq), full(2 * nq), headt, headt,
                  pl.BlockSpec((None, T, 1), lambda h, g, i: (h, i, 0)), pl.BlockSpec((None, 1, T), lambda h, g, i: (h, 0, i))],
        out_specs=[tile(0), full(0), full(0)],
        out_shape=[jax.ShapeDtypeStruct((S, W3), BF16)] * 3,
        scratch_shapes=[pltpu.VMEM((S, HEAD_DIM), F32), pltpu.VMEM((S, HEAD_DIM), F32)],
        compiler_params=pltpu.CompilerParams(dimension_semantics=("parallel", "arbitrary", "arbitrary"),
                                             vmem_limit_bytes=_vmem_limit(4 * S * HEAD_DIM * 2 + 8 * T * HEAD_DIM * 4,
                                                                          2 * S * HEAD_DIM * 4 + 8 * T * wmax * 4)),
    )(qkv, qkv, qkv, attn, dattn, lse_c, lse_r)


def _sg_parts(u, v, lng, lnb):
    gu = _gelu(u)
    gv = _gelu(v)
    mu = jnp.mean(gv, axis=-1, keepdims=True)
    xc = gv - mu
    rstd = lax.rsqrt(jnp.mean(xc * xc, axis=-1, keepdims=True) + NORM_EPS)
    xhat = xc * rstd
    vn = xhat * lng + lnb
    return gu, xhat, rstd, vn


def _sg_fwd(name, z, sg_w, sg_bc, lng, lnb, o_sg0):
    S = z.shape[0]
    T = SG_CHUNK
    cb = 512
    assert o_sg0 % cb == 0
    b0 = o_sg0 // cb

    def body(u0, u1, v0, v1, w_ref, b_ref, g_ref, be_ref, o_ref):
        u = jnp.concatenate([u0[...], u1[...]], axis=1)
        v = jnp.concatenate([v0[...], v1[...]], axis=1)
        gu, _, _, vn = _sg_parts(u, v, g_ref[...], be_ref[...])
        vnb = vn.astype(BF16)
        for g in range(SG_GROUPS):
            sl = slice(g * SG_CHUNK, (g + 1) * SG_CHUNK)
            mixed = jnp.dot(w_ref[g], vnb[:, sl], preferred_element_type=F32) + b_ref[g]
            o_ref[:, sl] = (gu[:, sl] * mixed).astype(BF16)

    zs = lambda k: pl.BlockSpec((T, cb), lambda i, k=k: (i, b0 + k))
    const3 = lambda shp: pl.BlockSpec(shp, lambda i: (0, 0, 0))
    vec = pl.BlockSpec((1, SG_W), lambda i: (0, 0))
    return pl.pallas_call(
        body, name=name, grid=(S // T,),
        in_specs=[zs(0), zs(1), zs(2), zs(3), const3((SG_GROUPS, SG_CHUNK, SG_CHUNK)), const3((SG_GROUPS, SG_CHUNK, 1)), vec, vec],
        out_specs=pl.BlockSpec((T, SG_W), lambda i: (i, 0)), out_shape=jax.ShapeDtypeStruct((S, SG_W), BF16),
        compiler_params=pltpu.CompilerParams(dimension_semantics=("parallel",), vmem_limit_bytes=_vmem_limit(4 * 1024 * 1024, 8 * T * SG_W * 4)),
    )(z, z, z, z, sg_w, sg_bc, lng, lnb)


def _sg_bwd(name, z, dsg, sg_w, sg_wt, sg_bc, lng, lnb, o_sg0, dz):
    S = z.shape[0]
    T = SG_CHUNK
    cb = 512
    b0 = o_sg0 // cb

    def body(u0, u1, v0, v1, d_ref, w_ref, wt_ref, b_ref, g_ref, be_ref, dz_in, dz_ref, dw_ref, db_ref, dg_ref, dbe_ref, stage):
        i, jj = pl.program_id(0), pl.program_id(1)

        @pl.when(jj == 0)
        def _():
            u = jnp.concatenate([u0[...], u1[...]], axis=1)
            v = jnp.concatenate([v0[...], v1[...]], axis=1)
            gu, xhat, rstd, vn = _sg_parts(u, v, g_ref[...], be_ref[...])
            vnb = vn.astype(BF16)
            dsg_v = d_ref[...].astype(F32)
            dmix = dsg_v * gu
            dmixb = dmix.astype(BF16)
            dvn_parts, mixed_parts, dw_parts, db_parts = [], [], [], []
            for g in range(SG_GROUPS):
                sl = slice(g * SG_CHUNK, (g + 1) * SG_CHUNK)
                mixed_parts.append(jnp.dot(w_ref[g], vnb[:, sl], preferred_element_type=F32) + b_ref[g])
                dvn_parts.append(jnp.dot(wt_ref[g], dmixb[:, sl], preferred_element_type=F32))
                dw_parts.append(lax.dot_general(dmixb[:, sl], vnb[:, sl], _NT, preferred_element_type=F32))
                db_parts.append(jnp.sum(dmix[:, sl], axis=1, keepdims=True))
            mixed = jnp.concatenate(mixed_parts, axis=1)
            dvn = jnp.concatenate(dvn_parts, axis=1)
            dzu = dsg_v * mixed * _gelu_grad(u)
            dxh = dvn * g_ref[...]
            dgv = rstd * (dxh - jnp.mean(dxh, axis=-1, keepdims=True) - xhat * jnp.mean(dxh * xhat, axis=-1, keepdims=True))
            dzv = dgv * _gelu_grad(v)
            stage[0] = dzu[:, :cb].astype(BF16)
            stage[1] = dzu[:, cb:].astype(BF16)
            stage[2] = dzv[:, :cb].astype(BF16)
            stage[3] = dzv[:, cb:].astype(BF16)
            dgp = jnp.sum(dvn * xhat, axis=0, keepdims=True)
            dbp = jnp.sum(dvn, axis=0, keepdims=True)

            @pl.when(i == 0)
            def _():
                for g in range(SG_GROUPS):
                    dw_ref[g] = dw_parts[g]
                    db_ref[g] = db_parts[g]
                dg_ref[...] = dgp
                dbe_ref[...] = dbp

            @pl.when(i > 0)
            def _():
                for g in range(SG_GROUPS):
                    dw_ref[g] += dw_parts[g]
                    db_ref[g] += db_parts[g]
                dg_ref[...] += dgp
                dbe_ref[...] += dbp

        dz_ref[...] = stage[jj]

    zs = lambda k: pl.BlockSpec((T, cb), lambda i, jj, k=k: (i, b0 + k))
    const3 = lambda shp: pl.BlockSpec(shp, lambda i, jj: (0, 0, 0))
    vec = pl.BlockSpec((1, SG_W), lambda i, jj: (0, 0))
    return pl.pallas_call(
        body, name=name, grid=(S // T, 4),
        in_specs=[zs(0), zs(1), zs(2), zs(3), pl.BlockSpec((T, SG_W), lambda i, jj: (i, 0)),
                  const3((SG_GROUPS, SG_CHUNK, SG_CHUNK)), const3((SG_GROUPS, SG_CHUNK, SG_CHUNK)), const3((SG_GROUPS, SG_CHUNK, 1)),
                  vec, vec, pl.BlockSpec(memory_space=pl.ANY)],
        out_specs=[pl.BlockSpec((T, cb), lambda i, jj: (i, b0 + jj)), const3((SG_GROUPS, SG_CHUNK, SG_CHUNK)),
                   const3((SG_GROUPS, SG_CHUNK, 1)), vec, vec],
        out_shape=[jax.ShapeDtypeStruct(dz.shape, dz.dtype), jax.ShapeDtypeStruct((SG_GROUPS, SG_CHUNK, SG_CHUNK), F32),
                   jax.ShapeDtypeStruct((SG_GROUPS, SG_CHUNK, 1), F32), jax.ShapeDtypeStruct((1, SG_W), F32),
                   jax.ShapeDtypeStruct((1, SG_W), F32)],
        scratch_shapes=[pltpu.VMEM((4, T, cb), BF16)],
        input_output_aliases={10: 0},
        compiler_params=pltpu.CompilerParams(dimension_semantics=("arbitrary", "arbitrary"),
                                             vmem_limit_bytes=_vmem_limit(6 * 1024 * 1024, 16 * T * SG_W * 4)),
    )(z, z, z, z, dsg, sg_w, sg_wt, sg_bc, lng, lnb, dz)


def _gate_bwd(name, z, dmerged, y_attn, y_sg, o_g0, in_w):
    S, D = dmerged.shape
    tr = _pick(S, (512, 256, 128, 8))
    cb = _pick(D, (512, 256, 128))
    assert o_g0 % cb == 0
    nd = D // cb
    b0 = o_g0 // cb

    def body(z_ref, dm_ref, ya_ref, ys_ref, dz_ref, dy_ref):
        jj = pl.program_id(1)
        gate = _sigmoid(z_ref[...])
        dm = dm_ref[...].astype(F32)
        y = jnp.where(jj < nd, ya_ref[...], ys_ref[...]).astype(F32)
        dz_ref[...] = (dm * y * gate * (1.0 - gate)).astype(BF16)
        dy_ref[...] = (dm * gate).astype(BF16)

    half = pl.BlockSpec((tr, cb), lambda i, jj: (i, jj % nd))
    return pl.pallas_call(
        body, name=name, grid=(S // tr, 2 * nd),
        in_specs=[pl.BlockSpec((tr, cb), lambda i, jj: (i, b0 + jj)), half, half, half],
        out_specs=[pl.BlockSpec((tr, cb), lambda i, jj: (i, b0 + jj)), pl.BlockSpec((tr, cb), lambda i, jj: (i, jj))],
        out_shape=[jax.ShapeDtypeStruct((S, in_w), BF16), jax.ShapeDtypeStruct((S, 2 * D), BF16)],
        compiler_params=pltpu.CompilerParams(dimension_semantics=("parallel", "arbitrary"),
                                             vmem_limit_bytes=_vmem_limit(tr * cb * 14, 6 * tr * cb * 4)),
    )(z, dmerged, y_attn, y_sg)


def _row(v):
    return v.reshape(1, -1)


def _local_step(x, p, target, wf, small, after_group):
    S, D = x.shape
    L = p.shape[0]
    in_w = wf["w_in"][0].shape[1]
    ff = wf["w_ff_gate"][0].shape[1]
    ple = p.shape[2]
    o_sg0, o_g0 = QKV_W, QKV_W + 2 * SG_W
    cosf, sinf = _rope_tables(S)
    pb = p.astype(BF16)
    tmb = _pick(S, (1024, 512, 256))
    tn_in = _pick(in_w, (768, 1024, 512))
    tn_d = _pick(D, (1024, 512, 256))
    tn_g = _pick(D, (512, 256))
    tn_ff = _pick(ff, (512, 256))

    saved = []
    xs = x
    for i in range(L):
        sv = {"x0": xs}
        h = _rmsnorm_fwd(f"norm_mix_{i}", xs, _row(small["norm_mix"][i]))
        (z,) = _mm(f"in_proj_{i}", [dict(a=h, b=wf["w_in"], bl=i, mode="nn", K=D)], S, in_w,
                   [dict(shape=(S, in_w), dtype=F32)], _first, tm=tmb, tn=tn_in)
        qkv = _rope_fwd(f"rope_{i}", z, cosf, sinf)
        attn, lse_c, lse_r = _attn_fwd(f"attn_{i}", qkv)
        sgw = small["sg_w"][i].astype(BF16)
        sgbc = small["sg_b"][i].reshape(SG_GROUPS, SG_CHUNK, 1)
        sg = _sg_fwd(f"sgu_{i}", z, sgw, sgbc, _row(small["sg_ln_g"][i]), _row(small["sg_ln_b"][i]), o_sg0)

        def merge(accs, tiles, rows):
            ya, ys = accs[0].astype(BF16), accs[1].astype(BF16)
            g0, g1 = _sigmoid(tiles[0]), _sigmoid(tiles[1])
            return [ya, ys, g0 * ya.astype(F32) + g1 * ys.astype(F32)]

        y_attn, y_sg, merged = _mm(
            f"branches_{i}",
            [dict(a=attn, b=wf["w_br_attn"], bl=i, mode="nn", K=ATTN_W), dict(a=sg, b=wf["w_br_sg"], bl=i, mode="nn", K=SG_W)],
            S, D, [dict(shape=(S, D), dtype=BF16)] * 3, merge,
            tiles=[dict(x=z, off=o_g0), dict(x=z, off=o_g0 + D)], tm=tmb, tn=tn_g)
        (x1,) = _mm(f"out_proj_{i}", [dict(a=merged, b=wf["w_out"], bl=i, mode="nn", K=D)], S, D,
                    [dict(shape=(S, D), dtype=F32)], lambda a, t, r: [t[0] + a[0]], tiles=[dict(x=xs)], tm=tmb, tn=tn_d)
        h2 = _rmsnorm_fwd(f"norm_ffn_{i}", x1, _row(small["norm_ffn"][i]))

        def swiglu(accs, tiles, rows):
            fg = accs[0].astype(BF16).astype(F32)
            fu = accs[1].astype(BF16).astype(F32)
            return [fg, fu, fg * _sigmoid(fg) * fu]

        ffg, ffu, act = _mm(
            f"ff_in_{i}",
            [dict(a=h2, b=wf["w_ff_gate"], bl=i, mode="nn", K=D), dict(a=h2, b=wf["w_ff_up"], bl=i, mode="nn", K=D)],
            S, ff, [dict(shape=(S, ff), dtype=BF16)] * 3, swiglu, tm=tmb, tn=tn_ff)
        (x2,) = _mm(f"ff_out_{i}", [dict(a=act, b=wf["w_ff_down"], bl=i, mode="nn", K=ff)], S, D,
                    [dict(shape=(S, D), dtype=F32)], lambda a, t, r: [t[0] + a[0]], tiles=[dict(x=x1)], tm=tmb, tn=tn_d)
        h3 = _rmsnorm_fwd(f"norm_ple_{i}", x2, _row(small["norm_ple"][i]))

        def ple_mix(accs, tiles, rows):
            gp = _sigmoid(accs[0]).astype(BF16)
            pe = accs[1].astype(BF16)
            return [tiles[0] + gp.astype(F32) * pe.astype(F32), gp, pe]

        x3, gp, pe = _mm(
            f"ple_{i}",
            [dict(a=h3, b=wf["w_ple_gate"], bl=i, mode="nn", K=D), dict(a=pb, al=i, b=wf["w_ple"], bl=i, mode="nn", K=ple)],
            S, D, [dict(shape=(S, D), dtype=F32), dict(shape=(S, D), dtype=BF16), dict(shape=(S, D), dtype=BF16)], ple_mix,
            tiles=[dict(x=x2)], tm=tmb, tn=tn_g)
        sv.update(h=h, z=z, qkv=qkv, attn=attn, lse_c=lse_c, lse_r=lse_r, sg=sg, y_attn=y_attn, y_sg=y_sg, merged=merged,
                  x1=x1, h2=h2, ffg=ffg, ffu=ffu, act=act, x2=x2, h3=h3, gp=gp, pe=pe, sgw=sgw, sgbc=sgbc)
        saved.append(sv)
        xs = x3

    loss_cell, dx, dxb, dg_final = _loss_head(xs, _row(small["norm_final"]), target)

    gw = {n: [None] * L for n in BIG}
    gs = {n: [None] * L for n in SMALL if n != "norm_final"}

    def dw(n, i, a, a_off, b, bn_off, K_rows, N_cols, tm, tn):
        (gw[n][i],) = _mm(f"d_{n}_{i}", [dict(a=a, b=b, mode="tn", K=S, a_off=a_off, bn_off=bn_off)], K_rows, N_cols,
                          [dict(shape=(K_rows, N_cols), dtype=BF16)], _first, tm=tm, tn=tn)

    for i in reversed(range(L)):
        sv = saved[i]
        dpre, dpe = _ew(f"ple_gate_bwd_{i}",
                        lambda d, g, e: [d * e.astype(F32) * g.astype(F32) * (1.0 - g.astype(F32)), d * g.astype(F32)],
                        [dx, sv["gp"], sv["pe"]], [BF16, BF16], S, D)
        (dh3,) = _mm(f"d_h3_{i}", [dict(a=dpre, b=wf["w_ple_gate"], bl=i, mode="nt", K=D)], S, D,
                     [dict(shape=(S, D), dtype=F32)], _first, tm=tmb, tn=tn_d)
        dw("w_ple_gate", i, sv["h3"], 0, dpre, 0, D, D, tn_d, tn_d)
        dw("w_ple", i, pb[i], 0, dpe, 0, ple, D, _pick(ple, (256, 128)), _pick(D, (2048, 1024, 512, 256)))
        dx, dxb, gs["norm_ple"][i] = _rmsnorm_bwd(f"norm_ple_bwd_{i}", sv["x2"], _row(small["norm_ple"][i]), dh3, dx)
        def swiglu_bwd(accs, tiles, rows):
            da = accs[0].astype(BF16).astype(F32)
            fg, fu = tiles[0].astype(F32), tiles[1].astype(F32)
            sg_ = _sigmoid(fg)
            return [da * fu * (sg_ * (1.0 + fg * (1.0 - sg_))), da * (fg * sg_)]

        dffg, dffu = _mm(f"d_act_{i}", [dict(a=dxb, b=wf["w_ff_down"], bl=i, mode="nt", K=D)], S, ff,
                         [dict(shape=(S, ff), dtype=BF16)] * 2, swiglu_bwd, tiles=[dict(x=sv["ffg"]), dict(x=sv["ffu"])],
                         tm=tmb, tn=tn_ff)
        dw("w_ff_down", i, sv["act"], 0, dxb, 0, ff, D, tn_ff, _pick(D, (2048, 1024, 512, 256)))
        dw("w_ff_gate", i, sv["h2"], 0, dffg, 0, D, ff, _pick(D, (2048, 1024, 512, 256)), tn_ff)
        dw("w_ff_up", i, sv["h2"], 0, dffu, 0, D, ff, _pick(D, (2048, 1024, 512, 256)), tn_ff)
        (dffg, dffu), _ = lax.optimization_barrier(((dffg, dffu), after_group(i, "ffn", {n: gw[n][i] for n in GRAD_GROUPS["ffn"]})))
        (dh2,) = _mm(f"d_h2_{i}", [dict(a=dffg, b=wf["w_ff_gate"], bl=i, mode="nt", K=ff),
                                   dict(a=dffu, b=wf["w_ff_up"], bl=i, mode="nt", K=ff)], S, D,
                     [dict(shape=(S, D), dtype=F32)], lambda a, t, r: [a[0] + a[1]], tm=tmb, tn=tn_d)
        dx, dxb, gs["norm_ffn"][i] = _rmsnorm_bwd(f"norm_ffn_bwd_{i}", sv["x1"], _row(small["norm_ffn"][i]), dh2, dx)
        (dmerged,) = _mm(f"d_merged_{i}", [dict(a=dxb, b=wf["w_out"], bl=i, mode="nt", K=D)], S, D,
                         [dict(shape=(S, D), dtype=BF16)], _first, tm=tmb, tn=tn_d)
        dw("w_out", i, sv["merged"], 0, dxb, 0, D, D, tn_d, tn_d)
        dz, dy = _gate_bwd(f"gate_bwd_{i}", sv["z"], dmerged, sv["y_attn"], sv["y_sg"], o_g0, in_w)
        (dattn,) = _mm(f"d_attn_{i}", [dict(a=dy, b=wf["w_br_attn"], bl=i, mode="nt", K=D)], S, ATTN_W,
                       [dict(shape=(S, ATTN_W), dtype=BF16)], _first, tm=tmb, tn=ATTN_W)
        (dsg,) = _mm(f"d_sg_{i}", [dict(a=dy, a_off=D, b=wf["w_br_sg"], bl=i, mode="nt", K=D)], S, SG_W,
                     [dict(shape=(S, SG_W), dtype=BF16)], _first, tm=tmb, tn=SG_W)
        dw("w_br_attn", i, sv["attn"], 0, dy, 0, ATTN_W, D, ATTN_W, _pick(D, (2048, 1024, 512, 256)))
        dw("w_br_sg", i, sv["sg"], 0, dy, D, SG_W, D, SG_W, _pick(D, (1024, 512, 256)))
        sgwt = jnp.swapaxes(small["sg_w"][i], 1, 2).astype(BF16)
        dz, gs["sg_w"][i], dsgb, dlg, dlb = _sg_bwd(f"sgu_bwd_{i}", sv["z"], dsg, sv["sgw"], sgwt, sv["sgbc"],
                                                    _row(small["sg_ln_g"][i]), _row(small["sg_ln_b"][i]), o_sg0, dz)
        gs["sg_b"][i], gs["sg_ln_g"][i], gs["sg_ln_b"][i] = dsgb.reshape(SG_GROUPS, SG_CHUNK), dlg[0], dlb[0]
        dq, dk, dv = _attn_bwd(f"attn_bwd_{i}", sv["qkv"], sv["attn"], dattn, sv["lse_c"], sv["lse_r"])
        dz = _rope_bwd(f"rope_bwd_{i}", dq, dk, dv, cosf, sinf, dz)
        dw("w_in", i, sv["h"], 0, dz, 0, D, in_w, tn_d, tn_in)
        dz, _ = lax.optimization_barrier((dz, after_group(i, "mix", {n: gw[n][i] for n in GRAD_GROUPS["mix"]})))
        (dh,) = _mm(f"d_h_{i}", [dict(a=dz, b=wf["w_in"], bl=i, mode="nt", K=in_w)], S, D,
                    [dict(shape=(S, D), dtype=F32)], _first, tm=tmb, tn=tn_d)
        dx, dxb, gs["norm_mix"][i] = _rmsnorm_bwd(f"norm_mix_bwd_{i}", sv["x0"], _row(small["norm_mix"][i]), dh, dx)

    gsmall ={n: jnp.stack([jnp.reshape(v, small[n].shape[1:]) for v in gs[n]]) for n in gs}
    gsmall["norm_final"] = dg_final[0]
    return loss_cell, dx, gsmall


def _place():
    x, y, c = lax.axis_index("x"), lax.axis_index("y"), lax.axis_index("c")
    return x, y, c, 2 * x + y


def _chip_of(s):
    return s // 2, s % 2


def _aligned(v, m):
    return v if isinstance(v, int) else pl.multiple_of(v, m)


def _piece(name, shape, s, c):
    K, N = shape
    if name in ROW_SHARDED:
        ks = K // 4
        return s * ks + c * (ks // 2), ks // 2, 0, N
    ns = N // 4
    return c * (K // 2), K // 2, s * ns, ns


def _handshake(peers):
    barrier = pltpu.get_barrier_semaphore()
    for peer in peers:
        pl.semaphore_signal(barrier, inc=1, device_id=peer, device_id_type=MESH)
    pl.semaphore_wait(barrier, len(peers))


def _gather_body(names, shapes, src, dst, send_sems, recv_sems, local_sems):
    n_w = len(names)
    x, y, c, s = _place()
    sib = (x, y, 1 - c)
    rel = [1, 2, 3]

    def where(w, ps, pc):
        r0, nr, c0, nc = _piece(names[w], shapes[names[w]], ps, pc)
        return dst[w].at[pl.ds(_aligned(r0, 16), nr), pl.ds(_aligned(c0, LANES), nc)]

    def copy(w, k, ps, pc, to, from_src=False):
        return pltpu.make_async_remote_copy(
            src_ref=src[w] if from_src else where(w, ps, pc), dst_ref=where(w, ps, pc),
            send_sem=send_sems.at[w, k], recv_sem=recv_sems.at[w, k], device_id=to, device_id_type=MESH)

    mine, first, passed = [], [], []
    for w in range(n_w):
        cp = pltpu.make_async_copy(src[w], where(w, s, c), local_sems.at[w])
        cp.start()
        mine.append(cp)
        first.append(copy(w, 0, s, c, sib, from_src=True))
        for j in rel:
            first.append(copy(w, j, s, c, (*_chip_of(s ^ j), c), from_src=True))
    for cp in first:
        cp.start()
    for w in range(n_w):
        for j in rel:
            copy(w, j, s ^ j, c, sib).wait_recv()
            fw = copy(w, 3 + j, s ^ j, c, sib)
            fw.start()
            passed.append(fw)
    for w in range(n_w):
        copy(w, 0, s, 1 - c, sib).wait_recv()
        for j in rel:
            copy(w, 3 + j, s ^ j, 1 - c, sib).wait_recv()
    for cp in first + passed:
        cp.wait_send()
    for cp in mine:
        cp.wait()


def _gather_sems(n_w):
    return (pltpu.SemaphoreType.DMA((n_w, 7)), pltpu.SemaphoreType.DMA((n_w, 7)), pltpu.SemaphoreType.DMA((n_w,)))


def _gather_peers():
    x, y, c, s = _place()
    return [(x, y, 1 - c)] + [(*_chip_of(s ^ j), c) for j in (1, 2, 3)]


def _gather_weights(name, pieces, shapes):
    names = list(pieces)
    n_w = len(names)

    def body(*refs):
        _gather_body(names, shapes, refs[:n_w], refs[n_w:2 * n_w], *refs[2 * n_w:])

    anyspec = pl.BlockSpec(memory_space=pl.ANY)
    out = pl.pallas_call(
        body, name=name, in_specs=[anyspec] * n_w, out_specs=[anyspec] * n_w,
        out_shape=[jax.ShapeDtypeStruct(tuple(shapes[n]), BF16) for n in names], scratch_shapes=list(_gather_sems(n_w)),
    )(*[pieces[n] for n in names])
    return dict(zip(names, out))


def _gather_weights_async(name, pieces, shapes):
    names = list(pieces)
    n_w = len(names)
    src = [jax.new_ref(pieces[n], memory_space=pltpu.MemorySpace.HBM) for n in names]
    dst = [jax.empty_ref(jax.ShapeDtypeStruct(tuple(shapes[n]), BF16), memory_space=pltpu.MemorySpace.HBM) for n in names]

    @pl.kernel(mesh=plsc.ScalarSubcoreMesh(axis_name="seq", num_cores=1), name=name, scratch_types=_gather_sems(n_w),
               compiler_params=pltpu.CompilerParams(collective_id=GATHER_COLLECTIVE_ID))
    def launch(send_sems, recv_sems, local_sems):
        _handshake(_gather_peers())
        _gather_body(names, shapes, src, dst, send_sems, recv_sems, local_sems)

    launch()
    return {n: d[...] for n, d in zip(names, dst)}


def _halves_view(name, g):
    L, K, N = g.shape
    if name in ROW_SHARDED:
        return g.reshape(L * 4, 2, K // 8, N)
    return g.reshape(L, 2, K // 2, N)


def _exchange_halves(name, views):
    names = list(views)
    n_w = len(names)

    def body(*refs):
        src = refs[:n_w]
        got = refs[n_w:2 * n_w]
        send_sems, recv_sems = refs[2 * n_w:]
        x, y, c, s = _place()
        remote = [pltpu.make_async_remote_copy(src_ref=src[w].at[:, 1 - c], dst_ref=got[w], send_sem=send_sems.at[w],
                                               recv_sem=recv_sems.at[w], device_id=(x, y, 1 - c), device_id_type=MESH)
                  for w in range(n_w)]
        for cp in remote:
            cp.start()
        for cp in remote:
            cp.wait()

    anyspec = pl.BlockSpec(memory_space=pl.ANY)
    out = pl.pallas_call(
        body, name=name, in_specs=[anyspec] * n_w, out_specs=[anyspec] * n_w,
        out_shape=[jax.ShapeDtypeStruct((v.shape[0],) + v.shape[2:], BF16) for v in views.values()],
        scratch_shapes=[pltpu.SemaphoreType.DMA((n_w,)), pltpu.SemaphoreType.DMA((n_w,))],
    )(*views.values())
    return dict(zip(names, out))


def _chip_sum(name, view, got, place):
    A, _, R, C = view.shape
    tc = _pick(C, (2048, 1536, 1408, 1024, 512, 256, 128))
    tr = _pick(R, [t for t in (512, 256, 128, 64, 32, 16) if t * tc <= EW_TILE_ELEMS] + [8])

    def body(p_ref, own_ref, got_ref, o_ref):
        o_ref[...] = (own_ref[...].astype(F32) + got_ref[...].astype(F32)).astype(BF16)

    flat = pl.BlockSpec((None, tr, tc), lambda a, i, j, p: (a, i, j))
    return pl.pallas_call(
        body, name=name, out_shape=jax.ShapeDtypeStruct((A, R, C), BF16),
        grid_spec=pltpu.PrefetchScalarGridSpec(
            num_scalar_prefetch=1, grid=(A, R // tr, C // tc),
            in_specs=[pl.BlockSpec((None, None, tr, tc), lambda a, i, j, p: (a, p[0], i, j)), flat], out_specs=flat),
        compiler_params=pltpu.CompilerParams(dimension_semantics=("parallel", "parallel", "parallel"),
                                             vmem_limit_bytes=_vmem_limit(6 * tr * tc, 3 * tr * tc * 4)),
    )(place, view, got)


def _shard_view(name, ps, L):
    return ps.reshape(L, 4, *ps.shape[1:]) if name in ROW_SHARDED else ps


def _scatter_body(names, src, dst, send_sems, recv_sems):
    x, y, c, s = _place()

    def shard(w, t):
        if names[w] in ROW_SHARDED:
            return src[w].at[:, t]
        ns = src[w].shape[2] // 4
        return src[w].at[:, :, pl.ds(pl.multiple_of(t * ns, LANES), ns)]

    remote = []
    for w in range(len(names)):
        for j in (1, 2, 3):
            remote.append(pltpu.make_async_remote_copy(
                src_ref=shard(w, s ^ j), dst_ref=dst[w].at[j - 1], send_sem=send_sems.at[w, j - 1],
                recv_sem=recv_sems.at[w, j - 1], device_id=(*_chip_of(s ^ j), c), device_id_type=MESH))
    for cp in remote:
        cp.start()
    for cp in remote:
        cp.wait()


def _scatter_out_shape(name, v):
    return (3, v[0], v[2], v[3]) if name in ROW_SHARDED else (3, v[0], v[1], v[2] // 4)


def _scatter_sems(n_w):
    return (pltpu.SemaphoreType.DMA((n_w, 3)), pltpu.SemaphoreType.DMA((n_w, 3)))


def _scatter_chip_sums_async(name, psum):
    names = list(psum)
    n_w = len(names)
    src = [jax.new_ref(psum[n], memory_space=pltpu.MemorySpace.HBM) for n in names]
    dst = [jax.empty_ref(jax.ShapeDtypeStruct(_scatter_out_shape(n, psum[n].shape), BF16), memory_space=pltpu.MemorySpace.HBM)
           for n in names]

    @pl.kernel(mesh=plsc.ScalarSubcoreMesh(axis_name="seq", num_cores=1), name=name, scratch_types=_scatter_sems(n_w),
               compiler_params=pltpu.CompilerParams(collective_id=SCATTER_COLLECTIVE_ID))
    def launch(send_sems, recv_sems):
        _handshake(_gather_peers()[1:])
        _scatter_body(names, src, dst, send_sems, recv_sems)

    launch()
    return {n: d[...] for n, d in zip(names, dst)}


def _shard_sum(name, ps, parts, place, row_sharded, layer, n_layers, into):
    _, _, R, C = parts.shape
    tc = _pick(C, (2048, 1408, 1024, 896, 512, 384, 256, 128))
    tr = _pick(R, [t for t in (512, 256, 128, 64, 32, 16) if t * tc <= EW_TILE_ELEMS] + [8])

    def body(p_ref, own_ref, a_ref, b_ref, c_ref, *rest):
        o_ref = rest[-1]
        o_ref[...] = ((own_ref[...].astype(F32) + a_ref[...].astype(F32)) + b_ref[...].astype(F32)) + c_ref[...].astype(F32)

    if row_sharded:
        own_spec = pl.BlockSpec((None, None, tr, tc), lambda i, j, p: (0, p[1], i, j))
    else:
        own_spec = pl.BlockSpec((None, tr, tc), lambda i, j, p: (0, i, p[1] * (C // tc) + j))
    part = lambda k: pl.BlockSpec((None, None, tr, tc), lambda i, j, p, k=k: (k, 0, i, j))
    in_specs, args, aliases = [own_spec, part(0), part(1), part(2)], [place, ps, parts, parts, parts], {}
    if into is not None:
        in_specs.append(pl.BlockSpec(memory_space=pl.ANY))
        args.append(into)
        aliases = {5: 0}
    return pl.pallas_call(
        body, name=name, out_shape=jax.ShapeDtypeStruct((n_layers, 2, R, C), F32),
        grid_spec=pltpu.PrefetchScalarGridSpec(
            num_scalar_prefetch=1, grid=(R // tr, C // tc), in_specs=in_specs,
            out_specs=pl.BlockSpec((None, None, tr, tc), lambda i, j, p: (layer, p[0], i, j))),
        input_output_aliases=aliases,
        compiler_params=pltpu.CompilerParams(dimension_semantics=("parallel", "parallel"),
                                             vmem_limit_bytes=_vmem_limit(12 * tr * tc, 5 * tr * tc * 4)),
    )(*args)


def _share_halves(ghalf):
    names = list(ghalf)
    n_w = len(names)

    def body(*refs):
        src = refs[:n_w]
        dst = refs[n_w:2 * n_w]
        send_sems, recv_sems = refs[2 * n_w:]
        x, y, c, s = _place()
        remote = [pltpu.make_async_remote_copy(src_ref=src[w].at[:, c], dst_ref=dst[w].at[:, c], send_sem=send_sems.at[w],
                                               recv_sem=recv_sems.at[w], device_id=(x, y, 1 - c), device_id_type=MESH)
                  for w in range(n_w)]
        for cp in remote:
            cp.start()
        for cp in remote:
            cp.wait()

    anyspec = pl.BlockSpec(memory_space=pl.ANY)
    out = pl.pallas_call(
        body, name="share_halves", in_specs=[anyspec] * n_w, out_specs=[anyspec] * n_w,
        out_shape=[jax.ShapeDtypeStruct(ghalf[n].shape, F32) for n in names],
        input_output_aliases={w: w for w in range(n_w)},
        scratch_shapes=[pltpu.SemaphoreType.DMA((n_w,)), pltpu.SemaphoreType.DMA((n_w,))],
    )(*[ghalf[n] for n in names])
    return dict(zip(names, out))


def _gather_small(v):
    m_per, n = v.shape

    def body(x_ref, out_ref, send_sems, recv_sems, local_sem):
        x, y, c, s = _place()
        me, sibling = (x, y, c), (x, y, 1 - c)
        chips = [(1 - x, y), (x, 1 - y), (1 - x, 1 - y)]

        def rows(px, py, pc):
            return out_ref.at[pl.ds(pl.multiple_of((4 * px + 2 * py + pc) * m_per, 8), m_per), :]

        def copy(k, block, to, src=None):
            return pltpu.make_async_remote_copy(src_ref=rows(*block) if src is None else src, dst_ref=rows(*block),
                                                send_sem=send_sems.at[k], recv_sem=recv_sems.at[k], device_id=to, device_id_type=MESH)

        mine = pltpu.make_async_copy(x_ref, rows(*me), local_sem)
        mine.start()
        first = [copy(0, me, sibling, src=x_ref)]
        first += [copy(1 + j, me, (*chip, c), src=x_ref) for j, chip in enumerate(chips)]
        for cp in first:
            cp.start()
        passed = [copy(4 + j, (*chip, c), sibling) for j, chip in enumerate(chips)]
        for j, chip in enumerate(chips):
            copy(1 + j, (*chip, c), me).wait_recv()
            passed[j].start()
        copy(0, sibling, me).wait_recv()
        for j, chip in enumerate(chips):
            copy(4 + j, (*chip, 1 - c), me).wait_recv()
        for cp in first + passed:
            cp.wait_send()
        mine.wait()

    return pl.pallas_call(
        body, name="gather_small", out_shape=jax.ShapeDtypeStruct((8 * m_per, n), v.dtype),
        in_specs=[pl.BlockSpec(memory_space=pltpu.VMEM)], out_specs=pl.BlockSpec(memory_space=pltpu.VMEM),
        scratch_shapes=[pltpu.SemaphoreType.DMA((7,)), pltpu.SemaphoreType.DMA((7,)), pltpu.SemaphoreType.DMA],
        compiler_params=pltpu.CompilerParams(vmem_limit_bytes=_vmem_limit(9 * m_per * n * 4)),
    )(v)


def _adamw_math(w, g, m, v):
    m = ADAM_B1 * m + (1.0 - ADAM_B1) * g
    v = ADAM_B2 * v + (1.0 - ADAM_B2) * (g * g)
    m_hat = m / (1.0 - ADAM_B1 ** ADAM_STEP)
    v_hat = v / (1.0 - ADAM_B2 ** ADAM_STEP)
    delta = -ADAM_LR * (m_hat / (jnp.sqrt(v_hat) + ADAM_EPS) + ADAM_WD * w)
    return delta, m, v


def _adamw(name, w, g, m, v):
    shape = w.shape
    C = shape[-1]
    R = math.prod(shape[:-1])
    f = lambda a: a.reshape(R, C)
    delta, nm, nv = _ew(name, lambda w_, g_, m_, v_: list(_adamw_math(w_, g_, m_, v_)), [f(w), f(g), f(m), f(v)], [F32] * 3, R, C)
    return delta.reshape(shape), nm.reshape(shape), nv.reshape(shape)


def _pack_small(d):
    return jnp.concatenate([d[n].reshape(-1, LANES) for n in SMALL], axis=0)


def _unpack_small(flat, like):
    out, r = {}, 0
    for n in SMALL:
        k = like[n].size // LANES
        out[n] = flat[r:r + k].reshape(like[n].shape)
        r += k
    return out


def _small_update(gall, w, m, v):
    M = w.shape[0]
    tr = _pick(M, (552, 276, 184, 96, 48, 24, 8))

    def body(*refs):
        g = refs[0][...]
        for d in range(1, 8):
            g = g + refs[d][...]
        delta, nm, nv = _adamw_math(refs[8][...], g, refs[9][...], refs[10][...])
        refs[11][...] = g
        refs[12][...] = delta
        refs[13][...] = nm
        refs[14][...] = nv

    blk = pl.BlockSpec((tr, LANES), lambda i: (i, 0))
    in_specs = [pl.BlockSpec((tr, LANES), lambda i, d=d: (d * (M // tr) + i, 0)) for d in range(8)] + [blk] * 3
    return pl.pallas_call(
        body, name="small_update", grid=(M // tr,), in_specs=in_specs, out_specs=[blk] * 4,
        out_shape=[jax.ShapeDtypeStruct((M, LANES), F32)] * 4,
        compiler_params=pltpu.CompilerParams(dimension_semantics=("parallel",), vmem_limit_bytes=_vmem_limit(15 * tr * LANES * 4)),
    )(*([gall] * 8), w, m, v)


def _step(x, p, target, w, m, v):
    L = p.shape[0]
    x_i, y_i, c, s = _place()
    shapes = {}
    for n in BIG:
        _, K, N = w[n].shape
        shapes[n] = (4 * K, N) if n in ROW_SHARDED else (K, 4 * N)
    def pieces_of(i):
        return {n: lax.dynamic_slice_in_dim(w[n][i], c * (w[n].shape[1] // 2), w[n].shape[1] // 2, axis=0).astype(BF16)
                for n in BIG}

    first = pieces_of(0)
    head = _gather_weights("gather_weights_0_w_in", {"w_in": first.pop("w_in")}, shapes)
    head, first = lax.optimization_barrier((head, first))
    layers = [{**head, **_gather_weights_async("gather_weights_0", first, shapes)}]
    for i in range(1, L):
        layers.append(_gather_weights_async(f"gather_weights_{i}", pieces_of(i), shapes))
    wf = {n: [layers[i][n] for i in range(L)] for n in BIG}
    small = {n: w[n] for n in SMALL}
    place = jnp.stack([c, s]).astype(jnp.int32)
    reduced = []

    def after_group(i, group, grads):
        tag = f"{i}_{group}"
        views = {n: _halves_view(n, g[None]) for n, g in grads.items()}
        got = _exchange_halves(f"exchange_halves_{tag}", views)
        chip_sum = {n: _shard_view(n, _chip_sum(f"chip_sum_{n}_{i}", views[n], got[n], place), 1) for n in grads}
        reduced.append((i, chip_sum, _scatter_chip_sums_async(f"scatter_chip_sums_{tag}", chip_sum)))
        return chip_sum

    loss_cell, dx, gsmall = _local_step(x[0], p[:, 0], target[0], wf, small, after_group)
    loss = lax.psum(jnp.sum(loss_cell), ("x", "y", "c"))
    ghalf = {n: None for n in BIG}
    done = None
    for i, chip_sum, parts in reduced:
        parts, _ = lax.optimization_barrier((parts, done))
        for n in chip_sum:
            ghalf[n] = _shard_sum(f"shard_sum_{n}_{i}", chip_sum[n], parts[n], place, n in ROW_SHARDED, i, L, ghalf[n])
        done = {n: ghalf[n] for n in chip_sum}
    gfull = _share_halves(ghalf)
    grad, delta, new_m, new_v = {}, {}, {}, {}
    for n in BIG:
        grad[n] = gfull[n].reshape(w[n].shape)
        delta[n], new_m[n], new_v[n] = _adamw(f"adamw_{n}", w[n], grad[n], m[n], v[n])
    gall = _gather_small(_pack_small(gsmall))
    gsum, dsm, nms, nvs = _small_update(gall, _pack_small(small), _pack_small({n: m[n] for n in SMALL}),
                                        _pack_small({n: v[n] for n in SMALL}))
    for dst, flat in ((grad, gsum), (delta, dsm), (new_m, nms), (new_v, nvs)):
        dst.update(_unpack_small(flat, small))
    return loss, dx[None], grad, delta, new_m, new_v


def kernel(x, p, w_in, w_br_attn, w_br_sg, w_out, sg_w, sg_b, sg_ln_g, sg_ln_b, norm_mix, norm_ffn, norm_ple, norm_final, w_ff_gate, w_ff_up, w_ff_down, w_ple_gate, w_ple, loss_target, m_w_in, m_w_br_attn, m_w_br_sg, m_w_out, m_sg_w, m_sg_b, m_sg_ln_g, m_sg_ln_b, m_norm_mix, m_norm_ffn, m_norm_ple, m_norm_final, m_w_ff_gate, m_w_ff_up, m_w_ff_down, m_w_ple_gate, m_w_ple, v_w_in, v_w_br_attn, v_w_br_sg, v_w_out, v_sg_w, v_sg_b, v_sg_ln_g, v_sg_ln_b, v_norm_mix, v_norm_ffn, v_norm_ple, v_norm_final, v_w_ff_gate, v_w_ff_up, v_w_ff_down, v_w_ple_gate, v_w_ple):
    w = dict(w_in=w_in, w_br_attn=w_br_attn, w_br_sg=w_br_sg, w_out=w_out, sg_w=sg_w, sg_b=sg_b, sg_ln_g=sg_ln_g, sg_ln_b=sg_ln_b,
             norm_mix=norm_mix, norm_ffn=norm_ffn, norm_ple=norm_ple, norm_final=norm_final, w_ff_gate=w_ff_gate, w_ff_up=w_ff_up,
             w_ff_down=w_ff_down, w_ple_gate=w_ple_gate, w_ple=w_ple)
    m = dict(w_in=m_w_in, w_br_attn=m_w_br_attn, w_br_sg=m_w_br_sg, w_out=m_w_out, sg_w=m_sg_w, sg_b=m_sg_b, sg_ln_g=m_sg_ln_g,
             sg_ln_b=m_sg_ln_b, norm_mix=m_norm_mix, norm_ffn=m_norm_ffn, norm_ple=m_norm_ple, norm_final=m_norm_final,
             w_ff_gate=m_w_ff_gate, w_ff_up=m_w_ff_up, w_ff_down=m_w_ff_down, w_ple_gate=m_w_ple_gate, w_ple=m_w_ple)
    v = dict(w_in=v_w_in, w_br_attn=v_w_br_attn, w_br_sg=v_w_br_sg, w_out=v_w_out, sg_w=v_sg_w, sg_b=v_sg_b, sg_ln_g=v_sg_ln_g,
             sg_ln_b=v_sg_ln_b, norm_mix=v_norm_mix, norm_ffn=v_norm_ffn, norm_ple=v_norm_ple, norm_final=v_norm_final,
             w_ff_gate=v_w_ff_gate, w_ff_up=v_w_ff_up, w_ff_down=v_w_ff_down, w_ple_gate=v_w_ple_gate, w_ple=v_w_ple)
    loss, grad_x, grad, delta, new_m, new_v = _step(x, p, loss_target, w, m, v)
    return (loss, grad_x, *[grad[n] for n in WEIGHTS], *[delta[n] for n in WEIGHTS], *[new_m[n] for n in WEIGHTS],
            *[new_v[n] for n in WEIGHTS])
```

```python
import functools
import math

import jax
import jax.numpy as jnp
from jax import lax
from jax.experimental import pallas as pl
from jax.experimental.pallas import tpu as pltpu
from jax.experimental.pallas import tpu_sc as plsc

F32 = jnp.float32
BF16 = jnp.bfloat16
MESH = pl.DeviceIdType.MESH

HEAD_DIM = 128
ATTN_GROUPS = ((128, 1), (512, 4), (2048, 16))
N_GROUPS = 3
HEADS = 4
QKV_W = 3 * N_GROUPS * HEADS * HEAD_DIM
ATTN_W = HEADS * HEAD_DIM
SG_CHUNK = 128
SG_GROUPS = 8
SG_W = 1024
RADIUS = 64
ROPE_THETA = 10000.0
NORM_EPS = 1e-6
NEG_INF = -1e30
ADAM_LR, ADAM_B1, ADAM_B2, ADAM_EPS, ADAM_WD, ADAM_STEP = 0.001, 0.9, 0.999, 1e-08, 0.01, 10

VMEM_CAP_V7X = 56 * 1024 * 1024
LANES = 128
EW_TILE_ELEMS = 256 * 1024
MM_VMEM_BUDGET = 44 * 1024 * 1024

GATHER_COLLECTIVE_ID = 1
SCATTER_COLLECTIVE_ID = 2

BIG = ("w_in", "w_br_attn", "w_br_sg", "w_out", "w_ff_gate", "w_ff_up", "w_ff_down", "w_ple_gate", "w_ple")
ROW_SHARDED = ("w_out", "w_ff_down", "w_ple_gate")
GRAD_GROUPS = {"ffn": ("w_ple_gate", "w_ple", "w_ff_down", "w_ff_gate", "w_ff_up"), "mix": ("w_out", "w_br_attn", "w_br_sg", "w_in")}
SMALL = ("sg_w", "sg_b", "sg_ln_g", "sg_ln_b", "norm_mix", "norm_ffn", "norm_ple", "norm_final")
WEIGHTS = ("w_in", "w_br_attn", "w_br_sg", "w_out", "sg_w", "sg_b", "sg_ln_g", "sg_ln_b", "norm_mix", "norm_ffn",
           "norm_ple", "norm_final", "w_ff_gate", "w_ff_up", "w_ff_down", "w_ple_gate", "w_ple")


def _pick(n, prefs):
    for t in prefs:
        if n % t == 0:
            return t
    return n


def _nbytes(shape, dtype):
    return math.prod(shape) * jnp.dtype(dtype).itemsize


def _vmem_limit(block_bytes, temp_bytes=0):
    est = 2 * block_bytes + temp_bytes
    assert est <= VMEM_CAP_V7X, est
    return VMEM_CAP_V7X


def _sigmoid(x):
    return 1.0 / (1.0 + jnp.exp(-x))


_GELU_C = math.sqrt(2.0 / math.pi)


def _gelu(x):
    return 0.5 * x * (1.0 + jnp.tanh(_GELU_C * (x + 0.044715 * (x * x * x))))


def _gelu_grad(x):
    t = jnp.tanh(_GELU_C * (x + 0.044715 * (x * x * x)))
    return 0.5 * (1.0 + t) + 0.5 * x * (1.0 - t * t) * (_GELU_C * (1.0 + 3.0 * 0.044715 * (x * x)))


def _lead(arr, l, blk, idx):
    if arr.ndim == 2:
        return pl.BlockSpec(blk, idx)
    return pl.BlockSpec((None,) + blk, lambda *g: (l,) + idx(*g))


def _k_steps(prods, tm, tn, fixed_bytes):
    for nk in range(1, 129):
        if any(p["K"] % nk or (p["K"] // nk) % LANES for p in prods):
            continue
        if 2 * sum((tm + tn) * (p["K"] // nk) * 2 for p in prods) + fixed_bytes <= MM_VMEM_BUDGET:
            return nk
    raise ValueError("no contraction split fits VMEM")


def _mm(name, prods, M, N, outs, epilogue, tiles=(), rows=(), tm=1024, tn=1024):
    assert M % tm == 0 and N % tn == 0, (name, M, N, tm, tn)
    fixed = 2 * tm * tn * (sum(t["x"].dtype.itemsize for t in tiles) + sum(jnp.dtype(o["dtype"]).itemsize for o in outs))
    fixed += (len(prods) + 2) * tm * tn * 4
    nk = _k_steps(prods, tm, tn, fixed)
    in_specs, args, block_bytes = [], [], 0
    for p in prods:
        if isinstance(p["b"], (list, tuple)):
            p["b"], p["bl"] = p["b"][p["bl"]], None
        K = p["K"]
        assert K % nk == 0, (name, K, nk)
        tk = K // nk
        p["tk"] = tk
        a_off, bk_off, bn_off = p.get("a_off", 0), p.get("bk_off", 0), p.get("bn_off", 0)
        assert bn_off % tn == 0 and bk_off % tk == 0
        if p["mode"] == "nn":
            assert a_off % tk == 0
            a_spec = _lead(p["a"], p.get("al"), (tm, tk), lambda i, j, k, o=a_off // tk: (i, o + k))
            b_spec = _lead(p["b"], p.get("bl"), (tk, tn), lambda i, j, k, ok=bk_off // tk, on=bn_off // tn: (ok + k, on + j))
        elif p["mode"] == "nt":
            assert a_off % tk == 0
            a_spec = _lead(p["a"], p.get("al"), (tm, tk), lambda i, j, k, o=a_off // tk: (i, o + k))
            b_spec = _lead(p["b"], p.get("bl"), (tn, tk), lambda i, j, k, ok=bk_off // tk, on=bn_off // tn: (on + j, ok + k))
        else:
            assert a_off % tm == 0
            a_spec = _lead(p["a"], p.get("al"), (tk, tm), lambda i, j, k, o=a_off // tm: (k, o + i))
            b_spec = _lead(p["b"], p.get("bl"), (tk, tn), lambda i, j, k, on=bn_off // tn: (k, on + j))
        in_specs += [a_spec, b_spec]
        args += [p["a"], p["b"]]
        block_bytes += (tm + tn) * tk * 2
    for t in tiles:
        off = t.get("off", 0)
        assert off % tn == 0
        in_specs.append(_lead(t["x"], t.get("l"), (tm, tn), lambda i, j, k, o=off // tn: (i, o + j)))
        args.append(t["x"])
        block_bytes += tm * tn * t["x"].dtype.itemsize
    for r in rows:
        in_specs.append(pl.BlockSpec((1, tn), lambda i, j, k: (0, j)))
        args.append(r)
    out_shapes, out_specs, aliases = [], [], {}
    for o_i, o in enumerate(outs):
        off = o.get("col_off", 0)
        assert off % tn == 0
        out_shapes.append(jax.ShapeDtypeStruct(o["shape"], o["dtype"]))
        idx = lambda i, j, k, oo=off // tn: (i, oo + j)
        if len(o["shape"]) == 2:
            out_specs.append(pl.BlockSpec((tm, tn), idx))
        else:
            out_specs.append(pl.BlockSpec((None, tm, tn), lambda i, j, k, l=o["l"], f=idx: (l,) + f(i, j, k)))
        if o.get("alias") is not None:
            aliases[len(args)] = o_i
            in_specs.append(pl.BlockSpec(memory_space=pl.ANY))
            args.append(o["alias"])
        block_bytes += tm * tn * jnp.dtype(o["dtype"]).itemsize
    n_p, n_t, n_r, n_o = len(prods), len(tiles), len(rows), len(outs)
    n_alias = len(aliases)
    modes = [p["mode"] for p in prods]

    def body(*refs):
        ab = refs[: 2 * n_p]
        t_refs = refs[2 * n_p: 2 * n_p + n_t]
        r_refs = refs[2 * n_p + n_t: 2 * n_p + n_t + n_r]
        o_refs = refs[2 * n_p + n_t + n_r + n_alias: 2 * n_p + n_t + n_r + n_alias + n_o]
        acc_refs = refs[2 * n_p + n_t + n_r + n_alias + n_o:]
        dims = {"nn": (((1,), (0,)), ((), ())), "nt": (((1,), (1,)), ((), ())), "tn": (((0,), (0,)), ((), ()))}

        def part(q):
            return lax.dot_general(ab[2 * q][...], ab[2 * q + 1][...], dims[modes[q]], preferred_element_type=F32)

        def finish(accs):
            res = epilogue(accs, [t[...] for t in t_refs], [r[...] for r in r_refs])
            for o_ref, val in zip(o_refs, res, strict=True):
                o_ref[...] = val.astype(o_ref.dtype)

        if nk == 1:
            finish([part(q) for q in range(n_p)])
        else:
            k = pl.program_id(2)

            @pl.when(k == 0)
            def _():
                for q, acc in enumerate(acc_refs):
                    acc[...] = part(q)

            @pl.when(k > 0)
            def _():
                for q, acc in enumerate(acc_refs):
                    acc[...] += part(q)

            @pl.when(k == nk - 1)
            def _():
                finish([acc[...] for acc in acc_refs])

    scratch = [pltpu.VMEM((tm, tn), F32) for _ in prods] if nk > 1 else []
    temp = (n_p + 2) * tm * tn * 4
    res = pl.pallas_call(
        body, name=name, grid=(M // tm, N // tn, nk), in_specs=in_specs, out_specs=out_specs, out_shape=out_shapes,
        scratch_shapes=scratch, input_output_aliases=aliases,
        compiler_params=pltpu.CompilerParams(dimension_semantics=("parallel", "parallel", "arbitrary"),
                                             vmem_limit_bytes=_vmem_limit(block_bytes, temp)),
    )(*args)
    return res


def _first(accs, tiles, rows):
    return [accs[0]]


def _ew(name, fn, ins, outs, R, C, tr=None, tc=None):
    tc = tc or _pick(C, (2048, 1536, 1408, 1024, 896, 512, 384, 256, 128))
    tr = tr or _pick(R, [t for t in (512, 256, 128, 64, 32, 16) if t * tc <= EW_TILE_ELEMS] + [8])
    in_specs, args, bb = [], [], 0
    for x in ins:
        if isinstance(x, tuple):
            arr, l = x
            in_specs.append(pl.BlockSpec((None, tr, tc), lambda i, j, l=l: (l, i, j)))
        else:
            arr = x
            in_specs.append(pl.BlockSpec((tr, tc), lambda i, j: (i, j)))
        args.append(arr)
        bb += tr * tc * arr.dtype.itemsize
    out_shapes = [jax.ShapeDtypeStruct((R, C), d) for d in outs]
    out_specs = [pl.BlockSpec((tr, tc), lambda i, j: (i, j)) for _ in outs]
    bb += sum(tr * tc * jnp.dtype(d).itemsize for d in outs)
    n_in = len(ins)

    def body(*refs):
        res = fn(*[r[...] for r in refs[:n_in]])
        for o_ref, val in zip(refs[n_in:], res, strict=True):
            o_ref[...] = val.astype(o_ref.dtype)

    return pl.pallas_call(
        body, name=name, grid=(R // tr, C // tc), in_specs=in_specs, out_specs=out_specs, out_shape=out_shapes,
        compiler_params=pltpu.CompilerParams(dimension_semantics=("parallel", "parallel"),
                                             vmem_limit_bytes=_vmem_limit(bb, 6 * tr * tc * 4)),
    )(*args)


def _rmsnorm_fwd(name, x, g):
    S, D = x.shape
    tr = _pick(S, (256, 128, 64, 8))

    def body(x_ref, g_ref, h_ref):
        xv = x_ref[...]
        r = lax.rsqrt(jnp.mean(xv * xv, axis=-1, keepdims=True) + NORM_EPS)
        h_ref[...] = (xv * r * g_ref[...]).astype(BF16)

    return pl.pallas_call(
        body, name=name, grid=(S // tr,),
        in_specs=[pl.BlockSpec((tr, D), lambda i: (i, 0)), pl.BlockSpec((1, D), lambda i: (0, 0))],
        out_specs=pl.BlockSpec((tr, D), lambda i: (i, 0)), out_shape=jax.ShapeDtypeStruct((S, D), BF16),
        compiler_params=pltpu.CompilerParams(dimension_semantics=("parallel",),
                                             vmem_limit_bytes=_vmem_limit(tr * D * 6, 3 * tr * D * 4)),
    )(x, g)


def _rmsnorm_bwd(name, x, g, dh, dres):
    S, D = x.shape
    tr = _pick(S, (256, 128, 64, 8))

    def body(x_ref, g_ref, dh_ref, dres_ref, dx_ref, dxb_ref, dg_ref):
        xv = x_ref[...]
        dy = dh_ref[...].astype(F32)
        r = lax.rsqrt(jnp.mean(xv * xv, axis=-1, keepdims=True) + NORM_EPS)
        a = dy * g_ref[...]
        dx = dres_ref[...] + r * a - xv * (r * r * r) * jnp.mean(a * xv, axis=-1, keepdims=True)
        dx_ref[...] = dx
        dxb_ref[...] = dx.astype(BF16)
        part = jnp.sum(dy * xv * r, axis=0, keepdims=True)

        @pl.when(pl.program_id(0) == 0)
        def _():
            dg_ref[...] = part

        @pl.when(pl.program_id(0) > 0)
        def _():
            dg_ref[...] += part

    row = pl.BlockSpec((tr, D), lambda i: (i, 0))
    vec = pl.BlockSpec((1, D), lambda i: (0, 0))
    return pl.pallas_call(
        body, name=name, grid=(S // tr,), in_specs=[row, vec, row, row], out_specs=[row, row, vec],
        out_shape=[jax.ShapeDtypeStruct((S, D), F32), jax.ShapeDtypeStruct((S, D), BF16), jax.ShapeDtypeStruct((1, D), F32)],
        compiler_params=pltpu.CompilerParams(dimension_semantics=("arbitrary",),
                                             vmem_limit_bytes=_vmem_limit(tr * D * 18, 5 * tr * D * 4)),
    )(x, g, dh, dres)


def _loss_head(x, g, target):
    S, D = x.shape
    tr = _pick(S, (256, 128, 64, 8))

    def body(x_ref, g_ref, t_ref, loss_ref, dx_ref, dxb_ref, dg_ref):
        xv = x_ref[...]
        r = lax.rsqrt(jnp.mean(xv * xv, axis=-1, keepdims=True) + NORM_EPS)
        xn = xv * r
        diff = xn * g_ref[...] - t_ref[...]
        dy = diff * (1.0 / D)
        a = dy * g_ref[...]
        dx = r * a - xv * (r * r * r) * jnp.mean(a * xv, axis=-1, keepdims=True)
        dx_ref[...] = dx
        dxb_ref[...] = dx.astype(BF16)
        part = jnp.sum(dy * xn, axis=0, keepdims=True)
        cell = (lax.broadcasted_iota(jnp.int32, (8, LANES), 0) == 0) & (lax.broadcasted_iota(jnp.int32, (8, LANES), 1) == 0)
        lpart = jnp.where(cell, 0.5 * jnp.sum(jnp.mean(diff * diff, axis=-1, keepdims=True)), 0.0)

        @pl.when(pl.program_id(0) == 0)
        def _():
            dg_ref[...] = part
            loss_ref[...] = lpart

        @pl.when(pl.program_id(0) > 0)
        def _():
            dg_ref[...] += part
            loss_ref[...] += lpart

    row = pl.BlockSpec((tr, D), lambda i: (i, 0))
    vec = pl.BlockSpec((1, D), lambda i: (0, 0))
    return pl.pallas_call(
        body, name="loss_head", grid=(S // tr,), in_specs=[row, vec, row],
        out_specs=[pl.BlockSpec((8, LANES), lambda i: (0, 0)), row, row, vec],
        out_shape=[jax.ShapeDtypeStruct((8, LANES), F32), jax.ShapeDtypeStruct((S, D), F32),
                   jax.ShapeDtypeStruct((S, D), BF16), jax.ShapeDtypeStruct((1, D), F32)],
        compiler_params=pltpu.CompilerParams(dimension_semantics=("arbitrary",),
                                             vmem_limit_bytes=_vmem_limit(tr * D * 14, 6 * tr * D * 4)),
    )(x, g, target)


def _rope_tables(S):
    pos = jnp.arange(S, dtype=F32)
    inv_freq = ROPE_THETA ** (-jnp.arange(0, HEAD_DIM, 2, dtype=F32) / HEAD_DIM)
    ang = pos[:, None] * inv_freq[None, :]
    cos, sin = jnp.cos(ang), jnp.sin(ang)
    return jnp.concatenate([cos, cos], axis=-1), jnp.concatenate([-sin, sin], axis=-1)


def _rope_fwd(name, z, cosf, sinf):
    S = z.shape[0]
    tr = _pick(S, (256, 128, 64, 8))
    n_q = N_GROUPS * HEADS

    def body(z_ref, c_ref, s_ref, o_ref):
        c, s = c_ref[...], s_ref[...]
        for j in range(QKV_W // HEAD_DIM):
            t = z_ref[:, j * HEAD_DIM:(j + 1) * HEAD_DIM]
            if j < 2 * n_q:
                t = t * c + pltpu.roll(t, HEAD_DIM // 2, axis=1) * s
            if j < n_q:
                t = t * ATTN_SCALE
            o_ref[:, j * HEAD_DIM:(j + 1) * HEAD_DIM] = t.astype(BF16)

    tab = pl.BlockSpec((tr, HEAD_DIM), lambda i: (i, 0))
    return pl.pallas_call(
        body, name=name, grid=(S // tr,), in_specs=[pl.BlockSpec((tr, QKV_W), lambda i: (i, 0)), tab, tab],
        out_specs=pl.BlockSpec((tr, QKV_W), lambda i: (i, 0)), out_shape=jax.ShapeDtypeStruct((S, QKV_W), BF16),
        compiler_params=pltpu.CompilerParams(dimension_semantics=("parallel",),
                                             vmem_limit_bytes=_vmem_limit(tr * QKV_W * 6, tr * QKV_W * 4)),
    )(z, cosf, sinf)


def _rope_bwd(name, dq, dk, dv, cosf, sinf, dz):
    S = dq.shape[0]
    tr = _pick(S, (256, 128, 64, 8))
    W3 = QKV_W // 3
    nh = W3 // HEAD_DIM

    def body(dq_ref, dk_ref, dv_ref, c_ref, s_ref, dz_in, o_ref):
        c, s = c_ref[...], s_ref[...]
        for part, ref in enumerate((dq_ref, dk_ref)):
            for j in range(nh):
                t = ref[:, j * HEAD_DIM:(j + 1) * HEAD_DIM].astype(F32)
                t = t * c - pltpu.roll(t, HEAD_DIM // 2, axis=1) * s
                o_ref[:, part * W3 + j * HEAD_DIM: part * W3 + (j + 1) * HEAD_DIM] = t.astype(BF16)
        o_ref[:, 2 * W3:] = dv_ref[...]

    third = pl.BlockSpec((tr, W3), lambda i: (i, 0))
    tab = pl.BlockSpec((tr, HEAD_DIM), lambda i: (i, 0))
    return pl.pallas_call(
        body, name=name, grid=(S // tr,),
        in_specs=[third, third, third, tab, tab, pl.BlockSpec(memory_space=pl.ANY)],
        out_specs=pl.BlockSpec((tr, QKV_W), lambda i: (i, 0)), out_shape=jax.ShapeDtypeStruct(dz.shape, dz.dtype),
        input_output_aliases={5: 0},
        compiler_params=pltpu.CompilerParams(dimension_semantics=("parallel",),
                                             vmem_limit_bytes=_vmem_limit(tr * QKV_W * 4, tr * QKV_W * 4)),
    )(dq, dk, dv, cosf, sinf, dz)


ATTN_TQ = 256
ATTN_SCALE = HEAD_DIM ** -0.5
ATTN_PAD_MAX = RADIUS * max(d for _, d in ATTN_GROUPS)


def _band_bias(shape, q_axis, d):
    kq = lax.broadcasted_iota(jnp.int32, shape, 1 - q_axis) - lax.broadcasted_iota(jnp.int32, shape, q_axis) - RADIUS * d
    return jnp.where((jnp.abs(kq) <= RADIUS * d) & ((kq & (d - 1)) == 0), 0.0, NEG_INF).astype(F32)


def _fill_padded(dst, src, d, S):
    pad = RADIUS * d
    dst[0:pad, :] = jnp.zeros((pad, HEAD_DIM), dst.dtype)
    dst[pad:pad + S, :] = src[...]
    dst[pad + S:pad + S + pad, :] = jnp.zeros((pad, HEAD_DIM), dst.dtype)


_NT = (((1,), (1,)), ((), ()))


def _attn_fwd(name, qkv):
    S = qkv.shape[0]
    T = ATTN_TQ
    nq = N_GROUPS * HEADS
    widths = [T + 2 * RADIUS * d for _, d in ATTN_GROUPS]

    def body(*refs):
        q_refs, k_refs, v_refs = refs[0:3], refs[3:6], refs[6:9]
        o_ref, lc_ref, lr_ref = refs[9:12]
        kp, vp, bias = refs[12:15], refs[15:18], refs[18:21]
        i0 = pl.multiple_of(pl.program_id(1) * T, T)

        @pl.when(pl.program_id(1) == 0)
        def _():
            for g, (_, d) in enumerate(ATTN_GROUPS):
                _fill_padded(kp[g], k_refs[g], d, S)
                _fill_padded(vp[g], v_refs[g], d, S)
                bias[g][...] = _band_bias((T, widths[g]), 0, d)

        m = jnp.full((T, 1), NEG_INF, F32)
        l = jnp.zeros((T, 1), F32)
        acc = jnp.zeros((T, HEAD_DIM), F32)
        for g, (_, d) in enumerate(ATTN_GROUPS):
            W = widths[g]
            kw = kp[g][pl.ds(i0, W), :]
            vw = vp[g][pl.ds(i0, W), :]
            key = i0 - RADIUS * d + lax.broadcasted_iota(jnp.int32, (1, W), 1)
            in_seq = jnp.where((key >= 0) & (key < S), 0.0, NEG_INF).astype(F32)
            s = lax.dot_general(q_refs[g][...], kw, _NT, preferred_element_type=F32) + bias[g][...] + in_seq
            m_new = jnp.maximum(m, jnp.max(s, axis=1, keepdims=True))
            alpha = jnp.exp(m - m_new)
            p = jnp.exp(s - m_new)
            l = l * alpha + jnp.sum(p, axis=1, keepdims=True)
            acc = acc * alpha + jnp.dot(p.astype(BF16), vw, preferred_element_type=F32)
            m = m_new
        o_ref[...] = (acc / l).astype(BF16)
        lse = m + jnp.log(l)
        lc_ref[...] = lse
        lr_ref[...] = jnp.broadcast_to(lse, (T, LANES)).T[0:1, :]

    in_specs = [pl.BlockSpec((T, HEAD_DIM), lambda h, i, g=g: (i, g * HEADS + h)) for g in range(N_GROUPS)]
    in_specs += [pl.BlockSpec((S, HEAD_DIM), lambda h, i, g=g: (0, nq + g * HEADS + h)) for g in range(N_GROUPS)]
    in_specs += [pl.BlockSpec((S, HEAD_DIM), lambda h, i, g=g: (0, 2 * nq + g * HEADS + h)) for g in range(N_GROUPS)]
    padded = [pltpu.VMEM((S + 2 * RADIUS * d, HEAD_DIM), BF16) for _, d in ATTN_GROUPS]
    scratch = padded + padded + [pltpu.VMEM((T, W), F32) for W in widths]
    scratch_bytes = sum(2 * (S + 2 * RADIUS * d) * HEAD_DIM * 2 for _, d in ATTN_GROUPS) + sum(T * W * 4 for W in widths)
    return pl.pallas_call(
        body, name=name, grid=(HEADS, S // T), in_specs=in_specs,
        out_specs=[pl.BlockSpec((T, HEAD_DIM), lambda h, i: (i, h)), pl.BlockSpec((None, T, 1), lambda h, i: (h, i, 0)),
                   pl.BlockSpec((None, 1, T), lambda h, i: (h, 0, i))],
        out_shape=[jax.ShapeDtypeStruct((S, ATTN_W), BF16), jax.ShapeDtypeStruct((HEADS, S, 1), F32),
                   jax.ShapeDtypeStruct((HEADS, 1, S), F32)],
        scratch_shapes=scratch,
        compiler_params=pltpu.CompilerParams(dimension_semantics=("parallel", "arbitrary"),
                                             vmem_limit_bytes=_vmem_limit(6 * S * HEAD_DIM * 2 + 8 * T * HEAD_DIM * 4,
                                                                          scratch_bytes + 4 * T * widths[-1] * 4)),
    )(*([qkv] * 9))


def _attn_bwd(name, qkv, attn, dattn, lse_c, lse_r):
    S = qkv.shape[0]
    T = ATTN_TQ
    nq = N_GROUPS * HEADS
    W3 = QKV_W // 3
    n_i = S // T
    wmax = T + 2 * ATTN_PAD_MAX
    s_pad = S + 2 * ATTN_PAD_MAX

    def body(q_ref, k_ref, v_ref, o_ref, do_ref, lc_ref, lr_ref, dq_ref, dk_ref, dv_ref, kp, vp, dk_acc, dv_acc, bias, bias_t):
        g_id, i = pl.program_id(1), pl.program_id(2)
        i0 = pl.multiple_of(i * T, T)
        q, do = q_ref[...], do_ref[...]
        dof = do.astype(F32)
        delta_c = jnp.sum(dof * o_ref[...].astype(F32), axis=1, keepdims=True)
        delta_r = jnp.broadcast_to(delta_c, (T, LANES)).T[0:1, :]
        lse_col, lse_row = lc_ref[...], lr_ref[...]

        def group(d):
            W, pad = T + 2 * RADIUS * d, RADIUS * d

            @pl.when(i == 0)
            def _():
                _fill_padded(kp, k_ref, d, S)
                _fill_padded(vp, v_ref, d, S)
                dk_acc[...] = jnp.zeros_like(dk_acc)
                dv_acc[...] = jnp.zeros_like(dv_acc)
                bias[:, 0:W] = _band_bias((T, W), 0, d)
                bias_t[0:W, :] = _band_bias((W, T), 1, d)

            kw = kp[pl.ds(i0, W), :]
            vw = vp[pl.ds(i0, W), :]
            key = i0 - pad + lax.broadcasted_iota(jnp.int32, (1, W), 1)
            in_seq = jnp.where((key >= 0) & (key < S), 0.0, NEG_INF).astype(F32)
            s = lax.dot_general(q, kw, _NT, preferred_element_type=F32) + bias[:, 0:W] + in_seq
            p = jnp.exp(s - lse_col)
            dp = lax.dot_general(do, vw, _NT, preferred_element_type=F32)
            ds = p * (dp - delta_c)
            dq_ref[...] = (jnp.dot(ds.astype(BF16), kw, preferred_element_type=F32) * ATTN_SCALE).astype(BF16)
            st = lax.dot_general(kw, q, _NT, preferred_element_type=F32) + bias_t[0:W, :]
            pt = jnp.exp(st - lse_row)
            dpt = lax.dot_general(vw, do, _NT, preferred_element_type=F32)
            dst = pt * (dpt - delta_r)
            dk_acc[pl.ds(i0, W), :] += jnp.dot(dst.astype(BF16), q, preferred_element_type=F32)
            dv_acc[pl.ds(i0, W), :] += jnp.dot(pt.astype(BF16), do, preferred_element_type=F32)

            @pl.when(i == n_i - 1)
            def _():
                dk_ref[...] = dk_acc[pad:pad + S, :].astype(BF16)
                dv_ref[...] = dv_acc[pad:pad + S, :].astype(BF16)

        for g, (_, d) in enumerate(ATTN_GROUPS):
            pl.when(g_id == g)(functools.partial(group, d))

    tile = lambda off: pl.BlockSpec((T, HEAD_DIM), lambda h, g, i: (i, off + g * HEADS + h))
    full = lambda off: pl.BlockSpec((S, HEAD_DIM), lambda h, g, i: (0, off + g * HEADS + h))
    headt = pl.BlockSpec((T, HEAD_DIM), lambda h, g, i: (i, h))
    scratch_bytes = 2 * s_pad * HEAD_DIM * (2 + 4) + 2 * T * wmax * 4
    return pl.pallas_call(
        body, name=name, grid=(HEADS, N_GROUPS, n_i),
        in_specs=[tile(0), full(nq), full(2 * nq), headt, headt,
                  pl.BlockSpec((None, T, 1), lambda h, g, i: (h, i, 0)), pl.BlockSpec((None, 1, T), lambda h, g, i: (h, 0, i))],
        out_specs=[tile(0), full(0), full(0)],
        out_shape=[jax.ShapeDtypeStruct((S, W3), BF16)] * 3,
        scratch_shapes=[pltpu.VMEM((s_pad, HEAD_DIM), BF16), pltpu.VMEM((s_pad, HEAD_DIM), BF16),
                        pltpu.VMEM((s_pad, HEAD_DIM), F32), pltpu.VMEM((s_pad, HEAD_DIM), F32),
                        pltpu.VMEM((T, wmax), F32), pltpu.VMEM((wmax, T), F32)],
        compiler_params=pltpu.CompilerParams(dimension_semantics=("parallel", "arbitrary", "arbitrary"),
                                             vmem_limit_bytes=_vmem_limit(4 * S * HEAD_DIM * 2 + 8 * T * HEAD_DIM * 4,
                                                                          scratch_bytes + 6 * T * wmax * 4)),
    )(qkv, qkv, qkv, attn, dattn, lse_c, lse_r)


def _sg_parts(u, v, lng, lnb):
    gu = _gelu(u)
    gv = _gelu(v)
    mu = jnp.mean(gv, axis=-1, keepdims=True)
    xc = gv - mu
    rstd = lax.rsqrt(jnp.mean(xc * xc, axis=-1, keepdims=True) + NORM_EPS)
    xhat = xc * rstd
    vn = xhat * lng + lnb
    return gu, xhat, rstd, vn


def _sg_fwd(name, z, sg_w, sg_bc, lng, lnb, o_sg0):
    S = z.shape[0]
    T = SG_CHUNK
    cb = 512
    assert o_sg0 % cb == 0
    b0 = o_sg0 // cb

    def body(u0, u1, v0, v1, w_ref, b_ref, g_ref, be_ref, o_ref):
        u = jnp.concatenate([u0[...], u1[...]], axis=1)
        v = jnp.concatenate([v0[...], v1[...]], axis=1)
        gu, _, _, vn = _sg_parts(u, v, g_ref[...], be_ref[...])
        vnb = vn.astype(BF16)
        for g in range(SG_GROUPS):
            sl = slice(g * SG_CHUNK, (g + 1) * SG_CHUNK)
            mixed = jnp.dot(w_ref[g], vnb[:, sl], preferred_element_type=F32) + b_ref[g]
            o_ref[:, sl] = (gu[:, sl] * mixed).astype(BF16)

    zs = lambda k: pl.BlockSpec((T, cb), lambda i, k=k: (i, b0 + k))
    const3 = lambda shp: pl.BlockSpec(shp, lambda i: (0, 0, 0))
    vec = pl.BlockSpec((1, SG_W), lambda i: (0, 0))
    return pl.pallas_call(
        body, name=name, grid=(S // T,),
        in_specs=[zs(0), zs(1), zs(2), zs(3), const3((SG_GROUPS, SG_CHUNK, SG_CHUNK)), const3((SG_GROUPS, SG_CHUNK, 1)), vec, vec],
        out_specs=pl.BlockSpec((T, SG_W), lambda i: (i, 0)), out_shape=jax.ShapeDtypeStruct((S, SG_W), BF16),
        compiler_params=pltpu.CompilerParams(dimension_semantics=("parallel",), vmem_limit_bytes=_vmem_limit(4 * 1024 * 1024, 8 * T * SG_W * 4)),
    )(z, z, z, z, sg_w, sg_bc, lng, lnb)


def _sg_bwd(name, z, dsg, sg_w, sg_wt, sg_bc, lng, lnb, o_sg0, dz):
    S = z.shape[0]
    T = SG_CHUNK
    cb = 512
    b0 = o_sg0 // cb

    def body(u0, u1, v0, v1, d_ref, w_ref, wt_ref, b_ref, g_ref, be_ref, dz_in, dz_ref, dw_ref, db_ref, dg_ref, dbe_ref, stage):
        i, jj = pl.program_id(0), pl.program_id(1)

        @pl.when(jj == 0)
        def _():
            u = jnp.concatenate([u0[...], u1[...]], axis=1)
            v = jnp.concatenate([v0[...], v1[...]], axis=1)
            gu, xhat, rstd, vn = _sg_parts(u, v, g_ref[...], be_ref[...])
            vnb = vn.astype(BF16)
            dsg_v = d_ref[...].astype(F32)
            dmix = dsg_v * gu
            dmixb = dmix.astype(BF16)
            dvn_parts, mixed_parts, dw_parts, db_parts = [], [], [], []
            for g in range(SG_GROUPS):
                sl = slice(g * SG_CHUNK, (g + 1) * SG_CHUNK)
                mixed_parts.append(jnp.dot(w_ref[g], vnb[:, sl], preferred_element_type=F32) + b_ref[g])
                dvn_parts.append(jnp.dot(wt_ref[g], dmixb[:, sl], preferred_element_type=F32))
                dw_parts.append(lax.dot_general(dmixb[:, sl], vnb[:, sl], _NT, preferred_element_type=F32))
                db_parts.append(jnp.sum(dmix[:, sl], axis=1, keepdims=True))
            mixed = jnp.concatenate(mixed_parts, axis=1)
            dvn = jnp.concatenate(dvn_parts, axis=1)
            dzu = dsg_v * mixed * _gelu_grad(u)
            dxh = dvn * g_ref[...]
            dgv = rstd * (dxh - jnp.mean(dxh, axis=-1, keepdims=True) - xhat * jnp.mean(dxh * xhat, axis=-1, keepdims=True))
            dzv = dgv * _gelu_grad(v)
            stage[0] = dzu[:, :cb].astype(BF16)
            stage[1] = dzu[:, cb:].astype(BF16)
            stage[2] = dzv[:, :cb].astype(BF16)
            stage[3] = dzv[:, cb:].astype(BF16)
            dgp = jnp.sum(dvn * xhat, axis=0, keepdims=True)
            dbp = jnp.sum(dvn, axis=0, keepdims=True)

            @pl.when(i == 0)
            def _():
                for g in range(SG_GROUPS):
                    dw_ref[g] = dw_parts[g]
                    db_ref[g] = db_parts[g]
                dg_ref[...] = dgp
                dbe_ref[...] = dbp

            @pl.when(i > 0)
            def _():
                for g in range(SG_GROUPS):
                    dw_ref[g] += dw_parts[g]
                    db_ref[g] += db_parts[g]
                dg_ref[...] += dgp
                dbe_ref[...] += dbp

        dz_ref[...] = stage[jj]

    zs = lambda k: pl.BlockSpec((T, cb), lambda i, jj, k=k: (i, b0 + k))
    const3 = lambda shp: pl.BlockSpec(shp, lambda i, jj: (0, 0, 0))
    vec = pl.BlockSpec((1, SG_W), lambda i, jj: (0, 0))
    return pl.pallas_call(
        body, name=name, grid=(S // T, 4),
        in_specs=[zs(0), zs(1), zs(2), zs(3), pl.BlockSpec((T, SG_W), lambda i, jj: (i, 0)),
                  const3((SG_GROUPS, SG_CHUNK, SG_CHUNK)), const3((SG_GROUPS, SG_CHUNK, SG_CHUNK)), const3((SG_GROUPS, SG_CHUNK, 1)),
                  vec, vec, pl.BlockSpec(memory_space=pl.ANY)],
        out_specs=[pl.BlockSpec((T, cb), lambda i, jj: (i, b0 + jj)), const3((SG_GROUPS, SG_CHUNK, SG_CHUNK)),
                   const3((SG_GROUPS, SG_CHUNK, 1)), vec, vec],
        out_shape=[jax.ShapeDtypeStruct(dz.shape, dz.dtype), jax.ShapeDtypeStruct((SG_GROUPS, SG_CHUNK, SG_CHUNK), F32),
                   jax.ShapeDtypeStruct((SG_GROUPS, SG_CHUNK, 1), F32), jax.ShapeDtypeStruct((1, SG_W), F32),
                   jax.ShapeDtypeStruct((1, SG_W), F32)],
        scratch_shapes=[pltpu.VMEM((4, T, cb), BF16)],
        input_output_aliases={10: 0},
        compiler_params=pltpu.CompilerParams(dimension_semantics=("arbitrary", "arbitrary"),
                                             vmem_limit_bytes=_vmem_limit(6 * 1024 * 1024, 16 * T * SG_W * 4)),
    )(z, z, z, z, dsg, sg_w, sg_wt, sg_bc, lng, lnb, dz)


def _gate_bwd(name, z, dmerged, y_attn, y_sg, o_g0, in_w):
    S, D = dmerged.shape
    tr = _pick(S, (512, 256, 128, 8))
    cb = _pick(D, (512, 256, 128))
    assert o_g0 % cb == 0
    nd = D // cb
    b0 = o_g0 // cb

    def body(z_ref, dm_ref, ya_ref, ys_ref, dz_ref, dy_ref):
        jj = pl.program_id(1)
        gate = _sigmoid(z_ref[...])
        dm = dm_ref[...].astype(F32)
        y = jnp.where(jj < nd, ya_ref[...], ys_ref[...]).astype(F32)
        dz_ref[...] = (dm * y * gate * (1.0 - gate)).astype(BF16)
        dy_ref[...] = (dm * gate).astype(BF16)

    half = pl.BlockSpec((tr, cb), lambda i, jj: (i, jj % nd))
    return pl.pallas_call(
        body, name=name, grid=(S // tr, 2 * nd),
        in_specs=[pl.BlockSpec((tr, cb), lambda i, jj: (i, b0 + jj)), half, half, half],
        out_specs=[pl.BlockSpec((tr, cb), lambda i, jj: (i, b0 + jj)), pl.BlockSpec((tr, cb), lambda i, jj: (i, jj))],
        out_shape=[jax.ShapeDtypeStruct((S, in_w), BF16), jax.ShapeDtypeStruct((S, 2 * D), BF16)],
        compiler_params=pltpu.CompilerParams(dimension_semantics=("parallel", "arbitrary"),
                                             vmem_limit_bytes=_vmem_limit(tr * cb * 14, 6 * tr * cb * 4)),
    )(z, dmerged, y_attn, y_sg)


def _row(v):
    return v.reshape(1, -1)


def _local_step(x, p, target, wf, small, after_group):
    S, D = x.shape
    L = p.shape[0]
    in_w = wf["w_in"][0].shape[1]
    ff = wf["w_ff_gate"][0].shape[1]
    ple = p.shape[2]
    o_sg0, o_g0 = QKV_W, QKV_W + 2 * SG_W
    cosf, sinf = _rope_tables(S)
    pb = p.astype(BF16)
    tmb = _pick(S, (1024, 512, 256))
    tn_in = _pick(in_w, (768, 1024, 512))
    tn_d = _pick(D, (1024, 512, 256))
    tn_g = _pick(D, (512, 256))
    tn_ff = _pick(ff, (512, 256))

    saved = []
    xs = x
    for i in range(L):
        sv = {"x0": xs}
        h = _rmsnorm_fwd(f"norm_mix_{i}", xs, _row(small["norm_mix"][i]))
        (z,) = _mm(f"in_proj_{i}", [dict(a=h, b=wf["w_in"], bl=i, mode="nn", K=D)], S, in_w,
                   [dict(shape=(S, in_w), dtype=F32)], _first, tm=tmb, tn=tn_in)
        qkv = _rope_fwd(f"rope_{i}", z, cosf, sinf)
        attn, lse_c, lse_r = _attn_fwd(f"attn_{i}", qkv)
        sgw = small["sg_w"][i].astype(BF16)
        sgbc = small["sg_b"][i].reshape(SG_GROUPS, SG_CHUNK, 1)
        sg = _sg_fwd(f"sgu_{i}", z, sgw, sgbc, _row(small["sg_ln_g"][i]), _row(small["sg_ln_b"][i]), o_sg0)

        def merge(accs, tiles, rows):
            ya, ys = accs[0].astype(BF16), accs[1].astype(BF16)
            g0, g1 = _sigmoid(tiles[0]), _sigmoid(tiles[1])
            return [ya, ys, g0 * ya.astype(F32) + g1 * ys.astype(F32)]

        y_attn, y_sg, merged = _mm(
            f"branches_{i}",
            [dict(a=attn, b=wf["w_br_attn"], bl=i, mode="nn", K=ATTN_W), dict(a=sg, b=wf["w_br_sg"], bl=i, mode="nn", K=SG_W)],
            S, D, [dict(shape=(S, D), dtype=BF16)] * 3, merge,
            tiles=[dict(x=z, off=o_g0), dict(x=z, off=o_g0 + D)], tm=tmb, tn=tn_g)
        (x1,) = _mm(f"out_proj_{i}", [dict(a=merged, b=wf["w_out"], bl=i, mode="nn", K=D)], S, D,
                    [dict(shape=(S, D), dtype=F32)], lambda a, t, r: [t[0] + a[0]], tiles=[dict(x=xs)], tm=tmb, tn=tn_d)
        h2 = _rmsnorm_fwd(f"norm_ffn_{i}", x1, _row(small["norm_ffn"][i]))

        def swiglu(accs, tiles, rows):
            fg = accs[0].astype(BF16).astype(F32)
            fu = accs[1].astype(BF16).astype(F32)
            return [fg, fu, fg * _sigmoid(fg) * fu]

        ffg, ffu, act = _mm(
            f"ff_in_{i}",
            [dict(a=h2, b=wf["w_ff_gate"], bl=i, mode="nn", K=D), dict(a=h2, b=wf["w_ff_up"], bl=i, mode="nn", K=D)],
            S, ff, [dict(shape=(S, ff), dtype=BF16)] * 3, swiglu, tm=tmb, tn=tn_ff)
        (x2,) = _mm(f"ff_out_{i}", [dict(a=act, b=wf["w_ff_down"], bl=i, mode="nn", K=ff)], S, D,
                    [dict(shape=(S, D), dtype=F32)], lambda a, t, r: [t[0] + a[0]], tiles=[dict(x=x1)], tm=tmb, tn=tn_d)
        h3 = _rmsnorm_fwd(f"norm_ple_{i}", x2, _row(small["norm_ple"][i]))

        def ple_mix(accs, tiles, rows):
            gp = _sigmoid(accs[0]).astype(BF16)
            pe = accs[1].astype(BF16)
            return [tiles[0] + gp.astype(F32) * pe.astype(F32), gp, pe]

        x3, gp, pe = _mm(
            f"ple_{i}",
            [dict(a=h3, b=wf["w_ple_gate"], bl=i, mode="nn", K=D), dict(a=pb, al=i, b=wf["w_ple"], bl=i, mode="nn", K=ple)],
            S, D, [dict(shape=(S, D), dtype=F32), dict(shape=(S, D), dtype=BF16), dict(shape=(S, D), dtype=BF16)], ple_mix,
            tiles=[dict(x=x2)], tm=tmb, tn=tn_g)
        sv.update(h=h, z=z, qkv=qkv, attn=attn, lse_c=lse_c, lse_r=lse_r, sg=sg, y_attn=y_attn, y_sg=y_sg, merged=merged,
                  x1=x1, h2=h2, ffg=ffg, ffu=ffu, act=act, x2=x2, h3=h3, gp=gp, pe=pe, sgw=sgw, sgbc=sgbc)
        saved.append(sv)
        xs = x3

    loss_cell, dx, dxb, dg_final = _loss_head(xs, _row(small["norm_final"]), target)

    gw = {n: [None] * L for n in BIG}
    gs = {n: [None] * L for n in SMALL if n != "norm_final"}

    def dw(n, i, a, a_off, b, bn_off, K_rows, N_cols, tm, tn):
        (gw[n][i],) = _mm(f"d_{n}_{i}", [dict(a=a, b=b, mode="tn", K=S, a_off=a_off, bn_off=bn_off)], K_rows, N_cols,
                          [dict(shape=(K_rows, N_cols), dtype=BF16)], _first, tm=tm, tn=tn)

    for i in reversed(range(L)):
        sv = saved[i]
        dpre, dpe = _ew(f"ple_gate_bwd_{i}",
                        lambda d, g, e: [d * e.astype(F32) * g.astype(F32) * (1.0 - g.astype(F32)), d * g.astype(F32)],
                        [dx, sv["gp"], sv["pe"]], [BF16, BF16], S, D)
        (dh3,) = _mm(f"d_h3_{i}", [dict(a=dpre, b=wf["w_ple_gate"], bl=i, mode="nt", K=D)], S, D,
                     [dict(shape=(S, D), dtype=F32)], _first, tm=tmb, tn=tn_d)
        dw("w_ple_gate", i, sv["h3"], 0, dpre, 0, D, D, tn_d, tn_d)
        dw("w_ple", i, pb[i], 0, dpe, 0, ple, D, _pick(ple, (256, 128)), _pick(D, (2048, 1024, 512, 256)))
        dx, dxb, gs["norm_ple"][i] = _rmsnorm_bwd(f"norm_ple_bwd_{i}", sv["x2"], _row(small["norm_ple"][i]), dh3, dx)
        def swiglu_bwd(accs, tiles, rows):
            da = accs[0].astype(BF16).astype(F32)
            fg, fu = tiles[0].astype(F32), tiles[1].astype(F32)
            sg_ = _sigmoid(fg)
            return [da * fu * (sg_ * (1.0 + fg * (1.0 - sg_))), da * (fg * sg_)]

        dffg, dffu = _mm(f"d_act_{i}", [dict(a=dxb, b=wf["w_ff_down"], bl=i, mode="nt", K=D)], S, ff,
                         [dict(shape=(S, ff), dtype=BF16)] * 2, swiglu_bwd, tiles=[dict(x=sv["ffg"]), dict(x=sv["ffu"])],
                         tm=tmb, tn=tn_ff)
        dw("w_ff_down", i, sv["act"], 0, dxb, 0, ff, D, tn_ff, _pick(D, (2048, 1024, 512, 256)))
        dw("w_ff_gate", i, sv["h2"], 0, dffg, 0, D, ff, _pick(D, (2048, 1024, 512, 256)), tn_ff)
        dw("w_ff_up", i, sv["h2"], 0, dffu, 0, D, ff, _pick(D, (2048, 1024, 512, 256)), tn_ff)
        (dffg, dffu), _ = lax.optimization_barrier(((dffg, dffu), after_group(i, "ffn", {n: gw[n][i] for n in GRAD_GROUPS["ffn"]})))
        (dh2,) = _mm(f"d_h2_{i}", [dict(a=dffg, b=wf["w_ff_gate"], bl=i, mode="nt", K=ff),
                                   dict(a=dffu, b=wf["w_ff_up"], bl=i, mode="nt", K=ff)], S, D,
                     [dict(shape=(S, D), dtype=F32)], lambda a, t, r: [a[0] + a[1]], tm=tmb, tn=tn_d)
        dx, dxb, gs["norm_ffn"][i] = _rmsnorm_bwd(f"norm_ffn_bwd_{i}", sv["x1"], _row(small["norm_ffn"][i]), dh2, dx)
        (dmerged,) = _mm(f"d_merged_{i}", [dict(a=dxb, b=wf["w_out"], bl=i, mode="nt", K=D)], S, D,
                         [dict(shape=(S, D), dtype=BF16)], _first, tm=tmb, tn=tn_d)
        dw("w_out", i, sv["merged"], 0, dxb, 0, D, D, tn_d, tn_d)
        dz, dy = _gate_bwd(f"gate_bwd_{i}", sv["z"], dmerged, sv["y_attn"], sv["y_sg"], o_g0, in_w)
        (dattn,) = _mm(f"d_attn_{i}", [dict(a=dy, b=wf["w_br_attn"], bl=i, mode="nt", K=D)], S, ATTN_W,
                       [dict(shape=(S, ATTN_W), dtype=BF16)], _first, tm=tmb, tn=ATTN_W)
        (dsg,) = _mm(f"d_sg_{i}", [dict(a=dy, a_off=D, b=wf["w_br_sg"], bl=i, mode="nt", K=D)], S, SG_W,
                     [dict(shape=(S, SG_W), dtype=BF16)], _first, tm=tmb, tn=SG_W)
        dw("w_br_attn", i, sv["attn"], 0, dy, 0, ATTN_W, D, ATTN_W, _pick(D, (2048, 1024, 512, 256)))
        dw("w_br_sg", i, sv["sg"], 0, dy, D, SG_W, D, SG_W, _pick(D, (1024, 512, 256)))
        sgwt = jnp.swapaxes(small["sg_w"][i], 1, 2).astype(BF16)
        dz, gs["sg_w"][i], dsgb, dlg, dlb = _sg_bwd(f"sgu_bwd_{i}", sv["z"], dsg, sv["sgw"], sgwt, sv["sgbc"],
                                                    _row(small["sg_ln_g"][i]), _row(small["sg_ln_b"][i]), o_sg0, dz)
        gs["sg_b"][i], gs["sg_ln_g"][i], gs["sg_ln_b"][i] = dsgb.reshape(SG_GROUPS, SG_CHUNK), dlg[0], dlb[0]
        dq, dk, dv = _attn_bwd(f"attn_bwd_{i}", sv["qkv"], sv["attn"], dattn, sv["lse_c"], sv["lse_r"])
        dz = _rope_bwd(f"rope_bwd_{i}", dq, dk, dv, cosf, sinf, dz)
        dw("w_in", i, sv["h"], 0, dz, 0, D, in_w, tn_d, tn_in)
        dz, _ = lax.optimization_barrier((dz, after_group(i, "mix", {n: gw[n][i] for n in GRAD_GROUPS["mix"]})))
        (dh,) = _mm(f"d_h_{i}", [dict(a=dz, b=wf["w_in"], bl=i, mode="nt", K=in_w)], S, D,
                    [dict(shape=(S, D), dtype=F32)], _first, tm=tmb, tn=tn_d)
        dx, dxb, gs["norm_mix"][i] = _rmsnorm_bwd(f"norm_mix_bwd_{i}", sv["x0"], _row(small["norm_mix"][i]), dh, dx)

    gsmall ={n: jnp.stack([jnp.reshape(v, small[n].shape[1:]) for v in gs[n]]) for n in gs}
    gsmall["norm_final"] = dg_final[0]
    return loss_cell, dx, gsmall


def _place():
    x, y, c = lax.axis_index("x"), lax.axis_index("y"), lax.axis_index("c")
    return x, y, c, 2 * x + y


def _chip_of(s):
    return s // 2, s % 2


def _aligned(v, m):
    return v if isinstance(v, int) else pl.multiple_of(v, m)


def _piece(name, shape, s, c):
    K, N = shape
    if name in ROW_SHARDED:
        ks = K // 4
        return s * ks + c * (ks // 2), ks // 2, 0, N
    ns = N // 4
    return c * (K // 2), K // 2, s * ns, ns


def _handshake(peers):
    barrier = pltpu.get_barrier_semaphore()
    for peer in peers:
        pl.semaphore_signal(barrier, inc=1, device_id=peer, device_id_type=MESH)
    pl.semaphore_wait(barrier, len(peers))


def _gather_body(names, shapes, src, dst, send_sems, recv_sems, local_sems):
    n_w = len(names)
    x, y, c, s = _place()
    sib = (x, y, 1 - c)
    rel = [1, 2, 3]

    def where(w, ps, pc):
        r0, nr, c0, nc = _piece(names[w], shapes[names[w]], ps, pc)
        return dst[w].at[pl.ds(_aligned(r0, 16), nr), pl.ds(_aligned(c0, LANES), nc)]

    def copy(w, k, ps, pc, to, from_src=False):
        return pltpu.make_async_remote_copy(
            src_ref=src[w] if from_src else where(w, ps, pc), dst_ref=where(w, ps, pc),
            send_sem=send_sems.at[w, k], recv_sem=recv_sems.at[w, k], device_id=to, device_id_type=MESH)

    mine, first, passed = [], [], []
    for w in range(n_w):
        cp = pltpu.make_async_copy(src[w], where(w, s, c), local_sems.at[w])
        cp.start()
        mine.append(cp)
        first.append(copy(w, 0, s, c, sib, from_src=True))
        for j in rel:
            first.append(copy(w, j, s, c, (*_chip_of(s ^ j), c), from_src=True))
    for cp in first:
        cp.start()
    for w in range(n_w):
        for j in rel:
            copy(w, j, s ^ j, c, sib).wait_recv()
            fw = copy(w, 3 + j, s ^ j, c, sib)
            fw.start()
            passed.append(fw)
    for w in range(n_w):
        copy(w, 0, s, 1 - c, sib).wait_recv()
        for j in rel:
            copy(w, 3 + j, s ^ j, 1 - c, sib).wait_recv()
    for cp in first + passed:
        cp.wait_send()
    for cp in mine:
        cp.wait()


def _gather_sems(n_w):
    return (pltpu.SemaphoreType.DMA((n_w, 7)), pltpu.SemaphoreType.DMA((n_w, 7)), pltpu.SemaphoreType.DMA((n_w,)))


def _gather_peers():
    x, y, c, s = _place()
    return [(x, y, 1 - c)] + [(*_chip_of(s ^ j), c) for j in (1, 2, 3)]


def _gather_weights(name, pieces, shapes):
    names = list(pieces)
    n_w = len(names)

    def body(*refs):
        _gather_body(names, shapes, refs[:n_w], refs[n_w:2 * n_w], *refs[2 * n_w:])

    anyspec = pl.BlockSpec(memory_space=pl.ANY)
    out = pl.pallas_call(
        body, name=name, in_specs=[anyspec] * n_w, out_specs=[anyspec] * n_w,
        out_shape=[jax.ShapeDtypeStruct(tuple(shapes[n]), BF16) for n in names], scratch_shapes=list(_gather_sems(n_w)),
    )(*[pieces[n] for n in names])
    return dict(zip(names, out))


def _gather_weights_async(name, pieces, shapes):
    names = list(pieces)
    n_w = len(names)
    src = [jax.new_ref(pieces[n], memory_space=pltpu.MemorySpace.HBM) for n in names]
    dst = [jax.empty_ref(jax.ShapeDtypeStruct(tuple(shapes[n]), BF16), memory_space=pltpu.MemorySpace.HBM) for n in names]

    @pl.kernel(mesh=plsc.ScalarSubcoreMesh(axis_name="seq", num_cores=1), name=name, scratch_types=_gather_sems(n_w),
               compiler_params=pltpu.CompilerParams(collective_id=GATHER_COLLECTIVE_ID))
    def launch(send_sems, recv_sems, local_sems):
        _handshake(_gather_peers())
        _gather_body(names, shapes, src, dst, send_sems, recv_sems, local_sems)

    launch()
    return {n: d[...] for n, d in zip(names, dst)}


def _halves_view(name, g):
    L, K, N = g.shape
    if name in ROW_SHARDED:
        return g.reshape(L * 4, 2, K // 8, N)
    return g.reshape(L, 2, K // 2, N)


def _exchange_halves(name, views):
    names = list(views)
    n_w = len(names)

    def body(*refs):
        src = refs[:n_w]
        got = refs[n_w:2 * n_w]
        send_sems, recv_sems = refs[2 * n_w:]
        x, y, c, s = _place()
        remote = [pltpu.make_async_remote_copy(src_ref=src[w].at[:, 1 - c], dst_ref=got[w], send_sem=send_sems.at[w],
                                               recv_sem=recv_sems.at[w], device_id=(x, y, 1 - c), device_id_type=MESH)
                  for w in range(n_w)]
        for cp in remote:
            cp.start()
        for cp in remote:
            cp.wait()

    anyspec = pl.BlockSpec(memory_space=pl.ANY)
    out = pl.pallas_call(
        body, name=name, in_specs=[anyspec] * n_w, out_specs=[anyspec] * n_w,
        out_shape=[jax.ShapeDtypeStruct((v.shape[0],) + v.shape[2:], BF16) for v in views.values()],
        scratch_shapes=[pltpu.SemaphoreType.DMA((n_w,)), pltpu.SemaphoreType.DMA((n_w,))],
    )(*views.values())
    return dict(zip(names, out))


def _chip_sum(name, view, got, place):
    A, _, R, C = view.shape
    tc = _pick(C, (2048, 1536, 1408, 1024, 512, 256, 128))
    tr = _pick(R, [t for t in (1024, 512, 256, 128, 64, 32, 16) if t * tc <= 4 * EW_TILE_ELEMS] + [8])

    def body(p_ref, own_ref, got_ref, o_ref):
        o_ref[...] = (own_ref[...].astype(F32) + got_ref[...].astype(F32)).astype(BF16)

    flat = pl.BlockSpec((None, tr, tc), lambda a, i, j, p: (a, i, j))
    return pl.pallas_call(
        body, name=name, out_shape=jax.ShapeDtypeStruct((A, R, C), BF16),
        grid_spec=pltpu.PrefetchScalarGridSpec(
            num_scalar_prefetch=1, grid=(A, R // tr, C // tc),
            in_specs=[pl.BlockSpec((None, None, tr, tc), lambda a, i, j, p: (a, p[0], i, j)), flat], out_specs=flat),
        compiler_params=pltpu.CompilerParams(dimension_semantics=("parallel", "parallel", "parallel"),
                                             vmem_limit_bytes=_vmem_limit(6 * tr * tc, 3 * tr * tc * 4)),
    )(place, view, got)


def _shard_view(name, ps, L):
    return ps.reshape(L, 4, *ps.shape[1:]) if name in ROW_SHARDED else ps


def _scatter_body(names, src, dst, send_sems, recv_sems):
    x, y, c, s = _place()

    def shard(w, t):
        if names[w] in ROW_SHARDED:
            return src[w].at[:, t]
        ns = src[w].shape[2] // 4
        return src[w].at[:, :, pl.ds(pl.multiple_of(t * ns, LANES), ns)]

    remote = []
    for w in range(len(names)):
        for j in (1, 2, 3):
            remote.append(pltpu.make_async_remote_copy(
                src_ref=shard(w, s ^ j), dst_ref=dst[w].at[j - 1], send_sem=send_sems.at[w, j - 1],
                recv_sem=recv_sems.at[w, j - 1], device_id=(*_chip_of(s ^ j), c), device_id_type=MESH))
    for cp in remote:
        cp.start()
    for cp in remote:
        cp.wait()


def _scatter_out_shape(name, v):
    return (3, v[0], v[2], v[3]) if name in ROW_SHARDED else (3, v[0], v[1], v[2] // 4)


def _scatter_sems(n_w):
    return (pltpu.SemaphoreType.DMA((n_w, 3)), pltpu.SemaphoreType.DMA((n_w, 3)))


def _scatter_chip_sums_async(name, psum):
    names = list(psum)
    n_w = len(names)
    src = [jax.new_ref(psum[n], memory_space=pltpu.MemorySpace.HBM) for n in names]
    dst = [jax.empty_ref(jax.ShapeDtypeStruct(_scatter_out_shape(n, psum[n].shape), BF16), memory_space=pltpu.MemorySpace.HBM)
           for n in names]

    @pl.kernel(mesh=plsc.ScalarSubcoreMesh(axis_name="seq", num_cores=1), name=name, scratch_types=_scatter_sems(n_w),
               compiler_params=pltpu.CompilerParams(collective_id=SCATTER_COLLECTIVE_ID))
    def launch(send_sems, recv_sems):
        _handshake(_gather_peers()[1:])
        _scatter_body(names, src, dst, send_sems, recv_sems)

    launch()
    return {n: d[...] for n, d in zip(names, dst)}


def _shard_sum(name, ps, parts, place, row_sharded, layer, n_layers, into):
    _, _, R, C = parts.shape
    tc = _pick(C, (2048, 1408, 1024, 896, 512, 384, 256, 128))
    tr = _pick(R, [t for t in (1024, 512, 256, 128, 64, 32, 16) if t * tc <= 2 * EW_TILE_ELEMS] + [8])

    def body(p_ref, own_ref, a_ref, b_ref, c_ref, *rest):
        o_ref = rest[-1]
        o_ref[...] = ((own_ref[...].astype(F32) + a_ref[...].astype(F32)) + b_ref[...].astype(F32)) + c_ref[...].astype(F32)

    if row_sharded:
        own_spec = pl.BlockSpec((None, None, tr, tc), lambda i, j, p: (0, p[1], i, j))
    else:
        own_spec = pl.BlockSpec((None, tr, tc), lambda i, j, p: (0, i, p[1] * (C // tc) + j))
    part = lambda k: pl.BlockSpec((None, None, tr, tc), lambda i, j, p, k=k: (k, 0, i, j))
    in_specs, args, aliases = [own_spec, part(0), part(1), part(2)], [place, ps, parts, parts, parts], {}
    if into is not None:
        in_specs.append(pl.BlockSpec(memory_space=pl.ANY))
        args.append(into)
        aliases = {5: 0}
    return pl.pallas_call(
        body, name=name, out_shape=jax.ShapeDtypeStruct((n_layers, 2, R, C), F32),
        grid_spec=pltpu.PrefetchScalarGridSpec(
            num_scalar_prefetch=1, grid=(R // tr, C // tc), in_specs=in_specs,
            out_specs=pl.BlockSpec((None, None, tr, tc), lambda i, j, p: (layer, p[0], i, j))),
        input_output_aliases=aliases,
        compiler_params=pltpu.CompilerParams(dimension_semantics=("parallel", "parallel"),
                                             vmem_limit_bytes=_vmem_limit(12 * tr * tc, 5 * tr * tc * 4)),
    )(*args)


def _share_halves(ghalf):
    names = list(ghalf)
    n_w = len(names)

    def body(*refs):
        src = refs[:n_w]
        dst = refs[n_w:2 * n_w]
        send_sems, recv_sems = refs[2 * n_w:]
        x, y, c, s = _place()
        remote = [pltpu.make_async_remote_copy(src_ref=src[w].at[:, c], dst_ref=dst[w].at[:, c], send_sem=send_sems.at[w],
                                               recv_sem=recv_sems.at[w], device_id=(x, y, 1 - c), device_id_type=MESH)
                  for w in range(n_w)]
        for cp in remote:
            cp.start()
        for cp in remote:
            cp.wait()

    anyspec = pl.BlockSpec(memory_space=pl.ANY)
    out = pl.pallas_call(
        body, name="share_halves", in_specs=[anyspec] * n_w, out_specs=[anyspec] * n_w,
        out_shape=[jax.ShapeDtypeStruct(ghalf[n].shape, F32) for n in names],
        input_output_aliases={w: w for w in range(n_w)},
        scratch_shapes=[pltpu.SemaphoreType.DMA((n_w,)), pltpu.SemaphoreType.DMA((n_w,))],
    )(*[ghalf[n] for n in names])
    return dict(zip(names, out))


def _gather_small(v):
    m_per, n = v.shape

    def body(x_ref, out_ref, send_sems, recv_sems, local_sem):
        x, y, c, s = _place()
        me, sibling = (x, y, c), (x, y, 1 - c)
        chips = [(1 - x, y), (x, 1 - y), (1 - x, 1 - y)]

        def rows(px, py, pc):
            return out_ref.at[pl.ds(pl.multiple_of((4 * px + 2 * py + pc) * m_per, 8), m_per), :]

        def copy(k, block, to, src=None):
            return pltpu.make_async_remote_copy(src_ref=rows(*block) if src is None else src, dst_ref=rows(*block),
                                                send_sem=send_sems.at[k], recv_sem=recv_sems.at[k], device_id=to, device_id_type=MESH)

        mine = pltpu.make_async_copy(x_ref, rows(*me), local_sem)
        mine.start()
        first = [copy(0, me, sibling, src=x_ref)]
        first += [copy(1 + j, me, (*chip, c), src=x_ref) for j, chip in enumerate(chips)]
        for cp in first:
            cp.start()
        passed = [copy(4 + j, (*chip, c), sibling) for j, chip in enumerate(chips)]
        for j, chip in enumerate(chips):
            copy(1 + j, (*chip, c), me).wait_recv()
            passed[j].start()
        copy(0, sibling, me).wait_recv()
        for j, chip in enumerate(chips):
            copy(4 + j, (*chip, 1 - c), me).wait_recv()
        for cp in first + passed:
            cp.wait_send()
        mine.wait()

    return pl.pallas_call(
        body, name="gather_small", out_shape=jax.ShapeDtypeStruct((8 * m_per, n), v.dtype),
        in_specs=[pl.BlockSpec(memory_space=pltpu.VMEM)], out_specs=pl.BlockSpec(memory_space=pltpu.VMEM),
        scratch_shapes=[pltpu.SemaphoreType.DMA((7,)), pltpu.SemaphoreType.DMA((7,)), pltpu.SemaphoreType.DMA],
        compiler_params=pltpu.CompilerParams(vmem_limit_bytes=_vmem_limit(9 * m_per * n * 4)),
    )(v)


def _adamw_math(w, g, m, v):
    m = ADAM_B1 * m + (1.0 - ADAM_B1) * g
    v = ADAM_B2 * v + (1.0 - ADAM_B2) * (g * g)
    m_hat = m / (1.0 - ADAM_B1 ** ADAM_STEP)
    v_hat = v / (1.0 - ADAM_B2 ** ADAM_STEP)
    delta = -ADAM_LR * (m_hat / (jnp.sqrt(v_hat) + ADAM_EPS) + ADAM_WD * w)
    return delta, m, v


def _adamw(name, w, g, m, v):
    shape = w.shape
    C = shape[-1]
    R = math.prod(shape[:-1])
    f = lambda a: a.reshape(R, C)
    delta, nm, nv = _ew(name, lambda w_, g_, m_, v_: list(_adamw_math(w_, g_, m_, v_)), [f(w), f(g), f(m), f(v)], [F32] * 3, R, C)
    return delta.reshape(shape), nm.reshape(shape), nv.reshape(shape)


def _pack_small(d):
    return jnp.concatenate([d[n].reshape(-1, LANES) for n in SMALL], axis=0)


def _unpack_small(flat, like):
    out, r = {}, 0
    for n in SMALL:
        k = like[n].size // LANES
        out[n] = flat[r:r + k].reshape(like[n].shape)
        r += k
    return out


def _small_update(gall, w, m, v):
    M = w.shape[0]
    tr = _pick(M, (552, 276, 184, 96, 48, 24, 8))

    def body(*refs):
        g = refs[0][...]
        for d in range(1, 8):
            g = g + refs[d][...]
        delta, nm, nv = _adamw_math(refs[8][...], g, refs[9][...], refs[10][...])
        refs[11][...] = g
        refs[12][...] = delta
        refs[13][...] = nm
        refs[14][...] = nv

    blk = pl.BlockSpec((tr, LANES), lambda i: (i, 0))
    in_specs = [pl.BlockSpec((tr, LANES), lambda i, d=d: (d * (M // tr) + i, 0)) for d in range(8)] + [blk] * 3
    return pl.pallas_call(
        body, name="small_update", grid=(M // tr,), in_specs=in_specs, out_specs=[blk] * 4,
        out_shape=[jax.ShapeDtypeStruct((M, LANES), F32)] * 4,
        compiler_params=pltpu.CompilerParams(dimension_semantics=("parallel",), vmem_limit_bytes=_vmem_limit(15 * tr * LANES * 4)),
    )(*([gall] * 8), w, m, v)


def _step(x, p, target, w, m, v):
    L = p.shape[0]
    x_i, y_i, c, s = _place()
    shapes = {}
    for n in BIG:
        _, K, N = w[n].shape
        shapes[n] = (4 * K, N) if n in ROW_SHARDED else (K, 4 * N)
    def pieces_of(i):
        return {n: lax.dynamic_slice_in_dim(w[n][i], c * (w[n].shape[1] // 2), w[n].shape[1] // 2, axis=0).astype(BF16)
                for n in BIG}

    first = pieces_of(0)
    head = _gather_weights("gather_weights_0_w_in", {"w_in": first.pop("w_in")}, shapes)
    head, first = lax.optimization_barrier((head, first))
    layers = [{**head, **_gather_weights_async("gather_weights_0", first, shapes)}]
    for i in range(1, L):
        layers.append(_gather_weights_async(f"gather_weights_{i}", pieces_of(i), shapes))
    wf = {n: [layers[i][n] for i in range(L)] for n in BIG}
    small = {n: w[n] for n in SMALL}
    place = jnp.stack([c, s]).astype(jnp.int32)
    reduced = []

    def after_group(i, group, grads):
        tag = f"{i}_{group}"
        views = {n: _halves_view(n, g[None]) for n, g in grads.items()}
        got = _exchange_halves(f"exchange_halves_{tag}", views)
        chip_sum = {n: _shard_view(n, _chip_sum(f"chip_sum_{n}_{i}", views[n], got[n], place), 1) for n in grads}
        reduced.append((i, chip_sum, _scatter_chip_sums_async(f"scatter_chip_sums_{tag}", chip_sum)))
        return chip_sum

    loss_cell, dx, gsmall = _local_step(x[0], p[:, 0], target[0], wf, small, after_group)
    loss = lax.psum(jnp.sum(loss_cell), ("x", "y", "c"))
    ghalf = {n: None for n in BIG}
    done = None
    for i, chip_sum, parts in reduced:
        parts, _ = lax.optimization_barrier((parts, done))
        for n in chip_sum:
            ghalf[n] = _shard_sum(f"shard_sum_{n}_{i}", chip_sum[n], parts[n], place, n in ROW_SHARDED, i, L, ghalf[n])
        done = {n: ghalf[n] for n in chip_sum}
    gfull = _share_halves(ghalf)
    grad, delta, new_m, new_v = {}, {}, {}, {}
    for n in BIG:
        grad[n] = gfull[n].reshape(w[n].shape)
        delta[n], new_m[n], new_v[n] = _adamw(f"adamw_{n}", w[n], grad[n], m[n], v[n])
    gall = _gather_small(_pack_small(gsmall))
    gsum, dsm, nms, nvs = _small_update(gall, _pack_small(small), _pack_small({n: m[n] for n in SMALL}),
                                        _pack_small({n: v[n] for n in SMALL}))
    for dst, flat in ((grad, gsum), (delta, dsm), (new_m, nms), (new_v, nvs)):
        dst.update(_unpack_small(flat, small))
    return loss, dx[None], grad, delta, new_m, new_v


def kernel(x, p, w_in, w_br_attn, w_br_sg, w_out, sg_w, sg_b, sg_ln_g, sg_ln_b, norm_mix, norm_ffn, norm_ple, norm_final, w_ff_gate, w_ff_up, w_ff_down, w_ple_gate, w_ple, loss_target, m_w_in, m_w_br_attn, m_w_br_sg, m_w_out, m_sg_w, m_sg_b, m_sg_ln_g, m_sg_ln_b, m_norm_mix, m_norm_ffn, m_norm_ple, m_norm_final, m_w_ff_gate, m_w_ff_up, m_w_ff_down, m_w_ple_gate, m_w_ple, v_w_in, v_w_br_attn, v_w_br_sg, v_w_out, v_sg_w, v_sg_b, v_sg_ln_g, v_sg_ln_b, v_norm_mix, v_norm_ffn, v_norm_ple, v_norm_final, v_w_ff_gate, v_w_ff_up, v_w_ff_down, v_w_ple_gate, v_w_ple):
    w = dict(w_in=w_in, w_br_attn=w_br_attn, w_br_sg=w_br_sg, w_out=w_out, sg_w=sg_w, sg_b=sg_b, sg_ln_g=sg_ln_g, sg_ln_b=sg_ln_b,
             norm_mix=norm_mix, norm_ffn=norm_ffn, norm_ple=norm_ple, norm_final=norm_final, w_ff_gate=w_ff_gate, w_ff_up=w_ff_up,
             w_ff_down=w_ff_down, w_ple_gate=w_ple_gate, w_ple=w_ple)
    m = dict(w_in=m_w_in, w_br_attn=m_w_br_attn, w_br_sg=m_w_br_sg, w_out=m_w_out, sg_w=m_sg_w, sg_b=m_sg_b, sg_ln_g=m_sg_ln_g,
             sg_ln_b=m_sg_ln_b, norm_mix=m_norm_mix, norm_ffn=m_norm_ffn, norm_ple=m_norm_ple, norm_final=m_norm_final,
             w_ff_gate=m_w_ff_gate, w_ff_up=m_w_ff_up, w_ff_down=m_w_ff_down, w_ple_gate=m_w_ple_gate, w_ple=m_w_ple)
    v = dict(w_in=v_w_in, w_br_attn=v_w_br_attn, w_br_sg=v_w_br_sg, w_out=v_w_out, sg_w=v_sg_w, sg_b=v_sg_b, sg_ln_g=v_sg_ln_g,
             sg_ln_b=v_sg_ln_b, norm_mix=v_norm_mix, norm_ffn=v_norm_ffn, norm_ple=v_norm_ple, norm_final=v_norm_final,
             w_ff_gate=v_w_ff_gate, w_ff_up=v_w_ff_up, w_ff_down=v_w_ff_down, w_ple_gate=v_w_ple_gate, w_ple=v_w_ple)
    loss, grad_x, grad, delta, new_m, new_v = _step(x, p, loss_target, w, m, v)
    return (loss, grad_x, *[grad[n] for n in WEIGHTS], *[delta[n] for n in WEIGHTS], *[new_m[n] for n in WEIGHTS],
            *[new_v[n] for n in WEIGHTS])
```

```python
import functools
import math

import jax
import jax.numpy as jnp
from jax import lax
from jax.experimental import pallas as pl
from jax.experimental.pallas import tpu as pltpu
from jax.experimental.pallas import tpu_sc as plsc

F32 = jnp.float32
BF16 = jnp.bfloat16
MESH = pl.DeviceIdType.MESH

HEAD_DIM = 128
ATTN_GROUPS = ((128, 1), (512, 4), (2048, 16))
N_GROUPS = 3
HEADS = 4
QKV_W = 3 * N_GROUPS * HEADS * HEAD_DIM
ATTN_W = HEADS * HEAD_DIM
SG_CHUNK = 128
SG_GROUPS = 8
SG_W = 1024
RADIUS = 64
ROPE_THETA = 10000.0
NORM_EPS = 1e-6
NEG_INF = -1e30
ADAM_LR, ADAM_B1, ADAM_B2, ADAM_EPS, ADAM_WD, ADAM_STEP = 0.001, 0.9, 0.999, 1e-08, 0.01, 10

VMEM_CAP_V7X = 56 * 1024 * 1024
LANES = 128
EW_TILE_ELEMS = 256 * 1024
MM_VMEM_BUDGET = 44 * 1024 * 1024

GATHER_COLLECTIVE_ID = 1
SCATTER_COLLECTIVE_ID = 2

BIG = ("w_in", "w_br_attn", "w_br_sg", "w_out", "w_ff_gate", "w_ff_up", "w_ff_down", "w_ple_gate", "w_ple")
ROW_SHARDED = ("w_out", "w_ff_down", "w_ple_gate")
GRAD_GROUPS = {"ffn": ("w_ple_gate", "w_ple", "w_ff_down", "w_ff_gate", "w_ff_up"), "mix": ("w_out", "w_br_attn", "w_br_sg", "w_in")}
SMALL = ("sg_w", "sg_b", "sg_ln_g", "sg_ln_b", "norm_mix", "norm_ffn", "norm_ple", "norm_final")
WEIGHTS = ("w_in", "w_br_attn", "w_br_sg", "w_out", "sg_w", "sg_b", "sg_ln_g", "sg_ln_b", "norm_mix", "norm_ffn",
           "norm_ple", "norm_final", "w_ff_gate", "w_ff_up", "w_ff_down", "w_ple_gate", "w_ple")


def _pick(n, prefs):
    for t in prefs:
        if n % t == 0:
            return t
    return n


def _nbytes(shape, dtype):
    return math.prod(shape) * jnp.dtype(dtype).itemsize


def _vmem_limit(block_bytes, temp_bytes=0):
    est = 2 * block_bytes + temp_bytes
    assert est <= VMEM_CAP_V7X, est
    return VMEM_CAP_V7X


def _sigmoid(x):
    return 1.0 / (1.0 + jnp.exp(-x))


_GELU_C = math.sqrt(2.0 / math.pi)


def _gelu(x):
    return 0.5 * x * (1.0 + jnp.tanh(_GELU_C * (x + 0.044715 * (x * x * x))))


def _gelu_grad(x):
    t = jnp.tanh(_GELU_C * (x + 0.044715 * (x * x * x)))
    return 0.5 * (1.0 + t) + 0.5 * x * (1.0 - t * t) * (_GELU_C * (1.0 + 3.0 * 0.044715 * (x * x)))


def _lead(arr, l, blk, idx):
    if arr.ndim == 2:
        return pl.BlockSpec(blk, idx)
    return pl.BlockSpec((None,) + blk, lambda *g: (l,) + idx(*g))


def _k_steps(prods, tm, tn, fixed_bytes):
    for nk in range(1, 129):
        if any(p["K"] % nk or (p["K"] // nk) % LANES for p in prods):
            continue
        if 2 * sum((tm + tn) * (p["K"] // nk) * 2 for p in prods) + fixed_bytes <= MM_VMEM_BUDGET:
            return nk
    raise ValueError("no contraction split fits VMEM")


def _mm(name, prods, M, N, outs, epilogue, tiles=(), rows=(), tm=1024, tn=1024):
    assert M % tm == 0 and N % tn == 0, (name, M, N, tm, tn)
    fixed = 2 * tm * tn * (sum(t["x"].dtype.itemsize for t in tiles) + sum(jnp.dtype(o["dtype"]).itemsize for o in outs))
    fixed += (len(prods) + 2) * tm * tn * 4
    nk = _k_steps(prods, tm, tn, fixed)
    in_specs, args, block_bytes = [], [], 0
    for p in prods:
        if isinstance(p["b"], (list, tuple)):
            p["b"], p["bl"] = p["b"][p["bl"]], None
        K = p["K"]
        assert K % nk == 0, (name, K, nk)
        tk = K // nk
        p["tk"] = tk
        a_off, bk_off, bn_off = p.get("a_off", 0), p.get("bk_off", 0), p.get("bn_off", 0)
        assert bn_off % tn == 0 and bk_off % tk == 0
        if p["mode"] == "nn":
            assert a_off % tk == 0
            a_spec = _lead(p["a"], p.get("al"), (tm, tk), lambda i, j, k, o=a_off // tk: (i, o + k))
            b_spec = _lead(p["b"], p.get("bl"), (tk, tn), lambda i, j, k, ok=bk_off // tk, on=bn_off // tn: (ok + k, on + j))
        elif p["mode"] == "nt":
            assert a_off % tk == 0
            a_spec = _lead(p["a"], p.get("al"), (tm, tk), lambda i, j, k, o=a_off // tk: (i, o + k))
            b_spec = _lead(p["b"], p.get("bl"), (tn, tk), lambda i, j, k, ok=bk_off // tk, on=bn_off // tn: (on + j, ok + k))
        else:
            assert a_off % tm == 0
            a_spec = _lead(p["a"], p.get("al"), (tk, tm), lambda i, j, k, o=a_off // tm: (k, o + i))
            b_spec = _lead(p["b"], p.get("bl"), (tk, tn), lambda i, j, k, on=bn_off // tn: (k, on + j))
        in_specs += [a_spec, b_spec]
        args += [p["a"], p["b"]]
        block_bytes += (tm + tn) * tk * 2
    for t in tiles:
        off = t.get("off", 0)
        assert off % tn == 0
        in_specs.append(_lead(t["x"], t.get("l"), (tm, tn), lambda i, j, k, o=off // tn: (i, o + j)))
        args.append(t["x"])
        block_bytes += tm * tn * t["x"].dtype.itemsize
    for r in rows:
        in_specs.append(pl.BlockSpec((1, tn), lambda i, j, k: (0, j)))
        args.append(r)
    out_shapes, out_specs, aliases = [], [], {}
    for o_i, o in enumerate(outs):
        off = o.get("col_off", 0)
        assert off % tn == 0
        out_shapes.append(jax.ShapeDtypeStruct(o["shape"], o["dtype"]))
        idx = lambda i, j, k, oo=off // tn: (i, oo + j)
        if len(o["shape"]) == 2:
            out_specs.append(pl.BlockSpec((tm, tn), idx))
        else:
            out_specs.append(pl.BlockSpec((None, tm, tn), lambda i, j, k, l=o["l"], f=idx: (l,) + f(i, j, k)))
        if o.get("alias") is not None:
            aliases[len(args)] = o_i
            in_specs.append(pl.BlockSpec(memory_space=pl.ANY))
            args.append(o["alias"])
        block_bytes += tm * tn * jnp.dtype(o["dtype"]).itemsize
    n_p, n_t, n_r, n_o = len(prods), len(tiles), len(rows), len(outs)
    n_alias = len(aliases)
    modes = [p["mode"] for p in prods]

    def body(*refs):
        ab = refs[: 2 * n_p]
        t_refs = refs[2 * n_p: 2 * n_p + n_t]
        r_refs = refs[2 * n_p + n_t: 2 * n_p + n_t + n_r]
        o_refs = refs[2 * n_p + n_t + n_r + n_alias: 2 * n_p + n_t + n_r + n_alias + n_o]
        acc_refs = refs[2 * n_p + n_t + n_r + n_alias + n_o:]
        dims = {"nn": (((1,), (0,)), ((), ())), "nt": (((1,), (1,)), ((), ())), "tn": (((0,), (0,)), ((), ()))}

        def part(q):
            return lax.dot_general(ab[2 * q][...], ab[2 * q + 1][...], dims[modes[q]], preferred_element_type=F32)

        def finish(accs):
            res = epilogue(accs, [t[...] for t in t_refs], [r[...] for r in r_refs])
            for o_ref, val in zip(o_refs, res, strict=True):
                o_ref[...] = val.astype(o_ref.dtype)

        if nk == 1:
            finish([part(q) for q in range(n_p)])
        else:
            k = pl.program_id(2)

            @pl.when(k == 0)
            def _():
                for q, acc in enumerate(acc_refs):
                    acc[...] = part(q)

            @pl.when(k > 0)
            def _():
                for q, acc in enumerate(acc_refs):
                    acc[...] += part(q)

            @pl.when(k == nk - 1)
            def _():
                finish([acc[...] for acc in acc_refs])

    scratch = [pltpu.VMEM((tm, tn), F32) for _ in prods] if nk > 1 else []
    temp = (n_p + 2) * tm * tn * 4
    res = pl.pallas_call(
        body, name=name, grid=(M // tm, N // tn, nk), in_specs=in_specs, out_specs=out_specs, out_shape=out_shapes,
        scratch_shapes=scratch, input_output_aliases=aliases,
        compiler_params=pltpu.CompilerParams(dimension_semantics=("parallel", "parallel", "arbitrary"),
                                             vmem_limit_bytes=_vmem_limit(block_bytes, temp)),
    )(*args)
    return res


def _first(accs, tiles, rows):
    return [accs[0]]


def _ew(name, fn, ins, outs, R, C, tr=None, tc=None):
    tc = tc or _pick(C, (2048, 1536, 1408, 1024, 896, 512, 384, 256, 128))
    tr = tr or _pick(R, [t for t in (512, 256, 128, 64, 32, 16) if t * tc <= EW_TILE_ELEMS] + [8])
    in_specs, args, bb = [], [], 0
    for x in ins:
        if isinstance(x, tuple):
            arr, l = x
            in_specs.append(pl.BlockSpec((None, tr, tc), lambda i, j, l=l: (l, i, j)))
        else:
            arr = x
            in_specs.append(pl.BlockSpec((tr, tc), lambda i, j: (i, j)))
        args.append(arr)
        bb += tr * tc * arr.dtype.itemsize
    out_shapes = [jax.ShapeDtypeStruct((R, C), d) for d in outs]
    out_specs = [pl.BlockSpec((tr, tc), lambda i, j: (i, j)) for _ in outs]
    bb += sum(tr * tc * jnp.dtype(d).itemsize for d in outs)
    n_in = len(ins)

    def body(*refs):
        res = fn(*[r[...] for r in refs[:n_in]])
        for o_ref, val in zip(refs[n_in:], res, strict=True):
            o_ref[...] = val.astype(o_ref.dtype)

    return pl.pallas_call(
        body, name=name, grid=(R // tr, C // tc), in_specs=in_specs, out_specs=out_specs, out_shape=out_shapes,
        compiler_params=pltpu.CompilerParams(dimension_semantics=("parallel", "parallel"),
                                             vmem_limit_bytes=_vmem_limit(bb, 6 * tr * tc * 4)),
    )(*args)


def _rmsnorm_fwd(name, x, g):
    S, D = x.shape
    tr = _pick(S, (256, 128, 64, 8))

    def body(x_ref, g_ref, h_ref):
        xv = x_ref[...]
        r = lax.rsqrt(jnp.mean(xv * xv, axis=-1, keepdims=True) + NORM_EPS)
        h_ref[...] = (xv * r * g_ref[...]).astype(BF16)

    return pl.pallas_call(
        body, name=name, grid=(S // tr,),
        in_specs=[pl.BlockSpec((tr, D), lambda i: (i, 0)), pl.BlockSpec((1, D), lambda i: (0, 0))],
        out_specs=pl.BlockSpec((tr, D), lambda i: (i, 0)), out_shape=jax.ShapeDtypeStruct((S, D), BF16),
        compiler_params=pltpu.CompilerParams(dimension_semantics=("parallel",),
                                             vmem_limit_bytes=_vmem_limit(tr * D * 6, 3 * tr * D * 4)),
    )(x, g)


def _rmsnorm_bwd(name, x, g, dh, dres):
    S, D = x.shape
    tr = _pick(S, (256, 128, 64, 8))

    def body(x_ref, g_ref, dh_ref, dres_ref, dx_ref, dxb_ref, dg_ref):
        xv = x_ref[...]
        dy = dh_ref[...].astype(F32)
        r = lax.rsqrt(jnp.mean(xv * xv, axis=-1, keepdims=True) + NORM_EPS)
        a = dy * g_ref[...]
        dx = dres_ref[...] + r * a - xv * (r * r * r) * jnp.mean(a * xv, axis=-1, keepdims=True)
        dx_ref[...] = dx
        dxb_ref[...] = dx.astype(BF16)
        part = jnp.sum(dy * xv * r, axis=0, keepdims=True)

        @pl.when(pl.program_id(0) == 0)
        def _():
            dg_ref[...] = part

        @pl.when(pl.program_id(0) > 0)
        def _():
            dg_ref[...] += part

    row = pl.BlockSpec((tr, D), lambda i: (i, 0))
    vec = pl.BlockSpec((1, D), lambda i: (0, 0))
    return pl.pallas_call(
        body, name=name, grid=(S // tr,), in_specs=[row, vec, row, row], out_specs=[row, row, vec],
        out_shape=[jax.ShapeDtypeStruct((S, D), F32), jax.ShapeDtypeStruct((S, D), BF16), jax.ShapeDtypeStruct((1, D), F32)],
        compiler_params=pltpu.CompilerParams(dimension_semantics=("arbitrary",),
                                             vmem_limit_bytes=_vmem_limit(tr * D * 18, 5 * tr * D * 4)),
    )(x, g, dh, dres)


def _loss_head(x, g, target):
    S, D = x.shape
    tr = _pick(S, (256, 128, 64, 8))

    def body(x_ref, g_ref, t_ref, loss_ref, dx_ref, dxb_ref, dg_ref):
        xv = x_ref[...]
        r = lax.rsqrt(jnp.mean(xv * xv, axis=-1, keepdims=True) + NORM_EPS)
        xn = xv * r
        diff = xn * g_ref[...] - t_ref[...]
        dy = diff * (1.0 / D)
        a = dy * g_ref[...]
        dx = r * a - xv * (r * r * r) * jnp.mean(a * xv, axis=-1, keepdims=True)
        dx_ref[...] = dx
        dxb_ref[...] = dx.astype(BF16)
        part = jnp.sum(dy * xn, axis=0, keepdims=True)
        cell = (lax.broadcasted_iota(jnp.int32, (8, LANES), 0) == 0) & (lax.broadcasted_iota(jnp.int32, (8, LANES), 1) == 0)
        lpart = jnp.where(cell, 0.5 * jnp.sum(jnp.mean(diff * diff, axis=-1, keepdims=True)), 0.0)

        @pl.when(pl.program_id(0) == 0)
        def _():
            dg_ref[...] = part
            loss_ref[...] = lpart

        @pl.when(pl.program_id(0) > 0)
        def _():
            dg_ref[...] += part
            loss_ref[...] += lpart

    row = pl.BlockSpec((tr, D), lambda i: (i, 0))
    vec = pl.BlockSpec((1, D), lambda i: (0, 0))
    return pl.pallas_call(
        body, name="loss_head", grid=(S // tr,), in_specs=[row, vec, row],
        out_specs=[pl.BlockSpec((8, LANES), lambda i: (0, 0)), row, row, vec],
        out_shape=[jax.ShapeDtypeStruct((8, LANES), F32), jax.ShapeDtypeStruct((S, D), F32),
                   jax.ShapeDtypeStruct((S, D), BF16), jax.ShapeDtypeStruct((1, D), F32)],
        compiler_params=pltpu.CompilerParams(dimension_semantics=("arbitrary",),
                                             vmem_limit_bytes=_vmem_limit(tr * D * 14, 6 * tr * D * 4)),
    )(x, g, target)


def _rope_tables(S):
    pos = jnp.arange(S, dtype=F32)
    inv_freq = ROPE_THETA ** (-jnp.arange(0, HEAD_DIM, 2, dtype=F32) / HEAD_DIM)
    ang = pos[:, None] * inv_freq[None, :]
    cos, sin = jnp.cos(ang), jnp.sin(ang)
    return jnp.concatenate([cos, cos], axis=-1), jnp.concatenate([-sin, sin], axis=-1)


def _rope_fwd(name, z, cosf, sinf):
    S = z.shape[0]
    tr = _pick(S, (256, 128, 64, 8))
    n_q = N_GROUPS * HEADS

    def body(z_ref, c_ref, s_ref, o_ref):
        c, s = c_ref[...], s_ref[...]
        for j in range(QKV_W // HEAD_DIM):
            t = z_ref[:, j * HEAD_DIM:(j + 1) * HEAD_DIM]
            if j < 2 * n_q:
                t = t * c + pltpu.roll(t, HEAD_DIM // 2, axis=1) * s
            if j < n_q:
                t = t * ATTN_SCALE
            o_ref[:, j * HEAD_DIM:(j + 1) * HEAD_DIM] = t.astype(BF16)

    tab = pl.BlockSpec((tr, HEAD_DIM), lambda i: (i, 0))
    return pl.pallas_call(
        body, name=name, grid=(S // tr,), in_specs=[pl.BlockSpec((tr, QKV_W), lambda i: (i, 0)), tab, tab],
        out_specs=pl.BlockSpec((tr, QKV_W), lambda i: (i, 0)), out_shape=jax.ShapeDtypeStruct((S, QKV_W), BF16),
        compiler_params=pltpu.CompilerParams(dimension_semantics=("parallel",),
                                             vmem_limit_bytes=_vmem_limit(tr * QKV_W * 6, tr * QKV_W * 4)),
    )(z, cosf, sinf)


def _rope_bwd(name, dq, dk, dv, dzuv, cosf, sinf, dz):
    S = dq.shape[0]
    tr = _pick(S, (256, 128, 64, 8))
    W3 = QKV_W // 3
    nh = W3 // HEAD_DIM
    wide = QKV_W + dzuv.shape[1]

    def body(dq_ref, dk_ref, dv_ref, uv_ref, c_ref, s_ref, dz_in, o_ref):
        c, s = c_ref[...], s_ref[...]
        for part, ref in enumerate((dq_ref, dk_ref)):
            for j in range(nh):
                t = ref[:, j * HEAD_DIM:(j + 1) * HEAD_DIM].astype(F32)
                t = t * c - pltpu.roll(t, HEAD_DIM // 2, axis=1) * s
                o_ref[:, part * W3 + j * HEAD_DIM: part * W3 + (j + 1) * HEAD_DIM] = t.astype(BF16)
        o_ref[:, 2 * W3:QKV_W] = dv_ref[...]
        o_ref[:, QKV_W:] = uv_ref[...]

    third = pl.BlockSpec((tr, W3), lambda i: (i, 0))
    tab = pl.BlockSpec((tr, HEAD_DIM), lambda i: (i, 0))
    return pl.pallas_call(
        body, name=name, grid=(S // tr,),
        in_specs=[third, third, third, pl.BlockSpec((tr, dzuv.shape[1]), lambda i: (i, 0)), tab, tab, pl.BlockSpec(memory_space=pl.ANY)],
        out_specs=pl.BlockSpec((tr, wide), lambda i: (i, 0)), out_shape=jax.ShapeDtypeStruct(dz.shape, dz.dtype),
        input_output_aliases={6: 0},
        compiler_params=pltpu.CompilerParams(dimension_semantics=("parallel",),
                                             vmem_limit_bytes=_vmem_limit(tr * wide * 4, tr * wide * 4)),
    )(dq, dk, dv, dzuv, cosf, sinf, dz)


ATTN_TQ = 256
ATTN_SCALE = HEAD_DIM ** -0.5
ATTN_PAD_MAX = RADIUS * max(d for _, d in ATTN_GROUPS)


def _band_bias(shape, q_axis, d):
    kq = lax.broadcasted_iota(jnp.int32, shape, 1 - q_axis) - lax.broadcasted_iota(jnp.int32, shape, q_axis) - RADIUS * d
    return jnp.where((jnp.abs(kq) <= RADIUS * d) & ((kq & (d - 1)) == 0), 0.0, NEG_INF).astype(F32)


def _fill_padded(dst, src, d, S):
    pad = RADIUS * d
    dst[0:pad, :] = jnp.zeros((pad, HEAD_DIM), dst.dtype)
    dst[pad:pad + S, :] = src[...]
    dst[pad + S:pad + S + pad, :] = jnp.zeros((pad, HEAD_DIM), dst.dtype)


_NT = (((1,), (1,)), ((), ()))


def _attn_fwd(name, qkv):
    S = qkv.shape[0]
    T = ATTN_TQ
    nq = N_GROUPS * HEADS
    widths = [T + 2 * RADIUS * d for _, d in ATTN_GROUPS]

    def body(*refs):
        q_refs, k_refs, v_refs = refs[0:3], refs[3:6], refs[6:9]
        o_ref, lc_ref = refs[9:11]
        kp, vp, bias = refs[11:14], refs[14:17], refs[17:20]
        i0 = pl.multiple_of(pl.program_id(1) * T, T)

        @pl.when(pl.program_id(1) == 0)
        def _():
            for g, (_, d) in enumerate(ATTN_GROUPS):
                _fill_padded(kp[g], k_refs[g], d, S)
                _fill_padded(vp[g], v_refs[g], d, S)
                bias[g][...] = _band_bias((T, widths[g]), 0, d)

        m = jnp.full((T, 1), NEG_INF, F32)
        l = jnp.zeros((T, 1), F32)
        acc = jnp.zeros((T, HEAD_DIM), F32)
        for g, (_, d) in enumerate(ATTN_GROUPS):
            W = widths[g]
            kw = kp[g][pl.ds(i0, W), :]
            vw = vp[g][pl.ds(i0, W), :]
            key = i0 - RADIUS * d + lax.broadcasted_iota(jnp.int32, (1, W), 1)
            in_seq = jnp.where((key >= 0) & (key < S), 0.0, NEG_INF).astype(F32)
            s = lax.dot_general(q_refs[g][...], kw, _NT, preferred_element_type=F32) + bias[g][...] + in_seq
            m_new = jnp.maximum(m, jnp.max(s, axis=1, keepdims=True))
            alpha = jnp.exp(m - m_new)
            p = jnp.exp(s - m_new)
            l = l * alpha + jnp.sum(p, axis=1, keepdims=True)
            acc = acc * alpha + jnp.dot(p.astype(BF16), vw, preferred_element_type=F32)
            m = m_new
        o_ref[...] = (acc / l).astype(BF16)
        lc_ref[...] = m + jnp.log(l)

    in_specs = [pl.BlockSpec((T, HEAD_DIM), lambda h, i, g=g: (i, g * HEADS + h)) for g in range(N_GROUPS)]
    in_specs += [pl.BlockSpec((S, HEAD_DIM), lambda h, i, g=g: (0, nq + g * HEADS + h)) for g in range(N_GROUPS)]
    in_specs += [pl.BlockSpec((S, HEAD_DIM), lambda h, i, g=g: (0, 2 * nq + g * HEADS + h)) for g in range(N_GROUPS)]
    padded = [pltpu.VMEM((S + 2 * RADIUS * d, HEAD_DIM), BF16) for _, d in ATTN_GROUPS]
    scratch = padded + padded + [pltpu.VMEM((T, W), F32) for W in widths]
    scratch_bytes = sum(2 * (S + 2 * RADIUS * d) * HEAD_DIM * 2 for _, d in ATTN_GROUPS) + sum(T * W * 4 for W in widths)
    return pl.pallas_call(
        body, name=name, grid=(HEADS, S // T), in_specs=in_specs,
        out_specs=[pl.BlockSpec((T, HEAD_DIM), lambda h, i: (i, h)), pl.BlockSpec((None, T, 1), lambda h, i: (h, i, 0))],
        out_shape=[jax.ShapeDtypeStruct((S, ATTN_W), BF16), jax.ShapeDtypeStruct((HEADS, S, 1), F32)],
        scratch_shapes=scratch,
        compiler_params=pltpu.CompilerParams(dimension_semantics=("parallel", "arbitrary"),
                                             vmem_limit_bytes=_vmem_limit(6 * S * HEAD_DIM * 2 + 8 * T * HEAD_DIM * 4,
                                                                          scratch_bytes + 4 * T * widths[-1] * 4)),
    )(*([qkv] * 9))


_TN = (((0,), (0,)), ((), ()))


def _attn_bwd(name, qkv, attn, dattn, lse_c):
    S = qkv.shape[0]
    T = ATTN_TQ
    nq = N_GROUPS * HEADS
    W3 = QKV_W // 3
    n_i = S // T
    wmax = T + 2 * ATTN_PAD_MAX
    s_pad = S + 2 * ATTN_PAD_MAX

    def body(q_ref, k_ref, v_ref, o_ref, do_ref, lc_ref, dq_ref, dk_ref, dv_ref, kp, vp, dk_acc, dv_acc, bias):
        g_id, i = pl.program_id(1), pl.program_id(2)
        i0 = pl.multiple_of(i * T, T)
        q, do = q_ref[...], do_ref[...]
        delta = jnp.sum(do.astype(F32) * o_ref[...].astype(F32), axis=1, keepdims=True)
        lse = lc_ref[...]

        def group(d):
            W, pad = T + 2 * RADIUS * d, RADIUS * d

            @pl.when(i == 0)
            def _():
                _fill_padded(kp, k_ref, d, S)
                _fill_padded(vp, v_ref, d, S)
                dk_acc[...] = jnp.zeros_like(dk_acc)
                dv_acc[...] = jnp.zeros_like(dv_acc)
                bias[:, 0:W] = _band_bias((T, W), 0, d)

            kw = kp[pl.ds(i0, W), :]
            vw = vp[pl.ds(i0, W), :]
            key = i0 - pad + lax.broadcasted_iota(jnp.int32, (1, W), 1)
            in_seq = jnp.where((key >= 0) & (key < S), 0.0, NEG_INF).astype(F32)
            s = lax.dot_general(q, kw, _NT, preferred_element_type=F32) + bias[:, 0:W] + in_seq
            p = jnp.exp(s - lse)
            dp = lax.dot_general(do, vw, _NT, preferred_element_type=F32)
            ds = (p * (dp - delta)).astype(BF16)
            dq_ref[...] = (jnp.dot(ds, kw, preferred_element_type=F32) * ATTN_SCALE).astype(BF16)
            dk_acc[pl.ds(i0, W), :] += lax.dot_general(ds, q, _TN, preferred_element_type=F32)
            dv_acc[pl.ds(i0, W), :] += lax.dot_general(p.astype(BF16), do, _TN, preferred_element_type=F32)

            @pl.when(i == n_i - 1)
            def _():
                dk_ref[...] = dk_acc[pad:pad + S, :].astype(BF16)
                dv_ref[...] = dv_acc[pad:pad + S, :].astype(BF16)

        for g, (_, d) in enumerate(ATTN_GROUPS):
            pl.when(g_id == g)(functools.partial(group, d))

    tile = lambda off: pl.BlockSpec((T, HEAD_DIM), lambda h, g, i: (i, off + g * HEADS + h))
    full = lambda off: pl.BlockSpec((S, HEAD_DIM), lambda h, g, i: (0, off + g * HEADS + h))
    headt = pl.BlockSpec((T, HEAD_DIM), lambda h, g, i: (i, h))
    scratch_bytes = 2 * s_pad * HEAD_DIM * (2 + 4) + T * wmax * 4
    return pl.pallas_call(
        body, name=name, grid=(HEADS, N_GROUPS, n_i),
        in_specs=[tile(0), full(nq), full(2 * nq), headt, headt, pl.BlockSpec((None, T, 1), lambda h, g, i: (h, i, 0))],
        out_specs=[tile(0), full(0), full(0)],
        out_shape=[jax.ShapeDtypeStruct((S, W3), BF16)] * 3,
        scratch_shapes=[pltpu.VMEM((s_pad, HEAD_DIM), BF16), pltpu.VMEM((s_pad, HEAD_DIM), BF16),
                        pltpu.VMEM((s_pad, HEAD_DIM), F32), pltpu.VMEM((s_pad, HEAD_DIM), F32), pltpu.VMEM((T, wmax), F32)],
        compiler_params=pltpu.CompilerParams(dimension_semantics=("parallel", "arbitrary", "arbitrary"),
                                             vmem_limit_bytes=_vmem_limit(4 * S * HEAD_DIM * 2 + 8 * T * HEAD_DIM * 4,
                                                                          scratch_bytes + 5 * T * wmax * 4)),
    )(qkv, qkv, qkv, attn, dattn, lse_c)


def _sg_parts(u, v, lng, lnb):
    gu = _gelu(u)
    gv = _gelu(v)
    mu = jnp.mean(gv, axis=-1, keepdims=True)
    xc = gv - mu
    rstd = lax.rsqrt(jnp.mean(xc * xc, axis=-1, keepdims=True) + NORM_EPS)
    xhat = xc * rstd
    vn = xhat * lng + lnb
    return gu, xhat, rstd, vn


def _sg_fwd(name, z, sg_w, sg_bc, lng, lnb, o_sg0):
    S = z.shape[0]
    T = SG_CHUNK
    cb = 512
    assert o_sg0 % cb == 0
    b0 = o_sg0 // cb

    def body(u0, u1, v0, v1, w_ref, b_ref, g_ref, be_ref, o_ref):
        u = jnp.concatenate([u0[...], u1[...]], axis=1)
        v = jnp.concatenate([v0[...], v1[...]], axis=1)
        gu, _, _, vn = _sg_parts(u, v, g_ref[...], be_ref[...])
        vnb = vn.astype(BF16)
        for g in range(SG_GROUPS):
            sl = slice(g * SG_CHUNK, (g + 1) * SG_CHUNK)
            mixed = jnp.dot(w_ref[g], vnb[:, sl], preferred_element_type=F32) + b_ref[g]
            o_ref[:, sl] = (gu[:, sl] * mixed).astype(BF16)

    zs = lambda k: pl.BlockSpec((T, cb), lambda i, k=k: (i, b0 + k))
    const3 = lambda shp: pl.BlockSpec(shp, lambda i: (0, 0, 0))
    vec = pl.BlockSpec((1, SG_W), lambda i: (0, 0))
    return pl.pallas_call(
        body, name=name, grid=(S // T,),
        in_specs=[zs(0), zs(1), zs(2), zs(3), const3((SG_GROUPS, SG_CHUNK, SG_CHUNK)), const3((SG_GROUPS, SG_CHUNK, 1)), vec, vec],
        out_specs=pl.BlockSpec((T, SG_W), lambda i: (i, 0)), out_shape=jax.ShapeDtypeStruct((S, SG_W), BF16),
        compiler_params=pltpu.CompilerParams(dimension_semantics=("parallel",), vmem_limit_bytes=_vmem_limit(4 * 1024 * 1024, 8 * T * SG_W * 4)),
    )(z, z, z, z, sg_w, sg_bc, lng, lnb)


def _sg_bwd(name, z, dsg, sg_w, sg_wt, sg_bc, lng, lnb, o_sg0):
    S = z.shape[0]
    T = SG_CHUNK
    cb = 512
    b0 = o_sg0 // cb

    def body(u0, u1, v0, v1, d_ref, w_ref, wt_ref, b_ref, g_ref, be_ref, dz_ref, dw_ref, db_ref, dg_ref, dbe_ref):
        i = pl.program_id(0)
        u = jnp.concatenate([u0[...], u1[...]], axis=1)
        v = jnp.concatenate([v0[...], v1[...]], axis=1)
        gu, xhat, rstd, vn = _sg_parts(u, v, g_ref[...], be_ref[...])
        vnb = vn.astype(BF16)
        dsg_v = d_ref[...].astype(F32)
        dmix = dsg_v * gu
        dmixb = dmix.astype(BF16)
        dvn_parts, mixed_parts, dw_parts, db_parts = [], [], [], []
        for g in range(SG_GROUPS):
            sl = slice(g * SG_CHUNK, (g + 1) * SG_CHUNK)
            mixed_parts.append(jnp.dot(w_ref[g], vnb[:, sl], preferred_element_type=F32) + b_ref[g])
            dvn_parts.append(jnp.dot(wt_ref[g], dmixb[:, sl], preferred_element_type=F32))
            dw_parts.append(lax.dot_general(dmixb[:, sl], vnb[:, sl], _NT, preferred_element_type=F32))
            db_parts.append(jnp.sum(dmix[:, sl], axis=1, keepdims=True))
        mixed = jnp.concatenate(mixed_parts, axis=1)
        dvn = jnp.concatenate(dvn_parts, axis=1)
        dzu = dsg_v * mixed * _gelu_grad(u)
        dxh = dvn * g_ref[...]
        dgv = rstd * (dxh - jnp.mean(dxh, axis=-1, keepdims=True) - xhat * jnp.mean(dxh * xhat, axis=-1, keepdims=True))
        dzv = dgv * _gelu_grad(v)
        dz_ref[:, :SG_W] = dzu.astype(BF16)
        dz_ref[:, SG_W:] = dzv.astype(BF16)
        dgp = jnp.sum(dvn * xhat, axis=0, keepdims=True)
        dbp = jnp.sum(dvn, axis=0, keepdims=True)

        @pl.when(i == 0)
        def _():
            for g in range(SG_GROUPS):
                dw_ref[g] = dw_parts[g]
                db_ref[g] = db_parts[g]
            dg_ref[...] = dgp
            dbe_ref[...] = dbp

        @pl.when(i > 0)
        def _():
            for g in range(SG_GROUPS):
                dw_ref[g] += dw_parts[g]
                db_ref[g] += db_parts[g]
            dg_ref[...] += dgp
            dbe_ref[...] += dbp

    zs = lambda k: pl.BlockSpec((T, cb), lambda i, k=k: (i, b0 + k))
    const3 = lambda shp: pl.BlockSpec(shp, lambda i: (0, 0, 0))
    vec = pl.BlockSpec((1, SG_W), lambda i: (0, 0))
    return pl.pallas_call(
        body, name=name, grid=(S // T,),
        in_specs=[zs(0), zs(1), zs(2), zs(3), pl.BlockSpec((T, SG_W), lambda i: (i, 0)),
                  const3((SG_GROUPS, SG_CHUNK, SG_CHUNK)), const3((SG_GROUPS, SG_CHUNK, SG_CHUNK)), const3((SG_GROUPS, SG_CHUNK, 1)),
                  vec, vec],
        out_specs=[pl.BlockSpec((T, 2 * SG_W), lambda i: (i, 0)), const3((SG_GROUPS, SG_CHUNK, SG_CHUNK)),
                   const3((SG_GROUPS, SG_CHUNK, 1)), vec, vec],
        out_shape=[jax.ShapeDtypeStruct((S, 2 * SG_W), BF16), jax.ShapeDtypeStruct((SG_GROUPS, SG_CHUNK, SG_CHUNK), F32),
                   jax.ShapeDtypeStruct((SG_GROUPS, SG_CHUNK, 1), F32), jax.ShapeDtypeStruct((1, SG_W), F32),
                   jax.ShapeDtypeStruct((1, SG_W), F32)],
        compiler_params=pltpu.CompilerParams(dimension_semantics=("arbitrary",),
                                             vmem_limit_bytes=_vmem_limit(6 * 1024 * 1024, 16 * T * SG_W * 4)),
    )(z, z, z, z, dsg, sg_w, sg_wt, sg_bc, lng, lnb)


def _gate_bwd(name, z, dmerged, y_attn, y_sg, o_g0, in_w):
    S, D = dmerged.shape
    tr = _pick(S, (512, 256, 128, 8))
    cb = _pick(D, (512, 256, 128))
    assert o_g0 % cb == 0
    nd = D // cb
    b0 = o_g0 // cb

    def body(z_ref, dm_ref, ya_ref, ys_ref, dz_ref, dy_ref):
        jj = pl.program_id(1)
        gate = _sigmoid(z_ref[...])
        dm = dm_ref[...].astype(F32)
        y = jnp.where(jj < nd, ya_ref[...], ys_ref[...]).astype(F32)
        dz_ref[...] = (dm * y * gate * (1.0 - gate)).astype(BF16)
        dy_ref[...] = (dm * gate).astype(BF16)

    half = pl.BlockSpec((tr, cb), lambda i, jj: (i, jj % nd))
    return pl.pallas_call(
        body, name=name, grid=(S // tr, 2 * nd),
        in_specs=[pl.BlockSpec((tr, cb), lambda i, jj: (i, b0 + jj)), half, half, half],
        out_specs=[pl.BlockSpec((tr, cb), lambda i, jj: (i, b0 + jj)), pl.BlockSpec((tr, cb), lambda i, jj: (i, jj))],
        out_shape=[jax.ShapeDtypeStruct((S, in_w), BF16), jax.ShapeDtypeStruct((S, 2 * D), BF16)],
        compiler_params=pltpu.CompilerParams(dimension_semantics=("parallel", "arbitrary"),
                                             vmem_limit_bytes=_vmem_limit(tr * cb * 14, 6 * tr * cb * 4)),
    )(z, dmerged, y_attn, y_sg)


def _row(v):
    return v.reshape(1, -1)


def _local_step(x, p, target, wf, small, after_group):
    S, D = x.shape
    L = p.shape[0]
    in_w = wf["w_in"][0].shape[1]
    ff = wf["w_ff_gate"][0].shape[1]
    ple = p.shape[2]
    o_sg0, o_g0 = QKV_W, QKV_W + 2 * SG_W
    cosf, sinf = _rope_tables(S)
    pb = p.astype(BF16)
    tmb = _pick(S, (1024, 512, 256))
    tn_in = _pick(in_w, (768, 1024, 512))
    tn_d = _pick(D, (1024, 512, 256))
    tn_g = _pick(D, (512, 256))
    tn_ff = _pick(ff, (512, 256))

    saved = []
    xs = x
    for i in range(L):
        sv = {"x0": xs}
        h = _rmsnorm_fwd(f"norm_mix_{i}", xs, _row(small["norm_mix"][i]))
        (z,) = _mm(f"in_proj_{i}", [dict(a=h, b=wf["w_in"], bl=i, mode="nn", K=D)], S, in_w,
                   [dict(shape=(S, in_w), dtype=F32)], _first, tm=tmb, tn=tn_in)
        qkv = _rope_fwd(f"rope_{i}", z, cosf, sinf)
        attn, lse_c = _attn_fwd(f"attn_{i}", qkv)
        sgw = small["sg_w"][i].astype(BF16)
        sgbc = small["sg_b"][i].reshape(SG_GROUPS, SG_CHUNK, 1)
        sg = _sg_fwd(f"sgu_{i}", z, sgw, sgbc, _row(small["sg_ln_g"][i]), _row(small["sg_ln_b"][i]), o_sg0)

        def merge(accs, tiles, rows):
            ya, ys = accs[0].astype(BF16), accs[1].astype(BF16)
            g0, g1 = _sigmoid(tiles[0]), _sigmoid(tiles[1])
            return [ya, ys, g0 * ya.astype(F32) + g1 * ys.astype(F32)]

        y_attn, y_sg, merged = _mm(
            f"branches_{i}",
            [dict(a=attn, b=wf["w_br_attn"], bl=i, mode="nn", K=ATTN_W), dict(a=sg, b=wf["w_br_sg"], bl=i, mode="nn", K=SG_W)],
            S, D, [dict(shape=(S, D), dtype=BF16)] * 3, merge,
            tiles=[dict(x=z, off=o_g0), dict(x=z, off=o_g0 + D)], tm=tmb, tn=tn_g)
        (x1,) = _mm(f"out_proj_{i}", [dict(a=merged, b=wf["w_out"], bl=i, mode="nn", K=D)], S, D,
                    [dict(shape=(S, D), dtype=F32)], lambda a, t, r: [t[0] + a[0]], tiles=[dict(x=xs)], tm=tmb, tn=tn_d)
        h2 = _rmsnorm_fwd(f"norm_ffn_{i}", x1, _row(small["norm_ffn"][i]))

        def swiglu(accs, tiles, rows):
            fg = accs[0].astype(BF16).astype(F32)
            fu = accs[1].astype(BF16).astype(F32)
            return [fg, fu, fg * _sigmoid(fg) * fu]

        ffg, ffu, act = _mm(
            f"ff_in_{i}",
            [dict(a=h2, b=wf["w_ff_gate"], bl=i, mode="nn", K=D), dict(a=h2, b=wf["w_ff_up"], bl=i, mode="nn", K=D)],
            S, ff, [dict(shape=(S, ff), dtype=BF16)] * 3, swiglu, tm=tmb, tn=tn_ff)
        (x2,) = _mm(f"ff_out_{i}", [dict(a=act, b=wf["w_ff_down"], bl=i, mode="nn", K=ff)], S, D,
                    [dict(shape=(S, D), dtype=F32)], lambda a, t, r: [t[0] + a[0]], tiles=[dict(x=x1)], tm=tmb, tn=tn_d)
        h3 = _rmsnorm_fwd(f"norm_ple_{i}", x2, _row(small["norm_ple"][i]))

        def ple_mix(accs, tiles, rows):
            gp = _sigmoid(accs[0]).astype(BF16)
            pe = accs[1].astype(BF16)
            return [tiles[0] + gp.astype(F32) * pe.astype(F32), gp, pe]

        x3, gp, pe = _mm(
            f"ple_{i}",
            [dict(a=h3, b=wf["w_ple_gate"], bl=i, mode="nn", K=D), dict(a=pb, al=i, b=wf["w_ple"], bl=i, mode="nn", K=ple)],
            S, D, [dict(shape=(S, D), dtype=F32), dict(shape=(S, D), dtype=BF16), dict(shape=(S, D), dtype=BF16)], ple_mix,
            tiles=[dict(x=x2)], tm=tmb, tn=tn_g)
        sv.update(h=h, z=z, qkv=qkv, attn=attn, lse_c=lse_c, sg=sg, y_attn=y_attn, y_sg=y_sg, merged=merged,
                  x1=x1, h2=h2, ffg=ffg, ffu=ffu, act=act, x2=x2, h3=h3, gp=gp, pe=pe, sgw=sgw, sgbc=sgbc)
        saved.append(sv)
        xs = x3

    loss_cell, dx, dxb, dg_final = _loss_head(xs, _row(small["norm_final"]), target)

    gw = {n: [None] * L for n in BIG}
    gs = {n: [None] * L for n in SMALL if n != "norm_final"}

    def dw(n, i, a, a_off, b, bn_off, K_rows, N_cols, tm, tn):
        (gw[n][i],) = _mm(f"d_{n}_{i}", [dict(a=a, b=b, mode="tn", K=S, a_off=a_off, bn_off=bn_off)], K_rows, N_cols,
                          [dict(shape=(K_rows, N_cols), dtype=BF16)], _first, tm=tm, tn=tn)

    for i in reversed(range(L)):
        sv = saved[i]
        dpre, dpe = _ew(f"ple_gate_bwd_{i}",
                        lambda d, g, e: [d * e.astype(F32) * g.astype(F32) * (1.0 - g.astype(F32)), d * g.astype(F32)],
                        [dx, sv["gp"], sv["pe"]], [BF16, BF16], S, D)
        (dh3,) = _mm(f"d_h3_{i}", [dict(a=dpre, b=wf["w_ple_gate"], bl=i, mode="nt", K=D)], S, D,
                     [dict(shape=(S, D), dtype=F32)], _first, tm=tmb, tn=tn_d)
        dw("w_ple_gate", i, sv["h3"], 0, dpre, 0, D, D, tn_d, tn_d)
        dw("w_ple", i, pb[i], 0, dpe, 0, ple, D, _pick(ple, (256, 128)), _pick(D, (2048, 1024, 512, 256)))
        dx, dxb, gs["norm_ple"][i] = _rmsnorm_bwd(f"norm_ple_bwd_{i}", sv["x2"], _row(small["norm_ple"][i]), dh3, dx)
        def swiglu_bwd(accs, tiles, rows):
            da = accs[0].astype(BF16).astype(F32)
            fg, fu = tiles[0].astype(F32), tiles[1].astype(F32)
            sg_ = _sigmoid(fg)
            return [da * fu * (sg_ * (1.0 + fg * (1.0 - sg_))), da * (fg * sg_)]

        dffg, dffu = _mm(f"d_act_{i}", [dict(a=dxb, b=wf["w_ff_down"], bl=i, mode="nt", K=D)], S, ff,
                         [dict(shape=(S, ff), dtype=BF16)] * 2, swiglu_bwd, tiles=[dict(x=sv["ffg"]), dict(x=sv["ffu"])],
                         tm=tmb, tn=tn_ff)
        dw("w_ff_down", i, sv["act"], 0, dxb, 0, ff, D, tn_ff, _pick(D, (2048, 1024, 512, 256)))
        dw("w_ff_gate", i, sv["h2"], 0, dffg, 0, D, ff, _pick(D, (2048, 1024, 512, 256)), tn_ff)
        dw("w_ff_up", i, sv["h2"], 0, dffu, 0, D, ff, _pick(D, (2048, 1024, 512, 256)), tn_ff)
        (dffg, dffu), _ = lax.optimization_barrier(((dffg, dffu), after_group(i, "ffn", {n: gw[n][i] for n in GRAD_GROUPS["ffn"]})))
        (dh2,) = _mm(f"d_h2_{i}", [dict(a=dffg, b=wf["w_ff_gate"], bl=i, mode="nt", K=ff),
                                   dict(a=dffu, b=wf["w_ff_up"], bl=i, mode="nt", K=ff)], S, D,
                     [dict(shape=(S, D), dtype=F32)], lambda a, t, r: [a[0] + a[1]], tm=tmb, tn=tn_d)
        dx, dxb, gs["norm_ffn"][i] = _rmsnorm_bwd(f"norm_ffn_bwd_{i}", sv["x1"], _row(small["norm_ffn"][i]), dh2, dx)
        (dmerged,) = _mm(f"d_merged_{i}", [dict(a=dxb, b=wf["w_out"], bl=i, mode="nt", K=D)], S, D,
                         [dict(shape=(S, D), dtype=BF16)], _first, tm=tmb, tn=tn_d)
        dw("w_out", i, sv["merged"], 0, dxb, 0, D, D, tn_d, tn_d)
        dz, dy = _gate_bwd(f"gate_bwd_{i}", sv["z"], dmerged, sv["y_attn"], sv["y_sg"], o_g0, in_w)
        (dattn,) = _mm(f"d_attn_{i}", [dict(a=dy, b=wf["w_br_attn"], bl=i, mode="nt", K=D)], S, ATTN_W,
                       [dict(shape=(S, ATTN_W), dtype=BF16)], _first, tm=tmb, tn=ATTN_W)
        (dsg,) = _mm(f"d_sg_{i}", [dict(a=dy, a_off=D, b=wf["w_br_sg"], bl=i, mode="nt", K=D)], S, SG_W,
                     [dict(shape=(S, SG_W), dtype=BF16)], _first, tm=tmb, tn=SG_W)
        dw("w_br_attn", i, sv["attn"], 0, dy, 0, ATTN_W, D, ATTN_W, _pick(D, (2048, 1024, 512, 256)))
        dw("w_br_sg", i, sv["sg"], 0, dy, D, SG_W, D, SG_W, _pick(D, (1024, 512, 256)))
        sgwt = jnp.swapaxes(small["sg_w"][i], 1, 2).astype(BF16)
        dzuv, gs["sg_w"][i], dsgb, dlg, dlb = _sg_bwd(f"sgu_bwd_{i}", sv["z"], dsg, sv["sgw"], sgwt, sv["sgbc"],
                                                      _row(small["sg_ln_g"][i]), _row(small["sg_ln_b"][i]), o_sg0)
        gs["sg_b"][i], gs["sg_ln_g"][i], gs["sg_ln_b"][i] = dsgb.reshape(SG_GROUPS, SG_CHUNK), dlg[0], dlb[0]
        dq, dk, dv = _attn_bwd(f"attn_bwd_{i}", sv["qkv"], sv["attn"], dattn, sv["lse_c"])
        dz = _rope_bwd(f"rope_bwd_{i}", dq, dk, dv, dzuv, cosf, sinf, dz)
        dw("w_in", i, sv["h"], 0, dz, 0, D, in_w, tn_d, tn_in)
        dz, _ = lax.optimization_barrier((dz, after_group(i, "mix", {n: gw[n][i] for n in GRAD_GROUPS["mix"]})))
        (dh,) = _mm(f"d_h_{i}", [dict(a=dz, b=wf["w_in"], bl=i, mode="nt", K=in_w)], S, D,
                    [dict(shape=(S, D), dtype=F32)], _first, tm=tmb, tn=tn_d)
        dx, dxb, gs["norm_mix"][i] = _rmsnorm_bwd(f"norm_mix_bwd_{i}", sv["x0"], _row(small["norm_mix"][i]), dh, dx)

    gsmall ={n: jnp.stack([jnp.reshape(v, small[n].shape[1:]) for v in gs[n]]) for n in gs}
    gsmall["norm_final"] = dg_final[0]
    return loss_cell, dx, gsmall


def _place():
    x, y, c = lax.axis_index("x"), lax.axis_index("y"), lax.axis_index("c")
    return x, y, c, 2 * x + y


def _chip_of(s):
    return s // 2, s % 2


def _aligned(v, m):
    return v if isinstance(v, int) else pl.multiple_of(v, m)


def _piece(name, shape, s, c):
    K, N = shape
    if name in ROW_SHARDED:
        ks = K // 4
        return s * ks + c * (ks // 2), ks // 2, 0, N
    ns = N // 4
    return c * (K // 2), K // 2, s * ns, ns


def _handshake(peers):
    barrier = pltpu.get_barrier_semaphore()
    for peer in peers:
        pl.semaphore_signal(barrier, inc=1, device_id=peer, device_id_type=MESH)
    pl.semaphore_wait(barrier, len(peers))


def _gather_body(names, shapes, src, dst, send_sems, recv_sems, local_sems):
    n_w = len(names)
    x, y, c, s = _place()
    sib = (x, y, 1 - c)
    rel = [1, 2, 3]

    def where(w, ps, pc):
        r0, nr, c0, nc = _piece(names[w], shapes[names[w]], ps, pc)
        return dst[w].at[pl.ds(_aligned(r0, 16), nr), pl.ds(_aligned(c0, LANES), nc)]

    def copy(w, k, ps, pc, to, from_src=False):
        return pltpu.make_async_remote_copy(
            src_ref=src[w] if from_src else where(w, ps, pc), dst_ref=where(w, ps, pc),
            send_sem=send_sems.at[w, k], recv_sem=recv_sems.at[w, k], device_id=to, device_id_type=MESH)

    mine, first, passed = [], [], []
    for w in range(n_w):
        cp = pltpu.make_async_copy(src[w], where(w, s, c), local_sems.at[w])
        cp.start()
        mine.append(cp)
        first.append(copy(w, 0, s, c, sib, from_src=True))
        for j in rel:
            first.append(copy(w, j, s, c, (*_chip_of(s ^ j), c), from_src=True))
    for cp in first:
        cp.start()
    for w in range(n_w):
        for j in rel:
            copy(w, j, s ^ j, c, sib).wait_recv()
            fw = copy(w, 3 + j, s ^ j, c, sib)
            fw.start()
            passed.append(fw)
    for w in range(n_w):
        copy(w, 0, s, 1 - c, sib).wait_recv()
        for j in rel:
            copy(w, 3 + j, s ^ j, 1 - c, sib).wait_recv()
    for cp in first + passed:
        cp.wait_send()
    for cp in mine:
        cp.wait()


def _gather_sems(n_w):
    return (pltpu.SemaphoreType.DMA((n_w, 7)), pltpu.SemaphoreType.DMA((n_w, 7)), pltpu.SemaphoreType.DMA((n_w,)))


def _gather_peers():
    x, y, c, s = _place()
    return [(x, y, 1 - c)] + [(*_chip_of(s ^ j), c) for j in (1, 2, 3)]


def _gather_weights(name, pieces, shapes):
    names = list(pieces)
    n_w = len(names)

    def body(*refs):
        _gather_body(names, shapes, refs[:n_w], refs[n_w:2 * n_w], *refs[2 * n_w:])

    anyspec = pl.BlockSpec(memory_space=pl.ANY)
    out = pl.pallas_call(
        body, name=name, in_specs=[anyspec] * n_w, out_specs=[anyspec] * n_w,
        out_shape=[jax.ShapeDtypeStruct(tuple(shapes[n]), BF16) for n in names], scratch_shapes=list(_gather_sems(n_w)),
    )(*[pieces[n] for n in names])
    return dict(zip(names, out))


def _gather_weights_async(name, pieces, shapes):
    names = list(pieces)
    n_w = len(names)
    src = [jax.new_ref(pieces[n], memory_space=pltpu.MemorySpace.HBM) for n in names]
    dst = [jax.empty_ref(jax.ShapeDtypeStruct(tuple(shapes[n]), BF16), memory_space=pltpu.MemorySpace.HBM) for n in names]

    @pl.kernel(mesh=plsc.ScalarSubcoreMesh(axis_name="seq", num_cores=1), name=name, scratch_types=_gather_sems(n_w),
               compiler_params=pltpu.CompilerParams(collective_id=GATHER_COLLECTIVE_ID))
    def launch(send_sems, recv_sems, local_sems):
        _handshake(_gather_peers())
        _gather_body(names, shapes, src, dst, send_sems, recv_sems, local_sems)

    launch()
    return {n: d[...] for n, d in zip(names, dst)}


def _halves_view(name, g):
    L, K, N = g.shape
    if name in ROW_SHARDED:
        return g.reshape(L * 4, 2, K // 8, N)
    return g.reshape(L, 2, K // 2, N)


def _exchange_halves(name, views):
    names = list(views)
    n_w = len(names)

    def body(*refs):
        src = refs[:n_w]
        got = refs[n_w:2 * n_w]
        send_sems, recv_sems = refs[2 * n_w:]
        x, y, c, s = _place()
        remote = [pltpu.make_async_remote_copy(src_ref=src[w].at[:, 1 - c], dst_ref=got[w], send_sem=send_sems.at[w],
                                               recv_sem=recv_sems.at[w], device_id=(x, y, 1 - c), device_id_type=MESH)
                  for w in range(n_w)]
        for cp in remote:
            cp.start()
        for cp in remote:
            cp.wait()

    anyspec = pl.BlockSpec(memory_space=pl.ANY)
    out = pl.pallas_call(
        body, name=name, in_specs=[anyspec] * n_w, out_specs=[anyspec] * n_w,
        out_shape=[jax.ShapeDtypeStruct((v.shape[0],) + v.shape[2:], BF16) for v in views.values()],
        scratch_shapes=[pltpu.SemaphoreType.DMA((n_w,)), pltpu.SemaphoreType.DMA((n_w,))],
    )(*views.values())
    return dict(zip(names, out))


def _chip_sum(name, view, got, place):
    A, _, R, C = view.shape
    tc = _pick(C, (2048, 1536, 1408, 1024, 512, 256, 128))
    tr = _pick(R, [t for t in (1024, 512, 256, 128, 64, 32, 16) if t * tc <= 4 * EW_TILE_ELEMS] + [8])

    def body(p_ref, own_ref, got_ref, o_ref):
        o_ref[...] = (own_ref[...].astype(F32) + got_ref[...].astype(F32)).astype(BF16)

    flat = pl.BlockSpec((None, tr, tc), lambda a, i, j, p: (a, i, j))
    return pl.pallas_call(
        body, name=name, out_shape=jax.ShapeDtypeStruct((A, R, C), BF16),
        grid_spec=pltpu.PrefetchScalarGridSpec(
            num_scalar_prefetch=1, grid=(A, R // tr, C // tc),
            in_specs=[pl.BlockSpec((None, None, tr, tc), lambda a, i, j, p: (a, p[0], i, j)), flat], out_specs=flat),
        compiler_params=pltpu.CompilerParams(dimension_semantics=("parallel", "parallel", "parallel"),
                                             vmem_limit_bytes=_vmem_limit(6 * tr * tc, 3 * tr * tc * 4)),
    )(place, view, got)


def _shard_view(name, ps, L):
    return ps.reshape(L, 4, *ps.shape[1:]) if name in ROW_SHARDED else ps


def _scatter_body(names, src, dst, send_sems, recv_sems):
    x, y, c, s = _place()

    def shard(w, t):
        if names[w] in ROW_SHARDED:
            return src[w].at[:, t]
        ns = src[w].shape[2] // 4
        return src[w].at[:, :, pl.ds(pl.multiple_of(t * ns, LANES), ns)]

    remote = []
    for w in range(len(names)):
        for j in (1, 2, 3):
            remote.append(pltpu.make_async_remote_copy(
                src_ref=shard(w, s ^ j), dst_ref=dst[w].at[j - 1], send_sem=send_sems.at[w, j - 1],
                recv_sem=recv_sems.at[w, j - 1], device_id=(*_chip_of(s ^ j), c), device_id_type=MESH))
    for cp in remote:
        cp.start()
    for cp in remote:
        cp.wait()


def _scatter_out_shape(name, v):
    return (3, v[0], v[2], v[3]) if name in ROW_SHARDED else (3, v[0], v[1], v[2] // 4)


def _scatter_sems(n_w):
    return (pltpu.SemaphoreType.DMA((n_w, 3)), pltpu.SemaphoreType.DMA((n_w, 3)))


def _scatter_chip_sums_async(name, psum):
    names = list(psum)
    n_w = len(names)
    src = [jax.new_ref(psum[n], memory_space=pltpu.MemorySpace.HBM) for n in names]
    dst = [jax.empty_ref(jax.ShapeDtypeStruct(_scatter_out_shape(n, psum[n].shape), BF16), memory_space=pltpu.MemorySpace.HBM)
           for n in names]

    @pl.kernel(mesh=plsc.ScalarSubcoreMesh(axis_name="seq", num_cores=1), name=name, scratch_types=_scatter_sems(n_w),
               compiler_params=pltpu.CompilerParams(collective_id=SCATTER_COLLECTIVE_ID))
    def launch(send_sems, recv_sems):
        _handshake(_gather_peers()[1:])
        _scatter_body(names, src, dst, send_sems, recv_sems)

    launch()
    return {n: d[...] for n, d in zip(names, dst)}


def _shard_sum(name, ps, parts, place, row_sharded, layer, n_layers, into):
    _, _, R, C = parts.shape
    tc = _pick(C, (2048, 1408, 1024, 896, 512, 384, 256, 128))
    tr = _pick(R, [t for t in (1024, 512, 256, 128, 64, 32, 16) if t * tc <= 2 * EW_TILE_ELEMS] + [8])

    def body(p_ref, own_ref, a_ref, b_ref, c_ref, *rest):
        o_ref = rest[-1]
        o_ref[...] = ((own_ref[...].astype(F32) + a_ref[...].astype(F32)) + b_ref[...].astype(F32)) + c_ref[...].astype(F32)

    if row_sharded:
        own_spec = pl.BlockSpec((None, None, tr, tc), lambda i, j, p: (0, p[1], i, j))
    else:
        own_spec = pl.BlockSpec((None, tr, tc), lambda i, j, p: (0, i, p[1] * (C // tc) + j))
    part = lambda k: pl.BlockSpec((None, None, tr, tc), lambda i, j, p, k=k: (k, 0, i, j))
    in_specs, args, aliases = [own_spec, part(0), part(1), part(2)], [place, ps, parts, parts, parts], {}
    if into is not None:
        in_specs.append(pl.BlockSpec(memory_space=pl.ANY))
        args.append(into)
        aliases = {5: 0}
    return pl.pallas_call(
        body, name=name, out_shape=jax.ShapeDtypeStruct((n_layers, 2, R, C), F32),
        grid_spec=pltpu.PrefetchScalarGridSpec(
            num_scalar_prefetch=1, grid=(R // tr, C // tc), in_specs=in_specs,
            out_specs=pl.BlockSpec((None, None, tr, tc), lambda i, j, p: (layer, p[0], i, j))),
        input_output_aliases=aliases,
        compiler_params=pltpu.CompilerParams(dimension_semantics=("parallel", "parallel"),
                                             vmem_limit_bytes=_vmem_limit(12 * tr * tc, 5 * tr * tc * 4)),
    )(*args)


def _share_halves(ghalf):
    names = list(ghalf)
    n_w = len(names)

    def body(*refs):
        src = refs[:n_w]
        dst = refs[n_w:2 * n_w]
        send_sems, recv_sems = refs[2 * n_w:]
        x, y, c, s = _place()
        remote = [pltpu.make_async_remote_copy(src_ref=src[w].at[:, c], dst_ref=dst[w].at[:, c], send_sem=send_sems.at[w],
                                               recv_sem=recv_sems.at[w], device_id=(x, y, 1 - c), device_id_type=MESH)
                  for w in range(n_w)]
        for cp in remote:
            cp.start()
        for cp in remote:
            cp.wait()

    anyspec = pl.BlockSpec(memory_space=pl.ANY)
    out = pl.pallas_call(
        body, name="share_halves", in_specs=[anyspec] * n_w, out_specs=[anyspec] * n_w,
        out_shape=[jax.ShapeDtypeStruct(ghalf[n].shape, F32) for n in names],
        input_output_aliases={w: w for w in range(n_w)},
        scratch_shapes=[pltpu.SemaphoreType.DMA((n_w,)), pltpu.SemaphoreType.DMA((n_w,))],
    )(*[ghalf[n] for n in names])
    return dict(zip(names, out))


def _gather_small(v):
    m_per, n = v.shape

    def body(x_ref, out_ref, send_sems, recv_sems, local_sem):
        x, y, c, s = _place()
        me, sibling = (x, y, c), (x, y, 1 - c)
        chips = [(1 - x, y), (x, 1 - y), (1 - x, 1 - y)]

        def rows(px, py, pc):
            return out_ref.at[pl.ds(pl.multiple_of((4 * px + 2 * py + pc) * m_per, 8), m_per), :]

        def copy(k, block, to, src=None):
            return pltpu.make_async_remote_copy(src_ref=rows(*block) if src is None else src, dst_ref=rows(*block),
                                                send_sem=send_sems.at[k], recv_sem=recv_sems.at[k], device_id=to, device_id_type=MESH)

        mine = pltpu.make_async_copy(x_ref, rows(*me), local_sem)
        mine.start()
        first = [copy(0, me, sibling, src=x_ref)]
        first += [copy(1 + j, me, (*chip, c), src=x_ref) for j, chip in enumerate(chips)]
        for cp in first:
            cp.start()
        passed = [copy(4 + j, (*chip, c), sibling) for j, chip in enumerate(chips)]
        for j, chip in enumerate(chips):
            copy(1 + j, (*chip, c), me).wait_recv()
            passed[j].start()
        copy(0, sibling, me).wait_recv()
        for j, chip in enumerate(chips):
            copy(4 + j, (*chip, 1 - c), me).wait_recv()
        for cp in first + passed:
            cp.wait_send()
        mine.wait()

    return pl.pallas_call(
        body, name="gather_small", out_shape=jax.ShapeDtypeStruct((8 * m_per, n), v.dtype),
        in_specs=[pl.BlockSpec(memory_space=pltpu.VMEM)], out_specs=pl.BlockSpec(memory_space=pltpu.VMEM),
        scratch_shapes=[pltpu.SemaphoreType.DMA((7,)), pltpu.SemaphoreType.DMA((7,)), pltpu.SemaphoreType.DMA],
        compiler_params=pltpu.CompilerParams(vmem_limit_bytes=_vmem_limit(9 * m_per * n * 4)),
    )(v)


def _adamw_math(w, g, m, v):
    m = ADAM_B1 * m + (1.0 - ADAM_B1) * g
    v = ADAM_B2 * v + (1.0 - ADAM_B2) * (g * g)
    m_hat = m / (1.0 - ADAM_B1 ** ADAM_STEP)
    v_hat = v / (1.0 - ADAM_B2 ** ADAM_STEP)
    delta = -ADAM_LR * (m_hat / (jnp.sqrt(v_hat) + ADAM_EPS) + ADAM_WD * w)
    return delta, m, v


def _adamw(name, w, g, m, v):
    shape = w.shape
    C = shape[-1]
    R = math.prod(shape[:-1])
    f = lambda a: a.reshape(R, C)
    delta, nm, nv = _ew(name, lambda w_, g_, m_, v_: list(_adamw_math(w_, g_, m_, v_)), [f(w), f(g), f(m), f(v)], [F32] * 3, R, C)
    return delta.reshape(shape), nm.reshape(shape), nv.reshape(shape)


def _pack_small(d):
    return jnp.concatenate([d[n].reshape(-1, LANES) for n in SMALL], axis=0)


def _unpack_small(flat, like):
    out, r = {}, 0
    for n in SMALL:
        k = like[n].size // LANES
        out[n] = flat[r:r + k].reshape(like[n].shape)
        r += k
    return out


def _small_update(gall, w, m, v):
    M = w.shape[0]
    tr = _pick(M, (552, 276, 184, 96, 48, 24, 8))

    def body(*refs):
        g = refs[0][...]
        for d in range(1, 8):
            g = g + refs[d][...]
        delta, nm, nv = _adamw_math(refs[8][...], g, refs[9][...], refs[10][...])
        refs[11][...] = g
        refs[12][...] = delta
        refs[13][...] = nm
        refs[14][...] = nv

    blk = pl.BlockSpec((tr, LANES), lambda i: (i, 0))
    in_specs = [pl.BlockSpec((tr, LANES), lambda i, d=d: (d * (M // tr) + i, 0)) for d in range(8)] + [blk] * 3
    return pl.pallas_call(
        body, name="small_update", grid=(M // tr,), in_specs=in_specs, out_specs=[blk] * 4,
        out_shape=[jax.ShapeDtypeStruct((M, LANES), F32)] * 4,
        compiler_params=pltpu.CompilerParams(dimension_semantics=("parallel",), vmem_limit_bytes=_vmem_limit(15 * tr * LANES * 4)),
    )(*([gall] * 8), w, m, v)


def _step(x, p, target, w, m, v):
    L = p.shape[0]
    x_i, y_i, c, s = _place()
    shapes = {}
    for n in BIG:
        _, K, N = w[n].shape
        shapes[n] = (4 * K, N) if n in ROW_SHARDED else (K, 4 * N)
    def pieces_of(i):
        return {n: lax.dynamic_slice_in_dim(w[n][i], c * (w[n].shape[1] // 2), w[n].shape[1] // 2, axis=0).astype(BF16)
                for n in BIG}

    first = pieces_of(0)
    head = _gather_weights("gather_weights_0_w_in", {"w_in": first.pop("w_in")}, shapes)
    head, first = lax.optimization_barrier((head, first))
    layers = [{**head, **_gather_weights_async("gather_weights_0", first, shapes)}]
    for i in range(1, L):
        layers.append(_gather_weights_async(f"gather_weights_{i}", pieces_of(i), shapes))
    wf = {n: [layers[i][n] for i in range(L)] for n in BIG}
    small = {n: w[n] for n in SMALL}
    place = jnp.stack([c, s]).astype(jnp.int32)
    reduced = []

    def after_group(i, group, grads):
        tag = f"{i}_{group}"
        views = {n: _halves_view(n, g[None]) for n, g in grads.items()}
        got = _exchange_halves(f"exchange_halves_{tag}", views)
        chip_sum = {n: _shard_view(n, _chip_sum(f"chip_sum_{n}_{i}", views[n], got[n], place), 1) for n in grads}
        reduced.append((i, chip_sum, _scatter_chip_sums_async(f"scatter_chip_sums_{tag}", chip_sum)))
        return chip_sum

    loss_cell, dx, gsmall = _local_step(x[0], p[:, 0], target[0], wf, small, after_group)
    loss = lax.psum(jnp.sum(loss_cell), ("x", "y", "c"))
    ghalf = {n: None for n in BIG}
    done = None
    for i, chip_sum, parts in reduced:
        parts, _ = lax.optimization_barrier((parts, done))
        for n in chip_sum:
            ghalf[n] = _shard_sum(f"shard_sum_{n}_{i}", chip_sum[n], parts[n], place, n in ROW_SHARDED, i, L, ghalf[n])
        done = {n: ghalf[n] for n in chip_sum}
    gfull = _share_halves(ghalf)
    grad, delta, new_m, new_v = {}, {}, {}, {}
    for n in BIG:
        grad[n] = gfull[n].reshape(w[n].shape)
        delta[n], new_m[n], new_v[n] = _adamw(f"adamw_{n}", w[n], grad[n], m[n], v[n])
    gall = _gather_small(_pack_small(gsmall))
    gsum, dsm, nms, nvs = _small_update(gall, _pack_small(small), _pack_small({n: m[n] for n in SMALL}),
                                        _pack_small({n: v[n] for n in SMALL}))
    for dst, flat in ((grad, gsum), (delta, dsm), (new_m, nms), (new_v, nvs)):
        dst.update(_unpack_small(flat, small))
    return loss, dx[None], grad, delta, new_m, new_v


def kernel(x, p, w_in, w_br_attn, w_br_sg, w_out, sg_w, sg_b, sg_ln_g, sg_ln_b, norm_mix, norm_ffn, norm_ple, norm_final, w_ff_gate, w_ff_up, w_ff_down, w_ple_gate, w_ple, loss_target, m_w_in, m_w_br_attn, m_w_br_sg, m_w_out, m_sg_w, m_sg_b, m_sg_ln_g, m_sg_ln_b, m_norm_mix, m_norm_ffn, m_norm_ple, m_norm_final, m_w_ff_gate, m_w_ff_up, m_w_ff_down, m_w_ple_gate, m_w_ple, v_w_in, v_w_br_attn, v_w_br_sg, v_w_out, v_sg_w, v_sg_b, v_sg_ln_g, v_sg_ln_b, v_norm_mix, v_norm_ffn, v_norm_ple, v_norm_final, v_w_ff_gate, v_w_ff_up, v_w_ff_down, v_w_ple_gate, v_w_ple):
    w = dict(w_in=w_in, w_br_attn=w_br_attn, w_br_sg=w_br_sg, w_out=w_out, sg_w=sg_w, sg_b=sg_b, sg_ln_g=sg_ln_g, sg_ln_b=sg_ln_b,
             norm_mix=norm_mix, norm_ffn=norm_ffn, norm_ple=norm_ple, norm_final=norm_final, w_ff_gate=w_ff_gate, w_ff_up=w_ff_up,
             w_ff_down=w_ff_down, w_ple_gate=w_ple_gate, w_ple=w_ple)
    m = dict(w_in=m_w_in, w_br_attn=m_w_br_attn, w_br_sg=m_w_br_sg, w_out=m_w_out, sg_w=m_sg_w, sg_b=m_sg_b, sg_ln_g=m_sg_ln_g,
             sg_ln_b=m_sg_ln_b, norm_mix=m_norm_mix, norm_ffn=m_norm_ffn, norm_ple=m_norm_ple, norm_final=m_norm_final,
             w_ff_gate=m_w_ff_gate, w_ff_up=m_w_ff_up, w_ff_down=m_w_ff_down, w_ple_gate=m_w_ple_gate, w_ple=m_w_ple)
    v = dict(w_in=v_w_in, w_br_attn=v_w_br_attn, w_br_sg=v_w_br_sg, w_out=v_w_out, sg_w=v_sg_w, sg_b=v_sg_b, sg_ln_g=v_sg_ln_g,
             sg_ln_b=v_sg_ln_b, norm_mix=v_norm_mix, norm_ffn=v_norm_ffn, norm_ple=v_norm_ple, norm_final=v_norm_final,
             w_ff_gate=v_w_ff_gate, w_ff_up=v_w_ff_up, w_ff_down=v_w_ff_down, w_ple_gate=v_w_ple_gate, w_ple=v_w_ple)
    loss, grad_x, grad, delta, new_m, new_v = _step(x, p, loss_target, w, m, v)
    return (loss, grad_x, *[grad[n] for n in WEIGHTS], *[delta[n] for n in WEIGHTS], *[new_m[n] for n in WEIGHTS],
            *[new_v[n] for n in WEIGHTS])
```

```python
import functools
import math

import jax
import jax.numpy as jnp
from jax import lax
from jax.experimental import pallas as pl
from jax.experimental.pallas import tpu as pltpu
from jax.experimental.pallas import tpu_sc as plsc

F32 = jnp.float32
BF16 = jnp.bfloat16
MESH = pl.DeviceIdType.MESH

HEAD_DIM = 128
ATTN_GROUPS = ((128, 1), (512, 4), (2048, 16))
N_GROUPS = 3
HEADS = 4
QKV_W = 3 * N_GROUPS * HEADS * HEAD_DIM
ATTN_W = HEADS * HEAD_DIM
SG_CHUNK = 128
SG_GROUPS = 8
SG_W = 1024
RADIUS = 64
ROPE_THETA = 10000.0
NORM_EPS = 1e-6
NEG_INF = -1e30
ADAM_LR, ADAM_B1, ADAM_B2, ADAM_EPS, ADAM_WD, ADAM_STEP = 0.001, 0.9, 0.999, 1e-08, 0.01, 10

VMEM_CAP_V7X = 56 * 1024 * 1024
LANES = 128
EW_TILE_ELEMS = 256 * 1024
MM_VMEM_BUDGET = 44 * 1024 * 1024

GATHER_COLLECTIVE_ID = 1
SCATTER_COLLECTIVE_ID = 2

BIG = ("w_in", "w_br_attn", "w_br_sg", "w_out", "w_ff_gate", "w_ff_up", "w_ff_down", "w_ple_gate", "w_ple")
ROW_SHARDED = ("w_out", "w_ff_down", "w_ple_gate")
SMALL_BLOCKS = "small_blocks"
GRAD_GROUPS = {"ffn": ("w_ple_gate", "w_ple", "w_ff_down", "w_ff_gate", "w_ff_up"), "mix": ("w_out", "w_br_attn", "w_br_sg", "w_in")}
SMALL = ("sg_w", "sg_b", "sg_ln_g", "sg_ln_b", "norm_mix", "norm_ffn", "norm_ple", "norm_final")
WEIGHTS = ("w_in", "w_br_attn", "w_br_sg", "w_out", "sg_w", "sg_b", "sg_ln_g", "sg_ln_b", "norm_mix", "norm_ffn",
           "norm_ple", "norm_final", "w_ff_gate", "w_ff_up", "w_ff_down", "w_ple_gate", "w_ple")


def _pick(n, prefs):
    for t in prefs:
        if n % t == 0:
            return t
    return n


def _nbytes(shape, dtype):
    return math.prod(shape) * jnp.dtype(dtype).itemsize


def _vmem_limit(block_bytes, temp_bytes=0):
    est = 2 * block_bytes + temp_bytes
    assert est <= VMEM_CAP_V7X, est
    return VMEM_CAP_V7X


def _sigmoid(x):
    return 1.0 / (1.0 + jnp.exp(-x))


_GELU_C = math.sqrt(2.0 / math.pi)


def _gelu(x):
    return 0.5 * x * (1.0 + jnp.tanh(_GELU_C * (x + 0.044715 * (x * x * x))))


def _gelu_grad(x):
    t = jnp.tanh(_GELU_C * (x + 0.044715 * (x * x * x)))
    return 0.5 * (1.0 + t) + 0.5 * x * (1.0 - t * t) * (_GELU_C * (1.0 + 3.0 * 0.044715 * (x * x)))


def _lead(arr, l, blk, idx):
    if arr.ndim == 2:
        return pl.BlockSpec(blk, idx)
    return pl.BlockSpec((None,) + blk, lambda *g: (l,) + idx(*g))


def _k_steps(prods, tm, tn, fixed_bytes):
    for nk in range(1, 129):
        if any(p["K"] % nk or (p["K"] // nk) % LANES for p in prods):
            continue
        if 2 * sum((tm + tn) * (p["K"] // nk) * 2 for p in prods) + fixed_bytes <= MM_VMEM_BUDGET:
            return nk
    raise ValueError("no contraction split fits VMEM")


def _mm(name, prods, M, N, outs, epilogue, tiles=(), rows=(), tm=1024, tn=1024):
    assert M % tm == 0 and N % tn == 0, (name, M, N, tm, tn)
    fixed = 2 * tm * tn * (sum(t["x"].dtype.itemsize for t in tiles) + sum(jnp.dtype(o["dtype"]).itemsize for o in outs))
    fixed += (len(prods) + 2) * tm * tn * 4
    nk = _k_steps(prods, tm, tn, fixed)
    in_specs, args, block_bytes = [], [], 0
    for p in prods:
        if isinstance(p["b"], (list, tuple)):
            p["b"], p["bl"] = p["b"][p["bl"]], None
        K = p["K"]
        assert K % nk == 0, (name, K, nk)
        tk = K // nk
        p["tk"] = tk
        a_off, bk_off, bn_off = p.get("a_off", 0), p.get("bk_off", 0), p.get("bn_off", 0)
        assert bn_off % tn == 0 and bk_off % tk == 0
        if p["mode"] == "nn":
            assert a_off % tk == 0
            a_spec = _lead(p["a"], p.get("al"), (tm, tk), lambda i, j, k, o=a_off // tk: (i, o + k))
            b_spec = _lead(p["b"], p.get("bl"), (tk, tn), lambda i, j, k, ok=bk_off // tk, on=bn_off // tn: (ok + k, on + j))
        elif p["mode"] == "nt":
            assert a_off % tk == 0
            a_spec = _lead(p["a"], p.get("al"), (tm, tk), lambda i, j, k, o=a_off // tk: (i, o + k))
            b_spec = _lead(p["b"], p.get("bl"), (tn, tk), lambda i, j, k, ok=bk_off // tk, on=bn_off // tn: (on + j, ok + k))
        else:
            assert a_off % tm == 0
            a_spec = _lead(p["a"], p.get("al"), (tk, tm), lambda i, j, k, o=a_off // tm: (k, o + i))
            b_spec = _lead(p["b"], p.get("bl"), (tk, tn), lambda i, j, k, on=bn_off // tn: (k, on + j))
        in_specs += [a_spec, b_spec]
        args += [p["a"], p["b"]]
        block_bytes += (tm + tn) * tk * 2
    for t in tiles:
        off = t.get("off", 0)
        assert off % tn == 0
        in_specs.append(_lead(t["x"], t.get("l"), (tm, tn), lambda i, j, k, o=off // tn: (i, o + j)))
        args.append(t["x"])
        block_bytes += tm * tn * t["x"].dtype.itemsize
    for r in rows:
        in_specs.append(pl.BlockSpec((1, tn), lambda i, j, k: (0, j)))
        args.append(r)
    out_shapes, out_specs, aliases = [], [], {}
    for o_i, o in enumerate(outs):
        off = o.get("col_off", 0)
        assert off % tn == 0
        out_shapes.append(jax.ShapeDtypeStruct(o["shape"], o["dtype"]))
        idx = lambda i, j, k, oo=off // tn: (i, oo + j)
        if len(o["shape"]) == 2:
            out_specs.append(pl.BlockSpec((tm, tn), idx))
        else:
            out_specs.append(pl.BlockSpec((None, tm, tn), lambda i, j, k, l=o["l"], f=idx: (l,) + f(i, j, k)))
        if o.get("alias") is not None:
            aliases[len(args)] = o_i
            in_specs.append(pl.BlockSpec(memory_space=pl.ANY))
            args.append(o["alias"])
        block_bytes += tm * tn * jnp.dtype(o["dtype"]).itemsize
    n_p, n_t, n_r, n_o = len(prods), len(tiles), len(rows), len(outs)
    n_alias = len(aliases)
    modes = [p["mode"] for p in prods]

    def body(*refs):
        ab = refs[: 2 * n_p]
        t_refs = refs[2 * n_p: 2 * n_p + n_t]
        r_refs = refs[2 * n_p + n_t: 2 * n_p + n_t + n_r]
        o_refs = refs[2 * n_p + n_t + n_r + n_alias: 2 * n_p + n_t + n_r + n_alias + n_o]
        acc_refs = refs[2 * n_p + n_t + n_r + n_alias + n_o:]
        dims = {"nn": (((1,), (0,)), ((), ())), "nt": (((1,), (1,)), ((), ())), "tn": (((0,), (0,)), ((), ()))}

        def part(q):
            return lax.dot_general(ab[2 * q][...], ab[2 * q + 1][...], dims[modes[q]], preferred_element_type=F32)

        def finish(accs):
            res = epilogue(accs, [t[...] for t in t_refs], [r[...] for r in r_refs])
            for o_ref, val in zip(o_refs, res, strict=True):
                o_ref[...] = val.astype(o_ref.dtype)

        if nk == 1:
            finish([part(q) for q in range(n_p)])
        else:
            k = pl.program_id(2)

            @pl.when(k == 0)
            def _():
                for q, acc in enumerate(acc_refs):
                    acc[...] = part(q)

            @pl.when(k > 0)
            def _():
                for q, acc in enumerate(acc_refs):
                    acc[...] += part(q)

            @pl.when(k == nk - 1)
            def _():
                finish([acc[...] for acc in acc_refs])

    scratch = [pltpu.VMEM((tm, tn), F32) for _ in prods] if nk > 1 else []
    temp = (n_p + 2) * tm * tn * 4
    res = pl.pallas_call(
        body, name=name, grid=(M // tm, N // tn, nk), in_specs=in_specs, out_specs=out_specs, out_shape=out_shapes,
        scratch_shapes=scratch, input_output_aliases=aliases,
        compiler_params=pltpu.CompilerParams(dimension_semantics=("parallel", "parallel", "arbitrary"),
                                             vmem_limit_bytes=_vmem_limit(block_bytes, temp)),
    )(*args)
    return res


def _first(accs, tiles, rows):
    return [accs[0]]


def _ew(name, fn, ins, outs, R, C, tr=None, tc=None):
    tc = tc or _pick(C, (2048, 1536, 1408, 1024, 896, 512, 384, 256, 128))
    tr = tr or _pick(R, [t for t in (512, 256, 128, 64, 32, 16) if t * tc <= EW_TILE_ELEMS] + [8])
    in_specs, args, bb = [], [], 0
    for x in ins:
        if isinstance(x, tuple):
            arr, l = x
            in_specs.append(pl.BlockSpec((None, tr, tc), lambda i, j, l=l: (l, i, j)))
        else:
            arr = x
            in_specs.append(pl.BlockSpec((tr, tc), lambda i, j: (i, j)))
        args.append(arr)
        bb += tr * tc * arr.dtype.itemsize
    out_shapes = [jax.ShapeDtypeStruct((R, C), d) for d in outs]
    out_specs = [pl.BlockSpec((tr, tc), lambda i, j: (i, j)) for _ in outs]
    bb += sum(tr * tc * jnp.dtype(d).itemsize for d in outs)
    n_in = len(ins)

    def body(*refs):
        res = fn(*[r[...] for r in refs[:n_in]])
        for o_ref, val in zip(refs[n_in:], res, strict=True):
            o_ref[...] = val.astype(o_ref.dtype)

    return pl.pallas_call(
        body, name=name, grid=(R // tr, C // tc), in_specs=in_specs, out_specs=out_specs, out_shape=out_shapes,
        compiler_params=pltpu.CompilerParams(dimension_semantics=("parallel", "parallel"),
                                             vmem_limit_bytes=_vmem_limit(bb, 6 * tr * tc * 4)),
    )(*args)


def _rmsnorm_fwd(name, x, g):
    S, D = x.shape
    tr = _pick(S, (256, 128, 64, 8))

    def body(x_ref, g_ref, h_ref):
        xv = x_ref[...]
        r = lax.rsqrt(jnp.mean(xv * xv, axis=-1, keepdims=True) + NORM_EPS)
        h_ref[...] = (xv * r * g_ref[...]).astype(BF16)

    return pl.pallas_call(
        body, name=name, grid=(S // tr,),
        in_specs=[pl.BlockSpec((tr, D), lambda i: (i, 0)), pl.BlockSpec((1, D), lambda i: (0, 0))],
        out_specs=pl.BlockSpec((tr, D), lambda i: (i, 0)), out_shape=jax.ShapeDtypeStruct((S, D), BF16),
        compiler_params=pltpu.CompilerParams(dimension_semantics=("parallel",),
                                             vmem_limit_bytes=_vmem_limit(tr * D * 6, 3 * tr * D * 4)),
    )(x, g)


def _rmsnorm_bwd(name, x, g, dh, dres):
    S, D = x.shape
    tr = _pick(S, (256, 128, 64, 8))

    def body(x_ref, g_ref, dh_ref, dres_ref, dx_ref, dxb_ref, dg_ref):
        xv = x_ref[...]
        dy = dh_ref[...].astype(F32)
        r = lax.rsqrt(jnp.mean(xv * xv, axis=-1, keepdims=True) + NORM_EPS)
        a = dy * g_ref[...]
        dx = dres_ref[...] + r * a - xv * (r * r * r) * jnp.mean(a * xv, axis=-1, keepdims=True)
        dx_ref[...] = dx
        dxb_ref[...] = dx.astype(BF16)
        part = jnp.sum(dy * xv * r, axis=0, keepdims=True)

        @pl.when(pl.program_id(0) == 0)
        def _():
            dg_ref[...] = part

        @pl.when(pl.program_id(0) > 0)
        def _():
            dg_ref[...] += part

    row = pl.BlockSpec((tr, D), lambda i: (i, 0))
    vec = pl.BlockSpec((1, D), lambda i: (0, 0))
    return pl.pallas_call(
        body, name=name, grid=(S // tr,), in_specs=[row, vec, row, row], out_specs=[row, row, vec],
        out_shape=[jax.ShapeDtypeStruct((S, D), F32), jax.ShapeDtypeStruct((S, D), BF16), jax.ShapeDtypeStruct((1, D), F32)],
        compiler_params=pltpu.CompilerParams(dimension_semantics=("arbitrary",),
                                             vmem_limit_bytes=_vmem_limit(tr * D * 18, 5 * tr * D * 4)),
    )(x, g, dh, dres)


def _loss_head(x, g, target):
    S, D = x.shape
    tr = _pick(S, (256, 128, 64, 8))

    def body(x_ref, g_ref, t_ref, loss_ref, dx_ref, dxb_ref, dg_ref):
        xv = x_ref[...]
        r = lax.rsqrt(jnp.mean(xv * xv, axis=-1, keepdims=True) + NORM_EPS)
        xn = xv * r
        diff = xn * g_ref[...] - t_ref[...]
        dy = diff * (1.0 / D)
        a = dy * g_ref[...]
        dx = r * a - xv * (r * r * r) * jnp.mean(a * xv, axis=-1, keepdims=True)
        dx_ref[...] = dx
        dxb_ref[...] = dx.astype(BF16)
        part = jnp.sum(dy * xn, axis=0, keepdims=True)
        cell = (lax.broadcasted_iota(jnp.int32, (8, LANES), 0) == 0) & (lax.broadcasted_iota(jnp.int32, (8, LANES), 1) == 0)
        lpart = jnp.where(cell, 0.5 * jnp.sum(jnp.mean(diff * diff, axis=-1, keepdims=True)), 0.0)

        @pl.when(pl.program_id(0) == 0)
        def _():
            dg_ref[...] = part
            loss_ref[...] = lpart

        @pl.when(pl.program_id(0) > 0)
        def _():
            dg_ref[...] += part
            loss_ref[...] += lpart

    row = pl.BlockSpec((tr, D), lambda i: (i, 0))
    vec = pl.BlockSpec((1, D), lambda i: (0, 0))
    return pl.pallas_call(
        body, name="loss_head", grid=(S // tr,), in_specs=[row, vec, row],
        out_specs=[pl.BlockSpec((8, LANES), lambda i: (0, 0)), row, row, vec],
        out_shape=[jax.ShapeDtypeStruct((8, LANES), F32), jax.ShapeDtypeStruct((S, D), F32),
                   jax.ShapeDtypeStruct((S, D), BF16), jax.ShapeDtypeStruct((1, D), F32)],
        compiler_params=pltpu.CompilerParams(dimension_semantics=("arbitrary",),
                                             vmem_limit_bytes=_vmem_limit(tr * D * 14, 6 * tr * D * 4)),
    )(x, g, target)


def _rope_tables(S):
    pos = jnp.arange(S, dtype=F32)
    inv_freq = ROPE_THETA ** (-jnp.arange(0, HEAD_DIM, 2, dtype=F32) / HEAD_DIM)
    ang = pos[:, None] * inv_freq[None, :]
    cos, sin = jnp.cos(ang), jnp.sin(ang)
    return jnp.concatenate([cos, cos], axis=-1), jnp.concatenate([-sin, sin], axis=-1)


def _rope_fwd(name, z, cosf, sinf):
    S = z.shape[0]
    tr = _pick(S, (256, 128, 64, 8))
    n_q = N_GROUPS * HEADS

    def body(z_ref, c_ref, s_ref, o_ref):
        c, s = c_ref[...], s_ref[...]
        for j in range(QKV_W // HEAD_DIM):
            t = z_ref[:, j * HEAD_DIM:(j + 1) * HEAD_DIM]
            if j < 2 * n_q:
                t = t * c + pltpu.roll(t, HEAD_DIM // 2, axis=1) * s
            if j < n_q:
                t = t * ATTN_SCALE
            o_ref[:, j * HEAD_DIM:(j + 1) * HEAD_DIM] = t.astype(BF16)

    tab = pl.BlockSpec((tr, HEAD_DIM), lambda i: (i, 0))
    return pl.pallas_call(
        body, name=name, grid=(S // tr,), in_specs=[pl.BlockSpec((tr, QKV_W), lambda i: (i, 0)), tab, tab],
        out_specs=pl.BlockSpec((tr, QKV_W), lambda i: (i, 0)), out_shape=jax.ShapeDtypeStruct((S, QKV_W), BF16),
        compiler_params=pltpu.CompilerParams(dimension_semantics=("parallel",),
                                             vmem_limit_bytes=_vmem_limit(tr * QKV_W * 6, tr * QKV_W * 4)),
    )(z, cosf, sinf)


def _rope_bwd(name, dq, dk, dv, dzuv, cosf, sinf, dz):
    S = dq.shape[0]
    tr = _pick(S, (256, 128, 64, 8))
    W3 = QKV_W // 3
    nh = W3 // HEAD_DIM
    wide = QKV_W + dzuv.shape[1]

    def body(dq_ref, dk_ref, dv_ref, uv_ref, c_ref, s_ref, dz_in, o_ref):
        c, s = c_ref[...], s_ref[...]
        for part, ref in enumerate((dq_ref, dk_ref)):
            for j in range(nh):
                t = ref[:, j * HEAD_DIM:(j + 1) * HEAD_DIM].astype(F32)
                t = t * c - pltpu.roll(t, HEAD_DIM // 2, axis=1) * s
                o_ref[:, part * W3 + j * HEAD_DIM: part * W3 + (j + 1) * HEAD_DIM] = t.astype(BF16)
        o_ref[:, 2 * W3:QKV_W] = dv_ref[...]
        o_ref[:, QKV_W:] = uv_ref[...]

    third = pl.BlockSpec((tr, W3), lambda i: (i, 0))
    tab = pl.BlockSpec((tr, HEAD_DIM), lambda i: (i, 0))
    return pl.pallas_call(
        body, name=name, grid=(S // tr,),
        in_specs=[third, third, third, pl.BlockSpec((tr, dzuv.shape[1]), lambda i: (i, 0)), tab, tab, pl.BlockSpec(memory_space=pl.ANY)],
        out_specs=pl.BlockSpec((tr, wide), lambda i: (i, 0)), out_shape=jax.ShapeDtypeStruct(dz.shape, dz.dtype),
        input_output_aliases={6: 0},
        compiler_params=pltpu.CompilerParams(dimension_semantics=("parallel",),
                                             vmem_limit_bytes=_vmem_limit(tr * wide * 4, tr * wide * 4)),
    )(dq, dk, dv, dzuv, cosf, sinf, dz)


ATTN_TQ = 256
ATTN_SCALE = HEAD_DIM ** -0.5
ATTN_PAD_MAX = RADIUS * max(d for _, d in ATTN_GROUPS)


def _band_bias(shape, q_axis, d):
    kq = lax.broadcasted_iota(jnp.int32, shape, 1 - q_axis) - lax.broadcasted_iota(jnp.int32, shape, q_axis) - RADIUS * d
    return jnp.where((jnp.abs(kq) <= RADIUS * d) & ((kq & (d - 1)) == 0), 0.0, NEG_INF).astype(F32)


def _fill_padded(dst, src, d, S):
    pad = RADIUS * d
    dst[0:pad, :] = jnp.zeros((pad, HEAD_DIM), dst.dtype)
    dst[pad:pad + S, :] = src[...]
    dst[pad + S:pad + S + pad, :] = jnp.zeros((pad, HEAD_DIM), dst.dtype)


_NT = (((1,), (1,)), ((), ()))


def _attn_fwd(name, qkv):
    S = qkv.shape[0]
    T = ATTN_TQ
    nq = N_GROUPS * HEADS
    widths = [T + 2 * RADIUS * d for _, d in ATTN_GROUPS]

    def body(*refs):
        q_refs, k_refs, v_refs = refs[0:3], refs[3:6], refs[6:9]
        o_ref, lc_ref = refs[9:11]
        kp, vp, bias = refs[11:14], refs[14:17], refs[17:20]
        i0 = pl.multiple_of(pl.program_id(1) * T, T)

        @pl.when(pl.program_id(1) == 0)
        def _():
            for g, (_, d) in enumerate(ATTN_GROUPS):
                _fill_padded(kp[g], k_refs[g], d, S)
                _fill_padded(vp[g], v_refs[g], d, S)
                bias[g][...] = _band_bias((T, widths[g]), 0, d)

        m = jnp.full((T, 1), NEG_INF, F32)
        l = jnp.zeros((T, 1), F32)
        acc = jnp.zeros((T, HEAD_DIM), F32)
        for g, (_, d) in enumerate(ATTN_GROUPS):
            W = widths[g]
            kw = kp[g][pl.ds(i0, W), :]
            vw = vp[g][pl.ds(i0, W), :]
            key = i0 - RADIUS * d + lax.broadcasted_iota(jnp.int32, (1, W), 1)
            in_seq = jnp.where((key >= 0) & (key < S), 0.0, NEG_INF).astype(F32)
            s = lax.dot_general(q_refs[g][...], kw, _NT, preferred_element_type=F32) + bias[g][...] + in_seq
            m_new = jnp.maximum(m, jnp.max(s, axis=1, keepdims=True))
            alpha = jnp.exp(m - m_new)
            p = jnp.exp(s - m_new)
            l = l * alpha + jnp.sum(p, axis=1, keepdims=True)
            acc = acc * alpha + jnp.dot(p.astype(BF16), vw, preferred_element_type=F32)
            m = m_new
        o_ref[...] = (acc / l).astype(BF16)
        lc_ref[...] = m + jnp.log(l)

    in_specs = [pl.BlockSpec((T, HEAD_DIM), lambda h, i, g=g: (i, g * HEADS + h)) for g in range(N_GROUPS)]
    in_specs += [pl.BlockSpec((S, HEAD_DIM), lambda h, i, g=g: (0, nq + g * HEADS + h)) for g in range(N_GROUPS)]
    in_specs += [pl.BlockSpec((S, HEAD_DIM), lambda h, i, g=g: (0, 2 * nq + g * HEADS + h)) for g in range(N_GROUPS)]
    padded = [pltpu.VMEM((S + 2 * RADIUS * d, HEAD_DIM), BF16) for _, d in ATTN_GROUPS]
    scratch = padded + padded + [pltpu.VMEM((T, W), F32) for W in widths]
    scratch_bytes = sum(2 * (S + 2 * RADIUS * d) * HEAD_DIM * 2 for _, d in ATTN_GROUPS) + sum(T * W * 4 for W in widths)
    return pl.pallas_call(
        body, name=name, grid=(HEADS, S // T), in_specs=in_specs,
        out_specs=[pl.BlockSpec((T, HEAD_DIM), lambda h, i: (i, h)), pl.BlockSpec((None, T, 1), lambda h, i: (h, i, 0))],
        out_shape=[jax.ShapeDtypeStruct((S, ATTN_W), BF16), jax.ShapeDtypeStruct((HEADS, S, 1), F32)],
        scratch_shapes=scratch,
        compiler_params=pltpu.CompilerParams(dimension_semantics=("parallel", "arbitrary"),
                                             vmem_limit_bytes=_vmem_limit(6 * S * HEAD_DIM * 2 + 8 * T * HEAD_DIM * 4,
                                                                          scratch_bytes + 4 * T * widths[-1] * 4)),
    )(*([qkv] * 9))


_TN = (((0,), (0,)), ((), ()))


def _attn_bwd(name, qkv, attn, dattn, lse_c):
    S = qkv.shape[0]
    T = ATTN_TQ
    nq = N_GROUPS * HEADS
    W3 = QKV_W // 3
    n_i = S // T
    wmax = T + 2 * ATTN_PAD_MAX
    s_pad = S + 2 * ATTN_PAD_MAX

    def body(q_ref, k_ref, v_ref, o_ref, do_ref, lc_ref, dq_ref, dk_ref, dv_ref, kp, vp, dk_acc, dv_acc, bias):
        g_id, i = pl.program_id(1), pl.program_id(2)
        i0 = pl.multiple_of(i * T, T)
        q, do = q_ref[...], do_ref[...]
        delta = jnp.sum(do.astype(F32) * o_ref[...].astype(F32), axis=1, keepdims=True)
        lse = lc_ref[...]

        def group(d):
            W, pad = T + 2 * RADIUS * d, RADIUS * d

            @pl.when(i == 0)
            def _():
                _fill_padded(kp, k_ref, d, S)
                _fill_padded(vp, v_ref, d, S)
                dk_acc[...] = jnp.zeros_like(dk_acc)
                dv_acc[...] = jnp.zeros_like(dv_acc)
                bias[:, 0:W] = _band_bias((T, W), 0, d)

            kw = kp[pl.ds(i0, W), :]
            vw = vp[pl.ds(i0, W), :]
            key = i0 - pad + lax.broadcasted_iota(jnp.int32, (1, W), 1)
            in_seq = jnp.where((key >= 0) & (key < S), 0.0, NEG_INF).astype(F32)
            s = lax.dot_general(q, kw, _NT, preferred_element_type=F32) + bias[:, 0:W] + in_seq
            p = jnp.exp(s - lse)
            dp = lax.dot_general(do, vw, _NT, preferred_element_type=F32)
            ds = (p * (dp - delta)).astype(BF16)
            dq_ref[...] = (jnp.dot(ds, kw, preferred_element_type=F32) * ATTN_SCALE).astype(BF16)
            dk_acc[pl.ds(i0, W), :] += lax.dot_general(ds, q, _TN, preferred_element_type=F32)
            dv_acc[pl.ds(i0, W), :] += lax.dot_general(p.astype(BF16), do, _TN, preferred_element_type=F32)

            @pl.when(i == n_i - 1)
            def _():
                dk_ref[...] = dk_acc[pad:pad + S, :].astype(BF16)
                dv_ref[...] = dv_acc[pad:pad + S, :].astype(BF16)

        for g, (_, d) in enumerate(ATTN_GROUPS):
            pl.when(g_id == g)(functools.partial(group, d))

    tile = lambda off: pl.BlockSpec((T, HEAD_DIM), lambda h, g, i: (i, off + g * HEADS + h))
    full = lambda off: pl.BlockSpec((S, HEAD_DIM), lambda h, g, i: (0, off + g * HEADS + h))
    headt = pl.BlockSpec((T, HEAD_DIM), lambda h, g, i: (i, h))
    scratch_bytes = 2 * s_pad * HEAD_DIM * (2 + 4) + T * wmax * 4
    return pl.pallas_call(
        body, name=name, grid=(HEADS, N_GROUPS, n_i),
        in_specs=[tile(0), full(nq), full(2 * nq), headt, headt, pl.BlockSpec((None, T, 1), lambda h, g, i: (h, i, 0))],
        out_specs=[tile(0), full(0), full(0)],
        out_shape=[jax.ShapeDtypeStruct((S, W3), BF16)] * 3,
        scratch_shapes=[pltpu.VMEM((s_pad, HEAD_DIM), BF16), pltpu.VMEM((s_pad, HEAD_DIM), BF16),
                        pltpu.VMEM((s_pad, HEAD_DIM), F32), pltpu.VMEM((s_pad, HEAD_DIM), F32), pltpu.VMEM((T, wmax), F32)],
        compiler_params=pltpu.CompilerParams(dimension_semantics=("parallel", "arbitrary", "arbitrary"),
                                             vmem_limit_bytes=_vmem_limit(4 * S * HEAD_DIM * 2 + 8 * T * HEAD_DIM * 4,
                                                                          scratch_bytes + 5 * T * wmax * 4)),
    )(qkv, qkv, qkv, attn, dattn, lse_c)


def _sg_parts(u, v, lng, lnb):
    gu = _gelu(u)
    gv = _gelu(v)
    mu = jnp.mean(gv, axis=-1, keepdims=True)
    xc = gv - mu
    rstd = lax.rsqrt(jnp.mean(xc * xc, axis=-1, keepdims=True) + NORM_EPS)
    xhat = xc * rstd
    vn = xhat * lng + lnb
    return gu, xhat, rstd, vn


def _sg_fwd(name, z, sg_w, sg_bc, lng, lnb, o_sg0):
    S = z.shape[0]
    T = SG_CHUNK
    cb = 512
    assert o_sg0 % cb == 0
    b0 = o_sg0 // cb

    def body(u0, u1, v0, v1, w_ref, b_ref, g_ref, be_ref, o_ref):
        u = jnp.concatenate([u0[...], u1[...]], axis=1)
        v = jnp.concatenate([v0[...], v1[...]], axis=1)
        gu, _, _, vn = _sg_parts(u, v, g_ref[...], be_ref[...])
        vnb = vn.astype(BF16)
        for g in range(SG_GROUPS):
            sl = slice(g * SG_CHUNK, (g + 1) * SG_CHUNK)
            mixed = jnp.dot(w_ref[g], vnb[:, sl], preferred_element_type=F32) + b_ref[g]
            o_ref[:, sl] = (gu[:, sl] * mixed).astype(BF16)

    zs = lambda k: pl.BlockSpec((T, cb), lambda i, k=k: (i, b0 + k))
    const3 = lambda shp: pl.BlockSpec(shp, lambda i: (0, 0, 0))
    vec = pl.BlockSpec((1, SG_W), lambda i: (0, 0))
    return pl.pallas_call(
        body, name=name, grid=(S // T,),
        in_specs=[zs(0), zs(1), zs(2), zs(3), const3((SG_GROUPS, SG_CHUNK, SG_CHUNK)), const3((SG_GROUPS, SG_CHUNK, 1)), vec, vec],
        out_specs=pl.BlockSpec((T, SG_W), lambda i: (i, 0)), out_shape=jax.ShapeDtypeStruct((S, SG_W), BF16),
        compiler_params=pltpu.CompilerParams(dimension_semantics=("parallel",), vmem_limit_bytes=_vmem_limit(4 * 1024 * 1024, 8 * T * SG_W * 4)),
    )(z, z, z, z, sg_w, sg_bc, lng, lnb)


def _sg_bwd(name, z, dsg, sg_w, sg_wt, sg_bc, lng, lnb, o_sg0):
    S = z.shape[0]
    T = SG_CHUNK
    cb = 512
    b0 = o_sg0 // cb

    def body(u0, u1, v0, v1, d_ref, w_ref, wt_ref, b_ref, g_ref, be_ref, dz_ref, dw_ref, db_ref, dg_ref, dbe_ref):
        i = pl.program_id(0)
        u = jnp.concatenate([u0[...], u1[...]], axis=1)
        v = jnp.concatenate([v0[...], v1[...]], axis=1)
        gu, xhat, rstd, vn = _sg_parts(u, v, g_ref[...], be_ref[...])
        vnb = vn.astype(BF16)
        dsg_v = d_ref[...].astype(F32)
        dmix = dsg_v * gu
        dmixb = dmix.astype(BF16)
        dvn_parts, mixed_parts, dw_parts, db_parts = [], [], [], []
        for g in range(SG_GROUPS):
            sl = slice(g * SG_CHUNK, (g + 1) * SG_CHUNK)
            mixed_parts.append(jnp.dot(w_ref[g], vnb[:, sl], preferred_element_type=F32) + b_ref[g])
            dvn_parts.append(jnp.dot(wt_ref[g], dmixb[:, sl], preferred_element_type=F32))
            dw_parts.append(lax.dot_general(dmixb[:, sl], vnb[:, sl], _NT, preferred_element_type=F32))
            db_parts.append(jnp.sum(dmix[:, sl], axis=1, keepdims=True))
        mixed = jnp.concatenate(mixed_parts, axis=1)
        dvn = jnp.concatenate(dvn_parts, axis=1)
        dzu = dsg_v * mixed * _gelu_grad(u)
        dxh = dvn * g_ref[...]
        dgv = rstd * (dxh - jnp.mean(dxh, axis=-1, keepdims=True) - xhat * jnp.mean(dxh * xhat, axis=-1, keepdims=True))
        dzv = dgv * _gelu_grad(v)
        dz_ref[:, :SG_W] = dzu.astype(BF16)
        dz_ref[:, SG_W:] = dzv.astype(BF16)
        dgp = jnp.sum(dvn * xhat, axis=0, keepdims=True)
        dbp = jnp.sum(dvn, axis=0, keepdims=True)

        @pl.when(i == 0)
        def _():
            for g in range(SG_GROUPS):
                dw_ref[g] = dw_parts[g]
                db_ref[g] = db_parts[g]
            dg_ref[...] = dgp
            dbe_ref[...] = dbp

        @pl.when(i > 0)
        def _():
            for g in range(SG_GROUPS):
                dw_ref[g] += dw_parts[g]
                db_ref[g] += db_parts[g]
            dg_ref[...] += dgp
            dbe_ref[...] += dbp

    zs = lambda k: pl.BlockSpec((T, cb), lambda i, k=k: (i, b0 + k))
    const3 = lambda shp: pl.BlockSpec(shp, lambda i: (0, 0, 0))
    vec = pl.BlockSpec((1, SG_W), lambda i: (0, 0))
    return pl.pallas_call(
        body, name=name, grid=(S // T,),
        in_specs=[zs(0), zs(1), zs(2), zs(3), pl.BlockSpec((T, SG_W), lambda i: (i, 0)),
                  const3((SG_GROUPS, SG_CHUNK, SG_CHUNK)), const3((SG_GROUPS, SG_CHUNK, SG_CHUNK)), const3((SG_GROUPS, SG_CHUNK, 1)),
                  vec, vec],
        out_specs=[pl.BlockSpec((T, 2 * SG_W), lambda i: (i, 0)), const3((SG_GROUPS, SG_CHUNK, SG_CHUNK)),
                   const3((SG_GROUPS, SG_CHUNK, 1)), vec, vec],
        out_shape=[jax.ShapeDtypeStruct((S, 2 * SG_W), BF16), jax.ShapeDtypeStruct((SG_GROUPS, SG_CHUNK, SG_CHUNK), F32),
                   jax.ShapeDtypeStruct((SG_GROUPS, SG_CHUNK, 1), F32), jax.ShapeDtypeStruct((1, SG_W), F32),
                   jax.ShapeDtypeStruct((1, SG_W), F32)],
        compiler_params=pltpu.CompilerParams(dimension_semantics=("arbitrary",),
                                             vmem_limit_bytes=_vmem_limit(6 * 1024 * 1024, 16 * T * SG_W * 4)),
    )(z, z, z, z, dsg, sg_w, sg_wt, sg_bc, lng, lnb)


def _gate_bwd(name, z, dmerged, y_attn, y_sg, o_g0, in_w):
    S, D = dmerged.shape
    tr = _pick(S, (512, 256, 128, 8))
    cb = _pick(D, (512, 256, 128))
    assert o_g0 % cb == 0
    nd = D // cb
    b0 = o_g0 // cb

    def body(z_ref, dm_ref, ya_ref, ys_ref, dz_ref, dy_ref):
        jj = pl.program_id(1)
        gate = _sigmoid(z_ref[...])
        dm = dm_ref[...].astype(F32)
        y = jnp.where(jj < nd, ya_ref[...], ys_ref[...]).astype(F32)
        dz_ref[...] = (dm * y * gate * (1.0 - gate)).astype(BF16)
        dy_ref[...] = (dm * gate).astype(BF16)

    half = pl.BlockSpec((tr, cb), lambda i, jj: (i, jj % nd))
    return pl.pallas_call(
        body, name=name, grid=(S // tr, 2 * nd),
        in_specs=[pl.BlockSpec((tr, cb), lambda i, jj: (i, b0 + jj)), half, half, half],
        out_specs=[pl.BlockSpec((tr, cb), lambda i, jj: (i, b0 + jj)), pl.BlockSpec((tr, cb), lambda i, jj: (i, jj))],
        out_shape=[jax.ShapeDtypeStruct((S, in_w), BF16), jax.ShapeDtypeStruct((S, 2 * D), BF16)],
        compiler_params=pltpu.CompilerParams(dimension_semantics=("parallel", "arbitrary"),
                                             vmem_limit_bytes=_vmem_limit(tr * cb * 14, 6 * tr * cb * 4)),
    )(z, dmerged, y_attn, y_sg)


def _row(v):
    return v.reshape(1, -1)


def _local_step(x, p, target, wf, small, after_group):
    S, D = x.shape
    L = p.shape[0]
    in_w = wf["w_in"][0].shape[1]
    ff = wf["w_ff_gate"][0].shape[1]
    ple = p.shape[2]
    o_sg0, o_g0 = QKV_W, QKV_W + 2 * SG_W
    cosf, sinf = _rope_tables(S)
    pb = p.astype(BF16)
    tmb = _pick(S, (1024, 512, 256))
    tn_in = _pick(in_w, (768, 1024, 512))
    tn_d = _pick(D, (1024, 512, 256))
    tn_g = _pick(D, (512, 256))
    tn_ff = _pick(ff, (512, 256))

    saved = []
    xs = x
    for i in range(L):
        sv = {"x0": xs}
        h = _rmsnorm_fwd(f"norm_mix_{i}", xs, _row(small["norm_mix"][i]))
        (z,) = _mm(f"in_proj_{i}", [dict(a=h, b=wf["w_in"], bl=i, mode="nn", K=D)], S, in_w,
                   [dict(shape=(S, in_w), dtype=F32)], _first, tm=tmb, tn=tn_in)
        qkv = _rope_fwd(f"rope_{i}", z, cosf, sinf)
        attn, lse_c = _attn_fwd(f"attn_{i}", qkv)
        sgw = small["sg_w"][i].astype(BF16)
        sgbc = small["sg_b"][i].reshape(SG_GROUPS, SG_CHUNK, 1)
        sg = _sg_fwd(f"sgu_{i}", z, sgw, sgbc, _row(small["sg_ln_g"][i]), _row(small["sg_ln_b"][i]), o_sg0)

        def merge(accs, tiles, rows):
            ya, ys = accs[0].astype(BF16), accs[1].astype(BF16)
            g0, g1 = _sigmoid(tiles[0]), _sigmoid(tiles[1])
            return [ya, ys, g0 * ya.astype(F32) + g1 * ys.astype(F32)]

        y_attn, y_sg, merged = _mm(
            f"branches_{i}",
            [dict(a=attn, b=wf["w_br_attn"], bl=i, mode="nn", K=ATTN_W), dict(a=sg, b=wf["w_br_sg"], bl=i, mode="nn", K=SG_W)],
            S, D, [dict(shape=(S, D), dtype=BF16)] * 3, merge,
            tiles=[dict(x=z, off=o_g0), dict(x=z, off=o_g0 + D)], tm=tmb, tn=tn_g)
        (x1,) = _mm(f"out_proj_{i}", [dict(a=merged, b=wf["w_out"], bl=i, mode="nn", K=D)], S, D,
                    [dict(shape=(S, D), dtype=F32)], lambda a, t, r: [t[0] + a[0]], tiles=[dict(x=xs)], tm=tmb, tn=tn_d)
        h2 = _rmsnorm_fwd(f"norm_ffn_{i}", x1, _row(small["norm_ffn"][i]))

        def swiglu(accs, tiles, rows):
            fg = accs[0].astype(BF16).astype(F32)
            fu = accs[1].astype(BF16).astype(F32)
            return [fg, fu, fg * _sigmoid(fg) * fu]

        ffg, ffu, act = _mm(
            f"ff_in_{i}",
            [dict(a=h2, b=wf["w_ff_gate"], bl=i, mode="nn", K=D), dict(a=h2, b=wf["w_ff_up"], bl=i, mode="nn", K=D)],
            S, ff, [dict(shape=(S, ff), dtype=BF16)] * 3, swiglu, tm=tmb, tn=tn_ff)
        (x2,) = _mm(f"ff_out_{i}", [dict(a=act, b=wf["w_ff_down"], bl=i, mode="nn", K=ff)], S, D,
                    [dict(shape=(S, D), dtype=F32)], lambda a, t, r: [t[0] + a[0]], tiles=[dict(x=x1)], tm=tmb, tn=tn_d)
        h3 = _rmsnorm_fwd(f"norm_ple_{i}", x2, _row(small["norm_ple"][i]))

        def ple_mix(accs, tiles, rows):
            gp = _sigmoid(accs[0]).astype(BF16)
            pe = accs[1].astype(BF16)
            return [tiles[0] + gp.astype(F32) * pe.astype(F32), gp, pe]

        x3, gp, pe = _mm(
            f"ple_{i}",
            [dict(a=h3, b=wf["w_ple_gate"], bl=i, mode="nn", K=D), dict(a=pb, al=i, b=wf["w_ple"], bl=i, mode="nn", K=ple)],
            S, D, [dict(shape=(S, D), dtype=F32), dict(shape=(S, D), dtype=BF16), dict(shape=(S, D), dtype=BF16)], ple_mix,
            tiles=[dict(x=x2)], tm=tmb, tn=tn_g)
        sv.update(h=h, z=z, qkv=qkv, attn=attn, lse_c=lse_c, sg=sg, y_attn=y_attn, y_sg=y_sg, merged=merged,
                  x1=x1, h2=h2, ffg=ffg, ffu=ffu, act=act, x2=x2, h3=h3, gp=gp, pe=pe, sgw=sgw, sgbc=sgbc)
        saved.append(sv)
        xs = x3

    loss_cell, dx, dxb, dg_final = _loss_head(xs, _row(small["norm_final"]), target)

    gw = {n: [None] * L for n in BIG}
    gs = {n: [None] * L for n in SMALL if n != "norm_final"}

    def dw(n, i, a, a_off, b, bn_off, K_rows, N_cols, tm, tn):
        (gw[n][i],) = _mm(f"d_{n}_{i}", [dict(a=a, b=b, mode="tn", K=S, a_off=a_off, bn_off=bn_off)], K_rows, N_cols,
                          [dict(shape=(K_rows, N_cols), dtype=BF16)], _first, tm=tm, tn=tn)

    for i in reversed(range(L)):
        sv = saved[i]
        dpre, dpe = _ew(f"ple_gate_bwd_{i}",
                        lambda d, g, e: [d * e.astype(F32) * g.astype(F32) * (1.0 - g.astype(F32)), d * g.astype(F32)],
                        [dx, sv["gp"], sv["pe"]], [BF16, BF16], S, D)
        (dh3,) = _mm(f"d_h3_{i}", [dict(a=dpre, b=wf["w_ple_gate"], bl=i, mode="nt", K=D)], S, D,
                     [dict(shape=(S, D), dtype=F32)], _first, tm=tmb, tn=tn_d)
        dw("w_ple_gate", i, sv["h3"], 0, dpre, 0, D, D, tn_d, tn_d)
        dw("w_ple", i, pb[i], 0, dpe, 0, ple, D, _pick(ple, (256, 128)), _pick(D, (2048, 1024, 512, 256)))
        dx, dxb, gs["norm_ple"][i] = _rmsnorm_bwd(f"norm_ple_bwd_{i}", sv["x2"], _row(small["norm_ple"][i]), dh3, dx)
        def swiglu_bwd(accs, tiles, rows):
            da = accs[0].astype(BF16).astype(F32)
            fg, fu = tiles[0].astype(F32), tiles[1].astype(F32)
            sg_ = _sigmoid(fg)
            return [da * fu * (sg_ * (1.0 + fg * (1.0 - sg_))), da * (fg * sg_)]

        dffg, dffu = _mm(f"d_act_{i}", [dict(a=dxb, b=wf["w_ff_down"], bl=i, mode="nt", K=D)], S, ff,
                         [dict(shape=(S, ff), dtype=BF16)] * 2, swiglu_bwd, tiles=[dict(x=sv["ffg"]), dict(x=sv["ffu"])],
                         tm=tmb, tn=tn_ff)
        dw("w_ff_down", i, sv["act"], 0, dxb, 0, ff, D, tn_ff, _pick(D, (2048, 1024, 512, 256)))
        dw("w_ff_gate", i, sv["h2"], 0, dffg, 0, D, ff, _pick(D, (2048, 1024, 512, 256)), tn_ff)
        dw("w_ff_up", i, sv["h2"], 0, dffu, 0, D, ff, _pick(D, (2048, 1024, 512, 256)), tn_ff)
        (dffg, dffu), _ = lax.optimization_barrier(((dffg, dffu), after_group(i, "ffn", {n: gw[n][i] for n in GRAD_GROUPS["ffn"]})))
        (dh2,) = _mm(f"d_h2_{i}", [dict(a=dffg, b=wf["w_ff_gate"], bl=i, mode="nt", K=ff),
                                   dict(a=dffu, b=wf["w_ff_up"], bl=i, mode="nt", K=ff)], S, D,
                     [dict(shape=(S, D), dtype=F32)], lambda a, t, r: [a[0] + a[1]], tm=tmb, tn=tn_d)
        dx, dxb, gs["norm_ffn"][i] = _rmsnorm_bwd(f"norm_ffn_bwd_{i}", sv["x1"], _row(small["norm_ffn"][i]), dh2, dx)
        (dmerged,) = _mm(f"d_merged_{i}", [dict(a=dxb, b=wf["w_out"], bl=i, mode="nt", K=D)], S, D,
                         [dict(shape=(S, D), dtype=BF16)], _first, tm=tmb, tn=tn_d)
        dw("w_out", i, sv["merged"], 0, dxb, 0, D, D, tn_d, tn_d)
        dz, dy = _gate_bwd(f"gate_bwd_{i}", sv["z"], dmerged, sv["y_attn"], sv["y_sg"], o_g0, in_w)
        (dattn,) = _mm(f"d_attn_{i}", [dict(a=dy, b=wf["w_br_attn"], bl=i, mode="nt", K=D)], S, ATTN_W,
                       [dict(shape=(S, ATTN_W), dtype=BF16)], _first, tm=tmb, tn=ATTN_W)
        (dsg,) = _mm(f"d_sg_{i}", [dict(a=dy, a_off=D, b=wf["w_br_sg"], bl=i, mode="nt", K=D)], S, SG_W,
                     [dict(shape=(S, SG_W), dtype=BF16)], _first, tm=tmb, tn=SG_W)
        dw("w_br_attn", i, sv["attn"], 0, dy, 0, ATTN_W, D, ATTN_W, _pick(D, (2048, 1024, 512, 256)))
        dw("w_br_sg", i, sv["sg"], 0, dy, D, SG_W, D, SG_W, _pick(D, (1024, 512, 256)))
        sgwt = jnp.swapaxes(small["sg_w"][i], 1, 2).astype(BF16)
        dzuv, gs["sg_w"][i], dsgb, dlg, dlb = _sg_bwd(f"sgu_bwd_{i}", sv["z"], dsg, sv["sgw"], sgwt, sv["sgbc"],
                                                      _row(small["sg_ln_g"][i]), _row(small["sg_ln_b"][i]), o_sg0)
        gs["sg_b"][i], gs["sg_ln_g"][i], gs["sg_ln_b"][i] = dsgb.reshape(SG_GROUPS, SG_CHUNK), dlg[0], dlb[0]
        dq, dk, dv = _attn_bwd(f"attn_bwd_{i}", sv["qkv"], sv["attn"], dattn, sv["lse_c"])
        dz = _rope_bwd(f"rope_bwd_{i}", dq, dk, dv, dzuv, cosf, sinf, dz)
        dw("w_in", i, sv["h"], 0, dz, 0, D, in_w, tn_d, tn_in)
        dz, _ = lax.optimization_barrier((dz, after_group(i, "mix", {n: gw[n][i] for n in GRAD_GROUPS["mix"]})))
        (dh,) = _mm(f"d_h_{i}", [dict(a=dz, b=wf["w_in"], bl=i, mode="nt", K=in_w)], S, D,
                    [dict(shape=(S, D), dtype=F32)], _first, tm=tmb, tn=tn_d)
        dx, dxb, gs["norm_mix"][i] = _rmsnorm_bwd(f"norm_mix_bwd_{i}", sv["x0"], _row(small["norm_mix"][i]), dh, dx)

    gsmall ={n: jnp.stack([jnp.reshape(v, small[n].shape[1:]) for v in gs[n]]) for n in gs}
    gsmall["norm_final"] = dg_final[0]
    return loss_cell, dx, gsmall


def _place():
    x, y, c = lax.axis_index("x"), lax.axis_index("y"), lax.axis_index("c")
    return x, y, c, 2 * x + y


def _chip_of(s):
    return s // 2, s % 2


def _aligned(v, m):
    return v if isinstance(v, int) else pl.multiple_of(v, m)


def _piece(name, shape, s, c):
    K, N = shape
    if name in ROW_SHARDED or name == SMALL_BLOCKS:
        ks = K // 4
        return s * ks + c * (ks // 2), ks // 2, 0, N
    ns = N // 4
    return c * (K // 2), K // 2, s * ns, ns


def _handshake(peers):
    barrier = pltpu.get_barrier_semaphore()
    for peer in peers:
        pl.semaphore_signal(barrier, inc=1, device_id=peer, device_id_type=MESH)
    pl.semaphore_wait(barrier, len(peers))


def _gather_body(names, shapes, src, dst, send_sems, recv_sems, local_sems):
    n_w = len(names)
    x, y, c, s = _place()
    sib = (x, y, 1 - c)
    rel = [1, 2, 3]

    def where(w, ps, pc):
        r0, nr, c0, nc = _piece(names[w], shapes[names[w]], ps, pc)
        return dst[w].at[pl.ds(_aligned(r0, 16), nr), pl.ds(_aligned(c0, LANES), nc)]

    def copy(w, k, ps, pc, to, from_src=False):
        return pltpu.make_async_remote_copy(
            src_ref=src[w] if from_src else where(w, ps, pc), dst_ref=where(w, ps, pc),
            send_sem=send_sems.at[w, k], recv_sem=recv_sems.at[w, k], device_id=to, device_id_type=MESH)

    mine, first, passed = [], [], []
    for w in range(n_w):
        cp = pltpu.make_async_copy(src[w], where(w, s, c), local_sems.at[w])
        cp.start()
        mine.append(cp)
        first.append(copy(w, 0, s, c, sib, from_src=True))
        for j in rel:
            first.append(copy(w, j, s, c, (*_chip_of(s ^ j), c), from_src=True))
    for cp in first:
        cp.start()
    for w in range(n_w):
        for j in rel:
            copy(w, j, s ^ j, c, sib).wait_recv()
            fw = copy(w, 3 + j, s ^ j, c, sib)
            fw.start()
            passed.append(fw)
    for w in range(n_w):
        copy(w, 0, s, 1 - c, sib).wait_recv()
        for j in rel:
            copy(w, 3 + j, s ^ j, 1 - c, sib).wait_recv()
    for cp in first + passed:
        cp.wait_send()
    for cp in mine:
        cp.wait()


def _gather_sems(n_w):
    return (pltpu.SemaphoreType.DMA((n_w, 7)), pltpu.SemaphoreType.DMA((n_w, 7)), pltpu.SemaphoreType.DMA((n_w,)))


def _gather_peers():
    x, y, c, s = _place()
    return [(x, y, 1 - c)] + [(*_chip_of(s ^ j), c) for j in (1, 2, 3)]


def _gather_weights(name, pieces, shapes):
    names = list(pieces)
    n_w = len(names)

    def body(*refs):
        _gather_body(names, shapes, refs[:n_w], refs[n_w:2 * n_w], *refs[2 * n_w:])

    anyspec = pl.BlockSpec(memory_space=pl.ANY)
    out = pl.pallas_call(
        body, name=name, in_specs=[anyspec] * n_w, out_specs=[anyspec] * n_w,
        out_shape=[jax.ShapeDtypeStruct(tuple(shapes[n]), pieces[n].dtype) for n in names], scratch_shapes=list(_gather_sems(n_w)),
    )(*[pieces[n] for n in names])
    return dict(zip(names, out))


def _gather_weights_async(name, pieces, shapes):
    names = list(pieces)
    n_w = len(names)
    src = [jax.new_ref(pieces[n], memory_space=pltpu.MemorySpace.HBM) for n in names]
    dst = [jax.empty_ref(jax.ShapeDtypeStruct(tuple(shapes[n]), pieces[n].dtype), memory_space=pltpu.MemorySpace.HBM)
           for n in names]

    @pl.kernel(mesh=plsc.ScalarSubcoreMesh(axis_name="seq", num_cores=1), name=name, scratch_types=_gather_sems(n_w),
               compiler_params=pltpu.CompilerParams(collective_id=GATHER_COLLECTIVE_ID))
    def launch(send_sems, recv_sems, local_sems):
        _handshake(_gather_peers())
        _gather_body(names, shapes, src, dst, send_sems, recv_sems, local_sems)

    launch()
    return {n: d[...] for n, d in zip(names, dst)}


def _halves_view(name, g):
    L, K, N = g.shape
    if name in ROW_SHARDED:
        return g.reshape(L * 4, 2, K // 8, N)
    return g.reshape(L, 2, K // 2, N)


def _exchange_halves(name, views):
    names = list(views)
    n_w = len(names)

    def body(*refs):
        src = refs[:n_w]
        got = refs[n_w:2 * n_w]
        send_sems, recv_sems = refs[2 * n_w:]
        x, y, c, s = _place()
        remote = [pltpu.make_async_remote_copy(src_ref=src[w].at[:, 1 - c], dst_ref=got[w], send_sem=send_sems.at[w],
                                               recv_sem=recv_sems.at[w], device_id=(x, y, 1 - c), device_id_type=MESH)
                  for w in range(n_w)]
        for cp in remote:
            cp.start()
        for cp in remote:
            cp.wait()

    anyspec = pl.BlockSpec(memory_space=pl.ANY)
    out = pl.pallas_call(
        body, name=name, in_specs=[anyspec] * n_w, out_specs=[anyspec] * n_w,
        out_shape=[jax.ShapeDtypeStruct((v.shape[0],) + v.shape[2:], BF16) for v in views.values()],
        scratch_shapes=[pltpu.SemaphoreType.DMA((n_w,)), pltpu.SemaphoreType.DMA((n_w,))],
    )(*views.values())
    return dict(zip(names, out))


def _chip_sum(name, view, got, place):
    A, _, R, C = view.shape
    tc = _pick(C, (2048, 1536, 1408, 1024, 512, 256, 128))
    tr = _pick(R, [t for t in (1024, 512, 256, 128, 64, 32, 16) if t * tc <= 4 * EW_TILE_ELEMS] + [8])

    def body(p_ref, own_ref, got_ref, o_ref):
        o_ref[...] = (own_ref[...].astype(F32) + got_ref[...].astype(F32)).astype(BF16)

    flat = pl.BlockSpec((None, tr, tc), lambda a, i, j, p: (a, i, j))
    return pl.pallas_call(
        body, name=name, out_shape=jax.ShapeDtypeStruct((A, R, C), BF16),
        grid_spec=pltpu.PrefetchScalarGridSpec(
            num_scalar_prefetch=1, grid=(A, R // tr, C // tc),
            in_specs=[pl.BlockSpec((None, None, tr, tc), lambda a, i, j, p: (a, p[0], i, j)), flat], out_specs=flat),
        compiler_params=pltpu.CompilerParams(dimension_semantics=("parallel", "parallel", "parallel"),
                                             vmem_limit_bytes=_vmem_limit(6 * tr * tc, 3 * tr * tc * 4)),
    )(place, view, got)


def _shard_view(name, ps, L):
    return ps.reshape(L, 4, *ps.shape[1:]) if name in ROW_SHARDED else ps


def _scatter_body(names, src, dst, send_sems, recv_sems):
    x, y, c, s = _place()

    def shard(w, t):
        if names[w] in ROW_SHARDED:
            return src[w].at[:, t]
        ns = src[w].shape[2] // 4
        return src[w].at[:, :, pl.ds(pl.multiple_of(t * ns, LANES), ns)]

    remote = []
    for w in range(len(names)):
        for j in (1, 2, 3):
            remote.append(pltpu.make_async_remote_copy(
                src_ref=shard(w, s ^ j), dst_ref=dst[w].at[j - 1], send_sem=send_sems.at[w, j - 1],
                recv_sem=recv_sems.at[w, j - 1], device_id=(*_chip_of(s ^ j), c), device_id_type=MESH))
    for cp in remote:
        cp.start()
    for cp in remote:
        cp.wait()


def _scatter_out_shape(name, v):
    return (3, v[0], v[2], v[3]) if name in ROW_SHARDED else (3, v[0], v[1], v[2] // 4)


def _scatter_sems(n_w):
    return (pltpu.SemaphoreType.DMA((n_w, 3)), pltpu.SemaphoreType.DMA((n_w, 3)))


def _scatter_chip_sums_async(name, psum):
    names = list(psum)
    n_w = len(names)
    src = [jax.new_ref(psum[n], memory_space=pltpu.MemorySpace.HBM) for n in names]
    dst = [jax.empty_ref(jax.ShapeDtypeStruct(_scatter_out_shape(n, psum[n].shape), BF16), memory_space=pltpu.MemorySpace.HBM)
           for n in names]

    @pl.kernel(mesh=plsc.ScalarSubcoreMesh(axis_name="seq", num_cores=1), name=name, scratch_types=_scatter_sems(n_w),
               compiler_params=pltpu.CompilerParams(collective_id=SCATTER_COLLECTIVE_ID))
    def launch(send_sems, recv_sems):
        _handshake(_gather_peers()[1:])
        _scatter_body(names, src, dst, send_sems, recv_sems)

    launch()
    return {n: d[...] for n, d in zip(names, dst)}


def _shard_sum(name, ps, parts, place, row_sharded, layer, n_layers, into):
    _, _, R, C = parts.shape
    tc = _pick(C, (2048, 1408, 1024, 896, 512, 384, 256, 128))
    tr = _pick(R, [t for t in (1024, 512, 256, 128, 64, 32, 16) if t * tc <= 2 * EW_TILE_ELEMS] + [8])

    def body(p_ref, own_ref, a_ref, b_ref, c_ref, *rest):
        o_ref = rest[-1]
        o_ref[...] = ((own_ref[...].astype(F32) + a_ref[...].astype(F32)) + b_ref[...].astype(F32)) + c_ref[...].astype(F32)

    if row_sharded:
        own_spec = pl.BlockSpec((None, None, tr, tc), lambda i, j, p: (0, p[1], i, j))
    else:
        own_spec = pl.BlockSpec((None, tr, tc), lambda i, j, p: (0, i, p[1] * (C // tc) + j))
    part = lambda k: pl.BlockSpec((None, None, tr, tc), lambda i, j, p, k=k: (k, 0, i, j))
    in_specs, args, aliases = [own_spec, part(0), part(1), part(2)], [place, ps, parts, parts, parts], {}
    if into is not None:
        in_specs.append(pl.BlockSpec(memory_space=pl.ANY))
        args.append(into)
        aliases = {5: 0}
    return pl.pallas_call(
        body, name=name, out_shape=jax.ShapeDtypeStruct((n_layers, 2, R, C), F32),
        grid_spec=pltpu.PrefetchScalarGridSpec(
            num_scalar_prefetch=1, grid=(R // tr, C // tc), in_specs=in_specs,
            out_specs=pl.BlockSpec((None, None, tr, tc), lambda i, j, p: (layer, p[0], i, j))),
        input_output_aliases=aliases,
        compiler_params=pltpu.CompilerParams(dimension_semantics=("parallel", "parallel"),
                                             vmem_limit_bytes=_vmem_limit(12 * tr * tc, 5 * tr * tc * 4)),
    )(*args)


def _share_halves(ghalf):
    names = list(ghalf)
    n_w = len(names)

    def body(*refs):
        src = refs[:n_w]
        dst = refs[n_w:2 * n_w]
        send_sems, recv_sems = refs[2 * n_w:]
        x, y, c, s = _place()
        remote = [pltpu.make_async_remote_copy(src_ref=src[w].at[:, c], dst_ref=dst[w].at[:, c], send_sem=send_sems.at[w],
                                               recv_sem=recv_sems.at[w], device_id=(x, y, 1 - c), device_id_type=MESH)
                  for w in range(n_w)]
        for cp in remote:
            cp.start()
        for cp in remote:
            cp.wait()

    anyspec = pl.BlockSpec(memory_space=pl.ANY)
    out = pl.pallas_call(
        body, name="share_halves", in_specs=[anyspec] * n_w, out_specs=[anyspec] * n_w,
        out_shape=[jax.ShapeDtypeStruct(ghalf[n].shape, F32) for n in names],
        input_output_aliases={w: w for w in range(n_w)},
        scratch_shapes=[pltpu.SemaphoreType.DMA((n_w,)), pltpu.SemaphoreType.DMA((n_w,))],
    )(*[ghalf[n] for n in names])
    return dict(zip(names, out))


def _adamw_math(w, g, m, v):
    m = ADAM_B1 * m + (1.0 - ADAM_B1) * g
    v = ADAM_B2 * v + (1.0 - ADAM_B2) * (g * g)
    m_hat = m / (1.0 - ADAM_B1 ** ADAM_STEP)
    v_hat = v / (1.0 - ADAM_B2 ** ADAM_STEP)
    delta = -ADAM_LR * (m_hat / (jnp.sqrt(v_hat) + ADAM_EPS) + ADAM_WD * w)
    return delta, m, v


def _adamw(name, w, g, m, v):
    shape = w.shape
    C = shape[-1]
    R = math.prod(shape[:-1])
    f = lambda a: a.reshape(R, C)
    delta, nm, nv = _ew(name, lambda w_, g_, m_, v_: list(_adamw_math(w_, g_, m_, v_)), [f(w), f(g), f(m), f(v)], [F32] * 3, R, C)
    return delta.reshape(shape), nm.reshape(shape), nv.reshape(shape)


def _pack_small(d):
    return jnp.concatenate([d[n].reshape(-1, LANES) for n in SMALL], axis=0)


def _unpack_small(flat, like):
    out, r = {}, 0
    for n in SMALL:
        k = like[n].size // LANES
        out[n] = flat[r:r + k].reshape(like[n].shape)
        r += k
    return out


def _small_update(gall, w, m, v):
    M = w.shape[0]
    tr = _pick(M, (552, 276, 184, 96, 48, 24, 8))

    def body(*refs):
        g = refs[0][...]
        for d in range(1, 8):
            g = g + refs[d][...]
        delta, nm, nv = _adamw_math(refs[8][...], g, refs[9][...], refs[10][...])
        refs[11][...] = g
        refs[12][...] = delta
        refs[13][...] = nm
        refs[14][...] = nv

    blk = pl.BlockSpec((tr, LANES), lambda i: (i, 0))
    in_specs = [pl.BlockSpec((tr, LANES), lambda i, d=d: (d * (M // tr) + i, 0)) for d in range(8)] + [blk] * 3
    return pl.pallas_call(
        body, name="small_update", grid=(M // tr,), in_specs=in_specs, out_specs=[blk] * 4,
        out_shape=[jax.ShapeDtypeStruct((M, LANES), F32)] * 4,
        compiler_params=pltpu.CompilerParams(dimension_semantics=("parallel",), vmem_limit_bytes=_vmem_limit(15 * tr * LANES * 4)),
    )(*([gall] * 8), w, m, v)


def _step(x, p, target, w, m, v):
    L = p.shape[0]
    x_i, y_i, c, s = _place()
    shapes = {}
    for n in BIG:
        _, K, N = w[n].shape
        shapes[n] = (4 * K, N) if n in ROW_SHARDED else (K, 4 * N)
    def pieces_of(i):
        return {n: lax.dynamic_slice_in_dim(w[n][i], c * (w[n].shape[1] // 2), w[n].shape[1] // 2, axis=0).astype(BF16)
                for n in BIG}

    first = pieces_of(0)
    head = _gather_weights("gather_weights_0_w_in", {"w_in": first.pop("w_in")}, shapes)
    head, first = lax.optimization_barrier((head, first))
    layers = [{**head, **_gather_weights_async("gather_weights_0", first, shapes)}]
    for i in range(1, L):
        mine = pieces_of(i)
        layers.append({**_gather_weights_async(f"gather_weights_{i}_mix", {n: mine[n] for n in GRAD_GROUPS["mix"]}, shapes),
                       **_gather_weights_async(f"gather_weights_{i}_ffn", {n: mine[n] for n in GRAD_GROUPS["ffn"]}, shapes)})
    wf ={n: [layers[i][n] for i in range(L)] for n in BIG}
    small = {n: w[n] for n in SMALL}
    place = jnp.stack([c, s]).astype(jnp.int32)
    reduced = []

    def after_group(i, group, grads):
        tag = f"{i}_{group}"
        views = {n: _halves_view(n, g[None]) for n, g in grads.items()}
        got = _exchange_halves(f"exchange_halves_{tag}", views)
        chip_sum = {n: _shard_view(n, _chip_sum(f"chip_sum_{n}_{i}", views[n], got[n], place), 1) for n in grads}
        reduced.append((i, chip_sum, _scatter_chip_sums_async(f"scatter_chip_sums_{tag}", chip_sum)))
        return chip_sum

    loss_cell, dx, gsmall = _local_step(x[0], p[:, 0], target[0], wf, small, after_group)
    loss = lax.psum(jnp.sum(loss_cell), ("x", "y", "c"))
    packed = _pack_small(gsmall)
    gall = _gather_weights_async("gather_small", {SMALL_BLOCKS: packed}, {SMALL_BLOCKS: (8 * packed.shape[0], LANES)})[SMALL_BLOCKS]
    ghalf = {n: None for n in BIG}
    done = None
    for i, chip_sum, parts in reduced:
        parts, _ = lax.optimization_barrier((parts, done))
        for n in chip_sum:
            ghalf[n] = _shard_sum(f"shard_sum_{n}_{i}", chip_sum[n], parts[n], place, n in ROW_SHARDED, i, L, ghalf[n])
        done = {n: ghalf[n] for n in chip_sum}
    gfull = _share_halves(ghalf)
    grad, delta, new_m, new_v = {}, {}, {}, {}
    for n in BIG:
        grad[n] = gfull[n].reshape(w[n].shape)
        delta[n], new_m[n], new_v[n] = _adamw(f"adamw_{n}", w[n], grad[n], m[n], v[n])
    gall, _ = lax.optimization_barrier((gall, (done, delta)))
    gsum, dsm, nms, nvs =_small_update(gall, _pack_small(small), _pack_small({n: m[n] for n in SMALL}),
                                        _pack_small({n: v[n] for n in SMALL}))
    for dst, flat in ((grad, gsum), (delta, dsm), (new_m, nms), (new_v, nvs)):
        dst.update(_unpack_small(flat, small))
    return loss, dx[None], grad, delta, new_m, new_v


def kernel(x, p, w_in, w_br_attn, w_br_sg, w_out, sg_w, sg_b, sg_ln_g, sg_ln_b, norm_mix, norm_ffn, norm_ple, norm_final, w_ff_gate, w_ff_up, w_ff_down, w_ple_gate, w_ple, loss_target, m_w_in, m_w_br_attn, m_w_br_sg, m_w_out, m_sg_w, m_sg_b, m_sg_ln_g, m_sg_ln_b, m_norm_mix, m_norm_ffn, m_norm_ple, m_norm_final, m_w_ff_gate, m_w_ff_up, m_w_ff_down, m_w_ple_gate, m_w_ple, v_w_in, v_w_br_attn, v_w_br_sg, v_w_out, v_sg_w, v_sg_b, v_sg_ln_g, v_sg_ln_b, v_norm_mix, v_norm_ffn, v_norm_ple, v_norm_final, v_w_ff_gate, v_w_ff_up, v_w_ff_down, v_w_ple_gate, v_w_ple):
    w = dict(w_in=w_in, w_br_attn=w_br_attn, w_br_sg=w_br_sg, w_out=w_out, sg_w=sg_w, sg_b=sg_b, sg_ln_g=sg_ln_g, sg_ln_b=sg_ln_b,
             norm_mix=norm_mix, norm_ffn=norm_ffn, norm_ple=norm_ple, norm_final=norm_final, w_ff_gate=w_ff_gate, w_ff_up=w_ff_up,
             w_ff_down=w_ff_down, w_ple_gate=w_ple_gate, w_ple=w_ple)
    m = dict(w_in=m_w_in, w_br_attn=m_w_br_attn, w_br_sg=m_w_br_sg, w_out=m_w_out, sg_w=m_sg_w, sg_b=m_sg_b, sg_ln_g=m_sg_ln_g,
             sg_ln_b=m_sg_ln_b, norm_mix=m_norm_mix, norm_ffn=m_norm_ffn, norm_ple=m_norm_ple, norm_final=m_norm_final,
             w_ff_gate=m_w_ff_gate, w_ff_up=m_w_ff_up, w_ff_down=m_w_ff_down, w_ple_gate=m_w_ple_gate, w_ple=m_w_ple)
    v = dict(w_in=v_w_in, w_br_attn=v_w_br_attn, w_br_sg=v_w_br_sg, w_out=v_w_out, sg_w=v_sg_w, sg_b=v_sg_b, sg_ln_g=v_sg_ln_g,
             sg_ln_b=v_sg_ln_b, norm_mix=v_norm_mix, norm_ffn=v_norm_ffn, norm_ple=v_norm_ple, norm_final=v_norm_final,
             w_ff_gate=v_w_ff_gate, w_ff_up=v_w_ff_up, w_ff_down=v_w_ff_down, w_ple_gate=v_w_ple_gate, w_ple=v_w_ple)
    loss, grad_x, grad, delta, new_m, new_v = _step(x, p, loss_target, w, m, v)
    return (loss, grad_x, *[grad[n] for n in WEIGHTS], *[delta[n] for n in WEIGHTS], *[new_m[n] for n in WEIGHTS],
            *[new_v[n] for n in WEIGHTS])
```

```python
import functools
import math

import jax
import jax.numpy as jnp
from jax import lax
from jax.experimental import pallas as pl
from jax.experimental.pallas import tpu as pltpu
from jax.experimental.pallas import tpu_sc as plsc

F32 = jnp.float32
BF16 = jnp.bfloat16
MESH = pl.DeviceIdType.MESH

HEAD_DIM = 128
ATTN_GROUPS = ((128, 1), (512, 4), (2048, 16))
N_GROUPS = 3
HEADS = 4
QKV_W = 3 * N_GROUPS * HEADS * HEAD_DIM
ATTN_W = HEADS * HEAD_DIM
SG_CHUNK = 128
SG_GROUPS = 8
SG_W = 1024
RADIUS = 64
ROPE_THETA = 10000.0
NORM_EPS = 1e-6
NEG_INF = -1e30
ADAM_LR, ADAM_B1, ADAM_B2, ADAM_EPS, ADAM_WD, ADAM_STEP = 0.001, 0.9, 0.999, 1e-08, 0.01, 10

VMEM_CAP_V7X = 56 * 1024 * 1024
LANES = 128
EW_TILE_ELEMS = 256 * 1024
MM_VMEM_BUDGET = 44 * 1024 * 1024

GATHER_COLLECTIVE_ID = 1
SCATTER_COLLECTIVE_ID = 2
PARTIALS_COLLECTIVE_ID = 3

BIG = ("w_in", "w_br_attn", "w_br_sg", "w_out", "w_ff_gate", "w_ff_up", "w_ff_down", "w_ple_gate", "w_ple")
ROW_SHARDED = ("w_out", "w_ff_down", "w_ple_gate")
SMALL_BLOCKS = "small_blocks"
GRAD_GROUPS = {"ffn": ("w_ple_gate", "w_ple", "w_ff_down", "w_ff_gate", "w_ff_up"), "mix": ("w_out", "w_br_attn", "w_br_sg", "w_in")}
SMALL = ("sg_w", "sg_b", "sg_ln_g", "sg_ln_b", "norm_mix", "norm_ffn", "norm_ple", "norm_final")
WEIGHTS = ("w_in", "w_br_attn", "w_br_sg", "w_out", "sg_w", "sg_b", "sg_ln_g", "sg_ln_b", "norm_mix", "norm_ffn",
           "norm_ple", "norm_final", "w_ff_gate", "w_ff_up", "w_ff_down", "w_ple_gate", "w_ple")


def _pick(n, prefs):
    for t in prefs:
        if n % t == 0:
            return t
    return n


def _nbytes(shape, dtype):
    return math.prod(shape) * jnp.dtype(dtype).itemsize


def _vmem_limit(block_bytes, temp_bytes=0):
    est = 2 * block_bytes + temp_bytes
    assert est <= VMEM_CAP_V7X, est
    return VMEM_CAP_V7X


def _sigmoid(x):
    return 1.0 / (1.0 + jnp.exp(-x))


_GELU_C = math.sqrt(2.0 / math.pi)


def _gelu(x):
    return 0.5 * x * (1.0 + jnp.tanh(_GELU_C * (x + 0.044715 * (x * x * x))))


def _gelu_grad(x):
    t = jnp.tanh(_GELU_C * (x + 0.044715 * (x * x * x)))
    return 0.5 * (1.0 + t) + 0.5 * x * (1.0 - t * t) * (_GELU_C * (1.0 + 3.0 * 0.044715 * (x * x)))


def _lead(arr, l, blk, idx):
    if arr.ndim == 2:
        return pl.BlockSpec(blk, idx)
    return pl.BlockSpec((None,) + blk, lambda *g: (l,) + idx(*g))


def _k_steps(prods, tm, tn, fixed_bytes):
    for nk in range(1, 129):
        if any(p["K"] % nk or (p["K"] // nk) % LANES for p in prods):
            continue
        if 2 * sum((tm + tn) * (p["K"] // nk) * 2 for p in prods) + fixed_bytes <= MM_VMEM_BUDGET:
            return nk
    raise ValueError("no contraction split fits VMEM")


def _mm(name, prods, M, N, outs, epilogue, tiles=(), rows=(), tm=1024, tn=1024):
    assert M % tm == 0 and N % tn == 0, (name, M, N, tm, tn)
    fixed = 2 * tm * tn * (sum(t["x"].dtype.itemsize for t in tiles) + sum(jnp.dtype(o["dtype"]).itemsize for o in outs))
    fixed += (len(prods) + 2) * tm * tn * 4
    nk = _k_steps(prods, tm, tn, fixed)
    in_specs, args, block_bytes = [], [], 0
    for p in prods:
        if isinstance(p["b"], (list, tuple)):
            p["b"], p["bl"] = p["b"][p["bl"]], None
        K = p["K"]
        assert K % nk == 0, (name, K, nk)
        tk = K // nk
        p["tk"] = tk
        a_off, bk_off, bn_off = p.get("a_off", 0), p.get("bk_off", 0), p.get("bn_off", 0)
        assert bn_off % tn == 0 and bk_off % tk == 0
        if p["mode"] == "nn":
            assert a_off % tk == 0
            a_spec = _lead(p["a"], p.get("al"), (tm, tk), lambda i, j, k, o=a_off // tk: (i, o + k))
            b_spec = _lead(p["b"], p.get("bl"), (tk, tn), lambda i, j, k, ok=bk_off // tk, on=bn_off // tn: (ok + k, on + j))
        elif p["mode"] == "nt":
            assert a_off % tk == 0
            a_spec = _lead(p["a"], p.get("al"), (tm, tk), lambda i, j, k, o=a_off // tk: (i, o + k))
            b_spec = _lead(p["b"], p.get("bl"), (tn, tk), lambda i, j, k, ok=bk_off // tk, on=bn_off // tn: (on + j, ok + k))
        else:
            assert a_off % tm == 0
            a_spec = _lead(p["a"], p.get("al"), (tk, tm), lambda i, j, k, o=a_off // tm: (k, o + i))
            b_spec = _lead(p["b"], p.get("bl"), (tk, tn), lambda i, j, k, on=bn_off // tn: (k, on + j))
        in_specs += [a_spec, b_spec]
        args += [p["a"], p["b"]]
        block_bytes += (tm + tn) * tk * 2
    for t in tiles:
        off = t.get("off", 0)
        assert off % tn == 0
        in_specs.append(_lead(t["x"], t.get("l"), (tm, tn), lambda i, j, k, o=off // tn: (i, o + j)))
        args.append(t["x"])
        block_bytes += tm * tn * t["x"].dtype.itemsize
    for r in rows:
        in_specs.append(pl.BlockSpec((1, tn), lambda i, j, k: (0, j)))
        args.append(r)
    out_shapes, out_specs, aliases = [], [], {}
    for o_i, o in enumerate(outs):
        off = o.get("col_off", 0)
        assert off % tn == 0
        out_shapes.append(jax.ShapeDtypeStruct(o["shape"], o["dtype"]))
        idx = lambda i, j, k, oo=off // tn: (i, oo + j)
        if len(o["shape"]) == 2:
            out_specs.append(pl.BlockSpec((tm, tn), idx))
        else:
            out_specs.append(pl.BlockSpec((None, tm, tn), lambda i, j, k, l=o["l"], f=idx: (l,) + f(i, j, k)))
        if o.get("alias") is not None:
            aliases[len(args)] = o_i
            in_specs.append(pl.BlockSpec(memory_space=pl.ANY))
            args.append(o["alias"])
        block_bytes += tm * tn * jnp.dtype(o["dtype"]).itemsize
    n_p, n_t, n_r, n_o = len(prods), len(tiles), len(rows), len(outs)
    n_alias = len(aliases)
    modes = [p["mode"] for p in prods]

    def body(*refs):
        ab = refs[: 2 * n_p]
        t_refs = refs[2 * n_p: 2 * n_p + n_t]
        r_refs = refs[2 * n_p + n_t: 2 * n_p + n_t + n_r]
        o_refs = refs[2 * n_p + n_t + n_r + n_alias: 2 * n_p + n_t + n_r + n_alias + n_o]
        acc_refs = refs[2 * n_p + n_t + n_r + n_alias + n_o:]
        dims = {"nn": (((1,), (0,)), ((), ())), "nt": (((1,), (1,)), ((), ())), "tn": (((0,), (0,)), ((), ()))}

        def part(q):
            return lax.dot_general(ab[2 * q][...], ab[2 * q + 1][...], dims[modes[q]], preferred_element_type=F32)

        def finish(accs):
            res = epilogue(accs, [t[...] for t in t_refs], [r[...] for r in r_refs])
            for o_ref, val in zip(o_refs, res, strict=True):
                o_ref[...] = val.astype(o_ref.dtype)

        if nk == 1:
            finish([part(q) for q in range(n_p)])
        else:
            k = pl.program_id(2)

            @pl.when(k == 0)
            def _():
                for q, acc in enumerate(acc_refs):
                    acc[...] = part(q)

            @pl.when(k > 0)
            def _():
                for q, acc in enumerate(acc_refs):
                    acc[...] += part(q)

            @pl.when(k == nk - 1)
            def _():
                finish([acc[...] for acc in acc_refs])

    scratch = [pltpu.VMEM((tm, tn), F32) for _ in prods] if nk > 1 else []
    temp = (n_p + 2) * tm * tn * 4
    res = pl.pallas_call(
        body, name=name, grid=(M // tm, N // tn, nk), in_specs=in_specs, out_specs=out_specs, out_shape=out_shapes,
        scratch_shapes=scratch, input_output_aliases=aliases,
        compiler_params=pltpu.CompilerParams(dimension_semantics=("parallel", "parallel", "arbitrary"),
                                             vmem_limit_bytes=_vmem_limit(block_bytes, temp)),
    )(*args)
    return res


def _first(accs, tiles, rows):
    return [accs[0]]


def _ew(name, fn, ins, outs, R, C, tr=None, tc=None):
    tc = tc or _pick(C, (2048, 1536, 1408, 1024, 896, 512, 384, 256, 128))
    tr = tr or _pick(R, [t for t in (512, 256, 128, 64, 32, 16) if t * tc <= EW_TILE_ELEMS] + [8])
    in_specs, args, bb = [], [], 0
    for x in ins:
        if isinstance(x, tuple):
            arr, l = x
            in_specs.append(pl.BlockSpec((None, tr, tc), lambda i, j, l=l: (l, i, j)))
        else:
            arr = x
            in_specs.append(pl.BlockSpec((tr, tc), lambda i, j: (i, j)))
        args.append(arr)
        bb += tr * tc * arr.dtype.itemsize
    out_shapes = [jax.ShapeDtypeStruct((R, C), d) for d in outs]
    out_specs = [pl.BlockSpec((tr, tc), lambda i, j: (i, j)) for _ in outs]
    bb += sum(tr * tc * jnp.dtype(d).itemsize for d in outs)
    n_in = len(ins)

    def body(*refs):
        res = fn(*[r[...] for r in refs[:n_in]])
        for o_ref, val in zip(refs[n_in:], res, strict=True):
            o_ref[...] = val.astype(o_ref.dtype)

    return pl.pallas_call(
        body, name=name, grid=(R // tr, C // tc), in_specs=in_specs, out_specs=out_specs, out_shape=out_shapes,
        compiler_params=pltpu.CompilerParams(dimension_semantics=("parallel", "parallel"),
                                             vmem_limit_bytes=_vmem_limit(bb, 6 * tr * tc * 4)),
    )(*args)


def _rmsnorm_fwd(name, x, g):
    S, D = x.shape
    tr = _pick(S, (256, 128, 64, 8))

    def body(x_ref, g_ref, h_ref):
        xv = x_ref[...]
        r = lax.rsqrt(jnp.mean(xv * xv, axis=-1, keepdims=True) + NORM_EPS)
        h_ref[...] = (xv * r * g_ref[...]).astype(BF16)

    return pl.pallas_call(
        body, name=name, grid=(S // tr,),
        in_specs=[pl.BlockSpec((tr, D), lambda i: (i, 0)), pl.BlockSpec((1, D), lambda i: (0, 0))],
        out_specs=pl.BlockSpec((tr, D), lambda i: (i, 0)), out_shape=jax.ShapeDtypeStruct((S, D), BF16),
        compiler_params=pltpu.CompilerParams(dimension_semantics=("parallel",),
                                             vmem_limit_bytes=_vmem_limit(tr * D * 6, 3 * tr * D * 4)),
    )(x, g)


def _rmsnorm_bwd(name, x, g, dh, dres):
    S, D = x.shape
    tr = _pick(S, (256, 128, 64, 8))

    def body(x_ref, g_ref, dh_ref, dres_ref, dx_ref, dxb_ref, dg_ref):
        xv = x_ref[...]
        dy = dh_ref[...].astype(F32)
        r = lax.rsqrt(jnp.mean(xv * xv, axis=-1, keepdims=True) + NORM_EPS)
        a = dy * g_ref[...]
        dx = dres_ref[...] + r * a - xv * (r * r * r) * jnp.mean(a * xv, axis=-1, keepdims=True)
        dx_ref[...] = dx
        dxb_ref[...] = dx.astype(BF16)
        part = jnp.sum(dy * xv * r, axis=0, keepdims=True)

        @pl.when(pl.program_id(0) == 0)
        def _():
            dg_ref[...] = part

        @pl.when(pl.program_id(0) > 0)
        def _():
            dg_ref[...] += part

    row = pl.BlockSpec((tr, D), lambda i: (i, 0))
    vec = pl.BlockSpec((1, D), lambda i: (0, 0))
    return pl.pallas_call(
        body, name=name, grid=(S // tr,), in_specs=[row, vec, row, row], out_specs=[row, row, vec],
        out_shape=[jax.ShapeDtypeStruct((S, D), F32), jax.ShapeDtypeStruct((S, D), BF16), jax.ShapeDtypeStruct((1, D), F32)],
        compiler_params=pltpu.CompilerParams(dimension_semantics=("arbitrary",),
                                             vmem_limit_bytes=_vmem_limit(tr * D * 18, 5 * tr * D * 4)),
    )(x, g, dh, dres)


def _loss_head(x, g, target):
    S, D = x.shape
    tr = _pick(S, (256, 128, 64, 8))

    def body(x_ref, g_ref, t_ref, loss_ref, dx_ref, dxb_ref, dg_ref):
        xv = x_ref[...]
        r = lax.rsqrt(jnp.mean(xv * xv, axis=-1, keepdims=True) + NORM_EPS)
        xn = xv * r
        diff = xn * g_ref[...] - t_ref[...]
        dy = diff * (1.0 / D)
        a = dy * g_ref[...]
        dx = r * a - xv * (r * r * r) * jnp.mean(a * xv, axis=-1, keepdims=True)
        dx_ref[...] = dx
        dxb_ref[...] = dx.astype(BF16)
        part = jnp.sum(dy * xn, axis=0, keepdims=True)
        cell = (lax.broadcasted_iota(jnp.int32, (8, LANES), 0) == 0) & (lax.broadcasted_iota(jnp.int32, (8, LANES), 1) == 0)
        lpart = jnp.where(cell, 0.5 * jnp.sum(jnp.mean(diff * diff, axis=-1, keepdims=True)), 0.0)

        @pl.when(pl.program_id(0) == 0)
        def _():
            dg_ref[...] = part
            loss_ref[...] = lpart

        @pl.when(pl.program_id(0) > 0)
        def _():
            dg_ref[...] += part
            loss_ref[...] += lpart

    row = pl.BlockSpec((tr, D), lambda i: (i, 0))
    vec = pl.BlockSpec((1, D), lambda i: (0, 0))
    return pl.pallas_call(
        body, name="loss_head", grid=(S // tr,), in_specs=[row, vec, row],
        out_specs=[pl.BlockSpec((8, LANES), lambda i: (0, 0)), row, row, vec],
        out_shape=[jax.ShapeDtypeStruct((8, LANES), F32), jax.ShapeDtypeStruct((S, D), F32),
                   jax.ShapeDtypeStruct((S, D), BF16), jax.ShapeDtypeStruct((1, D), F32)],
        compiler_params=pltpu.CompilerParams(dimension_semantics=("arbitrary",),
                                             vmem_limit_bytes=_vmem_limit(tr * D * 14, 6 * tr * D * 4)),
    )(x, g, target)


def _rope_tables(S):
    pos = jnp.arange(S, dtype=F32)
    inv_freq = ROPE_THETA ** (-jnp.arange(0, HEAD_DIM, 2, dtype=F32) / HEAD_DIM)
    ang = pos[:, None] * inv_freq[None, :]
    cos, sin = jnp.cos(ang), jnp.sin(ang)
    return jnp.concatenate([cos, cos], axis=-1), jnp.concatenate([-sin, sin], axis=-1)


def _rope_fwd(name, z, cosf, sinf):
    S = z.shape[0]
    tr = _pick(S, (256, 128, 64, 8))
    n_q = N_GROUPS * HEADS

    def body(z_ref, c_ref, s_ref, o_ref):
        c, s = c_ref[...], s_ref[...]
        for j in range(QKV_W // HEAD_DIM):
            t = z_ref[:, j * HEAD_DIM:(j + 1) * HEAD_DIM]
            if j < 2 * n_q:
                t = t * c + pltpu.roll(t, HEAD_DIM // 2, axis=1) * s
            if j < n_q:
                t = t * ATTN_SCALE
            o_ref[:, j * HEAD_DIM:(j + 1) * HEAD_DIM] = t.astype(BF16)

    tab = pl.BlockSpec((tr, HEAD_DIM), lambda i: (i, 0))
    return pl.pallas_call(
        body, name=name, grid=(S // tr,), in_specs=[pl.BlockSpec((tr, QKV_W), lambda i: (i, 0)), tab, tab],
        out_specs=pl.BlockSpec((tr, QKV_W), lambda i: (i, 0)), out_shape=jax.ShapeDtypeStruct((S, QKV_W), BF16),
        compiler_params=pltpu.CompilerParams(dimension_semantics=("parallel",),
                                             vmem_limit_bytes=_vmem_limit(tr * QKV_W * 6, tr * QKV_W * 4)),
    )(z, cosf, sinf)


def _rope_bwd(name, dq, dk, dv, dzuv, cosf, sinf, dz):
    S = dq.shape[0]
    tr = _pick(S, (256, 128, 64, 8))
    W3 = QKV_W // 3
    nh = W3 // HEAD_DIM
    wide = QKV_W + dzuv.shape[1]

    def body(dq_ref, dk_ref, dv_ref, uv_ref, c_ref, s_ref, dz_in, o_ref):
        c, s = c_ref[...], s_ref[...]
        for part, ref in enumerate((dq_ref, dk_ref)):
            for j in range(nh):
                t = ref[:, j * HEAD_DIM:(j + 1) * HEAD_DIM].astype(F32)
                t = t * c - pltpu.roll(t, HEAD_DIM // 2, axis=1) * s
                o_ref[:, part * W3 + j * HEAD_DIM: part * W3 + (j + 1) * HEAD_DIM] = t.astype(BF16)
        o_ref[:, 2 * W3:QKV_W] = dv_ref[...]
        o_ref[:, QKV_W:] = uv_ref[...]

    third = pl.BlockSpec((tr, W3), lambda i: (i, 0))
    tab = pl.BlockSpec((tr, HEAD_DIM), lambda i: (i, 0))
    return pl.pallas_call(
        body, name=name, grid=(S // tr,),
        in_specs=[third, third, third, pl.BlockSpec((tr, dzuv.shape[1]), lambda i: (i, 0)), tab, tab, pl.BlockSpec(memory_space=pl.ANY)],
        out_specs=pl.BlockSpec((tr, wide), lambda i: (i, 0)), out_shape=jax.ShapeDtypeStruct(dz.shape, dz.dtype),
        input_output_aliases={6: 0},
        compiler_params=pltpu.CompilerParams(dimension_semantics=("parallel",),
                                             vmem_limit_bytes=_vmem_limit(tr * wide * 4, tr * wide * 4)),
    )(dq, dk, dv, dzuv, cosf, sinf, dz)


ATTN_TQ = 256
ATTN_SCALE = HEAD_DIM ** -0.5
ATTN_PAD_MAX = RADIUS * max(d for _, d in ATTN_GROUPS)


def _band_bias(shape, q_axis, d):
    kq = lax.broadcasted_iota(jnp.int32, shape, 1 - q_axis) - lax.broadcasted_iota(jnp.int32, shape, q_axis) - RADIUS * d
    return jnp.where((jnp.abs(kq) <= RADIUS * d) & ((kq & (d - 1)) == 0), 0.0, NEG_INF).astype(F32)


def _fill_padded(dst, src, d, S):
    pad = RADIUS * d
    dst[0:pad, :] = jnp.zeros((pad, HEAD_DIM), dst.dtype)
    dst[pad:pad + S, :] = src[...]
    dst[pad + S:pad + S + pad, :] = jnp.zeros((pad, HEAD_DIM), dst.dtype)


_NT = (((1,), (1,)), ((), ()))


def _attn_fwd(name, qkv):
    S = qkv.shape[0]
    T = ATTN_TQ
    nq = N_GROUPS * HEADS
    widths = [T + 2 * RADIUS * d for _, d in ATTN_GROUPS]

    def body(*refs):
        q_refs, k_refs, v_refs = refs[0:3], refs[3:6], refs[6:9]
        o_ref, lc_ref = refs[9:11]
        kp, vp, bias = refs[11:14], refs[14:17], refs[17:20]
        i0 = pl.multiple_of(pl.program_id(1) * T, T)

        @pl.when(pl.program_id(1) == 0)
        def _():
            for g, (_, d) in enumerate(ATTN_GROUPS):
                _fill_padded(kp[g], k_refs[g], d, S)
                _fill_padded(vp[g], v_refs[g], d, S)
                bias[g][...] = _band_bias((T, widths[g]), 0, d)

        m = jnp.full((T, 1), NEG_INF, F32)
        l = jnp.zeros((T, 1), F32)
        acc = jnp.zeros((T, HEAD_DIM), F32)
        for g, (_, d) in enumerate(ATTN_GROUPS):
            W = widths[g]
            kw = kp[g][pl.ds(i0, W), :]
            vw = vp[g][pl.ds(i0, W), :]
            key = i0 - RADIUS * d + lax.broadcasted_iota(jnp.int32, (1, W), 1)
            in_seq = jnp.where((key >= 0) & (key < S), 0.0, NEG_INF).astype(F32)
            s = lax.dot_general(q_refs[g][...], kw, _NT, preferred_element_type=F32) + bias[g][...] + in_seq
            m_new = jnp.maximum(m, jnp.max(s, axis=1, keepdims=True))
            alpha = jnp.exp(m - m_new)
            p = jnp.exp(s - m_new)
            l = l * alpha + jnp.sum(p, axis=1, keepdims=True)
            acc = acc * alpha + jnp.dot(p.astype(BF16), vw, preferred_element_type=F32)
            m = m_new
        o_ref[...] = (acc / l).astype(BF16)
        lc_ref[...] = m + jnp.log(l)

    in_specs = [pl.BlockSpec((T, HEAD_DIM), lambda h, i, g=g: (i, g * HEADS + h)) for g in range(N_GROUPS)]
    in_specs += [pl.BlockSpec((S, HEAD_DIM), lambda h, i, g=g: (0, nq + g * HEADS + h)) for g in range(N_GROUPS)]
    in_specs += [pl.BlockSpec((S, HEAD_DIM), lambda h, i, g=g: (0, 2 * nq + g * HEADS + h)) for g in range(N_GROUPS)]
    padded = [pltpu.VMEM((S + 2 * RADIUS * d, HEAD_DIM), BF16) for _, d in ATTN_GROUPS]
    scratch = padded + padded + [pltpu.VMEM((T, W), F32) for W in widths]
    scratch_bytes = sum(2 * (S + 2 * RADIUS * d) * HEAD_DIM * 2 for _, d in ATTN_GROUPS) + sum(T * W * 4 for W in widths)
    return pl.pallas_call(
        body, name=name, grid=(HEADS, S // T), in_specs=in_specs,
        out_specs=[pl.BlockSpec((T, HEAD_DIM), lambda h, i: (i, h)), pl.BlockSpec((None, T, 1), lambda h, i: (h, i, 0))],
        out_shape=[jax.ShapeDtypeStruct((S, ATTN_W), BF16), jax.ShapeDtypeStruct((HEADS, S, 1), F32)],
        scratch_shapes=scratch,
        compiler_params=pltpu.CompilerParams(dimension_semantics=("parallel", "arbitrary"),
                                             vmem_limit_bytes=_vmem_limit(6 * S * HEAD_DIM * 2 + 8 * T * HEAD_DIM * 4,
                                                                          scratch_bytes + 4 * T * widths[-1] * 4)),
    )(*([qkv] * 9))


_TN = (((0,), (0,)), ((), ()))


def _attn_bwd(name, qkv, attn, dattn, lse_c):
    S = qkv.shape[0]
    T = ATTN_TQ
    nq = N_GROUPS * HEADS
    W3 = QKV_W // 3
    n_i = S // T
    wmax = T + 2 * ATTN_PAD_MAX
    s_pad = S + 2 * ATTN_PAD_MAX

    def body(q_ref, k_ref, v_ref, o_ref, do_ref, lc_ref, dq_ref, dk_ref, dv_ref, kp, vp, dk_acc, dv_acc, bias):
        g_id, i = pl.program_id(1), pl.program_id(2)
        i0 = pl.multiple_of(i * T, T)
        q, do = q_ref[...], do_ref[...]
        delta = jnp.sum(do.astype(F32) * o_ref[...].astype(F32), axis=1, keepdims=True)
        lse = lc_ref[...]

        def group(d):
            W, pad = T + 2 * RADIUS * d, RADIUS * d

            @pl.when(i == 0)
            def _():
                _fill_padded(kp, k_ref, d, S)
                _fill_padded(vp, v_ref, d, S)
                dk_acc[...] = jnp.zeros_like(dk_acc)
                dv_acc[...] = jnp.zeros_like(dv_acc)
                bias[:, 0:W] = _band_bias((T, W), 0, d)

            kw = kp[pl.ds(i0, W), :]
            vw = vp[pl.ds(i0, W), :]
            key = i0 - pad + lax.broadcasted_iota(jnp.int32, (1, W), 1)
            in_seq = jnp.where((key >= 0) & (key < S), 0.0, NEG_INF).astype(F32)
            s = lax.dot_general(q, kw, _NT, preferred_element_type=F32) + bias[:, 0:W] + in_seq
            p = jnp.exp(s - lse)
            dp = lax.dot_general(do, vw, _NT, preferred_element_type=F32)
            ds = (p * (dp - delta)).astype(BF16)
            dq_ref[...] = (jnp.dot(ds, kw, preferred_element_type=F32) * ATTN_SCALE).astype(BF16)
            dk_acc[pl.ds(i0, W), :] += lax.dot_general(ds, q, _TN, preferred_element_type=F32)
            dv_acc[pl.ds(i0, W), :] += lax.dot_general(p.astype(BF16), do, _TN, preferred_element_type=F32)

            @pl.when(i == n_i - 1)
            def _():
                dk_ref[...] = dk_acc[pad:pad + S, :].astype(BF16)
                dv_ref[...] = dv_acc[pad:pad + S, :].astype(BF16)

        for g, (_, d) in enumerate(ATTN_GROUPS):
            pl.when(g_id == g)(functools.partial(group, d))

    tile = lambda off: pl.BlockSpec((T, HEAD_DIM), lambda h, g, i: (i, off + g * HEADS + h))
    full = lambda off: pl.BlockSpec((S, HEAD_DIM), lambda h, g, i: (0, off + g * HEADS + h))
    headt = pl.BlockSpec((T, HEAD_DIM), lambda h, g, i: (i, h))
    scratch_bytes = 2 * s_pad * HEAD_DIM * (2 + 4) + T * wmax * 4
    return pl.pallas_call(
        body, name=name, grid=(HEADS, N_GROUPS, n_i),
        in_specs=[tile(0), full(nq), full(2 * nq), headt, headt, pl.BlockSpec((None, T, 1), lambda h, g, i: (h, i, 0))],
        out_specs=[tile(0), full(0), full(0)],
        out_shape=[jax.ShapeDtypeStruct((S, W3), BF16)] * 3,
        scratch_shapes=[pltpu.VMEM((s_pad, HEAD_DIM), BF16), pltpu.VMEM((s_pad, HEAD_DIM), BF16),
                        pltpu.VMEM((s_pad, HEAD_DIM), F32), pltpu.VMEM((s_pad, HEAD_DIM), F32), pltpu.VMEM((T, wmax), F32)],
        compiler_params=pltpu.CompilerParams(dimension_semantics=("parallel", "arbitrary", "arbitrary"),
                                             vmem_limit_bytes=_vmem_limit(4 * S * HEAD_DIM * 2 + 8 * T * HEAD_DIM * 4,
                                                                          scratch_bytes + 5 * T * wmax * 4)),
    )(qkv, qkv, qkv, attn, dattn, lse_c)


def _sg_parts(u, v, lng, lnb):
    gu = _gelu(u)
    gv = _gelu(v)
    mu = jnp.mean(gv, axis=-1, keepdims=True)
    xc = gv - mu
    rstd = lax.rsqrt(jnp.mean(xc * xc, axis=-1, keepdims=True) + NORM_EPS)
    xhat = xc * rstd
    vn = xhat * lng + lnb
    return gu, xhat, rstd, vn


def _sg_fwd(name, z, sg_w, sg_bc, lng, lnb, o_sg0):
    S = z.shape[0]
    T = SG_CHUNK
    cb = 512
    assert o_sg0 % cb == 0
    b0 = o_sg0 // cb

    def body(u0, u1, v0, v1, w_ref, b_ref, g_ref, be_ref, o_ref):
        u = jnp.concatenate([u0[...], u1[...]], axis=1)
        v = jnp.concatenate([v0[...], v1[...]], axis=1)
        gu, _, _, vn = _sg_parts(u, v, g_ref[...], be_ref[...])
        vnb = vn.astype(BF16)
        for g in range(SG_GROUPS):
            sl = slice(g * SG_CHUNK, (g + 1) * SG_CHUNK)
            mixed = jnp.dot(w_ref[g], vnb[:, sl], preferred_element_type=F32) + b_ref[g]
            o_ref[:, sl] = (gu[:, sl] * mixed).astype(BF16)

    zs = lambda k: pl.BlockSpec((T, cb), lambda i, k=k: (i, b0 + k))
    const3 = lambda shp: pl.BlockSpec(shp, lambda i: (0, 0, 0))
    vec = pl.BlockSpec((1, SG_W), lambda i: (0, 0))
    return pl.pallas_call(
        body, name=name, grid=(S // T,),
        in_specs=[zs(0), zs(1), zs(2), zs(3), const3((SG_GROUPS, SG_CHUNK, SG_CHUNK)), const3((SG_GROUPS, SG_CHUNK, 1)), vec, vec],
        out_specs=pl.BlockSpec((T, SG_W), lambda i: (i, 0)), out_shape=jax.ShapeDtypeStruct((S, SG_W), BF16),
        compiler_params=pltpu.CompilerParams(dimension_semantics=("parallel",), vmem_limit_bytes=_vmem_limit(4 * 1024 * 1024, 8 * T * SG_W * 4)),
    )(z, z, z, z, sg_w, sg_bc, lng, lnb)


def _sg_bwd(name, z, dsg, sg_w, sg_wt, sg_bc, lng, lnb, o_sg0):
    S = z.shape[0]
    T = SG_CHUNK
    cb = 512
    b0 = o_sg0 // cb

    def body(u0, u1, v0, v1, d_ref, w_ref, wt_ref, b_ref, g_ref, be_ref, dz_ref, dw_ref, db_ref, dg_ref, dbe_ref):
        i = pl.program_id(0)
        u = jnp.concatenate([u0[...], u1[...]], axis=1)
        v = jnp.concatenate([v0[...], v1[...]], axis=1)
        gu, xhat, rstd, vn = _sg_parts(u, v, g_ref[...], be_ref[...])
        vnb = vn.astype(BF16)
        dsg_v = d_ref[...].astype(F32)
        dmix = dsg_v * gu
        dmixb = dmix.astype(BF16)
        dvn_parts, mixed_parts, dw_parts, db_parts = [], [], [], []
        for g in range(SG_GROUPS):
            sl = slice(g * SG_CHUNK, (g + 1) * SG_CHUNK)
            mixed_parts.append(jnp.dot(w_ref[g], vnb[:, sl], preferred_element_type=F32) + b_ref[g])
            dvn_parts.append(jnp.dot(wt_ref[g], dmixb[:, sl], preferred_element_type=F32))
            dw_parts.append(lax.dot_general(dmixb[:, sl], vnb[:, sl], _NT, preferred_element_type=F32))
            db_parts.append(jnp.sum(dmix[:, sl], axis=1, keepdims=True))
        mixed = jnp.concatenate(mixed_parts, axis=1)
        dvn = jnp.concatenate(dvn_parts, axis=1)
        dzu = dsg_v * mixed * _gelu_grad(u)
        dxh = dvn * g_ref[...]
        dgv = rstd * (dxh - jnp.mean(dxh, axis=-1, keepdims=True) - xhat * jnp.mean(dxh * xhat, axis=-1, keepdims=True))
        dzv = dgv * _gelu_grad(v)
        dz_ref[:, :SG_W] = dzu.astype(BF16)
        dz_ref[:, SG_W:] = dzv.astype(BF16)
        dgp = jnp.sum(dvn * xhat, axis=0, keepdims=True)
        dbp = jnp.sum(dvn, axis=0, keepdims=True)

        @pl.when(i == 0)
        def _():
            for g in range(SG_GROUPS):
                dw_ref[g] = dw_parts[g]
                db_ref[g] = db_parts[g]
            dg_ref[...] = dgp
            dbe_ref[...] = dbp

        @pl.when(i > 0)
        def _():
            for g in range(SG_GROUPS):
                dw_ref[g] += dw_parts[g]
                db_ref[g] += db_parts[g]
            dg_ref[...] += dgp
            dbe_ref[...] += dbp

    zs = lambda k: pl.BlockSpec((T, cb), lambda i, k=k: (i, b0 + k))
    const3 = lambda shp: pl.BlockSpec(shp, lambda i: (0, 0, 0))
    vec = pl.BlockSpec((1, SG_W), lambda i: (0, 0))
    return pl.pallas_call(
        body, name=name, grid=(S // T,),
        in_specs=[zs(0), zs(1), zs(2), zs(3), pl.BlockSpec((T, SG_W), lambda i: (i, 0)),
                  const3((SG_GROUPS, SG_CHUNK, SG_CHUNK)), const3((SG_GROUPS, SG_CHUNK, SG_CHUNK)), const3((SG_GROUPS, SG_CHUNK, 1)),
                  vec, vec],
        out_specs=[pl.BlockSpec((T, 2 * SG_W), lambda i: (i, 0)), const3((SG_GROUPS, SG_CHUNK, SG_CHUNK)),
                   const3((SG_GROUPS, SG_CHUNK, 1)), vec, vec],
        out_shape=[jax.ShapeDtypeStruct((S, 2 * SG_W), BF16), jax.ShapeDtypeStruct((SG_GROUPS, SG_CHUNK, SG_CHUNK), F32),
                   jax.ShapeDtypeStruct((SG_GROUPS, SG_CHUNK, 1), F32), jax.ShapeDtypeStruct((1, SG_W), F32),
                   jax.ShapeDtypeStruct((1, SG_W), F32)],
        compiler_params=pltpu.CompilerParams(dimension_semantics=("arbitrary",),
                                             vmem_limit_bytes=_vmem_limit(6 * 1024 * 1024, 16 * T * SG_W * 4)),
    )(z, z, z, z, dsg, sg_w, sg_wt, sg_bc, lng, lnb)


def _gate_bwd(name, z, dmerged, y_attn, y_sg, o_g0, in_w):
    S, D = dmerged.shape
    tr = _pick(S, (512, 256, 128, 8))
    cb = _pick(D, (512, 256, 128))
    assert o_g0 % cb == 0
    nd = D // cb
    b0 = o_g0 // cb

    def body(z_ref, dm_ref, ya_ref, ys_ref, dz_ref, dy_ref):
        jj = pl.program_id(1)
        gate = _sigmoid(z_ref[...])
        dm = dm_ref[...].astype(F32)
        y = jnp.where(jj < nd, ya_ref[...], ys_ref[...]).astype(F32)
        dz_ref[...] = (dm * y * gate * (1.0 - gate)).astype(BF16)
        dy_ref[...] = (dm * gate).astype(BF16)

    half = pl.BlockSpec((tr, cb), lambda i, jj: (i, jj % nd))
    return pl.pallas_call(
        body, name=name, grid=(S // tr, 2 * nd),
        in_specs=[pl.BlockSpec((tr, cb), lambda i, jj: (i, b0 + jj)), half, half, half],
        out_specs=[pl.BlockSpec((tr, cb), lambda i, jj: (i, b0 + jj)), pl.BlockSpec((tr, cb), lambda i, jj: (i, jj))],
        out_shape=[jax.ShapeDtypeStruct((S, in_w), BF16), jax.ShapeDtypeStruct((S, 2 * D), BF16)],
        compiler_params=pltpu.CompilerParams(dimension_semantics=("parallel", "arbitrary"),
                                             vmem_limit_bytes=_vmem_limit(tr * cb * 14, 6 * tr * cb * 4)),
    )(z, dmerged, y_attn, y_sg)


def _row(v):
    return v.reshape(1, -1)


def _local_step(x, p, target, wf, small, after_group):
    S, D = x.shape
    L = p.shape[0]
    in_w = wf["w_in"][0].shape[1]
    ff = wf["w_ff_gate"][0].shape[1]
    ple = p.shape[2]
    o_sg0, o_g0 = QKV_W, QKV_W + 2 * SG_W
    cosf, sinf = _rope_tables(S)
    pb = p.astype(BF16)
    tmb = _pick(S, (1024, 512, 256))
    tn_in = _pick(in_w, (768, 1024, 512))
    tn_d = _pick(D, (1024, 512, 256))
    tn_g = _pick(D, (512, 256))
    tn_ff = _pick(ff, (512, 256))

    saved = []
    xs = x
    for i in range(L):
        sv = {"x0": xs}
        h = _rmsnorm_fwd(f"norm_mix_{i}", xs, _row(small["norm_mix"][i]))
        (z,) = _mm(f"in_proj_{i}", [dict(a=h, b=wf["w_in"], bl=i, mode="nn", K=D)], S, in_w,
                   [dict(shape=(S, in_w), dtype=F32)], _first, tm=tmb, tn=tn_in)
        qkv = _rope_fwd(f"rope_{i}", z, cosf, sinf)
        attn, lse_c = _attn_fwd(f"attn_{i}", qkv)
        sgw = small["sg_w"][i].astype(BF16)
        sgbc = small["sg_b"][i].reshape(SG_GROUPS, SG_CHUNK, 1)
        sg = _sg_fwd(f"sgu_{i}", z, sgw, sgbc, _row(small["sg_ln_g"][i]), _row(small["sg_ln_b"][i]), o_sg0)

        def merge(accs, tiles, rows):
            ya, ys = accs[0].astype(BF16), accs[1].astype(BF16)
            g0, g1 = _sigmoid(tiles[0]), _sigmoid(tiles[1])
            return [ya, ys, g0 * ya.astype(F32) + g1 * ys.astype(F32)]

        y_attn, y_sg, merged = _mm(
            f"branches_{i}",
            [dict(a=attn, b=wf["w_br_attn"], bl=i, mode="nn", K=ATTN_W), dict(a=sg, b=wf["w_br_sg"], bl=i, mode="nn", K=SG_W)],
            S, D, [dict(shape=(S, D), dtype=BF16)] * 3, merge,
            tiles=[dict(x=z, off=o_g0), dict(x=z, off=o_g0 + D)], tm=tmb, tn=tn_g)
        (x1,) = _mm(f"out_proj_{i}", [dict(a=merged, b=wf["w_out"], bl=i, mode="nn", K=D)], S, D,
                    [dict(shape=(S, D), dtype=F32)], lambda a, t, r: [t[0] + a[0]], tiles=[dict(x=xs)], tm=tmb, tn=tn_d)
        h2 = _rmsnorm_fwd(f"norm_ffn_{i}", x1, _row(small["norm_ffn"][i]))

        def swiglu(accs, tiles, rows):
            fg = accs[0].astype(BF16).astype(F32)
            fu = accs[1].astype(BF16).astype(F32)
            return [fg, fu, fg * _sigmoid(fg) * fu]

        ffg, ffu, act = _mm(
            f"ff_in_{i}",
            [dict(a=h2, b=wf["w_ff_gate"], bl=i, mode="nn", K=D), dict(a=h2, b=wf["w_ff_up"], bl=i, mode="nn", K=D)],
            S, ff, [dict(shape=(S, ff), dtype=BF16)] * 3, swiglu, tm=tmb, tn=tn_ff)
        (x2,) = _mm(f"ff_out_{i}", [dict(a=act, b=wf["w_ff_down"], bl=i, mode="nn", K=ff)], S, D,
                    [dict(shape=(S, D), dtype=F32)], lambda a, t, r: [t[0] + a[0]], tiles=[dict(x=x1)], tm=tmb, tn=tn_d)
        h3 = _rmsnorm_fwd(f"norm_ple_{i}", x2, _row(small["norm_ple"][i]))

        def ple_mix(accs, tiles, rows):
            gp = _sigmoid(accs[0]).astype(BF16)
            pe = accs[1].astype(BF16)
            return [tiles[0] + gp.astype(F32) * pe.astype(F32), gp, pe]

        x3, gp, pe = _mm(
            f"ple_{i}",
            [dict(a=h3, b=wf["w_ple_gate"], bl=i, mode="nn", K=D), dict(a=pb, al=i, b=wf["w_ple"], bl=i, mode="nn", K=ple)],
            S, D, [dict(shape=(S, D), dtype=F32), dict(shape=(S, D), dtype=BF16), dict(shape=(S, D), dtype=BF16)], ple_mix,
            tiles=[dict(x=x2)], tm=tmb, tn=tn_g)
        sv.update(h=h, z=z, qkv=qkv, attn=attn, lse_c=lse_c, sg=sg, y_attn=y_attn, y_sg=y_sg, merged=merged,
                  x1=x1, h2=h2, ffg=ffg, ffu=ffu, act=act, x2=x2, h3=h3, gp=gp, pe=pe, sgw=sgw, sgbc=sgbc)
        saved.append(sv)
        xs = x3

    loss_cell, dx, dxb, dg_final = _loss_head(xs, _row(small["norm_final"]), target)

    gw = {n: [None] * L for n in BIG}
    gs = {n: [None] * L for n in SMALL if n != "norm_final"}

    def dw(n, i, a, a_off, b, bn_off, K_rows, N_cols, tm, tn):
        (gw[n][i],) = _mm(f"d_{n}_{i}", [dict(a=a, b=b, mode="tn", K=S, a_off=a_off, bn_off=bn_off)], K_rows, N_cols,
                          [dict(shape=(K_rows, N_cols), dtype=BF16)], _first, tm=tm, tn=tn)

    for i in reversed(range(L)):
        sv = saved[i]
        dpre, dpe = _ew(f"ple_gate_bwd_{i}",
                        lambda d, g, e: [d * e.astype(F32) * g.astype(F32) * (1.0 - g.astype(F32)), d * g.astype(F32)],
                        [dx, sv["gp"], sv["pe"]], [BF16, BF16], S, D)
        (dh3,) = _mm(f"d_h3_{i}", [dict(a=dpre, b=wf["w_ple_gate"], bl=i, mode="nt", K=D)], S, D,
                     [dict(shape=(S, D), dtype=F32)], _first, tm=tmb, tn=tn_d)
        dw("w_ple_gate", i, sv["h3"], 0, dpre, 0, D, D, tn_d, tn_d)
        dw("w_ple", i, pb[i], 0, dpe, 0, ple, D, _pick(ple, (256, 128)), _pick(D, (2048, 1024, 512, 256)))
        dx, dxb, gs["norm_ple"][i] = _rmsnorm_bwd(f"norm_ple_bwd_{i}", sv["x2"], _row(small["norm_ple"][i]), dh3, dx)
        def swiglu_bwd(accs, tiles, rows):
            da = accs[0].astype(BF16).astype(F32)
            fg, fu = tiles[0].astype(F32), tiles[1].astype(F32)
            sg_ = _sigmoid(fg)
            return [da * fu * (sg_ * (1.0 + fg * (1.0 - sg_))), da * (fg * sg_)]

        dffg, dffu = _mm(f"d_act_{i}", [dict(a=dxb, b=wf["w_ff_down"], bl=i, mode="nt", K=D)], S, ff,
                         [dict(shape=(S, ff), dtype=BF16)] * 2, swiglu_bwd, tiles=[dict(x=sv["ffg"]), dict(x=sv["ffu"])],
                         tm=tmb, tn=tn_ff)
        dw("w_ff_down", i, sv["act"], 0, dxb, 0, ff, D, tn_ff, _pick(D, (2048, 1024, 512, 256)))
        dw("w_ff_gate", i, sv["h2"], 0, dffg, 0, D, ff, _pick(D, (2048, 1024, 512, 256)), tn_ff)
        dw("w_ff_up", i, sv["h2"], 0, dffu, 0, D, ff, _pick(D, (2048, 1024, 512, 256)), tn_ff)
        (dffg, dffu), _ = lax.optimization_barrier(((dffg, dffu), after_group(i, "ffn", {n: gw[n][i] for n in GRAD_GROUPS["ffn"]})))
        (dh2,) = _mm(f"d_h2_{i}", [dict(a=dffg, b=wf["w_ff_gate"], bl=i, mode="nt", K=ff),
                                   dict(a=dffu, b=wf["w_ff_up"], bl=i, mode="nt", K=ff)], S, D,
                     [dict(shape=(S, D), dtype=F32)], lambda a, t, r: [a[0] + a[1]], tm=tmb, tn=tn_d)
        dx, dxb, gs["norm_ffn"][i] = _rmsnorm_bwd(f"norm_ffn_bwd_{i}", sv["x1"], _row(small["norm_ffn"][i]), dh2, dx)
        (dmerged,) = _mm(f"d_merged_{i}", [dict(a=dxb, b=wf["w_out"], bl=i, mode="nt", K=D)], S, D,
                         [dict(shape=(S, D), dtype=BF16)], _first, tm=tmb, tn=tn_d)
        dw("w_out", i, sv["merged"], 0, dxb, 0, D, D, tn_d, tn_d)
        dz, dy = _gate_bwd(f"gate_bwd_{i}", sv["z"], dmerged, sv["y_attn"], sv["y_sg"], o_g0, in_w)
        (dattn,) = _mm(f"d_attn_{i}", [dict(a=dy, b=wf["w_br_attn"], bl=i, mode="nt", K=D)], S, ATTN_W,
                       [dict(shape=(S, ATTN_W), dtype=BF16)], _first, tm=tmb, tn=ATTN_W)
        (dsg,) = _mm(f"d_sg_{i}", [dict(a=dy, a_off=D, b=wf["w_br_sg"], bl=i, mode="nt", K=D)], S, SG_W,
                     [dict(shape=(S, SG_W), dtype=BF16)], _first, tm=tmb, tn=SG_W)
        dw("w_br_attn", i, sv["attn"], 0, dy, 0, ATTN_W, D, ATTN_W, _pick(D, (2048, 1024, 512, 256)))
        dw("w_br_sg", i, sv["sg"], 0, dy, D, SG_W, D, SG_W, _pick(D, (1024, 512, 256)))
        sgwt = jnp.swapaxes(small["sg_w"][i], 1, 2).astype(BF16)
        dzuv, gs["sg_w"][i], dsgb, dlg, dlb = _sg_bwd(f"sgu_bwd_{i}", sv["z"], dsg, sv["sgw"], sgwt, sv["sgbc"],
                                                      _row(small["sg_ln_g"][i]), _row(small["sg_ln_b"][i]), o_sg0)
        gs["sg_b"][i], gs["sg_ln_g"][i], gs["sg_ln_b"][i] = dsgb.reshape(SG_GROUPS, SG_CHUNK), dlg[0], dlb[0]
        dq, dk, dv = _attn_bwd(f"attn_bwd_{i}", sv["qkv"], sv["attn"], dattn, sv["lse_c"])
        dz = _rope_bwd(f"rope_bwd_{i}", dq, dk, dv, dzuv, cosf, sinf, dz)
        dw("w_in", i, sv["h"], 0, dz, 0, D, in_w, tn_d, tn_in)
        dz, _ = lax.optimization_barrier((dz, after_group(i, "mix", {n: gw[n][i] for n in GRAD_GROUPS["mix"]})))
        (dh,) = _mm(f"d_h_{i}", [dict(a=dz, b=wf["w_in"], bl=i, mode="nt", K=in_w)], S, D,
                    [dict(shape=(S, D), dtype=F32)], _first, tm=tmb, tn=tn_d)
        dx, dxb, gs["norm_mix"][i] = _rmsnorm_bwd(f"norm_mix_bwd_{i}", sv["x0"], _row(small["norm_mix"][i]), dh, dx)

    gsmall ={n: jnp.stack([jnp.reshape(v, small[n].shape[1:]) for v in gs[n]]) for n in gs}
    gsmall["norm_final"] = dg_final[0]
    return loss_cell, dx, gsmall


def _place():
    x, y, c = lax.axis_index("x"), lax.axis_index("y"), lax.axis_index("c")
    return x, y, c, 2 * x + y


def _chip_of(s):
    return s // 2, s % 2


def _aligned(v, m):
    return v if isinstance(v, int) else pl.multiple_of(v, m)


def _piece(name, shape, s, c):
    K, N = shape
    if name in ROW_SHARDED or name == SMALL_BLOCKS:
        ks = K // 4
        return s * ks + c * (ks // 2), ks // 2, 0, N
    ns = N // 4
    return c * (K // 2), K // 2, s * ns, ns


def _handshake(peers):
    barrier = pltpu.get_barrier_semaphore()
    for peer in peers:
        pl.semaphore_signal(barrier, inc=1, device_id=peer, device_id_type=MESH)
    pl.semaphore_wait(barrier, len(peers))


def _gather_body(names, shapes, src, dst, send_sems, recv_sems, local_sems):
    n_w = len(names)
    x, y, c, s = _place()
    sib = (x, y, 1 - c)
    rel = [1, 2, 3]

    def where(w, ps, pc):
        r0, nr, c0, nc = _piece(names[w], shapes[names[w]], ps, pc)
        return dst[w].at[pl.ds(_aligned(r0, 16), nr), pl.ds(_aligned(c0, LANES), nc)]

    def copy(w, k, ps, pc, to, from_src=False):
        return pltpu.make_async_remote_copy(
            src_ref=src[w] if from_src else where(w, ps, pc), dst_ref=where(w, ps, pc),
            send_sem=send_sems.at[w, k], recv_sem=recv_sems.at[w, k], device_id=to, device_id_type=MESH)

    mine, first, passed = [], [], []
    for w in range(n_w):
        cp = pltpu.make_async_copy(src[w], where(w, s, c), local_sems.at[w])
        cp.start()
        mine.append(cp)
        first.append(copy(w, 0, s, c, sib, from_src=True))
        for j in rel:
            first.append(copy(w, j, s, c, (*_chip_of(s ^ j), c), from_src=True))
    for cp in first:
        cp.start()
    for w in range(n_w):
        for j in rel:
            copy(w, j, s ^ j, c, sib).wait_recv()
            fw = copy(w, 3 + j, s ^ j, c, sib)
            fw.start()
            passed.append(fw)
    for w in range(n_w):
        copy(w, 0, s, 1 - c, sib).wait_recv()
        for j in rel:
            copy(w, 3 + j, s ^ j, 1 - c, sib).wait_recv()
    for cp in first + passed:
        cp.wait_send()
    for cp in mine:
        cp.wait()


def _gather_sems(n_w):
    return (pltpu.SemaphoreType.DMA((n_w, 7)), pltpu.SemaphoreType.DMA((n_w, 7)), pltpu.SemaphoreType.DMA((n_w,)))


def _gather_peers():
    x, y, c, s = _place()
    return [(x, y, 1 - c)] + [(*_chip_of(s ^ j), c) for j in (1, 2, 3)]


def _gather_weights(name, pieces, shapes):
    names = list(pieces)
    n_w = len(names)

    def body(*refs):
        _gather_body(names, shapes, refs[:n_w], refs[n_w:2 * n_w], *refs[2 * n_w:])

    anyspec = pl.BlockSpec(memory_space=pl.ANY)
    out = pl.pallas_call(
        body, name=name, in_specs=[anyspec] * n_w, out_specs=[anyspec] * n_w,
        out_shape=[jax.ShapeDtypeStruct(tuple(shapes[n]), pieces[n].dtype) for n in names], scratch_shapes=list(_gather_sems(n_w)),
    )(*[pieces[n] for n in names])
    return dict(zip(names, out))


def _gather_weights_async(name, pieces, shapes):
    names = list(pieces)
    n_w = len(names)
    src = [jax.new_ref(pieces[n], memory_space=pltpu.MemorySpace.HBM) for n in names]
    dst = [jax.empty_ref(jax.ShapeDtypeStruct(tuple(shapes[n]), pieces[n].dtype), memory_space=pltpu.MemorySpace.HBM)
           for n in names]

    @pl.kernel(mesh=plsc.ScalarSubcoreMesh(axis_name="seq", num_cores=1), name=name, scratch_types=_gather_sems(n_w),
               compiler_params=pltpu.CompilerParams(collective_id=GATHER_COLLECTIVE_ID))
    def launch(send_sems, recv_sems, local_sems):
        _handshake(_gather_peers())
        _gather_body(names, shapes, src, dst, send_sems, recv_sems, local_sems)

    launch()
    return {n: d[...] for n, d in zip(names, dst)}


def _halves_view(name, g):
    L, K, N = g.shape
    if name in ROW_SHARDED:
        return g.reshape(L * 4, 2, K // 8, N)
    return g.reshape(L, 2, K // 2, N)


def _exchange_halves(name, views):
    names = list(views)
    n_w = len(names)

    def body(*refs):
        src = refs[:n_w]
        got = refs[n_w:2 * n_w]
        send_sems, recv_sems = refs[2 * n_w:]
        x, y, c, s = _place()
        remote = [pltpu.make_async_remote_copy(src_ref=src[w].at[:, 1 - c], dst_ref=got[w], send_sem=send_sems.at[w],
                                               recv_sem=recv_sems.at[w], device_id=(x, y, 1 - c), device_id_type=MESH)
                  for w in range(n_w)]
        for cp in remote:
            cp.start()
        for cp in remote:
            cp.wait()

    anyspec = pl.BlockSpec(memory_space=pl.ANY)
    out = pl.pallas_call(
        body, name=name, in_specs=[anyspec] * n_w, out_specs=[anyspec] * n_w,
        out_shape=[jax.ShapeDtypeStruct((v.shape[0],) + v.shape[2:], BF16) for v in views.values()],
        scratch_shapes=[pltpu.SemaphoreType.DMA((n_w,)), pltpu.SemaphoreType.DMA((n_w,))],
    )(*views.values())
    return dict(zip(names, out))


def _chip_sum(name, view, got, place):
    A, _, R, C = view.shape
    tc = _pick(C, (2048, 1536, 1408, 1024, 512, 256, 128))
    tr = _pick(R, [t for t in (1024, 512, 256, 128, 64, 32, 16) if t * tc <= 4 * EW_TILE_ELEMS] + [8])

    def body(p_ref, own_ref, got_ref, o_ref):
        o_ref[...] = (own_ref[...].astype(F32) + got_ref[...].astype(F32)).astype(BF16)

    flat = pl.BlockSpec((None, tr, tc), lambda a, i, j, p: (a, i, j))
    return pl.pallas_call(
        body, name=name, out_shape=jax.ShapeDtypeStruct((A, R, C), BF16),
        grid_spec=pltpu.PrefetchScalarGridSpec(
            num_scalar_prefetch=1, grid=(A, R // tr, C // tc),
            in_specs=[pl.BlockSpec((None, None, tr, tc), lambda a, i, j, p: (a, p[0], i, j)), flat], out_specs=flat),
        compiler_params=pltpu.CompilerParams(dimension_semantics=("parallel", "parallel", "parallel"),
                                             vmem_limit_bytes=_vmem_limit(6 * tr * tc, 3 * tr * tc * 4)),
    )(place, view, got)


def _shard_view(name, ps, L):
    return ps.reshape(L, 4, *ps.shape[1:]) if name in ROW_SHARDED else ps


def _scatter_body(names, src, dst, send_sems, recv_sems):
    x, y, c, s = _place()

    def shard(w, t):
        if names[w] in ROW_SHARDED:
            return src[w].at[:, t]
        ns = src[w].shape[2] // 4
        return src[w].at[:, :, pl.ds(pl.multiple_of(t * ns, LANES), ns)]

    remote = []
    for w in range(len(names)):
        for j in (1, 2, 3):
            remote.append(pltpu.make_async_remote_copy(
                src_ref=shard(w, s ^ j), dst_ref=dst[w].at[j - 1], send_sem=send_sems.at[w, j - 1],
                recv_sem=recv_sems.at[w, j - 1], device_id=(*_chip_of(s ^ j), c), device_id_type=MESH))
    for cp in remote:
        cp.start()
    for cp in remote:
        cp.wait()


def _scatter_out_shape(name, v):
    return (3, v[0], v[2], v[3]) if name in ROW_SHARDED else (3, v[0], v[1], v[2] // 4)


def _scatter_sems(n_w):
    return (pltpu.SemaphoreType.DMA((n_w, 3)), pltpu.SemaphoreType.DMA((n_w, 3)))


def _scatter_chip_sums_async(name, psum):
    names = list(psum)
    n_w = len(names)
    src = [jax.new_ref(psum[n], memory_space=pltpu.MemorySpace.HBM) for n in names]
    dst = [jax.empty_ref(jax.ShapeDtypeStruct(_scatter_out_shape(n, psum[n].shape), BF16), memory_space=pltpu.MemorySpace.HBM)
           for n in names]

    @pl.kernel(mesh=plsc.ScalarSubcoreMesh(axis_name="seq", num_cores=1), name=name, scratch_types=_scatter_sems(n_w),
               compiler_params=pltpu.CompilerParams(collective_id=SCATTER_COLLECTIVE_ID))
    def launch(send_sems, recv_sems):
        _handshake(_gather_peers()[1:])
        _scatter_body(names, src, dst, send_sems, recv_sems)

    launch()
    return {n: d[...] for n, d in zip(names, dst)}


def _shard_sum(name, ps, parts, place, row_sharded, layer, n_layers, into):
    _, _, R, C = parts.shape
    tc = _pick(C, (2048, 1408, 1024, 896, 512, 384, 256, 128))
    tr = _pick(R, [t for t in (1024, 512, 256, 128, 64, 32, 16) if t * tc <= 2 * EW_TILE_ELEMS] + [8])

    def body(p_ref, own_ref, a_ref, b_ref, c_ref, *rest):
        o_ref = rest[-1]
        o_ref[...] = ((own_ref[...].astype(F32) + a_ref[...].astype(F32)) + b_ref[...].astype(F32)) + c_ref[...].astype(F32)

    if row_sharded:
        own_spec = pl.BlockSpec((None, None, tr, tc), lambda i, j, p: (0, p[1], i, j))
    else:
        own_spec = pl.BlockSpec((None, tr, tc), lambda i, j, p: (0, i, p[1] * (C // tc) + j))
    part = lambda k: pl.BlockSpec((None, None, tr, tc), lambda i, j, p, k=k: (k, 0, i, j))
    in_specs, args, aliases = [own_spec, part(0), part(1), part(2)], [place, ps, parts, parts, parts], {}
    if into is not None:
        in_specs.append(pl.BlockSpec(memory_space=pl.ANY))
        args.append(into)
        aliases = {5: 0}
    return pl.pallas_call(
        body, name=name, out_shape=jax.ShapeDtypeStruct((n_layers, 2, R, C), F32),
        grid_spec=pltpu.PrefetchScalarGridSpec(
            num_scalar_prefetch=1, grid=(R // tr, C // tc), in_specs=in_specs,
            out_specs=pl.BlockSpec((None, None, tr, tc), lambda i, j, p: (layer, p[0], i, j))),
        input_output_aliases=aliases,
        compiler_params=pltpu.CompilerParams(dimension_semantics=("parallel", "parallel"),
                                             vmem_limit_bytes=_vmem_limit(12 * tr * tc, 5 * tr * tc * 4)),
    )(*args)


def _all_peers():
    x, y, c, s = _place()
    return [(x, y, 1 - c)] + [(*_chip_of(s ^ j), h) for j in (1, 2, 3) for h in (0, 1)]


def _scatter_partials_async(name, views):
    names = list(views)
    n_w = len(names)
    src = [jax.new_ref(views[n], memory_space=pltpu.MemorySpace.HBM) for n in names]
    dst = [jax.empty_ref(jax.ShapeDtypeStruct(_partials_out_shape(n, views[n].shape), BF16), memory_space=pltpu.MemorySpace.HBM)
           for n in names]

    @pl.kernel(mesh=plsc.ScalarSubcoreMesh(axis_name="seq", num_cores=1), name=name, scratch_types=_partials_sems(n_w),
               compiler_params=pltpu.CompilerParams(collective_id=PARTIALS_COLLECTIVE_ID))
    def launch(send_sems, recv_sems):
        _handshake(_all_peers())
        _scatter_partials_body(names, src, dst, send_sems, recv_sems)

    launch()
    return {n: d[...] for n, d in zip(names, dst)}


def _partials_out_shape(name, v):
    return (7, 1, v[2], v[3] if name in ROW_SHARDED else v[3] // 4)


def _partials_sems(n_w):
    return (pltpu.SemaphoreType.DMA((n_w, 7)), pltpu.SemaphoreType.DMA((n_w, 7)))


def _scatter_partials_body(names, src, dst, send_sems, recv_sems):
    x, y, c, s = _place()

    def piece(w, t, h):
        if names[w] in ROW_SHARDED:
            return src[w].at[pl.ds(t, 1), h]
        ns = src[w].shape[3] // 4
        return src[w].at[:, h, :, pl.ds(pl.multiple_of(t * ns, LANES), ns)]

    sent = []
    for w in range(len(names)):
        for j in (1, 2, 3):
            for h in (0, 1):
                sent.append(pltpu.make_async_remote_copy(
                    src_ref=piece(w, s ^ j, h), dst_ref=dst[w].at[2 * (j - 1) + c], send_sem=send_sems.at[w, 2 * (j - 1) + h],
                    recv_sem=recv_sems.at[w, 2 * (j - 1) + c], device_id=(*_chip_of(s ^ j), h), device_id_type=MESH))
        sent.append(pltpu.make_async_remote_copy(
            src_ref=piece(w, s, 1 - c), dst_ref=dst[w].at[6], send_sem=send_sems.at[w, 6], recv_sem=recv_sems.at[w, 6],
            device_id=(x, y, 1 - c), device_id_type=MESH))
    for cp in sent:
        cp.start()
    for w in range(len(names)):
        for slot in range(7):
            pltpu.make_async_remote_copy(src_ref=dst[w].at[slot], dst_ref=dst[w].at[slot], send_sem=send_sems.at[w, slot],
                                         recv_sem=recv_sems.at[w, slot], device_id=(x, y, 1 - c), device_id_type=MESH).wait_recv()
    for cp in sent:
        cp.wait_send()


def _shard_sum_partials(name, view, parts, place, row_sharded, layer, n_layers, into):
    R, C = parts.shape[2:]
    tc = _pick(C, (2048, 1408, 1024, 896, 512, 384, 256, 128))
    tr = _pick(R, [t for t in (1024, 512, 256, 128, 64, 32, 16) if t * tc <= 2 * EW_TILE_ELEMS] + [8])

    def body(p_ref, own_ref, *rest):
        acc = own_ref[...].astype(F32) + rest[6][...].astype(F32)
        for k in range(6):
            acc = acc + rest[k][...].astype(F32)
        rest[-1][...] = acc

    if row_sharded:
        own_spec = pl.BlockSpec((None, None, tr, tc), lambda i, j, p: (p[1], p[0], i, j))
    else:
        own_spec = pl.BlockSpec((None, None, tr, tc), lambda i, j, p: (0, p[0], i, p[1] * (C // tc) + j))
    part = lambda k: pl.BlockSpec((None, None, tr, tc), lambda i, j, p, k=k: (k, 0, i, j))
    in_specs, args, aliases = [own_spec] + [part(k) for k in range(7)], [place, view] + [parts] * 7, {}
    if into is not None:
        in_specs.append(pl.BlockSpec(memory_space=pl.ANY))
        args.append(into)
        aliases = {9: 0}
    return pl.pallas_call(
        body, name=name, out_shape=jax.ShapeDtypeStruct((n_layers, 2, R, C), F32),
        grid_spec=pltpu.PrefetchScalarGridSpec(
            num_scalar_prefetch=1, grid=(R // tr, C // tc), in_specs=in_specs,
            out_specs=pl.BlockSpec((None, None, tr, tc), lambda i, j, p: (layer, p[0], i, j))),
        input_output_aliases=aliases,
        compiler_params=pltpu.CompilerParams(dimension_semantics=("parallel", "parallel"),
                                             vmem_limit_bytes=_vmem_limit(20 * tr * tc, 5 * tr * tc * 4)),
    )(*args)


def _share_halves(ghalf):
    names = list(ghalf)
    n_w = len(names)

    def body(*refs):
        src = refs[:n_w]
        dst = refs[n_w:2 * n_w]
        send_sems, recv_sems = refs[2 * n_w:]
        x, y, c, s = _place()
        remote = [pltpu.make_async_remote_copy(src_ref=src[w].at[:, c], dst_ref=dst[w].at[:, c], send_sem=send_sems.at[w],
                                               recv_sem=recv_sems.at[w], device_id=(x, y, 1 - c), device_id_type=MESH)
                  for w in range(n_w)]
        for cp in remote:
            cp.start()
        for cp in remote:
            cp.wait()

    anyspec = pl.BlockSpec(memory_space=pl.ANY)
    out = pl.pallas_call(
        body, name="share_halves", in_specs=[anyspec] * n_w, out_specs=[anyspec] * n_w,
        out_shape=[jax.ShapeDtypeStruct(ghalf[n].shape, F32) for n in names],
        input_output_aliases={w: w for w in range(n_w)},
        scratch_shapes=[pltpu.SemaphoreType.DMA((n_w,)), pltpu.SemaphoreType.DMA((n_w,))],
    )(*[ghalf[n] for n in names])
    return dict(zip(names, out))


def _adamw_math(w, g, m, v):
    m = ADAM_B1 * m + (1.0 - ADAM_B1) * g
    v = ADAM_B2 * v + (1.0 - ADAM_B2) * (g * g)
    m_hat = m / (1.0 - ADAM_B1 ** ADAM_STEP)
    v_hat = v / (1.0 - ADAM_B2 ** ADAM_STEP)
    delta = -ADAM_LR * (m_hat / (jnp.sqrt(v_hat) + ADAM_EPS) + ADAM_WD * w)
    return delta, m, v


def _adamw(name, w, g, m, v):
    shape = w.shape
    C = shape[-1]
    R = math.prod(shape[:-1])
    f = lambda a: a.reshape(R, C)
    delta, nm, nv = _ew(name, lambda w_, g_, m_, v_: list(_adamw_math(w_, g_, m_, v_)), [f(w), f(g), f(m), f(v)], [F32] * 3, R, C)
    return delta.reshape(shape), nm.reshape(shape), nv.reshape(shape)


def _pack_small(d):
    return jnp.concatenate([d[n].reshape(-1, LANES) for n in SMALL], axis=0)


def _unpack_small(flat, like):
    out, r = {}, 0
    for n in SMALL:
        k = like[n].size // LANES
        out[n] = flat[r:r + k].reshape(like[n].shape)
        r += k
    return out


def _small_update(gall, w, m, v):
    M = w.shape[0]
    tr = _pick(M, (552, 276, 184, 96, 48, 24, 8))

    def body(*refs):
        g = refs[0][...]
        for d in range(1, 8):
            g = g + refs[d][...]
        delta, nm, nv = _adamw_math(refs[8][...], g, refs[9][...], refs[10][...])
        refs[11][...] = g
        refs[12][...] = delta
        refs[13][...] = nm
        refs[14][...] = nv

    blk = pl.BlockSpec((tr, LANES), lambda i: (i, 0))
    in_specs = [pl.BlockSpec((tr, LANES), lambda i, d=d: (d * (M // tr) + i, 0)) for d in range(8)] + [blk] * 3
    return pl.pallas_call(
        body, name="small_update", grid=(M // tr,), in_specs=in_specs, out_specs=[blk] * 4,
        out_shape=[jax.ShapeDtypeStruct((M, LANES), F32)] * 4,
        compiler_params=pltpu.CompilerParams(dimension_semantics=("parallel",), vmem_limit_bytes=_vmem_limit(15 * tr * LANES * 4)),
    )(*([gall] * 8), w, m, v)


def _step(x, p, target, w, m, v):
    L = p.shape[0]
    x_i, y_i, c, s = _place()
    shapes = {}
    for n in BIG:
        _, K, N = w[n].shape
        shapes[n] = (4 * K, N) if n in ROW_SHARDED else (K, 4 * N)
    def pieces_of(i):
        return {n: lax.dynamic_slice_in_dim(w[n][i], c * (w[n].shape[1] // 2), w[n].shape[1] // 2, axis=0).astype(BF16)
                for n in BIG}

    first = pieces_of(0)
    head = _gather_weights("gather_weights_0_w_in", {"w_in": first.pop("w_in")}, shapes)
    head, first = lax.optimization_barrier((head, first))
    layers = [{**head, **_gather_weights_async("gather_weights_0", first, shapes)}]
    for i in range(1, L):
        mine = pieces_of(i)
        layers.append({**_gather_weights_async(f"gather_weights_{i}_mix", {n: mine[n] for n in GRAD_GROUPS["mix"]}, shapes),
                       **_gather_weights_async(f"gather_weights_{i}_ffn", {n: mine[n] for n in GRAD_GROUPS["ffn"]}, shapes)})
    wf ={n: [layers[i][n] for i in range(L)] for n in BIG}
    small = {n: w[n] for n in SMALL}
    place = jnp.stack([c, s]).astype(jnp.int32)
    reduced = []

    def after_group(i, group, grads):
        tag = f"{i}_{group}"
        views = {n: _halves_view(n, g[None]) for n, g in grads.items()}
        if (i, group) != (0, "mix"):
            reduced.append((i, True, views, _scatter_partials_async(f"scatter_partials_{tag}", views)))
            return views
        got = _exchange_halves(f"exchange_halves_{tag}", views)
        chip_sum = {n: _shard_view(n, _chip_sum(f"chip_sum_{n}_{i}", views[n], got[n], place), 1) for n in grads}
        reduced.append((i, False, chip_sum, _scatter_chip_sums_async(f"scatter_chip_sums_{tag}", chip_sum)))
        return chip_sum

    loss_cell, dx, gsmall = _local_step(x[0], p[:, 0], target[0], wf, small, after_group)
    loss = lax.psum(jnp.sum(loss_cell), ("x", "y", "c"))
    packed = _pack_small(gsmall)
    gall = _gather_weights_async("gather_small", {SMALL_BLOCKS: packed}, {SMALL_BLOCKS: (8 * packed.shape[0], LANES)})[SMALL_BLOCKS]
    ghalf = {n: None for n in BIG}
    done = None
    for i, direct, own, parts in reduced:
        parts, _ = lax.optimization_barrier((parts, done))
        shard_sum = _shard_sum_partials if direct else _shard_sum
        for n in own:
            ghalf[n] = shard_sum(f"shard_sum_{n}_{i}", own[n], parts[n], place, n in ROW_SHARDED, i, L, ghalf[n])
        done = {n: ghalf[n] for n in own}
    gfull = _share_halves(ghalf)
    grad, delta, new_m, new_v = {}, {}, {}, {}
    for n in BIG:
        grad[n] = gfull[n].reshape(w[n].shape)
        delta[n], new_m[n], new_v[n] = _adamw(f"adamw_{n}", w[n], grad[n], m[n], v[n])
    gall, _ = lax.optimization_barrier((gall, (done, delta)))
    gsum, dsm, nms, nvs =_small_update(gall, _pack_small(small), _pack_small({n: m[n] for n in SMALL}),
                                        _pack_small({n: v[n] for n in SMALL}))
    for dst, flat in ((grad, gsum), (delta, dsm), (new_m, nms), (new_v, nvs)):
        dst.update(_unpack_small(flat, small))
    return loss, dx[None], grad, delta, new_m, new_v


def kernel(x, p, w_in, w_br_attn, w_br_sg, w_out, sg_w, sg_b, sg_ln_g, sg_ln_b, norm_mix, norm_ffn, norm_ple, norm_final, w_ff_gate, w_ff_up, w_ff_down, w_ple_gate, w_ple, loss_target, m_w_in, m_w_br_attn, m_w_br_sg, m_w_out, m_sg_w, m_sg_b, m_sg_ln_g, m_sg_ln_b, m_norm_mix, m_norm_ffn, m_norm_ple, m_norm_final, m_w_ff_gate, m_w_ff_up, m_w_ff_down, m_w_ple_gate, m_w_ple, v_w_in, v_w_br_attn, v_w_br_sg, v_w_out, v_sg_w, v_sg_b, v_sg_ln_g, v_sg_ln_b, v_norm_mix, v_norm_ffn, v_norm_ple, v_norm_final, v_w_ff_gate, v_w_ff_up, v_w_ff_down, v_w_ple_gate, v_w_ple):
    w = dict(w_in=w_in, w_br_attn=w_br_attn, w_br_sg=w_br_sg, w_out=w_out, sg_w=sg_w, sg_b=sg_b, sg_ln_g=sg_ln_g, sg_ln_b=sg_ln_b,
             norm_mix=norm_mix, norm_ffn=norm_ffn, norm_ple=norm_ple, norm_final=norm_final, w_ff_gate=w_ff_gate, w_ff_up=w_ff_up,
             w_ff_down=w_ff_down, w_ple_gate=w_ple_gate, w_ple=w_ple)
    m = dict(w_in=m_w_in, w_br_attn=m_w_br_attn, w_br_sg=m_w_br_sg, w_out=m_w_out, sg_w=m_sg_w, sg_b=m_sg_b, sg_ln_g=m_sg_ln_g,
             sg_ln_b=m_sg_ln_b, norm_mix=m_norm_mix, norm_ffn=m_norm_ffn, norm_ple=m_norm_ple, norm_final=m_norm_final,
             w_ff_gate=m_w_ff_gate, w_ff_up=m_w_ff_up, w_ff_down=m_w_ff_down, w_ple_gate=m_w_ple_gate, w_ple=m_w_ple)
    v = dict(w_in=v_w_in, w_br_attn=v_w_br_attn, w_br_sg=v_w_br_sg, w_out=v_w_out, sg_w=v_sg_w, sg_b=v_sg_b, sg_ln_g=v_sg_ln_g,
             sg_ln_b=v_sg_ln_b, norm_mix=v_norm_mix, norm_ffn=v_norm_ffn, norm_ple=v_norm_ple, norm_final=v_norm_final,
             w_ff_gate=v_w_ff_gate, w_ff_up=v_w_ff_up, w_ff_down=v_w_ff_down, w_ple_gate=v_w_ple_gate, w_ple=v_w_ple)
    loss, grad_x, grad, delta, new_m, new_v = _step(x, p, loss_target, w, m, v)
    return (loss, grad_x, *[grad[n] for n in WEIGHTS], *[delta[n] for n in WEIGHTS], *[new_m[n] for n in WEIGHTS],
            *[new_v[n] for n in WEIGHTS])
```

```python
import functools
import math

import jax
import jax.numpy as jnp
from jax import lax
from jax.experimental import pallas as pl
from jax.experimental.pallas import tpu as pltpu
from jax.experimental.pallas import tpu_sc as plsc

F32 = jnp.float32
BF16 = jnp.bfloat16
MESH = pl.DeviceIdType.MESH

HEAD_DIM = 128
ATTN_GROUPS = ((128, 1), (512, 4), (2048, 16))
N_GROUPS = 3
HEADS = 4
QKV_W = 3 * N_GROUPS * HEADS * HEAD_DIM
ATTN_W = HEADS * HEAD_DIM
SG_CHUNK = 128
SG_GROUPS = 8
SG_W = 1024
RADIUS = 64
ROPE_THETA = 10000.0
NORM_EPS = 1e-6
NEG_INF = -1e30
ADAM_LR, ADAM_B1, ADAM_B2, ADAM_EPS, ADAM_WD, ADAM_STEP = 0.001, 0.9, 0.999, 1e-08, 0.01, 10

VMEM_CAP_V7X = 56 * 1024 * 1024
LANES = 128
EW_TILE_ELEMS = 256 * 1024
MM_VMEM_BUDGET = 44 * 1024 * 1024

GATHER_COLLECTIVE_ID = 1
SCATTER_COLLECTIVE_ID = 2
PARTIALS_COLLECTIVE_ID = 3

BIG = ("w_in", "w_br_attn", "w_br_sg", "w_out", "w_ff_gate", "w_ff_up", "w_ff_down", "w_ple_gate", "w_ple")
ROW_SHARDED = ("w_out", "w_ff_down", "w_ple_gate")
SMALL_BLOCKS = "small_blocks"
GRAD_GROUPS = {"ffn": ("w_ple_gate", "w_ple", "w_ff_down", "w_ff_gate", "w_ff_up"), "mix": ("w_out", "w_br_attn", "w_br_sg", "w_in")}
SMALL = ("sg_w", "sg_b", "sg_ln_g", "sg_ln_b", "norm_mix", "norm_ffn", "norm_ple", "norm_final")
WEIGHTS = ("w_in", "w_br_attn", "w_br_sg", "w_out", "sg_w", "sg_b", "sg_ln_g", "sg_ln_b", "norm_mix", "norm_ffn",
           "norm_ple", "norm_final", "w_ff_gate", "w_ff_up", "w_ff_down", "w_ple_gate", "w_ple")


def _pick(n, prefs):
    for t in prefs:
        if n % t == 0:
            return t
    return n


def _nbytes(shape, dtype):
    return math.prod(shape) * jnp.dtype(dtype).itemsize


def _vmem_limit(block_bytes, temp_bytes=0):
    est = 2 * block_bytes + temp_bytes
    assert est <= VMEM_CAP_V7X, est
    return VMEM_CAP_V7X


def _sigmoid(x):
    return 1.0 / (1.0 + jnp.exp(-x))


_GELU_C = math.sqrt(2.0 / math.pi)


def _gelu(x):
    return 0.5 * x * (1.0 + jnp.tanh(_GELU_C * (x + 0.044715 * (x * x * x))))


def _gelu_grad(x):
    t = jnp.tanh(_GELU_C * (x + 0.044715 * (x * x * x)))
    return 0.5 * (1.0 + t) + 0.5 * x * (1.0 - t * t) * (_GELU_C * (1.0 + 3.0 * 0.044715 * (x * x)))


def _lead(arr, l, blk, idx):
    if arr.ndim == 2:
        return pl.BlockSpec(blk, idx)
    return pl.BlockSpec((None,) + blk, lambda *g: (l,) + idx(*g))


def _k_steps(prods, tm, tn, fixed_bytes):
    for nk in range(1, 129):
        if any(p["K"] % nk or (p["K"] // nk) % LANES for p in prods):
            continue
        if 2 * sum((tm + tn) * (p["K"] // nk) * 2 for p in prods) + fixed_bytes <= MM_VMEM_BUDGET:
            return nk
    raise ValueError("no contraction split fits VMEM")


def _mm(name, prods, M, N, outs, epilogue, tiles=(), rows=(), tm=1024, tn=1024):
    assert M % tm == 0 and N % tn == 0, (name, M, N, tm, tn)
    fixed = 2 * tm * tn * (sum(t["x"].dtype.itemsize for t in tiles) + sum(jnp.dtype(o["dtype"]).itemsize for o in outs))
    fixed += (len(prods) + 2) * tm * tn * 4
    nk = _k_steps(prods, tm, tn, fixed)
    in_specs, args, block_bytes = [], [], 0
    for p in prods:
        if isinstance(p["b"], (list, tuple)):
            p["b"], p["bl"] = p["b"][p["bl"]], None
        K = p["K"]
        assert K % nk == 0, (name, K, nk)
        tk = K // nk
        p["tk"] = tk
        a_off, bk_off, bn_off = p.get("a_off", 0), p.get("bk_off", 0), p.get("bn_off", 0)
        assert bn_off % tn == 0 and bk_off % tk == 0
        if p["mode"] == "nn":
            assert a_off % tk == 0
            a_spec = _lead(p["a"], p.get("al"), (tm, tk), lambda i, j, k, o=a_off // tk: (i, o + k))
            b_spec = _lead(p["b"], p.get("bl"), (tk, tn), lambda i, j, k, ok=bk_off // tk, on=bn_off // tn: (ok + k, on + j))
        elif p["mode"] == "nt":
            assert a_off % tk == 0
            a_spec = _lead(p["a"], p.get("al"), (tm, tk), lambda i, j, k, o=a_off // tk: (i, o + k))
            b_spec = _lead(p["b"], p.get("bl"), (tn, tk), lambda i, j, k, ok=bk_off // tk, on=bn_off // tn: (on + j, ok + k))
        else:
            assert a_off % tm == 0
            a_spec = _lead(p["a"], p.get("al"), (tk, tm), lambda i, j, k, o=a_off // tm: (k, o + i))
            b_spec = _lead(p["b"], p.get("bl"), (tk, tn), lambda i, j, k, on=bn_off // tn: (k, on + j))
        in_specs += [a_spec, b_spec]
        args += [p["a"], p["b"]]
        block_bytes += (tm + tn) * tk * 2
    for t in tiles:
        off = t.get("off", 0)
        assert off % tn == 0
        in_specs.append(_lead(t["x"], t.get("l"), (tm, tn), lambda i, j, k, o=off // tn: (i, o + j)))
        args.append(t["x"])
        block_bytes += tm * tn * t["x"].dtype.itemsize
    for r in rows:
        in_specs.append(pl.BlockSpec((1, tn), lambda i, j, k: (0, j)))
        args.append(r)
    out_shapes, out_specs, aliases = [], [], {}
    for o_i, o in enumerate(outs):
        off = o.get("col_off", 0)
        assert off % tn == 0
        out_shapes.append(jax.ShapeDtypeStruct(o["shape"], o["dtype"]))
        idx = lambda i, j, k, oo=off // tn: (i, oo + j)
        if len(o["shape"]) == 2:
            out_specs.append(pl.BlockSpec((tm, tn), idx))
        else:
            out_specs.append(pl.BlockSpec((None, tm, tn), lambda i, j, k, l=o["l"], f=idx: (l,) + f(i, j, k)))
        if o.get("alias") is not None:
            aliases[len(args)] = o_i
            in_specs.append(pl.BlockSpec(memory_space=pl.ANY))
            args.append(o["alias"])
        block_bytes += tm * tn * jnp.dtype(o["dtype"]).itemsize
    n_p, n_t, n_r, n_o = len(prods), len(tiles), len(rows), len(outs)
    n_alias = len(aliases)
    modes = [p["mode"] for p in prods]

    def body(*refs):
        ab = refs[: 2 * n_p]
        t_refs = refs[2 * n_p: 2 * n_p + n_t]
        r_refs = refs[2 * n_p + n_t: 2 * n_p + n_t + n_r]
        o_refs = refs[2 * n_p + n_t + n_r + n_alias: 2 * n_p + n_t + n_r + n_alias + n_o]
        acc_refs = refs[2 * n_p + n_t + n_r + n_alias + n_o:]
        dims = {"nn": (((1,), (0,)), ((), ())), "nt": (((1,), (1,)), ((), ())), "tn": (((0,), (0,)), ((), ()))}

        def part(q):
            return lax.dot_general(ab[2 * q][...], ab[2 * q + 1][...], dims[modes[q]], preferred_element_type=F32)

        def finish(accs):
            res = epilogue(accs, [t[...] for t in t_refs], [r[...] for r in r_refs])
            for o_ref, val in zip(o_refs, res, strict=True):
                o_ref[...] = val.astype(o_ref.dtype)

        if nk == 1:
            finish([part(q) for q in range(n_p)])
        else:
            k = pl.program_id(2)

            @pl.when(k == 0)
            def _():
                for q, acc in enumerate(acc_refs):
                    acc[...] = part(q)

            @pl.when(k > 0)
            def _():
                for q, acc in enumerate(acc_refs):
                    acc[...] += part(q)

            @pl.when(k == nk - 1)
            def _():
                finish([acc[...] for acc in acc_refs])

    scratch = [pltpu.VMEM((tm, tn), F32) for _ in prods] if nk > 1 else []
    temp = (n_p + 2) * tm * tn * 4
    res = pl.pallas_call(
        body, name=name, grid=(M // tm, N // tn, nk), in_specs=in_specs, out_specs=out_specs, out_shape=out_shapes,
        scratch_shapes=scratch, input_output_aliases=aliases,
        compiler_params=pltpu.CompilerParams(dimension_semantics=("parallel", "parallel", "arbitrary"),
                                             vmem_limit_bytes=_vmem_limit(block_bytes, temp)),
    )(*args)
    return res


def _first(accs, tiles, rows):
    return [accs[0]]


def _ew(name, fn, ins, outs, R, C, tr=None, tc=None):
    tc = tc or _pick(C, (2048, 1536, 1408, 1024, 896, 512, 384, 256, 128))
    tr = tr or _pick(R, [t for t in (512, 256, 128, 64, 32, 16) if t * tc <= EW_TILE_ELEMS] + [8])
    in_specs, args, bb = [], [], 0
    for x in ins:
        if isinstance(x, tuple):
            arr, l = x
            in_specs.append(pl.BlockSpec((None, tr, tc), lambda i, j, l=l: (l, i, j)))
        else:
            arr = x
            in_specs.append(pl.BlockSpec((tr, tc), lambda i, j: (i, j)))
        args.append(arr)
        bb += tr * tc * arr.dtype.itemsize
    out_shapes = [jax.ShapeDtypeStruct((R, C), d) for d in outs]
    out_specs = [pl.BlockSpec((tr, tc), lambda i, j: (i, j)) for _ in outs]
    bb += sum(tr * tc * jnp.dtype(d).itemsize for d in outs)
    n_in = len(ins)

    def body(*refs):
        res = fn(*[r[...] for r in refs[:n_in]])
        for o_ref, val in zip(refs[n_in:], res, strict=True):
            o_ref[...] = val.astype(o_ref.dtype)

    return pl.pallas_call(
        body, name=name, grid=(R // tr, C // tc), in_specs=in_specs, out_specs=out_specs, out_shape=out_shapes,
        compiler_params=pltpu.CompilerParams(dimension_semantics=("parallel", "parallel"),
                                             vmem_limit_bytes=_vmem_limit(bb, 6 * tr * tc * 4)),
    )(*args)


def _rmsnorm_fwd(name, x, g):
    S, D = x.shape
    tr = _pick(S, (256, 128, 64, 8))

    def body(x_ref, g_ref, h_ref):
        xv = x_ref[...]
        r = lax.rsqrt(jnp.mean(xv * xv, axis=-1, keepdims=True) + NORM_EPS)
        h_ref[...] = (xv * r * g_ref[...]).astype(BF16)

    return pl.pallas_call(
        body, name=name, grid=(S // tr,),
        in_specs=[pl.BlockSpec((tr, D), lambda i: (i, 0)), pl.BlockSpec((1, D), lambda i: (0, 0))],
        out_specs=pl.BlockSpec((tr, D), lambda i: (i, 0)), out_shape=jax.ShapeDtypeStruct((S, D), BF16),
        compiler_params=pltpu.CompilerParams(dimension_semantics=("parallel",),
                                             vmem_limit_bytes=_vmem_limit(tr * D * 6, 3 * tr * D * 4)),
    )(x, g)


def _rmsnorm_bwd(name, x, g, dh, dres):
    S, D = x.shape
    tr = _pick(S, (256, 128, 64, 8))

    def body(x_ref, g_ref, dh_ref, dres_ref, dx_ref, dxb_ref, dg_ref):
        xv = x_ref[...]
        dy = dh_ref[...].astype(F32)
        r = lax.rsqrt(jnp.mean(xv * xv, axis=-1, keepdims=True) + NORM_EPS)
        a = dy * g_ref[...]
        dx = dres_ref[...] + r * a - xv * (r * r * r) * jnp.mean(a * xv, axis=-1, keepdims=True)
        dx_ref[...] = dx
        dxb_ref[...] = dx.astype(BF16)
        part = jnp.sum(dy * xv * r, axis=0, keepdims=True)

        @pl.when(pl.program_id(0) == 0)
        def _():
            dg_ref[...] = part

        @pl.when(pl.program_id(0) > 0)
        def _():
            dg_ref[...] += part

    row = pl.BlockSpec((tr, D), lambda i: (i, 0))
    vec = pl.BlockSpec((1, D), lambda i: (0, 0))
    return pl.pallas_call(
        body, name=name, grid=(S // tr,), in_specs=[row, vec, row, row], out_specs=[row, row, vec],
        out_shape=[jax.ShapeDtypeStruct((S, D), F32), jax.ShapeDtypeStruct((S, D), BF16), jax.ShapeDtypeStruct((1, D), F32)],
        compiler_params=pltpu.CompilerParams(dimension_semantics=("arbitrary",),
                                             vmem_limit_bytes=_vmem_limit(tr * D * 18, 5 * tr * D * 4)),
    )(x, g, dh, dres)


def _loss_head(x, g, target):
    S, D = x.shape
    tr = _pick(S, (256, 128, 64, 8))

    def body(x_ref, g_ref, t_ref, loss_ref, dx_ref, dxb_ref, dg_ref):
        xv = x_ref[...]
        r = lax.rsqrt(jnp.mean(xv * xv, axis=-1, keepdims=True) + NORM_EPS)
        xn = xv * r
        diff = xn * g_ref[...] - t_ref[...]
        dy = diff * (1.0 / D)
        a = dy * g_ref[...]
        dx = r * a - xv * (r * r * r) * jnp.mean(a * xv, axis=-1, keepdims=True)
        dx_ref[...] = dx
        dxb_ref[...] = dx.astype(BF16)
        part = jnp.sum(dy * xn, axis=0, keepdims=True)
        cell = (lax.broadcasted_iota(jnp.int32, (8, LANES), 0) == 0) & (lax.broadcasted_iota(jnp.int32, (8, LANES), 1) == 0)
        lpart = jnp.where(cell, 0.5 * jnp.sum(jnp.mean(diff * diff, axis=-1, keepdims=True)), 0.0)

        @pl.when(pl.program_id(0) == 0)
        def _():
            dg_ref[...] = part
            loss_ref[...] = lpart

        @pl.when(pl.program_id(0) > 0)
        def _():
            dg_ref[...] += part
            loss_ref[...] += lpart

    row = pl.BlockSpec((tr, D), lambda i: (i, 0))
    vec = pl.BlockSpec((1, D), lambda i: (0, 0))
    return pl.pallas_call(
        body, name="loss_head", grid=(S // tr,), in_specs=[row, vec, row],
        out_specs=[pl.BlockSpec((8, LANES), lambda i: (0, 0)), row, row, vec],
        out_shape=[jax.ShapeDtypeStruct((8, LANES), F32), jax.ShapeDtypeStruct((S, D), F32),
                   jax.ShapeDtypeStruct((S, D), BF16), jax.ShapeDtypeStruct((1, D), F32)],
        compiler_params=pltpu.CompilerParams(dimension_semantics=("arbitrary",),
                                             vmem_limit_bytes=_vmem_limit(tr * D * 14, 6 * tr * D * 4)),
    )(x, g, target)


def _rope_tables(S):
    pos = jnp.arange(S, dtype=F32)
    inv_freq = ROPE_THETA ** (-jnp.arange(0, HEAD_DIM, 2, dtype=F32) / HEAD_DIM)
    ang = pos[:, None] * inv_freq[None, :]
    cos, sin = jnp.cos(ang), jnp.sin(ang)
    return jnp.concatenate([cos, cos], axis=-1), jnp.concatenate([-sin, sin], axis=-1)


def _rope_fwd(name, z, cosf, sinf):
    S = z.shape[0]
    tr = _pick(S, (256, 128, 64, 8))
    n_q = N_GROUPS * HEADS

    def body(z_ref, c_ref, s_ref, o_ref):
        c, s = c_ref[...], s_ref[...]
        for j in range(QKV_W // HEAD_DIM):
            t = z_ref[:, j * HEAD_DIM:(j + 1) * HEAD_DIM]
            if j < 2 * n_q:
                t = t * c + pltpu.roll(t, HEAD_DIM // 2, axis=1) * s
            if j < n_q:
                t = t * ATTN_SCALE
            o_ref[:, j * HEAD_DIM:(j + 1) * HEAD_DIM] = t.astype(BF16)

    tab = pl.BlockSpec((tr, HEAD_DIM), lambda i: (i, 0))
    return pl.pallas_call(
        body, name=name, grid=(S // tr,), in_specs=[pl.BlockSpec((tr, QKV_W), lambda i: (i, 0)), tab, tab],
        out_specs=pl.BlockSpec((tr, QKV_W), lambda i: (i, 0)), out_shape=jax.ShapeDtypeStruct((S, QKV_W), BF16),
        compiler_params=pltpu.CompilerParams(dimension_semantics=("parallel",),
                                             vmem_limit_bytes=_vmem_limit(tr * QKV_W * 6, tr * QKV_W * 4)),
    )(z, cosf, sinf)


def _rope_bwd(name, dq, dk, dv, dzuv, cosf, sinf, dz):
    S = dq.shape[0]
    tr = _pick(S, (256, 128, 64, 8))
    W3 = QKV_W // 3
    nh = W3 // HEAD_DIM
    wide = QKV_W + dzuv.shape[1]

    def body(dq_ref, dk_ref, dv_ref, uv_ref, c_ref, s_ref, dz_in, o_ref):
        c, s = c_ref[...], s_ref[...]
        for part, ref in enumerate((dq_ref, dk_ref)):
            for j in range(nh):
                t = ref[:, j * HEAD_DIM:(j + 1) * HEAD_DIM].astype(F32)
                t = t * c - pltpu.roll(t, HEAD_DIM // 2, axis=1) * s
                o_ref[:, part * W3 + j * HEAD_DIM: part * W3 + (j + 1) * HEAD_DIM] = t.astype(BF16)
        o_ref[:, 2 * W3:QKV_W] = dv_ref[...]
        o_ref[:, QKV_W:] = uv_ref[...]

    third = pl.BlockSpec((tr, W3), lambda i: (i, 0))
    tab = pl.BlockSpec((tr, HEAD_DIM), lambda i: (i, 0))
    return pl.pallas_call(
        body, name=name, grid=(S // tr,),
        in_specs=[third, third, third, pl.BlockSpec((tr, dzuv.shape[1]), lambda i: (i, 0)), tab, tab, pl.BlockSpec(memory_space=pl.ANY)],
        out_specs=pl.BlockSpec((tr, wide), lambda i: (i, 0)), out_shape=jax.ShapeDtypeStruct(dz.shape, dz.dtype),
        input_output_aliases={6: 0},
        compiler_params=pltpu.CompilerParams(dimension_semantics=("parallel",),
                                             vmem_limit_bytes=_vmem_limit(tr * wide * 4, tr * wide * 4)),
    )(dq, dk, dv, dzuv, cosf, sinf, dz)


ATTN_TQ = 256
ATTN_SCALE = HEAD_DIM ** -0.5
ATTN_PAD_MAX = RADIUS * max(d for _, d in ATTN_GROUPS)


def _band_bias(shape, q_axis, d):
    kq = lax.broadcasted_iota(jnp.int32, shape, 1 - q_axis) - lax.broadcasted_iota(jnp.int32, shape, q_axis) - RADIUS * d
    return jnp.where((jnp.abs(kq) <= RADIUS * d) & ((kq & (d - 1)) == 0), 0.0, NEG_INF).astype(F32)


def _fill_padded(dst, src, d, S):
    pad = RADIUS * d
    dst[0:pad, :] = jnp.zeros((pad, HEAD_DIM), dst.dtype)
    dst[pad:pad + S, :] = src[...]
    dst[pad + S:pad + S + pad, :] = jnp.zeros((pad, HEAD_DIM), dst.dtype)


_NT = (((1,), (1,)), ((), ()))


def _attn_fwd(name, qkv):
    S = qkv.shape[0]
    T = ATTN_TQ
    nq = N_GROUPS * HEADS
    widths = [T + 2 * RADIUS * d for _, d in ATTN_GROUPS]

    def body(*refs):
        q_refs, k_refs, v_refs = refs[0:3], refs[3:6], refs[6:9]
        o_ref, lc_ref = refs[9:11]
        kp, vp, bias = refs[11:14], refs[14:17], refs[17:20]
        i0 = pl.multiple_of(pl.program_id(1) * T, T)

        @pl.when(pl.program_id(1) == 0)
        def _():
            for g, (_, d) in enumerate(ATTN_GROUPS):
                _fill_padded(kp[g], k_refs[g], d, S)
                _fill_padded(vp[g], v_refs[g], d, S)
                bias[g][...] = _band_bias((T, widths[g]), 0, d)

        m = jnp.full((T, 1), NEG_INF, F32)
        l = jnp.zeros((T, 1), F32)
        acc = jnp.zeros((T, HEAD_DIM), F32)
        for g, (_, d) in enumerate(ATTN_GROUPS):
            W = widths[g]
            kw = kp[g][pl.ds(i0, W), :]
            vw = vp[g][pl.ds(i0, W), :]
            key = i0 - RADIUS * d + lax.broadcasted_iota(jnp.int32, (1, W), 1)
            in_seq = jnp.where((key >= 0) & (key < S), 0.0, NEG_INF).astype(F32)
            s = lax.dot_general(q_refs[g][...], kw, _NT, preferred_element_type=F32) + bias[g][...] + in_seq
            m_new = jnp.maximum(m, jnp.max(s, axis=1, keepdims=True))
            alpha = jnp.exp(m - m_new)
            p = jnp.exp(s - m_new)
            l = l * alpha + jnp.sum(p, axis=1, keepdims=True)
            acc = acc * alpha + jnp.dot(p.astype(BF16), vw, preferred_element_type=F32)
            m = m_new
        o_ref[...] = (acc / l).astype(BF16)
        lc_ref[...] = m + jnp.log(l)

    in_specs = [pl.BlockSpec((T, HEAD_DIM), lambda h, i, g=g: (i, g * HEADS + h)) for g in range(N_GROUPS)]
    in_specs += [pl.BlockSpec((S, HEAD_DIM), lambda h, i, g=g: (0, nq + g * HEADS + h)) for g in range(N_GROUPS)]
    in_specs += [pl.BlockSpec((S, HEAD_DIM), lambda h, i, g=g: (0, 2 * nq + g * HEADS + h)) for g in range(N_GROUPS)]
    padded = [pltpu.VMEM((S + 2 * RADIUS * d, HEAD_DIM), BF16) for _, d in ATTN_GROUPS]
    scratch = padded + padded + [pltpu.VMEM((T, W), F32) for W in widths]
    scratch_bytes = sum(2 * (S + 2 * RADIUS * d) * HEAD_DIM * 2 for _, d in ATTN_GROUPS) + sum(T * W * 4 for W in widths)
    return pl.pallas_call(
        body, name=name, grid=(HEADS, S // T), in_specs=in_specs,
        out_specs=[pl.BlockSpec((T, HEAD_DIM), lambda h, i: (i, h)), pl.BlockSpec((None, T, 1), lambda h, i: (h, i, 0))],
        out_shape=[jax.ShapeDtypeStruct((S, ATTN_W), BF16), jax.ShapeDtypeStruct((HEADS, S, 1), F32)],
        scratch_shapes=scratch,
        compiler_params=pltpu.CompilerParams(dimension_semantics=("parallel", "arbitrary"),
                                             vmem_limit_bytes=_vmem_limit(6 * S * HEAD_DIM * 2 + 8 * T * HEAD_DIM * 4,
                                                                          scratch_bytes + 4 * T * widths[-1] * 4)),
    )(*([qkv] * 9))


_TN = (((0,), (0,)), ((), ()))


def _attn_bwd(name, qkv, attn, dattn, lse_c):
    S = qkv.shape[0]
    T = ATTN_TQ
    nq = N_GROUPS * HEADS
    W3 = QKV_W // 3
    n_i = S // T
    wmax = T + 2 * ATTN_PAD_MAX
    s_pad = S + 2 * ATTN_PAD_MAX

    def body(q_ref, k_ref, v_ref, o_ref, do_ref, lc_ref, dq_ref, dk_ref, dv_ref, kp, vp, dk_acc, dv_acc, bias):
        g_id, i = pl.program_id(1), pl.program_id(2)
        i0 = pl.multiple_of(i * T, T)
        q, do = q_ref[...], do_ref[...]
        delta = jnp.sum(do.astype(F32) * o_ref[...].astype(F32), axis=1, keepdims=True)
        lse = lc_ref[...]

        def group(d):
            W, pad = T + 2 * RADIUS * d, RADIUS * d

            @pl.when(i == 0)
            def _():
                _fill_padded(kp, k_ref, d, S)
                _fill_padded(vp, v_ref, d, S)
                dk_acc[...] = jnp.zeros_like(dk_acc)
                dv_acc[...] = jnp.zeros_like(dv_acc)
                bias[:, 0:W] = _band_bias((T, W), 0, d)

            kw = kp[pl.ds(i0, W), :]
            vw = vp[pl.ds(i0, W), :]
            key = i0 - pad + lax.broadcasted_iota(jnp.int32, (1, W), 1)
            in_seq = jnp.where((key >= 0) & (key < S), 0.0, NEG_INF).astype(F32)
            s = lax.dot_general(q, kw, _NT, preferred_element_type=F32) + bias[:, 0:W] + in_seq
            p = jnp.exp(s - lse)
            dp = lax.dot_general(do, vw, _NT, preferred_element_type=F32)
            ds = (p * (dp - delta)).astype(BF16)
            dq_ref[...] = (jnp.dot(ds, kw, preferred_element_type=F32) * ATTN_SCALE).astype(BF16)
            dk_acc[pl.ds(i0, W), :] += lax.dot_general(ds, q, _TN, preferred_element_type=F32)
            dv_acc[pl.ds(i0, W), :] += lax.dot_general(p.astype(BF16), do, _TN, preferred_element_type=F32)

            @pl.when(i == n_i - 1)
            def _():
                dk_ref[...] = dk_acc[pad:pad + S, :].astype(BF16)
                dv_ref[...] = dv_acc[pad:pad + S, :].astype(BF16)

        for g, (_, d) in enumerate(ATTN_GROUPS):
            pl.when(g_id == g)(functools.partial(group, d))

    tile = lambda off: pl.BlockSpec((T, HEAD_DIM), lambda h, g, i: (i, off + g * HEADS + h))
    full = lambda off: pl.BlockSpec((S, HEAD_DIM), lambda h, g, i: (0, off + g * HEADS + h))
    headt = pl.BlockSpec((T, HEAD_DIM), lambda h, g, i: (i, h))
    scratch_bytes = 2 * s_pad * HEAD_DIM * (2 + 4) + T * wmax * 4
    return pl.pallas_call(
        body, name=name, grid=(HEADS, N_GROUPS, n_i),
        in_specs=[tile(0), full(nq), full(2 * nq), headt, headt, pl.BlockSpec((None, T, 1), lambda h, g, i: (h, i, 0))],
        out_specs=[tile(0), full(0), full(0)],
        out_shape=[jax.ShapeDtypeStruct((S, W3), BF16)] * 3,
        scratch_shapes=[pltpu.VMEM((s_pad, HEAD_DIM), BF16), pltpu.VMEM((s_pad, HEAD_DIM), BF16),
                        pltpu.VMEM((s_pad, HEAD_DIM), F32), pltpu.VMEM((s_pad, HEAD_DIM), F32), pltpu.VMEM((T, wmax), F32)],
        compiler_params=pltpu.CompilerParams(dimension_semantics=("parallel", "arbitrary", "arbitrary"),
                                             vmem_limit_bytes=_vmem_limit(4 * S * HEAD_DIM * 2 + 8 * T * HEAD_DIM * 4,
                                                                          scratch_bytes + 5 * T * wmax * 4)),
    )(qkv, qkv, qkv, attn, dattn, lse_c)


def _sg_parts(u, v, lng, lnb):
    gu = _gelu(u)
    gv = _gelu(v)
    mu = jnp.mean(gv, axis=-1, keepdims=True)
    xc = gv - mu
    rstd = lax.rsqrt(jnp.mean(xc * xc, axis=-1, keepdims=True) + NORM_EPS)
    xhat = xc * rstd
    vn = xhat * lng + lnb
    return gu, xhat, rstd, vn


def _sg_fwd(name, z, sg_w, sg_bc, lng, lnb, o_sg0):
    S = z.shape[0]
    T = SG_CHUNK
    cb = 512
    assert o_sg0 % cb == 0
    b0 = o_sg0 // cb

    def body(u0, u1, v0, v1, w_ref, b_ref, g_ref, be_ref, o_ref):
        u = jnp.concatenate([u0[...], u1[...]], axis=1)
        v = jnp.concatenate([v0[...], v1[...]], axis=1)
        gu, _, _, vn = _sg_parts(u, v, g_ref[...], be_ref[...])
        vnb = vn.astype(BF16)
        for g in range(SG_GROUPS):
            sl = slice(g * SG_CHUNK, (g + 1) * SG_CHUNK)
            mixed = jnp.dot(w_ref[g], vnb[:, sl], preferred_element_type=F32) + b_ref[g]
            o_ref[:, sl] = (gu[:, sl] * mixed).astype(BF16)

    zs = lambda k: pl.BlockSpec((T, cb), lambda i, k=k: (i, b0 + k))
    const3 = lambda shp: pl.BlockSpec(shp, lambda i: (0, 0, 0))
    vec = pl.BlockSpec((1, SG_W), lambda i: (0, 0))
    return pl.pallas_call(
        body, name=name, grid=(S // T,),
        in_specs=[zs(0), zs(1), zs(2), zs(3), const3((SG_GROUPS, SG_CHUNK, SG_CHUNK)), const3((SG_GROUPS, SG_CHUNK, 1)), vec, vec],
        out_specs=pl.BlockSpec((T, SG_W), lambda i: (i, 0)), out_shape=jax.ShapeDtypeStruct((S, SG_W), BF16),
        compiler_params=pltpu.CompilerParams(dimension_semantics=("parallel",), vmem_limit_bytes=_vmem_limit(4 * 1024 * 1024, 8 * T * SG_W * 4)),
    )(z, z, z, z, sg_w, sg_bc, lng, lnb)


def _sg_bwd(name, z, dsg, sg_w, sg_wt, sg_bc, lng, lnb, o_sg0):
    S = z.shape[0]
    T = SG_CHUNK
    cb = 512
    b0 = o_sg0 // cb

    def body(u0, u1, v0, v1, d_ref, w_ref, wt_ref, b_ref, g_ref, be_ref, dz_ref, dw_ref, db_ref, dg_ref, dbe_ref):
        i = pl.program_id(0)
        u = jnp.concatenate([u0[...], u1[...]], axis=1)
        v = jnp.concatenate([v0[...], v1[...]], axis=1)
        gu, xhat, rstd, vn = _sg_parts(u, v, g_ref[...], be_ref[...])
        vnb = vn.astype(BF16)
        dsg_v = d_ref[...].astype(F32)
        dmix = dsg_v * gu
        dmixb = dmix.astype(BF16)
        dvn_parts, mixed_parts, dw_parts, db_parts = [], [], [], []
        for g in range(SG_GROUPS):
            sl = slice(g * SG_CHUNK, (g + 1) * SG_CHUNK)
            mixed_parts.append(jnp.dot(w_ref[g], vnb[:, sl], preferred_element_type=F32) + b_ref[g])
            dvn_parts.append(jnp.dot(wt_ref[g], dmixb[:, sl], preferred_element_type=F32))
            dw_parts.append(lax.dot_general(dmixb[:, sl], vnb[:, sl], _NT, preferred_element_type=F32))
            db_parts.append(jnp.sum(dmix[:, sl], axis=1, keepdims=True))
        mixed = jnp.concatenate(mixed_parts, axis=1)
        dvn = jnp.concatenate(dvn_parts, axis=1)
        dzu = dsg_v * mixed * _gelu_grad(u)
        dxh = dvn * g_ref[...]
        dgv = rstd * (dxh - jnp.mean(dxh, axis=-1, keepdims=True) - xhat * jnp.mean(dxh * xhat, axis=-1, keepdims=True))
        dzv = dgv * _gelu_grad(v)
        dz_ref[:, :SG_W] = dzu.astype(BF16)
        dz_ref[:, SG_W:] = dzv.astype(BF16)
        dgp = jnp.sum(dvn * xhat, axis=0, keepdims=True)
        dbp = jnp.sum(dvn, axis=0, keepdims=True)

        @pl.when(i == 0)
        def _():
            for g in range(SG_GROUPS):
                dw_ref[g] = dw_parts[g]
                db_ref[g] = db_parts[g]
            dg_ref[...] = dgp
            dbe_ref[...] = dbp

        @pl.when(i > 0)
        def _():
            for g in range(SG_GROUPS):
                dw_ref[g] += dw_parts[g]
                db_ref[g] += db_parts[g]
            dg_ref[...] += dgp
            dbe_ref[...] += dbp

    zs = lambda k: pl.BlockSpec((T, cb), lambda i, k=k: (i, b0 + k))
    const3 = lambda shp: pl.BlockSpec(shp, lambda i: (0, 0, 0))
    vec = pl.BlockSpec((1, SG_W), lambda i: (0, 0))
    return pl.pallas_call(
        body, name=name, grid=(S // T,),
        in_specs=[zs(0), zs(1), zs(2), zs(3), pl.BlockSpec((T, SG_W), lambda i: (i, 0)),
                  const3((SG_GROUPS, SG_CHUNK, SG_CHUNK)), const3((SG_GROUPS, SG_CHUNK, SG_CHUNK)), const3((SG_GROUPS, SG_CHUNK, 1)),
                  vec, vec],
        out_specs=[pl.BlockSpec((T, 2 * SG_W), lambda i: (i, 0)), const3((SG_GROUPS, SG_CHUNK, SG_CHUNK)),
                   const3((SG_GROUPS, SG_CHUNK, 1)), vec, vec],
        out_shape=[jax.ShapeDtypeStruct((S, 2 * SG_W), BF16), jax.ShapeDtypeStruct((SG_GROUPS, SG_CHUNK, SG_CHUNK), F32),
                   jax.ShapeDtypeStruct((SG_GROUPS, SG_CHUNK, 1), F32), jax.ShapeDtypeStruct((1, SG_W), F32),
                   jax.ShapeDtypeStruct((1, SG_W), F32)],
        compiler_params=pltpu.CompilerParams(dimension_semantics=("arbitrary",),
                                             vmem_limit_bytes=_vmem_limit(6 * 1024 * 1024, 16 * T * SG_W * 4)),
    )(z, z, z, z, dsg, sg_w, sg_wt, sg_bc, lng, lnb)


def _gate_bwd(name, z, dmerged, y_attn, y_sg, o_g0, in_w):
    S, D = dmerged.shape
    tr = _pick(S, (512, 256, 128, 8))
    cb = _pick(D, (512, 256, 128))
    assert o_g0 % cb == 0
    nd = D // cb
    b0 = o_g0 // cb

    def body(z_ref, dm_ref, ya_ref, ys_ref, dz_ref, dy_ref):
        jj = pl.program_id(1)
        gate = _sigmoid(z_ref[...])
        dm = dm_ref[...].astype(F32)
        y = jnp.where(jj < nd, ya_ref[...], ys_ref[...]).astype(F32)
        dz_ref[...] = (dm * y * gate * (1.0 - gate)).astype(BF16)
        dy_ref[...] = (dm * gate).astype(BF16)

    half = pl.BlockSpec((tr, cb), lambda i, jj: (i, jj % nd))
    return pl.pallas_call(
        body, name=name, grid=(S // tr, 2 * nd),
        in_specs=[pl.BlockSpec((tr, cb), lambda i, jj: (i, b0 + jj)), half, half, half],
        out_specs=[pl.BlockSpec((tr, cb), lambda i, jj: (i, b0 + jj)), pl.BlockSpec((tr, cb), lambda i, jj: (i, jj))],
        out_shape=[jax.ShapeDtypeStruct((S, in_w), BF16), jax.ShapeDtypeStruct((S, 2 * D), BF16)],
        compiler_params=pltpu.CompilerParams(dimension_semantics=("parallel", "arbitrary"),
                                             vmem_limit_bytes=_vmem_limit(tr * cb * 14, 6 * tr * cb * 4)),
    )(z, dmerged, y_attn, y_sg)


def _row(v):
    return v.reshape(1, -1)


def _local_step(x, p, target, wf, small, after_group):
    S, D = x.shape
    L = p.shape[0]
    in_w = wf["w_in"][0].shape[1]
    ff = wf["w_ff_gate"][0].shape[1]
    ple = p.shape[2]
    o_sg0, o_g0 = QKV_W, QKV_W + 2 * SG_W
    cosf, sinf = _rope_tables(S)
    pb = p.astype(BF16)
    tmb = _pick(S, (1024, 512, 256))
    tn_in = _pick(in_w, (768, 1024, 512))
    tn_d = _pick(D, (1024, 512, 256))
    tn_g = _pick(D, (512, 256))
    tn_ff = _pick(ff, (512, 256))

    saved = []
    xs = x
    for i in range(L):
        sv = {"x0": xs}
        h = _rmsnorm_fwd(f"norm_mix_{i}", xs, _row(small["norm_mix"][i]))
        (z,) = _mm(f"in_proj_{i}", [dict(a=h, b=wf["w_in"], bl=i, mode="nn", K=D)], S, in_w,
                   [dict(shape=(S, in_w), dtype=F32)], _first, tm=tmb, tn=tn_in)
        qkv = _rope_fwd(f"rope_{i}", z, cosf, sinf)
        attn, lse_c = _attn_fwd(f"attn_{i}", qkv)
        sgw = small["sg_w"][i].astype(BF16)
        sgbc = small["sg_b"][i].reshape(SG_GROUPS, SG_CHUNK, 1)
        sg = _sg_fwd(f"sgu_{i}", z, sgw, sgbc, _row(small["sg_ln_g"][i]), _row(small["sg_ln_b"][i]), o_sg0)

        def merge(accs, tiles, rows):
            ya, ys = accs[0].astype(BF16), accs[1].astype(BF16)
            g0, g1 = _sigmoid(tiles[0]), _sigmoid(tiles[1])
            return [ya, ys, g0 * ya.astype(F32) + g1 * ys.astype(F32)]

        y_attn, y_sg, merged = _mm(
            f"branches_{i}",
            [dict(a=attn, b=wf["w_br_attn"], bl=i, mode="nn", K=ATTN_W), dict(a=sg, b=wf["w_br_sg"], bl=i, mode="nn", K=SG_W)],
            S, D, [dict(shape=(S, D), dtype=BF16)] * 3, merge,
            tiles=[dict(x=z, off=o_g0), dict(x=z, off=o_g0 + D)], tm=tmb, tn=tn_g)
        (x1,) = _mm(f"out_proj_{i}", [dict(a=merged, b=wf["w_out"], bl=i, mode="nn", K=D)], S, D,
                    [dict(shape=(S, D), dtype=F32)], lambda a, t, r: [t[0] + a[0]], tiles=[dict(x=xs)], tm=tmb, tn=tn_d)
        h2 = _rmsnorm_fwd(f"norm_ffn_{i}", x1, _row(small["norm_ffn"][i]))

        def swiglu(accs, tiles, rows):
            fg = accs[0].astype(BF16).astype(F32)
            fu = accs[1].astype(BF16).astype(F32)
            return [fg, fu, fg * _sigmoid(fg) * fu]

        ffg, ffu, act = _mm(
            f"ff_in_{i}",
            [dict(a=h2, b=wf["w_ff_gate"], bl=i, mode="nn", K=D), dict(a=h2, b=wf["w_ff_up"], bl=i, mode="nn", K=D)],
            S, ff, [dict(shape=(S, ff), dtype=BF16)] * 3, swiglu, tm=tmb, tn=tn_ff)
        (x2,) = _mm(f"ff_out_{i}", [dict(a=act, b=wf["w_ff_down"], bl=i, mode="nn", K=ff)], S, D,
                    [dict(shape=(S, D), dtype=F32)], lambda a, t, r: [t[0] + a[0]], tiles=[dict(x=x1)], tm=tmb, tn=tn_d)
        h3 = _rmsnorm_fwd(f"norm_ple_{i}", x2, _row(small["norm_ple"][i]))

        def ple_mix(accs, tiles, rows):
            gp = _sigmoid(accs[0]).astype(BF16)
            pe = accs[1].astype(BF16)
            return [tiles[0] + gp.astype(F32) * pe.astype(F32), gp, pe]

        x3, gp, pe = _mm(
            f"ple_{i}",
            [dict(a=h3, b=wf["w_ple_gate"], bl=i, mode="nn", K=D), dict(a=pb, al=i, b=wf["w_ple"], bl=i, mode="nn", K=ple)],
            S, D, [dict(shape=(S, D), dtype=F32), dict(shape=(S, D), dtype=BF16), dict(shape=(S, D), dtype=BF16)], ple_mix,
            tiles=[dict(x=x2)], tm=tmb, tn=tn_g)
        sv.update(h=h, z=z, qkv=qkv, attn=attn, lse_c=lse_c, sg=sg, y_attn=y_attn, y_sg=y_sg, merged=merged,
                  x1=x1, h2=h2, ffg=ffg, ffu=ffu, act=act, x2=x2, h3=h3, gp=gp, pe=pe, sgw=sgw, sgbc=sgbc)
        saved.append(sv)
        xs = x3

    loss_cell, dx, dxb, dg_final = _loss_head(xs, _row(small["norm_final"]), target)

    gw = {n: [None] * L for n in BIG}
    gs = {n: [None] * L for n in SMALL if n != "norm_final"}

    def dw(n, i, a, a_off, b, bn_off, K_rows, N_cols, tm, tn):
        (gw[n][i],) = _mm(f"d_{n}_{i}", [dict(a=a, b=b, mode="tn", K=S, a_off=a_off, bn_off=bn_off)], K_rows, N_cols,
                          [dict(shape=(K_rows, N_cols), dtype=BF16)], _first, tm=tm, tn=tn)

    for i in reversed(range(L)):
        sv = saved[i]
        dpre, dpe = _ew(f"ple_gate_bwd_{i}",
                        lambda d, g, e: [d * e.astype(F32) * g.astype(F32) * (1.0 - g.astype(F32)), d * g.astype(F32)],
                        [dx, sv["gp"], sv["pe"]], [BF16, BF16], S, D)
        (dh3,) = _mm(f"d_h3_{i}", [dict(a=dpre, b=wf["w_ple_gate"], bl=i, mode="nt", K=D)], S, D,
                     [dict(shape=(S, D), dtype=BF16)], _first, tm=tmb, tn=tn_d)
        dw("w_ple_gate", i, sv["h3"], 0, dpre, 0, D, D, tn_d, tn_d)
        dw("w_ple", i, pb[i], 0, dpe, 0, ple, D, _pick(ple, (256, 128)), _pick(D, (2048, 1024, 512, 256)))
        dx, dxb, gs["norm_ple"][i] = _rmsnorm_bwd(f"norm_ple_bwd_{i}", sv["x2"], _row(small["norm_ple"][i]), dh3, dx)
        def swiglu_bwd(accs, tiles, rows):
            da = accs[0].astype(BF16).astype(F32)
            fg, fu = tiles[0].astype(F32), tiles[1].astype(F32)
            sg_ = _sigmoid(fg)
            return [da * fu * (sg_ * (1.0 + fg * (1.0 - sg_))), da * (fg * sg_)]

        dffg, dffu = _mm(f"d_act_{i}", [dict(a=dxb, b=wf["w_ff_down"], bl=i, mode="nt", K=D)], S, ff,
                         [dict(shape=(S, ff), dtype=BF16)] * 2, swiglu_bwd, tiles=[dict(x=sv["ffg"]), dict(x=sv["ffu"])],
                         tm=tmb, tn=tn_ff)
        dw("w_ff_down", i, sv["act"], 0, dxb, 0, ff, D, tn_ff, _pick(D, (2048, 1024, 512, 256)))
        dw("w_ff_gate", i, sv["h2"], 0, dffg, 0, D, ff, _pick(D, (2048, 1024, 512, 256)), tn_ff)
        dw("w_ff_up", i, sv["h2"], 0, dffu, 0, D, ff, _pick(D, (2048, 1024, 512, 256)), tn_ff)
        (dffg, dffu), _ = lax.optimization_barrier(((dffg, dffu), after_group(i, "ffn", {n: gw[n][i] for n in GRAD_GROUPS["ffn"]})))
        (dh2,) = _mm(f"d_h2_{i}", [dict(a=dffg, b=wf["w_ff_gate"], bl=i, mode="nt", K=ff),
                                   dict(a=dffu, b=wf["w_ff_up"], bl=i, mode="nt", K=ff)], S, D,
                     [dict(shape=(S, D), dtype=BF16)], lambda a, t, r: [a[0] + a[1]], tm=tmb, tn=tn_d)
        dx, dxb, gs["norm_ffn"][i] = _rmsnorm_bwd(f"norm_ffn_bwd_{i}", sv["x1"], _row(small["norm_ffn"][i]), dh2, dx)
        (dmerged,) = _mm(f"d_merged_{i}", [dict(a=dxb, b=wf["w_out"], bl=i, mode="nt", K=D)], S, D,
                         [dict(shape=(S, D), dtype=BF16)], _first, tm=tmb, tn=tn_d)
        dw("w_out", i, sv["merged"], 0, dxb, 0, D, D, tn_d, tn_d)
        dz, dy = _gate_bwd(f"gate_bwd_{i}", sv["z"], dmerged, sv["y_attn"], sv["y_sg"], o_g0, in_w)
        (dattn,) = _mm(f"d_attn_{i}", [dict(a=dy, b=wf["w_br_attn"], bl=i, mode="nt", K=D)], S, ATTN_W,
                       [dict(shape=(S, ATTN_W), dtype=BF16)], _first, tm=tmb, tn=ATTN_W)
        (dsg,) = _mm(f"d_sg_{i}", [dict(a=dy, a_off=D, b=wf["w_br_sg"], bl=i, mode="nt", K=D)], S, SG_W,
                     [dict(shape=(S, SG_W), dtype=BF16)], _first, tm=tmb, tn=SG_W)
        dw("w_br_attn", i, sv["attn"], 0, dy, 0, ATTN_W, D, ATTN_W, _pick(D, (2048, 1024, 512, 256)))
        dw("w_br_sg", i, sv["sg"], 0, dy, D, SG_W, D, SG_W, _pick(D, (1024, 512, 256)))
        sgwt = jnp.swapaxes(small["sg_w"][i], 1, 2).astype(BF16)
        dzuv, gs["sg_w"][i], dsgb, dlg, dlb = _sg_bwd(f"sgu_bwd_{i}", sv["z"], dsg, sv["sgw"], sgwt, sv["sgbc"],
                                                      _row(small["sg_ln_g"][i]), _row(small["sg_ln_b"][i]), o_sg0)
        gs["sg_b"][i], gs["sg_ln_g"][i], gs["sg_ln_b"][i] = dsgb.reshape(SG_GROUPS, SG_CHUNK), dlg[0], dlb[0]
        dq, dk, dv = _attn_bwd(f"attn_bwd_{i}", sv["qkv"], sv["attn"], dattn, sv["lse_c"])
        dz = _rope_bwd(f"rope_bwd_{i}", dq, dk, dv, dzuv, cosf, sinf, dz)
        dw("w_in", i, sv["h"], 0, dz, 0, D, in_w, tn_d, tn_in)
        dz, _ = lax.optimization_barrier((dz, after_group(i, "mix", {n: gw[n][i] for n in GRAD_GROUPS["mix"]})))
        (dh,) = _mm(f"d_h_{i}", [dict(a=dz, b=wf["w_in"], bl=i, mode="nt", K=in_w)], S, D,
                    [dict(shape=(S, D), dtype=BF16)], _first, tm=tmb, tn=tn_d)
        dx, dxb, gs["norm_mix"][i] = _rmsnorm_bwd(f"norm_mix_bwd_{i}", sv["x0"], _row(small["norm_mix"][i]), dh, dx)

    gsmall ={n: jnp.stack([jnp.reshape(v, small[n].shape[1:]) for v in gs[n]]) for n in gs}
    gsmall["norm_final"] = dg_final[0]
    return loss_cell, dx, gsmall


def _place():
    x, y, c = lax.axis_index("x"), lax.axis_index("y"), lax.axis_index("c")
    return x, y, c, 2 * x + y


def _chip_of(s):
    return s // 2, s % 2


def _aligned(v, m):
    return v if isinstance(v, int) else pl.multiple_of(v, m)


def _piece(name, shape, s, c):
    K, N = shape
    if name in ROW_SHARDED or name == SMALL_BLOCKS:
        ks = K // 4
        return s * ks + c * (ks // 2), ks // 2, 0, N
    ns = N // 4
    return c * (K // 2), K // 2, s * ns, ns


def _handshake(peers):
    barrier = pltpu.get_barrier_semaphore()
    for peer in peers:
        pl.semaphore_signal(barrier, inc=1, device_id=peer, device_id_type=MESH)
    pl.semaphore_wait(barrier, len(peers))


def _gather_body(names, shapes, src, dst, send_sems, recv_sems, local_sems):
    n_w = len(names)
    x, y, c, s = _place()
    sib = (x, y, 1 - c)
    rel = [1, 2, 3]

    def where(w, ps, pc):
        r0, nr, c0, nc = _piece(names[w], shapes[names[w]], ps, pc)
        return dst[w].at[pl.ds(_aligned(r0, 16), nr), pl.ds(_aligned(c0, LANES), nc)]

    def copy(w, k, ps, pc, to, from_src=False):
        return pltpu.make_async_remote_copy(
            src_ref=src[w] if from_src else where(w, ps, pc), dst_ref=where(w, ps, pc),
            send_sem=send_sems.at[w, k], recv_sem=recv_sems.at[w, k], device_id=to, device_id_type=MESH)

    mine, first, passed = [], [], []
    for w in range(n_w):
        cp = pltpu.make_async_copy(src[w], where(w, s, c), local_sems.at[w])
        cp.start()
        mine.append(cp)
        first.append(copy(w, 0, s, c, sib, from_src=True))
        for j in rel:
            first.append(copy(w, j, s, c, (*_chip_of(s ^ j), c), from_src=True))
    for cp in first:
        cp.start()
    for w in range(n_w):
        for j in rel:
            copy(w, j, s ^ j, c, sib).wait_recv()
            fw = copy(w, 3 + j, s ^ j, c, sib)
            fw.start()
            passed.append(fw)
    for w in range(n_w):
        copy(w, 0, s, 1 - c, sib).wait_recv()
        for j in rel:
            copy(w, 3 + j, s ^ j, 1 - c, sib).wait_recv()
    for cp in first + passed:
        cp.wait_send()
    for cp in mine:
        cp.wait()


def _gather_sems(n_w):
    return (pltpu.SemaphoreType.DMA((n_w, 7)), pltpu.SemaphoreType.DMA((n_w, 7)), pltpu.SemaphoreType.DMA((n_w,)))


def _gather_peers():
    x, y, c, s = _place()
    return [(x, y, 1 - c)] + [(*_chip_of(s ^ j), c) for j in (1, 2, 3)]


def _gather_weights(name, pieces, shapes):
    names = list(pieces)
    n_w = len(names)

    def body(*refs):
        _gather_body(names, shapes, refs[:n_w], refs[n_w:2 * n_w], *refs[2 * n_w:])

    anyspec = pl.BlockSpec(memory_space=pl.ANY)
    out = pl.pallas_call(
        body, name=name, in_specs=[anyspec] * n_w, out_specs=[anyspec] * n_w,
        out_shape=[jax.ShapeDtypeStruct(tuple(shapes[n]), pieces[n].dtype) for n in names], scratch_shapes=list(_gather_sems(n_w)),
    )(*[pieces[n] for n in names])
    return dict(zip(names, out))


def _gather_weights_async(name, pieces, shapes):
    names = list(pieces)
    n_w = len(names)
    src = [jax.new_ref(pieces[n], memory_space=pltpu.MemorySpace.HBM) for n in names]
    dst = [jax.empty_ref(jax.ShapeDtypeStruct(tuple(shapes[n]), pieces[n].dtype), memory_space=pltpu.MemorySpace.HBM)
           for n in names]

    @pl.kernel(mesh=plsc.ScalarSubcoreMesh(axis_name="seq", num_cores=1), name=name, scratch_types=_gather_sems(n_w),
               compiler_params=pltpu.CompilerParams(collective_id=GATHER_COLLECTIVE_ID))
    def launch(send_sems, recv_sems, local_sems):
        _handshake(_gather_peers())
        _gather_body(names, shapes, src, dst, send_sems, recv_sems, local_sems)

    launch()
    return {n: d[...] for n, d in zip(names, dst)}


def _halves_view(name, g):
    L, K, N = g.shape
    if name in ROW_SHARDED:
        return g.reshape(L * 4, 2, K // 8, N)
    return g.reshape(L, 2, K // 2, N)


def _exchange_halves(name, views):
    names = list(views)
    n_w = len(names)

    def body(*refs):
        src = refs[:n_w]
        got = refs[n_w:2 * n_w]
        send_sems, recv_sems = refs[2 * n_w:]
        x, y, c, s = _place()
        remote = [pltpu.make_async_remote_copy(src_ref=src[w].at[:, 1 - c], dst_ref=got[w], send_sem=send_sems.at[w],
                                               recv_sem=recv_sems.at[w], device_id=(x, y, 1 - c), device_id_type=MESH)
                  for w in range(n_w)]
        for cp in remote:
            cp.start()
        for cp in remote:
            cp.wait()

    anyspec = pl.BlockSpec(memory_space=pl.ANY)
    out = pl.pallas_call(
        body, name=name, in_specs=[anyspec] * n_w, out_specs=[anyspec] * n_w,
        out_shape=[jax.ShapeDtypeStruct((v.shape[0],) + v.shape[2:], BF16) for v in views.values()],
        scratch_shapes=[pltpu.SemaphoreType.DMA((n_w,)), pltpu.SemaphoreType.DMA((n_w,))],
    )(*views.values())
    return dict(zip(names, out))


def _chip_sum(name, view, got, place):
    A, _, R, C = view.shape
    tc = _pick(C, (2048, 1536, 1408, 1024, 512, 256, 128))
    tr = _pick(R, [t for t in (1024, 512, 256, 128, 64, 32, 16) if t * tc <= 4 * EW_TILE_ELEMS] + [8])

    def body(p_ref, own_ref, got_ref, o_ref):
        o_ref[...] = (own_ref[...].astype(F32) + got_ref[...].astype(F32)).astype(BF16)

    flat = pl.BlockSpec((None, tr, tc), lambda a, i, j, p: (a, i, j))
    return pl.pallas_call(
        body, name=name, out_shape=jax.ShapeDtypeStruct((A, R, C), BF16),
        grid_spec=pltpu.PrefetchScalarGridSpec(
            num_scalar_prefetch=1, grid=(A, R // tr, C // tc),
            in_specs=[pl.BlockSpec((None, None, tr, tc), lambda a, i, j, p: (a, p[0], i, j)), flat], out_specs=flat),
        compiler_params=pltpu.CompilerParams(dimension_semantics=("parallel", "parallel", "parallel"),
                                             vmem_limit_bytes=_vmem_limit(6 * tr * tc, 3 * tr * tc * 4)),
    )(place, view, got)


def _shard_view(name, ps, L):
    return ps.reshape(L, 4, *ps.shape[1:]) if name in ROW_SHARDED else ps


def _scatter_body(names, src, dst, send_sems, recv_sems):
    x, y, c, s = _place()

    def shard(w, t):
        if names[w] in ROW_SHARDED:
            return src[w].at[:, t]
        ns = src[w].shape[2] // 4
        return src[w].at[:, :, pl.ds(pl.multiple_of(t * ns, LANES), ns)]

    remote = []
    for w in range(len(names)):
        for j in (1, 2, 3):
            remote.append(pltpu.make_async_remote_copy(
                src_ref=shard(w, s ^ j), dst_ref=dst[w].at[j - 1], send_sem=send_sems.at[w, j - 1],
                recv_sem=recv_sems.at[w, j - 1], device_id=(*_chip_of(s ^ j), c), device_id_type=MESH))
    for cp in remote:
        cp.start()
    for cp in remote:
        cp.wait()


def _scatter_out_shape(name, v):
    return (3, v[0], v[2], v[3]) if name in ROW_SHARDED else (3, v[0], v[1], v[2] // 4)


def _scatter_sems(n_w):
    return (pltpu.SemaphoreType.DMA((n_w, 3)), pltpu.SemaphoreType.DMA((n_w, 3)))


def _scatter_chip_sums_async(name, psum):
    names = list(psum)
    n_w = len(names)
    src = [jax.new_ref(psum[n], memory_space=pltpu.MemorySpace.HBM) for n in names]
    dst = [jax.empty_ref(jax.ShapeDtypeStruct(_scatter_out_shape(n, psum[n].shape), BF16), memory_space=pltpu.MemorySpace.HBM)
           for n in names]

    @pl.kernel(mesh=plsc.ScalarSubcoreMesh(axis_name="seq", num_cores=1), name=name, scratch_types=_scatter_sems(n_w),
               compiler_params=pltpu.CompilerParams(collective_id=SCATTER_COLLECTIVE_ID))
    def launch(send_sems, recv_sems):
        _handshake(_gather_peers()[1:])
        _scatter_body(names, src, dst, send_sems, recv_sems)

    launch()
    return {n: d[...] for n, d in zip(names, dst)}


def _shard_sum(name, ps, parts, place, row_sharded, layer, n_layers, into):
    _, _, R, C = parts.shape
    tc = _pick(C, (2048, 1408, 1024, 896, 512, 384, 256, 128))
    tr = _pick(R, [t for t in (1024, 512, 256, 128, 64, 32, 16) if t * tc <= 2 * EW_TILE_ELEMS] + [8])

    def body(p_ref, own_ref, a_ref, b_ref, c_ref, *rest):
        o_ref = rest[-1]
        o_ref[...] = ((own_ref[...].astype(F32) + a_ref[...].astype(F32)) + b_ref[...].astype(F32)) + c_ref[...].astype(F32)

    if row_sharded:
        own_spec = pl.BlockSpec((None, None, tr, tc), lambda i, j, p: (0, p[1], i, j))
    else:
        own_spec = pl.BlockSpec((None, tr, tc), lambda i, j, p: (0, i, p[1] * (C // tc) + j))
    part = lambda k: pl.BlockSpec((None, None, tr, tc), lambda i, j, p, k=k: (k, 0, i, j))
    in_specs, args, aliases = [own_spec, part(0), part(1), part(2)], [place, ps, parts, parts, parts], {}
    if into is not None:
        in_specs.append(pl.BlockSpec(memory_space=pl.ANY))
        args.append(into)
        aliases = {5: 0}
    return pl.pallas_call(
        body, name=name, out_shape=jax.ShapeDtypeStruct((n_layers, 2, R, C), F32),
        grid_spec=pltpu.PrefetchScalarGridSpec(
            num_scalar_prefetch=1, grid=(R // tr, C // tc), in_specs=in_specs,
            out_specs=pl.BlockSpec((None, None, tr, tc), lambda i, j, p: (layer, p[0], i, j))),
        input_output_aliases=aliases,
        compiler_params=pltpu.CompilerParams(dimension_semantics=("parallel", "parallel"),
                                             vmem_limit_bytes=_vmem_limit(12 * tr * tc, 5 * tr * tc * 4)),
    )(*args)


def _all_peers():
    x, y, c, s = _place()
    return [(x, y, 1 - c)] + [(*_chip_of(s ^ j), h) for j in (1, 2, 3) for h in (0, 1)]


def _scatter_partials_async(name, views):
    names = list(views)
    n_w = len(names)
    src = [jax.new_ref(views[n], memory_space=pltpu.MemorySpace.HBM) for n in names]
    dst = [jax.empty_ref(jax.ShapeDtypeStruct(_partials_out_shape(n, views[n].shape), BF16), memory_space=pltpu.MemorySpace.HBM)
           for n in names]

    @pl.kernel(mesh=plsc.ScalarSubcoreMesh(axis_name="seq", num_cores=1), name=name, scratch_types=_partials_sems(n_w),
               compiler_params=pltpu.CompilerParams(collective_id=PARTIALS_COLLECTIVE_ID))
    def launch(send_sems, recv_sems):
        _handshake(_all_peers())
        _scatter_partials_body(names, src, dst, send_sems, recv_sems)

    launch()
    return {n: d[...] for n, d in zip(names, dst)}


def _partials_out_shape(name, v):
    return (7, 1, v[2], v[3] if name in ROW_SHARDED else v[3] // 4)


def _partials_sems(n_w):
    return (pltpu.SemaphoreType.DMA((n_w, 7)), pltpu.SemaphoreType.DMA((n_w, 7)))


def _scatter_partials_body(names, src, dst, send_sems, recv_sems):
    x, y, c, s = _place()

    def piece(w, t, h):
        if names[w] in ROW_SHARDED:
            return src[w].at[pl.ds(t, 1), h]
        ns = src[w].shape[3] // 4
        return src[w].at[:, h, :, pl.ds(pl.multiple_of(t * ns, LANES), ns)]

    sent = []
    for w in range(len(names)):
        for j in (1, 2, 3):
            for h in (0, 1):
                sent.append(pltpu.make_async_remote_copy(
                    src_ref=piece(w, s ^ j, h), dst_ref=dst[w].at[2 * (j - 1) + c], send_sem=send_sems.at[w, 2 * (j - 1) + h],
                    recv_sem=recv_sems.at[w, 2 * (j - 1) + c], device_id=(*_chip_of(s ^ j), h), device_id_type=MESH))
        sent.append(pltpu.make_async_remote_copy(
            src_ref=piece(w, s, 1 - c), dst_ref=dst[w].at[6], send_sem=send_sems.at[w, 6], recv_sem=recv_sems.at[w, 6],
            device_id=(x, y, 1 - c), device_id_type=MESH))
    for cp in sent:
        cp.start()
    for w in range(len(names)):
        for slot in range(7):
            pltpu.make_async_remote_copy(src_ref=dst[w].at[slot], dst_ref=dst[w].at[slot], send_sem=send_sems.at[w, slot],
                                         recv_sem=recv_sems.at[w, slot], device_id=(x, y, 1 - c), device_id_type=MESH).wait_recv()
    for cp in sent:
        cp.wait_send()


def _shard_sum_partials(name, view, parts, place, row_sharded, layer, n_layers, into):
    R, C = parts.shape[2:]
    tc = _pick(C, (2048, 1408, 1024, 896, 512, 384, 256, 128))
    tr = _pick(R, [t for t in (1024, 512, 256, 128, 64, 32, 16) if t * tc <= 2 * EW_TILE_ELEMS] + [8])

    def body(p_ref, own_ref, *rest):
        acc = own_ref[...].astype(F32) + rest[6][...].astype(F32)
        for k in range(6):
            acc = acc + rest[k][...].astype(F32)
        rest[-1][...] = acc

    if row_sharded:
        own_spec = pl.BlockSpec((None, None, tr, tc), lambda i, j, p: (p[1], p[0], i, j))
    else:
        own_spec = pl.BlockSpec((None, None, tr, tc), lambda i, j, p: (0, p[0], i, p[1] * (C // tc) + j))
    part = lambda k: pl.BlockSpec((None, None, tr, tc), lambda i, j, p, k=k: (k, 0, i, j))
    in_specs, args, aliases = [own_spec] + [part(k) for k in range(7)], [place, view] + [parts] * 7, {}
    if into is not None:
        in_specs.append(pl.BlockSpec(memory_space=pl.ANY))
        args.append(into)
        aliases = {9: 0}
    return pl.pallas_call(
        body, name=name, out_shape=jax.ShapeDtypeStruct((n_layers, 2, R, C), F32),
        grid_spec=pltpu.PrefetchScalarGridSpec(
            num_scalar_prefetch=1, grid=(R // tr, C // tc), in_specs=in_specs,
            out_specs=pl.BlockSpec((None, None, tr, tc), lambda i, j, p: (layer, p[0], i, j))),
        input_output_aliases=aliases,
        compiler_params=pltpu.CompilerParams(dimension_semantics=("parallel", "parallel"),
                                             vmem_limit_bytes=_vmem_limit(20 * tr * tc, 5 * tr * tc * 4)),
    )(*args)


def _share_halves(ghalf):
    names = list(ghalf)
    n_w = len(names)

    def body(*refs):
        src = refs[:n_w]
        dst = refs[n_w:2 * n_w]
        send_sems, recv_sems = refs[2 * n_w:]
        x, y, c, s = _place()
        remote = [pltpu.make_async_remote_copy(src_ref=src[w].at[:, c], dst_ref=dst[w].at[:, c], send_sem=send_sems.at[w],
                                               recv_sem=recv_sems.at[w], device_id=(x, y, 1 - c), device_id_type=MESH)
                  for w in range(n_w)]
        for cp in remote:
            cp.start()
        for cp in remote:
            cp.wait()

    anyspec = pl.BlockSpec(memory_space=pl.ANY)
    out = pl.pallas_call(
        body, name="share_halves", in_specs=[anyspec] * n_w, out_specs=[anyspec] * n_w,
        out_shape=[jax.ShapeDtypeStruct(ghalf[n].shape, F32) for n in names],
        input_output_aliases={w: w for w in range(n_w)},
        scratch_shapes=[pltpu.SemaphoreType.DMA((n_w,)), pltpu.SemaphoreType.DMA((n_w,))],
    )(*[ghalf[n] for n in names])
    return dict(zip(names, out))


def _adamw_math(w, g, m, v):
    m = ADAM_B1 * m + (1.0 - ADAM_B1) * g
    v = ADAM_B2 * v + (1.0 - ADAM_B2) * (g * g)
    m_hat = m / (1.0 - ADAM_B1 ** ADAM_STEP)
    v_hat = v / (1.0 - ADAM_B2 ** ADAM_STEP)
    delta = -ADAM_LR * (m_hat / (jnp.sqrt(v_hat) + ADAM_EPS) + ADAM_WD * w)
    return delta, m, v


def _adamw(name, w, g, m, v):
    shape = w.shape
    C = shape[-1]
    R = math.prod(shape[:-1])
    f = lambda a: a.reshape(R, C)
    res = _ew(name, lambda w_, g_, m_, v_: [g_, *_adamw_math(w_, g_, m_, v_)], [f(w), f(g), f(m), f(v)], [F32] * 4, R, C)
    return [r.reshape(shape) for r in res]


def _pack_small(d):
    return jnp.concatenate([d[n].reshape(-1, LANES) for n in SMALL], axis=0)


def _unpack_small(flat, like):
    out, r = {}, 0
    for n in SMALL:
        k = like[n].size // LANES
        out[n] = flat[r:r + k].reshape(like[n].shape)
        r += k
    return out


def _small_update(gall, w, m, v):
    M = w.shape[0]
    tr = _pick(M, (552, 276, 184, 96, 48, 24, 8))

    def body(*refs):
        g = refs[0][...]
        for d in range(1, 8):
            g = g + refs[d][...]
        delta, nm, nv = _adamw_math(refs[8][...], g, refs[9][...], refs[10][...])
        refs[11][...] = g
        refs[12][...] = delta
        refs[13][...] = nm
        refs[14][...] = nv

    blk = pl.BlockSpec((tr, LANES), lambda i: (i, 0))
    in_specs = [pl.BlockSpec((tr, LANES), lambda i, d=d: (d * (M // tr) + i, 0)) for d in range(8)] + [blk] * 3
    return pl.pallas_call(
        body, name="small_update", grid=(M // tr,), in_specs=in_specs, out_specs=[blk] * 4,
        out_shape=[jax.ShapeDtypeStruct((M, LANES), F32)] * 4,
        compiler_params=pltpu.CompilerParams(dimension_semantics=("parallel",), vmem_limit_bytes=_vmem_limit(15 * tr * LANES * 4)),
    )(*([gall] * 8), w, m, v)


def _step(x, p, target, w, m, v):
    L = p.shape[0]
    x_i, y_i, c, s = _place()
    shapes = {}
    for n in BIG:
        _, K, N = w[n].shape
        shapes[n] = (4 * K, N) if n in ROW_SHARDED else (K, 4 * N)
    def pieces_of(i):
        return {n: lax.dynamic_slice_in_dim(w[n][i], c * (w[n].shape[1] // 2), w[n].shape[1] // 2, axis=0).astype(BF16)
                for n in BIG}

    first = pieces_of(0)
    head = _gather_weights("gather_weights_0_w_in", {"w_in": first.pop("w_in")}, shapes)
    head, first = lax.optimization_barrier((head, first))
    layers = [{**head, **_gather_weights_async("gather_weights_0", first, shapes)}]
    for i in range(1, L):
        mine = pieces_of(i)
        layers.append({**_gather_weights_async(f"gather_weights_{i}_mix", {n: mine[n] for n in GRAD_GROUPS["mix"]}, shapes),
                       **_gather_weights_async(f"gather_weights_{i}_ffn", {n: mine[n] for n in GRAD_GROUPS["ffn"]}, shapes)})
    wf ={n: [layers[i][n] for i in range(L)] for n in BIG}
    small = {n: w[n] for n in SMALL}
    place = jnp.stack([c, s]).astype(jnp.int32)
    reduced = []

    def after_group(i, group, grads):
        tag = f"{i}_{group}"
        views = {n: _halves_view(n, g[None]) for n, g in grads.items()}
        if (i, group) != (0, "mix"):
            reduced.append((i, True, views, _scatter_partials_async(f"scatter_partials_{tag}", views)))
            return views
        got = _exchange_halves(f"exchange_halves_{tag}", views)
        chip_sum = {n: _shard_view(n, _chip_sum(f"chip_sum_{n}_{i}", views[n], got[n], place), 1) for n in grads}
        reduced.append((i, False, chip_sum, _scatter_chip_sums_async(f"scatter_chip_sums_{tag}", chip_sum)))
        return chip_sum

    loss_cell, dx, gsmall = _local_step(x[0], p[:, 0], target[0], wf, small, after_group)
    loss = lax.psum(jnp.sum(loss_cell), ("x", "y", "c"))
    packed = _pack_small(gsmall)
    gall = _gather_weights_async("gather_small", {SMALL_BLOCKS: packed}, {SMALL_BLOCKS: (8 * packed.shape[0], LANES)})[SMALL_BLOCKS]
    ghalf = {n: None for n in BIG}
    done = None
    for i, direct, own, parts in reduced:
        parts, _ = lax.optimization_barrier((parts, done))
        shard_sum = _shard_sum_partials if direct else _shard_sum
        for n in own:
            ghalf[n] = shard_sum(f"shard_sum_{n}_{i}", own[n], parts[n], place, n in ROW_SHARDED, i, L, ghalf[n])
        done = {n: ghalf[n] for n in own}
    gfull = _share_halves(ghalf)
    grad, delta, new_m, new_v = {}, {}, {}, {}
    for n in BIG:
        grad[n], delta[n], new_m[n], new_v[n] = _adamw(f"adamw_{n}", w[n], gfull[n].reshape(w[n].shape), m[n], v[n])
    gall, _ = lax.optimization_barrier((gall, (done, delta)))
    gsum, dsm, nms, nvs =_small_update(gall, _pack_small(small), _pack_small({n: m[n] for n in SMALL}),
                                        _pack_small({n: v[n] for n in SMALL}))
    for dst, flat in ((grad, gsum), (delta, dsm), (new_m, nms), (new_v, nvs)):
        dst.update(_unpack_small(flat, small))
    return loss, dx[None], grad, delta, new_m, new_v


def kernel(x, p, w_in, w_br_attn, w_br_sg, w_out, sg_w, sg_b, sg_ln_g, sg_ln_b, norm_mix, norm_ffn, norm_ple, norm_final, w_ff_gate, w_ff_up, w_ff_down, w_ple_gate, w_ple, loss_target, m_w_in, m_w_br_attn, m_w_br_sg, m_w_out, m_sg_w, m_sg_b, m_sg_ln_g, m_sg_ln_b, m_norm_mix, m_norm_ffn, m_norm_ple, m_norm_final, m_w_ff_gate, m_w_ff_up, m_w_ff_down, m_w_ple_gate, m_w_ple, v_w_in, v_w_br_attn, v_w_br_sg, v_w_out, v_sg_w, v_sg_b, v_sg_ln_g, v_sg_ln_b, v_norm_mix, v_norm_ffn, v_norm_ple, v_norm_final, v_w_ff_gate, v_w_ff_up, v_w_ff_down, v_w_ple_gate, v_w_ple):
    w = dict(w_in=w_in, w_br_attn=w_br_attn, w_br_sg=w_br_sg, w_out=w_out, sg_w=sg_w, sg_b=sg_b, sg_ln_g=sg_ln_g, sg_ln_b=sg_ln_b,
             norm_mix=norm_mix, norm_ffn=norm_ffn, norm_ple=norm_ple, norm_final=norm_final, w_ff_gate=w_ff_gate, w_ff_up=w_ff_up,
             w_ff_down=w_ff_down, w_ple_gate=w_ple_gate, w_ple=w_ple)
    m = dict(w_in=m_w_in, w_br_attn=m_w_br_attn, w_br_sg=m_w_br_sg, w_out=m_w_out, sg_w=m_sg_w, sg_b=m_sg_b, sg_ln_g=m_sg_ln_g,
             sg_ln_b=m_sg_ln_b, norm_mix=m_norm_mix, norm_ffn=m_norm_ffn, norm_ple=m_norm_ple, norm_final=m_norm_final,
             w_ff_gate=m_w_ff_gate, w_ff_up=m_w_ff_up, w_ff_down=m_w_ff_down, w_ple_gate=m_w_ple_gate, w_ple=m_w_ple)
    v = dict(w_in=v_w_in, w_br_attn=v_w_br_attn, w_br_sg=v_w_br_sg, w_out=v_w_out, sg_w=v_sg_w, sg_b=v_sg_b, sg_ln_g=v_sg_ln_g,
             sg_ln_b=v_sg_ln_b, norm_mix=v_norm_mix, norm_ffn=v_norm_ffn, norm_ple=v_norm_ple, norm_final=v_norm_final,
             w_ff_gate=v_w_ff_gate, w_ff_up=v_w_ff_up, w_ff_down=v_w_ff_down, w_ple_gate=v_w_ple_gate, w_ple=v_w_ple)
    loss, grad_x, grad, delta, new_m, new_v = _step(x, p, loss_target, w, m, v)
    return (loss, grad_x, *[grad[n] for n in WEIGHTS], *[delta[n] for n in WEIGHTS], *[new_m[n] for n in WEIGHTS],
            *[new_v[n] for n in WEIGHTS])
```

```python
import functools
import math

import jax
import jax.numpy as jnp
from jax import lax
from jax.experimental import pallas as pl
from jax.experimental.pallas import tpu as pltpu
from jax.experimental.pallas import tpu_sc as plsc

F32 = jnp.float32
BF16 = jnp.bfloat16
MESH = pl.DeviceIdType.MESH

HEAD_DIM = 128
ATTN_GROUPS = ((128, 1), (512, 4), (2048, 16))
N_GROUPS = 3
HEADS = 4
QKV_W = 3 * N_GROUPS * HEADS * HEAD_DIM
ATTN_W = HEADS * HEAD_DIM
SG_CHUNK = 128
SG_GROUPS = 8
SG_W = 1024
RADIUS = 64
ROPE_THETA = 10000.0
NORM_EPS = 1e-6
NEG_INF = -1e30
ADAM_LR, ADAM_B1, ADAM_B2, ADAM_EPS, ADAM_WD, ADAM_STEP = 0.001, 0.9, 0.999, 1e-08, 0.01, 10

VMEM_CAP_V7X = 56 * 1024 * 1024
LANES = 128
EW_TILE_ELEMS = 256 * 1024
MM_VMEM_BUDGET = 44 * 1024 * 1024

GATHER_COLLECTIVE_ID = 1
PARTIALS_COLLECTIVE_ID = 2

BIG = ("w_in", "w_br_attn", "w_br_sg", "w_out", "w_ff_gate", "w_ff_up", "w_ff_down", "w_ple_gate", "w_ple")
ROW_SHARDED = ("w_out", "w_ff_down", "w_ple_gate")
SMALL_BLOCKS = "small_blocks"
GRAD_GROUPS = {"ffn": ("w_ple_gate", "w_ple", "w_ff_down", "w_ff_gate", "w_ff_up"), "mix": ("w_out", "w_br_attn", "w_br_sg", "w_in")}
SMALL = ("sg_w", "sg_b", "sg_ln_g", "sg_ln_b", "norm_mix", "norm_ffn", "norm_ple", "norm_final")
WEIGHTS = ("w_in", "w_br_attn", "w_br_sg", "w_out", "sg_w", "sg_b", "sg_ln_g", "sg_ln_b", "norm_mix", "norm_ffn",
           "norm_ple", "norm_final", "w_ff_gate", "w_ff_up", "w_ff_down", "w_ple_gate", "w_ple")


def _pick(n, prefs):
    for t in prefs:
        if n % t == 0:
            return t
    return n


def _nbytes(shape, dtype):
    return math.prod(shape) * jnp.dtype(dtype).itemsize


def _vmem_limit(block_bytes, temp_bytes=0):
    est = 2 * block_bytes + temp_bytes
    assert est <= VMEM_CAP_V7X, est
    return VMEM_CAP_V7X


def _sigmoid(x):
    return 1.0 / (1.0 + jnp.exp(-x))


_GELU_C = math.sqrt(2.0 / math.pi)


def _gelu(x):
    return 0.5 * x * (1.0 + jnp.tanh(_GELU_C * (x + 0.044715 * (x * x * x))))


def _gelu_grad(x):
    t = jnp.tanh(_GELU_C * (x + 0.044715 * (x * x * x)))
    return 0.5 * (1.0 + t) + 0.5 * x * (1.0 - t * t) * (_GELU_C * (1.0 + 3.0 * 0.044715 * (x * x)))


def _lead(arr, l, blk, idx):
    if arr.ndim == 2:
        return pl.BlockSpec(blk, idx)
    return pl.BlockSpec((None,) + blk, lambda *g: (l,) + idx(*g))


def _k_steps(prods, tm, tn, fixed_bytes):
    for nk in range(1, 129):
        if any(p["K"] % nk or (p["K"] // nk) % LANES for p in prods):
            continue
        if 2 * sum((tm + tn) * (p["K"] // nk) * 2 for p in prods) + fixed_bytes <= MM_VMEM_BUDGET:
            return nk
    raise ValueError("no contraction split fits VMEM")


def _mm(name, prods, M, N, outs, epilogue, tiles=(), rows=(), tm=1024, tn=1024):
    assert M % tm == 0 and N % tn == 0, (name, M, N, tm, tn)
    fixed = 2 * tm * tn * (sum(t["x"].dtype.itemsize for t in tiles) + sum(jnp.dtype(o["dtype"]).itemsize for o in outs))
    fixed += (len(prods) + 2) * tm * tn * 4
    nk = _k_steps(prods, tm, tn, fixed)
    in_specs, args, block_bytes = [], [], 0
    for p in prods:
        if isinstance(p["b"], (list, tuple)):
            p["b"], p["bl"] = p["b"][p["bl"]], None
        K = p["K"]
        assert K % nk == 0, (name, K, nk)
        tk = K // nk
        p["tk"] = tk
        a_off, bk_off, bn_off = p.get("a_off", 0), p.get("bk_off", 0), p.get("bn_off", 0)
        assert bn_off % tn == 0 and bk_off % tk == 0
        if p["mode"] == "nn":
            assert a_off % tk == 0
            a_spec = _lead(p["a"], p.get("al"), (tm, tk), lambda i, j, k, o=a_off // tk: (i, o + k))
            b_spec = _lead(p["b"], p.get("bl"), (tk, tn), lambda i, j, k, ok=bk_off // tk, on=bn_off // tn: (ok + k, on + j))
        elif p["mode"] == "nt":
            assert a_off % tk == 0
            a_spec = _lead(p["a"], p.get("al"), (tm, tk), lambda i, j, k, o=a_off // tk: (i, o + k))
            b_spec = _lead(p["b"], p.get("bl"), (tn, tk), lambda i, j, k, ok=bk_off // tk, on=bn_off // tn: (on + j, ok + k))
        else:
            assert a_off % tm == 0
            a_spec = _lead(p["a"], p.get("al"), (tk, tm), lambda i, j, k, o=a_off // tm: (k, o + i))
            b_spec = _lead(p["b"], p.get("bl"), (tk, tn), lambda i, j, k, on=bn_off // tn: (k, on + j))
        in_specs += [a_spec, b_spec]
        args += [p["a"], p["b"]]
        block_bytes += (tm + tn) * tk * 2
    for t in tiles:
        off = t.get("off", 0)
        assert off % tn == 0
        in_specs.append(_lead(t["x"], t.get("l"), (tm, tn), lambda i, j, k, o=off // tn: (i, o + j)))
        args.append(t["x"])
        block_bytes += tm * tn * t["x"].dtype.itemsize
    for r in rows:
        in_specs.append(pl.BlockSpec((1, tn), lambda i, j, k: (0, j)))
        args.append(r)
    out_shapes, out_specs, aliases = [], [], {}
    for o_i, o in enumerate(outs):
        off = o.get("col_off", 0)
        assert off % tn == 0
        out_shapes.append(jax.ShapeDtypeStruct(o["shape"], o["dtype"]))
        idx = lambda i, j, k, oo=off // tn: (i, oo + j)
        if len(o["shape"]) == 2:
            out_specs.append(pl.BlockSpec((tm, tn), idx))
        else:
            out_specs.append(pl.BlockSpec((None, tm, tn), lambda i, j, k, l=o["l"], f=idx: (l,) + f(i, j, k)))
        if o.get("alias") is not None:
            aliases[len(args)] = o_i
            in_specs.append(pl.BlockSpec(memory_space=pl.ANY))
            args.append(o["alias"])
        block_bytes += tm * tn * jnp.dtype(o["dtype"]).itemsize
    n_p, n_t, n_r, n_o = len(prods), len(tiles), len(rows), len(outs)
    n_alias = len(aliases)
    modes = [p["mode"] for p in prods]

    def body(*refs):
        ab = refs[: 2 * n_p]
        t_refs = refs[2 * n_p: 2 * n_p + n_t]
        r_refs = refs[2 * n_p + n_t: 2 * n_p + n_t + n_r]
        o_refs = refs[2 * n_p + n_t + n_r + n_alias: 2 * n_p + n_t + n_r + n_alias + n_o]
        acc_refs = refs[2 * n_p + n_t + n_r + n_alias + n_o:]
        dims = {"nn": (((1,), (0,)), ((), ())), "nt": (((1,), (1,)), ((), ())), "tn": (((0,), (0,)), ((), ()))}

        def part(q):
            return lax.dot_general(ab[2 * q][...], ab[2 * q + 1][...], dims[modes[q]], preferred_element_type=F32)

        def finish(accs):
            res = epilogue(accs, [t[...] for t in t_refs], [r[...] for r in r_refs])
            for o_ref, val in zip(o_refs, res, strict=True):
                o_ref[...] = val.astype(o_ref.dtype)

        if nk == 1:
            finish([part(q) for q in range(n_p)])
        else:
            k = pl.program_id(2)

            @pl.when(k == 0)
            def _():
                for q, acc in enumerate(acc_refs):
                    acc[...] = part(q)

            @pl.when(k > 0)
            def _():
                for q, acc in enumerate(acc_refs):
                    acc[...] += part(q)

            @pl.when(k == nk - 1)
            def _():
                finish([acc[...] for acc in acc_refs])

    scratch = [pltpu.VMEM((tm, tn), F32) for _ in prods] if nk > 1 else []
    temp = (n_p + 2) * tm * tn * 4
    res = pl.pallas_call(
        body, name=name, grid=(M // tm, N // tn, nk), in_specs=in_specs, out_specs=out_specs, out_shape=out_shapes,
        scratch_shapes=scratch, input_output_aliases=aliases,
        compiler_params=pltpu.CompilerParams(dimension_semantics=("parallel", "parallel", "arbitrary"),
                                             vmem_limit_bytes=_vmem_limit(block_bytes, temp)),
    )(*args)
    return res


def _first(accs, tiles, rows):
    return [accs[0]]


def _ew(name, fn, ins, outs, R, C, tr=None, tc=None):
    tc = tc or _pick(C, (2048, 1536, 1408, 1024, 896, 512, 384, 256, 128))
    tr = tr or _pick(R, [t for t in (512, 256, 128, 64, 32, 16) if t * tc <= EW_TILE_ELEMS] + [8])
    in_specs, args, bb = [], [], 0
    for x in ins:
        if isinstance(x, tuple):
            arr, l = x
            in_specs.append(pl.BlockSpec((None, tr, tc), lambda i, j, l=l: (l, i, j)))
        else:
            arr = x
            in_specs.append(pl.BlockSpec((tr, tc), lambda i, j: (i, j)))
        args.append(arr)
        bb += tr * tc * arr.dtype.itemsize
    out_shapes = [jax.ShapeDtypeStruct((R, C), d) for d in outs]
    out_specs = [pl.BlockSpec((tr, tc), lambda i, j: (i, j)) for _ in outs]
    bb += sum(tr * tc * jnp.dtype(d).itemsize for d in outs)
    n_in = len(ins)

    def body(*refs):
        res = fn(*[r[...] for r in refs[:n_in]])
        for o_ref, val in zip(refs[n_in:], res, strict=True):
            o_ref[...] = val.astype(o_ref.dtype)

    return pl.pallas_call(
        body, name=name, grid=(R // tr, C // tc), in_specs=in_specs, out_specs=out_specs, out_shape=out_shapes,
        compiler_params=pltpu.CompilerParams(dimension_semantics=("parallel", "parallel"),
                                             vmem_limit_bytes=_vmem_limit(bb, 6 * tr * tc * 4)),
    )(*args)


def _rmsnorm_fwd(name, x, g):
    S, D = x.shape
    tr = _pick(S, (256, 128, 64, 8))

    def body(x_ref, g_ref, h_ref):
        xv = x_ref[...]
        r = lax.rsqrt(jnp.mean(xv * xv, axis=-1, keepdims=True) + NORM_EPS)
        h_ref[...] = (xv * r * g_ref[...]).astype(BF16)

    return pl.pallas_call(
        body, name=name, grid=(S // tr,),
        in_specs=[pl.BlockSpec((tr, D), lambda i: (i, 0)), pl.BlockSpec((1, D), lambda i: (0, 0))],
        out_specs=pl.BlockSpec((tr, D), lambda i: (i, 0)), out_shape=jax.ShapeDtypeStruct((S, D), BF16),
        compiler_params=pltpu.CompilerParams(dimension_semantics=("parallel",),
                                             vmem_limit_bytes=_vmem_limit(tr * D * 6, 3 * tr * D * 4)),
    )(x, g)


def _rmsnorm_bwd(name, x, g, dh, dres):
    S, D = x.shape
    tr = _pick(S, (256, 128, 64, 8))

    def body(x_ref, g_ref, dh_ref, dres_ref, dx_ref, dxb_ref, dg_ref):
        xv = x_ref[...]
        dy = dh_ref[...].astype(F32)
        r = lax.rsqrt(jnp.mean(xv * xv, axis=-1, keepdims=True) + NORM_EPS)
        a = dy * g_ref[...]
        dx = dres_ref[...] + r * a - xv * (r * r * r) * jnp.mean(a * xv, axis=-1, keepdims=True)
        dx_ref[...] = dx
        dxb_ref[...] = dx.astype(BF16)
        part = jnp.sum(dy * xv * r, axis=0, keepdims=True)

        @pl.when(pl.program_id(0) == 0)
        def _():
            dg_ref[...] = part

        @pl.when(pl.program_id(0) > 0)
        def _():
            dg_ref[...] += part

    row = pl.BlockSpec((tr, D), lambda i: (i, 0))
    vec = pl.BlockSpec((1, D), lambda i: (0, 0))
    return pl.pallas_call(
        body, name=name, grid=(S // tr,), in_specs=[row, vec, row, row], out_specs=[row, row, vec],
        out_shape=[jax.ShapeDtypeStruct((S, D), F32), jax.ShapeDtypeStruct((S, D), BF16), jax.ShapeDtypeStruct((1, D), F32)],
        compiler_params=pltpu.CompilerParams(dimension_semantics=("arbitrary",),
                                             vmem_limit_bytes=_vmem_limit(tr * D * 18, 5 * tr * D * 4)),
    )(x, g, dh, dres)


def _loss_head(x, g, target):
    S, D = x.shape
    tr = _pick(S, (256, 128, 64, 8))

    def body(x_ref, g_ref, t_ref, loss_ref, dx_ref, dxb_ref, dg_ref):
        xv = x_ref[...]
        r = lax.rsqrt(jnp.mean(xv * xv, axis=-1, keepdims=True) + NORM_EPS)
        xn = xv * r
        diff = xn * g_ref[...] - t_ref[...]
        dy = diff * (1.0 / D)
        a = dy * g_ref[...]
        dx = r * a - xv * (r * r * r) * jnp.mean(a * xv, axis=-1, keepdims=True)
        dx_ref[...] = dx
        dxb_ref[...] = dx.astype(BF16)
        part = jnp.sum(dy * xn, axis=0, keepdims=True)
        cell = (lax.broadcasted_iota(jnp.int32, (8, LANES), 0) == 0) & (lax.broadcasted_iota(jnp.int32, (8, LANES), 1) == 0)
        lpart = jnp.where(cell, 0.5 * jnp.sum(jnp.mean(diff * diff, axis=-1, keepdims=True)), 0.0)

        @pl.when(pl.program_id(0) == 0)
        def _():
            dg_ref[...] = part
            loss_ref[...] = lpart

        @pl.when(pl.program_id(0) > 0)
        def _():
            dg_ref[...] += part
            loss_ref[...] += lpart

    row = pl.BlockSpec((tr, D), lambda i: (i, 0))
    vec = pl.BlockSpec((1, D), lambda i: (0, 0))
    return pl.pallas_call(
        body, name="loss_head", grid=(S // tr,), in_specs=[row, vec, row],
        out_specs=[pl.BlockSpec((8, LANES), lambda i: (0, 0)), row, row, vec],
        out_shape=[jax.ShapeDtypeStruct((8, LANES), F32), jax.ShapeDtypeStruct((S, D), F32),
                   jax.ShapeDtypeStruct((S, D), BF16), jax.ShapeDtypeStruct((1, D), F32)],
        compiler_params=pltpu.CompilerParams(dimension_semantics=("arbitrary",),
                                             vmem_limit_bytes=_vmem_limit(tr * D * 14, 6 * tr * D * 4)),
    )(x, g, target)


def _rope_tables(S):
    pos = jnp.arange(S, dtype=F32)
    inv_freq = ROPE_THETA ** (-jnp.arange(0, HEAD_DIM, 2, dtype=F32) / HEAD_DIM)
    ang = pos[:, None] * inv_freq[None, :]
    cos, sin = jnp.cos(ang), jnp.sin(ang)
    return jnp.concatenate([cos, cos], axis=-1), jnp.concatenate([-sin, sin], axis=-1)


def _rope_fwd(name, z, cosf, sinf):
    S = z.shape[0]
    tr = _pick(S, (256, 128, 64, 8))
    n_q = N_GROUPS * HEADS

    def body(z_ref, c_ref, s_ref, o_ref):
        c, s = c_ref[...], s_ref[...]
        for j in range(QKV_W // HEAD_DIM):
            t = z_ref[:, j * HEAD_DIM:(j + 1) * HEAD_DIM]
            if j < 2 * n_q:
                t = t * c + pltpu.roll(t, HEAD_DIM // 2, axis=1) * s
            if j < n_q:
                t = t * ATTN_SCALE
            o_ref[:, j * HEAD_DIM:(j + 1) * HEAD_DIM] = t.astype(BF16)

    tab = pl.BlockSpec((tr, HEAD_DIM), lambda i: (i, 0))
    return pl.pallas_call(
        body, name=name, grid=(S // tr,), in_specs=[pl.BlockSpec((tr, QKV_W), lambda i: (i, 0)), tab, tab],
        out_specs=pl.BlockSpec((tr, QKV_W), lambda i: (i, 0)), out_shape=jax.ShapeDtypeStruct((S, QKV_W), BF16),
        compiler_params=pltpu.CompilerParams(dimension_semantics=("parallel",),
                                             vmem_limit_bytes=_vmem_limit(tr * QKV_W * 6, tr * QKV_W * 4)),
    )(z, cosf, sinf)


def _rope_bwd(name, dq, dk, dv, dzuv, cosf, sinf, dz):
    S = dq.shape[0]
    tr = _pick(S, (256, 128, 64, 8))
    W3 = QKV_W // 3
    nh = W3 // HEAD_DIM
    wide = QKV_W + dzuv.shape[1]

    def body(dq_ref, dk_ref, dv_ref, uv_ref, c_ref, s_ref, dz_in, o_ref):
        c, s = c_ref[...], s_ref[...]
        for part, ref in enumerate((dq_ref, dk_ref)):
            for j in range(nh):
                t = ref[:, j * HEAD_DIM:(j + 1) * HEAD_DIM].astype(F32)
                t = t * c - pltpu.roll(t, HEAD_DIM // 2, axis=1) * s
                o_ref[:, part * W3 + j * HEAD_DIM: part * W3 + (j + 1) * HEAD_DIM] = t.astype(BF16)
        o_ref[:, 2 * W3:QKV_W] = dv_ref[...]
        o_ref[:, QKV_W:] = uv_ref[...]

    third = pl.BlockSpec((tr, W3), lambda i: (i, 0))
    tab = pl.BlockSpec((tr, HEAD_DIM), lambda i: (i, 0))
    return pl.pallas_call(
        body, name=name, grid=(S // tr,),
        in_specs=[third, third, third, pl.BlockSpec((tr, dzuv.shape[1]), lambda i: (i, 0)), tab, tab, pl.BlockSpec(memory_space=pl.ANY)],
        out_specs=pl.BlockSpec((tr, wide), lambda i: (i, 0)), out_shape=jax.ShapeDtypeStruct(dz.shape, dz.dtype),
        input_output_aliases={6: 0},
        compiler_params=pltpu.CompilerParams(dimension_semantics=("parallel",),
                                             vmem_limit_bytes=_vmem_limit(tr * wide * 4, tr * wide * 4)),
    )(dq, dk, dv, dzuv, cosf, sinf, dz)


ATTN_TQ = 256
ATTN_SCALE = HEAD_DIM ** -0.5
ATTN_PAD_MAX = RADIUS * max(d for _, d in ATTN_GROUPS)


def _band_bias(shape, q_axis, d):
    kq = lax.broadcasted_iota(jnp.int32, shape, 1 - q_axis) - lax.broadcasted_iota(jnp.int32, shape, q_axis) - RADIUS * d
    return jnp.where((jnp.abs(kq) <= RADIUS * d) & ((kq & (d - 1)) == 0), 0.0, NEG_INF).astype(F32)


def _fill_padded(dst, src, d, S):
    pad = RADIUS * d
    dst[0:pad, :] = jnp.zeros((pad, HEAD_DIM), dst.dtype)
    dst[pad:pad + S, :] = src[...]
    dst[pad + S:pad + S + pad, :] = jnp.zeros((pad, HEAD_DIM), dst.dtype)


_NT = (((1,), (1,)), ((), ()))


def _attn_fwd(name, qkv):
    S = qkv.shape[0]
    T = ATTN_TQ
    nq = N_GROUPS * HEADS
    widths = [T + 2 * RADIUS * d for _, d in ATTN_GROUPS]

    def body(*refs):
        q_refs, k_refs, v_refs = refs[0:3], refs[3:6], refs[6:9]
        o_ref, lc_ref = refs[9:11]
        kp, vp, bias = refs[11:14], refs[14:17], refs[17:20]
        i0 = pl.multiple_of(pl.program_id(1) * T, T)

        @pl.when(pl.program_id(1) == 0)
        def _():
            for g, (_, d) in enumerate(ATTN_GROUPS):
                _fill_padded(kp[g], k_refs[g], d, S)
                _fill_padded(vp[g], v_refs[g], d, S)
                bias[g][...] = _band_bias((T, widths[g]), 0, d)

        m = jnp.full((T, 1), NEG_INF, F32)
        l = jnp.zeros((T, 1), F32)
        acc = jnp.zeros((T, HEAD_DIM), F32)
        for g, (_, d) in enumerate(ATTN_GROUPS):
            W = widths[g]
            kw = kp[g][pl.ds(i0, W), :]
            vw = vp[g][pl.ds(i0, W), :]
            key = i0 - RADIUS * d + lax.broadcasted_iota(jnp.int32, (1, W), 1)
            in_seq = jnp.where((key >= 0) & (key < S), 0.0, NEG_INF).astype(F32)
            s = lax.dot_general(q_refs[g][...], kw, _NT, preferred_element_type=F32) + bias[g][...] + in_seq
            m_new = jnp.maximum(m, jnp.max(s, axis=1, keepdims=True))
            alpha = jnp.exp(m - m_new)
            p = jnp.exp(s - m_new)
            l = l * alpha + jnp.sum(p, axis=1, keepdims=True)
            acc = acc * alpha + jnp.dot(p.astype(BF16), vw, preferred_element_type=F32)
            m = m_new
        o_ref[...] = (acc / l).astype(BF16)
        lc_ref[...] = m + jnp.log(l)

    in_specs = [pl.BlockSpec((T, HEAD_DIM), lambda h, i, g=g: (i, g * HEADS + h)) for g in range(N_GROUPS)]
    in_specs += [pl.BlockSpec((S, HEAD_DIM), lambda h, i, g=g: (0, nq + g * HEADS + h)) for g in range(N_GROUPS)]
    in_specs += [pl.BlockSpec((S, HEAD_DIM), lambda h, i, g=g: (0, 2 * nq + g * HEADS + h)) for g in range(N_GROUPS)]
    padded = [pltpu.VMEM((S + 2 * RADIUS * d, HEAD_DIM), BF16) for _, d in ATTN_GROUPS]
    scratch = padded + padded + [pltpu.VMEM((T, W), F32) for W in widths]
    scratch_bytes = sum(2 * (S + 2 * RADIUS * d) * HEAD_DIM * 2 for _, d in ATTN_GROUPS) + sum(T * W * 4 for W in widths)
    return pl.pallas_call(
        body, name=name, grid=(HEADS, S // T), in_specs=in_specs,
        out_specs=[pl.BlockSpec((T, HEAD_DIM), lambda h, i: (i, h)), pl.BlockSpec((None, T, 1), lambda h, i: (h, i, 0))],
        out_shape=[jax.ShapeDtypeStruct((S, ATTN_W), BF16), jax.ShapeDtypeStruct((HEADS, S, 1), F32)],
        scratch_shapes=scratch,
        compiler_params=pltpu.CompilerParams(dimension_semantics=("parallel", "arbitrary"),
                                             vmem_limit_bytes=_vmem_limit(6 * S * HEAD_DIM * 2 + 8 * T * HEAD_DIM * 4,
                                                                          scratch_bytes + 4 * T * widths[-1] * 4)),
    )(*([qkv] * 9))


_TN = (((0,), (0,)), ((), ()))


def _attn_bwd(name, qkv, attn, dattn, lse_c):
    S = qkv.shape[0]
    T = ATTN_TQ
    nq = N_GROUPS * HEADS
    W3 = QKV_W // 3
    n_i = S // T
    wmax = T + 2 * ATTN_PAD_MAX
    s_pad = S + 2 * ATTN_PAD_MAX

    def body(q_ref, k_ref, v_ref, o_ref, do_ref, lc_ref, dq_ref, dk_ref, dv_ref, kp, vp, dk_acc, dv_acc, bias):
        g_id, i = pl.program_id(1), pl.program_id(2)
        i0 = pl.multiple_of(i * T, T)
        q, do = q_ref[...], do_ref[...]
        delta = jnp.sum(do.astype(F32) * o_ref[...].astype(F32), axis=1, keepdims=True)
        lse = lc_ref[...]

        def group(d):
            W, pad = T + 2 * RADIUS * d, RADIUS * d

            @pl.when(i == 0)
            def _():
                _fill_padded(kp, k_ref, d, S)
                _fill_padded(vp, v_ref, d, S)
                dk_acc[...] = jnp.zeros_like(dk_acc)
                dv_acc[...] = jnp.zeros_like(dv_acc)
                bias[:, 0:W] = _band_bias((T, W), 0, d)

            kw = kp[pl.ds(i0, W), :]
            vw = vp[pl.ds(i0, W), :]
            key = i0 - pad + lax.broadcasted_iota(jnp.int32, (1, W), 1)
            in_seq = jnp.where((key >= 0) & (key < S), 0.0, NEG_INF).astype(F32)
            s = lax.dot_general(q, kw, _NT, preferred_element_type=F32) + bias[:, 0:W] + in_seq
            p = jnp.exp(s - lse)
            dp = lax.dot_general(do, vw, _NT, preferred_element_type=F32)
            ds = (p * (dp - delta)).astype(BF16)
            dq_ref[...] = (jnp.dot(ds, kw, preferred_element_type=F32) * ATTN_SCALE).astype(BF16)
            dk_acc[pl.ds(i0, W), :] += lax.dot_general(ds, q, _TN, preferred_element_type=F32)
            dv_acc[pl.ds(i0, W), :] += lax.dot_general(p.astype(BF16), do, _TN, preferred_element_type=F32)

            @pl.when(i == n_i - 1)
            def _():
                dk_ref[...] = dk_acc[pad:pad + S, :].astype(BF16)
                dv_ref[...] = dv_acc[pad:pad + S, :].astype(BF16)

        for g, (_, d) in enumerate(ATTN_GROUPS):
            pl.when(g_id == g)(functools.partial(group, d))

    tile = lambda off: pl.BlockSpec((T, HEAD_DIM), lambda h, g, i: (i, off + g * HEADS + h))
    full = lambda off: pl.BlockSpec((S, HEAD_DIM), lambda h, g, i: (0, off + g * HEADS + h))
    headt = pl.BlockSpec((T, HEAD_DIM), lambda h, g, i: (i, h))
    scratch_bytes = 2 * s_pad * HEAD_DIM * (2 + 4) + T * wmax * 4
    return pl.pallas_call(
        body, name=name, grid=(HEADS, N_GROUPS, n_i),
        in_specs=[tile(0), full(nq), full(2 * nq), headt, headt, pl.BlockSpec((None, T, 1), lambda h, g, i: (h, i, 0))],
        out_specs=[tile(0), full(0), full(0)],
        out_shape=[jax.ShapeDtypeStruct((S, W3), BF16)] * 3,
        scratch_shapes=[pltpu.VMEM((s_pad, HEAD_DIM), BF16), pltpu.VMEM((s_pad, HEAD_DIM), BF16),
                        pltpu.VMEM((s_pad, HEAD_DIM), F32), pltpu.VMEM((s_pad, HEAD_DIM), F32), pltpu.VMEM((T, wmax), F32)],
        compiler_params=pltpu.CompilerParams(dimension_semantics=("parallel", "arbitrary", "arbitrary"),
                                             vmem_limit_bytes=_vmem_limit(4 * S * HEAD_DIM * 2 + 8 * T * HEAD_DIM * 4,
                                                                          scratch_bytes + 5 * T * wmax * 4)),
    )(qkv, qkv, qkv, attn, dattn, lse_c)


def _sg_parts(u, v, lng, lnb):
    gu = _gelu(u)
    gv = _gelu(v)
    mu = jnp.mean(gv, axis=-1, keepdims=True)
    xc = gv - mu
    rstd = lax.rsqrt(jnp.mean(xc * xc, axis=-1, keepdims=True) + NORM_EPS)
    xhat = xc * rstd
    vn = xhat * lng + lnb
    return gu, xhat, rstd, vn


def _sg_fwd(name, z, sg_w, sg_bc, lng, lnb, o_sg0):
    S = z.shape[0]
    T = SG_CHUNK
    cb = 512
    assert o_sg0 % cb == 0
    b0 = o_sg0 // cb

    def body(u0, u1, v0, v1, w_ref, b_ref, g_ref, be_ref, o_ref):
        u = jnp.concatenate([u0[...], u1[...]], axis=1)
        v = jnp.concatenate([v0[...], v1[...]], axis=1)
        gu, _, _, vn = _sg_parts(u, v, g_ref[...], be_ref[...])
        vnb = vn.astype(BF16)
        for g in range(SG_GROUPS):
            sl = slice(g * SG_CHUNK, (g + 1) * SG_CHUNK)
            mixed = jnp.dot(w_ref[g], vnb[:, sl], preferred_element_type=F32) + b_ref[g]
            o_ref[:, sl] = (gu[:, sl] * mixed).astype(BF16)

    zs = lambda k: pl.BlockSpec((T, cb), lambda i, k=k: (i, b0 + k))
    const3 = lambda shp: pl.BlockSpec(shp, lambda i: (0, 0, 0))
    vec = pl.BlockSpec((1, SG_W), lambda i: (0, 0))
    return pl.pallas_call(
        body, name=name, grid=(S // T,),
        in_specs=[zs(0), zs(1), zs(2), zs(3), const3((SG_GROUPS, SG_CHUNK, SG_CHUNK)), const3((SG_GROUPS, SG_CHUNK, 1)), vec, vec],
        out_specs=pl.BlockSpec((T, SG_W), lambda i: (i, 0)), out_shape=jax.ShapeDtypeStruct((S, SG_W), BF16),
        compiler_params=pltpu.CompilerParams(dimension_semantics=("parallel",), vmem_limit_bytes=_vmem_limit(4 * 1024 * 1024, 8 * T * SG_W * 4)),
    )(z, z, z, z, sg_w, sg_bc, lng, lnb)


def _sg_bwd(name, z, dsg, sg_w, sg_wt, sg_bc, lng, lnb, o_sg0):
    S = z.shape[0]
    T = SG_CHUNK
    cb = 512
    b0 = o_sg0 // cb

    def body(u0, u1, v0, v1, d_ref, w_ref, wt_ref, b_ref, g_ref, be_ref, dz_ref, dw_ref, db_ref, dg_ref, dbe_ref):
        i = pl.program_id(0)
        u = jnp.concatenate([u0[...], u1[...]], axis=1)
        v = jnp.concatenate([v0[...], v1[...]], axis=1)
        gu, xhat, rstd, vn = _sg_parts(u, v, g_ref[...], be_ref[...])
        vnb = vn.astype(BF16)
        dsg_v = d_ref[...].astype(F32)
        dmix = dsg_v * gu
        dmixb = dmix.astype(BF16)
        dvn_parts, mixed_parts, dw_parts, db_parts = [], [], [], []
        for g in range(SG_GROUPS):
            sl = slice(g * SG_CHUNK, (g + 1) * SG_CHUNK)
            mixed_parts.append(jnp.dot(w_ref[g], vnb[:, sl], preferred_element_type=F32) + b_ref[g])
            dvn_parts.append(jnp.dot(wt_ref[g], dmixb[:, sl], preferred_element_type=F32))
            dw_parts.append(lax.dot_general(dmixb[:, sl], vnb[:, sl], _NT, preferred_element_type=F32))
            db_parts.append(jnp.sum(dmix[:, sl], axis=1, keepdims=True))
        mixed = jnp.concatenate(mixed_parts, axis=1)
        dvn = jnp.concatenate(dvn_parts, axis=1)
        dzu = dsg_v * mixed * _gelu_grad(u)
        dxh = dvn * g_ref[...]
        dgv = rstd * (dxh - jnp.mean(dxh, axis=-1, keepdims=True) - xhat * jnp.mean(dxh * xhat, axis=-1, keepdims=True))
        dzv = dgv * _gelu_grad(v)
        dz_ref[:, :SG_W] = dzu.astype(BF16)
        dz_ref[:, SG_W:] = dzv.astype(BF16)
        dgp = jnp.sum(dvn * xhat, axis=0, keepdims=True)
        dbp = jnp.sum(dvn, axis=0, keepdims=True)

        @pl.when(i == 0)
        def _():
            for g in range(SG_GROUPS):
                dw_ref[g] = dw_parts[g]
                db_ref[g] = db_parts[g]
            dg_ref[...] = dgp
            dbe_ref[...] = dbp

        @pl.when(i > 0)
        def _():
            for g in range(SG_GROUPS):
                dw_ref[g] += dw_parts[g]
                db_ref[g] += db_parts[g]
            dg_ref[...] += dgp
            dbe_ref[...] += dbp

    zs = lambda k: pl.BlockSpec((T, cb), lambda i, k=k: (i, b0 + k))
    const3 = lambda shp: pl.BlockSpec(shp, lambda i: (0, 0, 0))
    vec = pl.BlockSpec((1, SG_W), lambda i: (0, 0))
    return pl.pallas_call(
        body, name=name, grid=(S // T,),
        in_specs=[zs(0), zs(1), zs(2), zs(3), pl.BlockSpec((T, SG_W), lambda i: (i, 0)),
                  const3((SG_GROUPS, SG_CHUNK, SG_CHUNK)), const3((SG_GROUPS, SG_CHUNK, SG_CHUNK)), const3((SG_GROUPS, SG_CHUNK, 1)),
                  vec, vec],
        out_specs=[pl.BlockSpec((T, 2 * SG_W), lambda i: (i, 0)), const3((SG_GROUPS, SG_CHUNK, SG_CHUNK)),
                   const3((SG_GROUPS, SG_CHUNK, 1)), vec, vec],
        out_shape=[jax.ShapeDtypeStruct((S, 2 * SG_W), BF16), jax.ShapeDtypeStruct((SG_GROUPS, SG_CHUNK, SG_CHUNK), F32),
                   jax.ShapeDtypeStruct((SG_GROUPS, SG_CHUNK, 1), F32), jax.ShapeDtypeStruct((1, SG_W), F32),
                   jax.ShapeDtypeStruct((1, SG_W), F32)],
        compiler_params=pltpu.CompilerParams(dimension_semantics=("arbitrary",),
                                             vmem_limit_bytes=_vmem_limit(6 * 1024 * 1024, 16 * T * SG_W * 4)),
    )(z, z, z, z, dsg, sg_w, sg_wt, sg_bc, lng, lnb)


def _gate_bwd(name, z, dmerged, y_attn, y_sg, o_g0, in_w):
    S, D = dmerged.shape
    tr = _pick(S, (512, 256, 128, 8))
    cb = _pick(D, (512, 256, 128))
    assert o_g0 % cb == 0
    nd = D // cb
    b0 = o_g0 // cb

    def body(z_ref, dm_ref, ya_ref, ys_ref, dz_ref, dy_ref):
        jj = pl.program_id(1)
        gate = _sigmoid(z_ref[...])
        dm = dm_ref[...].astype(F32)
        y = jnp.where(jj < nd, ya_ref[...], ys_ref[...]).astype(F32)
        dz_ref[...] = (dm * y * gate * (1.0 - gate)).astype(BF16)
        dy_ref[...] = (dm * gate).astype(BF16)

    half = pl.BlockSpec((tr, cb), lambda i, jj: (i, jj % nd))
    return pl.pallas_call(
        body, name=name, grid=(S // tr, 2 * nd),
        in_specs=[pl.BlockSpec((tr, cb), lambda i, jj: (i, b0 + jj)), half, half, half],
        out_specs=[pl.BlockSpec((tr, cb), lambda i, jj: (i, b0 + jj)), pl.BlockSpec((tr, cb), lambda i, jj: (i, jj))],
        out_shape=[jax.ShapeDtypeStruct((S, in_w), BF16), jax.ShapeDtypeStruct((S, 2 * D), BF16)],
        compiler_params=pltpu.CompilerParams(dimension_semantics=("parallel", "arbitrary"),
                                             vmem_limit_bytes=_vmem_limit(tr * cb * 14, 6 * tr * cb * 4)),
    )(z, dmerged, y_attn, y_sg)


def _row(v):
    return v.reshape(1, -1)


def _local_step(x, p, target, wf, small, after_group):
    S, D = x.shape
    L = p.shape[0]
    in_w = wf["w_in"][0].shape[1]
    ff = wf["w_ff_gate"][0].shape[1]
    ple = p.shape[2]
    o_sg0, o_g0 = QKV_W, QKV_W + 2 * SG_W
    cosf, sinf = _rope_tables(S)
    pb = p.astype(BF16)
    tmb = _pick(S, (1024, 512, 256))
    tn_in = _pick(in_w, (768, 1024, 512))
    tn_d = _pick(D, (1024, 512, 256))
    tn_g = _pick(D, (512, 256))
    tn_ff = _pick(ff, (512, 256))

    saved = []
    xs = x
    for i in range(L):
        sv = {"x0": xs}
        h = _rmsnorm_fwd(f"norm_mix_{i}", xs, _row(small["norm_mix"][i]))
        (z,) = _mm(f"in_proj_{i}", [dict(a=h, b=wf["w_in"], bl=i, mode="nn", K=D)], S, in_w,
                   [dict(shape=(S, in_w), dtype=F32)], _first, tm=tmb, tn=tn_in)
        qkv = _rope_fwd(f"rope_{i}", z, cosf, sinf)
        attn, lse_c = _attn_fwd(f"attn_{i}", qkv)
        sgw = small["sg_w"][i].astype(BF16)
        sgbc = small["sg_b"][i].reshape(SG_GROUPS, SG_CHUNK, 1)
        sg = _sg_fwd(f"sgu_{i}", z, sgw, sgbc, _row(small["sg_ln_g"][i]), _row(small["sg_ln_b"][i]), o_sg0)

        def merge(accs, tiles, rows):
            ya, ys = accs[0].astype(BF16), accs[1].astype(BF16)
            g0, g1 = _sigmoid(tiles[0]), _sigmoid(tiles[1])
            return [ya, ys, g0 * ya.astype(F32) + g1 * ys.astype(F32)]

        y_attn, y_sg, merged = _mm(
            f"branches_{i}",
            [dict(a=attn, b=wf["w_br_attn"], bl=i, mode="nn", K=ATTN_W), dict(a=sg, b=wf["w_br_sg"], bl=i, mode="nn", K=SG_W)],
            S, D, [dict(shape=(S, D), dtype=BF16)] * 3, merge,
            tiles=[dict(x=z, off=o_g0), dict(x=z, off=o_g0 + D)], tm=tmb, tn=tn_g)
        (x1,) = _mm(f"out_proj_{i}", [dict(a=merged, b=wf["w_out"], bl=i, mode="nn", K=D)], S, D,
                    [dict(shape=(S, D), dtype=F32)], lambda a, t, r: [t[0] + a[0]], tiles=[dict(x=xs)], tm=tmb, tn=tn_d)
        h2 = _rmsnorm_fwd(f"norm_ffn_{i}", x1, _row(small["norm_ffn"][i]))

        def swiglu(accs, tiles, rows):
            fg = accs[0].astype(BF16).astype(F32)
            fu = accs[1].astype(BF16).astype(F32)
            return [fg, fu, fg * _sigmoid(fg) * fu]

        ffg, ffu, act = _mm(
            f"ff_in_{i}",
            [dict(a=h2, b=wf["w_ff_gate"], bl=i, mode="nn", K=D), dict(a=h2, b=wf["w_ff_up"], bl=i, mode="nn", K=D)],
            S, ff, [dict(shape=(S, ff), dtype=BF16)] * 3, swiglu, tm=tmb, tn=tn_ff)
        (x2,) = _mm(f"ff_out_{i}", [dict(a=act, b=wf["w_ff_down"], bl=i, mode="nn", K=ff)], S, D,
                    [dict(shape=(S, D), dtype=F32)], lambda a, t, r: [t[0] + a[0]], tiles=[dict(x=x1)], tm=tmb, tn=tn_d)
        h3 = _rmsnorm_fwd(f"norm_ple_{i}", x2, _row(small["norm_ple"][i]))

        def ple_mix(accs, tiles, rows):
            gp = _sigmoid(accs[0]).astype(BF16)
            pe = accs[1].astype(BF16)
            return [tiles[0] + gp.astype(F32) * pe.astype(F32), gp, pe]

        x3, gp, pe = _mm(
            f"ple_{i}",
            [dict(a=h3, b=wf["w_ple_gate"], bl=i, mode="nn", K=D), dict(a=pb, al=i, b=wf["w_ple"], bl=i, mode="nn", K=ple)],
            S, D, [dict(shape=(S, D), dtype=F32), dict(shape=(S, D), dtype=BF16), dict(shape=(S, D), dtype=BF16)], ple_mix,
            tiles=[dict(x=x2)], tm=tmb, tn=tn_g)
        sv.update(h=h, z=z, qkv=qkv, attn=attn, lse_c=lse_c, sg=sg, y_attn=y_attn, y_sg=y_sg, merged=merged,
                  x1=x1, h2=h2, ffg=ffg, ffu=ffu, act=act, x2=x2, h3=h3, gp=gp, pe=pe, sgw=sgw, sgbc=sgbc)
        saved.append(sv)
        xs = x3

    loss_cell, dx, dxb, dg_final = _loss_head(xs, _row(small["norm_final"]), target)

    gw = {n: [None] * L for n in BIG}
    gs = {n: [None] * L for n in SMALL if n != "norm_final"}

    def dw(n, i, a, a_off, b, bn_off, K_rows, N_cols, tm, tn):
        (gw[n][i],) = _mm(f"d_{n}_{i}", [dict(a=a, b=b, mode="tn", K=S, a_off=a_off, bn_off=bn_off)], K_rows, N_cols,
                          [dict(shape=(K_rows, N_cols), dtype=BF16)], _first, tm=tm, tn=tn)

    for i in reversed(range(L)):
        sv = saved[i]
        dpre, dpe = _ew(f"ple_gate_bwd_{i}",
                        lambda d, g, e: [d * e.astype(F32) * g.astype(F32) * (1.0 - g.astype(F32)), d * g.astype(F32)],
                        [dx, sv["gp"], sv["pe"]], [BF16, BF16], S, D)
        (dh3,) = _mm(f"d_h3_{i}", [dict(a=dpre, b=wf["w_ple_gate"], bl=i, mode="nt", K=D)], S, D,
                     [dict(shape=(S, D), dtype=BF16)], _first, tm=tmb, tn=tn_d)
        dw("w_ple_gate", i, sv["h3"], 0, dpre, 0, D, D, tn_d, tn_d)
        dw("w_ple", i, pb[i], 0, dpe, 0, ple, D, _pick(ple, (256, 128)), _pick(D, (2048, 1024, 512, 256)))
        dx, dxb, gs["norm_ple"][i] = _rmsnorm_bwd(f"norm_ple_bwd_{i}", sv["x2"], _row(small["norm_ple"][i]), dh3, dx)
        def swiglu_bwd(accs, tiles, rows):
            da = accs[0].astype(BF16).astype(F32)
            fg, fu = tiles[0].astype(F32), tiles[1].astype(F32)
            sg_ = _sigmoid(fg)
            return [da * fu * (sg_ * (1.0 + fg * (1.0 - sg_))), da * (fg * sg_)]

        dffg, dffu = _mm(f"d_act_{i}", [dict(a=dxb, b=wf["w_ff_down"], bl=i, mode="nt", K=D)], S, ff,
                         [dict(shape=(S, ff), dtype=BF16)] * 2, swiglu_bwd, tiles=[dict(x=sv["ffg"]), dict(x=sv["ffu"])],
                         tm=tmb, tn=tn_ff)
        dw("w_ff_down", i, sv["act"], 0, dxb, 0, ff, D, tn_ff, _pick(D, (2048, 1024, 512, 256)))
        dw("w_ff_gate", i, sv["h2"], 0, dffg, 0, D, ff, _pick(D, (2048, 1024, 512, 256)), tn_ff)
        dw("w_ff_up", i, sv["h2"], 0, dffu, 0, D, ff, _pick(D, (2048, 1024, 512, 256)), tn_ff)
        (dffg, dffu), _ = lax.optimization_barrier(((dffg, dffu), after_group(i, "ffn", {n: gw[n][i] for n in GRAD_GROUPS["ffn"]})))
        (dh2,) = _mm(f"d_h2_{i}", [dict(a=dffg, b=wf["w_ff_gate"], bl=i, mode="nt", K=ff),
                                   dict(a=dffu, b=wf["w_ff_up"], bl=i, mode="nt", K=ff)], S, D,
                     [dict(shape=(S, D), dtype=BF16)], lambda a, t, r: [a[0] + a[1]], tm=tmb, tn=tn_d)
        dx, dxb, gs["norm_ffn"][i] = _rmsnorm_bwd(f"norm_ffn_bwd_{i}", sv["x1"], _row(small["norm_ffn"][i]), dh2, dx)
        (dmerged,) = _mm(f"d_merged_{i}", [dict(a=dxb, b=wf["w_out"], bl=i, mode="nt", K=D)], S, D,
                         [dict(shape=(S, D), dtype=BF16)], _first, tm=tmb, tn=tn_d)
        dw("w_out", i, sv["merged"], 0, dxb, 0, D, D, tn_d, tn_d)
        dz, dy = _gate_bwd(f"gate_bwd_{i}", sv["z"], dmerged, sv["y_attn"], sv["y_sg"], o_g0, in_w)
        (dattn,) = _mm(f"d_attn_{i}", [dict(a=dy, b=wf["w_br_attn"], bl=i, mode="nt", K=D)], S, ATTN_W,
                       [dict(shape=(S, ATTN_W), dtype=BF16)], _first, tm=tmb, tn=ATTN_W)
        (dsg,) = _mm(f"d_sg_{i}", [dict(a=dy, a_off=D, b=wf["w_br_sg"], bl=i, mode="nt", K=D)], S, SG_W,
                     [dict(shape=(S, SG_W), dtype=BF16)], _first, tm=tmb, tn=SG_W)
        dw("w_br_attn", i, sv["attn"], 0, dy, 0, ATTN_W, D, ATTN_W, _pick(D, (2048, 1024, 512, 256)))
        dw("w_br_sg", i, sv["sg"], 0, dy, D, SG_W, D, SG_W, _pick(D, (1024, 512, 256)))
        sgwt = jnp.swapaxes(small["sg_w"][i], 1, 2).astype(BF16)
        dzuv, gs["sg_w"][i], dsgb, dlg, dlb = _sg_bwd(f"sgu_bwd_{i}", sv["z"], dsg, sv["sgw"], sgwt, sv["sgbc"],
                                                      _row(small["sg_ln_g"][i]), _row(small["sg_ln_b"][i]), o_sg0)
        gs["sg_b"][i], gs["sg_ln_g"][i], gs["sg_ln_b"][i] = dsgb.reshape(SG_GROUPS, SG_CHUNK), dlg[0], dlb[0]
        dq, dk, dv = _attn_bwd(f"attn_bwd_{i}", sv["qkv"], sv["attn"], dattn, sv["lse_c"])
        dz = _rope_bwd(f"rope_bwd_{i}", dq, dk, dv, dzuv, cosf, sinf, dz)
        dw("w_in", i, sv["h"], 0, dz, 0, D, in_w, tn_d, tn_in)
        dz, _ = lax.optimization_barrier((dz, after_group(i, "mix", {n: gw[n][i] for n in GRAD_GROUPS["mix"]})))
        (dh,) = _mm(f"d_h_{i}", [dict(a=dz, b=wf["w_in"], bl=i, mode="nt", K=in_w)], S, D,
                    [dict(shape=(S, D), dtype=BF16)], _first, tm=tmb, tn=tn_d)
        dx, dxb, gs["norm_mix"][i] = _rmsnorm_bwd(f"norm_mix_bwd_{i}", sv["x0"], _row(small["norm_mix"][i]), dh, dx)

    gsmall ={n: jnp.stack([jnp.reshape(v, small[n].shape[1:]) for v in gs[n]]) for n in gs}
    gsmall["norm_final"] = dg_final[0]
    return loss_cell, dx, gsmall


def _place():
    x, y, c = lax.axis_index("x"), lax.axis_index("y"), lax.axis_index("c")
    return x, y, c, 2 * x + y


def _chip_of(s):
    return s // 2, s % 2


def _aligned(v, m):
    return v if isinstance(v, int) else pl.multiple_of(v, m)


def _piece(name, shape, s, c):
    K, N = shape
    if name in ROW_SHARDED or name == SMALL_BLOCKS:
        ks = K // 4
        return s * ks + c * (ks // 2), ks // 2, 0, N
    ns = N // 4
    return c * (K // 2), K // 2, s * ns, ns


def _handshake(peers):
    barrier = pltpu.get_barrier_semaphore()
    for peer in peers:
        pl.semaphore_signal(barrier, inc=1, device_id=peer, device_id_type=MESH)
    pl.semaphore_wait(barrier, len(peers))


def _gather_body(names, shapes, src, dst, send_sems, recv_sems, local_sems):
    n_w = len(names)
    x, y, c, s = _place()
    sib = (x, y, 1 - c)
    rel = [1, 2, 3]

    def where(w, ps, pc):
        r0, nr, c0, nc = _piece(names[w], shapes[names[w]], ps, pc)
        return dst[w].at[pl.ds(_aligned(r0, 16), nr), pl.ds(_aligned(c0, LANES), nc)]

    def copy(w, k, ps, pc, to, from_src=False):
        return pltpu.make_async_remote_copy(
            src_ref=src[w] if from_src else where(w, ps, pc), dst_ref=where(w, ps, pc),
            send_sem=send_sems.at[w, k], recv_sem=recv_sems.at[w, k], device_id=to, device_id_type=MESH)

    mine, first, passed = [], [], []
    for w in range(n_w):
        cp = pltpu.make_async_copy(src[w], where(w, s, c), local_sems.at[w])
        cp.start()
        mine.append(cp)
        first.append(copy(w, 0, s, c, sib, from_src=True))
        for j in rel:
            first.append(copy(w, j, s, c, (*_chip_of(s ^ j), c), from_src=True))
    for cp in first:
        cp.start()
    for w in range(n_w):
        for j in rel:
            copy(w, j, s ^ j, c, sib).wait_recv()
            fw = copy(w, 3 + j, s ^ j, c, sib)
            fw.start()
            passed.append(fw)
    for w in range(n_w):
        copy(w, 0, s, 1 - c, sib).wait_recv()
        for j in rel:
            copy(w, 3 + j, s ^ j, 1 - c, sib).wait_recv()
    for cp in first + passed:
        cp.wait_send()
    for cp in mine:
        cp.wait()


def _gather_sems(n_w):
    return (pltpu.SemaphoreType.DMA((n_w, 7)), pltpu.SemaphoreType.DMA((n_w, 7)), pltpu.SemaphoreType.DMA((n_w,)))


def _gather_peers():
    x, y, c, s = _place()
    return [(x, y, 1 - c)] + [(*_chip_of(s ^ j), c) for j in (1, 2, 3)]


def _gather_weights_async(name, pieces, shapes):
    names = list(pieces)
    n_w = len(names)
    src = [jax.new_ref(pieces[n], memory_space=pltpu.MemorySpace.HBM) for n in names]
    dst = [jax.empty_ref(jax.ShapeDtypeStruct(tuple(shapes[n]), pieces[n].dtype), memory_space=pltpu.MemorySpace.HBM)
           for n in names]

    @pl.kernel(mesh=plsc.ScalarSubcoreMesh(axis_name="seq", num_cores=1), name=name, scratch_types=_gather_sems(n_w),
               compiler_params=pltpu.CompilerParams(collective_id=GATHER_COLLECTIVE_ID))
    def launch(send_sems, recv_sems, local_sems):
        _handshake(_gather_peers())
        _gather_body(names, shapes, src, dst, send_sems, recv_sems, local_sems)

    launch()
    return {n: d[...] for n, d in zip(names, dst)}


def _halves_view(name, g):
    L, K, N = g.shape
    if name in ROW_SHARDED:
        return g.reshape(L * 4, 2, K // 8, N)
    return g.reshape(L, 2, K // 2, N)


def _all_peers():
    x, y, c, s = _place()
    return [(x, y, 1 - c)] + [(*_chip_of(s ^ j), h) for j in (1, 2, 3) for h in (0, 1)]


def _scatter_partials_async(name, views):
    names = list(views)
    n_w = len(names)
    src = [jax.new_ref(views[n], memory_space=pltpu.MemorySpace.HBM) for n in names]
    dst = [jax.empty_ref(jax.ShapeDtypeStruct(_partials_out_shape(n, views[n].shape), BF16), memory_space=pltpu.MemorySpace.HBM)
           for n in names]

    @pl.kernel(mesh=plsc.ScalarSubcoreMesh(axis_name="seq", num_cores=1), name=name, scratch_types=_partials_sems(n_w),
               compiler_params=pltpu.CompilerParams(collective_id=PARTIALS_COLLECTIVE_ID))
    def launch(send_sems, recv_sems):
        _handshake(_all_peers())
        _scatter_partials_body(names, src, dst, send_sems, recv_sems)

    launch()
    return {n: d[...] for n, d in zip(names, dst)}


def _partials_out_shape(name, v):
    return (7, 1, v[2], v[3] if name in ROW_SHARDED else v[3] // 4)


def _partials_sems(n_w):
    return (pltpu.SemaphoreType.DMA((n_w, 7)), pltpu.SemaphoreType.DMA((n_w, 7)))


def _scatter_partials_body(names, src, dst, send_sems, recv_sems):
    x, y, c, s = _place()

    def piece(w, t, h):
        if names[w] in ROW_SHARDED:
            return src[w].at[pl.ds(t, 1), h]
        ns = src[w].shape[3] // 4
        return src[w].at[:, h, :, pl.ds(pl.multiple_of(t * ns, LANES), ns)]

    sent = []
    for w in range(len(names)):
        for j in (1, 2, 3):
            for h in (0, 1):
                sent.append(pltpu.make_async_remote_copy(
                    src_ref=piece(w, s ^ j, h), dst_ref=dst[w].at[2 * (j - 1) + c], send_sem=send_sems.at[w, 2 * (j - 1) + h],
                    recv_sem=recv_sems.at[w, 2 * (j - 1) + c], device_id=(*_chip_of(s ^ j), h), device_id_type=MESH))
        sent.append(pltpu.make_async_remote_copy(
            src_ref=piece(w, s, 1 - c), dst_ref=dst[w].at[6], send_sem=send_sems.at[w, 6], recv_sem=recv_sems.at[w, 6],
            device_id=(x, y, 1 - c), device_id_type=MESH))
    for cp in sent:
        cp.start()
    for w in range(len(names)):
        for slot in range(7):
            pltpu.make_async_remote_copy(src_ref=dst[w].at[slot], dst_ref=dst[w].at[slot], send_sem=send_sems.at[w, slot],
                                         recv_sem=recv_sems.at[w, slot], device_id=(x, y, 1 - c), device_id_type=MESH).wait_recv()
    for cp in sent:
        cp.wait_send()


def _shard_sum_partials(name, view, parts, place, row_sharded, layer, n_layers, into):
    R, C = parts.shape[2:]
    tc = _pick(C, (2048, 1408, 1024, 896, 512, 384, 256, 128))
    tr = _pick(R, [t for t in (1024, 512, 256, 128, 64, 32, 16) if t * tc <= 2 * EW_TILE_ELEMS] + [8])

    def body(p_ref, own_ref, *rest):
        acc = own_ref[...].astype(F32) + rest[6][...].astype(F32)
        for k in range(6):
            acc = acc + rest[k][...].astype(F32)
        rest[-1][...] = acc

    if row_sharded:
        own_spec = pl.BlockSpec((None, None, tr, tc), lambda i, j, p: (p[1], p[0], i, j))
    else:
        own_spec = pl.BlockSpec((None, None, tr, tc), lambda i, j, p: (0, p[0], i, p[1] * (C // tc) + j))
    part = lambda k: pl.BlockSpec((None, None, tr, tc), lambda i, j, p, k=k: (k, 0, i, j))
    in_specs, args, aliases = [own_spec] + [part(k) for k in range(7)], [place, view] + [parts] * 7, {}
    if into is not None:
        in_specs.append(pl.BlockSpec(memory_space=pl.ANY))
        args.append(into)
        aliases = {9: 0}
    return pl.pallas_call(
        body, name=name, out_shape=jax.ShapeDtypeStruct((n_layers, 2, R, C), F32),
        grid_spec=pltpu.PrefetchScalarGridSpec(
            num_scalar_prefetch=1, grid=(R // tr, C // tc), in_specs=in_specs,
            out_specs=pl.BlockSpec((None, None, tr, tc), lambda i, j, p: (layer, p[0], i, j))),
        input_output_aliases=aliases,
        compiler_params=pltpu.CompilerParams(dimension_semantics=("parallel", "parallel"),
                                             vmem_limit_bytes=_vmem_limit(20 * tr * tc, 5 * tr * tc * 4)),
    )(*args)


def _share_halves(name, ghalf):
    names = list(ghalf)
    n_w = len(names)

    def body(*refs):
        src = refs[:n_w]
        dst = refs[n_w:2 * n_w]
        send_sems, recv_sems = refs[2 * n_w:]
        x, y, c, s = _place()
        remote = [pltpu.make_async_remote_copy(src_ref=src[w].at[:, c], dst_ref=dst[w].at[:, c], send_sem=send_sems.at[w],
                                               recv_sem=recv_sems.at[w], device_id=(x, y, 1 - c), device_id_type=MESH)
                  for w in range(n_w)]
        for cp in remote:
            cp.start()
        for cp in remote:
            cp.wait()

    anyspec = pl.BlockSpec(memory_space=pl.ANY)
    out = pl.pallas_call(
        body, name=name, in_specs=[anyspec] * n_w, out_specs=[anyspec] * n_w,
        out_shape=[jax.ShapeDtypeStruct(ghalf[n].shape, F32) for n in names],
        input_output_aliases={w: w for w in range(n_w)},
        scratch_shapes=[pltpu.SemaphoreType.DMA((n_w,)), pltpu.SemaphoreType.DMA((n_w,))],
    )(*[ghalf[n] for n in names])
    return dict(zip(names, out))


def _adamw_math(w, g, m, v):
    m = ADAM_B1 * m + (1.0 - ADAM_B1) * g
    v = ADAM_B2 * v + (1.0 - ADAM_B2) * (g * g)
    m_hat = m / (1.0 - ADAM_B1 ** ADAM_STEP)
    v_hat = v / (1.0 - ADAM_B2 ** ADAM_STEP)
    delta = -ADAM_LR * (m_hat / (jnp.sqrt(v_hat) + ADAM_EPS) + ADAM_WD * w)
    return delta, m, v


def _adamw(name, w, g, m, v):
    shape = w.shape
    C = shape[-1]
    R = math.prod(shape[:-1])
    f = lambda a: a.reshape(R, C)
    res = _ew(name, lambda w_, g_, m_, v_: [g_, *_adamw_math(w_, g_, m_, v_)], [f(w), f(g), f(m), f(v)], [F32] * 4, R, C)
    return [r.reshape(shape) for r in res]


def _pack_small(d):
    return jnp.concatenate([d[n].reshape(-1, LANES) for n in SMALL], axis=0)


def _unpack_small(flat, like):
    out, r = {}, 0
    for n in SMALL:
        k = like[n].size // LANES
        out[n] = flat[r:r + k].reshape(like[n].shape)
        r += k
    return out


def _small_update(gall, w, m, v):
    M = w.shape[0]
    tr = _pick(M, (552, 276, 184, 96, 48, 24, 8))

    def body(*refs):
        g = refs[0][...]
        for d in range(1, 8):
            g = g + refs[d][...]
        delta, nm, nv = _adamw_math(refs[8][...], g, refs[9][...], refs[10][...])
        refs[11][...] = g
        refs[12][...] = delta
        refs[13][...] = nm
        refs[14][...] = nv

    blk = pl.BlockSpec((tr, LANES), lambda i: (i, 0))
    in_specs = [pl.BlockSpec((tr, LANES), lambda i, d=d: (d * (M // tr) + i, 0)) for d in range(8)] + [blk] * 3
    return pl.pallas_call(
        body, name="small_update", grid=(M // tr,), in_specs=in_specs, out_specs=[blk] * 4,
        out_shape=[jax.ShapeDtypeStruct((M, LANES), F32)] * 4,
        compiler_params=pltpu.CompilerParams(dimension_semantics=("parallel",), vmem_limit_bytes=_vmem_limit(15 * tr * LANES * 4)),
    )(*([gall] * 8), w, m, v)


def _step(x, p, target, w, m, v):
    L = p.shape[0]
    x_i, y_i, c, s = _place()
    shapes = {}
    for n in BIG:
        _, K, N = w[n].shape
        shapes[n] = (4 * K, N) if n in ROW_SHARDED else (K, 4 * N)
    def pieces_of(i):
        return {n: lax.dynamic_slice_in_dim(w[n][i], c * (w[n].shape[1] // 2), w[n].shape[1] // 2, axis=0).astype(BF16)
                for n in BIG}

    def piece(n, i, after=None):
        wn = w[n] if after is None else lax.optimization_barrier((w[n], after))[0]
        return lax.dynamic_slice_in_dim(wn[i], c * (w[n].shape[1] // 2), w[n].shape[1] // 2, axis=0).astype(BF16)

    w_in0 = piece("w_in", 0)
    layers = [{**_gather_weights_async("gather_weights_0_w_in", {"w_in": w_in0}, shapes),
               **_gather_weights_async("gather_weights_0", {n: piece(n, 0, w_in0) for n in BIG if n != "w_in"}, shapes)}]
    for i in range(1, L):
        layers.append({})
        for g in ("mix", "ffn"):
            layers[i].update(_gather_weights_async(f"gather_weights_{i}_{g}", {n: piece(n, i, w_in0) for n in GRAD_GROUPS[g]}, shapes))
    wf = {n: [layers[i][n] for i in range(L)] for n in BIG}
    small = {n: w[n] for n in SMALL}
    place = jnp.stack([c, s]).astype(jnp.int32)
    reduced = []

    def after_group(i, group, grads):
        views = {n: _halves_view(n, g[None]) for n, g in grads.items()}
        reduced.append((i, group, views, _scatter_partials_async(f"scatter_partials_{i}_{group}", views)))
        return views

    loss_cell, dx, gsmall = _local_step(x[0], p[:, 0], target[0], wf, small, after_group)
    loss = lax.psum(jnp.sum(loss_cell), ("x", "y", "c"))
    packed = _pack_small(gsmall)
    gall = _gather_weights_async("gather_small", {SMALL_BLOCKS: packed}, {SMALL_BLOCKS: (8 * packed.shape[0], LANES)})[SMALL_BLOCKS]
    ghalf = {n: None for n in BIG}
    grad, delta, new_m, new_v = {}, {}, {}, {}
    done = None

    def finish(group):
        gfull = _share_halves(f"share_halves_{group}", {n: ghalf[n] for n in GRAD_GROUPS[group]})
        for n in GRAD_GROUPS[group]:
            grad[n], delta[n], new_m[n], new_v[n] = _adamw(f"adamw_{n}", w[n], gfull[n].reshape(w[n].shape), m[n], v[n])
        return {n: delta[n] for n in GRAD_GROUPS[group]}

    for k, (i, group, own, parts) in enumerate(reduced):
        if k == len(reduced) - 1:
            done = finish("ffn")
        parts, _ = lax.optimization_barrier((parts, done))
        for n in own:
            ghalf[n] = _shard_sum_partials(f"shard_sum_{n}_{i}", own[n], parts[n], place, n in ROW_SHARDED, i, L, ghalf[n])
        done = {n: ghalf[n] for n in own}
    finish("mix")
    gall, _ = lax.optimization_barrier((gall, delta))
    gsum, dsm, nms, nvs =_small_update(gall, _pack_small(small), _pack_small({n: m[n] for n in SMALL}),
                                        _pack_small({n: v[n] for n in SMALL}))
    for dst, flat in ((grad, gsum), (delta, dsm), (new_m, nms), (new_v, nvs)):
        dst.update(_unpack_small(flat, small))
    return loss, dx[None], grad, delta, new_m, new_v


def kernel(x, p, w_in, w_br_attn, w_br_sg, w_out, sg_w, sg_b, sg_ln_g, sg_ln_b, norm_mix, norm_ffn, norm_ple, norm_final, w_ff_gate, w_ff_up, w_ff_down, w_ple_gate, w_ple, loss_target, m_w_in, m_w_br_attn, m_w_br_sg, m_w_out, m_sg_w, m_sg_b, m_sg_ln_g, m_sg_ln_b, m_norm_mix, m_norm_ffn, m_norm_ple, m_norm_final, m_w_ff_gate, m_w_ff_up, m_w_ff_down, m_w_ple_gate, m_w_ple, v_w_in, v_w_br_attn, v_w_br_sg, v_w_out, v_sg_w, v_sg_b, v_sg_ln_g, v_sg_ln_b, v_norm_mix, v_norm_ffn, v_norm_ple, v_norm_final, v_w_ff_gate, v_w_ff_up, v_w_ff_down, v_w_ple_gate, v_w_ple):
    w = dict(w_in=w_in, w_br_attn=w_br_attn, w_br_sg=w_br_sg, w_out=w_out, sg_w=sg_w, sg_b=sg_b, sg_ln_g=sg_ln_g, sg_ln_b=sg_ln_b,
             norm_mix=norm_mix, norm_ffn=norm_ffn, norm_ple=norm_ple, norm_final=norm_final, w_ff_gate=w_ff_gate, w_ff_up=w_ff_up,
             w_ff_down=w_ff_down, w_ple_gate=w_ple_gate, w_ple=w_ple)
    m = dict(w_in=m_w_in, w_br_attn=m_w_br_attn, w_br_sg=m_w_br_sg, w_out=m_w_out, sg_w=m_sg_w, sg_b=m_sg_b, sg_ln_g=m_sg_ln_g,
             sg_ln_b=m_sg_ln_b, norm_mix=m_norm_mix, norm_ffn=m_norm_ffn, norm_ple=m_norm_ple, norm_final=m_norm_final,
             w_ff_gate=m_w_ff_gate, w_ff_up=m_w_ff_up, w_ff_down=m_w_ff_down, w_ple_gate=m_w_ple_gate, w_ple=m_w_ple)
    v = dict(w_in=v_w_in, w_br_attn=v_w_br_attn, w_br_sg=v_w_br_sg, w_out=v_w_out, sg_w=v_sg_w, sg_b=v_sg_b, sg_ln_g=v_sg_ln_g,
             sg_ln_b=v_sg_ln_b, norm_mix=v_norm_mix, norm_ffn=v_norm_ffn, norm_ple=v_norm_ple, norm_final=v_norm_final,
             w_ff_gate=v_w_ff_gate, w_ff_up=v_w_ff_up, w_ff_down=v_w_ff_down, w_ple_gate=v_w_ple_gate, w_ple=v_w_ple)
    loss, grad_x, grad, delta, new_m, new_v = _step(x, p, loss_target, w, m, v)
    return (loss, grad_x, *[grad[n] for n in WEIGHTS], *[delta[n] for n in WEIGHTS], *[new_m[n] for n in WEIGHTS],
            *[new_v[n] for n in WEIGHTS])
```

```python
import functools
import math

import jax
import jax.numpy as jnp
from jax import lax
from jax.experimental import pallas as pl
from jax.experimental.pallas import tpu as pltpu
from jax.experimental.pallas import tpu_sc as plsc

F32 = jnp.float32
BF16 = jnp.bfloat16
MESH = pl.DeviceIdType.MESH

HEAD_DIM = 128
ATTN_GROUPS = ((128, 1), (512, 4), (2048, 16))
N_GROUPS = 3
HEADS = 4
QKV_W = 3 * N_GROUPS * HEADS * HEAD_DIM
ATTN_W = HEADS * HEAD_DIM
SG_CHUNK = 128
SG_GROUPS = 8
SG_W = 1024
RADIUS = 64
ROPE_THETA = 10000.0
NORM_EPS = 1e-6
NEG_INF = -1e30
ADAM_LR, ADAM_B1, ADAM_B2, ADAM_EPS, ADAM_WD, ADAM_STEP = 0.001, 0.9, 0.999, 1e-08, 0.01, 10

VMEM_CAP_V7X = 56 * 1024 * 1024
LANES = 128
EW_TILE_ELEMS = 256 * 1024
MM_VMEM_BUDGET = 44 * 1024 * 1024

GATHER_COLLECTIVE_ID = 1
PARTIALS_COLLECTIVE_ID = 2

BIG = ("w_in", "w_br_attn", "w_br_sg", "w_out", "w_ff_gate", "w_ff_up", "w_ff_down", "w_ple_gate", "w_ple")
ROW_SHARDED = ("w_out", "w_ff_down", "w_ple_gate")
SMALL_BLOCKS = "small_blocks"
GRAD_GROUPS = {"ffn": ("w_ple_gate", "w_ple", "w_ff_down", "w_ff_gate", "w_ff_up"), "mix": ("w_out", "w_br_attn", "w_br_sg", "w_in")}
SMALL = ("sg_w", "sg_b", "sg_ln_g", "sg_ln_b", "norm_mix", "norm_ffn", "norm_ple", "norm_final")
WEIGHTS = ("w_in", "w_br_attn", "w_br_sg", "w_out", "sg_w", "sg_b", "sg_ln_g", "sg_ln_b", "norm_mix", "norm_ffn",
           "norm_ple", "norm_final", "w_ff_gate", "w_ff_up", "w_ff_down", "w_ple_gate", "w_ple")


def _pick(n, prefs):
    for t in prefs:
        if n % t == 0:
            return t
    return n


def _vmem_limit(block_bytes, temp_bytes=0):
    est = 2 * block_bytes + temp_bytes
    assert est <= VMEM_CAP_V7X, est
    return VMEM_CAP_V7X


def _sigmoid(x):
    return 1.0 / (1.0 + jnp.exp(-x))


_GELU_C = math.sqrt(2.0 / math.pi)


def _gelu(x):
    return 0.5 * x * (1.0 + jnp.tanh(_GELU_C * (x + 0.044715 * (x * x * x))))


def _gelu_grad(x):
    t = jnp.tanh(_GELU_C * (x + 0.044715 * (x * x * x)))
    return 0.5 * (1.0 + t) + 0.5 * x * (1.0 - t * t) * (_GELU_C * (1.0 + 3.0 * 0.044715 * (x * x)))


def _lead(arr, l, blk, idx):
    if arr.ndim == 2:
        return pl.BlockSpec(blk, idx)
    return pl.BlockSpec((None,) + blk, lambda *g: (l,) + idx(*g))


def _k_steps(prods, tm, tn, fixed_bytes):
    for nk in range(1, 129):
        if any(p["K"] % nk or (p["K"] // nk) % LANES for p in prods):
            continue
        if 2 * sum((tm + tn) * (p["K"] // nk) * 2 for p in prods) + fixed_bytes <= MM_VMEM_BUDGET:
            return nk
    raise ValueError("no contraction split fits VMEM")


def _mm(name, prods, M, N, outs, epilogue, tiles=(), rows=(), tm=1024, tn=1024):
    assert M % tm == 0 and N % tn == 0, (name, M, N, tm, tn)
    fixed = 2 * tm * tn * (sum(t["x"].dtype.itemsize for t in tiles) + sum(jnp.dtype(o["dtype"]).itemsize for o in outs))
    fixed += (len(prods) + 2) * tm * tn * 4
    nk = _k_steps(prods, tm, tn, fixed)
    in_specs, args, block_bytes = [], [], 0
    for p in prods:
        if isinstance(p["b"], (list, tuple)):
            p["b"], p["bl"] = p["b"][p["bl"]], None
        K = p["K"]
        assert K % nk == 0, (name, K, nk)
        tk = K // nk
        p["tk"] = tk
        a_off, bk_off, bn_off = p.get("a_off", 0), p.get("bk_off", 0), p.get("bn_off", 0)
        assert bn_off % tn == 0 and bk_off % tk == 0
        if p["mode"] == "nn":
            assert a_off % tk == 0
            a_spec = _lead(p["a"], p.get("al"), (tm, tk), lambda i, j, k, o=a_off // tk: (i, o + k))
            b_spec = _lead(p["b"], p.get("bl"), (tk, tn), lambda i, j, k, ok=bk_off // tk, on=bn_off // tn: (ok + k, on + j))
        elif p["mode"] == "nt":
            assert a_off % tk == 0
            a_spec = _lead(p["a"], p.get("al"), (tm, tk), lambda i, j, k, o=a_off // tk: (i, o + k))
            b_spec = _lead(p["b"], p.get("bl"), (tn, tk), lambda i, j, k, ok=bk_off // tk, on=bn_off // tn: (on + j, ok + k))
        else:
            assert a_off % tm == 0
            a_spec = _lead(p["a"], p.get("al"), (tk, tm), lambda i, j, k, o=a_off // tm: (k, o + i))
            b_spec = _lead(p["b"], p.get("bl"), (tk, tn), lambda i, j, k, on=bn_off // tn: (k, on + j))
        in_specs += [a_spec, b_spec]
        args += [p["a"], p["b"]]
        block_bytes += (tm + tn) * tk * 2
    for t in tiles:
        off = t.get("off", 0)
        assert off % tn == 0
        in_specs.append(_lead(t["x"], t.get("l"), (tm, tn), lambda i, j, k, o=off // tn: (i, o + j)))
        args.append(t["x"])
        block_bytes += tm * tn * t["x"].dtype.itemsize
    for r in rows:
        in_specs.append(pl.BlockSpec((1, tn), lambda i, j, k: (0, j)))
        args.append(r)
    out_shapes, out_specs, aliases = [], [], {}
    for o_i, o in enumerate(outs):
        off = o.get("col_off", 0)
        assert off % tn == 0
        out_shapes.append(jax.ShapeDtypeStruct(o["shape"], o["dtype"]))
        idx = lambda i, j, k, oo=off // tn: (i, oo + j)
        if len(o["shape"]) == 2:
            out_specs.append(pl.BlockSpec((tm, tn), idx))
        else:
            out_specs.append(pl.BlockSpec((None, tm, tn), lambda i, j, k, l=o["l"], f=idx: (l,) + f(i, j, k)))
        if o.get("alias") is not None:
            aliases[len(args)] = o_i
            in_specs.append(pl.BlockSpec(memory_space=pl.ANY))
            args.append(o["alias"])
        block_bytes += tm * tn * jnp.dtype(o["dtype"]).itemsize
    n_p, n_t, n_r, n_o = len(prods), len(tiles), len(rows), len(outs)
    n_alias = len(aliases)
    modes = [p["mode"] for p in prods]

    def body(*refs):
        ab = refs[: 2 * n_p]
        t_refs = refs[2 * n_p: 2 * n_p + n_t]
        r_refs = refs[2 * n_p + n_t: 2 * n_p + n_t + n_r]
        o_refs = refs[2 * n_p + n_t + n_r + n_alias: 2 * n_p + n_t + n_r + n_alias + n_o]
        acc_refs = refs[2 * n_p + n_t + n_r + n_alias + n_o:]
        dims = {"nn": (((1,), (0,)), ((), ())), "nt": (((1,), (1,)), ((), ())), "tn": (((0,), (0,)), ((), ()))}

        def part(q):
            return lax.dot_general(ab[2 * q][...], ab[2 * q + 1][...], dims[modes[q]], preferred_element_type=F32)

        def finish(accs):
            res = epilogue(accs, [t[...] for t in t_refs], [r[...] for r in r_refs])
            for o_ref, val in zip(o_refs, res, strict=True):
                o_ref[...] = val.astype(o_ref.dtype)

        if nk == 1:
            finish([part(q) for q in range(n_p)])
        else:
            k = pl.program_id(2)

            @pl.when(k == 0)
            def _():
                for q, acc in enumerate(acc_refs):
                    acc[...] = part(q)

            @pl.when(k > 0)
            def _():
                for q, acc in enumerate(acc_refs):
                    acc[...] += part(q)

            @pl.when(k == nk - 1)
            def _():
                finish([acc[...] for acc in acc_refs])

    scratch = [pltpu.VMEM((tm, tn), F32) for _ in prods] if nk > 1 else []
    temp = (n_p + 2) * tm * tn * 4
    res = pl.pallas_call(
        body, name=name, grid=(M // tm, N // tn, nk), in_specs=in_specs, out_specs=out_specs, out_shape=out_shapes,
        scratch_shapes=scratch, input_output_aliases=aliases,
        compiler_params=pltpu.CompilerParams(dimension_semantics=("parallel", "parallel", "arbitrary"),
                                             vmem_limit_bytes=_vmem_limit(block_bytes, temp)),
    )(*args)
    return res


def _first(accs, tiles, rows):
    return [accs[0]]


def _ew(name, fn, ins, outs, R, C, tr=None, tc=None):
    tc = tc or _pick(C, (2048, 1536, 1408, 1024, 896, 512, 384, 256, 128))
    tr = tr or _pick(R, [t for t in (512, 256, 128, 64, 32, 16) if t * tc <= EW_TILE_ELEMS] + [8])
    in_specs, args, bb = [], [], 0
    for arr in ins:
        in_specs.append(pl.BlockSpec((tr, tc), lambda i, j: (i, j)))
        args.append(arr)
        bb += tr * tc * arr.dtype.itemsize
    out_shapes = [jax.ShapeDtypeStruct((R, C), d) for d in outs]
    out_specs = [pl.BlockSpec((tr, tc), lambda i, j: (i, j)) for _ in outs]
    bb += sum(tr * tc * jnp.dtype(d).itemsize for d in outs)
    n_in = len(ins)

    def body(*refs):
        res = fn(*[r[...] for r in refs[:n_in]])
        for o_ref, val in zip(refs[n_in:], res, strict=True):
            o_ref[...] = val.astype(o_ref.dtype)

    return pl.pallas_call(
        body, name=name, grid=(R // tr, C // tc), in_specs=in_specs, out_specs=out_specs, out_shape=out_shapes,
        compiler_params=pltpu.CompilerParams(dimension_semantics=("parallel", "parallel"),
                                             vmem_limit_bytes=_vmem_limit(bb, 6 * tr * tc * 4)),
    )(*args)


def _rmsnorm_fwd(name, x, g):
    S, D = x.shape
    tr = _pick(S, (256, 128, 64, 8))

    def body(x_ref, g_ref, h_ref):
        xv = x_ref[...]
        r = lax.rsqrt(jnp.mean(xv * xv, axis=-1, keepdims=True) + NORM_EPS)
        h_ref[...] = (xv * r * g_ref[...]).astype(BF16)

    return pl.pallas_call(
        body, name=name, grid=(S // tr,),
        in_specs=[pl.BlockSpec((tr, D), lambda i: (i, 0)), pl.BlockSpec((1, D), lambda i: (0, 0))],
        out_specs=pl.BlockSpec((tr, D), lambda i: (i, 0)), out_shape=jax.ShapeDtypeStruct((S, D), BF16),
        compiler_params=pltpu.CompilerParams(dimension_semantics=("parallel",),
                                             vmem_limit_bytes=_vmem_limit(tr * D * 6, 3 * tr * D * 4)),
    )(x, g)


def _rmsnorm_bwd(name, x, g, dh, dres):
    S, D = x.shape
    tr = _pick(S, (256, 128, 64, 8))

    def body(x_ref, g_ref, dh_ref, dres_ref, dx_ref, dxb_ref, dg_ref):
        xv = x_ref[...]
        dy = dh_ref[...].astype(F32)
        r = lax.rsqrt(jnp.mean(xv * xv, axis=-1, keepdims=True) + NORM_EPS)
        a = dy * g_ref[...]
        dx = dres_ref[...] + r * a - xv * (r * r * r) * jnp.mean(a * xv, axis=-1, keepdims=True)
        dx_ref[...] = dx
        dxb_ref[...] = dx.astype(BF16)
        part = jnp.sum(dy * xv * r, axis=0, keepdims=True)

        @pl.when(pl.program_id(0) == 0)
        def _():
            dg_ref[...] = part

        @pl.when(pl.program_id(0) > 0)
        def _():
            dg_ref[...] += part

    row = pl.BlockSpec((tr, D), lambda i: (i, 0))
    vec = pl.BlockSpec((1, D), lambda i: (0, 0))
    return pl.pallas_call(
        body, name=name, grid=(S // tr,), in_specs=[row, vec, row, row], out_specs=[row, row, vec],
        out_shape=[jax.ShapeDtypeStruct((S, D), F32), jax.ShapeDtypeStruct((S, D), BF16), jax.ShapeDtypeStruct((1, D), F32)],
        compiler_params=pltpu.CompilerParams(dimension_semantics=("arbitrary",),
                                             vmem_limit_bytes=_vmem_limit(tr * D * 18, 5 * tr * D * 4)),
    )(x, g, dh, dres)


def _loss_head(x, g, target):
    S, D = x.shape
    tr = _pick(S, (256, 128, 64, 8))

    def body(x_ref, g_ref, t_ref, loss_ref, dx_ref, dxb_ref, dg_ref):
        xv = x_ref[...]
        r = lax.rsqrt(jnp.mean(xv * xv, axis=-1, keepdims=True) + NORM_EPS)
        xn = xv * r
        diff = xn * g_ref[...] - t_ref[...]
        dy = diff * (1.0 / D)
        a = dy * g_ref[...]
        dx = r * a - xv * (r * r * r) * jnp.mean(a * xv, axis=-1, keepdims=True)
        dx_ref[...] = dx
        dxb_ref[...] = dx.astype(BF16)
        part = jnp.sum(dy * xn, axis=0, keepdims=True)
        cell = (lax.broadcasted_iota(jnp.int32, (8, LANES), 0) == 0) & (lax.broadcasted_iota(jnp.int32, (8, LANES), 1) == 0)
        lpart = jnp.where(cell, 0.5 * jnp.sum(jnp.mean(diff * diff, axis=-1, keepdims=True)), 0.0)

        @pl.when(pl.program_id(0) == 0)
        def _():
            dg_ref[...] = part
            loss_ref[...] = lpart

        @pl.when(pl.program_id(0) > 0)
        def _():
            dg_ref[...] += part
            loss_ref[...] += lpart

    row = pl.BlockSpec((tr, D), lambda i: (i, 0))
    vec = pl.BlockSpec((1, D), lambda i: (0, 0))
    return pl.pallas_call(
        body, name="loss_head", grid=(S // tr,), in_specs=[row, vec, row],
        out_specs=[pl.BlockSpec((8, LANES), lambda i: (0, 0)), row, row, vec],
        out_shape=[jax.ShapeDtypeStruct((8, LANES), F32), jax.ShapeDtypeStruct((S, D), F32),
                   jax.ShapeDtypeStruct((S, D), BF16), jax.ShapeDtypeStruct((1, D), F32)],
        compiler_params=pltpu.CompilerParams(dimension_semantics=("arbitrary",),
                                             vmem_limit_bytes=_vmem_limit(tr * D * 14, 6 * tr * D * 4)),
    )(x, g, target)


def _rope_tables(S):
    pos = jnp.arange(S, dtype=F32)
    inv_freq = ROPE_THETA ** (-jnp.arange(0, HEAD_DIM, 2, dtype=F32) / HEAD_DIM)
    ang = pos[:, None] * inv_freq[None, :]
    cos, sin = jnp.cos(ang), jnp.sin(ang)
    return jnp.concatenate([cos, cos], axis=-1), jnp.concatenate([-sin, sin], axis=-1)


def _rope_fwd(name, z, cosf, sinf):
    S = z.shape[0]
    tr = _pick(S, (256, 128, 64, 8))
    n_q = N_GROUPS * HEADS

    def body(z_ref, c_ref, s_ref, o_ref):
        c, s = c_ref[...], s_ref[...]
        for j in range(QKV_W // HEAD_DIM):
            t = z_ref[:, j * HEAD_DIM:(j + 1) * HEAD_DIM]
            if j < 2 * n_q:
                t = t * c + pltpu.roll(t, HEAD_DIM // 2, axis=1) * s
            if j < n_q:
                t = t * ATTN_SCALE
            o_ref[:, j * HEAD_DIM:(j + 1) * HEAD_DIM] = t.astype(BF16)

    tab = pl.BlockSpec((tr, HEAD_DIM), lambda i: (i, 0))
    return pl.pallas_call(
        body, name=name, grid=(S // tr,), in_specs=[pl.BlockSpec((tr, QKV_W), lambda i: (i, 0)), tab, tab],
        out_specs=pl.BlockSpec((tr, QKV_W), lambda i: (i, 0)), out_shape=jax.ShapeDtypeStruct((S, QKV_W), BF16),
        compiler_params=pltpu.CompilerParams(dimension_semantics=("parallel",),
                                             vmem_limit_bytes=_vmem_limit(tr * QKV_W * 6, tr * QKV_W * 4)),
    )(z, cosf, sinf)


def _rope_bwd(name, dq, dk, dv, dzuv, cosf, sinf, dz):
    S = dq.shape[0]
    tr = _pick(S, (256, 128, 64, 8))
    W3 = QKV_W // 3
    nh = W3 // HEAD_DIM
    wide = QKV_W + dzuv.shape[1]

    def body(dq_ref, dk_ref, dv_ref, uv_ref, c_ref, s_ref, dz_in, o_ref):
        c, s = c_ref[...], s_ref[...]
        for part, ref in enumerate((dq_ref, dk_ref)):
            for j in range(nh):
                t = ref[:, j * HEAD_DIM:(j + 1) * HEAD_DIM].astype(F32)
                t = t * c - pltpu.roll(t, HEAD_DIM // 2, axis=1) * s
                o_ref[:, part * W3 + j * HEAD_DIM: part * W3 + (j + 1) * HEAD_DIM] = t.astype(BF16)
        o_ref[:, 2 * W3:QKV_W] = dv_ref[...]
        o_ref[:, QKV_W:] = uv_ref[...]

    third = pl.BlockSpec((tr, W3), lambda i: (i, 0))
    tab = pl.BlockSpec((tr, HEAD_DIM), lambda i: (i, 0))
    return pl.pallas_call(
        body, name=name, grid=(S // tr,),
        in_specs=[third, third, third, pl.BlockSpec((tr, dzuv.shape[1]), lambda i: (i, 0)), tab, tab, pl.BlockSpec(memory_space=pl.ANY)],
        out_specs=pl.BlockSpec((tr, wide), lambda i: (i, 0)), out_shape=jax.ShapeDtypeStruct(dz.shape, dz.dtype),
        input_output_aliases={6: 0},
        compiler_params=pltpu.CompilerParams(dimension_semantics=("parallel",),
                                             vmem_limit_bytes=_vmem_limit(tr * wide * 4, tr * wide * 4)),
    )(dq, dk, dv, dzuv, cosf, sinf, dz)


ATTN_TQ = 256
ATTN_SCALE = HEAD_DIM ** -0.5
ATTN_PAD_MAX = RADIUS * max(d for _, d in ATTN_GROUPS)


def _band_bias(shape, q_axis, d):
    kq = lax.broadcasted_iota(jnp.int32, shape, 1 - q_axis) - lax.broadcasted_iota(jnp.int32, shape, q_axis) - RADIUS * d
    return jnp.where((jnp.abs(kq) <= RADIUS * d) & ((kq & (d - 1)) == 0), 0.0, NEG_INF).astype(F32)


def _fill_padded(dst, src, d, S):
    pad = RADIUS * d
    dst[0:pad, :] = jnp.zeros((pad, HEAD_DIM), dst.dtype)
    dst[pad:pad + S, :] = src[...]
    dst[pad + S:pad + S + pad, :] = jnp.zeros((pad, HEAD_DIM), dst.dtype)


_NT = (((1,), (1,)), ((), ()))


def _attn_fwd(name, qkv):
    S = qkv.shape[0]
    T = ATTN_TQ
    nq = N_GROUPS * HEADS
    widths = [T + 2 * RADIUS * d for _, d in ATTN_GROUPS]

    def body(*refs):
        q_refs, k_refs, v_refs = refs[0:3], refs[3:6], refs[6:9]
        o_ref, lc_ref = refs[9:11]
        kp, vp, bias = refs[11:14], refs[14:17], refs[17:20]
        i0 = pl.multiple_of(pl.program_id(1) * T, T)

        @pl.when(pl.program_id(1) == 0)
        def _():
            for g, (_, d) in enumerate(ATTN_GROUPS):
                _fill_padded(kp[g], k_refs[g], d, S)
                _fill_padded(vp[g], v_refs[g], d, S)
                bias[g][...] = _band_bias((T, widths[g]), 0, d)

        m = jnp.full((T, 1), NEG_INF, F32)
        l = jnp.zeros((T, 1), F32)
        acc = jnp.zeros((T, HEAD_DIM), F32)
        for g, (_, d) in enumerate(ATTN_GROUPS):
            W = widths[g]
            kw = kp[g][pl.ds(i0, W), :]
            vw = vp[g][pl.ds(i0, W), :]
            key = i0 - RADIUS * d + lax.broadcasted_iota(jnp.int32, (1, W), 1)
            in_seq = jnp.where((key >= 0) & (key < S), 0.0, NEG_INF).astype(F32)
            s = lax.dot_general(q_refs[g][...], kw, _NT, preferred_element_type=F32) + bias[g][...] + in_seq
            m_new = jnp.maximum(m, jnp.max(s, axis=1, keepdims=True))
            alpha = jnp.exp(m - m_new)
            p = jnp.exp(s - m_new)
            l = l * alpha + jnp.sum(p, axis=1, keepdims=True)
            acc = acc * alpha + jnp.dot(p.astype(BF16), vw, preferred_element_type=F32)
            m = m_new
        o_ref[...] = (acc / l).astype(BF16)
        lc_ref[...] = m + jnp.log(l)

    in_specs = [pl.BlockSpec((T, HEAD_DIM), lambda h, i, g=g: (i, g * HEADS + h)) for g in range(N_GROUPS)]
    in_specs += [pl.BlockSpec((S, HEAD_DIM), lambda h, i, g=g: (0, nq + g * HEADS + h)) for g in range(N_GROUPS)]
    in_specs += [pl.BlockSpec((S, HEAD_DIM), lambda h, i, g=g: (0, 2 * nq + g * HEADS + h)) for g in range(N_GROUPS)]
    padded = [pltpu.VMEM((S + 2 * RADIUS * d, HEAD_DIM), BF16) for _, d in ATTN_GROUPS]
    scratch = padded + padded + [pltpu.VMEM((T, W), F32) for W in widths]
    scratch_bytes = sum(2 * (S + 2 * RADIUS * d) * HEAD_DIM * 2 for _, d in ATTN_GROUPS) + sum(T * W * 4 for W in widths)
    return pl.pallas_call(
        body, name=name, grid=(HEADS, S // T), in_specs=in_specs,
        out_specs=[pl.BlockSpec((T, HEAD_DIM), lambda h, i: (i, h)), pl.BlockSpec((None, T, 1), lambda h, i: (h, i, 0))],
        out_shape=[jax.ShapeDtypeStruct((S, ATTN_W), BF16), jax.ShapeDtypeStruct((HEADS, S, 1), F32)],
        scratch_shapes=scratch,
        compiler_params=pltpu.CompilerParams(dimension_semantics=("parallel", "arbitrary"),
                                             vmem_limit_bytes=_vmem_limit(6 * S * HEAD_DIM * 2 + 8 * T * HEAD_DIM * 4,
                                                                          scratch_bytes + 4 * T * widths[-1] * 4)),
    )(*([qkv] * 9))


_TN = (((0,), (0,)), ((), ()))


def _attn_bwd(name, qkv, attn, dattn, lse_c):
    S = qkv.shape[0]
    T = ATTN_TQ
    nq = N_GROUPS * HEADS
    W3 = QKV_W // 3
    n_i = S // T
    wmax = T + 2 * ATTN_PAD_MAX
    s_pad = S + 2 * ATTN_PAD_MAX

    def body(q_ref, k_ref, v_ref, o_ref, do_ref, lc_ref, dq_ref, dk_ref, dv_ref, kp, vp, dk_acc, dv_acc, bias):
        g_id, i = pl.program_id(1), pl.program_id(2)
        i0 = pl.multiple_of(i * T, T)
        q, do = q_ref[...], do_ref[...]
        delta = jnp.sum(do.astype(F32) * o_ref[...].astype(F32), axis=1, keepdims=True)
        lse = lc_ref[...]

        def group(d):
            W, pad = T + 2 * RADIUS * d, RADIUS * d

            @pl.when(i == 0)
            def _():
                _fill_padded(kp, k_ref, d, S)
                _fill_padded(vp, v_ref, d, S)
                dk_acc[...] = jnp.zeros_like(dk_acc)
                dv_acc[...] = jnp.zeros_like(dv_acc)
                bias[:, 0:W] = _band_bias((T, W), 0, d)

            kw = kp[pl.ds(i0, W), :]
            vw = vp[pl.ds(i0, W), :]
            key = i0 - pad + lax.broadcasted_iota(jnp.int32, (1, W), 1)
            in_seq = jnp.where((key >= 0) & (key < S), 0.0, NEG_INF).astype(F32)
            s = lax.dot_general(q, kw, _NT, preferred_element_type=F32) + bias[:, 0:W] + in_seq
            p = jnp.exp(s - lse)
            dp = lax.dot_general(do, vw, _NT, preferred_element_type=F32)
            ds = (p * (dp - delta)).astype(BF16)
            dq_ref[...] = (jnp.dot(ds, kw, preferred_element_type=F32) * ATTN_SCALE).astype(BF16)
            dk_acc[pl.ds(i0, W), :] += lax.dot_general(ds, q, _TN, preferred_element_type=F32)
            dv_acc[pl.ds(i0, W), :] += lax.dot_general(p.astype(BF16), do, _TN, preferred_element_type=F32)

            @pl.when(i == n_i - 1)
            def _():
                dk_ref[...] = dk_acc[pad:pad + S, :].astype(BF16)
                dv_ref[...] = dv_acc[pad:pad + S, :].astype(BF16)

        for g, (_, d) in enumerate(ATTN_GROUPS):
            pl.when(g_id == g)(functools.partial(group, d))

    tile = lambda off: pl.BlockSpec((T, HEAD_DIM), lambda h, g, i: (i, off + g * HEADS + h))
    full = lambda off: pl.BlockSpec((S, HEAD_DIM), lambda h, g, i: (0, off + g * HEADS + h))
    headt = pl.BlockSpec((T, HEAD_DIM), lambda h, g, i: (i, h))
    scratch_bytes = 2 * s_pad * HEAD_DIM * (2 + 4) + T * wmax * 4
    return pl.pallas_call(
        body, name=name, grid=(HEADS, N_GROUPS, n_i),
        in_specs=[tile(0), full(nq), full(2 * nq), headt, headt, pl.BlockSpec((None, T, 1), lambda h, g, i: (h, i, 0))],
        out_specs=[tile(0), full(0), full(0)],
        out_shape=[jax.ShapeDtypeStruct((S, W3), BF16)] * 3,
        scratch_shapes=[pltpu.VMEM((s_pad, HEAD_DIM), BF16), pltpu.VMEM((s_pad, HEAD_DIM), BF16),
                        pltpu.VMEM((s_pad, HEAD_DIM), F32), pltpu.VMEM((s_pad, HEAD_DIM), F32), pltpu.VMEM((T, wmax), F32)],
        compiler_params=pltpu.CompilerParams(dimension_semantics=("parallel", "arbitrary", "arbitrary"),
                                             vmem_limit_bytes=_vmem_limit(4 * S * HEAD_DIM * 2 + 8 * T * HEAD_DIM * 4,
                                                                          scratch_bytes + 5 * T * wmax * 4)),
    )(qkv, qkv, qkv, attn, dattn, lse_c)


def _sg_parts(u, v, lng, lnb):
    gu = _gelu(u)
    gv = _gelu(v)
    mu = jnp.mean(gv, axis=-1, keepdims=True)
    xc = gv - mu
    rstd = lax.rsqrt(jnp.mean(xc * xc, axis=-1, keepdims=True) + NORM_EPS)
    xhat = xc * rstd
    vn = xhat * lng + lnb
    return gu, xhat, rstd, vn


def _sg_fwd(name, z, sg_w, sg_bc, lng, lnb, o_sg0):
    S = z.shape[0]
    T = SG_CHUNK
    cb = 512
    assert o_sg0 % cb == 0
    b0 = o_sg0 // cb

    def body(u0, u1, v0, v1, w_ref, b_ref, g_ref, be_ref, o_ref):
        u = jnp.concatenate([u0[...], u1[...]], axis=1)
        v = jnp.concatenate([v0[...], v1[...]], axis=1)
        gu, _, _, vn = _sg_parts(u, v, g_ref[...], be_ref[...])
        vnb = vn.astype(BF16)
        for g in range(SG_GROUPS):
            sl = slice(g * SG_CHUNK, (g + 1) * SG_CHUNK)
            mixed = jnp.dot(w_ref[g], vnb[:, sl], preferred_element_type=F32) + b_ref[g]
            o_ref[:, sl] = (gu[:, sl] * mixed).astype(BF16)

    zs = lambda k: pl.BlockSpec((T, cb), lambda i, k=k: (i, b0 + k))
    const3 = lambda shp: pl.BlockSpec(shp, lambda i: (0, 0, 0))
    vec = pl.BlockSpec((1, SG_W), lambda i: (0, 0))
    return pl.pallas_call(
        body, name=name, grid=(S // T,),
        in_specs=[zs(0), zs(1), zs(2), zs(3), const3((SG_GROUPS, SG_CHUNK, SG_CHUNK)), const3((SG_GROUPS, SG_CHUNK, 1)), vec, vec],
        out_specs=pl.BlockSpec((T, SG_W), lambda i: (i, 0)), out_shape=jax.ShapeDtypeStruct((S, SG_W), BF16),
        compiler_params=pltpu.CompilerParams(dimension_semantics=("parallel",), vmem_limit_bytes=_vmem_limit(4 * 1024 * 1024, 8 * T * SG_W * 4)),
    )(z, z, z, z, sg_w, sg_bc, lng, lnb)


def _sg_bwd(name, z, dsg, sg_w, sg_wt, sg_bc, lng, lnb, o_sg0):
    S = z.shape[0]
    T = SG_CHUNK
    cb = 512
    b0 = o_sg0 // cb

    def body(u0, u1, v0, v1, d_ref, w_ref, wt_ref, b_ref, g_ref, be_ref, dz_ref, dw_ref, db_ref, dg_ref, dbe_ref):
        i = pl.program_id(0)
        u = jnp.concatenate([u0[...], u1[...]], axis=1)
        v = jnp.concatenate([v0[...], v1[...]], axis=1)
        gu, xhat, rstd, vn = _sg_parts(u, v, g_ref[...], be_ref[...])
        vnb = vn.astype(BF16)
        dsg_v = d_ref[...].astype(F32)
        dmix = dsg_v * gu
        dmixb = dmix.astype(BF16)
        dvn_parts, mixed_parts, dw_parts, db_parts = [], [], [], []
        for g in range(SG_GROUPS):
            sl = slice(g * SG_CHUNK, (g + 1) * SG_CHUNK)
            mixed_parts.append(jnp.dot(w_ref[g], vnb[:, sl], preferred_element_type=F32) + b_ref[g])
            dvn_parts.append(jnp.dot(wt_ref[g], dmixb[:, sl], preferred_element_type=F32))
            dw_parts.append(lax.dot_general(dmixb[:, sl], vnb[:, sl], _NT, preferred_element_type=F32))
            db_parts.append(jnp.sum(dmix[:, sl], axis=1, keepdims=True))
        mixed = jnp.concatenate(mixed_parts, axis=1)
        dvn = jnp.concatenate(dvn_parts, axis=1)
        dzu = dsg_v * mixed * _gelu_grad(u)
        dxh = dvn * g_ref[...]
        dgv = rstd * (dxh - jnp.mean(dxh, axis=-1, keepdims=True) - xhat * jnp.mean(dxh * xhat, axis=-1, keepdims=True))
        dzv = dgv * _gelu_grad(v)
        dz_ref[:, :SG_W] = dzu.astype(BF16)
        dz_ref[:, SG_W:] = dzv.astype(BF16)
        dgp = jnp.sum(dvn * xhat, axis=0, keepdims=True)
        dbp = jnp.sum(dvn, axis=0, keepdims=True)

        @pl.when(i == 0)
        def _():
            for g in range(SG_GROUPS):
                dw_ref[g] = dw_parts[g]
                db_ref[g] = db_parts[g]
            dg_ref[...] = dgp
            dbe_ref[...] = dbp

        @pl.when(i > 0)
        def _():
            for g in range(SG_GROUPS):
                dw_ref[g] += dw_parts[g]
                db_ref[g] += db_parts[g]
            dg_ref[...] += dgp
            dbe_ref[...] += dbp

    zs = lambda k: pl.BlockSpec((T, cb), lambda i, k=k: (i, b0 + k))
    const3 = lambda shp: pl.BlockSpec(shp, lambda i: (0, 0, 0))
    vec = pl.BlockSpec((1, SG_W), lambda i: (0, 0))
    return pl.pallas_call(
        body, name=name, grid=(S // T,),
        in_specs=[zs(0), zs(1), zs(2), zs(3), pl.BlockSpec((T, SG_W), lambda i: (i, 0)),
                  const3((SG_GROUPS, SG_CHUNK, SG_CHUNK)), const3((SG_GROUPS, SG_CHUNK, SG_CHUNK)), const3((SG_GROUPS, SG_CHUNK, 1)),
                  vec, vec],
        out_specs=[pl.BlockSpec((T, 2 * SG_W), lambda i: (i, 0)), const3((SG_GROUPS, SG_CHUNK, SG_CHUNK)),
                   const3((SG_GROUPS, SG_CHUNK, 1)), vec, vec],
        out_shape=[jax.ShapeDtypeStruct((S, 2 * SG_W), BF16), jax.ShapeDtypeStruct((SG_GROUPS, SG_CHUNK, SG_CHUNK), F32),
                   jax.ShapeDtypeStruct((SG_GROUPS, SG_CHUNK, 1), F32), jax.ShapeDtypeStruct((1, SG_W), F32),
                   jax.ShapeDtypeStruct((1, SG_W), F32)],
        compiler_params=pltpu.CompilerParams(dimension_semantics=("arbitrary",),
                                             vmem_limit_bytes=_vmem_limit(6 * 1024 * 1024, 16 * T * SG_W * 4)),
    )(z, z, z, z, dsg, sg_w, sg_wt, sg_bc, lng, lnb)


def _gate_bwd(name, z, dmerged, y_attn, y_sg, o_g0, in_w):
    S, D = dmerged.shape
    tr = _pick(S, (512, 256, 128, 8))
    cb = _pick(D, (512, 256, 128))
    assert o_g0 % cb == 0
    nd = D // cb
    b0 = o_g0 // cb

    def body(z_ref, dm_ref, ya_ref, ys_ref, dz_ref, dy_ref):
        jj = pl.program_id(1)
        gate = _sigmoid(z_ref[...])
        dm = dm_ref[...].astype(F32)
        y = jnp.where(jj < nd, ya_ref[...], ys_ref[...]).astype(F32)
        dz_ref[...] = (dm * y * gate * (1.0 - gate)).astype(BF16)
        dy_ref[...] = (dm * gate).astype(BF16)

    half = pl.BlockSpec((tr, cb), lambda i, jj: (i, jj % nd))
    return pl.pallas_call(
        body, name=name, grid=(S // tr, 2 * nd),
        in_specs=[pl.BlockSpec((tr, cb), lambda i, jj: (i, b0 + jj)), half, half, half],
        out_specs=[pl.BlockSpec((tr, cb), lambda i, jj: (i, b0 + jj)), pl.BlockSpec((tr, cb), lambda i, jj: (i, jj))],
        out_shape=[jax.ShapeDtypeStruct((S, in_w), BF16), jax.ShapeDtypeStruct((S, 2 * D), BF16)],
        compiler_params=pltpu.CompilerParams(dimension_semantics=("parallel", "arbitrary"),
                                             vmem_limit_bytes=_vmem_limit(tr * cb * 14, 6 * tr * cb * 4)),
    )(z, dmerged, y_attn, y_sg)


def _row(v):
    return v.reshape(1, -1)


def _local_step(x, p, target, wf, small, after_group):
    S, D = x.shape
    L = p.shape[0]
    in_w = wf["w_in"][0].shape[1]
    ff = wf["w_ff_gate"][0].shape[1]
    ple = p.shape[2]
    o_sg0, o_g0 = QKV_W, QKV_W + 2 * SG_W
    cosf, sinf = _rope_tables(S)
    pb = p.astype(BF16)
    tmb = _pick(S, (1024, 512, 256))
    tn_in = _pick(in_w, (768, 1024, 512))
    tn_d = _pick(D, (1024, 512, 256))
    tn_g = _pick(D, (512, 256))
    tn_ff = _pick(ff, (512, 256))

    saved = []
    xs = x
    for i in range(L):
        sv = {"x0": xs}
        h = _rmsnorm_fwd(f"norm_mix_{i}", xs, _row(small["norm_mix"][i]))
        (z,) = _mm(f"in_proj_{i}", [dict(a=h, b=wf["w_in"], bl=i, mode="nn", K=D)], S, in_w,
                   [dict(shape=(S, in_w), dtype=F32)], _first, tm=tmb, tn=tn_in)
        qkv = _rope_fwd(f"rope_{i}", z, cosf, sinf)
        attn, lse_c = _attn_fwd(f"attn_{i}", qkv)
        sgw = small["sg_w"][i].astype(BF16)
        sgbc = small["sg_b"][i].reshape(SG_GROUPS, SG_CHUNK, 1)
        sg = _sg_fwd(f"sgu_{i}", z, sgw, sgbc, _row(small["sg_ln_g"][i]), _row(small["sg_ln_b"][i]), o_sg0)

        def merge(accs, tiles, rows):
            ya, ys = accs[0].astype(BF16), accs[1].astype(BF16)
            g0, g1 = _sigmoid(tiles[0]), _sigmoid(tiles[1])
            return [ya, ys, g0 * ya.astype(F32) + g1 * ys.astype(F32)]

        y_attn, y_sg, merged = _mm(
            f"branches_{i}",
            [dict(a=attn, b=wf["w_br_attn"], bl=i, mode="nn", K=ATTN_W), dict(a=sg, b=wf["w_br_sg"], bl=i, mode="nn", K=SG_W)],
            S, D, [dict(shape=(S, D), dtype=BF16)] * 3, merge,
            tiles=[dict(x=z, off=o_g0), dict(x=z, off=o_g0 + D)], tm=tmb, tn=tn_g)
        (x1,) = _mm(f"out_proj_{i}", [dict(a=merged, b=wf["w_out"], bl=i, mode="nn", K=D)], S, D,
                    [dict(shape=(S, D), dtype=F32)], lambda a, t, r: [t[0] + a[0]], tiles=[dict(x=xs)], tm=tmb, tn=tn_d)
        h2 = _rmsnorm_fwd(f"norm_ffn_{i}", x1, _row(small["norm_ffn"][i]))

        def swiglu(accs, tiles, rows):
            fg = accs[0].astype(BF16).astype(F32)
            fu = accs[1].astype(BF16).astype(F32)
            return [fg, fu, fg * _sigmoid(fg) * fu]

        ffg, ffu, act = _mm(
            f"ff_in_{i}",
            [dict(a=h2, b=wf["w_ff_gate"], bl=i, mode="nn", K=D), dict(a=h2, b=wf["w_ff_up"], bl=i, mode="nn", K=D)],
            S, ff, [dict(shape=(S, ff), dtype=BF16)] * 3, swiglu, tm=tmb, tn=tn_ff)
        (x2,) = _mm(f"ff_out_{i}", [dict(a=act, b=wf["w_ff_down"], bl=i, mode="nn", K=ff)], S, D,
                    [dict(shape=(S, D), dtype=F32)], lambda a, t, r: [t[0] + a[0]], tiles=[dict(x=x1)], tm=tmb, tn=tn_d)
        h3 = _rmsnorm_fwd(f"norm_ple_{i}", x2, _row(small["norm_ple"][i]))

        def ple_mix(accs, tiles, rows):
            gp = _sigmoid(accs[0]).astype(BF16)
            pe = accs[1].astype(BF16)
            return [tiles[0] + gp.astype(F32) * pe.astype(F32), gp, pe]

        x3, gp, pe = _mm(
            f"ple_{i}",
            [dict(a=h3, b=wf["w_ple_gate"], bl=i, mode="nn", K=D), dict(a=pb, al=i, b=wf["w_ple"], bl=i, mode="nn", K=ple)],
            S, D, [dict(shape=(S, D), dtype=F32), dict(shape=(S, D), dtype=BF16), dict(shape=(S, D), dtype=BF16)], ple_mix,
            tiles=[dict(x=x2)], tm=tmb, tn=tn_g)
        sv.update(h=h, z=z, qkv=qkv, attn=attn, lse_c=lse_c, sg=sg, y_attn=y_attn, y_sg=y_sg, merged=merged,
                  x1=x1, h2=h2, ffg=ffg, ffu=ffu, act=act, x2=x2, h3=h3, gp=gp, pe=pe, sgw=sgw, sgbc=sgbc)
        saved.append(sv)
        xs = x3

    loss_cell, dx, dxb, dg_final = _loss_head(xs, _row(small["norm_final"]), target)

    gw = {n: [None] * L for n in BIG}
    gs = {n: [None] * L for n in SMALL if n != "norm_final"}

    def dw(n, i, a, a_off, b, bn_off, K_rows, N_cols, tm, tn):
        (gw[n][i],) = _mm(f"d_{n}_{i}", [dict(a=a, b=b, mode="tn", K=S, a_off=a_off, bn_off=bn_off)], K_rows, N_cols,
                          [dict(shape=(K_rows, N_cols), dtype=BF16)], _first, tm=tm, tn=tn)

    for i in reversed(range(L)):
        sv = saved[i]
        dpre, dpe = _ew(f"ple_gate_bwd_{i}",
                        lambda d, g, e: [d * e.astype(F32) * g.astype(F32) * (1.0 - g.astype(F32)), d * g.astype(F32)],
                        [dx, sv["gp"], sv["pe"]], [BF16, BF16], S, D)
        (dh3,) = _mm(f"d_h3_{i}", [dict(a=dpre, b=wf["w_ple_gate"], bl=i, mode="nt", K=D)], S, D,
                     [dict(shape=(S, D), dtype=BF16)], _first, tm=tmb, tn=tn_d)
        dw("w_ple_gate", i, sv["h3"], 0, dpre, 0, D, D, tn_d, tn_d)
        dw("w_ple", i, pb[i], 0, dpe, 0, ple, D, _pick(ple, (256, 128)), _pick(D, (2048, 1024, 512, 256)))
        dx, dxb, gs["norm_ple"][i] = _rmsnorm_bwd(f"norm_ple_bwd_{i}", sv["x2"], _row(small["norm_ple"][i]), dh3, dx)
        def swiglu_bwd(accs, tiles, rows):
            da = accs[0].astype(BF16).astype(F32)
            fg, fu = tiles[0].astype(F32), tiles[1].astype(F32)
            sg_ = _sigmoid(fg)
            return [da * fu * (sg_ * (1.0 + fg * (1.0 - sg_))), da * (fg * sg_)]

        dffg, dffu = _mm(f"d_act_{i}", [dict(a=dxb, b=wf["w_ff_down"], bl=i, mode="nt", K=D)], S, ff,
                         [dict(shape=(S, ff), dtype=BF16)] * 2, swiglu_bwd, tiles=[dict(x=sv["ffg"]), dict(x=sv["ffu"])],
                         tm=tmb, tn=tn_ff)
        dw("w_ff_down", i, sv["act"], 0, dxb, 0, ff, D, tn_ff, _pick(D, (2048, 1024, 512, 256)))
        dw("w_ff_gate", i, sv["h2"], 0, dffg, 0, D, ff, _pick(D, (2048, 1024, 512, 256)), tn_ff)
        dw("w_ff_up", i, sv["h2"], 0, dffu, 0, D, ff, _pick(D, (2048, 1024, 512, 256)), tn_ff)
        (dffg, dffu), _ = lax.optimization_barrier(((dffg, dffu), after_group(i, "ffn", {n: gw[n][i] for n in GRAD_GROUPS["ffn"]})))
        (dh2,) = _mm(f"d_h2_{i}", [dict(a=dffg, b=wf["w_ff_gate"], bl=i, mode="nt", K=ff),
                                   dict(a=dffu, b=wf["w_ff_up"], bl=i, mode="nt", K=ff)], S, D,
                     [dict(shape=(S, D), dtype=BF16)], lambda a, t, r: [a[0] + a[1]], tm=tmb, tn=tn_d)
        dx, dxb, gs["norm_ffn"][i] = _rmsnorm_bwd(f"norm_ffn_bwd_{i}", sv["x1"], _row(small["norm_ffn"][i]), dh2, dx)
        (dmerged,) = _mm(f"d_merged_{i}", [dict(a=dxb, b=wf["w_out"], bl=i, mode="nt", K=D)], S, D,
                         [dict(shape=(S, D), dtype=BF16)], _first, tm=tmb, tn=tn_d)
        dw("w_out", i, sv["merged"], 0, dxb, 0, D, D, tn_d, tn_d)
        dz, dy = _gate_bwd(f"gate_bwd_{i}", sv["z"], dmerged, sv["y_attn"], sv["y_sg"], o_g0, in_w)
        (dattn,) = _mm(f"d_attn_{i}", [dict(a=dy, b=wf["w_br_attn"], bl=i, mode="nt", K=D)], S, ATTN_W,
                       [dict(shape=(S, ATTN_W), dtype=BF16)], _first, tm=tmb, tn=ATTN_W)
        (dsg,) = _mm(f"d_sg_{i}", [dict(a=dy, a_off=D, b=wf["w_br_sg"], bl=i, mode="nt", K=D)], S, SG_W,
                     [dict(shape=(S, SG_W), dtype=BF16)], _first, tm=tmb, tn=SG_W)
        dw("w_br_attn", i, sv["attn"], 0, dy, 0, ATTN_W, D, ATTN_W, _pick(D, (2048, 1024, 512, 256)))
        dw("w_br_sg", i, sv["sg"], 0, dy, D, SG_W, D, SG_W, _pick(D, (1024, 512, 256)))
        sgwt = jnp.swapaxes(small["sg_w"][i], 1, 2).astype(BF16)
        dzuv, gs["sg_w"][i], dsgb, dlg, dlb = _sg_bwd(f"sgu_bwd_{i}", sv["z"], dsg, sv["sgw"], sgwt, sv["sgbc"],
                                                      _row(small["sg_ln_g"][i]), _row(small["sg_ln_b"][i]), o_sg0)
        gs["sg_b"][i], gs["sg_ln_g"][i], gs["sg_ln_b"][i] = dsgb.reshape(SG_GROUPS, SG_CHUNK), dlg[0], dlb[0]
        dq, dk, dv = _attn_bwd(f"attn_bwd_{i}", sv["qkv"], sv["attn"], dattn, sv["lse_c"])
        dz = _rope_bwd(f"rope_bwd_{i}", dq, dk, dv, dzuv, cosf, sinf, dz)
        dw("w_in", i, sv["h"], 0, dz, 0, D, in_w, tn_d, tn_in)
        dz, _ = lax.optimization_barrier((dz, after_group(i, "mix", {n: gw[n][i] for n in GRAD_GROUPS["mix"]})))
        (dh,) = _mm(f"d_h_{i}", [dict(a=dz, b=wf["w_in"], bl=i, mode="nt", K=in_w)], S, D,
                    [dict(shape=(S, D), dtype=BF16)], _first, tm=tmb, tn=tn_d)
        dx, dxb, gs["norm_mix"][i] = _rmsnorm_bwd(f"norm_mix_bwd_{i}", sv["x0"], _row(small["norm_mix"][i]), dh, dx)

    gsmall ={n: jnp.stack([jnp.reshape(v, small[n].shape[1:]) for v in gs[n]]) for n in gs}
    gsmall["norm_final"] = dg_final[0]
    return loss_cell, dx, gsmall


def _place():
    x, y, c = lax.axis_index("x"), lax.axis_index("y"), lax.axis_index("c")
    return x, y, c, 2 * x + y


def _chip_of(s):
    return s // 2, s % 2


def _aligned(v, m):
    return v if isinstance(v, int) else pl.multiple_of(v, m)


def _piece(name, shape, s, c):
    K, N = shape
    if name in ROW_SHARDED or name == SMALL_BLOCKS:
        ks = K // 4
        return s * ks + c * (ks // 2), ks // 2, 0, N
    ns = N // 4
    return c * (K // 2), K // 2, s * ns, ns


def _handshake(peers):
    barrier = pltpu.get_barrier_semaphore()
    for peer in peers:
        pl.semaphore_signal(barrier, inc=1, device_id=peer, device_id_type=MESH)
    pl.semaphore_wait(barrier, len(peers))


def _gather_body(names, shapes, src, dst, send_sems, recv_sems, local_sems):
    n_w = len(names)
    x, y, c, s = _place()
    sib = (x, y, 1 - c)
    rel = [1, 2, 3]

    def where(w, ps, pc):
        r0, nr, c0, nc = _piece(names[w], shapes[names[w]], ps, pc)
        return dst[w].at[pl.ds(_aligned(r0, 16), nr), pl.ds(_aligned(c0, LANES), nc)]

    def copy(w, k, ps, pc, to, from_src=False):
        return pltpu.make_async_remote_copy(
            src_ref=src[w] if from_src else where(w, ps, pc), dst_ref=where(w, ps, pc),
            send_sem=send_sems.at[w, k], recv_sem=recv_sems.at[w, k], device_id=to, device_id_type=MESH)

    mine, first, passed = [], [], []
    for w in range(n_w):
        cp = pltpu.make_async_copy(src[w], where(w, s, c), local_sems.at[w])
        cp.start()
        mine.append(cp)
        first.append(copy(w, 0, s, c, sib, from_src=True))
        for j in rel:
            first.append(copy(w, j, s, c, (*_chip_of(s ^ j), c), from_src=True))
    for cp in first:
        cp.start()
    for w in range(n_w):
        for j in rel:
            copy(w, j, s ^ j, c, sib).wait_recv()
            fw = copy(w, 3 + j, s ^ j, c, sib)
            fw.start()
            passed.append(fw)
    for w in range(n_w):
        copy(w, 0, s, 1 - c, sib).wait_recv()
        for j in rel:
            copy(w, 3 + j, s ^ j, 1 - c, sib).wait_recv()
    for cp in first + passed:
        cp.wait_send()
    for cp in mine:
        cp.wait()


def _gather_sems(n_w):
    return (pltpu.SemaphoreType.DMA((n_w, 7)), pltpu.SemaphoreType.DMA((n_w, 7)), pltpu.SemaphoreType.DMA((n_w,)))


def _gather_peers():
    x, y, c, s = _place()
    return [(x, y, 1 - c)] + [(*_chip_of(s ^ j), c) for j in (1, 2, 3)]


def _gather_weights_async(name, pieces, shapes):
    names = list(pieces)
    n_w = len(names)
    src = [jax.new_ref(pieces[n], memory_space=pltpu.MemorySpace.HBM) for n in names]
    dst = [jax.empty_ref(jax.ShapeDtypeStruct(tuple(shapes[n]), pieces[n].dtype), memory_space=pltpu.MemorySpace.HBM)
           for n in names]

    @pl.kernel(mesh=plsc.ScalarSubcoreMesh(axis_name="seq", num_cores=1), name=name, scratch_types=_gather_sems(n_w),
               compiler_params=pltpu.CompilerParams(collective_id=GATHER_COLLECTIVE_ID))
    def launch(send_sems, recv_sems, local_sems):
        _handshake(_gather_peers())
        _gather_body(names, shapes, src, dst, send_sems, recv_sems, local_sems)

    launch()
    return {n: d[...] for n, d in zip(names, dst)}


def _halves_view(name, g):
    L, K, N = g.shape
    if name in ROW_SHARDED:
        return g.reshape(L * 4, 2, K // 8, N)
    return g.reshape(L, 2, K // 2, N)


def _all_peers():
    x, y, c, s = _place()
    return [(x, y, 1 - c)] + [(*_chip_of(s ^ j), h) for j in (1, 2, 3) for h in (0, 1)]


def _scatter_partials_async(name, views):
    names = list(views)
    n_w = len(names)
    src = [jax.new_ref(views[n], memory_space=pltpu.MemorySpace.HBM) for n in names]
    dst = [jax.empty_ref(jax.ShapeDtypeStruct(_partials_out_shape(n, views[n].shape), BF16), memory_space=pltpu.MemorySpace.HBM)
           for n in names]

    @pl.kernel(mesh=plsc.ScalarSubcoreMesh(axis_name="seq", num_cores=1), name=name, scratch_types=_partials_sems(n_w),
               compiler_params=pltpu.CompilerParams(collective_id=PARTIALS_COLLECTIVE_ID))
    def launch(send_sems, recv_sems):
        _handshake(_all_peers())
        _scatter_partials_body(names, src, dst, send_sems, recv_sems)

    launch()
    return {n: d[...] for n, d in zip(names, dst)}


def _partials_out_shape(name, v):
    return (7, 1, v[2], v[3] if name in ROW_SHARDED else v[3] // 4)


def _partials_sems(n_w):
    return (pltpu.SemaphoreType.DMA((n_w, 7)), pltpu.SemaphoreType.DMA((n_w, 7)))


def _scatter_partials_body(names, src, dst, send_sems, recv_sems):
    x, y, c, s = _place()

    def piece(w, t, h):
        if names[w] in ROW_SHARDED:
            return src[w].at[pl.ds(t, 1), h]
        ns = src[w].shape[3] // 4
        return src[w].at[:, h, :, pl.ds(pl.multiple_of(t * ns, LANES), ns)]

    sent = []
    for w in range(len(names)):
        for j in (1, 2, 3):
            for h in (0, 1):
                sent.append(pltpu.make_async_remote_copy(
                    src_ref=piece(w, s ^ j, h), dst_ref=dst[w].at[2 * (j - 1) + c], send_sem=send_sems.at[w, 2 * (j - 1) + h],
                    recv_sem=recv_sems.at[w, 2 * (j - 1) + c], device_id=(*_chip_of(s ^ j), h), device_id_type=MESH))
        sent.append(pltpu.make_async_remote_copy(
            src_ref=piece(w, s, 1 - c), dst_ref=dst[w].at[6], send_sem=send_sems.at[w, 6], recv_sem=recv_sems.at[w, 6],
            device_id=(x, y, 1 - c), device_id_type=MESH))
    for cp in sent:
        cp.start()
    for w in range(len(names)):
        for slot in range(7):
            pltpu.make_async_remote_copy(src_ref=dst[w].at[slot], dst_ref=dst[w].at[slot], send_sem=send_sems.at[w, slot],
                                         recv_sem=recv_sems.at[w, slot], device_id=(x, y, 1 - c), device_id_type=MESH).wait_recv()
    for cp in sent:
        cp.wait_send()


def _shard_sum_partials(name, view, parts, place, row_sharded, layer, n_layers, into):
    R, C = parts.shape[2:]
    tc = _pick(C, (2048, 1408, 1024, 896, 512, 384, 256, 128))
    tr = _pick(R, [t for t in (1024, 512, 256, 128, 64, 32, 16) if t * tc <= 2 * EW_TILE_ELEMS] + [8])

    def body(p_ref, own_ref, *rest):
        acc = own_ref[...].astype(F32) + rest[6][...].astype(F32)
        for k in range(6):
            acc = acc + rest[k][...].astype(F32)
        rest[-1][...] = acc

    if row_sharded:
        own_spec = pl.BlockSpec((None, None, tr, tc), lambda i, j, p: (p[1], p[0], i, j))
    else:
        own_spec = pl.BlockSpec((None, None, tr, tc), lambda i, j, p: (0, p[0], i, p[1] * (C // tc) + j))
    part = lambda k: pl.BlockSpec((None, None, tr, tc), lambda i, j, p, k=k: (k, 0, i, j))
    in_specs, args, aliases = [own_spec] + [part(k) for k in range(7)], [place, view] + [parts] * 7, {}
    if into is not None:
        in_specs.append(pl.BlockSpec(memory_space=pl.ANY))
        args.append(into)
        aliases = {9: 0}
    return pl.pallas_call(
        body, name=name, out_shape=jax.ShapeDtypeStruct((n_layers, 2, R, C), F32),
        grid_spec=pltpu.PrefetchScalarGridSpec(
            num_scalar_prefetch=1, grid=(R // tr, C // tc), in_specs=in_specs,
            out_specs=pl.BlockSpec((None, None, tr, tc), lambda i, j, p: (layer, p[0], i, j))),
        input_output_aliases=aliases,
        compiler_params=pltpu.CompilerParams(dimension_semantics=("parallel", "parallel"),
                                             vmem_limit_bytes=_vmem_limit(20 * tr * tc, 5 * tr * tc * 4)),
    )(*args)


def _share_halves(name, ghalf):
    names = list(ghalf)
    n_w = len(names)

    def body(*refs):
        src = refs[:n_w]
        dst = refs[n_w:2 * n_w]
        send_sems, recv_sems = refs[2 * n_w:]
        x, y, c, s = _place()
        remote = [pltpu.make_async_remote_copy(src_ref=src[w].at[:, c], dst_ref=dst[w].at[:, c], send_sem=send_sems.at[w],
                                               recv_sem=recv_sems.at[w], device_id=(x, y, 1 - c), device_id_type=MESH)
                  for w in range(n_w)]
        for cp in remote:
            cp.start()
        for cp in remote:
            cp.wait()

    anyspec = pl.BlockSpec(memory_space=pl.ANY)
    out = pl.pallas_call(
        body, name=name, in_specs=[anyspec] * n_w, out_specs=[anyspec] * n_w,
        out_shape=[jax.ShapeDtypeStruct(ghalf[n].shape, F32) for n in names],
        input_output_aliases={w: w for w in range(n_w)},
        scratch_shapes=[pltpu.SemaphoreType.DMA((n_w,)), pltpu.SemaphoreType.DMA((n_w,))],
    )(*[ghalf[n] for n in names])
    return dict(zip(names, out))


def _adamw_math(w, g, m, v):
    m = ADAM_B1 * m + (1.0 - ADAM_B1) * g
    v = ADAM_B2 * v + (1.0 - ADAM_B2) * (g * g)
    m_hat = m / (1.0 - ADAM_B1 ** ADAM_STEP)
    v_hat = v / (1.0 - ADAM_B2 ** ADAM_STEP)
    delta = -ADAM_LR * (m_hat / (jnp.sqrt(v_hat) + ADAM_EPS) + ADAM_WD * w)
    return delta, m, v


def _adamw(name, w, g, m, v):
    shape = w.shape
    C = shape[-1]
    R = math.prod(shape[:-1])
    f = lambda a: a.reshape(R, C)
    res = _ew(name, lambda w_, g_, m_, v_: [g_, *_adamw_math(w_, g_, m_, v_)], [f(w), f(g), f(m), f(v)], [F32] * 4, R, C)
    return [r.reshape(shape) for r in res]


def _pack_small(d):
    return jnp.concatenate([d[n].reshape(-1, LANES) for n in SMALL], axis=0)


def _unpack_small(flat, like):
    out, r = {}, 0
    for n in SMALL:
        k = like[n].size // LANES
        out[n] = flat[r:r + k].reshape(like[n].shape)
        r += k
    return out


def _small_update(gall, w, m, v):
    M = w.shape[0]
    tr = _pick(M, (552, 276, 184, 96, 48, 24, 8))

    def body(*refs):
        g = refs[0][...]
        for d in range(1, 8):
            g = g + refs[d][...]
        delta, nm, nv = _adamw_math(refs[8][...], g, refs[9][...], refs[10][...])
        refs[11][...] = g
        refs[12][...] = delta
        refs[13][...] = nm
        refs[14][...] = nv

    blk = pl.BlockSpec((tr, LANES), lambda i: (i, 0))
    in_specs = [pl.BlockSpec((tr, LANES), lambda i, d=d: (d * (M // tr) + i, 0)) for d in range(8)] + [blk] * 3
    return pl.pallas_call(
        body, name="small_update", grid=(M // tr,), in_specs=in_specs, out_specs=[blk] * 4,
        out_shape=[jax.ShapeDtypeStruct((M, LANES), F32)] * 4,
        compiler_params=pltpu.CompilerParams(dimension_semantics=("parallel",), vmem_limit_bytes=_vmem_limit(15 * tr * LANES * 4)),
    )(*([gall] * 8), w, m, v)


def _step(x, p, target, w, m, v):
    L = p.shape[0]
    x_i, y_i, c, s = _place()
    shapes = {}
    for n in BIG:
        _, K, N = w[n].shape
        shapes[n] = (4 * K, N) if n in ROW_SHARDED else (K, 4 * N)
    def pieces_of(i):
        return {n: lax.dynamic_slice_in_dim(w[n][i], c * (w[n].shape[1] // 2), w[n].shape[1] // 2, axis=0).astype(BF16)
                for n in BIG}

    def piece(n, i, after=None):
        wn = w[n] if after is None else lax.optimization_barrier((w[n], after))[0]
        return lax.dynamic_slice_in_dim(wn[i], c * (w[n].shape[1] // 2), w[n].shape[1] // 2, axis=0).astype(BF16)

    w_in0 = piece("w_in", 0)
    layers = [_gather_weights_async("gather_weights_0_w_in", {"w_in": w_in0}, shapes)]
    for i in range(L):
        if i > 0:
            layers.append({})
        for g in ("mix", "ffn"):
            mine = {n: piece(n, i, w_in0) for n in GRAD_GROUPS[g] if n not in layers[i]}
            layers[i].update(_gather_weights_async(f"gather_weights_{i}_{g}", mine, shapes))
    wf = {n: [layers[i][n] for i in range(L)] for n in BIG}
    small = {n: w[n] for n in SMALL}
    place = jnp.stack([c, s]).astype(jnp.int32)
    reduced = []

    def after_group(i, group, grads):
        views = {n: _halves_view(n, g[None]) for n, g in grads.items()}
        reduced.append((i, group, views, _scatter_partials_async(f"scatter_partials_{i}_{group}", views)))
        return views

    loss_cell, dx, gsmall = _local_step(x[0], p[:, 0], target[0], wf, small, after_group)
    loss = lax.psum(jnp.sum(loss_cell), ("x", "y", "c"))
    packed = _pack_small(gsmall)
    gall = _gather_weights_async("gather_small", {SMALL_BLOCKS: packed}, {SMALL_BLOCKS: (8 * packed.shape[0], LANES)})[SMALL_BLOCKS]
    ghalf = {n: None for n in BIG}
    grad, delta, new_m, new_v = {}, {}, {}, {}
    done = None

    def finish(group):
        gfull = _share_halves(f"share_halves_{group}", {n: ghalf[n] for n in GRAD_GROUPS[group]})
        for n in GRAD_GROUPS[group]:
            grad[n], delta[n], new_m[n], new_v[n] = _adamw(f"adamw_{n}", w[n], gfull[n].reshape(w[n].shape), m[n], v[n])
        return {n: delta[n] for n in GRAD_GROUPS[group]}

    for k, (i, group, own, parts) in enumerate(reduced):
        if k == len(reduced) - 1:
            done = finish("ffn")
        parts, _ = lax.optimization_barrier((parts, done))
        for n in own:
            ghalf[n] = _shard_sum_partials(f"shard_sum_{n}_{i}", own[n], parts[n], place, n in ROW_SHARDED, i, L, ghalf[n])
        done = {n: ghalf[n] for n in own}
    finish("mix")
    gall, _ = lax.optimization_barrier((gall, delta))
    gsum, dsm, nms, nvs =_small_update(gall, _pack_small(small), _pack_small({n: m[n] for n in SMALL}),
                                        _pack_small({n: v[n] for n in SMALL}))
    for dst, flat in ((grad, gsum), (delta, dsm), (new_m, nms), (new_v, nvs)):
        dst.update(_unpack_small(flat, small))
    return loss, dx[None], grad, delta, new_m, new_v


def kernel(x, p, w_in, w_br_attn, w_br_sg, w_out, sg_w, sg_b, sg_ln_g, sg_ln_b, norm_mix, norm_ffn, norm_ple, norm_final, w_ff_gate, w_ff_up, w_ff_down, w_ple_gate, w_ple, loss_target, m_w_in, m_w_br_attn, m_w_br_sg, m_w_out, m_sg_w, m_sg_b, m_sg_ln_g, m_sg_ln_b, m_norm_mix, m_norm_ffn, m_norm_ple, m_norm_final, m_w_ff_gate, m_w_ff_up, m_w_ff_down, m_w_ple_gate, m_w_ple, v_w_in, v_w_br_attn, v_w_br_sg, v_w_out, v_sg_w, v_sg_b, v_sg_ln_g, v_sg_ln_b, v_norm_mix, v_norm_ffn, v_norm_ple, v_norm_final, v_w_ff_gate, v_w_ff_up, v_w_ff_down, v_w_ple_gate, v_w_ple):
    w = dict(w_in=w_in, w_br_attn=w_br_attn, w_br_sg=w_br_sg, w_out=w_out, sg_w=sg_w, sg_b=sg_b, sg_ln_g=sg_ln_g, sg_ln_b=sg_ln_b,
             norm_mix=norm_mix, norm_ffn=norm_ffn, norm_ple=norm_ple, norm_final=norm_final, w_ff_gate=w_ff_gate, w_ff_up=w_ff_up,
             w_ff_down=w_ff_down, w_ple_gate=w_ple_gate, w_ple=w_ple)
    m = dict(w_in=m_w_in, w_br_attn=m_w_br_attn, w_br_sg=m_w_br_sg, w_out=m_w_out, sg_w=m_sg_w, sg_b=m_sg_b, sg_ln_g=m_sg_ln_g,
             sg_ln_b=m_sg_ln_b, norm_mix=m_norm_mix, norm_ffn=m_norm_ffn, norm_ple=m_norm_ple, norm_final=m_norm_final,
             w_ff_gate=m_w_ff_gate, w_ff_up=m_w_ff_up, w_ff_down=m_w_ff_down, w_ple_gate=m_w_ple_gate, w_ple=m_w_ple)
    v = dict(w_in=v_w_in, w_br_attn=v_w_br_attn, w_br_sg=v_w_br_sg, w_out=v_w_out, sg_w=v_sg_w, sg_b=v_sg_b, sg_ln_g=v_sg_ln_g,
             sg_ln_b=v_sg_ln_b, norm_mix=v_norm_mix, norm_ffn=v_norm_ffn, norm_ple=v_norm_ple, norm_final=v_norm_final,
             w_ff_gate=v_w_ff_gate, w_ff_up=v_w_ff_up, w_ff_down=v_w_ff_down, w_ple_gate=v_w_ple_gate, w_ple=v_w_ple)
    loss, grad_x, grad, delta, new_m, new_v = _step(x, p, loss_target, w, m, v)
    return (loss, grad_x, *[grad[n] for n in WEIGHTS], *[delta[n] for n in WEIGHTS], *[new_m[n] for n in WEIGHTS],
            *[new_v[n] for n in WEIGHTS])
```

```python
import functools
import math

import jax
import jax.numpy as jnp
from jax import lax
from jax.experimental import pallas as pl
from jax.experimental.pallas import tpu as pltpu
from jax.experimental.pallas import tpu_sc as plsc

F32 = jnp.float32
BF16 = jnp.bfloat16
MESH = pl.DeviceIdType.MESH

HEAD_DIM = 128
ATTN_GROUPS = ((128, 1), (512, 4), (2048, 16))
N_GROUPS = 3
HEADS = 4
QKV_W = 3 * N_GROUPS * HEADS * HEAD_DIM
ATTN_W = HEADS * HEAD_DIM
SG_CHUNK = 128
SG_GROUPS = 8
SG_W = 1024
RADIUS = 64
ROPE_THETA = 10000.0
NORM_EPS = 1e-6
NEG_INF = -1e30
ADAM_LR, ADAM_B1, ADAM_B2, ADAM_EPS, ADAM_WD, ADAM_STEP = 0.001, 0.9, 0.999, 1e-08, 0.01, 10

VMEM_CAP_V7X = 56 * 1024 * 1024
LANES = 128
EW_TILE_ELEMS = 256 * 1024
MM_VMEM_BUDGET = 44 * 1024 * 1024

GATHER_COLLECTIVE_ID = 1
PARTIALS_COLLECTIVE_ID = 2
SHARE_COLLECTIVE_ID = 3

BIG = ("w_in", "w_br_attn", "w_br_sg", "w_out", "w_ff_gate", "w_ff_up", "w_ff_down", "w_ple_gate", "w_ple")
ROW_SHARDED = ("w_out", "w_ff_down", "w_ple_gate")
SMALL_BLOCKS = "small_blocks"
GRAD_GROUPS = {"ffn": ("w_ple_gate", "w_ple", "w_ff_down", "w_ff_gate", "w_ff_up"), "mix": ("w_out", "w_br_attn", "w_br_sg", "w_in")}
SMALL = ("sg_w", "sg_b", "sg_ln_g", "sg_ln_b", "norm_mix", "norm_ffn", "norm_ple", "norm_final")
WEIGHTS = ("w_in", "w_br_attn", "w_br_sg", "w_out", "sg_w", "sg_b", "sg_ln_g", "sg_ln_b", "norm_mix", "norm_ffn",
           "norm_ple", "norm_final", "w_ff_gate", "w_ff_up", "w_ff_down", "w_ple_gate", "w_ple")


def _pick(n, prefs):
    for t in prefs:
        if n % t == 0:
            return t
    return n


def _vmem_limit(block_bytes, temp_bytes=0):
    est = 2 * block_bytes + temp_bytes
    assert est <= VMEM_CAP_V7X, est
    return VMEM_CAP_V7X


def _sigmoid(x):
    return 1.0 / (1.0 + jnp.exp(-x))


_GELU_C = math.sqrt(2.0 / math.pi)


def _gelu(x):
    return 0.5 * x * (1.0 + jnp.tanh(_GELU_C * (x + 0.044715 * (x * x * x))))


def _gelu_grad(x):
    t = jnp.tanh(_GELU_C * (x + 0.044715 * (x * x * x)))
    return 0.5 * (1.0 + t) + 0.5 * x * (1.0 - t * t) * (_GELU_C * (1.0 + 3.0 * 0.044715 * (x * x)))


def _lead(arr, l, blk, idx):
    if arr.ndim == 2:
        return pl.BlockSpec(blk, idx)
    return pl.BlockSpec((None,) + blk, lambda *g: (l,) + idx(*g))


def _k_steps(prods, tm, tn, fixed_bytes):
    for nk in range(1, 129):
        if any(p["K"] % nk or (p["K"] // nk) % LANES for p in prods):
            continue
        if 2 * sum((tm + tn) * (p["K"] // nk) * 2 for p in prods) + fixed_bytes <= MM_VMEM_BUDGET:
            return nk
    raise ValueError("no contraction split fits VMEM")


def _mm(name, prods, M, N, outs, epilogue, tiles=(), rows=(), tm=1024, tn=1024):
    assert M % tm == 0 and N % tn == 0, (name, M, N, tm, tn)
    fixed = 2 * tm * tn * (sum(t["x"].dtype.itemsize for t in tiles) + sum(jnp.dtype(o["dtype"]).itemsize for o in outs))
    fixed += (len(prods) + 2) * tm * tn * 4
    nk = _k_steps(prods, tm, tn, fixed)
    in_specs, args, block_bytes = [], [], 0
    for p in prods:
        if isinstance(p["b"], (list, tuple)):
            p["b"], p["bl"] = p["b"][p["bl"]], None
        K = p["K"]
        assert K % nk == 0, (name, K, nk)
        tk = K // nk
        p["tk"] = tk
        a_off, bk_off, bn_off = p.get("a_off", 0), p.get("bk_off", 0), p.get("bn_off", 0)
        assert bn_off % tn == 0 and bk_off % tk == 0
        if p["mode"] == "nn":
            assert a_off % tk == 0
            a_spec = _lead(p["a"], p.get("al"), (tm, tk), lambda i, j, k, o=a_off // tk: (i, o + k))
            b_spec = _lead(p["b"], p.get("bl"), (tk, tn), lambda i, j, k, ok=bk_off // tk, on=bn_off // tn: (ok + k, on + j))
        elif p["mode"] == "nt":
            assert a_off % tk == 0
            a_spec = _lead(p["a"], p.get("al"), (tm, tk), lambda i, j, k, o=a_off // tk: (i, o + k))
            b_spec = _lead(p["b"], p.get("bl"), (tn, tk), lambda i, j, k, ok=bk_off // tk, on=bn_off // tn: (on + j, ok + k))
        else:
            assert a_off % tm == 0
            a_spec = _lead(p["a"], p.get("al"), (tk, tm), lambda i, j, k, o=a_off // tm: (k, o + i))
            b_spec = _lead(p["b"], p.get("bl"), (tk, tn), lambda i, j, k, on=bn_off // tn: (k, on + j))
        in_specs += [a_spec, b_spec]
        args += [p["a"], p["b"]]
        block_bytes += (tm + tn) * tk * 2
    for t in tiles:
        off = t.get("off", 0)
        assert off % tn == 0
        in_specs.append(_lead(t["x"], t.get("l"), (tm, tn), lambda i, j, k, o=off // tn: (i, o + j)))
        args.append(t["x"])
        block_bytes += tm * tn * t["x"].dtype.itemsize
    for r in rows:
        in_specs.append(pl.BlockSpec((1, tn), lambda i, j, k: (0, j)))
        args.append(r)
    out_shapes, out_specs, aliases = [], [], {}
    for o_i, o in enumerate(outs):
        off = o.get("col_off", 0)
        assert off % tn == 0
        out_shapes.append(jax.ShapeDtypeStruct(o["shape"], o["dtype"]))
        idx = lambda i, j, k, oo=off // tn: (i, oo + j)
        if len(o["shape"]) == 2:
            out_specs.append(pl.BlockSpec((tm, tn), idx))
        else:
            out_specs.append(pl.BlockSpec((None, tm, tn), lambda i, j, k, l=o["l"], f=idx: (l,) + f(i, j, k)))
        if o.get("alias") is not None:
            aliases[len(args)] = o_i
            in_specs.append(pl.BlockSpec(memory_space=pl.ANY))
            args.append(o["alias"])
        block_bytes += tm * tn * jnp.dtype(o["dtype"]).itemsize
    n_p, n_t, n_r, n_o = len(prods), len(tiles), len(rows), len(outs)
    n_alias = len(aliases)
    modes = [p["mode"] for p in prods]

    def body(*refs):
        ab = refs[: 2 * n_p]
        t_refs = refs[2 * n_p: 2 * n_p + n_t]
        r_refs = refs[2 * n_p + n_t: 2 * n_p + n_t + n_r]
        o_refs = refs[2 * n_p + n_t + n_r + n_alias: 2 * n_p + n_t + n_r + n_alias + n_o]
        acc_refs = refs[2 * n_p + n_t + n_r + n_alias + n_o:]
        dims = {"nn": (((1,), (0,)), ((), ())), "nt": (((1,), (1,)), ((), ())), "tn": (((0,), (0,)), ((), ()))}

        def part(q):
            return lax.dot_general(ab[2 * q][...], ab[2 * q + 1][...], dims[modes[q]], preferred_element_type=F32)

        def finish(accs):
            res = epilogue(accs, [t[...] for t in t_refs], [r[...] for r in r_refs])
            for o_ref, val in zip(o_refs, res, strict=True):
                o_ref[...] = val.astype(o_ref.dtype)

        if nk == 1:
            finish([part(q) for q in range(n_p)])
        else:
            k = pl.program_id(2)

            @pl.when(k == 0)
            def _():
                for q, acc in enumerate(acc_refs):
                    acc[...] = part(q)

            @pl.when(k > 0)
            def _():
                for q, acc in enumerate(acc_refs):
                    acc[...] += part(q)

            @pl.when(k == nk - 1)
            def _():
                finish([acc[...] for acc in acc_refs])

    scratch = [pltpu.VMEM((tm, tn), F32) for _ in prods] if nk > 1 else []
    temp = (n_p + 2) * tm * tn * 4
    res = pl.pallas_call(
        body, name=name, grid=(M // tm, N // tn, nk), in_specs=in_specs, out_specs=out_specs, out_shape=out_shapes,
        scratch_shapes=scratch, input_output_aliases=aliases,
        compiler_params=pltpu.CompilerParams(dimension_semantics=("parallel", "parallel", "arbitrary"),
                                             vmem_limit_bytes=_vmem_limit(block_bytes, temp)),
    )(*args)
    return res


def _first(accs, tiles, rows):
    return [accs[0]]


def _ew(name, fn, ins, outs, R, C, tr=None, tc=None):
    tc = tc or _pick(C, (2048, 1536, 1408, 1024, 896, 512, 384, 256, 128))
    tr = tr or _pick(R, [t for t in (512, 256, 128, 64, 32, 16) if t * tc <= EW_TILE_ELEMS] + [8])
    in_specs, args, bb = [], [], 0
    for arr in ins:
        in_specs.append(pl.BlockSpec((tr, tc), lambda i, j: (i, j)))
        args.append(arr)
        bb += tr * tc * arr.dtype.itemsize
    out_shapes = [jax.ShapeDtypeStruct((R, C), d) for d in outs]
    out_specs = [pl.BlockSpec((tr, tc), lambda i, j: (i, j)) for _ in outs]
    bb += sum(tr * tc * jnp.dtype(d).itemsize for d in outs)
    n_in = len(ins)

    def body(*refs):
        res = fn(*[r[...] for r in refs[:n_in]])
        for o_ref, val in zip(refs[n_in:], res, strict=True):
            o_ref[...] = val.astype(o_ref.dtype)

    return pl.pallas_call(
        body, name=name, grid=(R // tr, C // tc), in_specs=in_specs, out_specs=out_specs, out_shape=out_shapes,
        compiler_params=pltpu.CompilerParams(dimension_semantics=("parallel", "parallel"),
                                             vmem_limit_bytes=_vmem_limit(bb, 6 * tr * tc * 4)),
    )(*args)


def _rmsnorm_fwd(name, x, g):
    S, D = x.shape
    tr = _pick(S, (256, 128, 64, 8))

    def body(x_ref, g_ref, h_ref):
        xv = x_ref[...]
        r = lax.rsqrt(jnp.mean(xv * xv, axis=-1, keepdims=True) + NORM_EPS)
        h_ref[...] = (xv * r * g_ref[...]).astype(BF16)

    return pl.pallas_call(
        body, name=name, grid=(S // tr,),
        in_specs=[pl.BlockSpec((tr, D), lambda i: (i, 0)), pl.BlockSpec((1, D), lambda i: (0, 0))],
        out_specs=pl.BlockSpec((tr, D), lambda i: (i, 0)), out_shape=jax.ShapeDtypeStruct((S, D), BF16),
        compiler_params=pltpu.CompilerParams(dimension_semantics=("parallel",),
                                             vmem_limit_bytes=_vmem_limit(tr * D * 6, 3 * tr * D * 4)),
    )(x, g)


def _rmsnorm_bwd(name, x, g, dh, dres):
    S, D = x.shape
    tr = _pick(S, (256, 128, 64, 8))

    def body(x_ref, g_ref, dh_ref, dres_ref, dx_ref, dxb_ref, dg_ref):
        xv = x_ref[...]
        dy = dh_ref[...].astype(F32)
        r = lax.rsqrt(jnp.mean(xv * xv, axis=-1, keepdims=True) + NORM_EPS)
        a = dy * g_ref[...]
        dx = dres_ref[...] + r * a - xv * (r * r * r) * jnp.mean(a * xv, axis=-1, keepdims=True)
        dx_ref[...] = dx
        dxb_ref[...] = dx.astype(BF16)
        part = jnp.sum(dy * xv * r, axis=0, keepdims=True)

        @pl.when(pl.program_id(0) == 0)
        def _():
            dg_ref[...] = part

        @pl.when(pl.program_id(0) > 0)
        def _():
            dg_ref[...] += part

    row = pl.BlockSpec((tr, D), lambda i: (i, 0))
    vec = pl.BlockSpec((1, D), lambda i: (0, 0))
    return pl.pallas_call(
        body, name=name, grid=(S // tr,), in_specs=[row, vec, row, row], out_specs=[row, row, vec],
        out_shape=[jax.ShapeDtypeStruct((S, D), F32), jax.ShapeDtypeStruct((S, D), BF16), jax.ShapeDtypeStruct((1, D), F32)],
        compiler_params=pltpu.CompilerParams(dimension_semantics=("arbitrary",),
                                             vmem_limit_bytes=_vmem_limit(tr * D * 18, 5 * tr * D * 4)),
    )(x, g, dh, dres)


def _loss_head(x, g, target):
    S, D = x.shape
    tr = _pick(S, (256, 128, 64, 8))

    def body(x_ref, g_ref, t_ref, loss_ref, dx_ref, dxb_ref, dg_ref):
        xv = x_ref[...]
        r = lax.rsqrt(jnp.mean(xv * xv, axis=-1, keepdims=True) + NORM_EPS)
        xn = xv * r
        diff = xn * g_ref[...] - t_ref[...]
        dy = diff * (1.0 / D)
        a = dy * g_ref[...]
        dx = r * a - xv * (r * r * r) * jnp.mean(a * xv, axis=-1, keepdims=True)
        dx_ref[...] = dx
        dxb_ref[...] = dx.astype(BF16)
        part = jnp.sum(dy * xn, axis=0, keepdims=True)
        cell = (lax.broadcasted_iota(jnp.int32, (8, LANES), 0) == 0) & (lax.broadcasted_iota(jnp.int32, (8, LANES), 1) == 0)
        lpart = jnp.where(cell, 0.5 * jnp.sum(jnp.mean(diff * diff, axis=-1, keepdims=True)), 0.0)

        @pl.when(pl.program_id(0) == 0)
        def _():
            dg_ref[...] = part
            loss_ref[...] = lpart

        @pl.when(pl.program_id(0) > 0)
        def _():
            dg_ref[...] += part
            loss_ref[...] += lpart

    row = pl.BlockSpec((tr, D), lambda i: (i, 0))
    vec = pl.BlockSpec((1, D), lambda i: (0, 0))
    return pl.pallas_call(
        body, name="loss_head", grid=(S // tr,), in_specs=[row, vec, row],
        out_specs=[pl.BlockSpec((8, LANES), lambda i: (0, 0)), row, row, vec],
        out_shape=[jax.ShapeDtypeStruct((8, LANES), F32), jax.ShapeDtypeStruct((S, D), F32),
                   jax.ShapeDtypeStruct((S, D), BF16), jax.ShapeDtypeStruct((1, D), F32)],
        compiler_params=pltpu.CompilerParams(dimension_semantics=("arbitrary",),
                                             vmem_limit_bytes=_vmem_limit(tr * D * 14, 6 * tr * D * 4)),
    )(x, g, target)


def _rope_tables(S):
    pos = jnp.arange(S, dtype=F32)
    inv_freq = ROPE_THETA ** (-jnp.arange(0, HEAD_DIM, 2, dtype=F32) / HEAD_DIM)
    ang = pos[:, None] * inv_freq[None, :]
    cos, sin = jnp.cos(ang), jnp.sin(ang)
    return jnp.concatenate([cos, cos], axis=-1), jnp.concatenate([-sin, sin], axis=-1)


def _rope_fwd(name, z, cosf, sinf):
    S = z.shape[0]
    tr = _pick(S, (256, 128, 64, 8))
    n_q = N_GROUPS * HEADS

    def body(z_ref, c_ref, s_ref, o_ref):
        c, s = c_ref[...], s_ref[...]
        for j in range(QKV_W // HEAD_DIM):
            t = z_ref[:, j * HEAD_DIM:(j + 1) * HEAD_DIM]
            if j < 2 * n_q:
                t = t * c + pltpu.roll(t, HEAD_DIM // 2, axis=1) * s
            if j < n_q:
                t = t * ATTN_SCALE
            o_ref[:, j * HEAD_DIM:(j + 1) * HEAD_DIM] = t.astype(BF16)

    tab = pl.BlockSpec((tr, HEAD_DIM), lambda i: (i, 0))
    return pl.pallas_call(
        body, name=name, grid=(S // tr,), in_specs=[pl.BlockSpec((tr, QKV_W), lambda i: (i, 0)), tab, tab],
        out_specs=pl.BlockSpec((tr, QKV_W), lambda i: (i, 0)), out_shape=jax.ShapeDtypeStruct((S, QKV_W), BF16),
        compiler_params=pltpu.CompilerParams(dimension_semantics=("parallel",),
                                             vmem_limit_bytes=_vmem_limit(tr * QKV_W * 6, tr * QKV_W * 4)),
    )(z, cosf, sinf)


def _rope_bwd(name, dq, dk, dv, dzuv, cosf, sinf, dz):
    S = dq.shape[0]
    tr = _pick(S, (256, 128, 64, 8))
    W3 = QKV_W // 3
    nh = W3 // HEAD_DIM
    wide = QKV_W + dzuv.shape[1]

    def body(dq_ref, dk_ref, dv_ref, uv_ref, c_ref, s_ref, dz_in, o_ref):
        c, s = c_ref[...], s_ref[...]
        for part, ref in enumerate((dq_ref, dk_ref)):
            for j in range(nh):
                t = ref[:, j * HEAD_DIM:(j + 1) * HEAD_DIM].astype(F32)
                t = t * c - pltpu.roll(t, HEAD_DIM // 2, axis=1) * s
                o_ref[:, part * W3 + j * HEAD_DIM: part * W3 + (j + 1) * HEAD_DIM] = t.astype(BF16)
        o_ref[:, 2 * W3:QKV_W] = dv_ref[...]
        o_ref[:, QKV_W:] = uv_ref[...]

    third = pl.BlockSpec((tr, W3), lambda i: (i, 0))
    tab = pl.BlockSpec((tr, HEAD_DIM), lambda i: (i, 0))
    return pl.pallas_call(
        body, name=name, grid=(S // tr,),
        in_specs=[third, third, third, pl.BlockSpec((tr, dzuv.shape[1]), lambda i: (i, 0)), tab, tab, pl.BlockSpec(memory_space=pl.ANY)],
        out_specs=pl.BlockSpec((tr, wide), lambda i: (i, 0)), out_shape=jax.ShapeDtypeStruct(dz.shape, dz.dtype),
        input_output_aliases={6: 0},
        compiler_params=pltpu.CompilerParams(dimension_semantics=("parallel",),
                                             vmem_limit_bytes=_vmem_limit(tr * wide * 4, tr * wide * 4)),
    )(dq, dk, dv, dzuv, cosf, sinf, dz)


ATTN_TQ = 256
ATTN_SCALE = HEAD_DIM ** -0.5
ATTN_PAD_MAX = RADIUS * max(d for _, d in ATTN_GROUPS)


def _band_bias(shape, q_axis, d):
    kq = lax.broadcasted_iota(jnp.int32, shape, 1 - q_axis) - lax.broadcasted_iota(jnp.int32, shape, q_axis) - RADIUS * d
    return jnp.where((jnp.abs(kq) <= RADIUS * d) & ((kq & (d - 1)) == 0), 0.0, NEG_INF).astype(F32)


def _fill_padded(dst, src, d, S):
    pad = RADIUS * d
    dst[0:pad, :] = jnp.zeros((pad, HEAD_DIM), dst.dtype)
    dst[pad:pad + S, :] = src[...]
    dst[pad + S:pad + S + pad, :] = jnp.zeros((pad, HEAD_DIM), dst.dtype)


_NT = (((1,), (1,)), ((), ()))


def _attn_fwd(name, qkv):
    S = qkv.shape[0]
    T = ATTN_TQ
    nq = N_GROUPS * HEADS
    widths = [T + 2 * RADIUS * d for _, d in ATTN_GROUPS]

    def body(*refs):
        q_refs, k_refs, v_refs = refs[0:3], refs[3:6], refs[6:9]
        o_ref, lc_ref = refs[9:11]
        kp, vp, bias = refs[11:14], refs[14:17], refs[17:20]
        i0 = pl.multiple_of(pl.program_id(1) * T, T)

        @pl.when(pl.program_id(1) == 0)
        def _():
            for g, (_, d) in enumerate(ATTN_GROUPS):
                _fill_padded(kp[g], k_refs[g], d, S)
                _fill_padded(vp[g], v_refs[g], d, S)
                bias[g][...] = _band_bias((T, widths[g]), 0, d)

        m = jnp.full((T, 1), NEG_INF, F32)
        l = jnp.zeros((T, 1), F32)
        acc = jnp.zeros((T, HEAD_DIM), F32)
        for g, (_, d) in enumerate(ATTN_GROUPS):
            W = widths[g]
            kw = kp[g][pl.ds(i0, W), :]
            vw = vp[g][pl.ds(i0, W), :]
            key = i0 - RADIUS * d + lax.broadcasted_iota(jnp.int32, (1, W), 1)
            in_seq = jnp.where((key >= 0) & (key < S), 0.0, NEG_INF).astype(F32)
            s = lax.dot_general(q_refs[g][...], kw, _NT, preferred_element_type=F32) + bias[g][...] + in_seq
            m_new = jnp.maximum(m, jnp.max(s, axis=1, keepdims=True))
            alpha = jnp.exp(m - m_new)
            p = jnp.exp(s - m_new)
            l = l * alpha + jnp.sum(p, axis=1, keepdims=True)
            acc = acc * alpha + jnp.dot(p.astype(BF16), vw, preferred_element_type=F32)
            m = m_new
        o_ref[...] = (acc / l).astype(BF16)
        lc_ref[...] = m + jnp.log(l)

    in_specs = [pl.BlockSpec((T, HEAD_DIM), lambda h, i, g=g: (i, g * HEADS + h)) for g in range(N_GROUPS)]
    in_specs += [pl.BlockSpec((S, HEAD_DIM), lambda h, i, g=g: (0, nq + g * HEADS + h)) for g in range(N_GROUPS)]
    in_specs += [pl.BlockSpec((S, HEAD_DIM), lambda h, i, g=g: (0, 2 * nq + g * HEADS + h)) for g in range(N_GROUPS)]
    padded = [pltpu.VMEM((S + 2 * RADIUS * d, HEAD_DIM), BF16) for _, d in ATTN_GROUPS]
    scratch = padded + padded + [pltpu.VMEM((T, W), F32) for W in widths]
    scratch_bytes = sum(2 * (S + 2 * RADIUS * d) * HEAD_DIM * 2 for _, d in ATTN_GROUPS) + sum(T * W * 4 for W in widths)
    return pl.pallas_call(
        body, name=name, grid=(HEADS, S // T), in_specs=in_specs,
        out_specs=[pl.BlockSpec((T, HEAD_DIM), lambda h, i: (i, h)), pl.BlockSpec((None, T, 1), lambda h, i: (h, i, 0))],
        out_shape=[jax.ShapeDtypeStruct((S, ATTN_W), BF16), jax.ShapeDtypeStruct((HEADS, S, 1), F32)],
        scratch_shapes=scratch,
        compiler_params=pltpu.CompilerParams(dimension_semantics=("parallel", "arbitrary"),
                                             vmem_limit_bytes=_vmem_limit(6 * S * HEAD_DIM * 2 + 8 * T * HEAD_DIM * 4,
                                                                          scratch_bytes + 4 * T * widths[-1] * 4)),
    )(*([qkv] * 9))


_TN = (((0,), (0,)), ((), ()))


def _attn_bwd(name, qkv, attn, dattn, lse_c):
    S = qkv.shape[0]
    T = ATTN_TQ
    nq = N_GROUPS * HEADS
    W3 = QKV_W // 3
    n_i = S // T
    wmax = T + 2 * ATTN_PAD_MAX
    s_pad = S + 2 * ATTN_PAD_MAX

    def body(q_ref, k_ref, v_ref, o_ref, do_ref, lc_ref, dq_ref, dk_ref, dv_ref, kp, vp, dk_acc, dv_acc, bias):
        g_id, i = pl.program_id(1), pl.program_id(2)
        i0 = pl.multiple_of(i * T, T)
        q, do = q_ref[...], do_ref[...]
        delta = jnp.sum(do.astype(F32) * o_ref[...].astype(F32), axis=1, keepdims=True)
        lse = lc_ref[...]

        def group(d):
            W, pad = T + 2 * RADIUS * d, RADIUS * d

            @pl.when(i == 0)
            def _():
                _fill_padded(kp, k_ref, d, S)
                _fill_padded(vp, v_ref, d, S)
                dk_acc[...] = jnp.zeros_like(dk_acc)
                dv_acc[...] = jnp.zeros_like(dv_acc)
                bias[:, 0:W] = _band_bias((T, W), 0, d)

            kw = kp[pl.ds(i0, W), :]
            vw = vp[pl.ds(i0, W), :]
            key = i0 - pad + lax.broadcasted_iota(jnp.int32, (1, W), 1)
            in_seq = jnp.where((key >= 0) & (key < S), 0.0, NEG_INF).astype(F32)
            s = lax.dot_general(q, kw, _NT, preferred_element_type=F32) + bias[:, 0:W] + in_seq
            p = jnp.exp(s - lse)
            dp = lax.dot_general(do, vw, _NT, preferred_element_type=F32)
            ds = (p * (dp - delta)).astype(BF16)
            dq_ref[...] = (jnp.dot(ds, kw, preferred_element_type=F32) * ATTN_SCALE).astype(BF16)
            dk_acc[pl.ds(i0, W), :] += lax.dot_general(ds, q, _TN, preferred_element_type=F32)
            dv_acc[pl.ds(i0, W), :] += lax.dot_general(p.astype(BF16), do, _TN, preferred_element_type=F32)

            @pl.when(i == n_i - 1)
            def _():
                dk_ref[...] = dk_acc[pad:pad + S, :].astype(BF16)
                dv_ref[...] = dv_acc[pad:pad + S, :].astype(BF16)

        for g, (_, d) in enumerate(ATTN_GROUPS):
            pl.when(g_id == g)(functools.partial(group, d))

    tile = lambda off: pl.BlockSpec((T, HEAD_DIM), lambda h, g, i: (i, off + g * HEADS + h))
    full = lambda off: pl.BlockSpec((S, HEAD_DIM), lambda h, g, i: (0, off + g * HEADS + h))
    headt = pl.BlockSpec((T, HEAD_DIM), lambda h, g, i: (i, h))
    scratch_bytes = 2 * s_pad * HEAD_DIM * (2 + 4) + T * wmax * 4
    return pl.pallas_call(
        body, name=name, grid=(HEADS, N_GROUPS, n_i),
        in_specs=[tile(0), full(nq), full(2 * nq), headt, headt, pl.BlockSpec((None, T, 1), lambda h, g, i: (h, i, 0))],
        out_specs=[tile(0), full(0), full(0)],
        out_shape=[jax.ShapeDtypeStruct((S, W3), BF16)] * 3,
        scratch_shapes=[pltpu.VMEM((s_pad, HEAD_DIM), BF16), pltpu.VMEM((s_pad, HEAD_DIM), BF16),
                        pltpu.VMEM((s_pad, HEAD_DIM), F32), pltpu.VMEM((s_pad, HEAD_DIM), F32), pltpu.VMEM((T, wmax), F32)],
        compiler_params=pltpu.CompilerParams(dimension_semantics=("parallel", "arbitrary", "arbitrary"),
                                             vmem_limit_bytes=_vmem_limit(4 * S * HEAD_DIM * 2 + 8 * T * HEAD_DIM * 4,
                                                                          scratch_bytes + 5 * T * wmax * 4)),
    )(qkv, qkv, qkv, attn, dattn, lse_c)


def _sg_parts(u, v, lng, lnb):
    gu = _gelu(u)
    gv = _gelu(v)
    mu = jnp.mean(gv, axis=-1, keepdims=True)
    xc = gv - mu
    rstd = lax.rsqrt(jnp.mean(xc * xc, axis=-1, keepdims=True) + NORM_EPS)
    xhat = xc * rstd
    vn = xhat * lng + lnb
    return gu, xhat, rstd, vn


def _sg_fwd(name, z, sg_w, sg_bc, lng, lnb, o_sg0):
    S = z.shape[0]
    T = SG_CHUNK
    cb = 512
    assert o_sg0 % cb == 0
    b0 = o_sg0 // cb

    def body(u0, u1, v0, v1, w_ref, b_ref, g_ref, be_ref, o_ref):
        u = jnp.concatenate([u0[...], u1[...]], axis=1)
        v = jnp.concatenate([v0[...], v1[...]], axis=1)
        gu, _, _, vn = _sg_parts(u, v, g_ref[...], be_ref[...])
        vnb = vn.astype(BF16)
        for g in range(SG_GROUPS):
            sl = slice(g * SG_CHUNK, (g + 1) * SG_CHUNK)
            mixed = jnp.dot(w_ref[g], vnb[:, sl], preferred_element_type=F32) + b_ref[g]
            o_ref[:, sl] = (gu[:, sl] * mixed).astype(BF16)

    zs = lambda k: pl.BlockSpec((T, cb), lambda i, k=k: (i, b0 + k))
    const3 = lambda shp: pl.BlockSpec(shp, lambda i: (0, 0, 0))
    vec = pl.BlockSpec((1, SG_W), lambda i: (0, 0))
    return pl.pallas_call(
        body, name=name, grid=(S // T,),
        in_specs=[zs(0), zs(1), zs(2), zs(3), const3((SG_GROUPS, SG_CHUNK, SG_CHUNK)), const3((SG_GROUPS, SG_CHUNK, 1)), vec, vec],
        out_specs=pl.BlockSpec((T, SG_W), lambda i: (i, 0)), out_shape=jax.ShapeDtypeStruct((S, SG_W), BF16),
        compiler_params=pltpu.CompilerParams(dimension_semantics=("parallel",), vmem_limit_bytes=_vmem_limit(4 * 1024 * 1024, 8 * T * SG_W * 4)),
    )(z, z, z, z, sg_w, sg_bc, lng, lnb)


def _sg_bwd(name, z, dsg, sg_w, sg_wt, sg_bc, lng, lnb, o_sg0):
    S = z.shape[0]
    T = SG_CHUNK
    cb = 512
    b0 = o_sg0 // cb

    def body(u0, u1, v0, v1, d_ref, w_ref, wt_ref, b_ref, g_ref, be_ref, dz_ref, dw_ref, db_ref, dg_ref, dbe_ref):
        i = pl.program_id(0)
        u = jnp.concatenate([u0[...], u1[...]], axis=1)
        v = jnp.concatenate([v0[...], v1[...]], axis=1)
        gu, xhat, rstd, vn = _sg_parts(u, v, g_ref[...], be_ref[...])
        vnb = vn.astype(BF16)
        dsg_v = d_ref[...].astype(F32)
        dmix = dsg_v * gu
        dmixb = dmix.astype(BF16)
        dvn_parts, mixed_parts, dw_parts, db_parts = [], [], [], []
        for g in range(SG_GROUPS):
            sl = slice(g * SG_CHUNK, (g + 1) * SG_CHUNK)
            mixed_parts.append(jnp.dot(w_ref[g], vnb[:, sl], preferred_element_type=F32) + b_ref[g])
            dvn_parts.append(jnp.dot(wt_ref[g], dmixb[:, sl], preferred_element_type=F32))
            dw_parts.append(lax.dot_general(dmixb[:, sl], vnb[:, sl], _NT, preferred_element_type=F32))
            db_parts.append(jnp.sum(dmix[:, sl], axis=1, keepdims=True))
        mixed = jnp.concatenate(mixed_parts, axis=1)
        dvn = jnp.concatenate(dvn_parts, axis=1)
        dzu = dsg_v * mixed * _gelu_grad(u)
        dxh = dvn * g_ref[...]
        dgv = rstd * (dxh - jnp.mean(dxh, axis=-1, keepdims=True) - xhat * jnp.mean(dxh * xhat, axis=-1, keepdims=True))
        dzv = dgv * _gelu_grad(v)
        dz_ref[:, :SG_W] = dzu.astype(BF16)
        dz_ref[:, SG_W:] = dzv.astype(BF16)
        dgp = jnp.sum(dvn * xhat, axis=0, keepdims=True)
        dbp = jnp.sum(dvn, axis=0, keepdims=True)

        @pl.when(i == 0)
        def _():
            for g in range(SG_GROUPS):
                dw_ref[g] = dw_parts[g]
                db_ref[g] = db_parts[g]
            dg_ref[...] = dgp
            dbe_ref[...] = dbp

        @pl.when(i > 0)
        def _():
            for g in range(SG_GROUPS):
                dw_ref[g] += dw_parts[g]
                db_ref[g] += db_parts[g]
            dg_ref[...] += dgp
            dbe_ref[...] += dbp

    zs = lambda k: pl.BlockSpec((T, cb), lambda i, k=k: (i, b0 + k))
    const3 = lambda shp: pl.BlockSpec(shp, lambda i: (0, 0, 0))
    vec = pl.BlockSpec((1, SG_W), lambda i: (0, 0))
    return pl.pallas_call(
        body, name=name, grid=(S // T,),
        in_specs=[zs(0), zs(1), zs(2), zs(3), pl.BlockSpec((T, SG_W), lambda i: (i, 0)),
                  const3((SG_GROUPS, SG_CHUNK, SG_CHUNK)), const3((SG_GROUPS, SG_CHUNK, SG_CHUNK)), const3((SG_GROUPS, SG_CHUNK, 1)),
                  vec, vec],
        out_specs=[pl.BlockSpec((T, 2 * SG_W), lambda i: (i, 0)), const3((SG_GROUPS, SG_CHUNK, SG_CHUNK)),
                   const3((SG_GROUPS, SG_CHUNK, 1)), vec, vec],
        out_shape=[jax.ShapeDtypeStruct((S, 2 * SG_W), BF16), jax.ShapeDtypeStruct((SG_GROUPS, SG_CHUNK, SG_CHUNK), F32),
                   jax.ShapeDtypeStruct((SG_GROUPS, SG_CHUNK, 1), F32), jax.ShapeDtypeStruct((1, SG_W), F32),
                   jax.ShapeDtypeStruct((1, SG_W), F32)],
        compiler_params=pltpu.CompilerParams(dimension_semantics=("arbitrary",),
                                             vmem_limit_bytes=_vmem_limit(6 * 1024 * 1024, 16 * T * SG_W * 4)),
    )(z, z, z, z, dsg, sg_w, sg_wt, sg_bc, lng, lnb)


def _gate_bwd(name, z, dmerged, y_attn, y_sg, o_g0, in_w):
    S, D = dmerged.shape
    tr = _pick(S, (512, 256, 128, 8))
    cb = _pick(D, (512, 256, 128))
    assert o_g0 % cb == 0
    nd = D // cb
    b0 = o_g0 // cb

    def body(z_ref, dm_ref, ya_ref, ys_ref, dz_ref, dy_ref):
        jj = pl.program_id(1)
        gate = _sigmoid(z_ref[...])
        dm = dm_ref[...].astype(F32)
        y = jnp.where(jj < nd, ya_ref[...], ys_ref[...]).astype(F32)
        dz_ref[...] = (dm * y * gate * (1.0 - gate)).astype(BF16)
        dy_ref[...] = (dm * gate).astype(BF16)

    half = pl.BlockSpec((tr, cb), lambda i, jj: (i, jj % nd))
    return pl.pallas_call(
        body, name=name, grid=(S // tr, 2 * nd),
        in_specs=[pl.BlockSpec((tr, cb), lambda i, jj: (i, b0 + jj)), half, half, half],
        out_specs=[pl.BlockSpec((tr, cb), lambda i, jj: (i, b0 + jj)), pl.BlockSpec((tr, cb), lambda i, jj: (i, jj))],
        out_shape=[jax.ShapeDtypeStruct((S, in_w), BF16), jax.ShapeDtypeStruct((S, 2 * D), BF16)],
        compiler_params=pltpu.CompilerParams(dimension_semantics=("parallel", "arbitrary"),
                                             vmem_limit_bytes=_vmem_limit(tr * cb * 14, 6 * tr * cb * 4)),
    )(z, dmerged, y_attn, y_sg)


def _row(v):
    return v.reshape(1, -1)


def _local_step(x, p, target, wf, small, after_group):
    S, D = x.shape
    L = p.shape[0]
    in_w = wf["w_in"][0].shape[1]
    ff = wf["w_ff_gate"][0].shape[1]
    ple = p.shape[2]
    o_sg0, o_g0 = QKV_W, QKV_W + 2 * SG_W
    cosf, sinf = _rope_tables(S)
    pb = p.astype(BF16)
    tmb = _pick(S, (1024, 512, 256))
    tn_in = _pick(in_w, (768, 1024, 512))
    tn_d = _pick(D, (1024, 512, 256))
    tn_g = _pick(D, (512, 256))
    tn_ff = _pick(ff, (512, 256))

    saved = []
    xs = x
    for i in range(L):
        sv = {"x0": xs}
        h = _rmsnorm_fwd(f"norm_mix_{i}", xs, _row(small["norm_mix"][i]))
        (z,) = _mm(f"in_proj_{i}", [dict(a=h, b=wf["w_in"], bl=i, mode="nn", K=D)], S, in_w,
                   [dict(shape=(S, in_w), dtype=F32)], _first, tm=tmb, tn=tn_in)
        qkv = _rope_fwd(f"rope_{i}", z, cosf, sinf)
        attn, lse_c = _attn_fwd(f"attn_{i}", qkv)
        sgw = small["sg_w"][i].astype(BF16)
        sgbc = small["sg_b"][i].reshape(SG_GROUPS, SG_CHUNK, 1)
        sg = _sg_fwd(f"sgu_{i}", z, sgw, sgbc, _row(small["sg_ln_g"][i]), _row(small["sg_ln_b"][i]), o_sg0)

        def merge(accs, tiles, rows):
            ya, ys = accs[0].astype(BF16), accs[1].astype(BF16)
            g0, g1 = _sigmoid(tiles[0]), _sigmoid(tiles[1])
            return [ya, ys, g0 * ya.astype(F32) + g1 * ys.astype(F32)]

        y_attn, y_sg, merged = _mm(
            f"branches_{i}",
            [dict(a=attn, b=wf["w_br_attn"], bl=i, mode="nn", K=ATTN_W), dict(a=sg, b=wf["w_br_sg"], bl=i, mode="nn", K=SG_W)],
            S, D, [dict(shape=(S, D), dtype=BF16)] * 3, merge,
            tiles=[dict(x=z, off=o_g0), dict(x=z, off=o_g0 + D)], tm=tmb, tn=tn_g)
        (x1,) = _mm(f"out_proj_{i}", [dict(a=merged, b=wf["w_out"], bl=i, mode="nn", K=D)], S, D,
                    [dict(shape=(S, D), dtype=F32)], lambda a, t, r: [t[0] + a[0]], tiles=[dict(x=xs)], tm=tmb, tn=tn_d)
        h2 = _rmsnorm_fwd(f"norm_ffn_{i}", x1, _row(small["norm_ffn"][i]))

        def swiglu(accs, tiles, rows):
            fg = accs[0].astype(BF16).astype(F32)
            fu = accs[1].astype(BF16).astype(F32)
            return [fg, fu, fg * _sigmoid(fg) * fu]

        ffg, ffu, act = _mm(
            f"ff_in_{i}",
            [dict(a=h2, b=wf["w_ff_gate"], bl=i, mode="nn", K=D), dict(a=h2, b=wf["w_ff_up"], bl=i, mode="nn", K=D)],
            S, ff, [dict(shape=(S, ff), dtype=BF16)] * 3, swiglu, tm=tmb, tn=tn_ff)
        (x2,) = _mm(f"ff_out_{i}", [dict(a=act, b=wf["w_ff_down"], bl=i, mode="nn", K=ff)], S, D,
                    [dict(shape=(S, D), dtype=F32)], lambda a, t, r: [t[0] + a[0]], tiles=[dict(x=x1)], tm=tmb, tn=tn_d)
        h3 = _rmsnorm_fwd(f"norm_ple_{i}", x2, _row(small["norm_ple"][i]))

        def ple_mix(accs, tiles, rows):
            gp = _sigmoid(accs[0]).astype(BF16)
            pe = accs[1].astype(BF16)
            return [tiles[0] + gp.astype(F32) * pe.astype(F32), gp, pe]

        x3, gp, pe = _mm(
            f"ple_{i}",
            [dict(a=h3, b=wf["w_ple_gate"], bl=i, mode="nn", K=D), dict(a=pb, al=i, b=wf["w_ple"], bl=i, mode="nn", K=ple)],
            S, D, [dict(shape=(S, D), dtype=F32), dict(shape=(S, D), dtype=BF16), dict(shape=(S, D), dtype=BF16)], ple_mix,
            tiles=[dict(x=x2)], tm=tmb, tn=tn_g)
        sv.update(h=h, z=z, qkv=qkv, attn=attn, lse_c=lse_c, sg=sg, y_attn=y_attn, y_sg=y_sg, merged=merged,
                  x1=x1, h2=h2, ffg=ffg, ffu=ffu, act=act, x2=x2, h3=h3, gp=gp, pe=pe, sgw=sgw, sgbc=sgbc)
        saved.append(sv)
        xs = x3

    loss_cell, dx, dxb, dg_final = _loss_head(xs, _row(small["norm_final"]), target)

    gw = {n: [None] * L for n in BIG}
    gs = {n: [None] * L for n in SMALL if n != "norm_final"}

    def dw(n, i, a, a_off, b, bn_off, K_rows, N_cols, tm, tn):
        (gw[n][i],) = _mm(f"d_{n}_{i}", [dict(a=a, b=b, mode="tn", K=S, a_off=a_off, bn_off=bn_off)], K_rows, N_cols,
                          [dict(shape=(K_rows, N_cols), dtype=BF16)], _first, tm=tm, tn=tn)

    for i in reversed(range(L)):
        sv = saved[i]
        dpre, dpe = _ew(f"ple_gate_bwd_{i}",
                        lambda d, g, e: [d * e.astype(F32) * g.astype(F32) * (1.0 - g.astype(F32)), d * g.astype(F32)],
                        [dx, sv["gp"], sv["pe"]], [BF16, BF16], S, D)
        (dh3,) = _mm(f"d_h3_{i}", [dict(a=dpre, b=wf["w_ple_gate"], bl=i, mode="nt", K=D)], S, D,
                     [dict(shape=(S, D), dtype=BF16)], _first, tm=tmb, tn=tn_d)
        dw("w_ple_gate", i, sv["h3"], 0, dpre, 0, D, D, tn_d, tn_d)
        dw("w_ple", i, pb[i], 0, dpe, 0, ple, D, _pick(ple, (256, 128)), _pick(D, (2048, 1024, 512, 256)))
        dx, dxb, gs["norm_ple"][i] = _rmsnorm_bwd(f"norm_ple_bwd_{i}", sv["x2"], _row(small["norm_ple"][i]), dh3, dx)
        def swiglu_bwd(accs, tiles, rows):
            da = accs[0].astype(BF16).astype(F32)
            fg, fu = tiles[0].astype(F32), tiles[1].astype(F32)
            sg_ = _sigmoid(fg)
            return [da * fu * (sg_ * (1.0 + fg * (1.0 - sg_))), da * (fg * sg_)]

        dffg, dffu = _mm(f"d_act_{i}", [dict(a=dxb, b=wf["w_ff_down"], bl=i, mode="nt", K=D)], S, ff,
                         [dict(shape=(S, ff), dtype=BF16)] * 2, swiglu_bwd, tiles=[dict(x=sv["ffg"]), dict(x=sv["ffu"])],
                         tm=tmb, tn=tn_ff)
        dw("w_ff_down", i, sv["act"], 0, dxb, 0, ff, D, tn_ff, _pick(D, (2048, 1024, 512, 256)))
        dw("w_ff_gate", i, sv["h2"], 0, dffg, 0, D, ff, _pick(D, (2048, 1024, 512, 256)), tn_ff)
        dw("w_ff_up", i, sv["h2"], 0, dffu, 0, D, ff, _pick(D, (2048, 1024, 512, 256)), tn_ff)
        (dffg, dffu), _ = lax.optimization_barrier(((dffg, dffu), after_group(i, "ffn", {n: gw[n][i] for n in GRAD_GROUPS["ffn"]})))
        (dh2,) = _mm(f"d_h2_{i}", [dict(a=dffg, b=wf["w_ff_gate"], bl=i, mode="nt", K=ff),
                                   dict(a=dffu, b=wf["w_ff_up"], bl=i, mode="nt", K=ff)], S, D,
                     [dict(shape=(S, D), dtype=BF16)], lambda a, t, r: [a[0] + a[1]], tm=tmb, tn=tn_d)
        dx, dxb, gs["norm_ffn"][i] = _rmsnorm_bwd(f"norm_ffn_bwd_{i}", sv["x1"], _row(small["norm_ffn"][i]), dh2, dx)
        (dmerged,) = _mm(f"d_merged_{i}", [dict(a=dxb, b=wf["w_out"], bl=i, mode="nt", K=D)], S, D,
                         [dict(shape=(S, D), dtype=BF16)], _first, tm=tmb, tn=tn_d)
        dw("w_out", i, sv["merged"], 0, dxb, 0, D, D, tn_d, tn_d)
        dz, dy = _gate_bwd(f"gate_bwd_{i}", sv["z"], dmerged, sv["y_attn"], sv["y_sg"], o_g0, in_w)
        (dattn,) = _mm(f"d_attn_{i}", [dict(a=dy, b=wf["w_br_attn"], bl=i, mode="nt", K=D)], S, ATTN_W,
                       [dict(shape=(S, ATTN_W), dtype=BF16)], _first, tm=tmb, tn=ATTN_W)
        (dsg,) = _mm(f"d_sg_{i}", [dict(a=dy, a_off=D, b=wf["w_br_sg"], bl=i, mode="nt", K=D)], S, SG_W,
                     [dict(shape=(S, SG_W), dtype=BF16)], _first, tm=tmb, tn=SG_W)
        dw("w_br_attn", i, sv["attn"], 0, dy, 0, ATTN_W, D, ATTN_W, _pick(D, (2048, 1024, 512, 256)))
        dw("w_br_sg", i, sv["sg"], 0, dy, D, SG_W, D, SG_W, _pick(D, (1024, 512, 256)))
        sgwt = jnp.swapaxes(small["sg_w"][i], 1, 2).astype(BF16)
        dzuv, gs["sg_w"][i], dsgb, dlg, dlb = _sg_bwd(f"sgu_bwd_{i}", sv["z"], dsg, sv["sgw"], sgwt, sv["sgbc"],
                                                      _row(small["sg_ln_g"][i]), _row(small["sg_ln_b"][i]), o_sg0)
        gs["sg_b"][i], gs["sg_ln_g"][i], gs["sg_ln_b"][i] = dsgb.reshape(SG_GROUPS, SG_CHUNK), dlg[0], dlb[0]
        dq, dk, dv = _attn_bwd(f"attn_bwd_{i}", sv["qkv"], sv["attn"], dattn, sv["lse_c"])
        dz = _rope_bwd(f"rope_bwd_{i}", dq, dk, dv, dzuv, cosf, sinf, dz)
        dw("w_in", i, sv["h"], 0, dz, 0, D, in_w, tn_d, tn_in)
        dz, _ = lax.optimization_barrier((dz, after_group(i, "mix", {n: gw[n][i] for n in GRAD_GROUPS["mix"]})))
        (dh,) = _mm(f"d_h_{i}", [dict(a=dz, b=wf["w_in"], bl=i, mode="nt", K=in_w)], S, D,
                    [dict(shape=(S, D), dtype=BF16)], _first, tm=tmb, tn=tn_d)
        dx, dxb, gs["norm_mix"][i] = _rmsnorm_bwd(f"norm_mix_bwd_{i}", sv["x0"], _row(small["norm_mix"][i]), dh, dx)

    gsmall ={n: jnp.stack([jnp.reshape(v, small[n].shape[1:]) for v in gs[n]]) for n in gs}
    gsmall["norm_final"] = dg_final[0]
    return loss_cell, dx, gsmall


def _place():
    x, y, c = lax.axis_index("x"), lax.axis_index("y"), lax.axis_index("c")
    return x, y, c, 2 * x + y


def _chip_of(s):
    return s // 2, s % 2


def _aligned(v, m):
    return v if isinstance(v, int) else pl.multiple_of(v, m)


def _piece(name, shape, s, c):
    K, N = shape
    if name in ROW_SHARDED or name == SMALL_BLOCKS:
        ks = K // 4
        return s * ks + c * (ks // 2), ks // 2, 0, N
    ns = N // 4
    return c * (K // 2), K // 2, s * ns, ns


def _handshake(peers):
    barrier = pltpu.get_barrier_semaphore()
    for peer in peers:
        pl.semaphore_signal(barrier, inc=1, device_id=peer, device_id_type=MESH)
    pl.semaphore_wait(barrier, len(peers))


def _gather_body(names, shapes, src, dst, send_sems, recv_sems, local_sems):
    n_w = len(names)
    x, y, c, s = _place()
    sib = (x, y, 1 - c)
    rel = [1, 2, 3]

    def where(w, ps, pc):
        r0, nr, c0, nc = _piece(names[w], shapes[names[w]], ps, pc)
        return dst[w].at[pl.ds(_aligned(r0, 16), nr), pl.ds(_aligned(c0, LANES), nc)]

    def copy(w, k, ps, pc, to, from_src=False):
        return pltpu.make_async_remote_copy(
            src_ref=src[w] if from_src else where(w, ps, pc), dst_ref=where(w, ps, pc),
            send_sem=send_sems.at[w, k], recv_sem=recv_sems.at[w, k], device_id=to, device_id_type=MESH)

    mine, first, passed = [], [], []
    for w in range(n_w):
        cp = pltpu.make_async_copy(src[w], where(w, s, c), local_sems.at[w])
        cp.start()
        mine.append(cp)
        first.append(copy(w, 0, s, c, sib, from_src=True))
        for j in rel:
            first.append(copy(w, j, s, c, (*_chip_of(s ^ j), c), from_src=True))
    for cp in first:
        cp.start()
    for w in range(n_w):
        for j in rel:
            copy(w, j, s ^ j, c, sib).wait_recv()
            fw = copy(w, 3 + j, s ^ j, c, sib)
            fw.start()
            passed.append(fw)
    for w in range(n_w):
        copy(w, 0, s, 1 - c, sib).wait_recv()
        for j in rel:
            copy(w, 3 + j, s ^ j, 1 - c, sib).wait_recv()
    for cp in first + passed:
        cp.wait_send()
    for cp in mine:
        cp.wait()


def _gather_sems(n_w):
    return (pltpu.SemaphoreType.DMA((n_w, 7)), pltpu.SemaphoreType.DMA((n_w, 7)), pltpu.SemaphoreType.DMA((n_w,)))


def _gather_peers():
    x, y, c, s = _place()
    return [(x, y, 1 - c)] + [(*_chip_of(s ^ j), c) for j in (1, 2, 3)]


def _gather_weights_async(name, pieces, shapes):
    names = list(pieces)
    n_w = len(names)
    src = [jax.new_ref(pieces[n], memory_space=pltpu.MemorySpace.HBM) for n in names]
    dst = [jax.empty_ref(jax.ShapeDtypeStruct(tuple(shapes[n]), pieces[n].dtype), memory_space=pltpu.MemorySpace.HBM)
           for n in names]

    @pl.kernel(mesh=plsc.ScalarSubcoreMesh(axis_name="seq", num_cores=1), name=name, scratch_types=_gather_sems(n_w),
               compiler_params=pltpu.CompilerParams(collective_id=GATHER_COLLECTIVE_ID))
    def launch(send_sems, recv_sems, local_sems):
        _handshake(_gather_peers())
        _gather_body(names, shapes, src, dst, send_sems, recv_sems, local_sems)

    launch()
    return {n: d[...] for n, d in zip(names, dst)}


def _halves_view(name, g):
    L, K, N = g.shape
    if name in ROW_SHARDED:
        return g.reshape(L * 4, 2, K // 8, N)
    return g.reshape(L, 2, K // 2, N)


def _all_peers():
    x, y, c, s = _place()
    return [(x, y, 1 - c)] + [(*_chip_of(s ^ j), h) for j in (1, 2, 3) for h in (0, 1)]


def _scatter_partials_async(name, views):
    names = list(views)
    n_w = len(names)
    src = [jax.new_ref(views[n], memory_space=pltpu.MemorySpace.HBM) for n in names]
    dst = [jax.empty_ref(jax.ShapeDtypeStruct(_partials_out_shape(n, views[n].shape), BF16), memory_space=pltpu.MemorySpace.HBM)
           for n in names]

    @pl.kernel(mesh=plsc.ScalarSubcoreMesh(axis_name="seq", num_cores=1), name=name, scratch_types=_partials_sems(n_w),
               compiler_params=pltpu.CompilerParams(collective_id=PARTIALS_COLLECTIVE_ID))
    def launch(send_sems, recv_sems):
        _handshake(_all_peers())
        _scatter_partials_body(names, src, dst, send_sems, recv_sems)

    launch()
    return {n: d[...] for n, d in zip(names, dst)}


def _partials_out_shape(name, v):
    return (7, 1, v[2], v[3] if name in ROW_SHARDED else v[3] // 4)


def _partials_sems(n_w):
    return (pltpu.SemaphoreType.DMA((n_w, 7)), pltpu.SemaphoreType.DMA((n_w, 7)))


def _scatter_partials_body(names, src, dst, send_sems, recv_sems):
    x, y, c, s = _place()

    def piece(w, t, h):
        if names[w] in ROW_SHARDED:
            return src[w].at[pl.ds(t, 1), h]
        ns = src[w].shape[3] // 4
        return src[w].at[:, h, :, pl.ds(pl.multiple_of(t * ns, LANES), ns)]

    sent = []
    for w in range(len(names)):
        for j in (1, 2, 3):
            for h in (0, 1):
                sent.append(pltpu.make_async_remote_copy(
                    src_ref=piece(w, s ^ j, h), dst_ref=dst[w].at[2 * (j - 1) + c], send_sem=send_sems.at[w, 2 * (j - 1) + h],
                    recv_sem=recv_sems.at[w, 2 * (j - 1) + c], device_id=(*_chip_of(s ^ j), h), device_id_type=MESH))
        sent.append(pltpu.make_async_remote_copy(
            src_ref=piece(w, s, 1 - c), dst_ref=dst[w].at[6], send_sem=send_sems.at[w, 6], recv_sem=recv_sems.at[w, 6],
            device_id=(x, y, 1 - c), device_id_type=MESH))
    for cp in sent:
        cp.start()
    for w in range(len(names)):
        for slot in range(7):
            pltpu.make_async_remote_copy(src_ref=dst[w].at[slot], dst_ref=dst[w].at[slot], send_sem=send_sems.at[w, slot],
                                         recv_sem=recv_sems.at[w, slot], device_id=(x, y, 1 - c), device_id_type=MESH).wait_recv()
    for cp in sent:
        cp.wait_send()


def _shard_sum_partials(name, view, parts, place, row_sharded, layer, n_layers, into):
    R, C = parts.shape[2:]
    tc = _pick(C, (2048, 1408, 1024, 896, 512, 384, 256, 128))
    tr = _pick(R, [t for t in (1024, 512, 256, 128, 64, 32, 16) if t * tc <= 2 * EW_TILE_ELEMS] + [8])

    def body(p_ref, own_ref, *rest):
        acc = own_ref[...].astype(F32) + rest[6][...].astype(F32)
        for k in range(6):
            acc = acc + rest[k][...].astype(F32)
        rest[-1][...] = acc

    if row_sharded:
        own_spec = pl.BlockSpec((None, None, tr, tc), lambda i, j, p: (p[1], p[0], i, j))
    else:
        own_spec = pl.BlockSpec((None, None, tr, tc), lambda i, j, p: (0, p[0], i, p[1] * (C // tc) + j))
    part = lambda k: pl.BlockSpec((None, None, tr, tc), lambda i, j, p, k=k: (k, 0, i, j))
    in_specs, args, aliases = [own_spec] + [part(k) for k in range(7)], [place, view] + [parts] * 7, {}
    if into is not None:
        in_specs.append(pl.BlockSpec(memory_space=pl.ANY))
        args.append(into)
        aliases = {9: 0}
    return pl.pallas_call(
        body, name=name, out_shape=jax.ShapeDtypeStruct((n_layers, 2, R, C), F32),
        grid_spec=pltpu.PrefetchScalarGridSpec(
            num_scalar_prefetch=1, grid=(R // tr, C // tc), in_specs=in_specs,
            out_specs=pl.BlockSpec((None, None, tr, tc), lambda i, j, p: (layer, p[0], i, j))),
        input_output_aliases=aliases,
        compiler_params=pltpu.CompilerParams(dimension_semantics=("parallel", "parallel"),
                                             vmem_limit_bytes=_vmem_limit(20 * tr * tc, 5 * tr * tc * 4)),
    )(*args)


def _share_halves_async(name, ghalf):
    names = list(ghalf)
    n_w = len(names)
    buf = [jax.new_ref(ghalf[n], memory_space=pltpu.MemorySpace.HBM) for n in names]

    @pl.kernel(mesh=plsc.ScalarSubcoreMesh(axis_name="seq", num_cores=1), name=name,
               scratch_types=(pltpu.SemaphoreType.DMA((n_w,)), pltpu.SemaphoreType.DMA((n_w,))),
               compiler_params=pltpu.CompilerParams(collective_id=SHARE_COLLECTIVE_ID))
    def launch(send_sems, recv_sems):
        x, y, c, s = _place()
        _handshake([(x, y, 1 - c)])
        copies = [pltpu.make_async_remote_copy(src_ref=buf[w].at[:, c], dst_ref=buf[w].at[:, c], send_sem=send_sems.at[w],
                                               recv_sem=recv_sems.at[w], device_id=(x, y, 1 - c), device_id_type=MESH)
                  for w in range(n_w)]
        for cp in copies:
            cp.start()
        for cp in copies:
            cp.wait()

    launch()
    return {n: b[...] for n, b in zip(names, buf)}


def _share_halves(name, ghalf):
    names = list(ghalf)
    n_w = len(names)

    def body(*refs):
        src = refs[:n_w]
        dst = refs[n_w:2 * n_w]
        send_sems, recv_sems = refs[2 * n_w:]
        x, y, c, s = _place()
        remote = [pltpu.make_async_remote_copy(src_ref=src[w].at[:, c], dst_ref=dst[w].at[:, c], send_sem=send_sems.at[w],
                                               recv_sem=recv_sems.at[w], device_id=(x, y, 1 - c), device_id_type=MESH)
                  for w in range(n_w)]
        for cp in remote:
            cp.start()
        for cp in remote:
            cp.wait()

    anyspec = pl.BlockSpec(memory_space=pl.ANY)
    out = pl.pallas_call(
        body, name=name, in_specs=[anyspec] * n_w, out_specs=[anyspec] * n_w,
        out_shape=[jax.ShapeDtypeStruct(ghalf[n].shape, F32) for n in names],
        input_output_aliases={w: w for w in range(n_w)},
        scratch_shapes=[pltpu.SemaphoreType.DMA((n_w,)), pltpu.SemaphoreType.DMA((n_w,))],
    )(*[ghalf[n] for n in names])
    return dict(zip(names, out))


def _adamw_math(w, g, m, v):
    m = ADAM_B1 * m + (1.0 - ADAM_B1) * g
    v = ADAM_B2 * v + (1.0 - ADAM_B2) * (g * g)
    m_hat = m / (1.0 - ADAM_B1 ** ADAM_STEP)
    v_hat = v / (1.0 - ADAM_B2 ** ADAM_STEP)
    delta = -ADAM_LR * (m_hat / (jnp.sqrt(v_hat) + ADAM_EPS) + ADAM_WD * w)
    return delta, m, v


def _adamw(name, w, g, m, v):
    shape = w.shape
    C = shape[-1]
    R = math.prod(shape[:-1])
    f = lambda a: a.reshape(R, C)
    res = _ew(name, lambda w_, g_, m_, v_: [g_, *_adamw_math(w_, g_, m_, v_)], [f(w), f(g), f(m), f(v)], [F32] * 4, R, C)
    return [r.reshape(shape) for r in res]


def _pack_small(d):
    return jnp.concatenate([d[n].reshape(-1, LANES) for n in SMALL], axis=0)


def _unpack_small(flat, like):
    out, r = {}, 0
    for n in SMALL:
        k = like[n].size // LANES
        out[n] = flat[r:r + k].reshape(like[n].shape)
        r += k
    return out


def _small_update(gall, w, m, v):
    M = w.shape[0]
    tr = _pick(M, (552, 276, 184, 96, 48, 24, 8))

    def body(*refs):
        g = refs[0][...]
        for d in range(1, 8):
            g = g + refs[d][...]
        delta, nm, nv = _adamw_math(refs[8][...], g, refs[9][...], refs[10][...])
        refs[11][...] = g
        refs[12][...] = delta
        refs[13][...] = nm
        refs[14][...] = nv

    blk = pl.BlockSpec((tr, LANES), lambda i: (i, 0))
    in_specs = [pl.BlockSpec((tr, LANES), lambda i, d=d: (d * (M // tr) + i, 0)) for d in range(8)] + [blk] * 3
    return pl.pallas_call(
        body, name="small_update", grid=(M // tr,), in_specs=in_specs, out_specs=[blk] * 4,
        out_shape=[jax.ShapeDtypeStruct((M, LANES), F32)] * 4,
        compiler_params=pltpu.CompilerParams(dimension_semantics=("parallel",), vmem_limit_bytes=_vmem_limit(15 * tr * LANES * 4)),
    )(*([gall] * 8), w, m, v)


def _step(x, p, target, w, m, v):
    L = p.shape[0]
    x_i, y_i, c, s = _place()
    shapes = {}
    for n in BIG:
        _, K, N = w[n].shape
        shapes[n] = (4 * K, N) if n in ROW_SHARDED else (K, 4 * N)
    def pieces_of(i):
        return {n: lax.dynamic_slice_in_dim(w[n][i], c * (w[n].shape[1] // 2), w[n].shape[1] // 2, axis=0).astype(BF16)
                for n in BIG}

    def piece(n, i, after=None):
        wn = w[n] if after is None else lax.optimization_barrier((w[n], after))[0]
        return lax.dynamic_slice_in_dim(wn[i], c * (w[n].shape[1] // 2), w[n].shape[1] // 2, axis=0).astype(BF16)

    w_in0 = piece("w_in", 0)
    layers = [_gather_weights_async("gather_weights_0_w_in", {"w_in": w_in0}, shapes)]
    for i in range(L):
        if i > 0:
            layers.append({})
        for g in ("mix", "ffn"):
            mine = {n: piece(n, i, w_in0) for n in GRAD_GROUPS[g] if n not in layers[i]}
            layers[i].update(_gather_weights_async(f"gather_weights_{i}_{g}", mine, shapes))
    wf = {n: [layers[i][n] for i in range(L)] for n in BIG}
    small = {n: w[n] for n in SMALL}
    place = jnp.stack([c, s]).astype(jnp.int32)
    reduced = []

    def after_group(i, group, grads):
        views = {n: _halves_view(n, g[None]) for n, g in grads.items()}
        reduced.append((i, group, views, _scatter_partials_async(f"scatter_partials_{i}_{group}", views)))
        return views

    loss_cell, dx, gsmall = _local_step(x[0], p[:, 0], target[0], wf, small, after_group)
    loss = lax.psum(jnp.sum(loss_cell), ("x", "y", "c"))
    packed = _pack_small(gsmall)
    gall = _gather_weights_async("gather_small", {SMALL_BLOCKS: packed}, {SMALL_BLOCKS: (8 * packed.shape[0], LANES)})[SMALL_BLOCKS]
    ghalf = {n: None for n in BIG}
    grad, delta, new_m, new_v = {}, {}, {}, {}
    done = None

    def update(group, gfull):
        for n in GRAD_GROUPS[group]:
            grad[n], delta[n], new_m[n], new_v[n] = _adamw(f"adamw_{n}", w[n], gfull[n].reshape(w[n].shape), m[n], v[n])
        return {n: delta[n] for n in GRAD_GROUPS[group]}

    for k, (i, group, own, parts) in enumerate(reduced):
        if k == len(reduced) - 1:
            shared_ffn = _share_halves_async("share_halves_ffn", {n: ghalf[n] for n in GRAD_GROUPS["ffn"]})
        parts, _ = lax.optimization_barrier((parts, done))
        for n in own:
            ghalf[n] = _shard_sum_partials(f"shard_sum_{n}_{i}", own[n], parts[n], place, n in ROW_SHARDED, i, L, ghalf[n])
        done = {n: ghalf[n] for n in own}
    done = update("mix", _share_halves("share_halves_mix", {n: ghalf[n] for n in GRAD_GROUPS["mix"]}))
    gall, _ = lax.optimization_barrier((gall, done))
    small_out = _small_update(gall, _pack_small(small), _pack_small({n: m[n] for n in SMALL}), _pack_small({n: v[n] for n in SMALL}))
    shared_ffn, _ = lax.optimization_barrier((shared_ffn, small_out))
    update("ffn", shared_ffn)
    for dst, flat in zip((grad, delta, new_m, new_v), small_out):
        dst.update(_unpack_small(flat, small))
    return loss, dx[None], grad, delta, new_m, new_v


def kernel(x, p, w_in, w_br_attn, w_br_sg, w_out, sg_w, sg_b, sg_ln_g, sg_ln_b, norm_mix, norm_ffn, norm_ple, norm_final, w_ff_gate, w_ff_up, w_ff_down, w_ple_gate, w_ple, loss_target, m_w_in, m_w_br_attn, m_w_br_sg, m_w_out, m_sg_w, m_sg_b, m_sg_ln_g, m_sg_ln_b, m_norm_mix, m_norm_ffn, m_norm_ple, m_norm_final, m_w_ff_gate, m_w_ff_up, m_w_ff_down, m_w_ple_gate, m_w_ple, v_w_in, v_w_br_attn, v_w_br_sg, v_w_out, v_sg_w, v_sg_b, v_sg_ln_g, v_sg_ln_b, v_norm_mix, v_norm_ffn, v_norm_ple, v_norm_final, v_w_ff_gate, v_w_ff_up, v_w_ff_down, v_w_ple_gate, v_w_ple):
    w = dict(w_in=w_in, w_br_attn=w_br_attn, w_br_sg=w_br_sg, w_out=w_out, sg_w=sg_w, sg_b=sg_b, sg_ln_g=sg_ln_g, sg_ln_b=sg_ln_b,
             norm_mix=norm_mix, norm_ffn=norm_ffn, norm_ple=norm_ple, norm_final=norm_final, w_ff_gate=w_ff_gate, w_ff_up=w_ff_up,
             w_ff_down=w_ff_down, w_ple_gate=w_ple_gate, w_ple=w_ple)
    m = dict(w_in=m_w_in, w_br_attn=m_w_br_attn, w_br_sg=m_w_br_sg, w_out=m_w_out, sg_w=m_sg_w, sg_b=m_sg_b, sg_ln_g=m_sg_ln_g,
             sg_ln_b=m_sg_ln_b, norm_mix=m_norm_mix, norm_ffn=m_norm_ffn, norm_ple=m_norm_ple, norm_final=m_norm_final,
             w_ff_gate=m_w_ff_gate, w_ff_up=m_w_ff_up, w_ff_down=m_w_ff_down, w_ple_gate=m_w_ple_gate, w_ple=m_w_ple)
    v = dict(w_in=v_w_in, w_br_attn=v_w_br_attn, w_br_sg=v_w_br_sg, w_out=v_w_out, sg_w=v_sg_w, sg_b=v_sg_b, sg_ln_g=v_sg_ln_g,
             sg_ln_b=v_sg_ln_b, norm_mix=v_norm_mix, norm_ffn=v_norm_ffn, norm_ple=v_norm_ple, norm_final=v_norm_final,
             w_ff_gate=v_w_ff_gate, w_ff_up=v_w_ff_up, w_ff_down=v_w_ff_down, w_ple_gate=v_w_ple_gate, w_ple=v_w_ple)
    loss, grad_x, grad, delta, new_m, new_v = _step(x, p, loss_target, w, m, v)
    return (loss, grad_x, *[grad[n] for n in WEIGHTS], *[delta[n] for n in WEIGHTS], *[new_m[n] for n in WEIGHTS],
            *[new_v[n] for n in WEIGHTS])
```

```python
import functools
import math

import jax
import jax.numpy as jnp
from jax import lax
from jax.experimental import pallas as pl
from jax.experimental.pallas import tpu as pltpu
from jax.experimental.pallas import tpu_sc as plsc

F32 = jnp.float32
BF16 = jnp.bfloat16
MESH = pl.DeviceIdType.MESH

HEAD_DIM = 128
ATTN_GROUPS = ((128, 1), (512, 4), (2048, 16))
N_GROUPS = 3
HEADS = 4
QKV_W = 3 * N_GROUPS * HEADS * HEAD_DIM
ATTN_W = HEADS * HEAD_DIM
SG_CHUNK = 128
SG_GROUPS = 8
SG_W = 1024
RADIUS = 64
ROPE_THETA = 10000.0
NORM_EPS = 1e-6
NEG_INF = -1e30
ADAM_LR, ADAM_B1, ADAM_B2, ADAM_EPS, ADAM_WD, ADAM_STEP = 0.001, 0.9, 0.999, 1e-08, 0.01, 10

VMEM_CAP_V7X = 56 * 1024 * 1024
LANES = 128
EW_TILE_ELEMS = 256 * 1024
MM_VMEM_BUDGET = 44 * 1024 * 1024

GATHER_COLLECTIVE_ID = 1
PARTIALS_COLLECTIVE_ID = 2

BIG = ("w_in", "w_br_attn", "w_br_sg", "w_out", "w_ff_gate", "w_ff_up", "w_ff_down", "w_ple_gate", "w_ple")
ROW_SHARDED = ("w_out", "w_ff_down", "w_ple_gate")
SMALL_BLOCKS = "small_blocks"
GRAD_GROUPS = {"ffn": ("w_ple_gate", "w_ple", "w_ff_down", "w_ff_gate", "w_ff_up"), "mix": ("w_out", "w_br_attn", "w_br_sg", "w_in")}
SMALL = ("sg_w", "sg_b", "sg_ln_g", "sg_ln_b", "norm_mix", "norm_ffn", "norm_ple", "norm_final")
WEIGHTS = ("w_in", "w_br_attn", "w_br_sg", "w_out", "sg_w", "sg_b", "sg_ln_g", "sg_ln_b", "norm_mix", "norm_ffn",
           "norm_ple", "norm_final", "w_ff_gate", "w_ff_up", "w_ff_down", "w_ple_gate", "w_ple")


def _pick(n, prefs):
    for t in prefs:
        if n % t == 0:
            return t
    return n


def _vmem_limit(block_bytes, temp_bytes=0):
    est = 2 * block_bytes + temp_bytes
    assert est <= VMEM_CAP_V7X, est
    return VMEM_CAP_V7X


def _sigmoid(x):
    return 1.0 / (1.0 + jnp.exp(-x))


_GELU_C = math.sqrt(2.0 / math.pi)


def _gelu(x):
    return 0.5 * x * (1.0 + jnp.tanh(_GELU_C * (x + 0.044715 * (x * x * x))))


def _gelu_grad(x):
    t = jnp.tanh(_GELU_C * (x + 0.044715 * (x * x * x)))
    return 0.5 * (1.0 + t) + 0.5 * x * (1.0 - t * t) * (_GELU_C * (1.0 + 3.0 * 0.044715 * (x * x)))


def _lead(arr, l, blk, idx):
    if arr.ndim == 2:
        return pl.BlockSpec(blk, idx)
    return pl.BlockSpec((None,) + blk, lambda *g: (l,) + idx(*g))


def _k_steps(prods, tm, tn, fixed_bytes):
    for nk in range(1, 129):
        if any(p["K"] % nk or (p["K"] // nk) % LANES for p in prods):
            continue
        if 2 * sum((tm + tn) * (p["K"] // nk) * 2 for p in prods) + fixed_bytes <= MM_VMEM_BUDGET:
            return nk
    raise ValueError("no contraction split fits VMEM")


def _mm(name, prods, M, N, outs, epilogue, tiles=(), rows=(), tm=1024, tn=1024):
    assert M % tm == 0 and N % tn == 0, (name, M, N, tm, tn)
    fixed = 2 * tm * tn * (sum(t["x"].dtype.itemsize for t in tiles) + sum(jnp.dtype(o["dtype"]).itemsize for o in outs))
    fixed += (len(prods) + 2) * tm * tn * 4
    nk = _k_steps(prods, tm, tn, fixed)
    in_specs, args, block_bytes = [], [], 0
    for p in prods:
        if isinstance(p["b"], (list, tuple)):
            p["b"], p["bl"] = p["b"][p["bl"]], None
        K = p["K"]
        assert K % nk == 0, (name, K, nk)
        tk = K // nk
        p["tk"] = tk
        a_off, bk_off, bn_off = p.get("a_off", 0), p.get("bk_off", 0), p.get("bn_off", 0)
        assert bn_off % tn == 0 and bk_off % tk == 0
        if p["mode"] == "nn":
            assert a_off % tk == 0
            a_spec = _lead(p["a"], p.get("al"), (tm, tk), lambda i, j, k, o=a_off // tk: (i, o + k))
            b_spec = _lead(p["b"], p.get("bl"), (tk, tn), lambda i, j, k, ok=bk_off // tk, on=bn_off // tn: (ok + k, on + j))
        elif p["mode"] == "nt":
            assert a_off % tk == 0
            a_spec = _lead(p["a"], p.get("al"), (tm, tk), lambda i, j, k, o=a_off // tk: (i, o + k))
            b_spec = _lead(p["b"], p.get("bl"), (tn, tk), lambda i, j, k, ok=bk_off // tk, on=bn_off // tn: (on + j, ok + k))
        else:
            assert a_off % tm == 0
            a_spec = _lead(p["a"], p.get("al"), (tk, tm), lambda i, j, k, o=a_off // tm: (k, o + i))
            b_spec = _lead(p["b"], p.get("bl"), (tk, tn), lambda i, j, k, on=bn_off // tn: (k, on + j))
        in_specs += [a_spec, b_spec]
        args += [p["a"], p["b"]]
        block_bytes += (tm + tn) * tk * 2
    for t in tiles:
        off = t.get("off", 0)
        assert off % tn == 0
        in_specs.append(_lead(t["x"], t.get("l"), (tm, tn), lambda i, j, k, o=off // tn: (i, o + j)))
        args.append(t["x"])
        block_bytes += tm * tn * t["x"].dtype.itemsize
    for r in rows:
        in_specs.append(pl.BlockSpec((1, tn), lambda i, j, k: (0, j)))
        args.append(r)
    out_shapes, out_specs, aliases = [], [], {}
    for o_i, o in enumerate(outs):
        off = o.get("col_off", 0)
        assert off % tn == 0
        out_shapes.append(jax.ShapeDtypeStruct(o["shape"], o["dtype"]))
        idx = lambda i, j, k, oo=off // tn: (i, oo + j)
        if len(o["shape"]) == 2:
            out_specs.append(pl.BlockSpec((tm, tn), idx))
        else:
            out_specs.append(pl.BlockSpec((None, tm, tn), lambda i, j, k, l=o["l"], f=idx: (l,) + f(i, j, k)))
        if o.get("alias") is not None:
            aliases[len(args)] = o_i
            in_specs.append(pl.BlockSpec(memory_space=pl.ANY))
            args.append(o["alias"])
        block_bytes += tm * tn * jnp.dtype(o["dtype"]).itemsize
    n_p, n_t, n_r, n_o = len(prods), len(tiles), len(rows), len(outs)
    n_alias = len(aliases)
    modes = [p["mode"] for p in prods]

    def body(*refs):
        ab = refs[: 2 * n_p]
        t_refs = refs[2 * n_p: 2 * n_p + n_t]
        r_refs = refs[2 * n_p + n_t: 2 * n_p + n_t + n_r]
        o_refs = refs[2 * n_p + n_t + n_r + n_alias: 2 * n_p + n_t + n_r + n_alias + n_o]
        acc_refs = refs[2 * n_p + n_t + n_r + n_alias + n_o:]
        dims = {"nn": (((1,), (0,)), ((), ())), "nt": (((1,), (1,)), ((), ())), "tn": (((0,), (0,)), ((), ()))}

        def part(q):
            return lax.dot_general(ab[2 * q][...], ab[2 * q + 1][...], dims[modes[q]], preferred_element_type=F32)

        def finish(accs):
            res = epilogue(accs, [t[...] for t in t_refs], [r[...] for r in r_refs])
            for o_ref, val in zip(o_refs, res, strict=True):
                o_ref[...] = val.astype(o_ref.dtype)

        if nk == 1:
            finish([part(q) for q in range(n_p)])
        else:
            k = pl.program_id(2)

            @pl.when(k == 0)
            def _():
                for q, acc in enumerate(acc_refs):
                    acc[...] = part(q)

            @pl.when(k > 0)
            def _():
                for q, acc in enumerate(acc_refs):
                    acc[...] += part(q)

            @pl.when(k == nk - 1)
            def _():
                finish([acc[...] for acc in acc_refs])

    scratch = [pltpu.VMEM((tm, tn), F32) for _ in prods] if nk > 1 else []
    temp = (n_p + 2) * tm * tn * 4
    res = pl.pallas_call(
        body, name=name, grid=(M // tm, N // tn, nk), in_specs=in_specs, out_specs=out_specs, out_shape=out_shapes,
        scratch_shapes=scratch, input_output_aliases=aliases,
        compiler_params=pltpu.CompilerParams(dimension_semantics=("parallel", "parallel", "arbitrary"),
                                             vmem_limit_bytes=_vmem_limit(block_bytes, temp)),
    )(*args)
    return res


def _first(accs, tiles, rows):
    return [accs[0]]


def _ew(name, fn, ins, outs, R, C, tr=None, tc=None):
    tc = tc or _pick(C, (2048, 1536, 1408, 1024, 896, 512, 384, 256, 128))
    tr = tr or _pick(R, [t for t in (512, 256, 128, 64, 32, 16) if t * tc <= 2 * EW_TILE_ELEMS] + [8])
    in_specs, args, bb = [], [], 0
    for arr in ins:
        in_specs.append(pl.BlockSpec((tr, tc), lambda i, j: (i, j)))
        args.append(arr)
        bb += tr * tc * arr.dtype.itemsize
    out_shapes = [jax.ShapeDtypeStruct((R, C), d) for d in outs]
    out_specs = [pl.BlockSpec((tr, tc), lambda i, j: (i, j)) for _ in outs]
    bb += sum(tr * tc * jnp.dtype(d).itemsize for d in outs)
    n_in = len(ins)

    def body(*refs):
        res = fn(*[r[...] for r in refs[:n_in]])
        for o_ref, val in zip(refs[n_in:], res, strict=True):
            o_ref[...] = val.astype(o_ref.dtype)

    return pl.pallas_call(
        body, name=name, grid=(R // tr, C // tc), in_specs=in_specs, out_specs=out_specs, out_shape=out_shapes,
        compiler_params=pltpu.CompilerParams(dimension_semantics=("parallel", "parallel"),
                                             vmem_limit_bytes=_vmem_limit(bb, 6 * tr * tc * 4)),
    )(*args)


def _rmsnorm_fwd(name, x, g):
    S, D = x.shape
    tr = _pick(S, (256, 128, 64, 8))

    def body(x_ref, g_ref, h_ref):
        xv = x_ref[...]
        r = lax.rsqrt(jnp.mean(xv * xv, axis=-1, keepdims=True) + NORM_EPS)
        h_ref[...] = (xv * r * g_ref[...]).astype(BF16)

    return pl.pallas_call(
        body, name=name, grid=(S // tr,),
        in_specs=[pl.BlockSpec((tr, D), lambda i: (i, 0)), pl.BlockSpec((1, D), lambda i: (0, 0))],
        out_specs=pl.BlockSpec((tr, D), lambda i: (i, 0)), out_shape=jax.ShapeDtypeStruct((S, D), BF16),
        compiler_params=pltpu.CompilerParams(dimension_semantics=("parallel",),
                                             vmem_limit_bytes=_vmem_limit(tr * D * 6, 3 * tr * D * 4)),
    )(x, g)


def _rmsnorm_bwd(name, x, g, dh, dres):
    S, D = x.shape
    tr = _pick(S, (256, 128, 64, 8))

    def body(x_ref, g_ref, dh_ref, dres_ref, dx_ref, dxb_ref, dg_ref):
        xv = x_ref[...]
        dy = dh_ref[...].astype(F32)
        r = lax.rsqrt(jnp.mean(xv * xv, axis=-1, keepdims=True) + NORM_EPS)
        a = dy * g_ref[...]
        dx = dres_ref[...] + r * a - xv * (r * r * r) * jnp.mean(a * xv, axis=-1, keepdims=True)
        dx_ref[...] = dx
        dxb_ref[...] = dx.astype(BF16)
        part = jnp.sum(dy * xv * r, axis=0, keepdims=True)

        @pl.when(pl.program_id(0) == 0)
        def _():
            dg_ref[...] = part

        @pl.when(pl.program_id(0) > 0)
        def _():
            dg_ref[...] += part

    row = pl.BlockSpec((tr, D), lambda i: (i, 0))
    vec = pl.BlockSpec((1, D), lambda i: (0, 0))
    return pl.pallas_call(
        body, name=name, grid=(S // tr,), in_specs=[row, vec, row, row], out_specs=[row, row, vec],
        out_shape=[jax.ShapeDtypeStruct((S, D), F32), jax.ShapeDtypeStruct((S, D), BF16), jax.ShapeDtypeStruct((1, D), F32)],
        compiler_params=pltpu.CompilerParams(dimension_semantics=("arbitrary",),
                                             vmem_limit_bytes=_vmem_limit(tr * D * 18, 5 * tr * D * 4)),
    )(x, g, dh, dres)


def _loss_head(x, g, target):
    S, D = x.shape
    tr = _pick(S, (256, 128, 64, 8))

    def body(x_ref, g_ref, t_ref, loss_ref, dx_ref, dxb_ref, dg_ref):
        xv = x_ref[...]
        r = lax.rsqrt(jnp.mean(xv * xv, axis=-1, keepdims=True) + NORM_EPS)
        xn = xv * r
        diff = xn * g_ref[...] - t_ref[...]
        dy = diff * (1.0 / D)
        a = dy * g_ref[...]
        dx = r * a - xv * (r * r * r) * jnp.mean(a * xv, axis=-1, keepdims=True)
        dx_ref[...] = dx
        dxb_ref[...] = dx.astype(BF16)
        part = jnp.sum(dy * xn, axis=0, keepdims=True)
        cell = (lax.broadcasted_iota(jnp.int32, (8, LANES), 0) == 0) & (lax.broadcasted_iota(jnp.int32, (8, LANES), 1) == 0)
        lpart = jnp.where(cell, 0.5 * jnp.sum(jnp.mean(diff * diff, axis=-1, keepdims=True)), 0.0)

        @pl.when(pl.program_id(0) == 0)
        def _():
            dg_ref[...] = part
            loss_ref[...] = lpart

        @pl.when(pl.program_id(0) > 0)
        def _():
            dg_ref[...] += part
            loss_ref[...] += lpart

    row = pl.BlockSpec((tr, D), lambda i: (i, 0))
    vec = pl.BlockSpec((1, D), lambda i: (0, 0))
    return pl.pallas_call(
        body, name="loss_head", grid=(S // tr,), in_specs=[row, vec, row],
        out_specs=[pl.BlockSpec((8, LANES), lambda i: (0, 0)), row, row, vec],
        out_shape=[jax.ShapeDtypeStruct((8, LANES), F32), jax.ShapeDtypeStruct((S, D), F32),
                   jax.ShapeDtypeStruct((S, D), BF16), jax.ShapeDtypeStruct((1, D), F32)],
        compiler_params=pltpu.CompilerParams(dimension_semantics=("arbitrary",),
                                             vmem_limit_bytes=_vmem_limit(tr * D * 14, 6 * tr * D * 4)),
    )(x, g, target)


def _rope_tables(S):
    pos = jnp.arange(S, dtype=F32)
    inv_freq = ROPE_THETA ** (-jnp.arange(0, HEAD_DIM, 2, dtype=F32) / HEAD_DIM)
    ang = pos[:, None] * inv_freq[None, :]
    cos, sin = jnp.cos(ang), jnp.sin(ang)
    return jnp.concatenate([cos, cos], axis=-1), jnp.concatenate([-sin, sin], axis=-1)


def _rope_fwd(name, z, cosf, sinf):
    S = z.shape[0]
    tr = _pick(S, (256, 128, 64, 8))
    n_q = N_GROUPS * HEADS

    def body(z_ref, c_ref, s_ref, o_ref):
        c, s = c_ref[...], s_ref[...]
        for j in range(QKV_W // HEAD_DIM):
            t = z_ref[:, j * HEAD_DIM:(j + 1) * HEAD_DIM]
            if j < 2 * n_q:
                t = t * c + pltpu.roll(t, HEAD_DIM // 2, axis=1) * s
            if j < n_q:
                t = t * ATTN_SCALE
            o_ref[:, j * HEAD_DIM:(j + 1) * HEAD_DIM] = t.astype(BF16)

    tab = pl.BlockSpec((tr, HEAD_DIM), lambda i: (i, 0))
    return pl.pallas_call(
        body, name=name, grid=(S // tr,), in_specs=[pl.BlockSpec((tr, QKV_W), lambda i: (i, 0)), tab, tab],
        out_specs=pl.BlockSpec((tr, QKV_W), lambda i: (i, 0)), out_shape=jax.ShapeDtypeStruct((S, QKV_W), BF16),
        compiler_params=pltpu.CompilerParams(dimension_semantics=("parallel",),
                                             vmem_limit_bytes=_vmem_limit(tr * QKV_W * 6, tr * QKV_W * 4)),
    )(z, cosf, sinf)


def _rope_bwd(name, dq, dk, dv, dzuv, cosf, sinf, dz):
    S = dq.shape[0]
    tr = _pick(S, (256, 128, 64, 8))
    W3 = QKV_W // 3
    nh = W3 // HEAD_DIM
    wide = QKV_W + dzuv.shape[1]

    def body(dq_ref, dk_ref, dv_ref, uv_ref, c_ref, s_ref, dz_in, o_ref):
        c, s = c_ref[...], s_ref[...]
        for part, ref in enumerate((dq_ref, dk_ref)):
            for j in range(nh):
                t = ref[:, j * HEAD_DIM:(j + 1) * HEAD_DIM].astype(F32)
                t = t * c - pltpu.roll(t, HEAD_DIM // 2, axis=1) * s
                o_ref[:, part * W3 + j * HEAD_DIM: part * W3 + (j + 1) * HEAD_DIM] = t.astype(BF16)
        o_ref[:, 2 * W3:QKV_W] = dv_ref[...]
        o_ref[:, QKV_W:] = uv_ref[...]

    third = pl.BlockSpec((tr, W3), lambda i: (i, 0))
    tab = pl.BlockSpec((tr, HEAD_DIM), lambda i: (i, 0))
    return pl.pallas_call(
        body, name=name, grid=(S // tr,),
        in_specs=[third, third, third, pl.BlockSpec((tr, dzuv.shape[1]), lambda i: (i, 0)), tab, tab, pl.BlockSpec(memory_space=pl.ANY)],
        out_specs=pl.BlockSpec((tr, wide), lambda i: (i, 0)), out_shape=jax.ShapeDtypeStruct(dz.shape, dz.dtype),
        input_output_aliases={6: 0},
        compiler_params=pltpu.CompilerParams(dimension_semantics=("parallel",),
                                             vmem_limit_bytes=_vmem_limit(tr * wide * 4, tr * wide * 4)),
    )(dq, dk, dv, dzuv, cosf, sinf, dz)


ATTN_TQ = 256
ATTN_SCALE = HEAD_DIM ** -0.5
ATTN_PAD_MAX = RADIUS * max(d for _, d in ATTN_GROUPS)


def _band_bias(shape, q_axis, d):
    kq = lax.broadcasted_iota(jnp.int32, shape, 1 - q_axis) - lax.broadcasted_iota(jnp.int32, shape, q_axis) - RADIUS * d
    return jnp.where((jnp.abs(kq) <= RADIUS * d) & ((kq & (d - 1)) == 0), 0.0, NEG_INF).astype(F32)


def _fill_padded(dst, src, d, S):
    pad = RADIUS * d
    dst[0:pad, :] = jnp.zeros((pad, HEAD_DIM), dst.dtype)
    dst[pad:pad + S, :] = src[...]
    dst[pad + S:pad + S + pad, :] = jnp.zeros((pad, HEAD_DIM), dst.dtype)


_NT = (((1,), (1,)), ((), ()))


def _attn_fwd(name, qkv):
    S = qkv.shape[0]
    T = ATTN_TQ
    nq = N_GROUPS * HEADS
    widths = [T + 2 * RADIUS * d for _, d in ATTN_GROUPS]

    def body(*refs):
        q_refs, k_refs, v_refs = refs[0:3], refs[3:6], refs[6:9]
        o_ref, lc_ref = refs[9:11]
        kp, vp, bias = refs[11:14], refs[14:17], refs[17:20]
        i0 = pl.multiple_of(pl.program_id(1) * T, T)

        @pl.when(pl.program_id(1) == 0)
        def _():
            for g, (_, d) in enumerate(ATTN_GROUPS):
                _fill_padded(kp[g], k_refs[g], d, S)
                _fill_padded(vp[g], v_refs[g], d, S)
                bias[g][...] = _band_bias((T, widths[g]), 0, d)

        m = jnp.full((T, 1), NEG_INF, F32)
        l = jnp.zeros((T, 1), F32)
        acc = jnp.zeros((T, HEAD_DIM), F32)
        for g, (_, d) in enumerate(ATTN_GROUPS):
            W = widths[g]
            kw = kp[g][pl.ds(i0, W), :]
            vw = vp[g][pl.ds(i0, W), :]
            key = i0 - RADIUS * d + lax.broadcasted_iota(jnp.int32, (1, W), 1)
            in_seq = jnp.where((key >= 0) & (key < S), 0.0, NEG_INF).astype(F32)
            s = lax.dot_general(q_refs[g][...], kw, _NT, preferred_element_type=F32) + bias[g][...] + in_seq
            m_new = jnp.maximum(m, jnp.max(s, axis=1, keepdims=True))
            alpha = jnp.exp(m - m_new)
            p = jnp.exp(s - m_new)
            l = l * alpha + jnp.sum(p, axis=1, keepdims=True)
            acc = acc * alpha + jnp.dot(p.astype(BF16), vw, preferred_element_type=F32)
            m = m_new
        o_ref[...] = (acc / l).astype(BF16)
        lc_ref[...] = m + jnp.log(l)

    in_specs = [pl.BlockSpec((T, HEAD_DIM), lambda h, i, g=g: (i, g * HEADS + h)) for g in range(N_GROUPS)]
    in_specs += [pl.BlockSpec((S, HEAD_DIM), lambda h, i, g=g: (0, nq + g * HEADS + h)) for g in range(N_GROUPS)]
    in_specs += [pl.BlockSpec((S, HEAD_DIM), lambda h, i, g=g: (0, 2 * nq + g * HEADS + h)) for g in range(N_GROUPS)]
    padded = [pltpu.VMEM((S + 2 * RADIUS * d, HEAD_DIM), BF16) for _, d in ATTN_GROUPS]
    scratch = padded + padded + [pltpu.VMEM((T, W), F32) for W in widths]
    scratch_bytes = sum(2 * (S + 2 * RADIUS * d) * HEAD_DIM * 2 for _, d in ATTN_GROUPS) + sum(T * W * 4 for W in widths)
    return pl.pallas_call(
        body, name=name, grid=(HEADS, S // T), in_specs=in_specs,
        out_specs=[pl.BlockSpec((T, HEAD_DIM), lambda h, i: (i, h)), pl.BlockSpec((None, T, 1), lambda h, i: (h, i, 0))],
        out_shape=[jax.ShapeDtypeStruct((S, ATTN_W), BF16), jax.ShapeDtypeStruct((HEADS, S, 1), F32)],
        scratch_shapes=scratch,
        compiler_params=pltpu.CompilerParams(dimension_semantics=("parallel", "arbitrary"),
                                             vmem_limit_bytes=_vmem_limit(6 * S * HEAD_DIM * 2 + 8 * T * HEAD_DIM * 4,
                                                                          scratch_bytes + 4 * T * widths[-1] * 4)),
    )(*([qkv] * 9))


_TN = (((0,), (0,)), ((), ()))


def _attn_bwd(name, qkv, attn, dattn, lse_c):
    S = qkv.shape[0]
    T = ATTN_TQ
    nq = N_GROUPS * HEADS
    W3 = QKV_W // 3
    n_i = S // T
    wmax = T + 2 * ATTN_PAD_MAX
    s_pad = S + 2 * ATTN_PAD_MAX

    def body(q_ref, k_ref, v_ref, o_ref, do_ref, lc_ref, dq_ref, dk_ref, dv_ref, kp, vp, dk_acc, dv_acc, bias):
        g_id, i = pl.program_id(1), pl.program_id(2)
        i0 = pl.multiple_of(i * T, T)
        q, do = q_ref[...], do_ref[...]
        delta = jnp.sum(do.astype(F32) * o_ref[...].astype(F32), axis=1, keepdims=True)
        lse = lc_ref[...]

        def group(d):
            W, pad = T + 2 * RADIUS * d, RADIUS * d

            @pl.when(i == 0)
            def _():
                _fill_padded(kp, k_ref, d, S)
                _fill_padded(vp, v_ref, d, S)
                dk_acc[...] = jnp.zeros_like(dk_acc)
                dv_acc[...] = jnp.zeros_like(dv_acc)
                bias[:, 0:W] = _band_bias((T, W), 0, d)

            kw = kp[pl.ds(i0, W), :]
            vw = vp[pl.ds(i0, W), :]
            key = i0 - pad + lax.broadcasted_iota(jnp.int32, (1, W), 1)
            in_seq = jnp.where((key >= 0) & (key < S), 0.0, NEG_INF).astype(F32)
            s = lax.dot_general(q, kw, _NT, preferred_element_type=F32) + bias[:, 0:W] + in_seq
            p = jnp.exp(s - lse)
            dp = lax.dot_general(do, vw, _NT, preferred_element_type=F32)
            ds = (p * (dp - delta)).astype(BF16)
            dq_ref[...] = (jnp.dot(ds, kw, preferred_element_type=F32) * ATTN_SCALE).astype(BF16)
            dk_acc[pl.ds(i0, W), :] += lax.dot_general(ds, q, _TN, preferred_element_type=F32)
            dv_acc[pl.ds(i0, W), :] += lax.dot_general(p.astype(BF16), do, _TN, preferred_element_type=F32)

            @pl.when(i == n_i - 1)
            def _():
                dk_ref[...] = dk_acc[pad:pad + S, :].astype(BF16)
                dv_ref[...] = dv_acc[pad:pad + S, :].astype(BF16)

        for g, (_, d) in enumerate(ATTN_GROUPS):
            pl.when(g_id == g)(functools.partial(group, d))

    tile = lambda off: pl.BlockSpec((T, HEAD_DIM), lambda h, g, i: (i, off + g * HEADS + h))
    full = lambda off: pl.BlockSpec((S, HEAD_DIM), lambda h, g, i: (0, off + g * HEADS + h))
    headt = pl.BlockSpec((T, HEAD_DIM), lambda h, g, i: (i, h))
    scratch_bytes = 2 * s_pad * HEAD_DIM * (2 + 4) + T * wmax * 4
    return pl.pallas_call(
        body, name=name, grid=(HEADS, N_GROUPS, n_i),
        in_specs=[tile(0), full(nq), full(2 * nq), headt, headt, pl.BlockSpec((None, T, 1), lambda h, g, i: (h, i, 0))],
        out_specs=[tile(0), full(0), full(0)],
        out_shape=[jax.ShapeDtypeStruct((S, W3), BF16)] * 3,
        scratch_shapes=[pltpu.VMEM((s_pad, HEAD_DIM), BF16), pltpu.VMEM((s_pad, HEAD_DIM), BF16),
                        pltpu.VMEM((s_pad, HEAD_DIM), F32), pltpu.VMEM((s_pad, HEAD_DIM), F32), pltpu.VMEM((T, wmax), F32)],
        compiler_params=pltpu.CompilerParams(dimension_semantics=("parallel", "arbitrary", "arbitrary"),
                                             vmem_limit_bytes=_vmem_limit(4 * S * HEAD_DIM * 2 + 8 * T * HEAD_DIM * 4,
                                                                          scratch_bytes + 5 * T * wmax * 4)),
    )(qkv, qkv, qkv, attn, dattn, lse_c)


def _sg_parts(u, v, lng, lnb):
    gu = _gelu(u)
    gv = _gelu(v)
    mu = jnp.mean(gv, axis=-1, keepdims=True)
    xc = gv - mu
    rstd = lax.rsqrt(jnp.mean(xc * xc, axis=-1, keepdims=True) + NORM_EPS)
    xhat = xc * rstd
    vn = xhat * lng + lnb
    return gu, xhat, rstd, vn


def _sg_fwd(name, z, sg_w, sg_bc, lng, lnb, o_sg0):
    S = z.shape[0]
    T = SG_CHUNK
    cb = 512
    assert o_sg0 % cb == 0
    b0 = o_sg0 // cb

    def body(u0, u1, v0, v1, w_ref, b_ref, g_ref, be_ref, o_ref):
        u = jnp.concatenate([u0[...], u1[...]], axis=1)
        v = jnp.concatenate([v0[...], v1[...]], axis=1)
        gu, _, _, vn = _sg_parts(u, v, g_ref[...], be_ref[...])
        vnb = vn.astype(BF16)
        for g in range(SG_GROUPS):
            sl = slice(g * SG_CHUNK, (g + 1) * SG_CHUNK)
            mixed = jnp.dot(w_ref[g], vnb[:, sl], preferred_element_type=F32) + b_ref[g]
            o_ref[:, sl] = (gu[:, sl] * mixed).astype(BF16)

    zs = lambda k: pl.BlockSpec((T, cb), lambda i, k=k: (i, b0 + k))
    const3 = lambda shp: pl.BlockSpec(shp, lambda i: (0, 0, 0))
    vec = pl.BlockSpec((1, SG_W), lambda i: (0, 0))
    return pl.pallas_call(
        body, name=name, grid=(S // T,),
        in_specs=[zs(0), zs(1), zs(2), zs(3), const3((SG_GROUPS, SG_CHUNK, SG_CHUNK)), const3((SG_GROUPS, SG_CHUNK, 1)), vec, vec],
        out_specs=pl.BlockSpec((T, SG_W), lambda i: (i, 0)), out_shape=jax.ShapeDtypeStruct((S, SG_W), BF16),
        compiler_params=pltpu.CompilerParams(dimension_semantics=("parallel",), vmem_limit_bytes=_vmem_limit(4 * 1024 * 1024, 8 * T * SG_W * 4)),
    )(z, z, z, z, sg_w, sg_bc, lng, lnb)


def _sg_bwd(name, z, dsg, sg_w, sg_wt, sg_bc, lng, lnb, o_sg0):
    S = z.shape[0]
    T = SG_CHUNK
    cb = 512
    b0 = o_sg0 // cb

    def body(u0, u1, v0, v1, d_ref, w_ref, wt_ref, b_ref, g_ref, be_ref, dz_ref, dw_ref, db_ref, dg_ref, dbe_ref):
        i = pl.program_id(0)
        u = jnp.concatenate([u0[...], u1[...]], axis=1)
        v = jnp.concatenate([v0[...], v1[...]], axis=1)
        gu, xhat, rstd, vn = _sg_parts(u, v, g_ref[...], be_ref[...])
        vnb = vn.astype(BF16)
        dsg_v = d_ref[...].astype(F32)
        dmix = dsg_v * gu
        dmixb = dmix.astype(BF16)
        dvn_parts, mixed_parts, dw_parts, db_parts = [], [], [], []
        for g in range(SG_GROUPS):
            sl = slice(g * SG_CHUNK, (g + 1) * SG_CHUNK)
            mixed_parts.append(jnp.dot(w_ref[g], vnb[:, sl], preferred_element_type=F32) + b_ref[g])
            dvn_parts.append(jnp.dot(wt_ref[g], dmixb[:, sl], preferred_element_type=F32))
            dw_parts.append(lax.dot_general(dmixb[:, sl], vnb[:, sl], _NT, preferred_element_type=F32))
            db_parts.append(jnp.sum(dmix[:, sl], axis=1, keepdims=True))
        mixed = jnp.concatenate(mixed_parts, axis=1)
        dvn = jnp.concatenate(dvn_parts, axis=1)
        dzu = dsg_v * mixed * _gelu_grad(u)
        dxh = dvn * g_ref[...]
        dgv = rstd * (dxh - jnp.mean(dxh, axis=-1, keepdims=True) - xhat * jnp.mean(dxh * xhat, axis=-1, keepdims=True))
        dzv = dgv * _gelu_grad(v)
        dz_ref[:, :SG_W] = dzu.astype(BF16)
        dz_ref[:, SG_W:] = dzv.astype(BF16)
        dgp = jnp.sum(dvn * xhat, axis=0, keepdims=True)
        dbp = jnp.sum(dvn, axis=0, keepdims=True)

        @pl.when(i == 0)
        def _():
            for g in range(SG_GROUPS):
                dw_ref[g] = dw_parts[g]
                db_ref[g] = db_parts[g]
            dg_ref[...] = dgp
            dbe_ref[...] = dbp

        @pl.when(i > 0)
        def _():
            for g in range(SG_GROUPS):
                dw_ref[g] += dw_parts[g]
                db_ref[g] += db_parts[g]
            dg_ref[...] += dgp
            dbe_ref[...] += dbp

    zs = lambda k: pl.BlockSpec((T, cb), lambda i, k=k: (i, b0 + k))
    const3 = lambda shp: pl.BlockSpec(shp, lambda i: (0, 0, 0))
    vec = pl.BlockSpec((1, SG_W), lambda i: (0, 0))
    return pl.pallas_call(
        body, name=name, grid=(S // T,),
        in_specs=[zs(0), zs(1), zs(2), zs(3), pl.BlockSpec((T, SG_W), lambda i: (i, 0)),
                  const3((SG_GROUPS, SG_CHUNK, SG_CHUNK)), const3((SG_GROUPS, SG_CHUNK, SG_CHUNK)), const3((SG_GROUPS, SG_CHUNK, 1)),
                  vec, vec],
        out_specs=[pl.BlockSpec((T, 2 * SG_W), lambda i: (i, 0)), const3((SG_GROUPS, SG_CHUNK, SG_CHUNK)),
                   const3((SG_GROUPS, SG_CHUNK, 1)), vec, vec],
        out_shape=[jax.ShapeDtypeStruct((S, 2 * SG_W), BF16), jax.ShapeDtypeStruct((SG_GROUPS, SG_CHUNK, SG_CHUNK), F32),
                   jax.ShapeDtypeStruct((SG_GROUPS, SG_CHUNK, 1), F32), jax.ShapeDtypeStruct((1, SG_W), F32),
                   jax.ShapeDtypeStruct((1, SG_W), F32)],
        compiler_params=pltpu.CompilerParams(dimension_semantics=("arbitrary",),
                                             vmem_limit_bytes=_vmem_limit(6 * 1024 * 1024, 16 * T * SG_W * 4)),
    )(z, z, z, z, dsg, sg_w, sg_wt, sg_bc, lng, lnb)


def _gate_bwd(name, z, dmerged, y_attn, y_sg, o_g0, in_w):
    S, D = dmerged.shape
    tr = _pick(S, (512, 256, 128, 8))
    cb = _pick(D, (512, 256, 128))
    assert o_g0 % cb == 0
    nd = D // cb
    b0 = o_g0 // cb

    def body(z_ref, dm_ref, ya_ref, ys_ref, dz_ref, dy_ref):
        jj = pl.program_id(1)
        gate = _sigmoid(z_ref[...])
        dm = dm_ref[...].astype(F32)
        y = jnp.where(jj < nd, ya_ref[...], ys_ref[...]).astype(F32)
        dz_ref[...] = (dm * y * gate * (1.0 - gate)).astype(BF16)
        dy_ref[...] = (dm * gate).astype(BF16)

    half = pl.BlockSpec((tr, cb), lambda i, jj: (i, jj % nd))
    return pl.pallas_call(
        body, name=name, grid=(S // tr, 2 * nd),
        in_specs=[pl.BlockSpec((tr, cb), lambda i, jj: (i, b0 + jj)), half, half, half],
        out_specs=[pl.BlockSpec((tr, cb), lambda i, jj: (i, b0 + jj)), pl.BlockSpec((tr, cb), lambda i, jj: (i, jj))],
        out_shape=[jax.ShapeDtypeStruct((S, in_w), BF16), jax.ShapeDtypeStruct((S, 2 * D), BF16)],
        compiler_params=pltpu.CompilerParams(dimension_semantics=("parallel", "arbitrary"),
                                             vmem_limit_bytes=_vmem_limit(tr * cb * 14, 6 * tr * cb * 4)),
    )(z, dmerged, y_attn, y_sg)


def _row(v):
    return v.reshape(1, -1)


def _local_step(x, p, target, wf, small, after_group):
    S, D = x.shape
    L = p.shape[0]
    in_w = wf["w_in"][0].shape[1]
    ff = wf["w_ff_gate"][0].shape[1]
    ple = p.shape[2]
    o_sg0, o_g0 = QKV_W, QKV_W + 2 * SG_W
    cosf, sinf = _rope_tables(S)
    pb = p.astype(BF16)
    tmb = _pick(S, (1024, 512, 256))
    tn_in = _pick(in_w, (768, 1024, 512))
    tn_d = _pick(D, (1024, 512, 256))
    tn_g = _pick(D, (512, 256))
    tn_ff = _pick(ff, (512, 256))

    saved = []
    xs = x
    for i in range(L):
        sv = {"x0": xs}
        h = _rmsnorm_fwd(f"norm_mix_{i}", xs, _row(small["norm_mix"][i]))
        (z,) = _mm(f"in_proj_{i}", [dict(a=h, b=wf["w_in"], bl=i, mode="nn", K=D)], S, in_w,
                   [dict(shape=(S, in_w), dtype=F32)], _first, tm=tmb, tn=tn_in)
        qkv = _rope_fwd(f"rope_{i}", z, cosf, sinf)
        attn, lse_c = _attn_fwd(f"attn_{i}", qkv)
        sgw = small["sg_w"][i].astype(BF16)
        sgbc = small["sg_b"][i].reshape(SG_GROUPS, SG_CHUNK, 1)
        sg = _sg_fwd(f"sgu_{i}", z, sgw, sgbc, _row(small["sg_ln_g"][i]), _row(small["sg_ln_b"][i]), o_sg0)

        def merge(accs, tiles, rows):
            ya, ys = accs[0].astype(BF16), accs[1].astype(BF16)
            g0, g1 = _sigmoid(tiles[0]), _sigmoid(tiles[1])
            return [ya, ys, g0 * ya.astype(F32) + g1 * ys.astype(F32)]

        y_attn, y_sg, merged = _mm(
            f"branches_{i}",
            [dict(a=attn, b=wf["w_br_attn"], bl=i, mode="nn", K=ATTN_W), dict(a=sg, b=wf["w_br_sg"], bl=i, mode="nn", K=SG_W)],
            S, D, [dict(shape=(S, D), dtype=BF16)] * 3, merge,
            tiles=[dict(x=z, off=o_g0), dict(x=z, off=o_g0 + D)], tm=tmb, tn=tn_g)
        (x1,) = _mm(f"out_proj_{i}", [dict(a=merged, b=wf["w_out"], bl=i, mode="nn", K=D)], S, D,
                    [dict(shape=(S, D), dtype=F32)], lambda a, t, r: [t[0] + a[0]], tiles=[dict(x=xs)], tm=tmb, tn=tn_d)
        h2 = _rmsnorm_fwd(f"norm_ffn_{i}", x1, _row(small["norm_ffn"][i]))

        def swiglu(accs, tiles, rows):
            fg = accs[0].astype(BF16).astype(F32)
            fu = accs[1].astype(BF16).astype(F32)
            return [fg, fu, fg * _sigmoid(fg) * fu]

        ffg, ffu, act = _mm(
            f"ff_in_{i}",
            [dict(a=h2, b=wf["w_ff_gate"], bl=i, mode="nn", K=D), dict(a=h2, b=wf["w_ff_up"], bl=i, mode="nn", K=D)],
            S, ff, [dict(shape=(S, ff), dtype=BF16)] * 3, swiglu, tm=tmb, tn=tn_ff)
        (x2,) = _mm(f"ff_out_{i}", [dict(a=act, b=wf["w_ff_down"], bl=i, mode="nn", K=ff)], S, D,
                    [dict(shape=(S, D), dtype=F32)], lambda a, t, r: [t[0] + a[0]], tiles=[dict(x=x1)], tm=tmb, tn=tn_d)
        h3 = _rmsnorm_fwd(f"norm_ple_{i}", x2, _row(small["norm_ple"][i]))

        def ple_mix(accs, tiles, rows):
            gp = _sigmoid(accs[0]).astype(BF16)
            pe = accs[1].astype(BF16)
            return [tiles[0] + gp.astype(F32) * pe.astype(F32), gp, pe]

        x3, gp, pe = _mm(
            f"ple_{i}",
            [dict(a=h3, b=wf["w_ple_gate"], bl=i, mode="nn", K=D), dict(a=pb, al=i, b=wf["w_ple"], bl=i, mode="nn", K=ple)],
            S, D, [dict(shape=(S, D), dtype=F32), dict(shape=(S, D), dtype=BF16), dict(shape=(S, D), dtype=BF16)], ple_mix,
            tiles=[dict(x=x2)], tm=tmb, tn=tn_g)
        sv.update(h=h, z=z, qkv=qkv, attn=attn, lse_c=lse_c, sg=sg, y_attn=y_attn, y_sg=y_sg, merged=merged,
                  x1=x1, h2=h2, ffg=ffg, ffu=ffu, act=act, x2=x2, h3=h3, gp=gp, pe=pe, sgw=sgw, sgbc=sgbc)
        saved.append(sv)
        xs = x3

    loss_cell, dx, dxb, dg_final = _loss_head(xs, _row(small["norm_final"]), target)

    gw = {n: [None] * L for n in BIG}
    gs = {n: [None] * L for n in SMALL if n != "norm_final"}

    def dw(n, i, a, a_off, b, bn_off, K_rows, N_cols, tm, tn):
        (gw[n][i],) = _mm(f"d_{n}_{i}", [dict(a=a, b=b, mode="tn", K=S, a_off=a_off, bn_off=bn_off)], K_rows, N_cols,
                          [dict(shape=(K_rows, N_cols), dtype=BF16)], _first, tm=tm, tn=tn)

    for i in reversed(range(L)):
        sv = saved[i]
        dpre, dpe = _ew(f"ple_gate_bwd_{i}",
                        lambda d, g, e: [d * e.astype(F32) * g.astype(F32) * (1.0 - g.astype(F32)), d * g.astype(F32)],
                        [dx, sv["gp"], sv["pe"]], [BF16, BF16], S, D)
        (dh3,) = _mm(f"d_h3_{i}", [dict(a=dpre, b=wf["w_ple_gate"], bl=i, mode="nt", K=D)], S, D,
                     [dict(shape=(S, D), dtype=BF16)], _first, tm=tmb, tn=tn_d)
        dw("w_ple_gate", i, sv["h3"], 0, dpre, 0, D, D, tn_d, tn_d)
        dw("w_ple", i, pb[i], 0, dpe, 0, ple, D, _pick(ple, (256, 128)), _pick(D, (2048, 1024, 512, 256)))
        dx, dxb, gs["norm_ple"][i] = _rmsnorm_bwd(f"norm_ple_bwd_{i}", sv["x2"], _row(small["norm_ple"][i]), dh3, dx)
        def swiglu_bwd(accs, tiles, rows):
            da = accs[0].astype(BF16).astype(F32)
            fg, fu = tiles[0].astype(F32), tiles[1].astype(F32)
            sg_ = _sigmoid(fg)
            return [da * fu * (sg_ * (1.0 + fg * (1.0 - sg_))), da * (fg * sg_)]

        dffg, dffu = _mm(f"d_act_{i}", [dict(a=dxb, b=wf["w_ff_down"], bl=i, mode="nt", K=D)], S, ff,
                         [dict(shape=(S, ff), dtype=BF16)] * 2, swiglu_bwd, tiles=[dict(x=sv["ffg"]), dict(x=sv["ffu"])],
                         tm=tmb, tn=tn_ff)
        dw("w_ff_down", i, sv["act"], 0, dxb, 0, ff, D, tn_ff, _pick(D, (2048, 1024, 512, 256)))
        dw("w_ff_gate", i, sv["h2"], 0, dffg, 0, D, ff, _pick(D, (2048, 1024, 512, 256)), tn_ff)
        dw("w_ff_up", i, sv["h2"], 0, dffu, 0, D, ff, _pick(D, (2048, 1024, 512, 256)), tn_ff)
        (dffg, dffu), _ = lax.optimization_barrier(((dffg, dffu), after_group(i, "ffn", {n: gw[n][i] for n in GRAD_GROUPS["ffn"]})))
        (dh2,) = _mm(f"d_h2_{i}", [dict(a=dffg, b=wf["w_ff_gate"], bl=i, mode="nt", K=ff),
                                   dict(a=dffu, b=wf["w_ff_up"], bl=i, mode="nt", K=ff)], S, D,
                     [dict(shape=(S, D), dtype=BF16)], lambda a, t, r: [a[0] + a[1]], tm=tmb, tn=tn_d)
        dx, dxb, gs["norm_ffn"][i] = _rmsnorm_bwd(f"norm_ffn_bwd_{i}", sv["x1"], _row(small["norm_ffn"][i]), dh2, dx)
        (dmerged,) = _mm(f"d_merged_{i}", [dict(a=dxb, b=wf["w_out"], bl=i, mode="nt", K=D)], S, D,
                         [dict(shape=(S, D), dtype=BF16)], _first, tm=tmb, tn=tn_d)
        dw("w_out", i, sv["merged"], 0, dxb, 0, D, D, tn_d, tn_d)
        dz, dy = _gate_bwd(f"gate_bwd_{i}", sv["z"], dmerged, sv["y_attn"], sv["y_sg"], o_g0, in_w)
        (dattn,) = _mm(f"d_attn_{i}", [dict(a=dy, b=wf["w_br_attn"], bl=i, mode="nt", K=D)], S, ATTN_W,
                       [dict(shape=(S, ATTN_W), dtype=BF16)], _first, tm=tmb, tn=ATTN_W)
        (dsg,) = _mm(f"d_sg_{i}", [dict(a=dy, a_off=D, b=wf["w_br_sg"], bl=i, mode="nt", K=D)], S, SG_W,
                     [dict(shape=(S, SG_W), dtype=BF16)], _first, tm=tmb, tn=SG_W)
        dw("w_br_attn", i, sv["attn"], 0, dy, 0, ATTN_W, D, ATTN_W, _pick(D, (2048, 1024, 512, 256)))
        dw("w_br_sg", i, sv["sg"], 0, dy, D, SG_W, D, SG_W, _pick(D, (1024, 512, 256)))
        sgwt = jnp.swapaxes(small["sg_w"][i], 1, 2).astype(BF16)
        dzuv, gs["sg_w"][i], dsgb, dlg, dlb = _sg_bwd(f"sgu_bwd_{i}", sv["z"], dsg, sv["sgw"], sgwt, sv["sgbc"],
                                                      _row(small["sg_ln_g"][i]), _row(small["sg_ln_b"][i]), o_sg0)
        gs["sg_b"][i], gs["sg_ln_g"][i], gs["sg_ln_b"][i] = dsgb.reshape(SG_GROUPS, SG_CHUNK), dlg[0], dlb[0]
        dq, dk, dv = _attn_bwd(f"attn_bwd_{i}", sv["qkv"], sv["attn"], dattn, sv["lse_c"])
        dz = _rope_bwd(f"rope_bwd_{i}", dq, dk, dv, dzuv, cosf, sinf, dz)
        dw("w_in", i, sv["h"], 0, dz, 0, D, in_w, tn_d, tn_in)
        dz, _ = lax.optimization_barrier((dz, after_group(i, "mix", {n: gw[n][i] for n in GRAD_GROUPS["mix"]})))
        (dh,) = _mm(f"d_h_{i}", [dict(a=dz, b=wf["w_in"], bl=i, mode="nt", K=in_w)], S, D,
                    [dict(shape=(S, D), dtype=BF16)], _first, tm=tmb, tn=tn_d)
        dx, dxb, gs["norm_mix"][i] = _rmsnorm_bwd(f"norm_mix_bwd_{i}", sv["x0"], _row(small["norm_mix"][i]), dh, dx)

    gsmall ={n: jnp.stack([jnp.reshape(v, small[n].shape[1:]) for v in gs[n]]) for n in gs}
    gsmall["norm_final"] = dg_final[0]
    return loss_cell, dx, gsmall


def _place():
    x, y, c = lax.axis_index("x"), lax.axis_index("y"), lax.axis_index("c")
    return x, y, c, 2 * x + y


def _chip_of(s):
    return s // 2, s % 2


def _aligned(v, m):
    return v if isinstance(v, int) else pl.multiple_of(v, m)


def _piece(name, shape, s, c):
    K, N = shape
    if name in ROW_SHARDED or name == SMALL_BLOCKS:
        ks = K // 4
        return s * ks + c * (ks // 2), ks // 2, 0, N
    ns = N // 4
    return c * (K // 2), K // 2, s * ns, ns


def _handshake(peers):
    barrier = pltpu.get_barrier_semaphore()
    for peer in peers:
        pl.semaphore_signal(barrier, inc=1, device_id=peer, device_id_type=MESH)
    pl.semaphore_wait(barrier, len(peers))


def _gather_body(names, shapes, src, dst, send_sems, recv_sems, local_sems):
    n_w = len(names)
    x, y, c, s = _place()
    sib = (x, y, 1 - c)
    rel = [1, 2, 3]

    def where(w, ps, pc):
        r0, nr, c0, nc = _piece(names[w], shapes[names[w]], ps, pc)
        return dst[w].at[pl.ds(_aligned(r0, 16), nr), pl.ds(_aligned(c0, LANES), nc)]

    def copy(w, k, ps, pc, to, from_src=False):
        return pltpu.make_async_remote_copy(
            src_ref=src[w] if from_src else where(w, ps, pc), dst_ref=where(w, ps, pc),
            send_sem=send_sems.at[w, k], recv_sem=recv_sems.at[w, k], device_id=to, device_id_type=MESH)

    mine, first, passed = [], [], []
    for w in range(n_w):
        cp = pltpu.make_async_copy(src[w], where(w, s, c), local_sems.at[w])
        cp.start()
        mine.append(cp)
        first.append(copy(w, 0, s, c, sib, from_src=True))
        for j in rel:
            first.append(copy(w, j, s, c, (*_chip_of(s ^ j), c), from_src=True))
    for cp in first:
        cp.start()
    for w in range(n_w):
        for j in rel:
            copy(w, j, s ^ j, c, sib).wait_recv()
            fw = copy(w, 3 + j, s ^ j, c, sib)
            fw.start()
            passed.append(fw)
    for w in range(n_w):
        copy(w, 0, s, 1 - c, sib).wait_recv()
        for j in rel:
            copy(w, 3 + j, s ^ j, 1 - c, sib).wait_recv()
    for cp in first + passed:
        cp.wait_send()
    for cp in mine:
        cp.wait()


def _gather_sems(n_w):
    return (pltpu.SemaphoreType.DMA((n_w, 7)), pltpu.SemaphoreType.DMA((n_w, 7)), pltpu.SemaphoreType.DMA((n_w,)))


def _gather_peers():
    x, y, c, s = _place()
    return [(x, y, 1 - c)] + [(*_chip_of(s ^ j), c) for j in (1, 2, 3)]


def _gather_weights_async(name, pieces, shapes):
    names = list(pieces)
    n_w = len(names)
    src = [jax.new_ref(pieces[n], memory_space=pltpu.MemorySpace.HBM) for n in names]
    dst = [jax.empty_ref(jax.ShapeDtypeStruct(tuple(shapes[n]), pieces[n].dtype), memory_space=pltpu.MemorySpace.HBM)
           for n in names]

    @pl.kernel(mesh=plsc.ScalarSubcoreMesh(axis_name="seq", num_cores=1), name=name, scratch_types=_gather_sems(n_w),
               compiler_params=pltpu.CompilerParams(collective_id=GATHER_COLLECTIVE_ID))
    def launch(send_sems, recv_sems, local_sems):
        _handshake(_gather_peers())
        _gather_body(names, shapes, src, dst, send_sems, recv_sems, local_sems)

    launch()
    return {n: d[...] for n, d in zip(names, dst)}


def _halves_view(name, g):
    L, K, N = g.shape
    if name in ROW_SHARDED:
        return g.reshape(L * 4, 2, K // 8, N)
    return g.reshape(L, 2, K // 2, N)


def _all_peers():
    x, y, c, s = _place()
    return [(x, y, 1 - c)] + [(*_chip_of(s ^ j), h) for j in (1, 2, 3) for h in (0, 1)]


def _scatter_partials_async(name, views):
    names = list(views)
    n_w = len(names)
    src = [jax.new_ref(views[n], memory_space=pltpu.MemorySpace.HBM) for n in names]
    dst = [jax.empty_ref(jax.ShapeDtypeStruct(_partials_out_shape(n, views[n].shape), BF16), memory_space=pltpu.MemorySpace.HBM)
           for n in names]

    @pl.kernel(mesh=plsc.ScalarSubcoreMesh(axis_name="seq", num_cores=1), name=name, scratch_types=_partials_sems(n_w),
               compiler_params=pltpu.CompilerParams(collective_id=PARTIALS_COLLECTIVE_ID))
    def launch(send_sems, recv_sems):
        _handshake(_all_peers())
        _scatter_partials_body(names, src, dst, send_sems, recv_sems)

    launch()
    return {n: d[...] for n, d in zip(names, dst)}


def _partials_out_shape(name, v):
    return (7, 1, v[2], v[3] if name in ROW_SHARDED else v[3] // 4)


def _partials_sems(n_w):
    return (pltpu.SemaphoreType.DMA((n_w, 7)), pltpu.SemaphoreType.DMA((n_w, 7)))


def _scatter_partials_body(names, src, dst, send_sems, recv_sems):
    x, y, c, s = _place()

    def piece(w, t, h):
        if names[w] in ROW_SHARDED:
            return src[w].at[pl.ds(t, 1), h]
        ns = src[w].shape[3] // 4
        return src[w].at[:, h, :, pl.ds(pl.multiple_of(t * ns, LANES), ns)]

    sent = []
    for w in range(len(names)):
        for j in (1, 2, 3):
            for h in (0, 1):
                sent.append(pltpu.make_async_remote_copy(
                    src_ref=piece(w, s ^ j, h), dst_ref=dst[w].at[2 * (j - 1) + c], send_sem=send_sems.at[w, 2 * (j - 1) + h],
                    recv_sem=recv_sems.at[w, 2 * (j - 1) + c], device_id=(*_chip_of(s ^ j), h), device_id_type=MESH))
        sent.append(pltpu.make_async_remote_copy(
            src_ref=piece(w, s, 1 - c), dst_ref=dst[w].at[6], send_sem=send_sems.at[w, 6], recv_sem=recv_sems.at[w, 6],
            device_id=(x, y, 1 - c), device_id_type=MESH))
    for cp in sent:
        cp.start()
    for w in range(len(names)):
        for slot in range(7):
            pltpu.make_async_remote_copy(src_ref=dst[w].at[slot], dst_ref=dst[w].at[slot], send_sem=send_sems.at[w, slot],
                                         recv_sem=recv_sems.at[w, slot], device_id=(x, y, 1 - c), device_id_type=MESH).wait_recv()
    for cp in sent:
        cp.wait_send()


def _shard_sum_partials(name, view, parts, place, row_sharded, layer, n_layers, into):
    R, C = parts.shape[2:]
    tc = _pick(C, (2048, 1408, 1024, 896, 512, 384, 256, 128))
    tr = _pick(R, [t for t in (1024, 512, 256, 128, 64, 32, 16) if t * tc <= 2 * EW_TILE_ELEMS] + [8])

    def body(p_ref, own_ref, *rest):
        acc = own_ref[...].astype(F32) + rest[6][...].astype(F32)
        for k in range(6):
            acc = acc + rest[k][...].astype(F32)
        rest[-1][...] = acc

    if row_sharded:
        own_spec = pl.BlockSpec((None, None, tr, tc), lambda i, j, p: (p[1], p[0], i, j))
    else:
        own_spec = pl.BlockSpec((None, None, tr, tc), lambda i, j, p: (0, p[0], i, p[1] * (C // tc) + j))
    part = lambda k: pl.BlockSpec((None, None, tr, tc), lambda i, j, p, k=k: (k, 0, i, j))
    in_specs, args, aliases = [own_spec] + [part(k) for k in range(7)], [place, view] + [parts] * 7, {}
    if into is not None:
        in_specs.append(pl.BlockSpec(memory_space=pl.ANY))
        args.append(into)
        aliases = {9: 0}
    return pl.pallas_call(
        body, name=name, out_shape=jax.ShapeDtypeStruct((n_layers, 2, R, C), F32),
        grid_spec=pltpu.PrefetchScalarGridSpec(
            num_scalar_prefetch=1, grid=(R // tr, C // tc), in_specs=in_specs,
            out_specs=pl.BlockSpec((None, None, tr, tc), lambda i, j, p: (layer, p[0], i, j))),
        input_output_aliases=aliases,
        compiler_params=pltpu.CompilerParams(dimension_semantics=("parallel", "parallel"),
                                             vmem_limit_bytes=_vmem_limit(20 * tr * tc, 5 * tr * tc * 4)),
    )(*args)


def _share_halves(name, ghalf):
    names = list(ghalf)
    n_w = len(names)

    def body(*refs):
        src = refs[:n_w]
        dst = refs[n_w:2 * n_w]
        send_sems, recv_sems = refs[2 * n_w:]
        x, y, c, s = _place()
        remote = [pltpu.make_async_remote_copy(src_ref=src[w].at[:, c], dst_ref=dst[w].at[:, c], send_sem=send_sems.at[w],
                                               recv_sem=recv_sems.at[w], device_id=(x, y, 1 - c), device_id_type=MESH)
                  for w in range(n_w)]
        for cp in remote:
            cp.start()
        for cp in remote:
            cp.wait()

    anyspec = pl.BlockSpec(memory_space=pl.ANY)
    out = pl.pallas_call(
        body, name=name, in_specs=[anyspec] * n_w, out_specs=[anyspec] * n_w,
        out_shape=[jax.ShapeDtypeStruct(ghalf[n].shape, F32) for n in names],
        input_output_aliases={w: w for w in range(n_w)},
        scratch_shapes=[pltpu.SemaphoreType.DMA((n_w,)), pltpu.SemaphoreType.DMA((n_w,))],
    )(*[ghalf[n] for n in names])
    return dict(zip(names, out))


def _adamw_math(w, g, m, v):
    m = ADAM_B1 * m + (1.0 - ADAM_B1) * g
    v = ADAM_B2 * v + (1.0 - ADAM_B2) * (g * g)
    m_hat = m / (1.0 - ADAM_B1 ** ADAM_STEP)
    v_hat = v / (1.0 - ADAM_B2 ** ADAM_STEP)
    delta = -ADAM_LR * (m_hat / (jnp.sqrt(v_hat) + ADAM_EPS) + ADAM_WD * w)
    return delta, m, v


def _adamw(name, w, g, m, v):
    shape = w.shape
    C = shape[-1]
    R = math.prod(shape[:-1])
    f = lambda a: a.reshape(R, C)
    res = _ew(name, lambda w_, g_, m_, v_: [g_, *_adamw_math(w_, g_, m_, v_)], [f(w), f(g), f(m), f(v)], [F32] * 4, R, C)
    return [r.reshape(shape) for r in res]


def _pack_small(d):
    return jnp.concatenate([d[n].reshape(-1, LANES) for n in SMALL], axis=0)


def _unpack_small(flat, like):
    out, r = {}, 0
    for n in SMALL:
        k = like[n].size // LANES
        out[n] = flat[r:r + k].reshape(like[n].shape)
        r += k
    return out


def _small_update(gall, w, m, v):
    M = w.shape[0]
    tr = _pick(M, (552, 276, 184, 96, 48, 24, 8))

    def body(*refs):
        g = refs[0][...]
        for d in range(1, 8):
            g = g + refs[d][...]
        delta, nm, nv = _adamw_math(refs[8][...], g, refs[9][...], refs[10][...])
        refs[11][...] = g
        refs[12][...] = delta
        refs[13][...] = nm
        refs[14][...] = nv

    blk = pl.BlockSpec((tr, LANES), lambda i: (i, 0))
    in_specs = [pl.BlockSpec((tr, LANES), lambda i, d=d: (d * (M // tr) + i, 0)) for d in range(8)] + [blk] * 3
    return pl.pallas_call(
        body, name="small_update", grid=(M // tr,), in_specs=in_specs, out_specs=[blk] * 4,
        out_shape=[jax.ShapeDtypeStruct((M, LANES), F32)] * 4,
        compiler_params=pltpu.CompilerParams(dimension_semantics=("parallel",), vmem_limit_bytes=_vmem_limit(15 * tr * LANES * 4)),
    )(*([gall] * 8), w, m, v)


def _step(x, p, target, w, m, v):
    L = p.shape[0]
    x_i, y_i, c, s = _place()
    shapes = {}
    for n in BIG:
        _, K, N = w[n].shape
        shapes[n] = (4 * K, N) if n in ROW_SHARDED else (K, 4 * N)
    def pieces_of(i):
        return {n: lax.dynamic_slice_in_dim(w[n][i], c * (w[n].shape[1] // 2), w[n].shape[1] // 2, axis=0).astype(BF16)
                for n in BIG}

    def piece(n, i, after=None):
        wn = w[n] if after is None else lax.optimization_barrier((w[n], after))[0]
        return lax.dynamic_slice_in_dim(wn[i], c * (w[n].shape[1] // 2), w[n].shape[1] // 2, axis=0).astype(BF16)

    w_in0 = piece("w_in", 0)
    layers = [_gather_weights_async("gather_weights_0_w_in", {"w_in": w_in0}, shapes)]
    for i in range(L):
        if i > 0:
            layers.append({})
        for g in ("mix", "ffn"):
            mine = {n: piece(n, i, w_in0) for n in GRAD_GROUPS[g] if n not in layers[i]}
            layers[i].update(_gather_weights_async(f"gather_weights_{i}_{g}", mine, shapes))
    wf = {n: [layers[i][n] for i in range(L)] for n in BIG}
    small = {n: w[n] for n in SMALL}
    place = jnp.stack([c, s]).astype(jnp.int32)
    reduced = []

    def after_group(i, group, grads):
        views = {n: _halves_view(n, g[None]) for n, g in grads.items()}
        reduced.append((i, group, views, _scatter_partials_async(f"scatter_partials_{i}_{group}", views)))
        return views

    loss_cell, dx, gsmall = _local_step(x[0], p[:, 0], target[0], wf, small, after_group)
    loss = lax.psum(jnp.sum(loss_cell), ("x", "y", "c"))
    packed = _pack_small(gsmall)
    gall = _gather_weights_async("gather_small", {SMALL_BLOCKS: packed}, {SMALL_BLOCKS: (8 * packed.shape[0], LANES)})[SMALL_BLOCKS]
    ghalf = {n: None for n in BIG}
    grad, delta, new_m, new_v = {}, {}, {}, {}
    done = None

    def finish(group):
        gfull = _share_halves(f"share_halves_{group}", {n: ghalf[n] for n in GRAD_GROUPS[group]})
        for n in GRAD_GROUPS[group]:
            grad[n], delta[n], new_m[n], new_v[n] = _adamw(f"adamw_{n}", w[n], gfull[n].reshape(w[n].shape), m[n], v[n])
        return {n: delta[n] for n in GRAD_GROUPS[group]}

    for k, (i, group, own, parts) in enumerate(reduced):
        if k == len(reduced) - 1:
            done = finish("ffn")
        parts, _ = lax.optimization_barrier((parts, done))
        for n in own:
            ghalf[n] = _shard_sum_partials(f"shard_sum_{n}_{i}", own[n], parts[n], place, n in ROW_SHARDED, i, L, ghalf[n])
        done = {n: ghalf[n] for n in own}
    finish("mix")
    gall, _ = lax.optimization_barrier((gall, delta))
    gsum, dsm, nms, nvs =_small_update(gall, _pack_small(small), _pack_small({n: m[n] for n in SMALL}),
                                        _pack_small({n: v[n] for n in SMALL}))
    for dst, flat in ((grad, gsum), (delta, dsm), (new_m, nms), (new_v, nvs)):
        dst.update(_unpack_small(flat, small))
    return loss, dx[None], grad, delta, new_m, new_v


def kernel(x, p, w_in, w_br_attn, w_br_sg, w_out, sg_w, sg_b, sg_ln_g, sg_ln_b, norm_mix, norm_ffn, norm_ple, norm_final, w_ff_gate, w_ff_up, w_ff_down, w_ple_gate, w_ple, loss_target, m_w_in, m_w_br_attn, m_w_br_sg, m_w_out, m_sg_w, m_sg_b, m_sg_ln_g, m_sg_ln_b, m_norm_mix, m_norm_ffn, m_norm_ple, m_norm_final, m_w_ff_gate, m_w_ff_up, m_w_ff_down, m_w_ple_gate, m_w_ple, v_w_in, v_w_br_attn, v_w_br_sg, v_w_out, v_sg_w, v_sg_b, v_sg_ln_g, v_sg_ln_b, v_norm_mix, v_norm_ffn, v_norm_ple, v_norm_final, v_w_ff_gate, v_w_ff_up, v_w_ff_down, v_w_ple_gate, v_w_ple):
    w = dict(w_in=w_in, w_br_attn=w_br_attn, w_br_sg=w_br_sg, w_out=w_out, sg_w=sg_w, sg_b=sg_b, sg_ln_g=sg_ln_g, sg_ln_b=sg_ln_b,
             norm_mix=norm_mix, norm_ffn=norm_ffn, norm_ple=norm_ple, norm_final=norm_final, w_ff_gate=w_ff_gate, w_ff_up=w_ff_up,
             w_ff_down=w_ff_down, w_ple_gate=w_ple_gate, w_ple=w_ple)
    m = dict(w_in=m_w_in, w_br_attn=m_w_br_attn, w_br_sg=m_w_br_sg, w_out=m_w_out, sg_w=m_sg_w, sg_b=m_sg_b, sg_ln_g=m_sg_ln_g,
             sg_ln_b=m_sg_ln_b, norm_mix=m_norm_mix, norm_ffn=m_norm_ffn, norm_ple=m_norm_ple, norm_final=m_norm_final,
             w_ff_gate=m_w_ff_gate, w_ff_up=m_w_ff_up, w_ff_down=m_w_ff_down, w_ple_gate=m_w_ple_gate, w_ple=m_w_ple)
    v = dict(w_in=v_w_in, w_br_attn=v_w_br_attn, w_br_sg=v_w_br_sg, w_out=v_w_out, sg_w=v_sg_w, sg_b=v_sg_b, sg_ln_g=v_sg_ln_g,
             sg_ln_b=v_sg_ln_b, norm_mix=v_norm_mix, norm_ffn=v_norm_ffn, norm_ple=v_norm_ple, norm_final=v_norm_final,
             w_ff_gate=v_w_ff_gate, w_ff_up=v_w_ff_up, w_ff_down=v_w_ff_down, w_ple_gate=v_w_ple_gate, w_ple=v_w_ple)
    loss, grad_x, grad, delta, new_m, new_v = _step(x, p, loss_target, w, m, v)
    return (loss, grad_x, *[grad[n] for n in WEIGHTS], *[delta[n] for n in WEIGHTS], *[new_m[n] for n in WEIGHTS],
            *[new_v[n] for n in WEIGHTS])
```

```python
import functools
import math

import jax
import jax.numpy as jnp
from jax import lax
from jax.experimental import pallas as pl
from jax.experimental.pallas import tpu as pltpu
from jax.experimental.pallas import tpu_sc as plsc

F32 = jnp.float32
BF16 = jnp.bfloat16
MESH = pl.DeviceIdType.MESH

HEAD_DIM = 128
ATTN_GROUPS = ((128, 1), (512, 4), (2048, 16))
N_GROUPS = 3
HEADS = 4
QKV_W = 3 * N_GROUPS * HEADS * HEAD_DIM
ATTN_W = HEADS * HEAD_DIM
SG_CHUNK = 128
SG_GROUPS = 8
SG_W = 1024
RADIUS = 64
ROPE_THETA = 10000.0
NORM_EPS = 1e-6
NEG_INF = -1e30
ADAM_LR, ADAM_B1, ADAM_B2, ADAM_EPS, ADAM_WD, ADAM_STEP = 0.001, 0.9, 0.999, 1e-08, 0.01, 10

VMEM_CAP_V7X = 56 * 1024 * 1024
LANES = 128
EW_TILE_ELEMS = 256 * 1024
MM_VMEM_BUDGET = 44 * 1024 * 1024

GATHER_COLLECTIVE_ID = 1
PARTIALS_COLLECTIVE_ID = 2

BIG = ("w_in", "w_br_attn", "w_br_sg", "w_out", "w_ff_gate", "w_ff_up", "w_ff_down", "w_ple_gate", "w_ple")
ROW_SHARDED = ("w_out", "w_ff_down", "w_ple_gate")
SMALL_BLOCKS = "small_blocks"
GRAD_GROUPS = {"ffn": ("w_ple_gate", "w_ple", "w_ff_down", "w_ff_gate", "w_ff_up"), "mix": ("w_out", "w_br_attn", "w_br_sg", "w_in")}
SMALL = ("sg_w", "sg_b", "sg_ln_g", "sg_ln_b", "norm_mix", "norm_ffn", "norm_ple", "norm_final")
WEIGHTS = ("w_in", "w_br_attn", "w_br_sg", "w_out", "sg_w", "sg_b", "sg_ln_g", "sg_ln_b", "norm_mix", "norm_ffn",
           "norm_ple", "norm_final", "w_ff_gate", "w_ff_up", "w_ff_down", "w_ple_gate", "w_ple")


def _pick(n, prefs):
    for t in prefs:
        if n % t == 0:
            return t
    return n


def _vmem_limit(block_bytes, temp_bytes=0):
    est = 2 * block_bytes + temp_bytes
    assert est <= VMEM_CAP_V7X, est
    return VMEM_CAP_V7X


def _sigmoid(x):
    return 1.0 / (1.0 + jnp.exp(-x))


_GELU_C = math.sqrt(2.0 / math.pi)


def _gelu(x):
    return 0.5 * x * (1.0 + jnp.tanh(_GELU_C * (x + 0.044715 * (x * x * x))))


def _gelu_grad(x):
    t = jnp.tanh(_GELU_C * (x + 0.044715 * (x * x * x)))
    return 0.5 * (1.0 + t) + 0.5 * x * (1.0 - t * t) * (_GELU_C * (1.0 + 3.0 * 0.044715 * (x * x)))


def _lead(arr, l, blk, idx):
    if arr.ndim == 2:
        return pl.BlockSpec(blk, idx)
    return pl.BlockSpec((None,) + blk, lambda *g: (l,) + idx(*g))


def _k_steps(prods, tm, tn, fixed_bytes):
    for nk in range(1, 129):
        if any(p["K"] % nk or (p["K"] // nk) % LANES for p in prods):
            continue
        if 2 * sum((tm + tn) * (p["K"] // nk) * 2 for p in prods) + fixed_bytes <= MM_VMEM_BUDGET:
            return nk
    raise ValueError("no contraction split fits VMEM")


def _mm(name, prods, M, N, outs, epilogue, tiles=(), rows=(), tm=1024, tn=1024):
    assert M % tm == 0 and N % tn == 0, (name, M, N, tm, tn)
    fixed = 2 * tm * tn * (sum(t["x"].dtype.itemsize for t in tiles) + sum(jnp.dtype(o["dtype"]).itemsize for o in outs))
    fixed += (len(prods) + 2) * tm * tn * 4
    nk = _k_steps(prods, tm, tn, fixed)
    in_specs, args, block_bytes = [], [], 0
    for p in prods:
        if isinstance(p["b"], (list, tuple)):
            p["b"], p["bl"] = p["b"][p["bl"]], None
        K = p["K"]
        assert K % nk == 0, (name, K, nk)
        tk = K // nk
        p["tk"] = tk
        a_off, bk_off, bn_off = p.get("a_off", 0), p.get("bk_off", 0), p.get("bn_off", 0)
        assert bn_off % tn == 0 and bk_off % tk == 0
        if p["mode"] == "nn":
            assert a_off % tk == 0
            a_spec = _lead(p["a"], p.get("al"), (tm, tk), lambda i, j, k, o=a_off // tk: (i, o + k))
            b_spec = _lead(p["b"], p.get("bl"), (tk, tn), lambda i, j, k, ok=bk_off // tk, on=bn_off // tn: (ok + k, on + j))
        elif p["mode"] == "nt":
            assert a_off % tk == 0
            a_spec = _lead(p["a"], p.get("al"), (tm, tk), lambda i, j, k, o=a_off // tk: (i, o + k))
            b_spec = _lead(p["b"], p.get("bl"), (tn, tk), lambda i, j, k, ok=bk_off // tk, on=bn_off // tn: (on + j, ok + k))
        else:
            assert a_off % tm == 0
            a_spec = _lead(p["a"], p.get("al"), (tk, tm), lambda i, j, k, o=a_off // tm: (k, o + i))
            b_spec = _lead(p["b"], p.get("bl"), (tk, tn), lambda i, j, k, on=bn_off // tn: (k, on + j))
        in_specs += [a_spec, b_spec]
        args += [p["a"], p["b"]]
        block_bytes += (tm + tn) * tk * 2
    for t in tiles:
        off = t.get("off", 0)
        assert off % tn == 0
        in_specs.append(_lead(t["x"], t.get("l"), (tm, tn), lambda i, j, k, o=off // tn: (i, o + j)))
        args.append(t["x"])
        block_bytes += tm * tn * t["x"].dtype.itemsize
    for r in rows:
        in_specs.append(pl.BlockSpec((1, tn), lambda i, j, k: (0, j)))
        args.append(r)
    out_shapes, out_specs, aliases = [], [], {}
    for o_i, o in enumerate(outs):
        off = o.get("col_off", 0)
        assert off % tn == 0
        out_shapes.append(jax.ShapeDtypeStruct(o["shape"], o["dtype"]))
        idx = lambda i, j, k, oo=off // tn: (i, oo + j)
        if len(o["shape"]) == 2:
            out_specs.append(pl.BlockSpec((tm, tn), idx))
        else:
            out_specs.append(pl.BlockSpec((None, tm, tn), lambda i, j, k, l=o["l"], f=idx: (l,) + f(i, j, k)))
        if o.get("alias") is not None:
            aliases[len(args)] = o_i
            in_specs.append(pl.BlockSpec(memory_space=pl.ANY))
            args.append(o["alias"])
        block_bytes += tm * tn * jnp.dtype(o["dtype"]).itemsize
    n_p, n_t, n_r, n_o = len(prods), len(tiles), len(rows), len(outs)
    n_alias = len(aliases)
    modes = [p["mode"] for p in prods]

    def body(*refs):
        ab = refs[: 2 * n_p]
        t_refs = refs[2 * n_p: 2 * n_p + n_t]
        r_refs = refs[2 * n_p + n_t: 2 * n_p + n_t + n_r]
        o_refs = refs[2 * n_p + n_t + n_r + n_alias: 2 * n_p + n_t + n_r + n_alias + n_o]
        acc_refs = refs[2 * n_p + n_t + n_r + n_alias + n_o:]
        dims = {"nn": (((1,), (0,)), ((), ())), "nt": (((1,), (1,)), ((), ())), "tn": (((0,), (0,)), ((), ()))}

        def part(q):
            return lax.dot_general(ab[2 * q][...], ab[2 * q + 1][...], dims[modes[q]], preferred_element_type=F32)

        def finish(accs):
            res = epilogue(accs, [t[...] for t in t_refs], [r[...] for r in r_refs])
            for o_ref, val in zip(o_refs, res, strict=True):
                o_ref[...] = val.astype(o_ref.dtype)

        if nk == 1:
            finish([part(q) for q in range(n_p)])
        else:
            k = pl.program_id(2)

            @pl.when(k == 0)
            def _():
                for q, acc in enumerate(acc_refs):
                    acc[...] = part(q)

            @pl.when(k > 0)
            def _():
                for q, acc in enumerate(acc_refs):
                    acc[...] += part(q)

            @pl.when(k == nk - 1)
            def _():
                finish([acc[...] for acc in acc_refs])

    scratch = [pltpu.VMEM((tm, tn), F32) for _ in prods] if nk > 1 else []
    temp = (n_p + 2) * tm * tn * 4
    res = pl.pallas_call(
        body, name=name, grid=(M // tm, N // tn, nk), in_specs=in_specs, out_specs=out_specs, out_shape=out_shapes,
        scratch_shapes=scratch, input_output_aliases=aliases,
        compiler_params=pltpu.CompilerParams(dimension_semantics=("parallel", "parallel", "arbitrary"),
                                             vmem_limit_bytes=_vmem_limit(block_bytes, temp)),
    )(*args)
    return res


def _first(accs, tiles, rows):
    return [accs[0]]


def _ew(name, fn, ins, outs, R, C, tr=None, tc=None):
    tc = tc or _pick(C, (2048, 1536, 1408, 1024, 896, 512, 384, 256, 128))
    tr = tr or _pick(R, [t for t in (512, 256, 128, 64, 32, 16) if t * tc <= 2 * EW_TILE_ELEMS] + [8])
    in_specs, args, bb = [], [], 0
    for arr in ins:
        in_specs.append(pl.BlockSpec((tr, tc), lambda i, j: (i, j)))
        args.append(arr)
        bb += tr * tc * arr.dtype.itemsize
    out_shapes = [jax.ShapeDtypeStruct((R, C), d) for d in outs]
    out_specs = [pl.BlockSpec((tr, tc), lambda i, j: (i, j)) for _ in outs]
    bb += sum(tr * tc * jnp.dtype(d).itemsize for d in outs)
    n_in = len(ins)

    def body(*refs):
        res = fn(*[r[...] for r in refs[:n_in]])
        for o_ref, val in zip(refs[n_in:], res, strict=True):
            o_ref[...] = val.astype(o_ref.dtype)

    return pl.pallas_call(
        body, name=name, grid=(R // tr, C // tc), in_specs=in_specs, out_specs=out_specs, out_shape=out_shapes,
        compiler_params=pltpu.CompilerParams(dimension_semantics=("parallel", "parallel"),
                                             vmem_limit_bytes=_vmem_limit(bb, 6 * tr * tc * 4)),
    )(*args)


def _rmsnorm_fwd(name, x, g):
    S, D = x.shape
    tr = _pick(S, (256, 128, 64, 8))

    def body(x_ref, g_ref, h_ref):
        xv = x_ref[...]
        r = lax.rsqrt(jnp.mean(xv * xv, axis=-1, keepdims=True) + NORM_EPS)
        h_ref[...] = (xv * r * g_ref[...]).astype(BF16)

    return pl.pallas_call(
        body, name=name, grid=(S // tr,),
        in_specs=[pl.BlockSpec((tr, D), lambda i: (i, 0)), pl.BlockSpec((1, D), lambda i: (0, 0))],
        out_specs=pl.BlockSpec((tr, D), lambda i: (i, 0)), out_shape=jax.ShapeDtypeStruct((S, D), BF16),
        compiler_params=pltpu.CompilerParams(dimension_semantics=("parallel",),
                                             vmem_limit_bytes=_vmem_limit(tr * D * 6, 3 * tr * D * 4)),
    )(x, g)


def _rmsnorm_bwd(name, x, g, dh, dres):
    S, D = x.shape
    tr = _pick(S, (256, 128, 64, 8))

    def body(x_ref, g_ref, dh_ref, dres_ref, dx_ref, dxb_ref, dg_ref):
        xv = x_ref[...]
        dy = dh_ref[...].astype(F32)
        r = lax.rsqrt(jnp.mean(xv * xv, axis=-1, keepdims=True) + NORM_EPS)
        a = dy * g_ref[...]
        dx = dres_ref[...] + r * a - xv * (r * r * r) * jnp.mean(a * xv, axis=-1, keepdims=True)
        dx_ref[...] = dx
        dxb_ref[...] = dx.astype(BF16)
        part = jnp.sum(dy * xv * r, axis=0, keepdims=True)

        @pl.when(pl.program_id(0) == 0)
        def _():
            dg_ref[...] = part

        @pl.when(pl.program_id(0) > 0)
        def _():
            dg_ref[...] += part

    row = pl.BlockSpec((tr, D), lambda i: (i, 0))
    vec = pl.BlockSpec((1, D), lambda i: (0, 0))
    return pl.pallas_call(
        body, name=name, grid=(S // tr,), in_specs=[row, vec, row, row], out_specs=[row, row, vec],
        out_shape=[jax.ShapeDtypeStruct((S, D), F32), jax.ShapeDtypeStruct((S, D), BF16), jax.ShapeDtypeStruct((1, D), F32)],
        compiler_params=pltpu.CompilerParams(dimension_semantics=("arbitrary",),
                                             vmem_limit_bytes=_vmem_limit(tr * D * 18, 5 * tr * D * 4)),
    )(x, g, dh, dres)


def _loss_head(x, g, target):
    S, D = x.shape
    tr = _pick(S, (256, 128, 64, 8))

    def body(x_ref, g_ref, t_ref, loss_ref, dx_ref, dxb_ref, dg_ref):
        xv = x_ref[...]
        r = lax.rsqrt(jnp.mean(xv * xv, axis=-1, keepdims=True) + NORM_EPS)
        xn = xv * r
        diff = xn * g_ref[...] - t_ref[...]
        dy = diff * (1.0 / D)
        a = dy * g_ref[...]
        dx = r * a - xv * (r * r * r) * jnp.mean(a * xv, axis=-1, keepdims=True)
        dx_ref[...] = dx
        dxb_ref[...] = dx.astype(BF16)
        part = jnp.sum(dy * xn, axis=0, keepdims=True)
        cell = (lax.broadcasted_iota(jnp.int32, (8, LANES), 0) == 0) & (lax.broadcasted_iota(jnp.int32, (8, LANES), 1) == 0)
        lpart = jnp.where(cell, 0.5 * jnp.sum(jnp.mean(diff * diff, axis=-1, keepdims=True)), 0.0)

        @pl.when(pl.program_id(0) == 0)
        def _():
            dg_ref[...] = part
            loss_ref[...] = lpart

        @pl.when(pl.program_id(0) > 0)
        def _():
            dg_ref[...] += part
            loss_ref[...] += lpart

    row = pl.BlockSpec((tr, D), lambda i: (i, 0))
    vec = pl.BlockSpec((1, D), lambda i: (0, 0))
    return pl.pallas_call(
        body, name="loss_head", grid=(S // tr,), in_specs=[row, vec, row],
        out_specs=[pl.BlockSpec((8, LANES), lambda i: (0, 0)), row, row, vec],
        out_shape=[jax.ShapeDtypeStruct((8, LANES), F32), jax.ShapeDtypeStruct((S, D), F32),
                   jax.ShapeDtypeStruct((S, D), BF16), jax.ShapeDtypeStruct((1, D), F32)],
        compiler_params=pltpu.CompilerParams(dimension_semantics=("arbitrary",),
                                             vmem_limit_bytes=_vmem_limit(tr * D * 14, 6 * tr * D * 4)),
    )(x, g, target)


def _rope_tables(S):
    pos = jnp.arange(S, dtype=F32)
    inv_freq = ROPE_THETA ** (-jnp.arange(0, HEAD_DIM, 2, dtype=F32) / HEAD_DIM)
    ang = pos[:, None] * inv_freq[None, :]
    cos, sin = jnp.cos(ang), jnp.sin(ang)
    return jnp.concatenate([cos, cos], axis=-1), jnp.concatenate([-sin, sin], axis=-1)


def _rope_fwd(name, z, cosf, sinf):
    S = z.shape[0]
    tr = _pick(S, (256, 128, 64, 8))
    n_q = N_GROUPS * HEADS

    def body(z_ref, c_ref, s_ref, o_ref):
        c, s = c_ref[...], s_ref[...]
        for j in range(QKV_W // HEAD_DIM):
            t = z_ref[:, j * HEAD_DIM:(j + 1) * HEAD_DIM]
            if j < 2 * n_q:
                t = t * c + pltpu.roll(t, HEAD_DIM // 2, axis=1) * s
            if j < n_q:
                t = t * ATTN_SCALE
            o_ref[:, j * HEAD_DIM:(j + 1) * HEAD_DIM] = t.astype(BF16)

    tab = pl.BlockSpec((tr, HEAD_DIM), lambda i: (i, 0))
    return pl.pallas_call(
        body, name=name, grid=(S // tr,), in_specs=[pl.BlockSpec((tr, QKV_W), lambda i: (i, 0)), tab, tab],
        out_specs=pl.BlockSpec((tr, QKV_W), lambda i: (i, 0)), out_shape=jax.ShapeDtypeStruct((S, QKV_W), BF16),
        compiler_params=pltpu.CompilerParams(dimension_semantics=("parallel",),
                                             vmem_limit_bytes=_vmem_limit(tr * QKV_W * 6, tr * QKV_W * 4)),
    )(z, cosf, sinf)


def _rope_bwd(name, dq, dk, dv, dzuv, cosf, sinf, dz):
    S = dq.shape[0]
    tr = _pick(S, (256, 128, 64, 8))
    W3 = QKV_W // 3
    nh = W3 // HEAD_DIM
    wide = QKV_W + dzuv.shape[1]

    def body(dq_ref, dk_ref, dv_ref, uv_ref, c_ref, s_ref, dz_in, o_ref):
        c, s = c_ref[...], s_ref[...]
        for part, ref in enumerate((dq_ref, dk_ref)):
            for j in range(nh):
                t = ref[:, j * HEAD_DIM:(j + 1) * HEAD_DIM].astype(F32)
                t = t * c - pltpu.roll(t, HEAD_DIM // 2, axis=1) * s
                o_ref[:, part * W3 + j * HEAD_DIM: part * W3 + (j + 1) * HEAD_DIM] = t.astype(BF16)
        o_ref[:, 2 * W3:QKV_W] = dv_ref[...]
        o_ref[:, QKV_W:] = uv_ref[...]

    third = pl.BlockSpec((tr, W3), lambda i: (i, 0))
    tab = pl.BlockSpec((tr, HEAD_DIM), lambda i: (i, 0))
    return pl.pallas_call(
        body, name=name, grid=(S // tr,),
        in_specs=[third, third, third, pl.BlockSpec((tr, dzuv.shape[1]), lambda i: (i, 0)), tab, tab, pl.BlockSpec(memory_space=pl.ANY)],
        out_specs=pl.BlockSpec((tr, wide), lambda i: (i, 0)), out_shape=jax.ShapeDtypeStruct(dz.shape, dz.dtype),
        input_output_aliases={6: 0},
        compiler_params=pltpu.CompilerParams(dimension_semantics=("parallel",),
                                             vmem_limit_bytes=_vmem_limit(tr * wide * 4, tr * wide * 4)),
    )(dq, dk, dv, dzuv, cosf, sinf, dz)


ATTN_TQ = 256
ATTN_KEY_CHUNK = 768
ATTN_SCALE = HEAD_DIM ** -0.5
ATTN_PAD_MAX = RADIUS * max(d for _, d in ATTN_GROUPS)


def _band_bias(shape, q_axis, d):
    kq = lax.broadcasted_iota(jnp.int32, shape, 1 - q_axis) - lax.broadcasted_iota(jnp.int32, shape, q_axis) - RADIUS * d
    return jnp.where((jnp.abs(kq) <= RADIUS * d) & ((kq & (d - 1)) == 0), 0.0, NEG_INF).astype(F32)


def _fill_padded(dst, src, d, S):
    pad = RADIUS * d
    dst[0:pad, :] = jnp.zeros((pad, HEAD_DIM), dst.dtype)
    dst[pad:pad + S, :] = src[...]
    dst[pad + S:pad + S + pad, :] = jnp.zeros((pad, HEAD_DIM), dst.dtype)


_NT = (((1,), (1,)), ((), ()))


def _attn_fwd(name, qkv):
    S = qkv.shape[0]
    T = ATTN_TQ
    nq = N_GROUPS * HEADS
    widths = [T + 2 * RADIUS * d for _, d in ATTN_GROUPS]

    def body(*refs):
        q_refs, k_refs, v_refs = refs[0:3], refs[3:6], refs[6:9]
        o_ref, lc_ref = refs[9:11]
        kp, vp, bias = refs[11:14], refs[14:17], refs[17:20]
        i0 = pl.multiple_of(pl.program_id(1) * T, T)

        @pl.when(pl.program_id(1) == 0)
        def _():
            for g, (_, d) in enumerate(ATTN_GROUPS):
                _fill_padded(kp[g], k_refs[g], d, S)
                _fill_padded(vp[g], v_refs[g], d, S)
                bias[g][...] = _band_bias((T, widths[g]), 0, d)

        m = jnp.full((T, 1), NEG_INF, F32)
        l = jnp.zeros((T, 1), F32)
        acc = jnp.zeros((T, HEAD_DIM), F32)
        for g, (_, d) in enumerate(ATTN_GROUPS):
            for c0 in range(0, widths[g], ATTN_KEY_CHUNK):
                W = min(ATTN_KEY_CHUNK, widths[g] - c0)
                kw = kp[g][pl.ds(i0 + c0, W), :]
                vw = vp[g][pl.ds(i0 + c0, W), :]
                key = i0 + c0 - RADIUS * d + lax.broadcasted_iota(jnp.int32, (1, W), 1)
                in_seq = jnp.where((key >= 0) & (key < S), 0.0, NEG_INF).astype(F32)
                s = lax.dot_general(q_refs[g][...], kw, _NT, preferred_element_type=F32) + bias[g][:, c0:c0 + W] + in_seq
                m_new = jnp.maximum(m, jnp.max(s, axis=1, keepdims=True))
                alpha = jnp.exp(m - m_new)
                p = jnp.exp(s - m_new)
                l = l * alpha + jnp.sum(p, axis=1, keepdims=True)
                acc = acc * alpha + jnp.dot(p.astype(BF16), vw, preferred_element_type=F32)
                m = m_new
        o_ref[...] = (acc / l).astype(BF16)
        lc_ref[...] = m + jnp.log(l)

    in_specs = [pl.BlockSpec((T, HEAD_DIM), lambda h, i, g=g: (i, g * HEADS + h)) for g in range(N_GROUPS)]
    in_specs += [pl.BlockSpec((S, HEAD_DIM), lambda h, i, g=g: (0, nq + g * HEADS + h)) for g in range(N_GROUPS)]
    in_specs += [pl.BlockSpec((S, HEAD_DIM), lambda h, i, g=g: (0, 2 * nq + g * HEADS + h)) for g in range(N_GROUPS)]
    padded = [pltpu.VMEM((S + 2 * RADIUS * d, HEAD_DIM), BF16) for _, d in ATTN_GROUPS]
    scratch = padded + padded + [pltpu.VMEM((T, W), F32) for W in widths]
    scratch_bytes = sum(2 * (S + 2 * RADIUS * d) * HEAD_DIM * 2 for _, d in ATTN_GROUPS) + sum(T * W * 4 for W in widths)
    return pl.pallas_call(
        body, name=name, grid=(HEADS, S // T), in_specs=in_specs,
        out_specs=[pl.BlockSpec((T, HEAD_DIM), lambda h, i: (i, h)), pl.BlockSpec((None, T, 1), lambda h, i: (h, i, 0))],
        out_shape=[jax.ShapeDtypeStruct((S, ATTN_W), BF16), jax.ShapeDtypeStruct((HEADS, S, 1), F32)],
        scratch_shapes=scratch,
        compiler_params=pltpu.CompilerParams(dimension_semantics=("parallel", "arbitrary"),
                                             vmem_limit_bytes=_vmem_limit(6 * S * HEAD_DIM * 2 + 8 * T * HEAD_DIM * 4,
                                                                          scratch_bytes + 4 * T * widths[-1] * 4)),
    )(*([qkv] * 9))


_TN = (((0,), (0,)), ((), ()))


def _attn_bwd(name, qkv, attn, dattn, lse_c):
    S = qkv.shape[0]
    T = ATTN_TQ
    nq = N_GROUPS * HEADS
    W3 = QKV_W // 3
    n_i = S // T
    wmax = T + 2 * ATTN_PAD_MAX
    s_pad = S + 2 * ATTN_PAD_MAX

    def body(q_ref, k_ref, v_ref, o_ref, do_ref, lc_ref, dq_ref, dk_ref, dv_ref, kp, vp, dk_acc, dv_acc, bias):
        g_id, i = pl.program_id(1), pl.program_id(2)
        i0 = pl.multiple_of(i * T, T)
        q, do = q_ref[...], do_ref[...]
        delta = jnp.sum(do.astype(F32) * o_ref[...].astype(F32), axis=1, keepdims=True)
        lse = lc_ref[...]

        def group(d):
            W, pad = T + 2 * RADIUS * d, RADIUS * d

            @pl.when(i == 0)
            def _():
                _fill_padded(kp, k_ref, d, S)
                _fill_padded(vp, v_ref, d, S)
                dk_acc[...] = jnp.zeros_like(dk_acc)
                dv_acc[...] = jnp.zeros_like(dv_acc)
                bias[:, 0:W] = _band_bias((T, W), 0, d)

            kw = kp[pl.ds(i0, W), :]
            vw = vp[pl.ds(i0, W), :]
            key = i0 - pad + lax.broadcasted_iota(jnp.int32, (1, W), 1)
            in_seq = jnp.where((key >= 0) & (key < S), 0.0, NEG_INF).astype(F32)
            s = lax.dot_general(q, kw, _NT, preferred_element_type=F32) + bias[:, 0:W] + in_seq
            p = jnp.exp(s - lse)
            dp = lax.dot_general(do, vw, _NT, preferred_element_type=F32)
            ds = (p * (dp - delta)).astype(BF16)
            dq_ref[...] = (jnp.dot(ds, kw, preferred_element_type=F32) * ATTN_SCALE).astype(BF16)
            dk_acc[pl.ds(i0, W), :] += lax.dot_general(ds, q, _TN, preferred_element_type=F32)
            dv_acc[pl.ds(i0, W), :] += lax.dot_general(p.astype(BF16), do, _TN, preferred_element_type=F32)

            @pl.when(i == n_i - 1)
            def _():
                dk_ref[...] = dk_acc[pad:pad + S, :].astype(BF16)
                dv_ref[...] = dv_acc[pad:pad + S, :].astype(BF16)

        for g, (_, d) in enumerate(ATTN_GROUPS):
            pl.when(g_id == g)(functools.partial(group, d))

    tile = lambda off: pl.BlockSpec((T, HEAD_DIM), lambda h, g, i: (i, off + g * HEADS + h))
    full = lambda off: pl.BlockSpec((S, HEAD_DIM), lambda h, g, i: (0, off + g * HEADS + h))
    headt = pl.BlockSpec((T, HEAD_DIM), lambda h, g, i: (i, h))
    scratch_bytes = 2 * s_pad * HEAD_DIM * (2 + 4) + T * wmax * 4
    return pl.pallas_call(
        body, name=name, grid=(HEADS, N_GROUPS, n_i),
        in_specs=[tile(0), full(nq), full(2 * nq), headt, headt, pl.BlockSpec((None, T, 1), lambda h, g, i: (h, i, 0))],
        out_specs=[tile(0), full(0), full(0)],
        out_shape=[jax.ShapeDtypeStruct((S, W3), BF16)] * 3,
        scratch_shapes=[pltpu.VMEM((s_pad, HEAD_DIM), BF16), pltpu.VMEM((s_pad, HEAD_DIM), BF16),
                        pltpu.VMEM((s_pad, HEAD_DIM), F32), pltpu.VMEM((s_pad, HEAD_DIM), F32), pltpu.VMEM((T, wmax), F32)],
        compiler_params=pltpu.CompilerParams(dimension_semantics=("parallel", "arbitrary", "arbitrary"),
                                             vmem_limit_bytes=_vmem_limit(4 * S * HEAD_DIM * 2 + 8 * T * HEAD_DIM * 4,
                                                                          scratch_bytes + 5 * T * wmax * 4)),
    )(qkv, qkv, qkv, attn, dattn, lse_c)


def _sg_parts(u, v, lng, lnb):
    gu = _gelu(u)
    gv = _gelu(v)
    mu = jnp.mean(gv, axis=-1, keepdims=True)
    xc = gv - mu
    rstd = lax.rsqrt(jnp.mean(xc * xc, axis=-1, keepdims=True) + NORM_EPS)
    xhat = xc * rstd
    vn = xhat * lng + lnb
    return gu, xhat, rstd, vn


def _sg_fwd(name, z, sg_w, sg_bc, lng, lnb, o_sg0):
    S = z.shape[0]
    T = SG_CHUNK
    cb = 512
    assert o_sg0 % cb == 0
    b0 = o_sg0 // cb

    def body(u0, u1, v0, v1, w_ref, b_ref, g_ref, be_ref, o_ref):
        u = jnp.concatenate([u0[...], u1[...]], axis=1)
        v = jnp.concatenate([v0[...], v1[...]], axis=1)
        gu, _, _, vn = _sg_parts(u, v, g_ref[...], be_ref[...])
        vnb = vn.astype(BF16)
        for g in range(SG_GROUPS):
            sl = slice(g * SG_CHUNK, (g + 1) * SG_CHUNK)
            mixed = jnp.dot(w_ref[g], vnb[:, sl], preferred_element_type=F32) + b_ref[g]
            o_ref[:, sl] = (gu[:, sl] * mixed).astype(BF16)

    zs = lambda k: pl.BlockSpec((T, cb), lambda i, k=k: (i, b0 + k))
    const3 = lambda shp: pl.BlockSpec(shp, lambda i: (0, 0, 0))
    vec = pl.BlockSpec((1, SG_W), lambda i: (0, 0))
    return pl.pallas_call(
        body, name=name, grid=(S // T,),
        in_specs=[zs(0), zs(1), zs(2), zs(3), const3((SG_GROUPS, SG_CHUNK, SG_CHUNK)), const3((SG_GROUPS, SG_CHUNK, 1)), vec, vec],
        out_specs=pl.BlockSpec((T, SG_W), lambda i: (i, 0)), out_shape=jax.ShapeDtypeStruct((S, SG_W), BF16),
        compiler_params=pltpu.CompilerParams(dimension_semantics=("parallel",), vmem_limit_bytes=_vmem_limit(4 * 1024 * 1024, 8 * T * SG_W * 4)),
    )(z, z, z, z, sg_w, sg_bc, lng, lnb)


def _sg_bwd(name, z, dsg, sg_w, sg_wt, sg_bc, lng, lnb, o_sg0):
    S = z.shape[0]
    T = SG_CHUNK
    cb = 512
    b0 = o_sg0 // cb

    def body(u0, u1, v0, v1, d_ref, w_ref, wt_ref, b_ref, g_ref, be_ref, dz_ref, dw_ref, db_ref, dg_ref, dbe_ref):
        i = pl.program_id(0)
        u = jnp.concatenate([u0[...], u1[...]], axis=1)
        v = jnp.concatenate([v0[...], v1[...]], axis=1)
        gu, xhat, rstd, vn = _sg_parts(u, v, g_ref[...], be_ref[...])
        vnb = vn.astype(BF16)
        dsg_v = d_ref[...].astype(F32)
        dmix = dsg_v * gu
        dmixb = dmix.astype(BF16)
        dvn_parts, mixed_parts, dw_parts, db_parts = [], [], [], []
        for g in range(SG_GROUPS):
            sl = slice(g * SG_CHUNK, (g + 1) * SG_CHUNK)
            mixed_parts.append(jnp.dot(w_ref[g], vnb[:, sl], preferred_element_type=F32) + b_ref[g])
            dvn_parts.append(jnp.dot(wt_ref[g], dmixb[:, sl], preferred_element_type=F32))
            dw_parts.append(lax.dot_general(dmixb[:, sl], vnb[:, sl], _NT, preferred_element_type=F32))
            db_parts.append(jnp.sum(dmix[:, sl], axis=1, keepdims=True))
        mixed = jnp.concatenate(mixed_parts, axis=1)
        dvn = jnp.concatenate(dvn_parts, axis=1)
        dzu = dsg_v * mixed * _gelu_grad(u)
        dxh = dvn * g_ref[...]
        dgv = rstd * (dxh - jnp.mean(dxh, axis=-1, keepdims=True) - xhat * jnp.mean(dxh * xhat, axis=-1, keepdims=True))
        dzv = dgv * _gelu_grad(v)
        dz_ref[:, :SG_W] = dzu.astype(BF16)
        dz_ref[:, SG_W:] = dzv.astype(BF16)
        dgp = jnp.sum(dvn * xhat, axis=0, keepdims=True)
        dbp = jnp.sum(dvn, axis=0, keepdims=True)

        @pl.when(i == 0)
        def _():
            for g in range(SG_GROUPS):
                dw_ref[g] = dw_parts[g]
                db_ref[g] = db_parts[g]
            dg_ref[...] = dgp
            dbe_ref[...] = dbp

        @pl.when(i > 0)
        def _():
            for g in range(SG_GROUPS):
                dw_ref[g] += dw_parts[g]
                db_ref[g] += db_parts[g]
            dg_ref[...] += dgp
            dbe_ref[...] += dbp

    zs = lambda k: pl.BlockSpec((T, cb), lambda i, k=k: (i, b0 + k))
    const3 = lambda shp: pl.BlockSpec(shp, lambda i: (0, 0, 0))
    vec = pl.BlockSpec((1, SG_W), lambda i: (0, 0))
    return pl.pallas_call(
        body, name=name, grid=(S // T,),
        in_specs=[zs(0), zs(1), zs(2), zs(3), pl.BlockSpec((T, SG_W), lambda i: (i, 0)),
                  const3((SG_GROUPS, SG_CHUNK, SG_CHUNK)), const3((SG_GROUPS, SG_CHUNK, SG_CHUNK)), const3((SG_GROUPS, SG_CHUNK, 1)),
                  vec, vec],
        out_specs=[pl.BlockSpec((T, 2 * SG_W), lambda i: (i, 0)), const3((SG_GROUPS, SG_CHUNK, SG_CHUNK)),
                   const3((SG_GROUPS, SG_CHUNK, 1)), vec, vec],
        out_shape=[jax.ShapeDtypeStruct((S, 2 * SG_W), BF16), jax.ShapeDtypeStruct((SG_GROUPS, SG_CHUNK, SG_CHUNK), F32),
                   jax.ShapeDtypeStruct((SG_GROUPS, SG_CHUNK, 1), F32), jax.ShapeDtypeStruct((1, SG_W), F32),
                   jax.ShapeDtypeStruct((1, SG_W), F32)],
        compiler_params=pltpu.CompilerParams(dimension_semantics=("arbitrary",),
                                             vmem_limit_bytes=_vmem_limit(6 * 1024 * 1024, 16 * T * SG_W * 4)),
    )(z, z, z, z, dsg, sg_w, sg_wt, sg_bc, lng, lnb)


def _gate_bwd(name, z, dmerged, y_attn, y_sg, o_g0, in_w):
    S, D = dmerged.shape
    tr = _pick(S, (512, 256, 128, 8))
    cb = _pick(D, (512, 256, 128))
    assert o_g0 % cb == 0
    nd = D // cb
    b0 = o_g0 // cb

    def body(z_ref, dm_ref, ya_ref, ys_ref, dz_ref, dy_ref):
        jj = pl.program_id(1)
        gate = _sigmoid(z_ref[...])
        dm = dm_ref[...].astype(F32)
        y = jnp.where(jj < nd, ya_ref[...], ys_ref[...]).astype(F32)
        dz_ref[...] = (dm * y * gate * (1.0 - gate)).astype(BF16)
        dy_ref[...] = (dm * gate).astype(BF16)

    half = pl.BlockSpec((tr, cb), lambda i, jj: (i, jj % nd))
    return pl.pallas_call(
        body, name=name, grid=(S // tr, 2 * nd),
        in_specs=[pl.BlockSpec((tr, cb), lambda i, jj: (i, b0 + jj)), half, half, half],
        out_specs=[pl.BlockSpec((tr, cb), lambda i, jj: (i, b0 + jj)), pl.BlockSpec((tr, cb), lambda i, jj: (i, jj))],
        out_shape=[jax.ShapeDtypeStruct((S, in_w), BF16), jax.ShapeDtypeStruct((S, 2 * D), BF16)],
        compiler_params=pltpu.CompilerParams(dimension_semantics=("parallel", "arbitrary"),
                                             vmem_limit_bytes=_vmem_limit(tr * cb * 14, 6 * tr * cb * 4)),
    )(z, dmerged, y_attn, y_sg)


def _row(v):
    return v.reshape(1, -1)


def _local_step(x, p, target, wf, small, after_group):
    S, D = x.shape
    L = p.shape[0]
    in_w = wf["w_in"][0].shape[1]
    ff = wf["w_ff_gate"][0].shape[1]
    ple = p.shape[2]
    o_sg0, o_g0 = QKV_W, QKV_W + 2 * SG_W
    cosf, sinf = _rope_tables(S)
    pb = p.astype(BF16)
    tmb = _pick(S, (1024, 512, 256))
    tn_in = _pick(in_w, (768, 1024, 512))
    tn_d = _pick(D, (1024, 512, 256))
    tn_g = _pick(D, (512, 256))
    tn_ff = _pick(ff, (512, 256))

    saved = []
    xs = x
    for i in range(L):
        sv = {"x0": xs}
        h = _rmsnorm_fwd(f"norm_mix_{i}", xs, _row(small["norm_mix"][i]))
        (z,) = _mm(f"in_proj_{i}", [dict(a=h, b=wf["w_in"], bl=i, mode="nn", K=D)], S, in_w,
                   [dict(shape=(S, in_w), dtype=F32)], _first, tm=tmb, tn=tn_in)
        qkv = _rope_fwd(f"rope_{i}", z, cosf, sinf)
        attn, lse_c = _attn_fwd(f"attn_{i}", qkv)
        sgw = small["sg_w"][i].astype(BF16)
        sgbc = small["sg_b"][i].reshape(SG_GROUPS, SG_CHUNK, 1)
        sg = _sg_fwd(f"sgu_{i}", z, sgw, sgbc, _row(small["sg_ln_g"][i]), _row(small["sg_ln_b"][i]), o_sg0)

        def merge(accs, tiles, rows):
            ya, ys = accs[0].astype(BF16), accs[1].astype(BF16)
            g0, g1 = _sigmoid(tiles[0]), _sigmoid(tiles[1])
            return [ya, ys, g0 * ya.astype(F32) + g1 * ys.astype(F32)]

        y_attn, y_sg, merged = _mm(
            f"branches_{i}",
            [dict(a=attn, b=wf["w_br_attn"], bl=i, mode="nn", K=ATTN_W), dict(a=sg, b=wf["w_br_sg"], bl=i, mode="nn", K=SG_W)],
            S, D, [dict(shape=(S, D), dtype=BF16)] * 3, merge,
            tiles=[dict(x=z, off=o_g0), dict(x=z, off=o_g0 + D)], tm=tmb, tn=tn_g)
        (x1,) = _mm(f"out_proj_{i}", [dict(a=merged, b=wf["w_out"], bl=i, mode="nn", K=D)], S, D,
                    [dict(shape=(S, D), dtype=F32)], lambda a, t, r: [t[0] + a[0]], tiles=[dict(x=xs)], tm=tmb, tn=tn_d)
        h2 = _rmsnorm_fwd(f"norm_ffn_{i}", x1, _row(small["norm_ffn"][i]))

        def swiglu(accs, tiles, rows):
            fg = accs[0].astype(BF16).astype(F32)
            fu = accs[1].astype(BF16).astype(F32)
            return [fg, fu, fg * _sigmoid(fg) * fu]

        ffg, ffu, act = _mm(
            f"ff_in_{i}",
            [dict(a=h2, b=wf["w_ff_gate"], bl=i, mode="nn", K=D), dict(a=h2, b=wf["w_ff_up"], bl=i, mode="nn", K=D)],
            S, ff, [dict(shape=(S, ff), dtype=BF16)] * 3, swiglu, tm=tmb, tn=tn_ff)
        (x2,) = _mm(f"ff_out_{i}", [dict(a=act, b=wf["w_ff_down"], bl=i, mode="nn", K=ff)], S, D,
                    [dict(shape=(S, D), dtype=F32)], lambda a, t, r: [t[0] + a[0]], tiles=[dict(x=x1)], tm=tmb, tn=tn_d)
        h3 = _rmsnorm_fwd(f"norm_ple_{i}", x2, _row(small["norm_ple"][i]))

        def ple_mix(accs, tiles, rows):
            gp = _sigmoid(accs[0]).astype(BF16)
            pe = accs[1].astype(BF16)
            return [tiles[0] + gp.astype(F32) * pe.astype(F32), gp, pe]

        x3, gp, pe = _mm(
            f"ple_{i}",
            [dict(a=h3, b=wf["w_ple_gate"], bl=i, mode="nn", K=D), dict(a=pb, al=i, b=wf["w_ple"], bl=i, mode="nn", K=ple)],
            S, D, [dict(shape=(S, D), dtype=F32), dict(shape=(S, D), dtype=BF16), dict(shape=(S, D), dtype=BF16)], ple_mix,
            tiles=[dict(x=x2)], tm=tmb, tn=tn_g)
        sv.update(h=h, z=z, qkv=qkv, attn=attn, lse_c=lse_c, sg=sg, y_attn=y_attn, y_sg=y_sg, merged=merged,
                  x1=x1, h2=h2, ffg=ffg, ffu=ffu, act=act, x2=x2, h3=h3, gp=gp, pe=pe, sgw=sgw, sgbc=sgbc)
        saved.append(sv)
        xs = x3

    loss_cell, dx, dxb, dg_final = _loss_head(xs, _row(small["norm_final"]), target)

    gw = {n: [None] * L for n in BIG}
    gs = {n: [None] * L for n in SMALL if n != "norm_final"}

    def dw(n, i, a, a_off, b, bn_off, K_rows, N_cols, tm, tn):
        (gw[n][i],) = _mm(f"d_{n}_{i}", [dict(a=a, b=b, mode="tn", K=S, a_off=a_off, bn_off=bn_off)], K_rows, N_cols,
                          [dict(shape=(K_rows, N_cols), dtype=BF16)], _first, tm=tm, tn=tn)

    for i in reversed(range(L)):
        sv = saved[i]
        dpre, dpe = _ew(f"ple_gate_bwd_{i}",
                        lambda d, g, e: [d * e.astype(F32) * g.astype(F32) * (1.0 - g.astype(F32)), d * g.astype(F32)],
                        [dx, sv["gp"], sv["pe"]], [BF16, BF16], S, D)
        (dh3,) = _mm(f"d_h3_{i}", [dict(a=dpre, b=wf["w_ple_gate"], bl=i, mode="nt", K=D)], S, D,
                     [dict(shape=(S, D), dtype=BF16)], _first, tm=tmb, tn=tn_d)
        dw("w_ple_gate", i, sv["h3"], 0, dpre, 0, D, D, tn_d, tn_d)
        dw("w_ple", i, pb[i], 0, dpe, 0, ple, D, _pick(ple, (256, 128)), _pick(D, (2048, 1024, 512, 256)))
        dx, dxb, gs["norm_ple"][i] = _rmsnorm_bwd(f"norm_ple_bwd_{i}", sv["x2"], _row(small["norm_ple"][i]), dh3, dx)
        def swiglu_bwd(accs, tiles, rows):
            da = accs[0].astype(BF16).astype(F32)
            fg, fu = tiles[0].astype(F32), tiles[1].astype(F32)
            sg_ = _sigmoid(fg)
            return [da * fu * (sg_ * (1.0 + fg * (1.0 - sg_))), da * (fg * sg_)]

        dffg, dffu = _mm(f"d_act_{i}", [dict(a=dxb, b=wf["w_ff_down"], bl=i, mode="nt", K=D)], S, ff,
                         [dict(shape=(S, ff), dtype=BF16)] * 2, swiglu_bwd, tiles=[dict(x=sv["ffg"]), dict(x=sv["ffu"])],
                         tm=tmb, tn=tn_ff)
        dw("w_ff_down", i, sv["act"], 0, dxb, 0, ff, D, tn_ff, _pick(D, (2048, 1024, 512, 256)))
        dw("w_ff_gate", i, sv["h2"], 0, dffg, 0, D, ff, _pick(D, (2048, 1024, 512, 256)), tn_ff)
        dw("w_ff_up", i, sv["h2"], 0, dffu, 0, D, ff, _pick(D, (2048, 1024, 512, 256)), tn_ff)
        (dffg, dffu), _ = lax.optimization_barrier(((dffg, dffu), after_group(i, "ffn", {n: gw[n][i] for n in GRAD_GROUPS["ffn"]})))
        (dh2,) = _mm(f"d_h2_{i}", [dict(a=dffg, b=wf["w_ff_gate"], bl=i, mode="nt", K=ff),
                                   dict(a=dffu, b=wf["w_ff_up"], bl=i, mode="nt", K=ff)], S, D,
                     [dict(shape=(S, D), dtype=BF16)], lambda a, t, r: [a[0] + a[1]], tm=tmb, tn=tn_d)
        dx, dxb, gs["norm_ffn"][i] = _rmsnorm_bwd(f"norm_ffn_bwd_{i}", sv["x1"], _row(small["norm_ffn"][i]), dh2, dx)
        (dmerged,) = _mm(f"d_merged_{i}", [dict(a=dxb, b=wf["w_out"], bl=i, mode="nt", K=D)], S, D,
                         [dict(shape=(S, D), dtype=BF16)], _first, tm=tmb, tn=tn_d)
        dw("w_out", i, sv["merged"], 0, dxb, 0, D, D, tn_d, tn_d)
        dz, dy = _gate_bwd(f"gate_bwd_{i}", sv["z"], dmerged, sv["y_attn"], sv["y_sg"], o_g0, in_w)
        (dattn,) = _mm(f"d_attn_{i}", [dict(a=dy, b=wf["w_br_attn"], bl=i, mode="nt", K=D)], S, ATTN_W,
                       [dict(shape=(S, ATTN_W), dtype=BF16)], _first, tm=tmb, tn=ATTN_W)
        (dsg,) = _mm(f"d_sg_{i}", [dict(a=dy, a_off=D, b=wf["w_br_sg"], bl=i, mode="nt", K=D)], S, SG_W,
                     [dict(shape=(S, SG_W), dtype=BF16)], _first, tm=tmb, tn=SG_W)
        dw("w_br_attn", i, sv["attn"], 0, dy, 0, ATTN_W, D, ATTN_W, _pick(D, (2048, 1024, 512, 256)))
        dw("w_br_sg", i, sv["sg"], 0, dy, D, SG_W, D, SG_W, _pick(D, (1024, 512, 256)))
        sgwt = jnp.swapaxes(small["sg_w"][i], 1, 2).astype(BF16)
        dzuv, gs["sg_w"][i], dsgb, dlg, dlb = _sg_bwd(f"sgu_bwd_{i}", sv["z"], dsg, sv["sgw"], sgwt, sv["sgbc"],
                                                      _row(small["sg_ln_g"][i]), _row(small["sg_ln_b"][i]), o_sg0)
        gs["sg_b"][i], gs["sg_ln_g"][i], gs["sg_ln_b"][i] = dsgb.reshape(SG_GROUPS, SG_CHUNK), dlg[0], dlb[0]
        dq, dk, dv = _attn_bwd(f"attn_bwd_{i}", sv["qkv"], sv["attn"], dattn, sv["lse_c"])
        dz = _rope_bwd(f"rope_bwd_{i}", dq, dk, dv, dzuv, cosf, sinf, dz)
        dw("w_in", i, sv["h"], 0, dz, 0, D, in_w, tn_d, tn_in)
        dz, _ = lax.optimization_barrier((dz, after_group(i, "mix", {n: gw[n][i] for n in GRAD_GROUPS["mix"]})))
        (dh,) = _mm(f"d_h_{i}", [dict(a=dz, b=wf["w_in"], bl=i, mode="nt", K=in_w)], S, D,
                    [dict(shape=(S, D), dtype=BF16)], _first, tm=tmb, tn=tn_d)
        dx, dxb, gs["norm_mix"][i] = _rmsnorm_bwd(f"norm_mix_bwd_{i}", sv["x0"], _row(small["norm_mix"][i]), dh, dx)

    gsmall ={n: jnp.stack([jnp.reshape(v, small[n].shape[1:]) for v in gs[n]]) for n in gs}
    gsmall["norm_final"] = dg_final[0]
    return loss_cell, dx, gsmall


def _place():
    x, y, c = lax.axis_index("x"), lax.axis_index("y"), lax.axis_index("c")
    return x, y, c, 2 * x + y


def _chip_of(s):
    return s // 2, s % 2


def _aligned(v, m):
    return v if isinstance(v, int) else pl.multiple_of(v, m)


def _piece(name, shape, s, c):
    K, N = shape
    if name in ROW_SHARDED or name == SMALL_BLOCKS:
        ks = K // 4
        return s * ks + c * (ks // 2), ks // 2, 0, N
    ns = N // 4
    return c * (K // 2), K // 2, s * ns, ns


def _handshake(peers):
    barrier = pltpu.get_barrier_semaphore()
    for peer in peers:
        pl.semaphore_signal(barrier, inc=1, device_id=peer, device_id_type=MESH)
    pl.semaphore_wait(barrier, len(peers))


def _gather_body(names, shapes, src, dst, send_sems, recv_sems, local_sems):
    n_w = len(names)
    x, y, c, s = _place()
    sib = (x, y, 1 - c)
    rel = [1, 2, 3]

    def where(w, ps, pc):
        r0, nr, c0, nc = _piece(names[w], shapes[names[w]], ps, pc)
        return dst[w].at[pl.ds(_aligned(r0, 16), nr), pl.ds(_aligned(c0, LANES), nc)]

    def copy(w, k, ps, pc, to, from_src=False):
        return pltpu.make_async_remote_copy(
            src_ref=src[w] if from_src else where(w, ps, pc), dst_ref=where(w, ps, pc),
            send_sem=send_sems.at[w, k], recv_sem=recv_sems.at[w, k], device_id=to, device_id_type=MESH)

    mine, first, passed = [], [], []
    for w in range(n_w):
        cp = pltpu.make_async_copy(src[w], where(w, s, c), local_sems.at[w])
        cp.start()
        mine.append(cp)
        first.append(copy(w, 0, s, c, sib, from_src=True))
        for j in rel:
            first.append(copy(w, j, s, c, (*_chip_of(s ^ j), c), from_src=True))
    for cp in first:
        cp.start()
    for w in range(n_w):
        for j in rel:
            copy(w, j, s ^ j, c, sib).wait_recv()
            fw = copy(w, 3 + j, s ^ j, c, sib)
            fw.start()
            passed.append(fw)
    for w in range(n_w):
        copy(w, 0, s, 1 - c, sib).wait_recv()
        for j in rel:
            copy(w, 3 + j, s ^ j, 1 - c, sib).wait_recv()
    for cp in first + passed:
        cp.wait_send()
    for cp in mine:
        cp.wait()


def _gather_sems(n_w):
    return (pltpu.SemaphoreType.DMA((n_w, 7)), pltpu.SemaphoreType.DMA((n_w, 7)), pltpu.SemaphoreType.DMA((n_w,)))


def _gather_peers():
    x, y, c, s = _place()
    return [(x, y, 1 - c)] + [(*_chip_of(s ^ j), c) for j in (1, 2, 3)]


def _gather_weights_async(name, pieces, shapes):
    names = list(pieces)
    n_w = len(names)
    src = [jax.new_ref(pieces[n], memory_space=pltpu.MemorySpace.HBM) for n in names]
    dst = [jax.empty_ref(jax.ShapeDtypeStruct(tuple(shapes[n]), pieces[n].dtype), memory_space=pltpu.MemorySpace.HBM)
           for n in names]

    @pl.kernel(mesh=plsc.ScalarSubcoreMesh(axis_name="seq", num_cores=1), name=name, scratch_types=_gather_sems(n_w),
               compiler_params=pltpu.CompilerParams(collective_id=GATHER_COLLECTIVE_ID))
    def launch(send_sems, recv_sems, local_sems):
        _handshake(_gather_peers())
        _gather_body(names, shapes, src, dst, send_sems, recv_sems, local_sems)

    launch()
    return {n: d[...] for n, d in zip(names, dst)}


def _halves_view(name, g):
    L, K, N = g.shape
    if name in ROW_SHARDED:
        return g.reshape(L * 4, 2, K // 8, N)
    return g.reshape(L, 2, K // 2, N)


def _all_peers():
    x, y, c, s = _place()
    return [(x, y, 1 - c)] + [(*_chip_of(s ^ j), h) for j in (1, 2, 3) for h in (0, 1)]


def _scatter_partials_async(name, views):
    names = list(views)
    n_w = len(names)
    src = [jax.new_ref(views[n], memory_space=pltpu.MemorySpace.HBM) for n in names]
    dst = [jax.empty_ref(jax.ShapeDtypeStruct(_partials_out_shape(n, views[n].shape), BF16), memory_space=pltpu.MemorySpace.HBM)
           for n in names]

    @pl.kernel(mesh=plsc.ScalarSubcoreMesh(axis_name="seq", num_cores=1), name=name, scratch_types=_partials_sems(n_w),
               compiler_params=pltpu.CompilerParams(collective_id=PARTIALS_COLLECTIVE_ID))
    def launch(send_sems, recv_sems):
        _handshake(_all_peers())
        _scatter_partials_body(names, src, dst, send_sems, recv_sems)

    launch()
    return {n: d[...] for n, d in zip(names, dst)}


def _partials_out_shape(name, v):
    return (7, 1, v[2], v[3] if name in ROW_SHARDED else v[3] // 4)


def _partials_sems(n_w):
    return (pltpu.SemaphoreType.DMA((n_w, 7)), pltpu.SemaphoreType.DMA((n_w, 7)))


def _scatter_partials_body(names, src, dst, send_sems, recv_sems):
    x, y, c, s = _place()

    def piece(w, t, h):
        if names[w] in ROW_SHARDED:
            return src[w].at[pl.ds(t, 1), h]
        ns = src[w].shape[3] // 4
        return src[w].at[:, h, :, pl.ds(pl.multiple_of(t * ns, LANES), ns)]

    sent = []
    for w in range(len(names)):
        for j in (1, 2, 3):
            for h in (0, 1):
                sent.append(pltpu.make_async_remote_copy(
                    src_ref=piece(w, s ^ j, h), dst_ref=dst[w].at[2 * (j - 1) + c], send_sem=send_sems.at[w, 2 * (j - 1) + h],
                    recv_sem=recv_sems.at[w, 2 * (j - 1) + c], device_id=(*_chip_of(s ^ j), h), device_id_type=MESH))
        sent.append(pltpu.make_async_remote_copy(
            src_ref=piece(w, s, 1 - c), dst_ref=dst[w].at[6], send_sem=send_sems.at[w, 6], recv_sem=recv_sems.at[w, 6],
            device_id=(x, y, 1 - c), device_id_type=MESH))
    for cp in sent:
        cp.start()
    for w in range(len(names)):
        for slot in range(7):
            pltpu.make_async_remote_copy(src_ref=dst[w].at[slot], dst_ref=dst[w].at[slot], send_sem=send_sems.at[w, slot],
                                         recv_sem=recv_sems.at[w, slot], device_id=(x, y, 1 - c), device_id_type=MESH).wait_recv()
    for cp in sent:
        cp.wait_send()


def _shard_sum_partials(name, view, parts, place, row_sharded, layer, n_layers, into):
    R, C = parts.shape[2:]
    tc = _pick(C, (2048, 1408, 1024, 896, 512, 384, 256, 128))
    tr = _pick(R, [t for t in (1024, 512, 256, 128, 64, 32, 16) if t * tc <= 2 * EW_TILE_ELEMS] + [8])

    def body(p_ref, own_ref, *rest):
        acc = own_ref[...].astype(F32) + rest[6][...].astype(F32)
        for k in range(6):
            acc = acc + rest[k][...].astype(F32)
        rest[-1][...] = acc

    if row_sharded:
        own_spec = pl.BlockSpec((None, None, tr, tc), lambda i, j, p: (p[1], p[0], i, j))
    else:
        own_spec = pl.BlockSpec((None, None, tr, tc), lambda i, j, p: (0, p[0], i, p[1] * (C // tc) + j))
    part = lambda k: pl.BlockSpec((None, None, tr, tc), lambda i, j, p, k=k: (k, 0, i, j))
    in_specs, args, aliases = [own_spec] + [part(k) for k in range(7)], [place, view] + [parts] * 7, {}
    if into is not None:
        in_specs.append(pl.BlockSpec(memory_space=pl.ANY))
        args.append(into)
        aliases = {9: 0}
    return pl.pallas_call(
        body, name=name, out_shape=jax.ShapeDtypeStruct((n_layers, 2, R, C), F32),
        grid_spec=pltpu.PrefetchScalarGridSpec(
            num_scalar_prefetch=1, grid=(R // tr, C // tc), in_specs=in_specs,
            out_specs=pl.BlockSpec((None, None, tr, tc), lambda i, j, p: (layer, p[0], i, j))),
        input_output_aliases=aliases,
        compiler_params=pltpu.CompilerParams(dimension_semantics=("parallel", "parallel"),
                                             vmem_limit_bytes=_vmem_limit(20 * tr * tc, 5 * tr * tc * 4)),
    )(*args)


def _share_halves(name, ghalf):
    names = list(ghalf)
    n_w = len(names)

    def body(*refs):
        src = refs[:n_w]
        dst = refs[n_w:2 * n_w]
        send_sems, recv_sems = refs[2 * n_w:]
        x, y, c, s = _place()
        remote = [pltpu.make_async_remote_copy(src_ref=src[w].at[:, c], dst_ref=dst[w].at[:, c], send_sem=send_sems.at[w],
                                               recv_sem=recv_sems.at[w], device_id=(x, y, 1 - c), device_id_type=MESH)
                  for w in range(n_w)]
        for cp in remote:
            cp.start()
        for cp in remote:
            cp.wait()

    anyspec = pl.BlockSpec(memory_space=pl.ANY)
    out = pl.pallas_call(
        body, name=name, in_specs=[anyspec] * n_w, out_specs=[anyspec] * n_w,
        out_shape=[jax.ShapeDtypeStruct(ghalf[n].shape, F32) for n in names],
        input_output_aliases={w: w for w in range(n_w)},
        scratch_shapes=[pltpu.SemaphoreType.DMA((n_w,)), pltpu.SemaphoreType.DMA((n_w,))],
    )(*[ghalf[n] for n in names])
    return dict(zip(names, out))


def _adamw_math(w, g, m, v):
    m = ADAM_B1 * m + (1.0 - ADAM_B1) * g
    v = ADAM_B2 * v + (1.0 - ADAM_B2) * (g * g)
    m_hat = m / (1.0 - ADAM_B1 ** ADAM_STEP)
    v_hat = v / (1.0 - ADAM_B2 ** ADAM_STEP)
    delta = -ADAM_LR * (m_hat / (jnp.sqrt(v_hat) + ADAM_EPS) + ADAM_WD * w)
    return delta, m, v


def _adamw(name, w, g, m, v):
    shape = w.shape
    C = shape[-1]
    R = math.prod(shape[:-1])
    f = lambda a: a.reshape(R, C)
    res = _ew(name, lambda w_, g_, m_, v_: [g_, *_adamw_math(w_, g_, m_, v_)], [f(w), f(g), f(m), f(v)], [F32] * 4, R, C)
    return [r.reshape(shape) for r in res]


def _pack_small(d):
    return jnp.concatenate([d[n].reshape(-1, LANES) for n in SMALL], axis=0)


def _unpack_small(flat, like):
    out, r = {}, 0
    for n in SMALL:
        k = like[n].size // LANES
        out[n] = flat[r:r + k].reshape(like[n].shape)
        r += k
    return out


def _small_update(gall, w, m, v):
    M = w.shape[0]
    tr = _pick(M, (552, 276, 184, 96, 48, 24, 8))

    def body(*refs):
        g = refs[0][...]
        for d in range(1, 8):
            g = g + refs[d][...]
        delta, nm, nv = _adamw_math(refs[8][...], g, refs[9][...], refs[10][...])
        refs[11][...] = g
        refs[12][...] = delta
        refs[13][...] = nm
        refs[14][...] = nv

    blk = pl.BlockSpec((tr, LANES), lambda i: (i, 0))
    in_specs = [pl.BlockSpec((tr, LANES), lambda i, d=d: (d * (M // tr) + i, 0)) for d in range(8)] + [blk] * 3
    return pl.pallas_call(
        body, name="small_update", grid=(M // tr,), in_specs=in_specs, out_specs=[blk] * 4,
        out_shape=[jax.ShapeDtypeStruct((M, LANES), F32)] * 4,
        compiler_params=pltpu.CompilerParams(dimension_semantics=("parallel",), vmem_limit_bytes=_vmem_limit(15 * tr * LANES * 4)),
    )(*([gall] * 8), w, m, v)


def _step(x, p, target, w, m, v):
    L = p.shape[0]
    x_i, y_i, c, s = _place()
    shapes = {}
    for n in BIG:
        _, K, N = w[n].shape
        shapes[n] = (4 * K, N) if n in ROW_SHARDED else (K, 4 * N)
    def pieces_of(i):
        return {n: lax.dynamic_slice_in_dim(w[n][i], c * (w[n].shape[1] // 2), w[n].shape[1] // 2, axis=0).astype(BF16)
                for n in BIG}

    def piece(n, i, after=None):
        wn = w[n] if after is None else lax.optimization_barrier((w[n], after))[0]
        return lax.dynamic_slice_in_dim(wn[i], c * (w[n].shape[1] // 2), w[n].shape[1] // 2, axis=0).astype(BF16)

    w_in0 = piece("w_in", 0)
    layers = [_gather_weights_async("gather_weights_0_w_in", {"w_in": w_in0}, shapes)]
    for i in range(L):
        if i > 0:
            layers.append({})
        for g in ("mix", "ffn"):
            mine = {n: piece(n, i, w_in0) for n in GRAD_GROUPS[g] if n not in layers[i]}
            layers[i].update(_gather_weights_async(f"gather_weights_{i}_{g}", mine, shapes))
    wf = {n: [layers[i][n] for i in range(L)] for n in BIG}
    small = {n: w[n] for n in SMALL}
    place = jnp.stack([c, s]).astype(jnp.int32)
    reduced = []

    def after_group(i, group, grads):
        views = {n: _halves_view(n, g[None]) for n, g in grads.items()}
        reduced.append((i, group, views, _scatter_partials_async(f"scatter_partials_{i}_{group}", views)))
        return views

    loss_cell, dx, gsmall = _local_step(x[0], p[:, 0], target[0], wf, small, after_group)
    loss = lax.psum(jnp.sum(loss_cell), ("x", "y", "c"))
    packed = _pack_small(gsmall)
    gall = _gather_weights_async("gather_small", {SMALL_BLOCKS: packed}, {SMALL_BLOCKS: (8 * packed.shape[0], LANES)})[SMALL_BLOCKS]
    ghalf = {n: None for n in BIG}
    grad, delta, new_m, new_v = {}, {}, {}, {}
    done = None

    def finish(group):
        gfull = _share_halves(f"share_halves_{group}", {n: ghalf[n] for n in GRAD_GROUPS[group]})
        for n in GRAD_GROUPS[group]:
            grad[n], delta[n], new_m[n], new_v[n] = _adamw(f"adamw_{n}", w[n], gfull[n].reshape(w[n].shape), m[n], v[n])
        return {n: delta[n] for n in GRAD_GROUPS[group]}

    for k, (i, group, own, parts) in enumerate(reduced):
        if k == len(reduced) - 1:
            done = finish("ffn")
        parts, _ = lax.optimization_barrier((parts, done))
        for n in own:
            ghalf[n] = _shard_sum_partials(f"shard_sum_{n}_{i}", own[n], parts[n], place, n in ROW_SHARDED, i, L, ghalf[n])
        done = {n: ghalf[n] for n in own}
    finish("mix")
    gall, _ = lax.optimization_barrier((gall, delta))
    gsum, dsm, nms, nvs =_small_update(gall, _pack_small(small), _pack_small({n: m[n] for n in SMALL}),
                                        _pack_small({n: v[n] for n in SMALL}))
    for dst, flat in ((grad, gsum), (delta, dsm), (new_m, nms), (new_v, nvs)):
        dst.update(_unpack_small(flat, small))
    return loss, dx[None], grad, delta, new_m, new_v


def kernel(x, p, w_in, w_br_attn, w_br_sg, w_out, sg_w, sg_b, sg_ln_g, sg_ln_b, norm_mix, norm_ffn, norm_ple, norm_final, w_ff_gate, w_ff_up, w_ff_down, w_ple_gate, w_ple, loss_target, m_w_in, m_w_br_attn, m_w_br_sg, m_w_out, m_sg_w, m_sg_b, m_sg_ln_g, m_sg_ln_b, m_norm_mix, m_norm_ffn, m_norm_ple, m_norm_final, m_w_ff_gate, m_w_ff_up, m_w_ff_down, m_w_ple_gate, m_w_ple, v_w_in, v_w_br_attn, v_w_br_sg, v_w_out, v_sg_w, v_sg_b, v_sg_ln_g, v_sg_ln_b, v_norm_mix, v_norm_ffn, v_norm_ple, v_norm_final, v_w_ff_gate, v_w_ff_up, v_w_ff_down, v_w_ple_gate, v_w_ple):
    w = dict(w_in=w_in, w_br_attn=w_br_attn, w_br_sg=w_br_sg, w_out=w_out, sg_w=sg_w, sg_b=sg_b, sg_ln_g=sg_ln_g, sg_ln_b=sg_ln_b,
             norm_mix=norm_mix, norm_ffn=norm_ffn, norm_ple=norm_ple, norm_final=norm_final, w_ff_gate=w_ff_gate, w_ff_up=w_ff_up,
             w_ff_down=w_ff_down, w_ple_gate=w_ple_gate, w_ple=w_ple)
    m = dict(w_in=m_w_in, w_br_attn=m_w_br_attn, w_br_sg=m_w_br_sg, w_out=m_w_out, sg_w=m_sg_w, sg_b=m_sg_b, sg_ln_g=m_sg_ln_g,
             sg_ln_b=m_sg_ln_b, norm_mix=m_norm_mix, norm_ffn=m_norm_ffn, norm_ple=m_norm_ple, norm_final=m_norm_final,
             w_ff_gate=m_w_ff_gate, w_ff_up=m_w_ff_up, w_ff_down=m_w_ff_down, w_ple_gate=m_w_ple_gate, w_ple=m_w_ple)
    v = dict(w_in=v_w_in, w_br_attn=v_w_br_attn, w_br_sg=v_w_br_sg, w_out=v_w_out, sg_w=v_sg_w, sg_b=v_sg_b, sg_ln_g=v_sg_ln_g,
             sg_ln_b=v_sg_ln_b, norm_mix=v_norm_mix, norm_ffn=v_norm_ffn, norm_ple=v_norm_ple, norm_final=v_norm_final,
             w_ff_gate=v_w_ff_gate, w_ff_up=v_w_ff_up, w_ff_down=v_w_ff_down, w_ple_gate=v_w_ple_gate, w_ple=v_w_ple)
    loss, grad_x, grad, delta, new_m, new_v = _step(x, p, loss_target, w, m, v)
    return (loss, grad_x, *[grad[n] for n in WEIGHTS], *[delta[n] for n in WEIGHTS], *[new_m[n] for n in WEIGHTS],
            *[new_v[n] for n in WEIGHTS])
```
